```python
import math
import jax, jax.numpy as jnp
from jax import lax
import numpy as np

D_MODEL = 1024
BATCH = 8
SEQ = 2048
DEPTH = 4

N_MIXERS = 2
CHUNK = 64
A_HEADS = 8
A_DK = 128
A_DV = 128
A_CONV = 5
A_W = A_HEADS * A_DK
A_VW = A_HEADS * A_DV
A_CONV_CH = 2 * A_W + A_VW
A_IN = 2 * A_W + 2 * A_VW + 4 * A_HEADS
B_HEADS = 4
B_DK = 128
B_DV = 256
B_RANK = 16
B_TAU = 16.0
B_KW = B_HEADS * B_DK
B_VW = B_HEADS * B_DV
B_IN = 2 * B_KW + 2 * B_VW + 2 * B_RANK
D_FF = 4 * D_MODEL
DEEPNORM_ALPHA = (2 * DEPTH) ** 0.25
DEEPNORM_BETA = (8 * DEPTH) ** -0.25
LN_EPS = 1e-5
RMS_EPS = 1e-6
N_A_LAYERS = (DEPTH + 1) // 2
N_B_LAYERS = DEPTH // 2

kernel_name = "hybrid_gdn_gla_deepnorm_encoder"

F32 = jnp.float32


def _layernorm(x, g, b):
    xf = x.astype(F32)
    mu = jnp.mean(xf, axis=-1, keepdims=True)
    xc = xf - mu
    var = jnp.mean(xc * xc, axis=-1, keepdims=True)
    return (xc * lax.rsqrt(var + LN_EPS) * g.astype(F32) + b.astype(F32)).astype(x.dtype)


def _rmsnorm(x, g):
    return x * lax.rsqrt(jnp.mean(x * x, axis=-1, keepdims=True) + RMS_EPS) * g.astype(F32)


def _l2norm(x):
    return x * lax.rsqrt(jnp.sum(x * x, axis=-1, keepdims=True) + 1e-6)


def _flip(t):
    return jnp.flip(t, axis=2)


def _depthwise_conv(u, w):
    k = w.shape[0]
    return lax.conv_general_dilated(
        u, w[:, None, :], window_strides=(1,), padding=[(k // 2, k // 2)],
        dimension_numbers=("NWC", "WIO", "NWC"), feature_group_count=u.shape[-1])


def _gated_delta_chunked(q, k, v, beta, g):
    bn, h, s, dk = q.shape
    dv = v.shape[-1]
    n = s // CHUNK
    q = q.reshape(bn, h, n, CHUNK, dk)
    k = k.reshape(bn, h, n, CHUNK, dk)
    v = v.reshape(bn, h, n, CHUNK, dv)
    beta = beta.reshape(bn, h, n, CHUNK)
    gc = jnp.cumsum(g.reshape(bn, h, n, CHUNK), axis=-1)
    causal = jnp.tril(jnp.ones((CHUNK, CHUNK), bool))
    strict = jnp.tril(jnp.ones((CHUNK, CHUNK), bool), -1)
    decay_mat = jnp.exp(jnp.where(causal, gc[..., :, None] - gc[..., None, :], -jnp.inf))
    kb = k * beta[..., None]
    a_mat = jnp.where(strict, jnp.einsum('bhncd,bhnsd->bhncs', kb, k) * decay_mat, 0.0)
    lhs = a_mat + jnp.eye(CHUNK, dtype=F32)
    rhs = jnp.concatenate([v * beta[..., None], kb * jnp.exp(gc)[..., None]], axis=-1)
    sol = lax.linalg.triangular_solve(lhs, rhs, left_side=True, lower=True, unit_diagonal=True)
    u, w = sol[..., :dv], sol[..., dv:]
    qk = jnp.einsum('bhncd,bhnsd->bhncs', q, k) * decay_mat
    q_dec = q * jnp.exp(gc)[..., None]
    k_dec = k * jnp.exp(gc[..., -1:] - gc)[..., None]
    g_last = jnp.exp(gc[..., -1])
    xs = tuple(jnp.moveaxis(t, 2, 0) for t in (qk, q_dec, k_dec, u, w, g_last))

    def step(state, inp):
        qk_c, qd_c, kd_c, u_c, w_c, gl_c = inp
        v_new = u_c - jnp.einsum('bhcd,bhde->bhce', w_c, state)
        o = jnp.einsum('bhcd,bhde->bhce', qd_c, state) + jnp.einsum('bhcs,bhse->bhce', qk_c, v_new)
        state = state * gl_c[..., None, None] + jnp.einsum('bhcd,bhce->bhde', kd_c, v_new)
        return state, o

    s0 = jnp.zeros((bn, h, dk, dv), F32)
    _, o = lax.scan(step, s0, xs)
    return jnp.moveaxis(o, 0, 2).reshape(bn, h, s, dv)


def _gla_chunked(q, k, v, log_a):
    bn, h, s, dk = q.shape
    dv = v.shape[-1]
    n = s // CHUNK

    def to_chunks(t):
        return jnp.moveaxis(t.reshape(bn, h, n, CHUNK, t.shape[-1]), 2, 0)

    qc, kc, vc = to_chunks(q), to_chunks(k), to_chunks(v)
    bc = jnp.cumsum(to_chunks(log_a), axis=3)
    causal = jnp.tril(jnp.ones((CHUNK, CHUNK), bool))[:, :, None]

    def step(state, inp):
        q_c, k_c, v_c, b_c = inp
        dec = jnp.exp(jnp.where(causal, b_c[:, :, :, None, :] - b_c[:, :, None, :, :], -jnp.inf))
        scores = jnp.einsum('bhid,bhjd,bhijd->bhij', q_c, k_c, dec)
        o = jnp.einsum('bhid,bhde->bhie', q_c * jnp.exp(b_c), state) + jnp.einsum('bhij,bhje->bhie', scores, v_c)
        b_last = b_c[:, :, -1:, :]
        state = jnp.exp(b_last[:, :, 0, :])[..., None] * state + jnp.einsum(
            'bhjd,bhje->bhde', k_c * jnp.exp(b_last - b_c), v_c)
        return state, o

    s0 = jnp.zeros((bn, h, dk, dv), F32)
    _, o = lax.scan(step, s0, (qc, kc, vc, bc))
    return jnp.moveaxis(o, 0, 2).reshape(bn, h, s, dv)


def _mixer_gdn(h, w_in, conv_w, a_log, dt_bias, norm_g, w_out):
    bn, s, _ = h.shape
    proj = jnp.matmul(h, w_in).astype(F32)
    qkv = jax.nn.silu(_depthwise_conv(proj[..., :A_CONV_CH], conv_w.astype(F32)))
    z = proj[..., A_CONV_CH:A_CONV_CH + A_VW]
    ba = proj[..., A_CONV_CH + A_VW:].reshape(bn, s, 2, 2, A_HEADS)

    def heads(t, d):
        return t.reshape(bn, s, A_HEADS, d).transpose(0, 2, 1, 3)

    q = _l2norm(heads(qkv[..., :A_W], A_DK)) * (A_DK ** -0.5)
    k = _l2norm(heads(qkv[..., A_W:2 * A_W], A_DK))
    v = heads(qkv[..., 2 * A_W:], A_DV)
    beta = jax.nn.sigmoid(ba[:, :, 0]).transpose(2, 0, 3, 1)
    g = (-jnp.exp(a_log.astype(F32)) * jax.nn.softplus(ba[:, :, 1] + dt_bias.astype(F32))).transpose(2, 0, 3, 1)
    o_f = _gated_delta_chunked(q, k, v, beta[0], g[0])
    o_b = _flip(_gated_delta_chunked(_flip(q), _flip(k), _flip(v), _flip(beta[1]), _flip(g[1])))
    o = (o_f + o_b).transpose(0, 2, 1, 3)
    o = _rmsnorm(o, norm_g) * jax.nn.silu(z.reshape(bn, s, A_HEADS, A_DV))
    return jnp.matmul(o.reshape(bn, s, A_VW).astype(w_out.dtype), w_out)


def _mixer_gla(h, w_in, gate_w2, gate_b, norm_g, w_out):
    bn, s, _ = h.shape
    proj = jnp.matmul(h, w_in).astype(F32)

    def heads(t, d):
        return t.reshape(bn, s, B_HEADS, d).transpose(0, 2, 1, 3)

    q = heads(proj[..., :B_KW], B_DK) * (B_DK ** -0.5)
    k = heads(proj[..., B_KW:2 * B_KW], B_DK)
    v = heads(proj[..., 2 * B_KW:2 * B_KW + B_VW], B_DV)
    r = proj[..., 2 * B_KW + B_VW:2 * B_KW + 2 * B_VW]
    gl = proj[..., 2 * B_KW + 2 * B_VW:].reshape(bn, s, 2, B_RANK)
    gate_logit = jnp.einsum('bsnr,nrk->nbsk', gl, gate_w2.astype(F32)) + gate_b.astype(F32)[:, None, None, :]
    log_a = (jax.nn.log_sigmoid(gate_logit) / B_TAU).reshape(2, bn, s, B_HEADS, B_DK).transpose(0, 1, 3, 2, 4)
    o_f = _gla_chunked(q, k, v, log_a[0])
    o_b = _flip(_gla_chunked(_flip(q), _flip(k), _flip(v), _flip(log_a[1])))
    o = (o_f + o_b).transpose(0, 2, 1, 3)
    o = _rmsnorm(o, norm_g) * jax.nn.silu(r.reshape(bn, s, B_HEADS, B_DV))
    return jnp.matmul(o.reshape(bn, s, B_VW).astype(w_out.dtype), w_out)


def _sq_relu_mlp(h, w1, w2):
    a = jax.nn.relu(jnp.matmul(h, w1))
    return jnp.matmul(a * a, w2)


def setup_inputs(seed: int = 0) -> dict:
    key = jax.random.key(seed)
    ks = jax.random.split(key, 20)
    nrm = jax.random.normal
    x = nrm(ks[0], (BATCH, SEQ, D_MODEL), F32)
    a_w_in = nrm(ks[1], (N_A_LAYERS, D_MODEL, A_IN), F32) * D_MODEL ** -0.5
    a_conv = nrm(ks[2], (N_A_LAYERS, A_CONV, A_CONV_CH), F32) * A_CONV ** -0.5
    a_alog = jnp.log(jax.random.uniform(ks[3], (N_A_LAYERS, 2, A_HEADS), F32, 1.0, 16.0))
    dt = jnp.exp(jax.random.uniform(ks[4], (N_A_LAYERS, 2, A_HEADS), F32, math.log(1e-3), math.log(1e-1)))
    a_dt_bias = dt + jnp.log(-jnp.expm1(-dt))
    a_norm_g = 1.0 + 0.02 * nrm(ks[5], (N_A_LAYERS, A_DV), F32)
    a_w_out = nrm(ks[6], (N_A_LAYERS, A_VW, D_MODEL), F32) * (A_VW ** -0.5 * DEEPNORM_BETA)
    b_w_in = nrm(ks[7], (N_B_LAYERS, D_MODEL, B_IN), F32) * D_MODEL ** -0.5
    b_gate_w2 = nrm(ks[8], (N_B_LAYERS, 2, B_RANK, B_KW), F32) * B_RANK ** -0.5
    b_gate_b = 0.1 * nrm(ks[9], (N_B_LAYERS, 2, B_KW), F32)
    b_norm_g = 1.0 + 0.02 * nrm(ks[10], (N_B_LAYERS, B_DV), F32)
    b_w_out = nrm(ks[11], (N_B_LAYERS, B_VW, D_MODEL), F32) * (B_VW ** -0.5 * DEEPNORM_BETA)
    ln1_g = 1.0 + 0.02 * nrm(ks[12], (DEPTH, D_MODEL), F32)
    ln1_b = 0.02 * nrm(ks[13], (DEPTH, D_MODEL), F32)
    mlp_w1 = nrm(ks[14], (DEPTH, D_MODEL, D_FF), F32) * D_MODEL ** -0.5
    mlp_w2 = nrm(ks[15], (DEPTH, D_FF, D_MODEL), F32) * (D_FF ** -0.5 * DEEPNORM_BETA)
    ln2_g = 1.0 + 0.02 * nrm(ks[16], (DEPTH, D_MODEL), F32)
    ln2_b = 0.02 * nrm(ks[17], (DEPTH, D_MODEL), F32)
    return {"x": x, "a_w_in": a_w_in, "a_conv": a_conv, "a_alog": a_alog, "a_dt_bias": a_dt_bias,
            "a_norm_g": a_norm_g, "a_w_out": a_w_out, "b_w_in": b_w_in, "b_gate_w2": b_gate_w2,
            "b_gate_b": b_gate_b, "b_norm_g": b_norm_g, "b_w_out": b_w_out, "ln1_g": ln1_g,
            "ln1_b": ln1_b, "mlp_w1": mlp_w1, "mlp_w2": mlp_w2, "ln2_g": ln2_g, "ln2_b": ln2_b}


def reference(x, a_w_in, a_conv, a_alog, a_dt_bias, a_norm_g, a_w_out, b_w_in, b_gate_w2,
              b_gate_b, b_norm_g, b_w_out, ln1_g, ln1_b, mlp_w1, mlp_w2, ln2_g, ln2_b):
    for i in range(DEPTH):
        j = i // N_MIXERS
        if i % N_MIXERS == 0:
            m = _mixer_gdn(x, a_w_in[j], a_conv[j], a_alog[j], a_dt_bias[j], a_norm_g[j], a_w_out[j])
        else:
            m = _mixer_gla(x, b_w_in[j], b_gate_w2[j], b_gate_b[j], b_norm_g[j], b_w_out[j])
        x = _layernorm(DEEPNORM_ALPHA * x + m.astype(x.dtype), ln1_g[i], ln1_b[i])
        x = _layernorm(DEEPNORM_ALPHA * x + _sq_relu_mlp(x, mlp_w1[i], mlp_w2[i]).astype(x.dtype), ln2_g[i], ln2_b[i])
    return x
```

```python
import functools

import jax
import jax.numpy as jnp
from jax import lax
from jax.experimental import pallas as pl
from jax.experimental.pallas import tpu as pltpu

F32 = jnp.float32
BF16 = jnp.bfloat16

A_HEADS, A_DK, A_DV, A_CONV = 8, 128, 128, 5
B_HEADS, B_DK, B_DV, B_RANK, B_TAU = 4, 128, 256, 16, 16.0
LN_EPS, RMS_EPS, L2_EPS = 1e-5, 1e-6, 1e-6

CH = 128
HALO = 8
GLA_DIAG = 16
GLA_CLAMP = 60.0
NEG_BIG = -1e30
VMEM_LIMIT = 56 * 1024 * 1024
ROW_TILE = 512
FF_TILE = 1024


def _dot(a, b):
    return jnp.dot(a, b, preferred_element_type=F32)


def _dot_nt(a, b):
    return lax.dot_general(a, b, (((1,), (1,)), ((), ())), preferred_element_type=F32)


def _dot_tn(a, b):
    return lax.dot_general(a, b, (((0,), (0,)), ((), ())), preferred_element_type=F32)


def _dot_exact(m01, x):
    hi = x.astype(BF16)
    r1 = x - hi.astype(F32)
    mid = r1.astype(BF16)
    lo = (r1 - mid.astype(F32)).astype(BF16)
    return _dot(m01, hi) + _dot(m01, mid) + _dot(m01, lo)


def _sigmoid(x):
    return 1.0 / (1.0 + jnp.exp(-x))


def _softplus(x):
    return jnp.maximum(x, 0.0) + jnp.log(1.0 + jnp.exp(-jnp.abs(x)))


def _layernorm(y, g, b):
    mu = jnp.mean(y, axis=-1, keepdims=True)
    yc = y - mu
    var = jnp.mean(yc * yc, axis=-1, keepdims=True)
    return yc * lax.rsqrt(var + LN_EPS) * g + b


def _order_masks(rev):
    row = lax.broadcasted_iota(jnp.int32, (CH, CH), 0)
    col = lax.broadcasted_iota(jnp.int32, (CH, CH), 1)
    if rev:
        return col >= row, col > row
    return col <= row, col < row


def _xor_index():
    row = lax.broadcasted_iota(jnp.int32, (CH, CH), 0)
    col = lax.broadcasted_iota(jnp.int32, (CH, CH), 1)
    return row ^ col


def _gdn_kernel(xb_ref, wh_ref, cw_ref, hp_ref, ng_ref, lm_ref, o_ref,
                p_ref, a_ref, t_ref, qk_ref, rhs_ref, qd_ref, kd_ref, gl_ref,
                mc_ref, qc_ref, rc_ref, oc_ref, *, seq):
    nc = seq // CH
    n_levels = lm_ref.shape[0]

    p_ref[0:HALO, :] = jnp.zeros((HALO, p_ref.shape[1]), F32)
    p_ref[HALO + seq:, :] = jnp.zeros((HALO, p_ref.shape[1]), F32)
    p_ref[HALO:HALO + seq, :] = _dot(xb_ref[0], wh_ref[0])

    cw = cw_ref[0]
    hp = hp_ref[0]

    def prep(c, carry):
        base = pl.multiple_of(c * CH, CH)
        win = p_ref[pl.ds(base, CH + 2 * HALO), 0:3 * A_DK]
        acc = jnp.zeros((CH, 3 * A_DK), F32)
        for i in range(A_CONV):
            off = HALO + i - A_CONV // 2
            acc = acc + win[off:off + CH, :] * cw[i:i + 1, :]
        s = acc * _sigmoid(acc)
        q = s[:, 0:A_DK]
        k = s[:, A_DK:2 * A_DK]
        v = s[:, 2 * A_DK:3 * A_DK]
        q = q * lax.rsqrt(jnp.sum(q * q, axis=-1, keepdims=True) + L2_EPS) * (A_DK ** -0.5)
        k = k * lax.rsqrt(jnp.sum(k * k, axis=-1, keepdims=True) + L2_EPS)
        ba = p_ref[pl.ds(base + HALO, CH), 4 * A_DK:8 * A_DK]
        kb16 = k.astype(BF16)
        for d, rev in ((0, False), (1, True)):
            incl, strict = _order_masks(rev)
            beta = _sigmoid(ba[:, d * A_DK:(d + 1) * A_DK])
            g = -hp[d:d + 1, :] * _softplus(ba[:, (2 + d) * A_DK:(3 + d) * A_DK] + hp[2 + d:3 + d, :])
            gc = _dot_exact(incl.astype(BF16), g)
            gtot = gc[0:1, :] if rev else gc[CH - 1:CH, :]
            eg = jnp.exp(gc)
            kbeta = k * beta
            dmat = jnp.exp(jnp.where(incl, gc - gc.T, NEG_BIG))
            kq = _dot_nt(jnp.concatenate([kbeta, q], axis=0).astype(BF16), kb16)
            idx = d * nc + c
            a_ref[idx] = jnp.where(strict, kq[0:CH] * dmat, 0.0).astype(BF16)
            qk_ref[idx] = (kq[CH:] * dmat).astype(BF16)
            rhs_ref[idx] = jnp.concatenate([v * beta, kbeta * eg], axis=1).astype(BF16)
            qd_ref[idx] = q * eg
            kd_ref[idx] = (k * jnp.exp(gtot - gc)).astype(BF16)
            gl_ref[idx] = jnp.broadcast_to(jnp.exp(gtot), (8, A_DK))
        return carry

    lax.fori_loop(0, nc, prep, 0)

    def inv(i, carry):
        a = a_ref[i]
        eye = (lax.broadcasted_iota(jnp.int32, (CH, CH), 0)
               == lax.broadcasted_iota(jnp.int32, (CH, CH), 1)).astype(F32)
        t = (eye - (a * lm_ref[0]).astype(F32)).astype(BF16)
        for lv in range(1, n_levels):
            y = _dot(_dot(t, a * lm_ref[lv]).astype(BF16), t)
            t = t - y.astype(BF16)
        t_ref[i] = t
        return carry

    lax.fori_loop(0, 2 * nc, inv, 0)

    def ops(i, carry):
        uw = _dot(t_ref[i], rhs_ref[i]).astype(BF16)
        kd = kd_ref[i]
        qkm = qk_ref[i]
        kuw = _dot_tn(kd, uw)
        quw = _dot(qkm, uw)
        qc_ref[i] = kuw[:, 0:A_DV]
        mc_ref[i] = kuw[:, A_DV:].astype(BF16)
        oc_ref[i] = quw[:, 0:A_DV]
        rc_ref[i] = (qd_ref[i] - quw[:, A_DV:]).astype(BF16)
        return carry

    lax.fori_loop(0, 2 * nc, ops, 0)

    def scan(c, carry):
        sf, sb = carry
        cf = c
        cb = 2 * nc - 1 - c
        of = _dot(rc_ref[cf], sf.astype(BF16)) + oc_ref[cf]
        ob = _dot(rc_ref[cb], sb.astype(BF16)) + oc_ref[cb]
        oc_ref[cf] = of
        oc_ref[cb] = ob
        sf = gl_ref[cf][0:1, :] * sf - _dot(mc_ref[cf], sf.astype(BF16)) + qc_ref[cf]
        sb = gl_ref[cb][0:1, :] * sb - _dot(mc_ref[cb], sb.astype(BF16)) + qc_ref[cb]
        return sf, sb

    zero = jnp.zeros((A_DK, A_DV), F32)
    lax.fori_loop(0, nc, scan, (zero, zero))

    ng = ng_ref[...]

    def fin(c, carry):
        base = pl.multiple_of(c * CH, CH)
        o = oc_ref[c] + oc_ref[nc + c]
        z = p_ref[pl.ds(base + HALO, CH), 3 * A_DK:4 * A_DK]
        o = o * lax.rsqrt(jnp.mean(o * o, axis=-1, keepdims=True) + RMS_EPS) * ng
        o_ref[0, pl.ds(base, CH), :] = (o * (z * _sigmoid(z))).astype(BF16)
        return carry

    lax.fori_loop(0, nc, fin, 0)


def _gdn_level_masks():
    idx = jnp.arange(CH)
    x = idx[:, None] ^ idx[None, :]
    levels = []
    s = 1
    while s < CH:
        levels.append(((x // s) == 1).astype(BF16))
        s *= 2
    return jnp.stack(levels)


def _gdn_mixer(xb, w_in, conv_w, a_log, dt_bias, norm_g):
    bn, seq, dm = xb.shape
    h, dk = A_HEADS, A_DK
    nc = seq // CH
    w = w_in
    hw = h * dk
    ba = w[:, 4 * hw:].reshape(dm, 2, 2, h)
    per_head = [w[:, i * hw:(i + 1) * hw].reshape(dm, h, dk).transpose(1, 0, 2) for i in range(4)]
    rep = [jnp.broadcast_to(ba[:, kind, d, :].T[:, :, None], (h, dm, dk))
           for kind in range(2) for d in range(2)]
    wh = jnp.concatenate(per_head + rep, axis=2).astype(BF16)
    cw = conv_w.reshape(A_CONV, 3, h, dk).transpose(2, 0, 1, 3).reshape(h, A_CONV, 3 * dk)
    cw = jnp.pad(cw, ((0, 0), (0, 8 - A_CONV), (0, 0))).astype(F32)
    hp = jnp.concatenate([jnp.exp(a_log.astype(F32)), dt_bias.astype(F32)], axis=0)
    hp = jnp.broadcast_to(jnp.pad(hp, ((0, 4), (0, 0))).T[:, :, None], (h, 8, dk))
    ng = norm_g.astype(F32).reshape(1, A_DV)
    lm = _gdn_level_masks()
    nw = wh.shape[2]

    kern = functools.partial(_gdn_kernel, seq=seq)
    tile = lambda dt: pltpu.VMEM((2 * nc, CH, CH), dt)
    return pl.pallas_call(
        kern,
        out_shape=jax.ShapeDtypeStruct((bn, seq, h * A_DV), BF16),
        grid=(bn, h),
        in_specs=[
            pl.BlockSpec((1, seq, dm), lambda b, i: (b, 0, 0)),
            pl.BlockSpec((1, dm, nw), lambda b, i: (i, 0, 0)),
            pl.BlockSpec((1, 8, 3 * dk), lambda b, i: (i, 0, 0)),
            pl.BlockSpec((1, 8, dk), lambda b, i: (i, 0, 0)),
            pl.BlockSpec((1, A_DV), lambda b, i: (0, 0)),
            pl.BlockSpec(lm.shape, lambda b, i: (0, 0, 0)),
        ],
        out_specs=pl.BlockSpec((1, seq, A_DV), lambda b, i: (b, 0, i)),
        scratch_shapes=[
            pltpu.VMEM((seq + 2 * HALO, nw), F32),
            tile(BF16),
            tile(BF16),
            tile(BF16),
            pltpu.VMEM((2 * nc, CH, A_DV + A_DK), BF16),
            tile(F32),
            tile(BF16),
            pltpu.VMEM((2 * nc, 8, A_DK), F32),
            tile(BF16),
            tile(F32),
            tile(BF16),
            tile(F32),
        ],
        compiler_params=pltpu.CompilerParams(
            dimension_semantics=("arbitrary", "arbitrary"), vmem_limit_bytes=VMEM_LIMIT),
        name="gdn_mixer",
    )(xb, wh, cw, hp, ng, lm)


def _block_ref_rows(bc, size, rev, before):
    pieces = []
    for m in range(CH // size):
        lo = m * size
        if before:
            r = lo + size if rev else lo - 1
            if r < 0 or r >= CH:
                row = jnp.zeros((1, bc.shape[1]), F32)
            else:
                row = bc[r:r + 1, :]
        else:
            r = lo + size // 2 if rev else lo + size // 2 - 1
            row = bc[r:r + 1, :]
        pieces.append(jnp.broadcast_to(row, (size, bc.shape[1])))
    return jnp.concatenate(pieces, axis=0)


def _gla_kernel(xb_ref, wh_ref, w2_ref, gb_ref, ng_ref, o_ref,
                p_ref, qs_ref, kd_ref, dec_ref, oi_ref, *, seq):
    nc = seq // CH
    dk, dv = B_DK, B_DV
    p_ref[...] = _dot(xb_ref[0], wh_ref[0])

    c_q, c_k, c_v, c_r, c_g = 0, dk, 2 * dk, 2 * dk + dv, 2 * dk + 2 * dv

    def prep(c, carry):
        base = pl.multiple_of(c * CH, CH)
        rows = pl.ds(base, CH)
        q = p_ref[rows, c_q:c_q + dk] * (dk ** -0.5)
        k = p_ref[rows, c_k:c_k + dk]
        v16 = p_ref[rows, c_v:c_v + dv].astype(BF16)
        gin = p_ref[rows, c_g:c_g + dk].astype(BF16)
        xor = _xor_index()
        for d, rev in ((0, False), (1, True)):
            incl, _ = _order_masks(rev)
            logit = _dot(gin, w2_ref[0, d]) + gb_ref[0, d][0:1, :]
            la = -_softplus(-logit) * (1.0 / B_TAU)
            bc = _dot_exact(incl.astype(BF16), la)
            btot = bc[0:1, :] if rev else bc[CH - 1:CH, :]
            scores = jnp.zeros((CH, CH), F32)
            size = CH
            while size > GLA_DIAG:
                ref = _block_ref_rows(bc, size, rev, before=False)
                ql = (q * jnp.exp(jnp.minimum(bc - ref, 0.0))).astype(BF16)
                kl = (k * jnp.exp(jnp.minimum(ref - bc, 0.0))).astype(BF16)
                half = size // 2
                mask = jnp.logical_and(incl, lax.shift_right_logical(xor, half.bit_length() - 1) == 1)
                scores = scores + jnp.where(mask, _dot_nt(ql, kl), 0.0)
                size = half
            ref = _block_ref_rows(bc, GLA_DIAG, rev, before=True)
            ql = (q * jnp.exp(jnp.minimum(bc - ref, 0.0))).astype(BF16)
            kl = (k * jnp.exp(jnp.minimum(ref - bc, GLA_CLAMP))).astype(BF16)
            mask = jnp.logical_and(incl, lax.shift_right_logical(xor, GLA_DIAG.bit_length() - 1) == 0)
            scores = scores + jnp.where(mask, _dot_nt(ql, kl), 0.0)
            idx = d * nc + c
            oi_ref[idx] = _dot(scores.astype(BF16), v16)
            qs_ref[idx] = (q * jnp.exp(bc)).astype(BF16)
            kd_ref[idx] = (k * jnp.exp(btot - bc)).astype(BF16)
            dec_ref[idx] = jnp.exp(jnp.broadcast_to(btot, (CH, dk)).T)
        return carry

    lax.fori_loop(0, nc, prep, 0)

    def scan(c, carry):
        sf, sb = carry
        cf = c
        cb = nc - 1 - c
        rf = pl.ds(pl.multiple_of(cf * CH, CH), CH)
        rb = pl.ds(pl.multiple_of(cb * CH, CH), CH)
        ib = nc + cb
        oi_ref[cf] = oi_ref[cf] + _dot(qs_ref[cf], sf.astype(BF16))
        oi_ref[ib] = oi_ref[ib] + _dot(qs_ref[ib], sb.astype(BF16))
        vf = p_ref[rf, c_v:c_v + dv].astype(BF16)
        vb = p_ref[rb, c_v:c_v + dv].astype(BF16)
        ef = dec_ref[cf]
        eb = dec_ref[ib]
        sf = sf * jnp.concatenate([ef, ef], axis=1) + _dot_tn(kd_ref[cf], vf)
        sb = sb * jnp.concatenate([eb, eb], axis=1) + _dot_tn(kd_ref[ib], vb)
        return sf, sb

    zero = jnp.zeros((dk, dv), F32)
    lax.fori_loop(0, nc, scan, (zero, zero))

    ng = ng_ref[...]

    def fin(c, carry):
        base = pl.multiple_of(c * CH, CH)
        rows = pl.ds(base, CH)
        o = oi_ref[c] + oi_ref[nc + c]
        r = p_ref[rows, c_r:c_r + dv]
        o = o * lax.rsqrt(jnp.mean(o * o, axis=-1, keepdims=True) + RMS_EPS) * ng
        o_ref[0, rows, :] = (o * (r * _sigmoid(r))).astype(BF16)
        return carry

    lax.fori_loop(0, nc, fin, 0)


def _gla_mixer(xb, w_in, gate_w2, gate_b, norm_g):
    bn, seq, dm = xb.shape
    h, dk, dv = B_HEADS, B_DK, B_DV
    nc = seq // CH
    kw, vw = h * dk, h * dv
    w = w_in

    def heads(cols, width):
        return cols.reshape(dm, h, width).transpose(1, 0, 2)

    gl = jnp.pad(w[:, 2 * kw + 2 * vw:], ((0, 0), (0, dk - 2 * B_RANK)))
    wh = jnp.concatenate([
        heads(w[:, 0:kw], dk), heads(w[:, kw:2 * kw], dk),
        heads(w[:, 2 * kw:2 * kw + vw], dv), heads(w[:, 2 * kw + vw:2 * kw + 2 * vw], dv),
        jnp.broadcast_to(gl[None], (h, dm, dk))], axis=2).astype(BF16)
    w2 = gate_w2.reshape(2, B_RANK, h, dk).transpose(2, 0, 1, 3)
    w2p = jnp.zeros((h, 2, dk, dk), F32)
    w2p = w2p.at[:, 0, 0:B_RANK].set(w2[:, 0]).at[:, 1, B_RANK:2 * B_RANK].set(w2[:, 1]).astype(BF16)
    gb = gate_b.reshape(2, h, dk).transpose(1, 0, 2).astype(F32)
    gb = jnp.broadcast_to(gb[:, :, None, :], (h, 2, 8, dk))
    ng = norm_g.astype(F32).reshape(1, dv)
    nw = wh.shape[2]

    kern = functools.partial(_gla_kernel, seq=seq)
    return pl.pallas_call(
        kern,
        out_shape=jax.ShapeDtypeStruct((bn, seq, vw), BF16),
        grid=(bn, h),
        in_specs=[
            pl.BlockSpec((1, seq, dm), lambda b, i: (b, 0, 0)),
            pl.BlockSpec((1, dm, nw), lambda b, i: (i, 0, 0)),
            pl.BlockSpec((1, 2, dk, dk), lambda b, i: (i, 0, 0, 0)),
            pl.BlockSpec((1, 2, 8, dk), lambda b, i: (i, 0, 0, 0)),
            pl.BlockSpec((1, dv), lambda b, i: (0, 0)),
        ],
        out_specs=pl.BlockSpec((1, seq, dv), lambda b, i: (b, 0, i)),
        scratch_shapes=[
            pltpu.VMEM((seq, nw), F32),
            pltpu.VMEM((2 * nc, CH, dk), BF16),
            pltpu.VMEM((2 * nc, CH, dk), BF16),
            pltpu.VMEM((2 * nc, dk, dk), F32),
            pltpu.VMEM((2 * nc, CH, dv), F32),
        ],
        compiler_params=pltpu.CompilerParams(
            dimension_semantics=("arbitrary", "arbitrary"), vmem_limit_bytes=VMEM_LIMIT),
        name="gla_mixer",
    )(xb, wh, w2p, gb, ng)


def _post_kernel(o_ref, x_ref, wo_ref, w1_ref, w2_ref, ln_ref, y_ref, yb_ref, *, alpha):
    ln = ln_ref[...]
    x = x_ref[...]
    x1 = _layernorm(alpha * x + _dot(o_ref[...], wo_ref[...]), ln[0:1, :], ln[1:2, :])
    x1b = x1.astype(BF16)
    acc = jnp.zeros(x.shape, F32)
    dff = w1_ref.shape[1]
    for j in range(dff // FF_TILE):
        cols = slice(j * FF_TILE, (j + 1) * FF_TILE)
        hcur = jnp.maximum(_dot(x1b, w1_ref[:, cols]), 0.0)
        acc = acc + _dot((hcur * hcur).astype(BF16), w2_ref[cols, :])
    y = _layernorm(alpha * x1 + acc, ln[2:3, :], ln[3:4, :])
    y_ref[...] = y
    yb_ref[...] = y.astype(BF16)


def _post(o, x, w_out, w1, w2, g1, b1, g2, b2, alpha):
    t, dm = x.shape
    vw = o.shape[1]
    dff = w1.shape[1]
    tm = min(ROW_TILE, t)
    ln = jnp.pad(jnp.stack([g1, b1, g2, b2]).astype(F32), ((0, 4), (0, 0)))
    const = lambda shape: pl.BlockSpec(shape, lambda i: (0, 0), pipeline_mode=pl.Buffered(1))
    return pl.pallas_call(
        functools.partial(_post_kernel, alpha=alpha),
        out_shape=(jax.ShapeDtypeStruct((t, dm), F32), jax.ShapeDtypeStruct((t, dm), BF16)),
        grid=(t // tm,),
        in_specs=[
            pl.BlockSpec((tm, vw), lambda i: (i, 0)),
            pl.BlockSpec((tm, dm), lambda i: (i, 0)),
            const((vw, dm)), const((dm, dff)), const((dff, dm)), const((8, dm)),
        ],
        out_specs=(pl.BlockSpec((tm, dm), lambda i: (i, 0)), pl.BlockSpec((tm, dm), lambda i: (i, 0))),
        compiler_params=pltpu.CompilerParams(
            dimension_semantics=("arbitrary",), vmem_limit_bytes=VMEM_LIMIT),
        name="post",
    )(o, x, w_out.astype(BF16), w1.astype(BF16), w2.astype(BF16), ln)


def kernel(x, a_w_in, a_conv, a_alog, a_dt_bias, a_norm_g, a_w_out, b_w_in, b_gate_w2, b_gate_b,
           b_norm_g, b_w_out, ln1_g, ln1_b, mlp_w1, mlp_w2, ln2_g, ln2_b):
    bn, seq, dm = x.shape
    depth = ln1_g.shape[0]
    alpha = (2 * depth) ** 0.25
    xf = x.astype(F32).reshape(bn * seq, dm)
    xb = xf.astype(BF16)
    for i in range(depth):
        j = i // 2
        xb3 = xb.reshape(bn, seq, dm)
        if i % 2 == 0:
            o = _gdn_mixer(xb3, a_w_in[j], a_conv[j], a_alog[j], a_dt_bias[j], a_norm_g[j])
            w_out = a_w_out[j]
        else:
            o = _gla_mixer(xb3, b_w_in[j], b_gate_w2[j], b_gate_b[j], b_norm_g[j])
            w_out = b_w_out[j]
        xf, xb = _post(o.reshape(bn * seq, -1), xf, w_out, mlp_w1[i], mlp_w2[i],
                       ln1_g[i], ln1_b[i], ln2_g[i], ln2_b[i], alpha)
    return xf.reshape(bn, seq, dm).astype(x.dtype)
```

```python
import functools

import jax
import jax.numpy as jnp
from jax import lax
from jax.experimental import pallas as pl
from jax.experimental.pallas import tpu as pltpu

F32 = jnp.float32
BF16 = jnp.bfloat16

A_HEADS, A_DK, A_DV, A_CONV = 8, 128, 128, 5
B_HEADS, B_DK, B_DV, B_RANK, B_TAU = 4, 128, 256, 16, 16.0
LN_EPS, RMS_EPS, L2_EPS = 1e-5, 1e-6, 1e-6

CH = 128
HALO = 8
SOLVE_GROUP = 16
GLA_GROUP = 2
GLA_DIAG = 16
GLA_CLAMP = 60.0
NEG_BIG = -1e30
VMEM_LIMIT = 56 * 1024 * 1024
ROW_TILE = 512
FF_TILE = 1024


def _dot(a, b):
    return jnp.dot(a, b, preferred_element_type=F32)


def _dot_nt(a, b):
    return lax.dot_general(a, b, (((1,), (1,)), ((), ())), preferred_element_type=F32)


def _dot_tn(a, b):
    return lax.dot_general(a, b, (((0,), (0,)), ((), ())), preferred_element_type=F32)


def _dot_exact(m01, x):
    hi = x.astype(BF16)
    r1 = x - hi.astype(F32)
    mid = r1.astype(BF16)
    lo = (r1 - mid.astype(F32)).astype(BF16)
    return _dot(m01, hi) + _dot(m01, mid) + _dot(m01, lo)


def _sigmoid(x):
    return 1.0 / (1.0 + jnp.exp(-x))


def _softplus(x):
    return jnp.maximum(x, 0.0) + jnp.log(1.0 + jnp.exp(-jnp.abs(x)))


def _layernorm(y, g, b):
    mu = jnp.mean(y, axis=-1, keepdims=True)
    yc = y - mu
    var = jnp.mean(yc * yc, axis=-1, keepdims=True)
    return yc * lax.rsqrt(var + LN_EPS) * g + b


def _order_masks(rev):
    row = lax.broadcasted_iota(jnp.int32, (CH, CH), 0)
    col = lax.broadcasted_iota(jnp.int32, (CH, CH), 1)
    if rev:
        return col >= row, col > row
    return col <= row, col < row


def _xor_index():
    row = lax.broadcasted_iota(jnp.int32, (CH, CH), 0)
    col = lax.broadcasted_iota(jnp.int32, (CH, CH), 1)
    return row ^ col


def _gdn_kernel(xb_ref, wh_ref, cw_ref, hp_ref, ng_ref, lm_ref, o_ref,
                p_ref, a_ref, qk_ref, rhs_ref, qd_ref, kd_ref, gl_ref,
                mc_ref, qc_ref, rc_ref, oc_ref, *, seq):
    nc = seq // CH
    n_levels = lm_ref.shape[0]

    p_ref[0:HALO, :] = jnp.zeros((HALO, p_ref.shape[1]), F32)
    p_ref[HALO + seq:, :] = jnp.zeros((HALO, p_ref.shape[1]), F32)
    p_ref[HALO:HALO + seq, :] = _dot(xb_ref[0], wh_ref[0])

    cw = cw_ref[0]
    hp = hp_ref[0]

    def prep(c, carry):
        base = pl.multiple_of(c * CH, CH)
        win = p_ref[pl.ds(base, CH + 2 * HALO), 0:3 * A_DK]
        acc = jnp.zeros((CH, 3 * A_DK), F32)
        for i in range(A_CONV):
            off = HALO + i - A_CONV // 2
            acc = acc + win[off:off + CH, :] * cw[i:i + 1, :]
        s = acc * _sigmoid(acc)
        q = s[:, 0:A_DK]
        k = s[:, A_DK:2 * A_DK]
        v = s[:, 2 * A_DK:3 * A_DK]
        q = q * lax.rsqrt(jnp.sum(q * q, axis=-1, keepdims=True) + L2_EPS) * (A_DK ** -0.5)
        k = k * lax.rsqrt(jnp.sum(k * k, axis=-1, keepdims=True) + L2_EPS)
        ba = p_ref[pl.ds(base + HALO, CH), 4 * A_DK:8 * A_DK]
        kb16 = k.astype(BF16)
        for d, rev in ((0, False), (1, True)):
            incl, strict = _order_masks(rev)
            beta = _sigmoid(ba[:, d * A_DK:(d + 1) * A_DK])
            g = -hp[d:d + 1, :] * _softplus(ba[:, (2 + d) * A_DK:(3 + d) * A_DK] + hp[2 + d:3 + d, :])
            gc = _dot_exact(incl.astype(BF16), g)
            gtot = gc[0:1, :] if rev else gc[CH - 1:CH, :]
            eg = jnp.exp(gc)
            kbeta = k * beta
            dmat = jnp.exp(jnp.where(incl, gc - gc.T, NEG_BIG))
            kq = _dot_nt(jnp.concatenate([kbeta, q], axis=0).astype(BF16), kb16)
            idx = d * nc + c
            a_ref[idx] = jnp.where(strict, kq[0:CH] * dmat, 0.0).astype(BF16)
            qk_ref[idx] = (kq[CH:] * dmat).astype(BF16)
            rhs_ref[idx] = jnp.concatenate([v * beta, kbeta * eg], axis=1).astype(BF16)
            qd_ref[idx] = q * eg
            kd_ref[idx] = (k * jnp.exp(gtot - gc)).astype(BF16)
            gl_ref[idx] = jnp.broadcast_to(jnp.exp(gtot), (8, A_DK))
        return carry

    lax.fori_loop(0, nc, prep, 0)

    def solve(it, carry):
        ids = [it * SOLVE_GROUP + u for u in range(SOLVE_GROUP)]
        eye = (lax.broadcasted_iota(jnp.int32, (CH, CH), 0)
               == lax.broadcasted_iota(jnp.int32, (CH, CH), 1)).astype(F32)
        ts = [(eye - (a_ref[i] * lm_ref[0]).astype(F32)).astype(BF16) for i in ids]
        for lv in range(1, n_levels):
            ys = [_dot(_dot(t, a_ref[i] * lm_ref[lv]).astype(BF16), t) for i, t in zip(ids, ts)]
            ts = [t - y.astype(BF16) for t, y in zip(ts, ys)]
        for i, t in zip(ids, ts):
            uw = _dot(t, rhs_ref[i]).astype(BF16)
            kuw = _dot_tn(kd_ref[i], uw)
            quw = _dot(qk_ref[i], uw)
            qc_ref[i] = kuw[:, 0:A_DV]
            mc_ref[i] = kuw[:, A_DV:].astype(BF16)
            oc_ref[i] = quw[:, 0:A_DV]
            rc_ref[i] = (qd_ref[i] - quw[:, A_DV:]).astype(BF16)
        return carry

    lax.fori_loop(0, 2 * nc // SOLVE_GROUP, solve, 0)

    def scan(c, carry):
        sf, sb = carry
        cf = c
        cb = 2 * nc - 1 - c
        of = _dot(rc_ref[cf], sf.astype(BF16)) + oc_ref[cf]
        ob = _dot(rc_ref[cb], sb.astype(BF16)) + oc_ref[cb]
        oc_ref[cf] = of
        oc_ref[cb] = ob
        sf = gl_ref[cf][0:1, :] * sf - _dot(mc_ref[cf], sf.astype(BF16)) + qc_ref[cf]
        sb = gl_ref[cb][0:1, :] * sb - _dot(mc_ref[cb], sb.astype(BF16)) + qc_ref[cb]
        return sf, sb

    zero = jnp.zeros((A_DK, A_DV), F32)
    lax.fori_loop(0, nc, scan, (zero, zero))

    ng = ng_ref[...]

    def fin(c, carry):
        base = pl.multiple_of(c * CH, CH)
        o = oc_ref[c] + oc_ref[nc + c]
        z = p_ref[pl.ds(base + HALO, CH), 3 * A_DK:4 * A_DK]
        o = o * lax.rsqrt(jnp.mean(o * o, axis=-1, keepdims=True) + RMS_EPS) * ng
        o_ref[0, pl.ds(base, CH), :] = (o * (z * _sigmoid(z))).astype(BF16)
        return carry

    lax.fori_loop(0, nc, fin, 0)


def _gdn_level_masks():
    idx = jnp.arange(CH)
    x = idx[:, None] ^ idx[None, :]
    levels = []
    s = 1
    while s < CH:
        levels.append(((x // s) == 1).astype(BF16))
        s *= 2
    return jnp.stack(levels)


def _gdn_mixer(xb, w_in, conv_w, a_log, dt_bias, norm_g):
    bn, seq, dm = xb.shape
    h, dk = A_HEADS, A_DK
    nc = seq // CH
    w = w_in
    hw = h * dk
    ba = w[:, 4 * hw:].reshape(dm, 2, 2, h)
    per_head = [w[:, i * hw:(i + 1) * hw].reshape(dm, h, dk).transpose(1, 0, 2) for i in range(4)]
    rep = [jnp.broadcast_to(ba[:, kind, d, :].T[:, :, None], (h, dm, dk))
           for kind in range(2) for d in range(2)]
    wh = jnp.concatenate(per_head + rep, axis=2).astype(BF16)
    cw = conv_w.reshape(A_CONV, 3, h, dk).transpose(2, 0, 1, 3).reshape(h, A_CONV, 3 * dk)
    cw = jnp.pad(cw, ((0, 0), (0, 8 - A_CONV), (0, 0))).astype(F32)
    hp = jnp.concatenate([jnp.exp(a_log.astype(F32)), dt_bias.astype(F32)], axis=0)
    hp = jnp.broadcast_to(jnp.pad(hp, ((0, 4), (0, 0))).T[:, :, None], (h, 8, dk))
    ng = norm_g.astype(F32).reshape(1, A_DV)
    lm = _gdn_level_masks()
    nw = wh.shape[2]

    kern = functools.partial(_gdn_kernel, seq=seq)
    tile = lambda dt: pltpu.VMEM((2 * nc, CH, CH), dt)
    return pl.pallas_call(
        kern,
        out_shape=jax.ShapeDtypeStruct((bn, seq, h * A_DV), BF16),
        grid=(bn, h),
        in_specs=[
            pl.BlockSpec((1, seq, dm), lambda b, i: (b, 0, 0)),
            pl.BlockSpec((1, dm, nw), lambda b, i: (i, 0, 0)),
            pl.BlockSpec((1, 8, 3 * dk), lambda b, i: (i, 0, 0)),
            pl.BlockSpec((1, 8, dk), lambda b, i: (i, 0, 0)),
            pl.BlockSpec((1, A_DV), lambda b, i: (0, 0)),
            pl.BlockSpec(lm.shape, lambda b, i: (0, 0, 0)),
        ],
        out_specs=pl.BlockSpec((1, seq, A_DV), lambda b, i: (b, 0, i)),
        scratch_shapes=[
            pltpu.VMEM((seq + 2 * HALO, nw), F32),
            tile(BF16),
            tile(BF16),
            pltpu.VMEM((2 * nc, CH, A_DV + A_DK), BF16),
            tile(F32),
            tile(BF16),
            pltpu.VMEM((2 * nc, 8, A_DK), F32),
            tile(BF16),
            tile(F32),
            tile(BF16),
            tile(F32),
        ],
        compiler_params=pltpu.CompilerParams(
            dimension_semantics=("arbitrary", "arbitrary"), vmem_limit_bytes=VMEM_LIMIT),
        name="gdn_mixer",
    )(xb, wh, cw, hp, ng, lm)


def _block_ref_rows(bc, size, rev, before):
    pieces = []
    for m in range(CH // size):
        lo = m * size
        if before:
            r = lo + size if rev else lo - 1
            if r < 0 or r >= CH:
                row = jnp.zeros((1, bc.shape[1]), F32)
            else:
                row = bc[r:r + 1, :]
        else:
            r = lo + size // 2 if rev else lo + size // 2 - 1
            row = bc[r:r + 1, :]
        pieces.append(jnp.broadcast_to(row, (size, bc.shape[1])))
    return jnp.concatenate(pieces, axis=0)


def _gla_kernel(xb_ref, wh_ref, w2_ref, gb_ref, ng_ref, o_ref,
                p_ref, qs_ref, kd_ref, dec_ref, oi_ref, *, seq):
    nc = seq // CH
    dk, dv = B_DK, B_DV
    p_ref[...] = _dot(xb_ref[0], wh_ref[0])

    c_q, c_k, c_v, c_r, c_g = 0, dk, 2 * dk, 2 * dk + dv, 2 * dk + 2 * dv

    def prep_chunk(c):
        base = pl.multiple_of(c * CH, CH)
        rows = pl.ds(base, CH)
        q = p_ref[rows, c_q:c_q + dk] * (dk ** -0.5)
        k = p_ref[rows, c_k:c_k + dk]
        v16 = p_ref[rows, c_v:c_v + dv].astype(BF16)
        gin = p_ref[rows, c_g:c_g + dk].astype(BF16)
        xor = _xor_index()
        for d, rev in ((0, False), (1, True)):
            incl, _ = _order_masks(rev)
            logit = _dot(gin, w2_ref[0, d]) + gb_ref[0, d][0:1, :]
            la = -_softplus(-logit) * (1.0 / B_TAU)
            bc = _dot_exact(incl.astype(BF16), la)
            btot = bc[0:1, :] if rev else bc[CH - 1:CH, :]
            scores = jnp.zeros((CH, CH), F32)
            size = CH
            while size > GLA_DIAG:
                ref = _block_ref_rows(bc, size, rev, before=False)
                ql = (q * jnp.exp(jnp.minimum(bc - ref, 0.0))).astype(BF16)
                kl = (k * jnp.exp(jnp.minimum(ref - bc, 0.0))).astype(BF16)
                half = size // 2
                mask = jnp.logical_and(incl, lax.shift_right_logical(xor, half.bit_length() - 1) == 1)
                scores = scores + jnp.where(mask, _dot_nt(ql, kl), 0.0)
                size = half
            ref = _block_ref_rows(bc, GLA_DIAG, rev, before=True)
            ql = (q * jnp.exp(jnp.minimum(bc - ref, 0.0))).astype(BF16)
            kl = (k * jnp.exp(jnp.minimum(ref - bc, GLA_CLAMP))).astype(BF16)
            mask = jnp.logical_and(incl, lax.shift_right_logical(xor, GLA_DIAG.bit_length() - 1) == 0)
            scores = scores + jnp.where(mask, _dot_nt(ql, kl), 0.0)
            idx = d * nc + c
            oi_ref[idx] = _dot(scores.astype(BF16), v16)
            qs_ref[idx] = (q * jnp.exp(bc)).astype(BF16)
            kd_ref[idx] = (k * jnp.exp(btot - bc)).astype(BF16)
            dec_ref[idx] = jnp.exp(jnp.broadcast_to(btot, (CH, dk)).T)

    def prep(it, carry):
        for u in range(GLA_GROUP):
            prep_chunk(it * GLA_GROUP + u)
        return carry

    lax.fori_loop(0, nc // GLA_GROUP, prep, 0)

    def scan(c, carry):
        sf, sb = carry
        cf = c
        cb = nc - 1 - c
        rf = pl.ds(pl.multiple_of(cf * CH, CH), CH)
        rb = pl.ds(pl.multiple_of(cb * CH, CH), CH)
        ib = nc + cb
        oi_ref[cf] = oi_ref[cf] + _dot(qs_ref[cf], sf.astype(BF16))
        oi_ref[ib] = oi_ref[ib] + _dot(qs_ref[ib], sb.astype(BF16))
        vf = p_ref[rf, c_v:c_v + dv].astype(BF16)
        vb = p_ref[rb, c_v:c_v + dv].astype(BF16)
        ef = dec_ref[cf]
        eb = dec_ref[ib]
        sf = sf * jnp.concatenate([ef, ef], axis=1) + _dot_tn(kd_ref[cf], vf)
        sb = sb * jnp.concatenate([eb, eb], axis=1) + _dot_tn(kd_ref[ib], vb)
        return sf, sb

    zero = jnp.zeros((dk, dv), F32)
    lax.fori_loop(0, nc, scan, (zero, zero))

    ng = ng_ref[...]

    def fin(c, carry):
        base = pl.multiple_of(c * CH, CH)
        rows = pl.ds(base, CH)
        o = oi_ref[c] + oi_ref[nc + c]
        r = p_ref[rows, c_r:c_r + dv]
        o = o * lax.rsqrt(jnp.mean(o * o, axis=-1, keepdims=True) + RMS_EPS) * ng
        o_ref[0, rows, :] = (o * (r * _sigmoid(r))).astype(BF16)
        return carry

    lax.fori_loop(0, nc, fin, 0)


def _gla_mixer(xb, w_in, gate_w2, gate_b, norm_g):
    bn, seq, dm = xb.shape
    h, dk, dv = B_HEADS, B_DK, B_DV
    nc = seq // CH
    kw, vw = h * dk, h * dv
    w = w_in

    def heads(cols, width):
        return cols.reshape(dm, h, width).transpose(1, 0, 2)

    gl = jnp.pad(w[:, 2 * kw + 2 * vw:], ((0, 0), (0, dk - 2 * B_RANK)))
    wh = jnp.concatenate([
        heads(w[:, 0:kw], dk), heads(w[:, kw:2 * kw], dk),
        heads(w[:, 2 * kw:2 * kw + vw], dv), heads(w[:, 2 * kw + vw:2 * kw + 2 * vw], dv),
        jnp.broadcast_to(gl[None], (h, dm, dk))], axis=2).astype(BF16)
    w2 = gate_w2.reshape(2, B_RANK, h, dk).transpose(2, 0, 1, 3)
    w2p = jnp.zeros((h, 2, dk, dk), F32)
    w2p = w2p.at[:, 0, 0:B_RANK].set(w2[:, 0]).at[:, 1, B_RANK:2 * B_RANK].set(w2[:, 1]).astype(BF16)
    gb = gate_b.reshape(2, h, dk).transpose(1, 0, 2).astype(F32)
    gb = jnp.broadcast_to(gb[:, :, None, :], (h, 2, 8, dk))
    ng = norm_g.astype(F32).reshape(1, dv)
    nw = wh.shape[2]

    kern = functools.partial(_gla_kernel, seq=seq)
    return pl.pallas_call(
        kern,
        out_shape=jax.ShapeDtypeStruct((bn, seq, vw), BF16),
        grid=(bn, h),
        in_specs=[
            pl.BlockSpec((1, seq, dm), lambda b, i: (b, 0, 0)),
            pl.BlockSpec((1, dm, nw), lambda b, i: (i, 0, 0)),
            pl.BlockSpec((1, 2, dk, dk), lambda b, i: (i, 0, 0, 0)),
            pl.BlockSpec((1, 2, 8, dk), lambda b, i: (i, 0, 0, 0)),
            pl.BlockSpec((1, dv), lambda b, i: (0, 0)),
        ],
        out_specs=pl.BlockSpec((1, seq, dv), lambda b, i: (b, 0, i)),
        scratch_shapes=[
            pltpu.VMEM((seq, nw), F32),
            pltpu.VMEM((2 * nc, CH, dk), BF16),
            pltpu.VMEM((2 * nc, CH, dk), BF16),
            pltpu.VMEM((2 * nc, dk, dk), F32),
            pltpu.VMEM((2 * nc, CH, dv), F32),
        ],
        compiler_params=pltpu.CompilerParams(
            dimension_semantics=("arbitrary", "arbitrary"), vmem_limit_bytes=VMEM_LIMIT),
        name="gla_mixer",
    )(xb, wh, w2p, gb, ng)


def _post_kernel(o_ref, x_ref, wo_ref, w1_ref, w2_ref, ln_ref, y_ref, yb_ref, *, alpha):
    ln = ln_ref[...]
    x = x_ref[...]
    x1 = _layernorm(alpha * x + _dot(o_ref[...], wo_ref[...]), ln[0:1, :], ln[1:2, :])
    x1b = x1.astype(BF16)
    acc = jnp.zeros(x.shape, F32)
    dff = w1_ref.shape[1]
    for j in range(dff // FF_TILE):
        cols = slice(j * FF_TILE, (j + 1) * FF_TILE)
        hcur = jnp.maximum(_dot(x1b, w1_ref[:, cols]), 0.0)
        acc = acc + _dot((hcur * hcur).astype(BF16), w2_ref[cols, :])
    y = _layernorm(alpha * x1 + acc, ln[2:3, :], ln[3:4, :])
    y_ref[...] = y
    yb_ref[...] = y.astype(BF16)


def _post(o, x, w_out, w1, w2, g1, b1, g2, b2, alpha):
    t, dm = x.shape
    vw = o.shape[1]
    dff = w1.shape[1]
    tm = min(ROW_TILE, t)
    ln = jnp.pad(jnp.stack([g1, b1, g2, b2]).astype(F32), ((0, 4), (0, 0)))
    const = lambda shape: pl.BlockSpec(shape, lambda i: (0, 0), pipeline_mode=pl.Buffered(1))
    return pl.pallas_call(
        functools.partial(_post_kernel, alpha=alpha),
        out_shape=(jax.ShapeDtypeStruct((t, dm), F32), jax.ShapeDtypeStruct((t, dm), BF16)),
        grid=(t // tm,),
        in_specs=[
            pl.BlockSpec((tm, vw), lambda i: (i, 0)),
            pl.BlockSpec((tm, dm), lambda i: (i, 0)),
            const((vw, dm)), const((dm, dff)), const((dff, dm)), const((8, dm)),
        ],
        out_specs=(pl.BlockSpec((tm, dm), lambda i: (i, 0)), pl.BlockSpec((tm, dm), lambda i: (i, 0))),
        compiler_params=pltpu.CompilerParams(
            dimension_semantics=("arbitrary",), vmem_limit_bytes=VMEM_LIMIT),
        name="post",
    )(o, x, w_out.astype(BF16), w1.astype(BF16), w2.astype(BF16), ln)


def kernel(x, a_w_in, a_conv, a_alog, a_dt_bias, a_norm_g, a_w_out, b_w_in, b_gate_w2, b_gate_b,
           b_norm_g, b_w_out, ln1_g, ln1_b, mlp_w1, mlp_w2, ln2_g, ln2_b):
    bn, seq, dm = x.shape
    depth = ln1_g.shape[0]
    alpha = (2 * depth) ** 0.25
    xf = x.astype(F32).reshape(bn * seq, dm)
    xb = xf.astype(BF16)
    for i in range(depth):
        j = i // 2
        xb3 = xb.reshape(bn, seq, dm)
        if i % 2 == 0:
            o = _gdn_mixer(xb3, a_w_in[j], a_conv[j], a_alog[j], a_dt_bias[j], a_norm_g[j])
            w_out = a_w_out[j]
        else:
            o = _gla_mixer(xb3, b_w_in[j], b_gate_w2[j], b_gate_b[j], b_norm_g[j])
            w_out = b_w_out[j]
        xf, xb = _post(o.reshape(bn * seq, -1), xf, w_out, mlp_w1[i], mlp_w2[i],
                       ln1_g[i], ln1_b[i], ln2_g[i], ln2_b[i], alpha)
    return xf.reshape(bn, seq, dm).astype(x.dtype)
```

```python
import functools
import math

import numpy as np

import jax
import jax.numpy as jnp
from jax import lax
from jax.experimental import pallas as pl
from jax.experimental.pallas import tpu as pltpu

F32 = jnp.float32
BF16 = jnp.bfloat16

A_HEADS, A_DK, A_DV, A_CONV = 8, 128, 128, 5
B_HEADS, B_DK, B_DV, B_RANK, B_TAU = 4, 128, 256, 16, 16.0
LN_EPS, RMS_EPS, L2_EPS = 1e-5, 1e-6, 1e-6

CH = 128
N_LEVELS = 7
HALO = 8
SOLVE_GROUP = 16
GLA_GROUP = 2
GLA_PIECES = 2
FIN_GROUP = 4
NEG_BIG = -1e30
VMEM_LIMIT = 56 * 1024 * 1024
ROW_TILE = 512
FF_TILE = 1024

assert CH == A_DK == B_DK and 2 ** N_LEVELS == CH


def _dot(a, b):
    return jnp.dot(a, b, preferred_element_type=F32)


def _dot_nt(a, b):
    return lax.dot_general(a, b, (((1,), (1,)), ((), ())), preferred_element_type=F32)


def _dot_tn(a, b):
    return lax.dot_general(a, b, (((0,), (0,)), ((), ())), preferred_element_type=F32)


def _split(x, n):
    pieces = []
    for _ in range(n - 1):
        p = x.astype(BF16)
        pieces.append(p)
        x = x - p.astype(F32)
    pieces.append(x.astype(BF16))
    return jnp.concatenate(pieces, axis=1)


def _fold(y, n):
    w = y.shape[1] // n
    out = y[:, 0:w]
    for i in range(1, n):
        out = out + y[:, i * w:(i + 1) * w]
    return out


def _dot_exact(m01, x):
    return _fold(_dot(m01, _split(x, 3)), 3)


def _sigmoid(x):
    return 0.5 + 0.5 * jnp.tanh(0.5 * x)


def _silu(x):
    h = 0.5 * x
    return h + h * jnp.tanh(h)


def _softplus(x):
    return jnp.maximum(x, 0.0) + jnp.log(1.0 + jnp.exp(-jnp.abs(x)))


def _layernorm(y, g, b):
    mu = jnp.mean(y, axis=-1, keepdims=True)
    yc = y - mu
    var = jnp.mean(yc * yc, axis=-1, keepdims=True)
    return yc * lax.rsqrt(var + LN_EPS) * g + b


def _order_masks(rev):
    row = lax.broadcasted_iota(jnp.int32, (CH, CH), 0)
    col = lax.broadcasted_iota(jnp.int32, (CH, CH), 1)
    if rev:
        return col >= row, col > row
    return col <= row, col < row


def _group(n, want):
    return math.gcd(n, want)


def _gdn_kernel(xb_ref, wh_ref, cw_ref, hp_ref, ng_ref, lm_ref, o_ref,
                p_ref, a_ref, qk_ref, rhs_ref, qd_ref, kd_ref, gl_ref,
                mc_ref, qc_ref, rc_ref, oc_ref, *, seq):
    nc = seq // CH

    p_ref[0:HALO, :] = jnp.zeros((HALO, p_ref.shape[1]), F32)
    p_ref[HALO + seq:, :] = jnp.zeros((HALO, p_ref.shape[1]), F32)
    p_ref[HALO:HALO + seq, :] = _dot(xb_ref[0], wh_ref[0])

    cw = cw_ref[0]
    hp = hp_ref[0]

    def prep(c, carry):
        base = pl.multiple_of(c * CH, CH)
        win = p_ref[pl.ds(base, CH + 2 * HALO), 0:3 * A_DK]
        acc = jnp.zeros((CH, 3 * A_DK), F32)
        for i in range(A_CONV):
            off = HALO + i - A_CONV // 2
            acc = acc + win[off:off + CH, :] * cw[i:i + 1, :]
        s = _silu(acc)
        q = s[:, 0:A_DK]
        k = s[:, A_DK:2 * A_DK]
        v = s[:, 2 * A_DK:3 * A_DK]
        q = q * (lax.rsqrt(jnp.sum(q * q, axis=-1, keepdims=True) + L2_EPS) * (A_DK ** -0.5))
        k = k * lax.rsqrt(jnp.sum(k * k, axis=-1, keepdims=True) + L2_EPS)
        ba = p_ref[pl.ds(base + HALO, CH), 4 * A_DK:8 * A_DK]
        kb16 = k.astype(BF16)
        for d, rev in ((0, False), (1, True)):
            incl, strict = _order_masks(rev)
            beta = _sigmoid(ba[:, d * A_DK:(d + 1) * A_DK])
            g = -hp[d:d + 1, :] * _softplus(ba[:, (2 + d) * A_DK:(3 + d) * A_DK] + hp[2 + d:3 + d, :])
            gc = _dot_exact(incl.astype(BF16), g)
            gtot = gc[0:1, :] if rev else gc[CH - 1:CH, :]
            eg = jnp.exp(gc)
            kbeta = k * beta
            dmat = jnp.exp(jnp.where(incl, gc - gc.T, NEG_BIG))
            kq = _dot_nt(jnp.concatenate([kbeta, q], axis=0).astype(BF16), kb16)
            idx = d * nc + c
            a_ref[idx] = jnp.where(strict, kq[0:CH] * dmat, 0.0).astype(BF16)
            qk_ref[idx] = (kq[CH:] * dmat).astype(BF16)
            rhs_ref[idx] = jnp.concatenate([v * beta, kbeta * eg], axis=1).astype(BF16)
            qd_ref[idx] = q * eg
            kd_ref[idx] = (k * jnp.exp(gtot - gc)).astype(BF16)
            gl_ref[idx] = jnp.broadcast_to(jnp.exp(gtot), (8, A_DK))
        return carry

    lax.fori_loop(0, nc, prep, 0)

    sg = _group(2 * nc, SOLVE_GROUP)

    def solve(it, carry):
        ids = [it * sg + u for u in range(sg)]
        eye = (lax.broadcasted_iota(jnp.int32, (CH, CH), 0)
               == lax.broadcasted_iota(jnp.int32, (CH, CH), 1)).astype(F32)
        ts = [(eye - (a_ref[i] * lm_ref[0]).astype(F32)).astype(BF16) for i in ids]
        for lv in range(1, N_LEVELS):
            ys = [_dot(_dot(t, a_ref[i] * lm_ref[lv]).astype(BF16), t) for i, t in zip(ids, ts)]
            ts = [t - y.astype(BF16) for t, y in zip(ts, ys)]
        for i, t in zip(ids, ts):
            uw = _dot(t, rhs_ref[i]).astype(BF16)
            kuw = _dot_tn(kd_ref[i], uw)
            quw = _dot(qk_ref[i], uw)
            qc_ref[i] = kuw[:, 0:A_DV]
            mc_ref[i] = kuw[:, A_DV:].astype(BF16)
            oc_ref[i] = quw[:, 0:A_DV]
            rc_ref[i] = (qd_ref[i] - quw[:, A_DV:]).astype(BF16)
        return carry

    lax.fori_loop(0, 2 * nc // sg, solve, 0)

    def scan(c, carry):
        sf, sb = carry
        cf = c
        cb = 2 * nc - 1 - c
        of = _dot(rc_ref[cf], sf.astype(BF16)) + oc_ref[cf]
        ob = _dot(rc_ref[cb], sb.astype(BF16)) + oc_ref[cb]
        oc_ref[cf] = of
        oc_ref[cb] = ob
        sf = gl_ref[cf][0:1, :] * sf - _dot(mc_ref[cf], sf.astype(BF16)) + qc_ref[cf]
        sb = gl_ref[cb][0:1, :] * sb - _dot(mc_ref[cb], sb.astype(BF16)) + qc_ref[cb]
        return sf, sb

    zero = jnp.zeros((A_DK, A_DV), F32)
    lax.fori_loop(0, nc, scan, (zero, zero))

    ng = ng_ref[...]
    fg = _group(nc, FIN_GROUP)

    def fin(it, carry):
        cs = [it * fg + u for u in range(fg)]
        rows = [pl.ds(pl.multiple_of(c * CH, CH), CH) for c in cs]
        os_ = [oc_ref[c] + oc_ref[nc + c] for c in cs]
        inv = [lax.rsqrt(jnp.mean(o * o, axis=-1, keepdims=True) + RMS_EPS) for o in os_]
        for c, r, o, s in zip(cs, rows, os_, inv):
            z = p_ref[pl.ds(pl.multiple_of(c * CH, CH) + HALO, CH), 3 * A_DK:4 * A_DK]
            o_ref[0, r, :] = (o * s * ng * _silu(z)).astype(BF16)
        return carry

    lax.fori_loop(0, nc // fg, fin, 0)


def _gdn_level_masks():
    idx = np.arange(CH)
    x = idx[:, None] ^ idx[None, :]
    return jnp.asarray(np.stack([(x >> lv) == 1 for lv in range(N_LEVELS)]), BF16)


def _gdn_mixer(xb, w_in, conv_w, a_log, dt_bias, norm_g):
    bn, seq, dm = xb.shape
    h, dk = A_HEADS, A_DK
    nc = seq // CH
    w = w_in
    hw = h * dk
    ba = w[:, 4 * hw:].reshape(dm, 2, 2, h)
    per_head = [w[:, i * hw:(i + 1) * hw].reshape(dm, h, dk).transpose(1, 0, 2) for i in range(4)]
    rep = [jnp.broadcast_to(ba[:, kind, d, :].T[:, :, None], (h, dm, dk))
           for kind in range(2) for d in range(2)]
    wh = jnp.concatenate(per_head + rep, axis=2).astype(BF16)
    cw = conv_w.reshape(A_CONV, 3, h, dk).transpose(2, 0, 1, 3).reshape(h, A_CONV, 3 * dk)
    cw = jnp.pad(cw, ((0, 0), (0, 8 - A_CONV), (0, 0))).astype(F32)
    hp = jnp.concatenate([jnp.exp(a_log.astype(F32)), dt_bias.astype(F32)], axis=0)
    hp = jnp.broadcast_to(jnp.pad(hp, ((0, 4), (0, 0))).T[:, :, None], (h, 8, dk))
    ng = norm_g.astype(F32).reshape(1, A_DV)
    lm = _gdn_level_masks()
    nw = wh.shape[2]

    kern = functools.partial(_gdn_kernel, seq=seq)
    tile = lambda dt: pltpu.VMEM((2 * nc, CH, CH), dt)
    return pl.pallas_call(
        kern,
        out_shape=jax.ShapeDtypeStruct((bn, seq, h * A_DV), BF16),
        grid=(bn, h),
        in_specs=[
            pl.BlockSpec((1, seq, dm), lambda b, i: (b, 0, 0)),
            pl.BlockSpec((1, dm, nw), lambda b, i: (i, 0, 0)),
            pl.BlockSpec((1, 8, 3 * dk), lambda b, i: (i, 0, 0)),
            pl.BlockSpec((1, 8, dk), lambda b, i: (i, 0, 0)),
            pl.BlockSpec((1, A_DV), lambda b, i: (0, 0)),
            pl.BlockSpec(lm.shape, lambda b, i: (0, 0, 0)),
        ],
        out_specs=pl.BlockSpec((1, seq, A_DV), lambda b, i: (b, 0, i)),
        scratch_shapes=[
            pltpu.VMEM((seq + 2 * HALO, nw), F32),
            tile(BF16),
            tile(BF16),
            pltpu.VMEM((2 * nc, CH, A_DV + A_DK), BF16),
            tile(F32),
            tile(BF16),
            pltpu.VMEM((2 * nc, 8, A_DK), F32),
            tile(BF16),
            tile(F32),
            tile(BF16),
            tile(F32),
        ],
        compiler_params=pltpu.CompilerParams(
            dimension_semantics=("arbitrary", "arbitrary"), vmem_limit_bytes=VMEM_LIMIT),
        name="gdn_mixer",
    )(xb, wh, cw, hp, ng, lm)


def _gla_tables():
    i = np.arange(CH)[:, None]
    t = np.arange(CH)[None, :]
    seg = np.zeros((2, N_LEVELS + 1, CH, CH), np.float32)
    lvl = np.zeros((2, CH, CH), np.int32)
    for d in range(2):
        rev = d == 1
        seg[d, 0] = (t >= i) if rev else (t <= i)
        lv = np.full((CH, CH), N_LEVELS + 1, np.int32)
        lv[np.arange(CH), np.arange(CH)] = N_LEVELS
        x = i ^ t
        for l in range(N_LEVELS):
            h = 2 ** (N_LEVELS - 1 - l)
            b0 = (i // (2 * h)) * (2 * h)
            if rev:
                r = b0 + h
                late = i < r
                m = np.where(late, (t >= i) & (t < r), (t >= r) & (t < i))
                own = ((x >> (N_LEVELS - 1 - l)) == 1) & (t > i)
            else:
                r = b0 + h - 1
                late = i > r
                m = np.where(late, (t > r) & (t <= i), (t > i) & (t <= r))
                own = ((x >> (N_LEVELS - 1 - l)) == 1) & (t < i)
            seg[d, 1 + l] = m
            lv[own] = l
        lvl[d] = lv
    return jnp.asarray(seg, BF16), jnp.asarray(lvl)


def _gla_kernel(xb_ref, wh_ref, w2_ref, gb_ref, ng_ref, seg_ref, lvl_ref, o_ref,
                p_ref, qs_ref, kd_ref, dec_ref, oi_ref, *, seq):
    nc = seq // CH
    dk, dv = B_DK, B_DV
    p_ref[...] = _dot(xb_ref[0], wh_ref[0])

    c_q, c_k, c_v, c_r, c_g = 0, dk, 2 * dk, 2 * dk + dv, 2 * dk + 2 * dv
    gg = _group(nc, GLA_GROUP)
    lanes = [(u, d) for u in range(gg) for d in range(2)]

    def prep(it, carry):
        cs = [it * gg + u for u in range(gg)]
        rows = [pl.ds(pl.multiple_of(c * CH, CH), CH) for c in cs]
        q = [p_ref[r, c_q:c_q + dk] * (dk ** -0.5) for r in rows]
        k = [p_ref[r, c_k:c_k + dk] for r in rows]
        v16 = [p_ref[r, c_v:c_v + dv].astype(BF16) for r in rows]
        gin = [p_ref[r, c_g:c_g + dk].astype(BF16) for r in rows]
        logit = [_dot(gin[u], w2_ref[0, d]) + gb_ref[0, d][0:1, :] for u, d in lanes]
        la2 = [_split(-_softplus(-x) * (1.0 / B_TAU), GLA_PIECES) for x in logit]
        seg_sum = lambda d, l, y: _fold(_dot(seg_ref[d, l], y), GLA_PIECES)
        bc = [seg_sum(d, 0, y) for (u, d), y in zip(lanes, la2)]
        xs = [[seg_sum(d, 1 + l, y) for (u, d), y in zip(lanes, la2)] for l in range(N_LEVELS)]
        scores = [jnp.zeros((CH, CH), F32) for _ in lanes]
        for l in range(N_LEVELS):
            own = [lvl_ref[d] == l for d in range(2)]
            e = [jnp.exp(x) for x in xs[l]]
            prod = [_dot_nt((q[u] * x).astype(BF16), (k[u] * x).astype(BF16)) for (u, d), x in zip(lanes, e)]
            scores = [jnp.where(own[d], p, s) for (u, d), p, s in zip(lanes, prod, scores)]
        diag = [_dot_nt(q[u].astype(BF16), k[u].astype(BF16)) for u in range(gg)]
        own = [lvl_ref[d] == N_LEVELS for d in range(2)]
        scores = [jnp.where(own[d], diag[u], s) for (u, d), s in zip(lanes, scores)]
        for (u, d), s, b in zip(lanes, scores, bc):
            idx = d * nc + cs[u]
            btot = b[0:1, :] if d == 1 else b[CH - 1:CH, :]
            oi_ref[idx] = _dot(s.astype(BF16), v16[u])
            qs_ref[idx] = (q[u] * jnp.exp(b)).astype(BF16)
            kd_ref[idx] = (k[u] * jnp.exp(btot - b)).astype(BF16)
            dec_ref[idx] = jnp.exp(jnp.broadcast_to(btot, (CH, dk)).T)
        return carry

    lax.fori_loop(0, nc // gg, prep, 0)

    def scan(c, carry):
        sf, sb = carry
        cf = c
        cb = nc - 1 - c
        rf = pl.ds(pl.multiple_of(cf * CH, CH), CH)
        rb = pl.ds(pl.multiple_of(cb * CH, CH), CH)
        ib = nc + cb
        oi_ref[cf] = oi_ref[cf] + _dot(qs_ref[cf], sf.astype(BF16))
        oi_ref[ib] = oi_ref[ib] + _dot(qs_ref[ib], sb.astype(BF16))
        vf = p_ref[rf, c_v:c_v + dv].astype(BF16)
        vb = p_ref[rb, c_v:c_v + dv].astype(BF16)
        ef = dec_ref[cf]
        eb = dec_ref[ib]
        sf = sf * jnp.concatenate([ef, ef], axis=1) + _dot_tn(kd_ref[cf], vf)
        sb = sb * jnp.concatenate([eb, eb], axis=1) + _dot_tn(kd_ref[ib], vb)
        return sf, sb

    zero = jnp.zeros((dk, dv), F32)
    lax.fori_loop(0, nc, scan, (zero, zero))

    ng = ng_ref[...]
    fg = _group(nc, FIN_GROUP)

    def fin(it, carry):
        cs = [it * fg + u for u in range(fg)]
        rows = [pl.ds(pl.multiple_of(c * CH, CH), CH) for c in cs]
        os_ = [oi_ref[c] + oi_ref[nc + c] for c in cs]
        inv = [lax.rsqrt(jnp.mean(o * o, axis=-1, keepdims=True) + RMS_EPS) for o in os_]
        for r, o, s in zip(rows, os_, inv):
            o_ref[0, r, :] = (o * s * ng * _silu(p_ref[r, c_r:c_r + dv])).astype(BF16)
        return carry

    lax.fori_loop(0, nc // fg, fin, 0)


def _gla_mixer(xb, w_in, gate_w2, gate_b, norm_g):
    bn, seq, dm = xb.shape
    h, dk, dv = B_HEADS, B_DK, B_DV
    nc = seq // CH
    kw, vw = h * dk, h * dv
    w = w_in

    def heads(cols, width):
        return cols.reshape(dm, h, width).transpose(1, 0, 2)

    gl = jnp.pad(w[:, 2 * kw + 2 * vw:], ((0, 0), (0, dk - 2 * B_RANK)))
    wh = jnp.concatenate([
        heads(w[:, 0:kw], dk), heads(w[:, kw:2 * kw], dk),
        heads(w[:, 2 * kw:2 * kw + vw], dv), heads(w[:, 2 * kw + vw:2 * kw + 2 * vw], dv),
        jnp.broadcast_to(gl[None], (h, dm, dk))], axis=2).astype(BF16)
    w2 = gate_w2.reshape(2, B_RANK, h, dk).transpose(2, 0, 1, 3)
    w2p = jnp.zeros((h, 2, dk, dk), F32)
    w2p = w2p.at[:, 0, 0:B_RANK].set(w2[:, 0]).at[:, 1, B_RANK:2 * B_RANK].set(w2[:, 1]).astype(BF16)
    gb = gate_b.reshape(2, h, dk).transpose(1, 0, 2).astype(F32)
    gb = jnp.broadcast_to(gb[:, :, None, :], (h, 2, 8, dk))
    ng = norm_g.astype(F32).reshape(1, dv)
    seg, lvl = _gla_tables()
    nw = wh.shape[2]

    kern = functools.partial(_gla_kernel, seq=seq)
    return pl.pallas_call(
        kern,
        out_shape=jax.ShapeDtypeStruct((bn, seq, vw), BF16),
        grid=(bn, h),
        in_specs=[
            pl.BlockSpec((1, seq, dm), lambda b, i: (b, 0, 0)),
            pl.BlockSpec((1, dm, nw), lambda b, i: (i, 0, 0)),
            pl.BlockSpec((1, 2, dk, dk), lambda b, i: (i, 0, 0, 0)),
            pl.BlockSpec((1, 2, 8, dk), lambda b, i: (i, 0, 0, 0)),
            pl.BlockSpec((1, dv), lambda b, i: (0, 0)),
            pl.BlockSpec(seg.shape, lambda b, i: (0, 0, 0, 0)),
            pl.BlockSpec(lvl.shape, lambda b, i: (0, 0, 0)),
        ],
        out_specs=pl.BlockSpec((1, seq, dv), lambda b, i: (b, 0, i)),
        scratch_shapes=[
            pltpu.VMEM((seq, nw), F32),
            pltpu.VMEM((2 * nc, CH, dk), BF16),
            pltpu.VMEM((2 * nc, CH, dk), BF16),
            pltpu.VMEM((2 * nc, dk, dk), F32),
            pltpu.VMEM((2 * nc, CH, dv), F32),
        ],
        compiler_params=pltpu.CompilerParams(
            dimension_semantics=("arbitrary", "arbitrary"), vmem_limit_bytes=VMEM_LIMIT),
        name="gla_mixer",
    )(xb, wh, w2p, gb, ng, seg, lvl)


def _post_kernel(o_ref, x_ref, wo_ref, w1_ref, w2_ref, ln_ref, y_ref, yb_ref, *, alpha):
    ln = ln_ref[...]
    x = x_ref[...]
    x1 = _layernorm(alpha * x + _dot(o_ref[...], wo_ref[...]), ln[0:1, :], ln[1:2, :])
    x1b = x1.astype(BF16)
    acc = jnp.zeros(x.shape, F32)
    dff = w1_ref.shape[1]
    for j in range(dff // FF_TILE):
        cols = slice(j * FF_TILE, (j + 1) * FF_TILE)
        hcur = jnp.maximum(_dot(x1b, w1_ref[:, cols]), 0.0)
        acc = acc + _dot((hcur * hcur).astype(BF16), w2_ref[cols, :])
    y = _layernorm(alpha * x1 + acc, ln[2:3, :], ln[3:4, :])
    y_ref[...] = y
    yb_ref[...] = y.astype(BF16)


def _post(o, x, w_out, w1, w2, g1, b1, g2, b2, alpha):
    t, dm = x.shape
    vw = o.shape[1]
    dff = w1.shape[1]
    tm = min(ROW_TILE, t)
    ln = jnp.pad(jnp.stack([g1, b1, g2, b2]).astype(F32), ((0, 4), (0, 0)))
    const = lambda shape: pl.BlockSpec(shape, lambda i: (0, 0), pipeline_mode=pl.Buffered(1))
    return pl.pallas_call(
        functools.partial(_post_kernel, alpha=alpha),
        out_shape=(jax.ShapeDtypeStruct((t, dm), F32), jax.ShapeDtypeStruct((t, dm), BF16)),
        grid=(t // tm,),
        in_specs=[
            pl.BlockSpec((tm, vw), lambda i: (i, 0)),
            pl.BlockSpec((tm, dm), lambda i: (i, 0)),
            const((vw, dm)), const((dm, dff)), const((dff, dm)), const((8, dm)),
        ],
        out_specs=(pl.BlockSpec((tm, dm), lambda i: (i, 0)), pl.BlockSpec((tm, dm), lambda i: (i, 0))),
        compiler_params=pltpu.CompilerParams(
            dimension_semantics=("arbitrary",), vmem_limit_bytes=VMEM_LIMIT),
        name="post",
    )(o, x, w_out.astype(BF16), w1.astype(BF16), w2.astype(BF16), ln)


def kernel(x, a_w_in, a_conv, a_alog, a_dt_bias, a_norm_g, a_w_out, b_w_in, b_gate_w2, b_gate_b,
           b_norm_g, b_w_out, ln1_g, ln1_b, mlp_w1, mlp_w2, ln2_g, ln2_b):
    bn, seq, dm = x.shape
    depth = ln1_g.shape[0]
    alpha = (2 * depth) ** 0.25
    xf = x.astype(F32).reshape(bn * seq, dm)
    xb = xf.astype(BF16)
    for i in range(depth):
        j = i // 2
        xb3 = xb.reshape(bn, seq, dm)
        if i % 2 == 0:
            o = _gdn_mixer(xb3, a_w_in[j], a_conv[j], a_alog[j], a_dt_bias[j], a_norm_g[j])
            w_out = a_w_out[j]
        else:
            o = _gla_mixer(xb3, b_w_in[j], b_gate_w2[j], b_gate_b[j], b_norm_g[j])
            w_out = b_w_out[j]
        xf, xb = _post(o.reshape(bn * seq, -1), xf, w_out, mlp_w1[i], mlp_w2[i],
                       ln1_g[i], ln1_b[i], ln2_g[i], ln2_b[i], alpha)
    return xf.reshape(bn, seq, dm).astype(x.dtype)
```

```python
import functools
import math

import numpy as np

import jax
import jax.numpy as jnp
from jax import lax
from jax.experimental import pallas as pl
from jax.experimental.pallas import tpu as pltpu

F32 = jnp.float32
BF16 = jnp.bfloat16

A_HEADS, A_DK, A_DV, A_CONV = 8, 128, 128, 5
B_HEADS, B_DK, B_DV, B_RANK, B_TAU = 4, 128, 256, 16, 16.0
LN_EPS, RMS_EPS, L2_EPS = 1e-5, 1e-6, 1e-6

CH = 128
N_LEVELS = 7
HALO = 8
SOLVE_LEVELS_PER_STAGE = 1
PREP_SLICES = 8
GLA_GROUP = 4
GLA_AHEAD = 2
GLA_PIECES = 2
FIN_GROUP = 4
NEG_BIG = -1e30
VMEM_LIMIT = 56 * 1024 * 1024
ROW_TILE = 512
FF_TILE = 1024

assert CH == A_DK == B_DK and 2 ** N_LEVELS == CH


def _dot(a, b):
    return jnp.dot(a, b, preferred_element_type=F32)


def _dot_nt(a, b):
    return lax.dot_general(a, b, (((1,), (1,)), ((), ())), preferred_element_type=F32)


def _dot_tn(a, b):
    return lax.dot_general(a, b, (((0,), (0,)), ((), ())), preferred_element_type=F32)


def _split(x, n):
    pieces = []
    for _ in range(n - 1):
        p = x.astype(BF16)
        pieces.append(p)
        x = x - p.astype(F32)
    pieces.append(x.astype(BF16))
    return jnp.concatenate(pieces, axis=1)


def _fold(y, n):
    w = y.shape[1] // n
    out = y[:, 0:w]
    for i in range(1, n):
        out = out + y[:, i * w:(i + 1) * w]
    return out


def _dot_exact(m01, x):
    return _fold(_dot(m01, _split(x, 3)), 3)


def _sigmoid(x):
    return 0.5 + 0.5 * jnp.tanh(0.5 * x)


def _silu(x):
    h = 0.5 * x
    return h + h * jnp.tanh(h)


def _softplus(x):
    return jnp.maximum(x, 0.0) + jnp.log(1.0 + jnp.exp(-jnp.abs(x)))


def _layernorm(y, g, b):
    mu = jnp.mean(y, axis=-1, keepdims=True)
    yc = y - mu
    var = jnp.mean(yc * yc, axis=-1, keepdims=True)
    return yc * lax.rsqrt(var + LN_EPS) * g + b


def _order_masks(rev):
    row = lax.broadcasted_iota(jnp.int32, (CH, CH), 0)
    col = lax.broadcasted_iota(jnp.int32, (CH, CH), 1)
    if rev:
        return col >= row, col > row
    return col <= row, col < row


def _group(n, want):
    return math.gcd(n, want)


def _gdn_kernel(xb_ref, wh_ref, cw_ref, hp_ref, ng_ref, lm_ref, o_ref,
                p_ref, a_ref, t_ref, qk_ref, rhs_ref, qd_ref, kd_ref, gl_ref,
                mc_ref, qc_ref, rc_ref, oc_ref, *, seq):
    nc = seq // CH
    spare = 2 * nc

    p_ref[0:HALO, :] = jnp.zeros((HALO, p_ref.shape[1]), F32)
    p_ref[HALO + seq:, :] = jnp.zeros((HALO, p_ref.shape[1]), F32)
    p_ref[HALO:HALO + seq, :] = _dot(xb_ref[0], wh_ref[0])
    for ref in (a_ref, t_ref, qk_ref, kd_ref, rhs_ref, qd_ref):
        ref[spare] = jnp.zeros(ref.shape[1:], ref.dtype)

    cw = cw_ref[0]
    hp = hp_ref[0]
    zero16 = jnp.zeros((CH, CH), BF16)

    def pair(f, b):
        return jnp.concatenate([jnp.concatenate([f, zero16], axis=1),
                                jnp.concatenate([zero16, b], axis=1)], axis=0)

    def tiles(it, lag):
        c = it - lag
        ok = jnp.logical_and(c >= 0, c < nc)
        return jnp.where(ok, c, spare), jnp.where(ok, nc + c, spare)

    def prep(c):
        base = pl.multiple_of(c * CH, CH)
        win = p_ref[pl.ds(base, CH + 2 * HALO), 0:3 * A_DK]
        acc = jnp.zeros((CH, 3 * A_DK), F32)
        for i in range(A_CONV):
            off = HALO + i - A_CONV // 2
            acc = acc + win[off:off + CH, :] * cw[i:i + 1, :]
            if i % 2 == 1:
                yield
        s = _silu(acc)
        q = s[:, 0:A_DK]
        k = s[:, A_DK:2 * A_DK]
        v = s[:, 2 * A_DK:3 * A_DK]
        q = q * (lax.rsqrt(jnp.sum(q * q, axis=-1, keepdims=True) + L2_EPS) * (A_DK ** -0.5))
        k = k * lax.rsqrt(jnp.sum(k * k, axis=-1, keepdims=True) + L2_EPS)
        ba = p_ref[pl.ds(base + HALO, CH), 4 * A_DK:8 * A_DK]
        kb16 = k.astype(BF16)
        eye = (lax.broadcasted_iota(jnp.int32, (CH, CH), 0)
               == lax.broadcasted_iota(jnp.int32, (CH, CH), 1)).astype(F32)
        yield
        dirs = ((0, False), (1, True))
        beta = [_sigmoid(ba[:, d * A_DK:(d + 1) * A_DK]) for d, _ in dirs]
        g = [-hp[d:d + 1, :] * _softplus(ba[:, (2 + d) * A_DK:(3 + d) * A_DK] + hp[2 + d:3 + d, :])
             for d, _ in dirs]
        gc = [_dot_exact(_order_masks(rev)[0].astype(BF16), g[d]) for d, rev in dirs]
        yield
        kbeta = [k * beta[d] for d, _ in dirs]
        kq = [_dot_nt(jnp.concatenate([kbeta[d], q], axis=0).astype(BF16), kb16) for d, _ in dirs]
        gtot = [gc[d][0:1, :] if rev else gc[d][CH - 1:CH, :] for d, rev in dirs]
        eg = [jnp.exp(gc[d]) for d, _ in dirs]
        yield
        dmat = [jnp.exp(jnp.where(_order_masks(rev)[0], gc[d] - gc[d].T, NEG_BIG)) for d, rev in dirs]
        yield
        for d, rev in dirs:
            idx = d * nc + c
            a = jnp.where(_order_masks(rev)[1], kq[d][0:CH] * dmat[d], 0.0).astype(BF16)
            a_ref[idx] = a
            t_ref[idx] = (eye - (a * lm_ref[0]).astype(F32)).astype(BF16)
            qk_ref[idx] = (kq[d][CH:] * dmat[d]).astype(BF16)
        yield
        for d, rev in dirs:
            idx = d * nc + c
            rhs_ref[idx] = jnp.concatenate([v * beta[d], kbeta[d] * eg[d]], axis=1).astype(BF16)
            qd_ref[idx] = q * eg[d]
            kd_ref[idx] = (k * jnp.exp(gtot[d] - gc[d])).astype(BF16)
            gl_ref[idx] = jnp.broadcast_to(jnp.exp(gtot[d]), (8, A_DK))
        yield

    per_stage = SOLVE_LEVELS_PER_STAGE
    n_stages = (N_LEVELS - 1) // per_stage
    solve_stages = tuple((1 + s, tuple(range(1 + s * per_stage, 1 + (s + 1) * per_stage)))
                         for s in range(n_stages))
    ops_lag = n_stages + 1
    fills = -(-PREP_SLICES // (2 * per_stage))

    def step(it, with_prep):
        filler = prep(it) if with_prep else iter(())

        def fill(n):
            for _ in range(n):
                next(filler, None)

        loaded = []
        for lag, lvs in solve_stages:
            jf, jb = tiles(it, lag)
            loaded.append((jf, jb, t_ref[jf], t_ref[jb], a_ref[jf], a_ref[jb]))
        ops_in = [(i, t_ref[i], rhs_ref[i], kd_ref[i], qk_ref[i], qd_ref[i]) for i in tiles(it, ops_lag)]
        uws = [_dot(t, rhs) for i, t, rhs, kd, qkm, qd in ops_in]
        for half in range(per_stage):
            xs = []
            for (lag, lvs), (jf, jb, tf, tb, af, ab) in zip(solve_stages, loaded):
                m = lm_ref[lvs[half]]
                xs.append(_dot(jnp.concatenate([tf, tb], axis=1), pair(af * m, ab * m)))
            fill(fills)
            ys = [_dot(x.astype(BF16), pair(tf, tb)) for x, (jf, jb, tf, tb, af, ab) in zip(xs, loaded)]
            if half == 0:
                uws = [uw.astype(BF16) for uw in uws]
                kuws = [_dot_tn(kd, uw) for uw, (i, t, rhs, kd, qkm, qd) in zip(uws, ops_in)]
                quws = [_dot(qkm, uw) for uw, (i, t, rhs, kd, qkm, qd) in zip(uws, ops_in)]
            fill(fills)
            loaded = [(jf, jb, tf - y[:, 0:CH].astype(BF16), tb - y[:, CH:].astype(BF16), af, ab)
                      for y, (jf, jb, tf, tb, af, ab) in zip(ys, loaded)]
        for _ in filler:
            pass
        for jf, jb, tf, tb, _, _ in loaded:
            t_ref[jf] = tf
            t_ref[jb] = tb
        for kuw, quw, (i, t, rhs, kd, qkm, qd) in zip(kuws, quws, ops_in):
            qc_ref[i] = kuw[:, 0:A_DV]
            mc_ref[i] = kuw[:, A_DV:].astype(BF16)
            oc_ref[i] = quw[:, 0:A_DV]
            rc_ref[i] = (qd - quw[:, A_DV:]).astype(BF16)

    def main(it, carry):
        step(it, True)
        return carry

    def drain(it, carry):
        step(it, False)
        return carry

    lax.fori_loop(0, nc, main, 0)
    lax.fori_loop(nc, nc + ops_lag, drain, 0)

    def scan(c, carry):
        sf, sb = carry
        cf = c
        cb = 2 * nc - 1 - c
        of = _dot(rc_ref[cf], sf.astype(BF16)) + oc_ref[cf]
        ob = _dot(rc_ref[cb], sb.astype(BF16)) + oc_ref[cb]
        oc_ref[cf] = of
        oc_ref[cb] = ob
        sf = gl_ref[cf][0:1, :] * sf - _dot(mc_ref[cf], sf.astype(BF16)) + qc_ref[cf]
        sb = gl_ref[cb][0:1, :] * sb - _dot(mc_ref[cb], sb.astype(BF16)) + qc_ref[cb]
        return sf, sb

    zero = jnp.zeros((A_DK, A_DV), F32)
    lax.fori_loop(0, nc, scan, (zero, zero))

    ng = ng_ref[...]
    fg = _group(nc, FIN_GROUP)

    def fin(it, carry):
        cs = [it * fg + u for u in range(fg)]
        rows = [pl.ds(pl.multiple_of(c * CH, CH), CH) for c in cs]
        os_ = [oc_ref[c] + oc_ref[nc + c] for c in cs]
        inv = [lax.rsqrt(jnp.mean(o * o, axis=-1, keepdims=True) + RMS_EPS) for o in os_]
        for c, r, o, s in zip(cs, rows, os_, inv):
            z = p_ref[pl.ds(pl.multiple_of(c * CH, CH) + HALO, CH), 3 * A_DK:4 * A_DK]
            o_ref[0, r, :] = (o * s * ng * _silu(z)).astype(BF16)
        return carry

    lax.fori_loop(0, nc // fg, fin, 0)


def _gdn_level_masks():
    idx = np.arange(CH)
    x = idx[:, None] ^ idx[None, :]
    return jnp.asarray(np.stack([(x >> lv) == 1 for lv in range(N_LEVELS)]), BF16)


def _gdn_mixer(xb, w_in, conv_w, a_log, dt_bias, norm_g):
    bn, seq, dm = xb.shape
    h, dk = A_HEADS, A_DK
    nc = seq // CH
    w = w_in
    hw = h * dk
    ba = w[:, 4 * hw:].reshape(dm, 2, 2, h)
    per_head = [w[:, i * hw:(i + 1) * hw].reshape(dm, h, dk).transpose(1, 0, 2) for i in range(4)]
    rep = [jnp.broadcast_to(ba[:, kind, d, :].T[:, :, None], (h, dm, dk))
           for kind in range(2) for d in range(2)]
    wh = jnp.concatenate(per_head + rep, axis=2).astype(BF16)
    cw = conv_w.reshape(A_CONV, 3, h, dk).transpose(2, 0, 1, 3).reshape(h, A_CONV, 3 * dk)
    cw = jnp.pad(cw, ((0, 0), (0, 8 - A_CONV), (0, 0))).astype(F32)
    hp = jnp.concatenate([jnp.exp(a_log.astype(F32)), dt_bias.astype(F32)], axis=0)
    hp = jnp.broadcast_to(jnp.pad(hp, ((0, 4), (0, 0))).T[:, :, None], (h, 8, dk))
    ng = norm_g.astype(F32).reshape(1, A_DV)
    lm = _gdn_level_masks()
    nw = wh.shape[2]

    kern = functools.partial(_gdn_kernel, seq=seq)
    tile = lambda dt: pltpu.VMEM((2 * nc + 1, CH, CH), dt)
    return pl.pallas_call(
        kern,
        out_shape=jax.ShapeDtypeStruct((bn, seq, h * A_DV), BF16),
        grid=(bn, h),
        in_specs=[
            pl.BlockSpec((1, seq, dm), lambda b, i: (b, 0, 0)),
            pl.BlockSpec((1, dm, nw), lambda b, i: (i, 0, 0)),
            pl.BlockSpec((1, 8, 3 * dk), lambda b, i: (i, 0, 0)),
            pl.BlockSpec((1, 8, dk), lambda b, i: (i, 0, 0)),
            pl.BlockSpec((1, A_DV), lambda b, i: (0, 0)),
            pl.BlockSpec(lm.shape, lambda b, i: (0, 0, 0)),
        ],
        out_specs=pl.BlockSpec((1, seq, A_DV), lambda b, i: (b, 0, i)),
        scratch_shapes=[
            pltpu.VMEM((seq + 2 * HALO, nw), F32),
            tile(BF16),
            tile(BF16),
            tile(BF16),
            pltpu.VMEM((2 * nc + 1, CH, A_DV + A_DK), BF16),
            tile(F32),
            tile(BF16),
            pltpu.VMEM((2 * nc, 8, A_DK), F32),
            tile(BF16),
            tile(F32),
            tile(BF16),
            tile(F32),
        ],
        compiler_params=pltpu.CompilerParams(
            dimension_semantics=("arbitrary", "arbitrary"), vmem_limit_bytes=VMEM_LIMIT),
        name="gdn_mixer",
    )(xb, wh, cw, hp, ng, lm)


def _gla_tables():
    i = np.arange(CH)[:, None]
    t = np.arange(CH)[None, :]
    seg = np.zeros((2, N_LEVELS + 1, CH, CH), np.float32)
    lvl = np.zeros((2, CH, CH), np.int32)
    for d in range(2):
        rev = d == 1
        seg[d, 0] = (t >= i) if rev else (t <= i)
        lv = np.full((CH, CH), N_LEVELS + 1, np.int32)
        lv[np.arange(CH), np.arange(CH)] = N_LEVELS
        x = i ^ t
        for l in range(N_LEVELS):
            h = 2 ** (N_LEVELS - 1 - l)
            b0 = (i // (2 * h)) * (2 * h)
            if rev:
                r = b0 + h
                late = i < r
                m = np.where(late, (t >= i) & (t < r), (t >= r) & (t < i))
                own = ((x >> (N_LEVELS - 1 - l)) == 1) & (t > i)
            else:
                r = b0 + h - 1
                late = i > r
                m = np.where(late, (t > r) & (t <= i), (t > i) & (t <= r))
                own = ((x >> (N_LEVELS - 1 - l)) == 1) & (t < i)
            seg[d, 1 + l] = m
            lv[own] = l
        lvl[d] = lv
    return jnp.asarray(seg, BF16), jnp.asarray(lvl)


def _gla_kernel(xb_ref, wh_ref, w2_ref, gb_ref, ng_ref, seg_ref, lvl_ref, o_ref,
                p_ref, qs_ref, kv_ref, st_ref, dec_ref, oi_ref, *, seq):
    nc = seq // CH
    dk, dv = B_DK, B_DV
    p_ref[...] = _dot(xb_ref[0], wh_ref[0])

    c_q, c_k, c_v, c_r, c_g = 0, dk, 2 * dk, 2 * dk + dv, 2 * dk + 2 * dv
    gg = _group(nc, GLA_GROUP)
    lanes = [(u, d) for u in range(gg) for d in range(2)]

    def prep(it, carry):
        cs = [it * gg + u for u in range(gg)]
        rows = [pl.ds(pl.multiple_of(c * CH, CH), CH) for c in cs]
        q = [p_ref[r, c_q:c_q + dk] * (dk ** -0.5) for r in rows]
        k = [p_ref[r, c_k:c_k + dk] for r in rows]
        q16 = [x.astype(BF16) for x in q]
        k16 = [x.astype(BF16) for x in k]
        v16 = [p_ref[r, c_v:c_v + dv].astype(BF16) for r in rows]
        gin = [p_ref[r, c_g:c_g + dk].astype(BF16) for r in rows]
        logit = [_dot(gin[u], w2_ref[0, d]) + gb_ref[0, d][0:1, :] for u, d in lanes]
        la3 = [_split(-_softplus(-x) * (1.0 / B_TAU), 3) for x in logit]
        la2 = [y[:, 0:GLA_PIECES * dk] for y in la3]
        bc = [_fold(_dot(seg_ref[d, 0], y), 3) for (u, d), y in zip(lanes, la3)]

        def level_sums(l):
            h = CH >> (l + 1)
            if h < HALO:
                return [_fold(_dot(seg_ref[d, 1 + l], y), GLA_PIECES) for (u, d), y in zip(lanes, la2)]
            out = []
            for (u, d), b in zip(lanes, bc):
                blocks = []
                for lo in range(0, CH, 2 * h):
                    r = lo + h if d == 1 else lo + h - 1
                    diff = b[lo:lo + 2 * h, :] - b[r:r + 1, :]
                    late = diff[0:h] if d == 1 else diff[h:]
                    early = -(diff[h:] if d == 1 else diff[0:h])
                    blocks += [late, early] if d == 1 else [early, late]
                out.append(jnp.concatenate(blocks, axis=0))
            return out

        scores = [jnp.zeros((CH, CH), F32) for _ in lanes]
        ahead = [level_sums(l) for l in range(GLA_AHEAD)]
        prod = None
        for l in range(N_LEVELS):
            if l + GLA_AHEAD < N_LEVELS:
                ahead.append(level_sums(l + GLA_AHEAD))
            e = [jnp.exp(x) for x in ahead[l]]
            ql = [(q[u] * x).astype(BF16) for (u, d), x in zip(lanes, e)]
            kl = [(k[u] * x).astype(BF16) for (u, d), x in zip(lanes, e)]
            if prod is not None:
                own = [lvl_ref[d] == l - 1 for d in range(2)]
                scores = [jnp.where(own[d], p, s) for (u, d), p, s in zip(lanes, prod, scores)]
            prod = [_dot_nt(a, b) for a, b in zip(ql, kl)]
        own = [lvl_ref[d] == N_LEVELS - 1 for d in range(2)]
        scores = [jnp.where(own[d], p, s) for (u, d), p, s in zip(lanes, prod, scores)]
        diag = [_dot_nt(q16[u], k16[u]) for u in range(gg)]
        own = [lvl_ref[d] == N_LEVELS for d in range(2)]
        scores = [jnp.where(own[d], diag[u], s) for (u, d), s in zip(lanes, scores)]
        for (u, d), s, b in zip(lanes, scores, bc):
            idx = d * nc + cs[u]
            btot = b[0:1, :] if d == 1 else b[CH - 1:CH, :]
            oi_ref[idx] = _dot(s.astype(BF16), v16[u])
            qs_ref[idx] = (q[u] * jnp.exp(b)).astype(BF16)
            kv_ref[idx] = _dot_tn((k[u] * jnp.exp(btot - b)).astype(BF16), v16[u])
            dec_ref[idx] = jnp.exp(jnp.broadcast_to(btot, (CH, dk)).T)
        return carry

    lax.fori_loop(0, nc // gg, prep, 0)

    def scan(c, carry):
        sf, sb = carry
        cf = c
        ib = 2 * nc - 1 - c
        st_ref[cf] = sf.astype(BF16)
        st_ref[ib] = sb.astype(BF16)
        ef = dec_ref[cf]
        eb = dec_ref[ib]
        sf = sf * jnp.concatenate([ef, ef], axis=1) + kv_ref[cf]
        sb = sb * jnp.concatenate([eb, eb], axis=1) + kv_ref[ib]
        return sf, sb

    zero = jnp.zeros((dk, dv), F32)
    lax.fori_loop(0, nc, scan, (zero, zero))

    ng = ng_ref[...]
    fg = _group(nc, FIN_GROUP)

    def fin(it, carry):
        cs = [it * fg + u for u in range(fg)]
        rows = [pl.ds(pl.multiple_of(c * CH, CH), CH) for c in cs]
        os_ = [oi_ref[c] + oi_ref[nc + c] + _dot(qs_ref[c], st_ref[c]) + _dot(qs_ref[nc + c], st_ref[nc + c])
               for c in cs]
        inv = [lax.rsqrt(jnp.mean(o * o, axis=-1, keepdims=True) + RMS_EPS) for o in os_]
        for r, o, s in zip(rows, os_, inv):
            o_ref[0, r, :] = (o * s * ng * _silu(p_ref[r, c_r:c_r + dv])).astype(BF16)
        return carry

    lax.fori_loop(0, nc // fg, fin, 0)


def _gla_mixer(xb, w_in, gate_w2, gate_b, norm_g):
    bn, seq, dm = xb.shape
    h, dk, dv = B_HEADS, B_DK, B_DV
    nc = seq // CH
    kw, vw = h * dk, h * dv
    w = w_in

    def heads(cols, width):
        return cols.reshape(dm, h, width).transpose(1, 0, 2)

    gl = jnp.pad(w[:, 2 * kw + 2 * vw:], ((0, 0), (0, dk - 2 * B_RANK)))
    wh = jnp.concatenate([
        heads(w[:, 0:kw], dk), heads(w[:, kw:2 * kw], dk),
        heads(w[:, 2 * kw:2 * kw + vw], dv), heads(w[:, 2 * kw + vw:2 * kw + 2 * vw], dv),
        jnp.broadcast_to(gl[None], (h, dm, dk))], axis=2).astype(BF16)
    w2 = gate_w2.reshape(2, B_RANK, h, dk).transpose(2, 0, 1, 3)
    w2p = jnp.zeros((h, 2, dk, dk), F32)
    w2p = w2p.at[:, 0, 0:B_RANK].set(w2[:, 0]).at[:, 1, B_RANK:2 * B_RANK].set(w2[:, 1]).astype(BF16)
    gb = gate_b.reshape(2, h, dk).transpose(1, 0, 2).astype(F32)
    gb = jnp.broadcast_to(gb[:, :, None, :], (h, 2, 8, dk))
    ng = norm_g.astype(F32).reshape(1, dv)
    seg, lvl = _gla_tables()
    nw = wh.shape[2]

    kern = functools.partial(_gla_kernel, seq=seq)
    return pl.pallas_call(
        kern,
        out_shape=jax.ShapeDtypeStruct((bn, seq, vw), BF16),
        grid=(bn, h),
        in_specs=[
            pl.BlockSpec((1, seq, dm), lambda b, i: (b, 0, 0)),
            pl.BlockSpec((1, dm, nw), lambda b, i: (i, 0, 0)),
            pl.BlockSpec((1, 2, dk, dk), lambda b, i: (i, 0, 0, 0)),
            pl.BlockSpec((1, 2, 8, dk), lambda b, i: (i, 0, 0, 0)),
            pl.BlockSpec((1, dv), lambda b, i: (0, 0)),
            pl.BlockSpec(seg.shape, lambda b, i: (0, 0, 0, 0)),
            pl.BlockSpec(lvl.shape, lambda b, i: (0, 0, 0)),
        ],
        out_specs=pl.BlockSpec((1, seq, dv), lambda b, i: (b, 0, i)),
        scratch_shapes=[
            pltpu.VMEM((seq, nw), F32),
            pltpu.VMEM((2 * nc, CH, dk), BF16),
            pltpu.VMEM((2 * nc, dk, dv), F32),
            pltpu.VMEM((2 * nc, dk, dv), BF16),
            pltpu.VMEM((2 * nc, dk, dk), F32),
            pltpu.VMEM((2 * nc, CH, dv), F32),
        ],
        compiler_params=pltpu.CompilerParams(
            dimension_semantics=("arbitrary", "arbitrary"), vmem_limit_bytes=VMEM_LIMIT),
        name="gla_mixer",
    )(xb, wh, w2p, gb, ng, seg, lvl)


def _post_kernel(o_ref, x_ref, wo_ref, w1_ref, w2_ref, ln_ref, y_ref, yb_ref, *, alpha):
    ln = ln_ref[...]
    x = x_ref[...]
    x1 = _layernorm(alpha * x + _dot(o_ref[...], wo_ref[...]), ln[0:1, :], ln[1:2, :])
    x1b = x1.astype(BF16)
    acc = jnp.zeros(x.shape, F32)
    dff = w1_ref.shape[1]
    for j in range(dff // FF_TILE):
        cols = slice(j * FF_TILE, (j + 1) * FF_TILE)
        hcur = jnp.maximum(_dot(x1b, w1_ref[:, cols]), 0.0)
        acc = acc + _dot((hcur * hcur).astype(BF16), w2_ref[cols, :])
    y = _layernorm(alpha * x1 + acc, ln[2:3, :], ln[3:4, :])
    y_ref[...] = y
    yb_ref[...] = y.astype(BF16)


def _post(o, x, w_out, w1, w2, g1, b1, g2, b2, alpha):
    t, dm = x.shape
    vw = o.shape[1]
    dff = w1.shape[1]
    tm = min(ROW_TILE, t)
    ln = jnp.pad(jnp.stack([g1, b1, g2, b2]).astype(F32), ((0, 4), (0, 0)))
    const = lambda shape: pl.BlockSpec(shape, lambda i: (0, 0), pipeline_mode=pl.Buffered(1))
    return pl.pallas_call(
        functools.partial(_post_kernel, alpha=alpha),
        out_shape=(jax.ShapeDtypeStruct((t, dm), F32), jax.ShapeDtypeStruct((t, dm), BF16)),
        grid=(t // tm,),
        in_specs=[
            pl.BlockSpec((tm, vw), lambda i: (i, 0)),
            pl.BlockSpec((tm, dm), lambda i: (i, 0)),
            const((vw, dm)), const((dm, dff)), const((dff, dm)), const((8, dm)),
        ],
        out_specs=(pl.BlockSpec((tm, dm), lambda i: (i, 0)), pl.BlockSpec((tm, dm), lambda i: (i, 0))),
        compiler_params=pltpu.CompilerParams(
            dimension_semantics=("arbitrary",), vmem_limit_bytes=VMEM_LIMIT),
        name="post",
    )(o, x, w_out.astype(BF16), w1.astype(BF16), w2.astype(BF16), ln)


def kernel(x, a_w_in, a_conv, a_alog, a_dt_bias, a_norm_g, a_w_out, b_w_in, b_gate_w2, b_gate_b,
           b_norm_g, b_w_out, ln1_g, ln1_b, mlp_w1, mlp_w2, ln2_g, ln2_b):
    bn, seq, dm = x.shape
    depth = ln1_g.shape[0]
    alpha = (2 * depth) ** 0.25
    xf = x.astype(F32).reshape(bn * seq, dm)
    xb = xf.astype(BF16)
    for i in range(depth):
        j = i // 2
        xb3 = xb.reshape(bn, seq, dm)
        if i % 2 == 0:
            o = _gdn_mixer(xb3, a_w_in[j], a_conv[j], a_alog[j], a_dt_bias[j], a_norm_g[j])
            w_out = a_w_out[j]
        else:
            o = _gla_mixer(xb3, b_w_in[j], b_gate_w2[j], b_gate_b[j], b_norm_g[j])
            w_out = b_w_out[j]
        xf, xb = _post(o.reshape(bn * seq, -1), xf, w_out, mlp_w1[i], mlp_w2[i],
                       ln1_g[i], ln1_b[i], ln2_g[i], ln2_b[i], alpha)
    return xf.reshape(bn, seq, dm).astype(x.dtype)
```

```python
import functools
import math

import numpy as np

import jax
import jax.numpy as jnp
from jax import lax
from jax.experimental import pallas as pl
from jax.experimental.pallas import tpu as pltpu

F32 = jnp.float32
BF16 = jnp.bfloat16

A_HEADS, A_DK, A_DV, A_CONV = 8, 128, 128, 5
B_HEADS, B_DK, B_DV, B_RANK, B_TAU = 4, 128, 256, 16, 16.0
LN_EPS, RMS_EPS, L2_EPS = 1e-5, 1e-6, 1e-6

CH = 128
N_LEVELS = 7
HALO = 8
SOLVE_LEVELS_PER_STAGE = 2
PREP_SLICES = 8
GLA_GROUP = 4
GLA_AHEAD = 2
GLA_PIECES = 2
FIN_GROUP = 4
NEG_BIG = -1e30
VMEM_LIMIT = 56 * 1024 * 1024
ROW_TILE = 512
FF_TILE = 1024

assert CH == A_DK == B_DK and 2 ** N_LEVELS == CH


def _dot(a, b):
    return jnp.dot(a, b, preferred_element_type=F32)


def _dot_nt(a, b):
    return lax.dot_general(a, b, (((1,), (1,)), ((), ())), preferred_element_type=F32)


def _dot_tn(a, b):
    return lax.dot_general(a, b, (((0,), (0,)), ((), ())), preferred_element_type=F32)


def _split(x, n):
    pieces = []
    for _ in range(n - 1):
        p = x.astype(BF16)
        pieces.append(p)
        x = x - p.astype(F32)
    pieces.append(x.astype(BF16))
    return jnp.concatenate(pieces, axis=1)


def _fold(y, n):
    w = y.shape[1] // n
    out = y[:, 0:w]
    for i in range(1, n):
        out = out + y[:, i * w:(i + 1) * w]
    return out


def _dot_exact(m01, x):
    return _fold(_dot(m01, _split(x, 3)), 3)


def _sigmoid(x):
    return 0.5 + 0.5 * jnp.tanh(0.5 * x)


def _silu(x):
    h = 0.5 * x
    return h + h * jnp.tanh(h)


def _softplus(x):
    return jnp.maximum(x, 0.0) + jnp.log(1.0 + jnp.exp(-jnp.abs(x)))


def _layernorm(y, g, b):
    mu = jnp.mean(y, axis=-1, keepdims=True)
    yc = y - mu
    var = jnp.mean(yc * yc, axis=-1, keepdims=True)
    return yc * lax.rsqrt(var + LN_EPS) * g + b


def _order_masks(rev):
    row = lax.broadcasted_iota(jnp.int32, (CH, CH), 0)
    col = lax.broadcasted_iota(jnp.int32, (CH, CH), 1)
    if rev:
        return col >= row, col > row
    return col <= row, col < row


def _group(n, want):
    return math.gcd(n, want)


def _gdn_kernel(xb_ref, wh_ref, cw_ref, hp_ref, ng_ref, lm_ref, o_ref,
                p_ref, a_ref, t_ref, qk_ref, rhs_ref, qd_ref, kd_ref, gl_ref,
                mc_ref, qc_ref, rc_ref, oc_ref, *, seq):
    nc = seq // CH
    spare = 2 * nc

    p_ref[0:HALO, :] = jnp.zeros((HALO, p_ref.shape[1]), F32)
    p_ref[HALO + seq:, :] = jnp.zeros((HALO, p_ref.shape[1]), F32)
    p_ref[HALO:HALO + seq, :] = _dot(xb_ref[0], wh_ref[0])
    for ref in (a_ref, t_ref, qk_ref, kd_ref, rhs_ref, qd_ref):
        ref[spare] = jnp.zeros(ref.shape[1:], ref.dtype)

    cw = cw_ref[0]
    hp = hp_ref[0]
    zero16 = jnp.zeros((CH, CH), BF16)

    def pair(f, b):
        return jnp.concatenate([jnp.concatenate([f, zero16], axis=1),
                                jnp.concatenate([zero16, b], axis=1)], axis=0)

    def tiles(it, lag):
        c = it - lag
        ok = jnp.logical_and(c >= 0, c < nc)
        return jnp.where(ok, c, spare), jnp.where(ok, nc + c, spare)

    def prep(c):
        base = pl.multiple_of(c * CH, CH)
        win = p_ref[pl.ds(base, CH + 2 * HALO), 0:3 * A_DK]
        acc = jnp.zeros((CH, 3 * A_DK), F32)
        for i in range(A_CONV):
            off = HALO + i - A_CONV // 2
            acc = acc + win[off:off + CH, :] * cw[i:i + 1, :]
            if i % 2 == 1:
                yield
        s = _silu(acc)
        q = s[:, 0:A_DK]
        k = s[:, A_DK:2 * A_DK]
        v = s[:, 2 * A_DK:3 * A_DK]
        q = q * (lax.rsqrt(jnp.sum(q * q, axis=-1, keepdims=True) + L2_EPS) * (A_DK ** -0.5))
        k = k * lax.rsqrt(jnp.sum(k * k, axis=-1, keepdims=True) + L2_EPS)
        ba = p_ref[pl.ds(base + HALO, CH), 4 * A_DK:8 * A_DK]
        kb16 = k.astype(BF16)
        eye = (lax.broadcasted_iota(jnp.int32, (CH, CH), 0)
               == lax.broadcasted_iota(jnp.int32, (CH, CH), 1)).astype(F32)
        yield
        dirs = ((0, False), (1, True))
        beta = [_sigmoid(ba[:, d * A_DK:(d + 1) * A_DK]) for d, _ in dirs]
        g = [-hp[d:d + 1, :] * _softplus(ba[:, (2 + d) * A_DK:(3 + d) * A_DK] + hp[2 + d:3 + d, :])
             for d, _ in dirs]
        gc = [_dot_exact(_order_masks(rev)[0].astype(BF16), g[d]) for d, rev in dirs]
        yield
        kbeta = [k * beta[d] for d, _ in dirs]
        kq = [_dot_nt(jnp.concatenate([kbeta[d], q], axis=0).astype(BF16), kb16) for d, _ in dirs]
        gtot = [gc[d][0:1, :] if rev else gc[d][CH - 1:CH, :] for d, rev in dirs]
        eg = [jnp.exp(gc[d]) for d, _ in dirs]
        yield
        dmat = [jnp.exp(jnp.where(_order_masks(rev)[0], gc[d] - gc[d].T, NEG_BIG)) for d, rev in dirs]
        yield
        for d, rev in dirs:
            idx = d * nc + c
            a = jnp.where(_order_masks(rev)[1], kq[d][0:CH] * dmat[d], 0.0).astype(BF16)
            a_ref[idx] = a
            t_ref[idx] = (eye - (a * lm_ref[0]).astype(F32)).astype(BF16)
            qk_ref[idx] = (kq[d][CH:] * dmat[d]).astype(BF16)
        yield
        for d, rev in dirs:
            idx = d * nc + c
            rhs_ref[idx] = jnp.concatenate([v * beta[d], kbeta[d] * eg[d]], axis=1).astype(BF16)
            qd_ref[idx] = q * eg[d]
            kd_ref[idx] = (k * jnp.exp(gtot[d] - gc[d])).astype(BF16)
            gl_ref[idx] = jnp.broadcast_to(jnp.exp(gtot[d]), (8, A_DK))
        yield

    per_stage = SOLVE_LEVELS_PER_STAGE
    n_stages = (N_LEVELS - 1) // per_stage
    solve_stages = tuple((1 + s, tuple(range(1 + s * per_stage, 1 + (s + 1) * per_stage)))
                         for s in range(n_stages))
    ops_lag = n_stages + 1
    fills = -(-PREP_SLICES // (2 * per_stage))

    def step(it, with_prep):
        filler = prep(it) if with_prep else iter(())

        def fill(n):
            for _ in range(n):
                next(filler, None)

        loaded = []
        for lag, lvs in solve_stages:
            jf, jb = tiles(it, lag)
            loaded.append((jf, jb, t_ref[jf], t_ref[jb], a_ref[jf], a_ref[jb]))
        ops_in = [(i, t_ref[i], rhs_ref[i], kd_ref[i], qk_ref[i], qd_ref[i]) for i in tiles(it, ops_lag)]
        uws = [_dot(t, rhs) for i, t, rhs, kd, qkm, qd in ops_in]
        for half in range(per_stage):
            xs = []
            for (lag, lvs), (jf, jb, tf, tb, af, ab) in zip(solve_stages, loaded):
                m = lm_ref[lvs[half]]
                xs.append(_dot(jnp.concatenate([tf, tb], axis=1), pair(af * m, ab * m)))
            fill(fills)
            ys = [_dot(x.astype(BF16), pair(tf, tb)) for x, (jf, jb, tf, tb, af, ab) in zip(xs, loaded)]
            if half == 0:
                uws = [uw.astype(BF16) for uw in uws]
                kuws = [_dot_tn(kd, uw) for uw, (i, t, rhs, kd, qkm, qd) in zip(uws, ops_in)]
                quws = [_dot(qkm, uw) for uw, (i, t, rhs, kd, qkm, qd) in zip(uws, ops_in)]
            fill(fills)
            loaded = [(jf, jb, tf - y[:, 0:CH].astype(BF16), tb - y[:, CH:].astype(BF16), af, ab)
                      for y, (jf, jb, tf, tb, af, ab) in zip(ys, loaded)]
        for _ in filler:
            pass
        for jf, jb, tf, tb, _, _ in loaded:
            t_ref[jf] = tf
            t_ref[jb] = tb
        for kuw, quw, (i, t, rhs, kd, qkm, qd) in zip(kuws, quws, ops_in):
            qc_ref[i] = kuw[:, 0:A_DV]
            mc_ref[i] = kuw[:, A_DV:].astype(BF16)
            oc_ref[i] = quw[:, 0:A_DV]
            rc_ref[i] = (qd - quw[:, A_DV:]).astype(BF16)

    def main(it, carry):
        step(it, True)
        return carry

    def drain(it, carry):
        step(it, False)
        return carry

    lax.fori_loop(0, nc, main, 0)
    lax.fori_loop(nc, nc + ops_lag, drain, 0)

    def scan(c, carry):
        sf, sb = carry
        cf = c
        cb = 2 * nc - 1 - c
        of = _dot(rc_ref[cf], sf.astype(BF16)) + oc_ref[cf]
        ob = _dot(rc_ref[cb], sb.astype(BF16)) + oc_ref[cb]
        oc_ref[cf] = of
        oc_ref[cb] = ob
        sf = gl_ref[cf][0:1, :] * sf - _dot(mc_ref[cf], sf.astype(BF16)) + qc_ref[cf]
        sb = gl_ref[cb][0:1, :] * sb - _dot(mc_ref[cb], sb.astype(BF16)) + qc_ref[cb]
        return sf, sb

    zero = jnp.zeros((A_DK, A_DV), F32)
    lax.fori_loop(0, nc, scan, (zero, zero))

    ng = ng_ref[...]
    fg = _group(nc, FIN_GROUP)

    def fin(it, carry):
        cs = [it * fg + u for u in range(fg)]
        rows = [pl.ds(pl.multiple_of(c * CH, CH), CH) for c in cs]
        os_ = [oc_ref[c] + oc_ref[nc + c] for c in cs]
        inv = [lax.rsqrt(jnp.mean(o * o, axis=-1, keepdims=True) + RMS_EPS) for o in os_]
        for c, r, o, s in zip(cs, rows, os_, inv):
            z = p_ref[pl.ds(pl.multiple_of(c * CH, CH) + HALO, CH), 3 * A_DK:4 * A_DK]
            o_ref[0, r, :] = (o * s * ng * _silu(z)).astype(BF16)
        return carry

    lax.fori_loop(0, nc // fg, fin, 0)


def _gdn_level_masks():
    idx = np.arange(CH)
    x = idx[:, None] ^ idx[None, :]
    return jnp.asarray(np.stack([(x >> lv) == 1 for lv in range(N_LEVELS)]), BF16)


def _gdn_mixer(xb, w_in, conv_w, a_log, dt_bias, norm_g):
    bn, seq, dm = xb.shape
    h, dk = A_HEADS, A_DK
    nc = seq // CH
    w = w_in
    hw = h * dk
    ba = w[:, 4 * hw:].reshape(dm, 2, 2, h)
    per_head = [w[:, i * hw:(i + 1) * hw].reshape(dm, h, dk).transpose(1, 0, 2) for i in range(4)]
    rep = [jnp.broadcast_to(ba[:, kind, d, :].T[:, :, None], (h, dm, dk))
           for kind in range(2) for d in range(2)]
    wh = jnp.concatenate(per_head + rep, axis=2).astype(BF16)
    cw = conv_w.reshape(A_CONV, 3, h, dk).transpose(2, 0, 1, 3).reshape(h, A_CONV, 3 * dk)
    cw = jnp.pad(cw, ((0, 0), (0, 8 - A_CONV), (0, 0))).astype(F32)
    hp = jnp.concatenate([jnp.exp(a_log.astype(F32)), dt_bias.astype(F32)], axis=0)
    hp = jnp.broadcast_to(jnp.pad(hp, ((0, 4), (0, 0))).T[:, :, None], (h, 8, dk))
    ng = norm_g.astype(F32).reshape(1, A_DV)
    lm = _gdn_level_masks()
    nw = wh.shape[2]

    kern = functools.partial(_gdn_kernel, seq=seq)
    tile = lambda dt: pltpu.VMEM((2 * nc + 1, CH, CH), dt)
    return pl.pallas_call(
        kern,
        out_shape=jax.ShapeDtypeStruct((bn, seq, h * A_DV), BF16),
        grid=(bn, h),
        in_specs=[
            pl.BlockSpec((1, seq, dm), lambda b, i: (b, 0, 0)),
            pl.BlockSpec((1, dm, nw), lambda b, i: (i, 0, 0)),
            pl.BlockSpec((1, 8, 3 * dk), lambda b, i: (i, 0, 0)),
            pl.BlockSpec((1, 8, dk), lambda b, i: (i, 0, 0)),
            pl.BlockSpec((1, A_DV), lambda b, i: (0, 0)),
            pl.BlockSpec(lm.shape, lambda b, i: (0, 0, 0)),
        ],
        out_specs=pl.BlockSpec((1, seq, A_DV), lambda b, i: (b, 0, i)),
        scratch_shapes=[
            pltpu.VMEM((seq + 2 * HALO, nw), F32),
            tile(BF16),
            tile(BF16),
            tile(BF16),
            pltpu.VMEM((2 * nc + 1, CH, A_DV + A_DK), BF16),
            tile(F32),
            tile(BF16),
            pltpu.VMEM((2 * nc, 8, A_DK), F32),
            tile(BF16),
            tile(F32),
            tile(BF16),
            tile(F32),
        ],
        compiler_params=pltpu.CompilerParams(
            dimension_semantics=("arbitrary", "arbitrary"), vmem_limit_bytes=VMEM_LIMIT),
        name="gdn_mixer",
    )(xb, wh, cw, hp, ng, lm)


def _gla_tables():
    i = np.arange(CH)[:, None]
    t = np.arange(CH)[None, :]
    seg = np.zeros((2, N_LEVELS + 1, CH, CH), np.float32)
    lvl = np.zeros((2, CH, CH), np.int32)
    for d in range(2):
        rev = d == 1
        seg[d, 0] = (t >= i) if rev else (t <= i)
        lv = np.full((CH, CH), N_LEVELS + 1, np.int32)
        lv[np.arange(CH), np.arange(CH)] = N_LEVELS
        x = i ^ t
        for l in range(N_LEVELS):
            h = 2 ** (N_LEVELS - 1 - l)
            b0 = (i // (2 * h)) * (2 * h)
            if rev:
                r = b0 + h
                late = i < r
                m = np.where(late, (t >= i) & (t < r), (t >= r) & (t < i))
                own = ((x >> (N_LEVELS - 1 - l)) == 1) & (t > i)
            else:
                r = b0 + h - 1
                late = i > r
                m = np.where(late, (t > r) & (t <= i), (t > i) & (t <= r))
                own = ((x >> (N_LEVELS - 1 - l)) == 1) & (t < i)
            seg[d, 1 + l] = m
            lv[own] = l
        lvl[d] = lv
    return jnp.asarray(seg, BF16), jnp.asarray(lvl)


def _gla_kernel(xb_ref, wh_ref, w2_ref, gb_ref, ng_ref, seg_ref, lvl_ref, o_ref,
                p_ref, qs_ref, kv_ref, st_ref, dec_ref, oi_ref, *, seq):
    nc = seq // CH
    dk, dv = B_DK, B_DV
    p_ref[...] = _dot(xb_ref[0], wh_ref[0])

    c_q, c_k, c_v, c_r, c_g = 0, dk, 2 * dk, 2 * dk + dv, 2 * dk + 2 * dv
    gg = _group(nc, GLA_GROUP)
    lanes = [(u, d) for u in range(gg) for d in range(2)]

    def prep(it, carry):
        cs = [it * gg + u for u in range(gg)]
        rows = [pl.ds(pl.multiple_of(c * CH, CH), CH) for c in cs]
        q = [p_ref[r, c_q:c_q + dk] * (dk ** -0.5) for r in rows]
        k = [p_ref[r, c_k:c_k + dk] for r in rows]
        q16 = [x.astype(BF16) for x in q]
        k16 = [x.astype(BF16) for x in k]
        v16 = [p_ref[r, c_v:c_v + dv].astype(BF16) for r in rows]
        gin = [p_ref[r, c_g:c_g + dk].astype(BF16) for r in rows]
        logit = [_dot(gin[u], w2_ref[0, d]) + gb_ref[0, d][0:1, :] for u, d in lanes]
        la3 = [_split(-_softplus(-x) * (1.0 / B_TAU), 3) for x in logit]
        la2 = [y[:, 0:GLA_PIECES * dk] for y in la3]
        bc = [_fold(_dot(seg_ref[d, 0], y), 3) for (u, d), y in zip(lanes, la3)]

        def level_sums(l):
            h = CH >> (l + 1)
            if h < HALO:
                return [_fold(_dot(seg_ref[d, 1 + l], y), GLA_PIECES) for (u, d), y in zip(lanes, la2)]
            out = []
            for (u, d), b in zip(lanes, bc):
                blocks = []
                for lo in range(0, CH, 2 * h):
                    r = lo + h if d == 1 else lo + h - 1
                    diff = b[lo:lo + 2 * h, :] - b[r:r + 1, :]
                    late = diff[0:h] if d == 1 else diff[h:]
                    early = -(diff[h:] if d == 1 else diff[0:h])
                    blocks += [late, early] if d == 1 else [early, late]
                out.append(jnp.concatenate(blocks, axis=0))
            return out

        scores = [jnp.zeros((CH, CH), F32) for _ in lanes]
        ahead = [level_sums(l) for l in range(GLA_AHEAD)]
        prod = None
        for l in range(N_LEVELS):
            if l + GLA_AHEAD < N_LEVELS:
                ahead.append(level_sums(l + GLA_AHEAD))
            e = [jnp.exp(x) for x in ahead[l]]
            ql = [(q[u] * x).astype(BF16) for (u, d), x in zip(lanes, e)]
            kl = [(k[u] * x).astype(BF16) for (u, d), x in zip(lanes, e)]
            if prod is not None:
                own = [lvl_ref[d] == l - 1 for d in range(2)]
                scores = [jnp.where(own[d], p, s) for (u, d), p, s in zip(lanes, prod, scores)]
            prod = [_dot_nt(a, b) for a, b in zip(ql, kl)]
        own = [lvl_ref[d] == N_LEVELS - 1 for d in range(2)]
        scores = [jnp.where(own[d], p, s) for (u, d), p, s in zip(lanes, prod, scores)]
        diag = [_dot_nt(q16[u], k16[u]) for u in range(gg)]
        own = [lvl_ref[d] == N_LEVELS for d in range(2)]
        scores = [jnp.where(own[d], diag[u], s) for (u, d), s in zip(lanes, scores)]
        for (u, d), s, b in zip(lanes, scores, bc):
            idx = d * nc + cs[u]
            btot = b[0:1, :] if d == 1 else b[CH - 1:CH, :]
            oi_ref[idx] = _dot(s.astype(BF16), v16[u])
            qs_ref[idx] = (q[u] * jnp.exp(b)).astype(BF16)
            kv_ref[idx] = _dot_tn((k[u] * jnp.exp(btot - b)).astype(BF16), v16[u])
            dec_ref[idx] = jnp.exp(jnp.broadcast_to(btot, (CH, dk)).T)
        return carry

    lax.fori_loop(0, nc // gg, prep, 0)

    def scan(c, carry):
        sf, sb = carry
        cf = c
        ib = 2 * nc - 1 - c
        st_ref[cf] = sf.astype(BF16)
        st_ref[ib] = sb.astype(BF16)
        ef = dec_ref[cf]
        eb = dec_ref[ib]
        sf = sf * jnp.concatenate([ef, ef], axis=1) + kv_ref[cf]
        sb = sb * jnp.concatenate([eb, eb], axis=1) + kv_ref[ib]
        return sf, sb

    zero = jnp.zeros((dk, dv), F32)
    lax.fori_loop(0, nc, scan, (zero, zero))

    ng = ng_ref[...]
    fg = _group(nc, FIN_GROUP)

    def fin(it, carry):
        cs = [it * fg + u for u in range(fg)]
        rows = [pl.ds(pl.multiple_of(c * CH, CH), CH) for c in cs]
        os_ = [oi_ref[c] + oi_ref[nc + c] + _dot(qs_ref[c], st_ref[c]) + _dot(qs_ref[nc + c], st_ref[nc + c])
               for c in cs]
        inv = [lax.rsqrt(jnp.mean(o * o, axis=-1, keepdims=True) + RMS_EPS) for o in os_]
        for r, o, s in zip(rows, os_, inv):
            o_ref[0, r, :] = (o * s * ng * _silu(p_ref[r, c_r:c_r + dv])).astype(BF16)
        return carry

    lax.fori_loop(0, nc // fg, fin, 0)


def _gla_mixer(xb, w_in, gate_w2, gate_b, norm_g):
    bn, seq, dm = xb.shape
    h, dk, dv = B_HEADS, B_DK, B_DV
    nc = seq // CH
    kw, vw = h * dk, h * dv
    w = w_in

    def heads(cols, width):
        return cols.reshape(dm, h, width).transpose(1, 0, 2)

    gl = jnp.pad(w[:, 2 * kw + 2 * vw:], ((0, 0), (0, dk - 2 * B_RANK)))
    wh = jnp.concatenate([
        heads(w[:, 0:kw], dk), heads(w[:, kw:2 * kw], dk),
        heads(w[:, 2 * kw:2 * kw + vw], dv), heads(w[:, 2 * kw + vw:2 * kw + 2 * vw], dv),
        jnp.broadcast_to(gl[None], (h, dm, dk))], axis=2).astype(BF16)
    w2 = gate_w2.reshape(2, B_RANK, h, dk).transpose(2, 0, 1, 3)
    w2p = jnp.zeros((h, 2, dk, dk), F32)
    w2p = w2p.at[:, 0, 0:B_RANK].set(w2[:, 0]).at[:, 1, B_RANK:2 * B_RANK].set(w2[:, 1]).astype(BF16)
    gb = gate_b.reshape(2, h, dk).transpose(1, 0, 2).astype(F32)
    gb = jnp.broadcast_to(gb[:, :, None, :], (h, 2, 8, dk))
    ng = norm_g.astype(F32).reshape(1, dv)
    seg, lvl = _gla_tables()
    nw = wh.shape[2]

    kern = functools.partial(_gla_kernel, seq=seq)
    return pl.pallas_call(
        kern,
        out_shape=jax.ShapeDtypeStruct((bn, seq, vw), BF16),
        grid=(bn, h),
        in_specs=[
            pl.BlockSpec((1, seq, dm), lambda b, i: (b, 0, 0)),
            pl.BlockSpec((1, dm, nw), lambda b, i: (i, 0, 0)),
            pl.BlockSpec((1, 2, dk, dk), lambda b, i: (i, 0, 0, 0)),
            pl.BlockSpec((1, 2, 8, dk), lambda b, i: (i, 0, 0, 0)),
            pl.BlockSpec((1, dv), lambda b, i: (0, 0)),
            pl.BlockSpec(seg.shape, lambda b, i: (0, 0, 0, 0)),
            pl.BlockSpec(lvl.shape, lambda b, i: (0, 0, 0)),
        ],
        out_specs=pl.BlockSpec((1, seq, dv), lambda b, i: (b, 0, i)),
        scratch_shapes=[
            pltpu.VMEM((seq, nw), F32),
            pltpu.VMEM((2 * nc, CH, dk), BF16),
            pltpu.VMEM((2 * nc, dk, dv), F32),
            pltpu.VMEM((2 * nc, dk, dv), BF16),
            pltpu.VMEM((2 * nc, dk, dk), F32),
            pltpu.VMEM((2 * nc, CH, dv), F32),
        ],
        compiler_params=pltpu.CompilerParams(
            dimension_semantics=("arbitrary", "arbitrary"), vmem_limit_bytes=VMEM_LIMIT),
        name="gla_mixer",
    )(xb, wh, w2p, gb, ng, seg, lvl)


def _post_kernel(o_ref, x_ref, wo_ref, w1_ref, w2_ref, ln_ref, y_ref, yb_ref, *, alpha):
    ln = ln_ref[...]
    x = x_ref[...]
    x1 = _layernorm(alpha * x + _dot(o_ref[...], wo_ref[...]), ln[0:1, :], ln[1:2, :])
    x1b = x1.astype(BF16)
    acc = jnp.zeros(x.shape, F32)
    dff = w1_ref.shape[1]
    for j in range(dff // FF_TILE):
        cols = slice(j * FF_TILE, (j + 1) * FF_TILE)
        hcur = jnp.maximum(_dot(x1b, w1_ref[:, cols]), 0.0)
        acc = acc + _dot((hcur * hcur).astype(BF16), w2_ref[cols, :])
    y = _layernorm(alpha * x1 + acc, ln[2:3, :], ln[3:4, :])
    y_ref[...] = y
    yb_ref[...] = y.astype(BF16)


def _post(o, x, w_out, w1, w2, g1, b1, g2, b2, alpha):
    t, dm = x.shape
    vw = o.shape[1]
    dff = w1.shape[1]
    tm = min(ROW_TILE, t)
    ln = jnp.pad(jnp.stack([g1, b1, g2, b2]).astype(F32), ((0, 4), (0, 0)))
    const = lambda shape: pl.BlockSpec(shape, lambda i: (0, 0), pipeline_mode=pl.Buffered(1))
    return pl.pallas_call(
        functools.partial(_post_kernel, alpha=alpha),
        out_shape=(jax.ShapeDtypeStruct((t, dm), F32), jax.ShapeDtypeStruct((t, dm), BF16)),
        grid=(t // tm,),
        in_specs=[
            pl.BlockSpec((tm, vw), lambda i: (i, 0)),
            pl.BlockSpec((tm, dm), lambda i: (i, 0)),
            const((vw, dm)), const((dm, dff)), const((dff, dm)), const((8, dm)),
        ],
        out_specs=(pl.BlockSpec((tm, dm), lambda i: (i, 0)), pl.BlockSpec((tm, dm), lambda i: (i, 0))),
        compiler_params=pltpu.CompilerParams(
            dimension_semantics=("arbitrary",), vmem_limit_bytes=VMEM_LIMIT),
        name="post",
    )(o, x, w_out.astype(BF16), w1.astype(BF16), w2.astype(BF16), ln)


def kernel(x, a_w_in, a_conv, a_alog, a_dt_bias, a_norm_g, a_w_out, b_w_in, b_gate_w2, b_gate_b,
           b_norm_g, b_w_out, ln1_g, ln1_b, mlp_w1, mlp_w2, ln2_g, ln2_b):
    bn, seq, dm = x.shape
    depth = ln1_g.shape[0]
    alpha = (2 * depth) ** 0.25
    xf = x.astype(F32).reshape(bn * seq, dm)
    xb = xf.astype(BF16)
    for i in range(depth):
        j = i // 2
        xb3 = xb.reshape(bn, seq, dm)
        if i % 2 == 0:
            o = _gdn_mixer(xb3, a_w_in[j], a_conv[j], a_alog[j], a_dt_bias[j], a_norm_g[j])
            w_out = a_w_out[j]
        else:
            o = _gla_mixer(xb3, b_w_in[j], b_gate_w2[j], b_gate_b[j], b_norm_g[j])
            w_out = b_w_out[j]
        xf, xb = _post(o.reshape(bn * seq, -1), xf, w_out, mlp_w1[i], mlp_w2[i],
                       ln1_g[i], ln1_b[i], ln2_g[i], ln2_b[i], alpha)
    return xf.reshape(bn, seq, dm).astype(x.dtype)
```

```python
import functools
import math

import numpy as np

import jax
import jax.numpy as jnp
from jax import lax
from jax.experimental import pallas as pl
from jax.experimental.pallas import tpu as pltpu

F32 = jnp.float32
BF16 = jnp.bfloat16

A_HEADS, A_DK, A_DV, A_CONV = 8, 128, 128, 5
B_HEADS, B_DK, B_DV, B_RANK, B_TAU = 4, 128, 256, 16, 16.0
LN_EPS, RMS_EPS, L2_EPS = 1e-5, 1e-6, 1e-6

CH = 128
N_LEVELS = 7
HALO = 8
SOLVE_LEVELS_PER_STAGE = 1
PREP_SLICES = 8
GLA_GROUP = 4
GLA_AHEAD = 2
GLA_PIECES = 2
FIN_GROUP = 4
NEG_BIG = -1e30
VMEM_LIMIT = 56 * 1024 * 1024
ROW_TILE = 512
FF_TILE = 1024

assert CH == A_DK == B_DK and 2 ** N_LEVELS == CH


def _dot(a, b):
    return jnp.dot(a, b, preferred_element_type=F32)


def _dot_nt(a, b):
    return lax.dot_general(a, b, (((1,), (1,)), ((), ())), preferred_element_type=F32)


def _dot_tn(a, b):
    return lax.dot_general(a, b, (((0,), (0,)), ((), ())), preferred_element_type=F32)


def _split(x, n):
    pieces = []
    for _ in range(n - 1):
        p = x.astype(BF16)
        pieces.append(p)
        x = x - p.astype(F32)
    pieces.append(x.astype(BF16))
    return jnp.concatenate(pieces, axis=1)


def _fold(y, n):
    w = y.shape[1] // n
    out = y[:, 0:w]
    for i in range(1, n):
        out = out + y[:, i * w:(i + 1) * w]
    return out


def _dot_exact(m01, x):
    return _fold(_dot(m01, _split(x, 3)), 3)


def _sigmoid(x):
    return 0.5 + 0.5 * jnp.tanh(0.5 * x)


def _silu(x):
    h = 0.5 * x
    return h + h * jnp.tanh(h)


def _softplus(x):
    return jnp.maximum(x, 0.0) + jnp.log(1.0 + jnp.exp(-jnp.abs(x)))


def _layernorm(y, g, b):
    mu = jnp.mean(y, axis=-1, keepdims=True)
    yc = y - mu
    var = jnp.mean(yc * yc, axis=-1, keepdims=True)
    return yc * lax.rsqrt(var + LN_EPS) * g + b


def _order_masks(rev):
    row = lax.broadcasted_iota(jnp.int32, (CH, CH), 0)
    col = lax.broadcasted_iota(jnp.int32, (CH, CH), 1)
    if rev:
        return col >= row, col > row
    return col <= row, col < row


def _group(n, want):
    return math.gcd(n, want)


def _gdn_kernel(xb_ref, wh_ref, cw_ref, hp_ref, ng_ref, lm_ref, sh_ref, o_ref,
                p_ref, a_ref, t_ref, qk_ref, rhs_ref, qd_ref, kd_ref, gl_ref,
                mc_ref, qc_ref, rc_ref, oc_ref, *, seq):
    nc = seq // CH
    spare = 2 * nc

    p_ref[0:HALO, :] = jnp.zeros((HALO, p_ref.shape[1]), F32)
    p_ref[HALO + seq:, :] = jnp.zeros((HALO, p_ref.shape[1]), F32)
    p_ref[HALO:HALO + seq, :] = _dot(xb_ref[0], wh_ref[0])
    for ref in (a_ref, t_ref, qk_ref, kd_ref, rhs_ref, qd_ref):
        ref[spare] = jnp.zeros(ref.shape[1:], ref.dtype)

    cw = cw_ref[0]
    hp = hp_ref[0]
    zero16 = jnp.zeros((CH, CH), BF16)

    def pair(f, b):
        return jnp.concatenate([jnp.concatenate([f, zero16], axis=1),
                                jnp.concatenate([zero16, b], axis=1)], axis=0)

    def tiles(it, lag):
        c = it - lag
        ok = jnp.logical_and(c >= 0, c < nc)
        return jnp.where(ok, c, spare), jnp.where(ok, nc + c, spare)

    def prep(c):
        base = pl.multiple_of(c * CH, CH)
        win = p_ref[pl.ds(base, CH + 2 * HALO), 0:3 * A_DK]
        shifted = _dot(sh_ref[...], win.astype(BF16))
        mid = A_CONV // 2
        acc = win[HALO:HALO + CH, :] * cw[mid:mid + 1, :]
        yield
        for n, i in enumerate([i for i in range(A_CONV) if i != mid]):
            acc = acc + shifted[n * CH:(n + 1) * CH, :] * cw[i:i + 1, :]
        yield
        s = _silu(acc)
        q = s[:, 0:A_DK]
        k = s[:, A_DK:2 * A_DK]
        v = s[:, 2 * A_DK:3 * A_DK]
        q = q * (lax.rsqrt(jnp.sum(q * q, axis=-1, keepdims=True) + L2_EPS) * (A_DK ** -0.5))
        k = k * lax.rsqrt(jnp.sum(k * k, axis=-1, keepdims=True) + L2_EPS)
        ba = p_ref[pl.ds(base + HALO, CH), 4 * A_DK:8 * A_DK]
        kb16 = k.astype(BF16)
        eye = (lax.broadcasted_iota(jnp.int32, (CH, CH), 0)
               == lax.broadcasted_iota(jnp.int32, (CH, CH), 1)).astype(F32)
        yield
        dirs = ((0, False), (1, True))
        beta = [_sigmoid(ba[:, d * A_DK:(d + 1) * A_DK]) for d, _ in dirs]
        g = [-hp[d:d + 1, :] * _softplus(ba[:, (2 + d) * A_DK:(3 + d) * A_DK] + hp[2 + d:3 + d, :])
             for d, _ in dirs]
        gc = [_dot_exact(_order_masks(rev)[0].astype(BF16), g[d]) for d, rev in dirs]
        yield
        kbeta = [k * beta[d] for d, _ in dirs]
        kq = [_dot_nt(jnp.concatenate([kbeta[d], q], axis=0).astype(BF16), kb16) for d, _ in dirs]
        gtot = [gc[d][0:1, :] if rev else gc[d][CH - 1:CH, :] for d, rev in dirs]
        eg = [jnp.exp(gc[d]) for d, _ in dirs]
        yield
        dmat = [jnp.exp(jnp.where(_order_masks(rev)[0], gc[d] - gc[d].T, NEG_BIG)) for d, rev in dirs]
        yield
        for d, rev in dirs:
            idx = d * nc + c
            a = jnp.where(_order_masks(rev)[1], kq[d][0:CH] * dmat[d], 0.0).astype(BF16)
            a_ref[idx] = a
            t_ref[idx] = (eye - (a * lm_ref[0]).astype(F32)).astype(BF16)
            qk_ref[idx] = (kq[d][CH:] * dmat[d]).astype(BF16)
        yield
        for d, rev in dirs:
            idx = d * nc + c
            rhs_ref[idx] = jnp.concatenate([v * beta[d], kbeta[d] * eg[d]], axis=1).astype(BF16)
            qd_ref[idx] = q * eg[d]
            kd_ref[idx] = (k * jnp.exp(gtot[d] - gc[d])).astype(BF16)
            gl_ref[idx] = jnp.broadcast_to(jnp.exp(gtot[d]), (8, A_DK))
        yield

    per_stage = SOLVE_LEVELS_PER_STAGE
    n_stages = (N_LEVELS - 1) // per_stage
    solve_stages = tuple((1 + s, tuple(range(1 + s * per_stage, 1 + (s + 1) * per_stage)))
                         for s in range(n_stages))
    ops_lag = n_stages + 1
    fills = -(-PREP_SLICES // (2 * per_stage))

    def step(it, with_prep):
        filler = prep(it) if with_prep else iter(())

        def fill(n):
            for _ in range(n):
                next(filler, None)

        loaded = []
        for lag, lvs in solve_stages:
            jf, jb = tiles(it, lag)
            loaded.append((jf, jb, t_ref[jf], t_ref[jb], a_ref[jf], a_ref[jb]))
        ops_in = [(i, t_ref[i], rhs_ref[i], kd_ref[i], qk_ref[i], qd_ref[i]) for i in tiles(it, ops_lag)]
        uws = [_dot(t, rhs) for i, t, rhs, kd, qkm, qd in ops_in]
        for half in range(per_stage):
            xs = []
            for (lag, lvs), (jf, jb, tf, tb, af, ab) in zip(solve_stages, loaded):
                m = lm_ref[lvs[half]]
                xs.append(_dot(jnp.concatenate([tf, tb], axis=1), pair(af * m, ab * m)))
            fill(fills)
            ys = [_dot(x.astype(BF16), pair(tf, tb)) for x, (jf, jb, tf, tb, af, ab) in zip(xs, loaded)]
            if half == 0:
                uws = [uw.astype(BF16) for uw in uws]
                kuws = [_dot_tn(kd, uw) for uw, (i, t, rhs, kd, qkm, qd) in zip(uws, ops_in)]
                quws = [_dot(qkm, uw) for uw, (i, t, rhs, kd, qkm, qd) in zip(uws, ops_in)]
            fill(fills)
            loaded = [(jf, jb, tf - y[:, 0:CH].astype(BF16), tb - y[:, CH:].astype(BF16), af, ab)
                      for y, (jf, jb, tf, tb, af, ab) in zip(ys, loaded)]
        for _ in filler:
            pass
        for jf, jb, tf, tb, _, _ in loaded:
            t_ref[jf] = tf
            t_ref[jb] = tb
        for kuw, quw, (i, t, rhs, kd, qkm, qd) in zip(kuws, quws, ops_in):
            qc_ref[i] = kuw[:, 0:A_DV]
            mc_ref[i] = kuw[:, A_DV:].astype(BF16)
            oc_ref[i] = quw[:, 0:A_DV]
            rc_ref[i] = (qd - quw[:, A_DV:]).astype(BF16)

    def main(it, carry):
        step(it, True)
        return carry

    def drain(it, carry):
        step(it, False)
        return carry

    lax.fori_loop(0, nc, main, 0)
    lax.fori_loop(nc, nc + ops_lag, drain, 0)

    def scan(c, carry):
        sf, sb = carry
        cf = c
        cb = 2 * nc - 1 - c
        of = _dot(rc_ref[cf], sf.astype(BF16)) + oc_ref[cf]
        ob = _dot(rc_ref[cb], sb.astype(BF16)) + oc_ref[cb]
        oc_ref[cf] = of
        oc_ref[cb] = ob
        sf = gl_ref[cf][0:1, :] * sf - _dot(mc_ref[cf], sf.astype(BF16)) + qc_ref[cf]
        sb = gl_ref[cb][0:1, :] * sb - _dot(mc_ref[cb], sb.astype(BF16)) + qc_ref[cb]
        return sf, sb

    zero = jnp.zeros((A_DK, A_DV), F32)
    lax.fori_loop(0, nc, scan, (zero, zero))

    ng = ng_ref[...]
    fg = _group(nc, FIN_GROUP)

    def fin(it, carry):
        cs = [it * fg + u for u in range(fg)]
        rows = [pl.ds(pl.multiple_of(c * CH, CH), CH) for c in cs]
        os_ = [oc_ref[c] + oc_ref[nc + c] for c in cs]
        inv = [lax.rsqrt(jnp.mean(o * o, axis=-1, keepdims=True) + RMS_EPS) for o in os_]
        for c, r, o, s in zip(cs, rows, os_, inv):
            z = p_ref[pl.ds(pl.multiple_of(c * CH, CH) + HALO, CH), 3 * A_DK:4 * A_DK]
            o_ref[0, r, :] = (o * s * ng * _silu(z)).astype(BF16)
        return carry

    lax.fori_loop(0, nc // fg, fin, 0)


def _gdn_level_masks():
    idx = np.arange(CH)
    x = idx[:, None] ^ idx[None, :]
    return jnp.asarray(np.stack([(x >> lv) == 1 for lv in range(N_LEVELS)]), BF16)


def _conv_shift_matrices():
    t = np.arange(CH)[:, None]
    r = np.arange(CH + 2 * HALO)[None, :]
    taps = [i for i in range(A_CONV) if i != A_CONV // 2]
    return jnp.asarray(np.concatenate([r == t + HALO + i - A_CONV // 2 for i in taps], axis=0), BF16)


def _gdn_mixer(xb, w_in, conv_w, a_log, dt_bias, norm_g):
    bn, seq, dm = xb.shape
    h, dk = A_HEADS, A_DK
    nc = seq // CH
    w = w_in
    hw = h * dk
    ba = w[:, 4 * hw:].reshape(dm, 2, 2, h)
    per_head = [w[:, i * hw:(i + 1) * hw].reshape(dm, h, dk).transpose(1, 0, 2) for i in range(4)]
    rep = [jnp.broadcast_to(ba[:, kind, d, :].T[:, :, None], (h, dm, dk))
           for kind in range(2) for d in range(2)]
    wh = jnp.concatenate(per_head + rep, axis=2).astype(BF16)
    cw = conv_w.reshape(A_CONV, 3, h, dk).transpose(2, 0, 1, 3).reshape(h, A_CONV, 3 * dk)
    cw = jnp.pad(cw, ((0, 0), (0, 8 - A_CONV), (0, 0))).astype(F32)
    hp = jnp.concatenate([jnp.exp(a_log.astype(F32)), dt_bias.astype(F32)], axis=0)
    hp = jnp.broadcast_to(jnp.pad(hp, ((0, 4), (0, 0))).T[:, :, None], (h, 8, dk))
    ng = norm_g.astype(F32).reshape(1, A_DV)
    lm = _gdn_level_masks()
    sh = _conv_shift_matrices()
    nw = wh.shape[2]

    kern = functools.partial(_gdn_kernel, seq=seq)
    tile = lambda dt: pltpu.VMEM((2 * nc + 1, CH, CH), dt)
    return pl.pallas_call(
        kern,
        out_shape=jax.ShapeDtypeStruct((bn, seq, h * A_DV), BF16),
        grid=(bn, h),
        in_specs=[
            pl.BlockSpec((1, seq, dm), lambda b, i: (b, 0, 0)),
            pl.BlockSpec((1, dm, nw), lambda b, i: (i, 0, 0)),
            pl.BlockSpec((1, 8, 3 * dk), lambda b, i: (i, 0, 0)),
            pl.BlockSpec((1, 8, dk), lambda b, i: (i, 0, 0)),
            pl.BlockSpec((1, A_DV), lambda b, i: (0, 0)),
            pl.BlockSpec(lm.shape, lambda b, i: (0, 0, 0)),
            pl.BlockSpec(sh.shape, lambda b, i: (0, 0)),
        ],
        out_specs=pl.BlockSpec((1, seq, A_DV), lambda b, i: (b, 0, i)),
        scratch_shapes=[
            pltpu.VMEM((seq + 2 * HALO, nw), F32),
            tile(BF16),
            tile(BF16),
            tile(BF16),
            pltpu.VMEM((2 * nc + 1, CH, A_DV + A_DK), BF16),
            tile(F32),
            tile(BF16),
            pltpu.VMEM((2 * nc, 8, A_DK), F32),
            tile(BF16),
            tile(F32),
            tile(BF16),
            tile(F32),
        ],
        compiler_params=pltpu.CompilerParams(
            dimension_semantics=("arbitrary", "arbitrary"), vmem_limit_bytes=VMEM_LIMIT),
        name="gdn_mixer",
    )(xb, wh, cw, hp, ng, lm, sh)


def _gla_tables():
    i = np.arange(CH)[:, None]
    t = np.arange(CH)[None, :]
    seg = np.zeros((2, N_LEVELS + 1, CH, CH), np.float32)
    lvl = np.zeros((2, CH, CH), np.int32)
    for d in range(2):
        rev = d == 1
        seg[d, 0] = (t >= i) if rev else (t <= i)
        lv = np.full((CH, CH), N_LEVELS + 1, np.int32)
        lv[np.arange(CH), np.arange(CH)] = N_LEVELS
        x = i ^ t
        for l in range(N_LEVELS):
            h = 2 ** (N_LEVELS - 1 - l)
            b0 = (i // (2 * h)) * (2 * h)
            if rev:
                r = b0 + h
                late = i < r
                m = np.where(late, (t >= i) & (t < r), (t >= r) & (t < i))
                own = ((x >> (N_LEVELS - 1 - l)) == 1) & (t > i)
            else:
                r = b0 + h - 1
                late = i > r
                m = np.where(late, (t > r) & (t <= i), (t > i) & (t <= r))
                own = ((x >> (N_LEVELS - 1 - l)) == 1) & (t < i)
            seg[d, 1 + l] = m
            lv[own] = l
        lvl[d] = lv
    return jnp.asarray(seg, BF16), jnp.asarray(lvl)


def _gla_kernel(xb_ref, wh_ref, w2_ref, gb_ref, ng_ref, seg_ref, lvl_ref, o_ref,
                p_ref, qs_ref, kv_ref, st_ref, dec_ref, oi_ref, *, seq):
    nc = seq // CH
    dk, dv = B_DK, B_DV
    p_ref[...] = _dot(xb_ref[0], wh_ref[0])

    c_q, c_k, c_v, c_r, c_g = 0, dk, 2 * dk, 2 * dk + dv, 2 * dk + 2 * dv
    gg = _group(nc, GLA_GROUP)
    lanes = [(u, d) for u in range(gg) for d in range(2)]

    def prep(it, carry):
        cs = [it * gg + u for u in range(gg)]
        rows = [pl.ds(pl.multiple_of(c * CH, CH), CH) for c in cs]
        q = [p_ref[r, c_q:c_q + dk] * (dk ** -0.5) for r in rows]
        k = [p_ref[r, c_k:c_k + dk] for r in rows]
        q16 = [x.astype(BF16) for x in q]
        k16 = [x.astype(BF16) for x in k]
        v16 = [p_ref[r, c_v:c_v + dv].astype(BF16) for r in rows]
        gin = [p_ref[r, c_g:c_g + dk].astype(BF16) for r in rows]
        logit = [_dot(gin[u], w2_ref[0, d]) + gb_ref[0, d][0:1, :] for u, d in lanes]
        la3 = [_split(-_softplus(-x) * (1.0 / B_TAU), 3) for x in logit]
        la2 = [y[:, 0:GLA_PIECES * dk] for y in la3]
        bc = [_fold(_dot(seg_ref[d, 0], y), 3) for (u, d), y in zip(lanes, la3)]

        def level_sums(l):
            h = CH >> (l + 1)
            if h < HALO:
                return [_fold(_dot(seg_ref[d, 1 + l], y), GLA_PIECES) for (u, d), y in zip(lanes, la2)]
            out = []
            for (u, d), b in zip(lanes, bc):
                blocks = []
                for lo in range(0, CH, 2 * h):
                    r = lo + h if d == 1 else lo + h - 1
                    diff = b[lo:lo + 2 * h, :] - b[r:r + 1, :]
                    late = diff[0:h] if d == 1 else diff[h:]
                    early = -(diff[h:] if d == 1 else diff[0:h])
                    blocks += [late, early] if d == 1 else [early, late]
                out.append(jnp.concatenate(blocks, axis=0))
            return out

        scores = [jnp.zeros((CH, CH), F32) for _ in lanes]
        ahead = [level_sums(l) for l in range(GLA_AHEAD)]
        prod = None
        for l in range(N_LEVELS):
            if l + GLA_AHEAD < N_LEVELS:
                ahead.append(level_sums(l + GLA_AHEAD))
            e = [jnp.exp(x) for x in ahead[l]]
            ql = [(q[u] * x).astype(BF16) for (u, d), x in zip(lanes, e)]
            kl = [(k[u] * x).astype(BF16) for (u, d), x in zip(lanes, e)]
            if prod is not None:
                own = [lvl_ref[d] == l - 1 for d in range(2)]
                scores = [jnp.where(own[d], p, s) for (u, d), p, s in zip(lanes, prod, scores)]
            prod = [_dot_nt(a, b) for a, b in zip(ql, kl)]
        own = [lvl_ref[d] == N_LEVELS - 1 for d in range(2)]
        scores = [jnp.where(own[d], p, s) for (u, d), p, s in zip(lanes, prod, scores)]
        diag = [_dot_nt(q16[u], k16[u]) for u in range(gg)]
        own = [lvl_ref[d] == N_LEVELS for d in range(2)]
        scores = [jnp.where(own[d], diag[u], s) for (u, d), s in zip(lanes, scores)]
        for (u, d), s, b in zip(lanes, scores, bc):
            idx = d * nc + cs[u]
            btot = b[0:1, :] if d == 1 else b[CH - 1:CH, :]
            oi_ref[idx] = _dot(s.astype(BF16), v16[u])
            qs_ref[idx] = (q[u] * jnp.exp(b)).astype(BF16)
            kv_ref[idx] = _dot_tn((k[u] * jnp.exp(btot - b)).astype(BF16), v16[u])
            dec_ref[idx] = jnp.exp(jnp.broadcast_to(btot, (CH, dk)).T)
        return carry

    lax.fori_loop(0, nc // gg, prep, 0)

    def scan(c, carry):
        sf, sb = carry
        cf = c
        ib = 2 * nc - 1 - c
        st_ref[cf] = sf.astype(BF16)
        st_ref[ib] = sb.astype(BF16)
        ef = dec_ref[cf]
        eb = dec_ref[ib]
        sf = sf * jnp.concatenate([ef, ef], axis=1) + kv_ref[cf]
        sb = sb * jnp.concatenate([eb, eb], axis=1) + kv_ref[ib]
        return sf, sb

    zero = jnp.zeros((dk, dv), F32)
    lax.fori_loop(0, nc, scan, (zero, zero))

    ng = ng_ref[...]
    fg = _group(nc, FIN_GROUP)

    def fin(it, carry):
        cs = [it * fg + u for u in range(fg)]
        rows = [pl.ds(pl.multiple_of(c * CH, CH), CH) for c in cs]
        os_ = [oi_ref[c] + oi_ref[nc + c] + _dot(qs_ref[c], st_ref[c]) + _dot(qs_ref[nc + c], st_ref[nc + c])
               for c in cs]
        inv = [lax.rsqrt(jnp.mean(o * o, axis=-1, keepdims=True) + RMS_EPS) for o in os_]
        for r, o, s in zip(rows, os_, inv):
            o_ref[0, r, :] = (o * s * ng * _silu(p_ref[r, c_r:c_r + dv])).astype(BF16)
        return carry

    lax.fori_loop(0, nc // fg, fin, 0)


def _gla_mixer(xb, w_in, gate_w2, gate_b, norm_g):
    bn, seq, dm = xb.shape
    h, dk, dv = B_HEADS, B_DK, B_DV
    nc = seq // CH
    kw, vw = h * dk, h * dv
    w = w_in

    def heads(cols, width):
        return cols.reshape(dm, h, width).transpose(1, 0, 2)

    gl = jnp.pad(w[:, 2 * kw + 2 * vw:], ((0, 0), (0, dk - 2 * B_RANK)))
    wh = jnp.concatenate([
        heads(w[:, 0:kw], dk), heads(w[:, kw:2 * kw], dk),
        heads(w[:, 2 * kw:2 * kw + vw], dv), heads(w[:, 2 * kw + vw:2 * kw + 2 * vw], dv),
        jnp.broadcast_to(gl[None], (h, dm, dk))], axis=2).astype(BF16)
    w2 = gate_w2.reshape(2, B_RANK, h, dk).transpose(2, 0, 1, 3)
    w2p = jnp.zeros((h, 2, dk, dk), F32)
    w2p = w2p.at[:, 0, 0:B_RANK].set(w2[:, 0]).at[:, 1, B_RANK:2 * B_RANK].set(w2[:, 1]).astype(BF16)
    gb = gate_b.reshape(2, h, dk).transpose(1, 0, 2).astype(F32)
    gb = jnp.broadcast_to(gb[:, :, None, :], (h, 2, 8, dk))
    ng = norm_g.astype(F32).reshape(1, dv)
    seg, lvl = _gla_tables()
    nw = wh.shape[2]

    kern = functools.partial(_gla_kernel, seq=seq)
    return pl.pallas_call(
        kern,
        out_shape=jax.ShapeDtypeStruct((bn, seq, vw), BF16),
        grid=(bn, h),
        in_specs=[
            pl.BlockSpec((1, seq, dm), lambda b, i: (b, 0, 0)),
            pl.BlockSpec((1, dm, nw), lambda b, i: (i, 0, 0)),
            pl.BlockSpec((1, 2, dk, dk), lambda b, i: (i, 0, 0, 0)),
            pl.BlockSpec((1, 2, 8, dk), lambda b, i: (i, 0, 0, 0)),
            pl.BlockSpec((1, dv), lambda b, i: (0, 0)),
            pl.BlockSpec(seg.shape, lambda b, i: (0, 0, 0, 0)),
            pl.BlockSpec(lvl.shape, lambda b, i: (0, 0, 0)),
        ],
        out_specs=pl.BlockSpec((1, seq, dv), lambda b, i: (b, 0, i)),
        scratch_shapes=[
            pltpu.VMEM((seq, nw), F32),
            pltpu.VMEM((2 * nc, CH, dk), BF16),
            pltpu.VMEM((2 * nc, dk, dv), F32),
            pltpu.VMEM((2 * nc, dk, dv), BF16),
            pltpu.VMEM((2 * nc, dk, dk), F32),
            pltpu.VMEM((2 * nc, CH, dv), F32),
        ],
        compiler_params=pltpu.CompilerParams(
            dimension_semantics=("arbitrary", "arbitrary"), vmem_limit_bytes=VMEM_LIMIT),
        name="gla_mixer",
    )(xb, wh, w2p, gb, ng, seg, lvl)


def _post_kernel(o_ref, x_ref, wo_ref, w1_ref, w2_ref, ln_ref, y_ref, yb_ref, *, alpha):
    ln = ln_ref[...]
    x = x_ref[...]
    x1 = _layernorm(alpha * x + _dot(o_ref[...], wo_ref[...]), ln[0:1, :], ln[1:2, :])
    x1b = x1.astype(BF16)
    acc = jnp.zeros(x.shape, F32)
    dff = w1_ref.shape[1]
    for j in range(dff // FF_TILE):
        cols = slice(j * FF_TILE, (j + 1) * FF_TILE)
        hcur = jnp.maximum(_dot(x1b, w1_ref[:, cols]), 0.0)
        acc = acc + _dot((hcur * hcur).astype(BF16), w2_ref[cols, :])
    y = _layernorm(alpha * x1 + acc, ln[2:3, :], ln[3:4, :])
    y_ref[...] = y
    yb_ref[...] = y.astype(BF16)


def _post(o, x, w_out, w1, w2, g1, b1, g2, b2, alpha):
    t, dm = x.shape
    vw = o.shape[1]
    dff = w1.shape[1]
    tm = min(ROW_TILE, t)
    ln = jnp.pad(jnp.stack([g1, b1, g2, b2]).astype(F32), ((0, 4), (0, 0)))
    const = lambda shape: pl.BlockSpec(shape, lambda i: (0, 0), pipeline_mode=pl.Buffered(1))
    return pl.pallas_call(
        functools.partial(_post_kernel, alpha=alpha),
        out_shape=(jax.ShapeDtypeStruct((t, dm), F32), jax.ShapeDtypeStruct((t, dm), BF16)),
        grid=(t // tm,),
        in_specs=[
            pl.BlockSpec((tm, vw), lambda i: (i, 0)),
            pl.BlockSpec((tm, dm), lambda i: (i, 0)),
            const((vw, dm)), const((dm, dff)), const((dff, dm)), const((8, dm)),
        ],
        out_specs=(pl.BlockSpec((tm, dm), lambda i: (i, 0)), pl.BlockSpec((tm, dm), lambda i: (i, 0))),
        compiler_params=pltpu.CompilerParams(
            dimension_semantics=("arbitrary",), vmem_limit_bytes=VMEM_LIMIT),
        name="post",
    )(o, x, w_out.astype(BF16), w1.astype(BF16), w2.astype(BF16), ln)


def kernel(x, a_w_in, a_conv, a_alog, a_dt_bias, a_norm_g, a_w_out, b_w_in, b_gate_w2, b_gate_b,
           b_norm_g, b_w_out, ln1_g, ln1_b, mlp_w1, mlp_w2, ln2_g, ln2_b):
    bn, seq, dm = x.shape
    depth = ln1_g.shape[0]
    alpha = (2 * depth) ** 0.25
    xf = x.astype(F32).reshape(bn * seq, dm)
    xb = xf.astype(BF16)
    for i in range(depth):
        j = i // 2
        xb3 = xb.reshape(bn, seq, dm)
        if i % 2 == 0:
            o = _gdn_mixer(xb3, a_w_in[j], a_conv[j], a_alog[j], a_dt_bias[j], a_norm_g[j])
            w_out = a_w_out[j]
        else:
            o = _gla_mixer(xb3, b_w_in[j], b_gate_w2[j], b_gate_b[j], b_norm_g[j])
            w_out = b_w_out[j]
        xf, xb = _post(o.reshape(bn * seq, -1), xf, w_out, mlp_w1[i], mlp_w2[i],
                       ln1_g[i], ln1_b[i], ln2_g[i], ln2_b[i], alpha)
    return xf.reshape(bn, seq, dm).astype(x.dtype)
```

```python
import functools
import math

import numpy as np

import jax
import jax.numpy as jnp
from jax import lax
from jax.experimental import pallas as pl
from jax.experimental.pallas import tpu as pltpu

F32 = jnp.float32
BF16 = jnp.bfloat16

A_HEADS, A_DK, A_DV, A_CONV = 8, 128, 128, 5
B_HEADS, B_DK, B_DV, B_RANK, B_TAU = 4, 128, 256, 16, 16.0
LN_EPS, RMS_EPS, L2_EPS = 1e-5, 1e-6, 1e-6

CH = 128
N_LEVELS = 7
HALO = 8
SOLVE_LEVELS_PER_STAGE = 1
PREP_SLICES = 8
GLA_GROUP = 4
GLA_AHEAD = 2
GLA_PIECES = 2
FIN_GROUP = 4
NEG_BIG = -1e30
LOG2E = math.log2(math.e)
VMEM_LIMIT = 56 * 1024 * 1024
ROW_TILE = 512
FF_TILE = 1024

assert CH == A_DK == B_DK and 2 ** N_LEVELS == CH


def _dot(a, b):
    return jnp.dot(a, b, preferred_element_type=F32)


def _dot_nt(a, b):
    return lax.dot_general(a, b, (((1,), (1,)), ((), ())), preferred_element_type=F32)


def _dot_tn(a, b):
    return lax.dot_general(a, b, (((0,), (0,)), ((), ())), preferred_element_type=F32)


def _split(x, n):
    pieces = []
    for _ in range(n - 1):
        p = x.astype(BF16)
        pieces.append(p)
        x = x - p.astype(F32)
    pieces.append(x.astype(BF16))
    return jnp.concatenate(pieces, axis=1)


def _fold(y, n):
    w = y.shape[1] // n
    out = y[:, 0:w]
    for i in range(1, n):
        out = out + y[:, i * w:(i + 1) * w]
    return out


def _dot_exact(m01, x):
    return _fold(_dot(m01, _split(x, 3)), 3)


def _sigmoid(x):
    return 0.5 + 0.5 * jnp.tanh(0.5 * x)


def _silu(x):
    h = 0.5 * x
    return h + h * jnp.tanh(h)


def _softplus(x):
    return jnp.maximum(x, 0.0) + jnp.log(1.0 + jnp.exp(-jnp.abs(x)))


def _layernorm(y, g, b):
    mu = jnp.mean(y, axis=-1, keepdims=True)
    yc = y - mu
    var = jnp.mean(yc * yc, axis=-1, keepdims=True)
    return yc * lax.rsqrt(var + LN_EPS) * g + b


def _order_masks(rev):
    row = lax.broadcasted_iota(jnp.int32, (CH, CH), 0)
    col = lax.broadcasted_iota(jnp.int32, (CH, CH), 1)
    if rev:
        return col >= row, col > row
    return col <= row, col < row


def _group(n, want):
    return math.gcd(n, want)


def _gdn_kernel(xb_ref, wh_ref, cw_ref, hp_ref, ng_ref, lm_ref, sh_ref, o_ref,
                p_ref, a_ref, t_ref, qk_ref, rhs_ref, qd_ref, kd_ref, gl_ref,
                mc_ref, qc_ref, rc_ref, oc_ref, *, seq):
    nc = seq // CH
    spare = 2 * nc

    p_ref[0:HALO, :] = jnp.zeros((HALO, p_ref.shape[1]), F32)
    p_ref[HALO + seq:, :] = jnp.zeros((HALO, p_ref.shape[1]), F32)
    p_ref[HALO:HALO + seq, :] = _dot(xb_ref[0], wh_ref[0])
    for ref in (a_ref, t_ref, qk_ref, kd_ref, rhs_ref, qd_ref):
        ref[spare] = jnp.zeros(ref.shape[1:], ref.dtype)

    cw = cw_ref[0]
    hp = hp_ref[0]
    zero16 = jnp.zeros((CH, CH), BF16)

    def pair(f, b):
        return jnp.concatenate([jnp.concatenate([f, zero16], axis=1),
                                jnp.concatenate([zero16, b], axis=1)], axis=0)

    def tiles(it, lag):
        c = it - lag
        ok = jnp.logical_and(c >= 0, c < nc)
        return jnp.where(ok, c, spare), jnp.where(ok, nc + c, spare)

    def prep(c):
        base = pl.multiple_of(c * CH, CH)
        win = p_ref[pl.ds(base, CH + 2 * HALO), 0:3 * A_DK]
        shifted = _dot(sh_ref[...], win.astype(BF16))
        mid = A_CONV // 2
        acc = win[HALO:HALO + CH, :] * cw[mid:mid + 1, :]
        yield
        for n, i in enumerate([i for i in range(A_CONV) if i != mid]):
            acc = acc + shifted[n * CH:(n + 1) * CH, :] * cw[i:i + 1, :]
        yield
        s = _silu(acc)
        q = s[:, 0:A_DK]
        k = s[:, A_DK:2 * A_DK]
        v = s[:, 2 * A_DK:3 * A_DK]
        q = q * (lax.rsqrt(jnp.sum(q * q, axis=-1, keepdims=True) + L2_EPS) * (A_DK ** -0.5))
        k = k * lax.rsqrt(jnp.sum(k * k, axis=-1, keepdims=True) + L2_EPS)
        ba = p_ref[pl.ds(base + HALO, CH), 4 * A_DK:8 * A_DK]
        kb16 = k.astype(BF16)
        eye = (lax.broadcasted_iota(jnp.int32, (CH, CH), 0)
               == lax.broadcasted_iota(jnp.int32, (CH, CH), 1)).astype(F32)
        yield
        dirs = ((0, False), (1, True))
        beta = [_sigmoid(ba[:, d * A_DK:(d + 1) * A_DK]) for d, _ in dirs]
        g = [-hp[d:d + 1, :] * _softplus(ba[:, (2 + d) * A_DK:(3 + d) * A_DK] + hp[2 + d:3 + d, :])
             for d, _ in dirs]
        gc = [_dot_exact(_order_masks(rev)[0].astype(BF16), g[d]) for d, rev in dirs]
        yield
        kbeta = [k * beta[d] for d, _ in dirs]
        kq = [_dot_nt(jnp.concatenate([kbeta[d], q], axis=0).astype(BF16), kb16) for d, _ in dirs]
        gtot = [gc[d][0:1, :] if rev else gc[d][CH - 1:CH, :] for d, rev in dirs]
        eg = [jnp.exp2(gc[d]) for d, _ in dirs]
        yield
        dmat = [jnp.exp2(jnp.where(_order_masks(rev)[0], gc[d] - gc[d].T, NEG_BIG)) for d, rev in dirs]
        yield
        for d, rev in dirs:
            idx = d * nc + c
            a = jnp.where(_order_masks(rev)[1], kq[d][0:CH] * dmat[d], 0.0).astype(BF16)
            a_ref[idx] = a
            t_ref[idx] = (eye - (a * lm_ref[0]).astype(F32)).astype(BF16)
            qk_ref[idx] = (kq[d][CH:] * dmat[d]).astype(BF16)
        yield
        for d, rev in dirs:
            idx = d * nc + c
            rhs_ref[idx] = jnp.concatenate([v * beta[d], kbeta[d] * eg[d]], axis=1).astype(BF16)
            qd_ref[idx] = q * eg[d]
            kd_ref[idx] = (k * jnp.exp2(gtot[d] - gc[d])).astype(BF16)
            gl_ref[idx] = jnp.broadcast_to(jnp.exp2(gtot[d]), (8, A_DK))
        yield

    per_stage = SOLVE_LEVELS_PER_STAGE
    n_stages = (N_LEVELS - 1) // per_stage
    solve_stages = tuple((1 + s, tuple(range(1 + s * per_stage, 1 + (s + 1) * per_stage)))
                         for s in range(n_stages))
    ops_lag = n_stages + 1
    fills = -(-PREP_SLICES // (2 * per_stage))

    def step(it, with_prep):
        filler = prep(it) if with_prep else iter(())

        def fill(n):
            for _ in range(n):
                next(filler, None)

        loaded = []
        for lag, lvs in solve_stages:
            jf, jb = tiles(it, lag)
            loaded.append((jf, jb, t_ref[jf], t_ref[jb], a_ref[jf], a_ref[jb]))
        ops_in = [(i, t_ref[i], rhs_ref[i], kd_ref[i], qk_ref[i], qd_ref[i]) for i in tiles(it, ops_lag)]
        uws = [_dot(t, rhs) for i, t, rhs, kd, qkm, qd in ops_in]
        for half in range(per_stage):
            xs = []
            for (lag, lvs), (jf, jb, tf, tb, af, ab) in zip(solve_stages, loaded):
                m = lm_ref[lvs[half]]
                xs.append(_dot(jnp.concatenate([tf, tb], axis=1), pair(af * m, ab * m)))
            fill(fills)
            ys = [_dot(x.astype(BF16), pair(tf, tb)) for x, (jf, jb, tf, tb, af, ab) in zip(xs, loaded)]
            if half == 0:
                uws = [uw.astype(BF16) for uw in uws]
                kuws = [_dot_tn(kd, uw) for uw, (i, t, rhs, kd, qkm, qd) in zip(uws, ops_in)]
                quws = [_dot(qkm, uw) for uw, (i, t, rhs, kd, qkm, qd) in zip(uws, ops_in)]
            fill(fills)
            loaded = [(jf, jb, tf - y[:, 0:CH].astype(BF16), tb - y[:, CH:].astype(BF16), af, ab)
                      for y, (jf, jb, tf, tb, af, ab) in zip(ys, loaded)]
        for _ in filler:
            pass
        for jf, jb, tf, tb, _, _ in loaded:
            t_ref[jf] = tf
            t_ref[jb] = tb
        for kuw, quw, (i, t, rhs, kd, qkm, qd) in zip(kuws, quws, ops_in):
            qc_ref[i] = kuw[:, 0:A_DV]
            mc_ref[i] = kuw[:, A_DV:].astype(BF16)
            oc_ref[i] = quw[:, 0:A_DV]
            rc_ref[i] = (qd - quw[:, A_DV:]).astype(BF16)

    def main(it, carry):
        step(it, True)
        return carry

    def drain(it, carry):
        step(it, False)
        return carry

    lax.fori_loop(0, nc, main, 0)
    lax.fori_loop(nc, nc + ops_lag, drain, 0)

    def scan(c, carry):
        sf, sb = carry
        cf = c
        cb = 2 * nc - 1 - c
        of = _dot(rc_ref[cf], sf.astype(BF16)) + oc_ref[cf]
        ob = _dot(rc_ref[cb], sb.astype(BF16)) + oc_ref[cb]
        oc_ref[cf] = of
        oc_ref[cb] = ob
        sf = gl_ref[cf][0:1, :] * sf - _dot(mc_ref[cf], sf.astype(BF16)) + qc_ref[cf]
        sb = gl_ref[cb][0:1, :] * sb - _dot(mc_ref[cb], sb.astype(BF16)) + qc_ref[cb]
        return sf, sb

    zero = jnp.zeros((A_DK, A_DV), F32)
    lax.fori_loop(0, nc, scan, (zero, zero))

    ng = ng_ref[...]
    fg = _group(nc, FIN_GROUP)

    def fin(it, carry):
        cs = [it * fg + u for u in range(fg)]
        rows = [pl.ds(pl.multiple_of(c * CH, CH), CH) for c in cs]
        os_ = [oc_ref[c] + oc_ref[nc + c] for c in cs]
        inv = [lax.rsqrt(jnp.mean(o * o, axis=-1, keepdims=True) + RMS_EPS) for o in os_]
        for c, r, o, s in zip(cs, rows, os_, inv):
            z = p_ref[pl.ds(pl.multiple_of(c * CH, CH) + HALO, CH), 3 * A_DK:4 * A_DK]
            o_ref[0, r, :] = (o * s * ng * _silu(z)).astype(BF16)
        return carry

    lax.fori_loop(0, nc // fg, fin, 0)


def _gdn_level_masks():
    idx = np.arange(CH)
    x = idx[:, None] ^ idx[None, :]
    return jnp.asarray(np.stack([(x >> lv) == 1 for lv in range(N_LEVELS)]), BF16)


def _conv_shift_matrices():
    t = np.arange(CH)[:, None]
    r = np.arange(CH + 2 * HALO)[None, :]
    taps = [i for i in range(A_CONV) if i != A_CONV // 2]
    return jnp.asarray(np.concatenate([r == t + HALO + i - A_CONV // 2 for i in taps], axis=0), BF16)


def _gdn_mixer(xb, w_in, conv_w, a_log, dt_bias, norm_g):
    bn, seq, dm = xb.shape
    h, dk = A_HEADS, A_DK
    nc = seq // CH
    w = w_in
    hw = h * dk
    ba = w[:, 4 * hw:].reshape(dm, 2, 2, h)
    per_head = [w[:, i * hw:(i + 1) * hw].reshape(dm, h, dk).transpose(1, 0, 2) for i in range(4)]
    rep = [jnp.broadcast_to(ba[:, kind, d, :].T[:, :, None], (h, dm, dk))
           for kind in range(2) for d in range(2)]
    wh = jnp.concatenate(per_head + rep, axis=2).astype(BF16)
    cw = conv_w.reshape(A_CONV, 3, h, dk).transpose(2, 0, 1, 3).reshape(h, A_CONV, 3 * dk)
    cw = jnp.pad(cw, ((0, 0), (0, 8 - A_CONV), (0, 0))).astype(F32)
    hp = jnp.concatenate([jnp.exp(a_log.astype(F32)) * LOG2E, dt_bias.astype(F32)], axis=0)
    hp = jnp.broadcast_to(jnp.pad(hp, ((0, 4), (0, 0))).T[:, :, None], (h, 8, dk))
    ng = norm_g.astype(F32).reshape(1, A_DV)
    lm = _gdn_level_masks()
    sh = _conv_shift_matrices()
    nw = wh.shape[2]

    kern = functools.partial(_gdn_kernel, seq=seq)
    tile = lambda dt: pltpu.VMEM((2 * nc + 1, CH, CH), dt)
    return pl.pallas_call(
        kern,
        out_shape=jax.ShapeDtypeStruct((bn, seq, h * A_DV), BF16),
        grid=(bn, h),
        in_specs=[
            pl.BlockSpec((1, seq, dm), lambda b, i: (b, 0, 0)),
            pl.BlockSpec((1, dm, nw), lambda b, i: (i, 0, 0)),
            pl.BlockSpec((1, 8, 3 * dk), lambda b, i: (i, 0, 0)),
            pl.BlockSpec((1, 8, dk), lambda b, i: (i, 0, 0)),
            pl.BlockSpec((1, A_DV), lambda b, i: (0, 0)),
            pl.BlockSpec(lm.shape, lambda b, i: (0, 0, 0)),
            pl.BlockSpec(sh.shape, lambda b, i: (0, 0)),
        ],
        out_specs=pl.BlockSpec((1, seq, A_DV), lambda b, i: (b, 0, i)),
        scratch_shapes=[
            pltpu.VMEM((seq + 2 * HALO, nw), F32),
            tile(BF16),
            tile(BF16),
            tile(BF16),
            pltpu.VMEM((2 * nc + 1, CH, A_DV + A_DK), BF16),
            tile(F32),
            tile(BF16),
            pltpu.VMEM((2 * nc, 8, A_DK), F32),
            tile(BF16),
            tile(F32),
            tile(BF16),
            tile(F32),
        ],
        compiler_params=pltpu.CompilerParams(
            dimension_semantics=("arbitrary", "arbitrary"), vmem_limit_bytes=VMEM_LIMIT),
        name="gdn_mixer",
    )(xb, wh, cw, hp, ng, lm, sh)


def _gla_tables():
    i = np.arange(CH)[:, None]
    t = np.arange(CH)[None, :]
    seg = np.zeros((2, N_LEVELS + 1, CH, CH), np.float32)
    lvl = np.zeros((2, CH, CH), np.int32)
    for d in range(2):
        rev = d == 1
        seg[d, 0] = (t >= i) if rev else (t <= i)
        lv = np.full((CH, CH), N_LEVELS + 1, np.int32)
        lv[np.arange(CH), np.arange(CH)] = N_LEVELS
        x = i ^ t
        for l in range(N_LEVELS):
            h = 2 ** (N_LEVELS - 1 - l)
            b0 = (i // (2 * h)) * (2 * h)
            if rev:
                r = b0 + h
                late = i < r
                m = np.where(late, (t >= i) & (t < r), (t >= r) & (t < i))
                own = ((x >> (N_LEVELS - 1 - l)) == 1) & (t > i)
            else:
                r = b0 + h - 1
                late = i > r
                m = np.where(late, (t > r) & (t <= i), (t > i) & (t <= r))
                own = ((x >> (N_LEVELS - 1 - l)) == 1) & (t < i)
            seg[d, 1 + l] = m
            lv[own] = l
        lvl[d] = lv
    return jnp.asarray(seg, BF16), jnp.asarray(lvl)


def _gla_kernel(xb_ref, wh_ref, w2_ref, gb_ref, ng_ref, seg_ref, lvl_ref, o_ref,
                p_ref, qs_ref, kv_ref, st_ref, dec_ref, oi_ref, *, seq):
    nc = seq // CH
    dk, dv = B_DK, B_DV
    p_ref[...] = _dot(xb_ref[0], wh_ref[0])

    c_q, c_k, c_v, c_r, c_g = 0, dk, 2 * dk, 2 * dk + dv, 2 * dk + 2 * dv
    gg = _group(nc, GLA_GROUP)
    lanes = [(u, d) for u in range(gg) for d in range(2)]

    def prep(it, carry):
        cs = [it * gg + u for u in range(gg)]
        rows = [pl.ds(pl.multiple_of(c * CH, CH), CH) for c in cs]
        q = [p_ref[r, c_q:c_q + dk] * (dk ** -0.5) for r in rows]
        k = [p_ref[r, c_k:c_k + dk] for r in rows]
        q16 = [x.astype(BF16) for x in q]
        k16 = [x.astype(BF16) for x in k]
        v16 = [p_ref[r, c_v:c_v + dv].astype(BF16) for r in rows]
        gin = [p_ref[r, c_g:c_g + dk].astype(BF16) for r in rows]
        logit = [_dot(gin[u], w2_ref[0, d]) + gb_ref[0, d][0:1, :] for u, d in lanes]
        la3 = [_split(-_softplus(-x) * (LOG2E / B_TAU), 3) for x in logit]
        la2 = [y[:, 0:GLA_PIECES * dk] for y in la3]
        bc = [_fold(_dot(seg_ref[d, 0], y), 3) for (u, d), y in zip(lanes, la3)]

        def level_sums(l):
            h = CH >> (l + 1)
            if h < HALO:
                return [_fold(_dot(seg_ref[d, 1 + l], y), GLA_PIECES) for (u, d), y in zip(lanes, la2)]
            out = []
            for (u, d), b in zip(lanes, bc):
                blocks = []
                for lo in range(0, CH, 2 * h):
                    if d == 1:
                        ref = b[lo + h:lo + h + 1, :]
                        blocks += [b[lo:lo + h, :] - ref, ref - b[lo + h:lo + 2 * h, :]]
                    else:
                        ref = b[lo + h - 1:lo + h, :]
                        blocks += [ref - b[lo:lo + h, :], b[lo + h:lo + 2 * h, :] - ref]
                out.append(jnp.concatenate(blocks, axis=0))
            return out

        half = CH // 2
        zero_half = jnp.zeros((half, dk), BF16)

        def top_level(x, e, d, late):
            upper = (d == 1) != late
            rows = slice(half, CH) if upper else slice(0, half)
            kept = (x[rows] * e[rows]).astype(BF16)
            return jnp.concatenate([zero_half, kept] if upper else [kept, zero_half], axis=0)

        ahead = [level_sums(l) for l in range(GLA_AHEAD)]
        scores = prod = None
        for l in range(N_LEVELS):
            if l + GLA_AHEAD < N_LEVELS:
                ahead.append(level_sums(l + GLA_AHEAD))
            e = [jnp.exp2(x) for x in ahead[l]]
            if l == 0:
                ql = [top_level(q[u], x, d, True) for (u, d), x in zip(lanes, e)]
                kl = [top_level(k[u], x, d, False) for (u, d), x in zip(lanes, e)]
            else:
                ql = [(q[u] * x).astype(BF16) for (u, d), x in zip(lanes, e)]
                kl = [(k[u] * x).astype(BF16) for (u, d), x in zip(lanes, e)]
            if l == 1:
                scores = prod
            elif l > 1:
                own = [lvl_ref[d] == l - 1 for d in range(2)]
                scores = [jnp.where(own[d], p, s) for (u, d), p, s in zip(lanes, prod, scores)]
            prod = [_dot_nt(a, b) for a, b in zip(ql, kl)]
        own = [lvl_ref[d] == N_LEVELS - 1 for d in range(2)]
        scores = [jnp.where(own[d], p, s) for (u, d), p, s in zip(lanes, prod, scores)]
        diag = [_dot_nt(q16[u], k16[u]) for u in range(gg)]
        own = [lvl_ref[d] == N_LEVELS for d in range(2)]
        scores = [jnp.where(own[d], diag[u], s) for (u, d), s in zip(lanes, scores)]
        for (u, d), s, b in zip(lanes, scores, bc):
            idx = d * nc + cs[u]
            btot = b[0:1, :] if d == 1 else b[CH - 1:CH, :]
            oi_ref[idx] = _dot(s.astype(BF16), v16[u])
            qs_ref[idx] = (q[u] * jnp.exp2(b)).astype(BF16)
            kv_ref[idx] = _dot_tn((k[u] * jnp.exp2(btot - b)).astype(BF16), v16[u])
            dec_ref[idx] = jnp.exp2(jnp.broadcast_to(btot, (CH, dk)).T)
        return carry

    lax.fori_loop(0, nc // gg, prep, 0)

    def scan(c, carry):
        sf, sb = carry
        cf = c
        ib = 2 * nc - 1 - c
        st_ref[cf] = sf.astype(BF16)
        st_ref[ib] = sb.astype(BF16)
        ef = dec_ref[cf]
        eb = dec_ref[ib]
        sf = sf * jnp.concatenate([ef, ef], axis=1) + kv_ref[cf]
        sb = sb * jnp.concatenate([eb, eb], axis=1) + kv_ref[ib]
        return sf, sb

    zero = jnp.zeros((dk, dv), F32)
    lax.fori_loop(0, nc, scan, (zero, zero))

    ng = ng_ref[...]
    fg = _group(nc, FIN_GROUP)

    def fin(it, carry):
        cs = [it * fg + u for u in range(fg)]
        rows = [pl.ds(pl.multiple_of(c * CH, CH), CH) for c in cs]
        os_ = [oi_ref[c] + oi_ref[nc + c] + _dot(qs_ref[c], st_ref[c]) + _dot(qs_ref[nc + c], st_ref[nc + c])
               for c in cs]
        inv = [lax.rsqrt(jnp.mean(o * o, axis=-1, keepdims=True) + RMS_EPS) for o in os_]
        for r, o, s in zip(rows, os_, inv):
            o_ref[0, r, :] = (o * s * ng * _silu(p_ref[r, c_r:c_r + dv])).astype(BF16)
        return carry

    lax.fori_loop(0, nc // fg, fin, 0)


def _gla_mixer(xb, w_in, gate_w2, gate_b, norm_g):
    bn, seq, dm = xb.shape
    h, dk, dv = B_HEADS, B_DK, B_DV
    nc = seq // CH
    kw, vw = h * dk, h * dv
    w = w_in

    def heads(cols, width):
        return cols.reshape(dm, h, width).transpose(1, 0, 2)

    gl = jnp.pad(w[:, 2 * kw + 2 * vw:], ((0, 0), (0, dk - 2 * B_RANK)))
    wh = jnp.concatenate([
        heads(w[:, 0:kw], dk), heads(w[:, kw:2 * kw], dk),
        heads(w[:, 2 * kw:2 * kw + vw], dv), heads(w[:, 2 * kw + vw:2 * kw + 2 * vw], dv),
        jnp.broadcast_to(gl[None], (h, dm, dk))], axis=2).astype(BF16)
    w2 = gate_w2.reshape(2, B_RANK, h, dk).transpose(2, 0, 1, 3)
    w2p = jnp.zeros((h, 2, dk, dk), F32)
    w2p = w2p.at[:, 0, 0:B_RANK].set(w2[:, 0]).at[:, 1, B_RANK:2 * B_RANK].set(w2[:, 1]).astype(BF16)
    gb = gate_b.reshape(2, h, dk).transpose(1, 0, 2).astype(F32)
    gb = jnp.broadcast_to(gb[:, :, None, :], (h, 2, 8, dk))
    ng = norm_g.astype(F32).reshape(1, dv)
    seg, lvl = _gla_tables()
    nw = wh.shape[2]

    kern = functools.partial(_gla_kernel, seq=seq)
    return pl.pallas_call(
        kern,
        out_shape=jax.ShapeDtypeStruct((bn, seq, vw), BF16),
        grid=(bn, h),
        in_specs=[
            pl.BlockSpec((1, seq, dm), lambda b, i: (b, 0, 0)),
            pl.BlockSpec((1, dm, nw), lambda b, i: (i, 0, 0)),
            pl.BlockSpec((1, 2, dk, dk), lambda b, i: (i, 0, 0, 0)),
            pl.BlockSpec((1, 2, 8, dk), lambda b, i: (i, 0, 0, 0)),
            pl.BlockSpec((1, dv), lambda b, i: (0, 0)),
            pl.BlockSpec(seg.shape, lambda b, i: (0, 0, 0, 0)),
            pl.BlockSpec(lvl.shape, lambda b, i: (0, 0, 0)),
        ],
        out_specs=pl.BlockSpec((1, seq, dv), lambda b, i: (b, 0, i)),
        scratch_shapes=[
            pltpu.VMEM((seq, nw), F32),
            pltpu.VMEM((2 * nc, CH, dk), BF16),
            pltpu.VMEM((2 * nc, dk, dv), F32),
            pltpu.VMEM((2 * nc, dk, dv), BF16),
            pltpu.VMEM((2 * nc, dk, dk), F32),
            pltpu.VMEM((2 * nc, CH, dv), F32),
        ],
        compiler_params=pltpu.CompilerParams(
            dimension_semantics=("arbitrary", "arbitrary"), vmem_limit_bytes=VMEM_LIMIT),
        name="gla_mixer",
    )(xb, wh, w2p, gb, ng, seg, lvl)


def _post_kernel(o_ref, x_ref, wo_ref, w1_ref, w2_ref, ln_ref, y_ref, yb_ref, *, alpha):
    ln = ln_ref[...]
    x = x_ref[...]
    x1 = _layernorm(alpha * x + _dot(o_ref[...], wo_ref[...]), ln[0:1, :], ln[1:2, :])
    x1b = x1.astype(BF16)
    acc = jnp.zeros(x.shape, F32)
    dff = w1_ref.shape[1]
    for j in range(dff // FF_TILE):
        cols = slice(j * FF_TILE, (j + 1) * FF_TILE)
        hcur = jnp.maximum(_dot(x1b, w1_ref[:, cols]), 0.0)
        acc = acc + _dot((hcur * hcur).astype(BF16), w2_ref[cols, :])
    y = _layernorm(alpha * x1 + acc, ln[2:3, :], ln[3:4, :])
    y_ref[...] = y
    yb_ref[...] = y.astype(BF16)


def _post(o, x, w_out, w1, w2, g1, b1, g2, b2, alpha):
    t, dm = x.shape
    vw = o.shape[1]
    dff = w1.shape[1]
    tm = min(ROW_TILE, t)
    ln = jnp.pad(jnp.stack([g1, b1, g2, b2]).astype(F32), ((0, 4), (0, 0)))
    const = lambda shape: pl.BlockSpec(shape, lambda i: (0, 0), pipeline_mode=pl.Buffered(1))
    return pl.pallas_call(
        functools.partial(_post_kernel, alpha=alpha),
        out_shape=(jax.ShapeDtypeStruct((t, dm), F32), jax.ShapeDtypeStruct((t, dm), BF16)),
        grid=(t // tm,),
        in_specs=[
            pl.BlockSpec((tm, vw), lambda i: (i, 0)),
            pl.BlockSpec((tm, dm), lambda i: (i, 0)),
            const((vw, dm)), const((dm, dff)), const((dff, dm)), const((8, dm)),
        ],
        out_specs=(pl.BlockSpec((tm, dm), lambda i: (i, 0)), pl.BlockSpec((tm, dm), lambda i: (i, 0))),
        compiler_params=pltpu.CompilerParams(
            dimension_semantics=("arbitrary",), vmem_limit_bytes=VMEM_LIMIT),
        name="post",
    )(o, x, w_out.astype(BF16), w1.astype(BF16), w2.astype(BF16), ln)


def kernel(x, a_w_in, a_conv, a_alog, a_dt_bias, a_norm_g, a_w_out, b_w_in, b_gate_w2, b_gate_b,
           b_norm_g, b_w_out, ln1_g, ln1_b, mlp_w1, mlp_w2, ln2_g, ln2_b):
    bn, seq, dm = x.shape
    depth = ln1_g.shape[0]
    alpha = (2 * depth) ** 0.25
    xf = x.astype(F32).reshape(bn * seq, dm)
    xb = xf.astype(BF16)
    for i in range(depth):
        j = i // 2
        xb3 = xb.reshape(bn, seq, dm)
        if i % 2 == 0:
            o = _gdn_mixer(xb3, a_w_in[j], a_conv[j], a_alog[j], a_dt_bias[j], a_norm_g[j])
            w_out = a_w_out[j]
        else:
            o = _gla_mixer(xb3, b_w_in[j], b_gate_w2[j], b_gate_b[j], b_norm_g[j])
            w_out = b_w_out[j]
        xf, xb = _post(o.reshape(bn * seq, -1), xf, w_out, mlp_w1[i], mlp_w2[i],
                       ln1_g[i], ln1_b[i], ln2_g[i], ln2_b[i], alpha)
    return xf.reshape(bn, seq, dm).astype(x.dtype)
```

```python
import functools
import math

import numpy as np

import jax
import jax.numpy as jnp
from jax import lax
from jax.experimental import pallas as pl
from jax.experimental.pallas import tpu as pltpu

F32 = jnp.float32
BF16 = jnp.bfloat16

A_HEADS, A_DK, A_DV, A_CONV = 8, 128, 128, 5
B_HEADS, B_DK, B_DV, B_RANK, B_TAU = 4, 128, 256, 16, 16.0
LN_EPS, RMS_EPS, L2_EPS = 1e-5, 1e-6, 1e-6

CH = 128
N_LEVELS = 7
HALO = 8
GATE_ROWS = 16
SOLVE_LEVELS_PER_STAGE = 1
PREP_SLICES = 8
GLA_GROUP = 4
GLA_AHEAD = 2
GLA_PIECES = 2
FIN_GROUP = 4
NEG_BIG = -1e30
LOG2E = math.log2(math.e)
VMEM_LIMIT = 56 * 1024 * 1024
ROW_TILE = 512
FF_TILE = 1024

assert CH == A_DK == B_DK and 2 ** N_LEVELS == CH


def _dot(a, b):
    return jnp.dot(a, b, preferred_element_type=F32)


def _dot_nt(a, b):
    return lax.dot_general(a, b, (((1,), (1,)), ((), ())), preferred_element_type=F32)


def _dot_tn(a, b):
    return lax.dot_general(a, b, (((0,), (0,)), ((), ())), preferred_element_type=F32)


def _split(x, n, axis=1):
    pieces = []
    for _ in range(n - 1):
        p = x.astype(BF16)
        pieces.append(p)
        x = x - p.astype(F32)
    pieces.append(x.astype(BF16))
    return jnp.concatenate(pieces, axis=axis)


def _fold(y, n, axis=1):
    w = y.shape[axis] // n
    blocks = [lax.slice_in_dim(y, i * w, (i + 1) * w, axis=axis) for i in range(n)]
    out = blocks[0]
    for b in blocks[1:]:
        out = out + b
    return out


def _dot_exact(m01, x):
    return _fold(_dot(m01, _split(x, 3)), 3)


def _sigmoid(x):
    return 0.5 + 0.5 * jnp.tanh(0.5 * x)


def _silu(x):
    h = 0.5 * x
    return h + h * jnp.tanh(h)


def _softplus(x):
    return jnp.maximum(x, 0.0) + jnp.log(1.0 + jnp.exp(-jnp.abs(x)))


def _layernorm(y, g, b):
    mu = jnp.mean(y, axis=-1, keepdims=True)
    yc = y - mu
    var = jnp.mean(yc * yc, axis=-1, keepdims=True)
    return yc * lax.rsqrt(var + LN_EPS) * g + b


def _order_masks(rev):
    row = lax.broadcasted_iota(jnp.int32, (CH, CH), 0)
    col = lax.broadcasted_iota(jnp.int32, (CH, CH), 1)
    if rev:
        return col >= row, col > row
    return col <= row, col < row


def _group(n, want):
    return math.gcd(n, want)


def _gdn_kernel(xb_ref, wh_ref, cw_ref, hp_ref, ng_ref, lm_ref, sh_ref, o_ref,
                p_ref, a_ref, t_ref, qk_ref, rhs_ref, qd_ref, kd_ref, gl_ref,
                mc_ref, qc_ref, rc_ref, oc_ref, *, seq):
    nc = seq // CH
    spare = 2 * nc

    p_ref[0:HALO, :] = jnp.zeros((HALO, p_ref.shape[1]), F32)
    p_ref[HALO + seq:, :] = jnp.zeros((HALO, p_ref.shape[1]), F32)
    p_ref[HALO:HALO + seq, :] = _dot(xb_ref[0], wh_ref[0])
    for ref in (a_ref, t_ref, qk_ref, kd_ref, rhs_ref, qd_ref):
        ref[spare] = jnp.zeros(ref.shape[1:], ref.dtype)

    cw = cw_ref[0]
    hp = hp_ref[0]
    zero16 = jnp.zeros((CH, CH), BF16)

    def pair(f, b):
        return jnp.concatenate([jnp.concatenate([f, zero16], axis=1),
                                jnp.concatenate([zero16, b], axis=1)], axis=0)

    def tiles(it, lag):
        c = it - lag
        ok = jnp.logical_and(c >= 0, c < nc)
        return jnp.where(ok, c, spare), jnp.where(ok, nc + c, spare)

    def prep(c):
        base = pl.multiple_of(c * CH, CH)
        win = p_ref[pl.ds(base, CH + 2 * HALO), 0:3 * A_DK]
        shifted = _dot(sh_ref[...], win.astype(BF16))
        mid = A_CONV // 2
        acc = win[HALO:HALO + CH, :] * cw[mid:mid + 1, :]
        yield
        for n, i in enumerate([i for i in range(A_CONV) if i != mid]):
            acc = acc + shifted[n * CH:(n + 1) * CH, :] * cw[i:i + 1, :]
        yield
        s = _silu(acc)
        q = s[:, 0:A_DK]
        k = s[:, A_DK:2 * A_DK]
        v = s[:, 2 * A_DK:3 * A_DK]
        q = q * (lax.rsqrt(jnp.sum(q * q, axis=-1, keepdims=True) + L2_EPS) * (A_DK ** -0.5))
        k = k * lax.rsqrt(jnp.sum(k * k, axis=-1, keepdims=True) + L2_EPS)
        gates = p_ref[pl.ds(base + HALO, CH), 4 * A_DK:5 * A_DK].T[0:GATE_ROWS, :]
        kb16 = k.astype(BF16)
        eye = (lax.broadcasted_iota(jnp.int32, (CH, CH), 0)
               == lax.broadcasted_iota(jnp.int32, (CH, CH), 1)).astype(F32)
        yield
        dirs = ((0, False), (1, True))
        beta_rows = _sigmoid(gates)
        g_rows = -hp[0] * _softplus(gates + hp[1])
        pieces = _split(g_rows, 3, axis=0)
        gr = [jnp.broadcast_to(
            _fold(_dot(pieces, _order_masks(not rev)[0].astype(BF16)), 3, axis=0)[2 + d:3 + d, :], (CH, CH))
            for d, rev in dirs]
        gc = [x.T for x in gr]
        beta = [jnp.broadcast_to(beta_rows[d:d + 1, :], (CH, CH)).T for d, _ in dirs]
        yield
        kbeta = [k * beta[d] for d, _ in dirs]
        kq = [_dot_nt(jnp.concatenate([kbeta[d], q], axis=0).astype(BF16), kb16) for d, _ in dirs]
        gtot = [gc[d][0:1, :] if rev else gc[d][CH - 1:CH, :] for d, rev in dirs]
        eg = [jnp.exp2(gc[d]) for d, _ in dirs]
        yield
        dmat = [jnp.exp2(jnp.where(_order_masks(rev)[0], gc[d] - gr[d], NEG_BIG)) for d, rev in dirs]
        yield
        for d, rev in dirs:
            idx = d * nc + c
            a = jnp.where(_order_masks(rev)[1], kq[d][0:CH] * dmat[d], 0.0).astype(BF16)
            a_ref[idx] = a
            t_ref[idx] = (eye - (a * lm_ref[0]).astype(F32)).astype(BF16)
            qk_ref[idx] = (kq[d][CH:] * dmat[d]).astype(BF16)
        yield
        for d, rev in dirs:
            idx = d * nc + c
            rhs_ref[idx] = jnp.concatenate([v * beta[d], kbeta[d] * eg[d]], axis=1).astype(BF16)
            qd_ref[idx] = q * eg[d]
            kd_ref[idx] = (k * jnp.exp2(gtot[d] - gc[d])).astype(BF16)
            gl_ref[idx] = jnp.broadcast_to(jnp.exp2(gtot[d]), (8, A_DK))
        yield

    per_stage = SOLVE_LEVELS_PER_STAGE
    n_stages = (N_LEVELS - 1) // per_stage
    solve_stages = tuple((1 + s, tuple(range(1 + s * per_stage, 1 + (s + 1) * per_stage)))
                         for s in range(n_stages))
    ops_lag = n_stages + 1
    fills = -(-PREP_SLICES // (2 * per_stage))

    def step(it, with_prep):
        filler = prep(it) if with_prep else iter(())

        def fill(n):
            for _ in range(n):
                next(filler, None)

        loaded = []
        for lag, lvs in solve_stages:
            jf, jb = tiles(it, lag)
            loaded.append((jf, jb, t_ref[jf], t_ref[jb], a_ref[jf], a_ref[jb]))
        ops_in = [(i, t_ref[i], rhs_ref[i], kd_ref[i], qk_ref[i], qd_ref[i]) for i in tiles(it, ops_lag)]
        uws = [_dot(t, rhs) for i, t, rhs, kd, qkm, qd in ops_in]
        for half in range(per_stage):
            xs = []
            for (lag, lvs), (jf, jb, tf, tb, af, ab) in zip(solve_stages, loaded):
                m = lm_ref[lvs[half]]
                xs.append(_dot(jnp.concatenate([tf, tb], axis=1), pair(af * m, ab * m)))
            fill(fills)
            ys = [_dot(x.astype(BF16), pair(tf, tb)) for x, (jf, jb, tf, tb, af, ab) in zip(xs, loaded)]
            if half == 0:
                uws = [uw.astype(BF16) for uw in uws]
                kuws = [_dot_tn(kd, uw) for uw, (i, t, rhs, kd, qkm, qd) in zip(uws, ops_in)]
                quws = [_dot(qkm, uw) for uw, (i, t, rhs, kd, qkm, qd) in zip(uws, ops_in)]
            fill(fills)
            loaded = [(jf, jb, tf - y[:, 0:CH].astype(BF16), tb - y[:, CH:].astype(BF16), af, ab)
                      for y, (jf, jb, tf, tb, af, ab) in zip(ys, loaded)]
        for _ in filler:
            pass
        for jf, jb, tf, tb, _, _ in loaded:
            t_ref[jf] = tf
            t_ref[jb] = tb
        for kuw, quw, (i, t, rhs, kd, qkm, qd) in zip(kuws, quws, ops_in):
            qc_ref[i] = kuw[:, 0:A_DV]
            mc_ref[i] = kuw[:, A_DV:].astype(BF16)
            oc_ref[i] = quw[:, 0:A_DV]
            rc_ref[i] = (qd - quw[:, A_DV:]).astype(BF16)

    def main(it, carry):
        step(it, True)
        return carry

    def drain(it, carry):
        step(it, False)
        return carry

    lax.fori_loop(0, nc, main, 0)
    lax.fori_loop(nc, nc + ops_lag, drain, 0)

    def scan(c, carry):
        sf, sb = carry
        cf = c
        cb = 2 * nc - 1 - c
        of = _dot(rc_ref[cf], sf.astype(BF16)) + oc_ref[cf]
        ob = _dot(rc_ref[cb], sb.astype(BF16)) + oc_ref[cb]
        oc_ref[cf] = of
        oc_ref[cb] = ob
        sf = gl_ref[cf][0:1, :] * sf - _dot(mc_ref[cf], sf.astype(BF16)) + qc_ref[cf]
        sb = gl_ref[cb][0:1, :] * sb - _dot(mc_ref[cb], sb.astype(BF16)) + qc_ref[cb]
        return sf, sb

    zero = jnp.zeros((A_DK, A_DV), F32)
    lax.fori_loop(0, nc, scan, (zero, zero))

    ng = ng_ref[...]
    fg = _group(nc, FIN_GROUP)

    def fin(it, carry):
        cs = [it * fg + u for u in range(fg)]
        rows = [pl.ds(pl.multiple_of(c * CH, CH), CH) for c in cs]
        os_ = [oc_ref[c] + oc_ref[nc + c] for c in cs]
        inv = [lax.rsqrt(jnp.mean(o * o, axis=-1, keepdims=True) + RMS_EPS) for o in os_]
        for c, r, o, s in zip(cs, rows, os_, inv):
            z = p_ref[pl.ds(pl.multiple_of(c * CH, CH) + HALO, CH), 3 * A_DK:4 * A_DK]
            o_ref[0, r, :] = (o * s * ng * _silu(z)).astype(BF16)
        return carry

    lax.fori_loop(0, nc // fg, fin, 0)


def _gdn_level_masks():
    idx = np.arange(CH)
    x = idx[:, None] ^ idx[None, :]
    return jnp.asarray(np.stack([(x >> lv) == 1 for lv in range(N_LEVELS)]), BF16)


def _conv_shift_matrices():
    t = np.arange(CH)[:, None]
    r = np.arange(CH + 2 * HALO)[None, :]
    taps = [i for i in range(A_CONV) if i != A_CONV // 2]
    return jnp.asarray(np.concatenate([r == t + HALO + i - A_CONV // 2 for i in taps], axis=0), BF16)


def _gdn_mixer(xb, w_in, conv_w, a_log, dt_bias, norm_g):
    bn, seq, dm = xb.shape
    h, dk = A_HEADS, A_DK
    nc = seq // CH
    w = w_in
    hw = h * dk
    ba = w[:, 4 * hw:].reshape(dm, 2, 2, h)
    per_head = [w[:, i * hw:(i + 1) * hw].reshape(dm, h, dk).transpose(1, 0, 2) for i in range(4)]
    gate_cols = jnp.pad(ba.reshape(dm, 4, h).transpose(2, 0, 1), ((0, 0), (0, 0), (0, dk - 4)))
    wh = jnp.concatenate(per_head + [gate_cols], axis=2).astype(BF16)
    cw = conv_w.reshape(A_CONV, 3, h, dk).transpose(2, 0, 1, 3).reshape(h, A_CONV, 3 * dk)
    cw = jnp.pad(cw, ((0, 0), (0, 8 - A_CONV), (0, 0))).astype(F32)
    scale = jnp.zeros((h, GATE_ROWS), F32).at[:, 2:4].set((jnp.exp(a_log.astype(F32)) * LOG2E).T)
    bias = jnp.zeros((h, GATE_ROWS), F32).at[:, 2:4].set(dt_bias.astype(F32).T)
    hp = jnp.broadcast_to(jnp.stack([scale, bias], axis=1)[:, :, :, None], (h, 2, GATE_ROWS, dk))
    ng = norm_g.astype(F32).reshape(1, A_DV)
    lm = _gdn_level_masks()
    sh = _conv_shift_matrices()
    nw = wh.shape[2]

    kern = functools.partial(_gdn_kernel, seq=seq)
    tile = lambda dt: pltpu.VMEM((2 * nc + 1, CH, CH), dt)
    return pl.pallas_call(
        kern,
        out_shape=jax.ShapeDtypeStruct((bn, seq, h * A_DV), BF16),
        grid=(bn, h),
        in_specs=[
            pl.BlockSpec((1, seq, dm), lambda b, i: (b, 0, 0)),
            pl.BlockSpec((1, dm, nw), lambda b, i: (i, 0, 0)),
            pl.BlockSpec((1, 8, 3 * dk), lambda b, i: (i, 0, 0)),
            pl.BlockSpec((1, 2, GATE_ROWS, dk), lambda b, i: (i, 0, 0, 0)),
            pl.BlockSpec((1, A_DV), lambda b, i: (0, 0)),
            pl.BlockSpec(lm.shape, lambda b, i: (0, 0, 0)),
            pl.BlockSpec(sh.shape, lambda b, i: (0, 0)),
        ],
        out_specs=pl.BlockSpec((1, seq, A_DV), lambda b, i: (b, 0, i)),
        scratch_shapes=[
            pltpu.VMEM((seq + 2 * HALO, nw), F32),
            tile(BF16),
            tile(BF16),
            tile(BF16),
            pltpu.VMEM((2 * nc + 1, CH, A_DV + A_DK), BF16),
            tile(F32),
            tile(BF16),
            pltpu.VMEM((2 * nc, 8, A_DK), F32),
            tile(BF16),
            tile(F32),
            tile(BF16),
            tile(F32),
        ],
        compiler_params=pltpu.CompilerParams(
            dimension_semantics=("arbitrary", "arbitrary"), vmem_limit_bytes=VMEM_LIMIT),
        name="gdn_mixer",
    )(xb, wh, cw, hp, ng, lm, sh)


def _gla_tables():
    i = np.arange(CH)[:, None]
    t = np.arange(CH)[None, :]
    seg = np.zeros((2, N_LEVELS + 1, CH, CH), np.float32)
    lvl = np.zeros((2, CH, CH), np.int32)
    for d in range(2):
        rev = d == 1
        seg[d, 0] = (t >= i) if rev else (t <= i)
        lv = np.full((CH, CH), N_LEVELS + 1, np.int32)
        lv[np.arange(CH), np.arange(CH)] = N_LEVELS
        x = i ^ t
        for l in range(N_LEVELS):
            h = 2 ** (N_LEVELS - 1 - l)
            b0 = (i // (2 * h)) * (2 * h)
            if rev:
                r = b0 + h
                late = i < r
                m = np.where(late, (t >= i) & (t < r), (t >= r) & (t < i))
                own = ((x >> (N_LEVELS - 1 - l)) == 1) & (t > i)
            else:
                r = b0 + h - 1
                late = i > r
                m = np.where(late, (t > r) & (t <= i), (t > i) & (t <= r))
                own = ((x >> (N_LEVELS - 1 - l)) == 1) & (t < i)
            seg[d, 1 + l] = m
            lv[own] = l
        lvl[d] = lv
    return jnp.asarray(seg, BF16), jnp.asarray(lvl)


def _gla_kernel(xb_ref, wh_ref, w2_ref, gb_ref, ng_ref, seg_ref, lvl_ref, o_ref,
                p_ref, qs_ref, kv_ref, st_ref, dec_ref, oi_ref, *, seq):
    nc = seq // CH
    dk, dv = B_DK, B_DV
    p_ref[...] = _dot(xb_ref[0], wh_ref[0])

    c_q, c_k, c_v, c_r, c_g = 0, dk, 2 * dk, 2 * dk + dv, 2 * dk + 2 * dv
    gg = _group(nc, GLA_GROUP)
    lanes = [(u, d) for u in range(gg) for d in range(2)]

    def prep(it, carry):
        cs = [it * gg + u for u in range(gg)]
        rows = [pl.ds(pl.multiple_of(c * CH, CH), CH) for c in cs]
        q = [p_ref[r, c_q:c_q + dk] * (dk ** -0.5) for r in rows]
        k = [p_ref[r, c_k:c_k + dk] for r in rows]
        q16 = [x.astype(BF16) for x in q]
        k16 = [x.astype(BF16) for x in k]
        v16 = [p_ref[r, c_v:c_v + dv].astype(BF16) for r in rows]
        gin = [p_ref[r, c_g:c_g + dk].astype(BF16) for r in rows]
        logit = [_dot(gin[u], w2_ref[0, d]) + gb_ref[0, d][0:1, :] for u, d in lanes]
        la3 = [_split(-_softplus(-x) * (LOG2E / B_TAU), 3) for x in logit]
        la2 = [y[:, 0:GLA_PIECES * dk] for y in la3]
        bc = [_fold(_dot(seg_ref[d, 0], y), 3) for (u, d), y in zip(lanes, la3)]

        def level_sums(l):
            h = CH >> (l + 1)
            if h < HALO:
                return [_fold(_dot(seg_ref[d, 1 + l], y), GLA_PIECES) for (u, d), y in zip(lanes, la2)]
            out = []
            for (u, d), b in zip(lanes, bc):
                blocks = []
                for lo in range(0, CH, 2 * h):
                    if d == 1:
                        ref = b[lo + h:lo + h + 1, :]
                        blocks += [b[lo:lo + h, :] - ref, ref - b[lo + h:lo + 2 * h, :]]
                    else:
                        ref = b[lo + h - 1:lo + h, :]
                        blocks += [ref - b[lo:lo + h, :], b[lo + h:lo + 2 * h, :] - ref]
                out.append(jnp.concatenate(blocks, axis=0))
            return out

        half = CH // 2
        zero_half = jnp.zeros((half, dk), BF16)

        def top_level(x, e, d, late):
            upper = (d == 1) != late
            rows = slice(half, CH) if upper else slice(0, half)
            kept = (x[rows] * e[rows]).astype(BF16)
            return jnp.concatenate([zero_half, kept] if upper else [kept, zero_half], axis=0)

        ahead = [level_sums(l) for l in range(GLA_AHEAD)]
        scores = prod = None
        for l in range(N_LEVELS):
            if l + GLA_AHEAD < N_LEVELS:
                ahead.append(level_sums(l + GLA_AHEAD))
            e = [jnp.exp2(x) for x in ahead[l]]
            if l == 0:
                ql = [top_level(q[u], x, d, True) for (u, d), x in zip(lanes, e)]
                kl = [top_level(k[u], x, d, False) for (u, d), x in zip(lanes, e)]
            else:
                ql = [(q[u] * x).astype(BF16) for (u, d), x in zip(lanes, e)]
                kl = [(k[u] * x).astype(BF16) for (u, d), x in zip(lanes, e)]
            if l == 1:
                scores = prod
            elif l > 1:
                own = [lvl_ref[d] == l - 1 for d in range(2)]
                scores = [jnp.where(own[d], p, s) for (u, d), p, s in zip(lanes, prod, scores)]
            prod = [_dot_nt(a, b) for a, b in zip(ql, kl)]
        own = [lvl_ref[d] == N_LEVELS - 1 for d in range(2)]
        scores = [jnp.where(own[d], p, s) for (u, d), p, s in zip(lanes, prod, scores)]
        diag = [_dot_nt(q16[u], k16[u]) for u in range(gg)]
        own = [lvl_ref[d] == N_LEVELS for d in range(2)]
        scores = [jnp.where(own[d], diag[u], s) for (u, d), s in zip(lanes, scores)]
        for (u, d), s, b in zip(lanes, scores, bc):
            idx = d * nc + cs[u]
            btot = b[0:1, :] if d == 1 else b[CH - 1:CH, :]
            oi_ref[idx] = _dot(s.astype(BF16), v16[u])
            qs_ref[idx] = (q[u] * jnp.exp2(b)).astype(BF16)
            kv_ref[idx] = _dot_tn((k[u] * jnp.exp2(btot - b)).astype(BF16), v16[u])
            dec_ref[idx] = jnp.exp2(jnp.broadcast_to(btot, (CH, dk)).T)
        return carry

    lax.fori_loop(0, nc // gg, prep, 0)

    def scan(c, carry):
        sf, sb = carry
        cf = c
        ib = 2 * nc - 1 - c
        st_ref[cf] = sf.astype(BF16)
        st_ref[ib] = sb.astype(BF16)
        ef = dec_ref[cf]
        eb = dec_ref[ib]
        sf = sf * jnp.concatenate([ef, ef], axis=1) + kv_ref[cf]
        sb = sb * jnp.concatenate([eb, eb], axis=1) + kv_ref[ib]
        return sf, sb

    zero = jnp.zeros((dk, dv), F32)
    lax.fori_loop(0, nc, scan, (zero, zero))

    ng = ng_ref[...]
    fg = _group(nc, FIN_GROUP)

    def fin(it, carry):
        cs = [it * fg + u for u in range(fg)]
        rows = [pl.ds(pl.multiple_of(c * CH, CH), CH) for c in cs]
        os_ = [oi_ref[c] + oi_ref[nc + c] + _dot(qs_ref[c], st_ref[c]) + _dot(qs_ref[nc + c], st_ref[nc + c])
               for c in cs]
        inv = [lax.rsqrt(jnp.mean(o * o, axis=-1, keepdims=True) + RMS_EPS) for o in os_]
        for r, o, s in zip(rows, os_, inv):
            o_ref[0, r, :] = (o * s * ng * _silu(p_ref[r, c_r:c_r + dv])).astype(BF16)
        return carry

    lax.fori_loop(0, nc // fg, fin, 0)


def _gla_mixer(xb, w_in, gate_w2, gate_b, norm_g):
    bn, seq, dm = xb.shape
    h, dk, dv = B_HEADS, B_DK, B_DV
    nc = seq // CH
    kw, vw = h * dk, h * dv
    w = w_in

    def heads(cols, width):
        return cols.reshape(dm, h, width).transpose(1, 0, 2)

    gl = jnp.pad(w[:, 2 * kw + 2 * vw:], ((0, 0), (0, dk - 2 * B_RANK)))
    wh = jnp.concatenate([
        heads(w[:, 0:kw], dk), heads(w[:, kw:2 * kw], dk),
        heads(w[:, 2 * kw:2 * kw + vw], dv), heads(w[:, 2 * kw + vw:2 * kw + 2 * vw], dv),
        jnp.broadcast_to(gl[None], (h, dm, dk))], axis=2).astype(BF16)
    w2 = gate_w2.reshape(2, B_RANK, h, dk).transpose(2, 0, 1, 3)
    w2p = jnp.zeros((h, 2, dk, dk), F32)
    w2p = w2p.at[:, 0, 0:B_RANK].set(w2[:, 0]).at[:, 1, B_RANK:2 * B_RANK].set(w2[:, 1]).astype(BF16)
    gb = gate_b.reshape(2, h, dk).transpose(1, 0, 2).astype(F32)
    gb = jnp.broadcast_to(gb[:, :, None, :], (h, 2, 8, dk))
    ng = norm_g.astype(F32).reshape(1, dv)
    seg, lvl = _gla_tables()
    nw = wh.shape[2]

    kern = functools.partial(_gla_kernel, seq=seq)
    return pl.pallas_call(
        kern,
        out_shape=jax.ShapeDtypeStruct((bn, seq, vw), BF16),
        grid=(bn, h),
        in_specs=[
            pl.BlockSpec((1, seq, dm), lambda b, i: (b, 0, 0)),
            pl.BlockSpec((1, dm, nw), lambda b, i: (i, 0, 0)),
            pl.BlockSpec((1, 2, dk, dk), lambda b, i: (i, 0, 0, 0)),
            pl.BlockSpec((1, 2, 8, dk), lambda b, i: (i, 0, 0, 0)),
            pl.BlockSpec((1, dv), lambda b, i: (0, 0)),
            pl.BlockSpec(seg.shape, lambda b, i: (0, 0, 0, 0)),
            pl.BlockSpec(lvl.shape, lambda b, i: (0, 0, 0)),
        ],
        out_specs=pl.BlockSpec((1, seq, dv), lambda b, i: (b, 0, i)),
        scratch_shapes=[
            pltpu.VMEM((seq, nw), F32),
            pltpu.VMEM((2 * nc, CH, dk), BF16),
            pltpu.VMEM((2 * nc, dk, dv), F32),
            pltpu.VMEM((2 * nc, dk, dv), BF16),
            pltpu.VMEM((2 * nc, dk, dk), F32),
            pltpu.VMEM((2 * nc, CH, dv), F32),
        ],
        compiler_params=pltpu.CompilerParams(
            dimension_semantics=("arbitrary", "arbitrary"), vmem_limit_bytes=VMEM_LIMIT),
        name="gla_mixer",
    )(xb, wh, w2p, gb, ng, seg, lvl)


def _post_kernel(o_ref, x_ref, wo_ref, w1_ref, w2_ref, ln_ref, y_ref, yb_ref, *, alpha):
    ln = ln_ref[...]
    x = x_ref[...]
    x1 = _layernorm(alpha * x + _dot(o_ref[...], wo_ref[...]), ln[0:1, :], ln[1:2, :])
    x1b = x1.astype(BF16)
    acc = jnp.zeros(x.shape, F32)
    dff = w1_ref.shape[1]
    for j in range(dff // FF_TILE):
        cols = slice(j * FF_TILE, (j + 1) * FF_TILE)
        hcur = jnp.maximum(_dot(x1b, w1_ref[:, cols]), 0.0)
        acc = acc + _dot((hcur * hcur).astype(BF16), w2_ref[cols, :])
    y = _layernorm(alpha * x1 + acc, ln[2:3, :], ln[3:4, :])
    y_ref[...] = y
    yb_ref[...] = y.astype(BF16)


def _post(o, x, w_out, w1, w2, g1, b1, g2, b2, alpha):
    t, dm = x.shape
    vw = o.shape[1]
    dff = w1.shape[1]
    tm = min(ROW_TILE, t)
    ln = jnp.pad(jnp.stack([g1, b1, g2, b2]).astype(F32), ((0, 4), (0, 0)))
    const = lambda shape: pl.BlockSpec(shape, lambda i: (0, 0), pipeline_mode=pl.Buffered(1))
    return pl.pallas_call(
        functools.partial(_post_kernel, alpha=alpha),
        out_shape=(jax.ShapeDtypeStruct((t, dm), F32), jax.ShapeDtypeStruct((t, dm), BF16)),
        grid=(t // tm,),
        in_specs=[
            pl.BlockSpec((tm, vw), lambda i: (i, 0)),
            pl.BlockSpec((tm, dm), lambda i: (i, 0)),
            const((vw, dm)), const((dm, dff)), const((dff, dm)), const((8, dm)),
        ],
        out_specs=(pl.BlockSpec((tm, dm), lambda i: (i, 0)), pl.BlockSpec((tm, dm), lambda i: (i, 0))),
        compiler_params=pltpu.CompilerParams(
            dimension_semantics=("arbitrary",), vmem_limit_bytes=VMEM_LIMIT),
        name="post",
    )(o, x, w_out.astype(BF16), w1.astype(BF16), w2.astype(BF16), ln)


def kernel(x, a_w_in, a_conv, a_alog, a_dt_bias, a_norm_g, a_w_out, b_w_in, b_gate_w2, b_gate_b,
           b_norm_g, b_w_out, ln1_g, ln1_b, mlp_w1, mlp_w2, ln2_g, ln2_b):
    bn, seq, dm = x.shape
    depth = ln1_g.shape[0]
    alpha = (2 * depth) ** 0.25
    xf = x.astype(F32).reshape(bn * seq, dm)
    xb = xf.astype(BF16)
    for i in range(depth):
        j = i // 2
        xb3 = xb.reshape(bn, seq, dm)
        if i % 2 == 0:
            o = _gdn_mixer(xb3, a_w_in[j], a_conv[j], a_alog[j], a_dt_bias[j], a_norm_g[j])
            w_out = a_w_out[j]
        else:
            o = _gla_mixer(xb3, b_w_in[j], b_gate_w2[j], b_gate_b[j], b_norm_g[j])
            w_out = b_w_out[j]
        xf, xb = _post(o.reshape(bn * seq, -1), xf, w_out, mlp_w1[i], mlp_w2[i],
                       ln1_g[i], ln1_b[i], ln2_g[i], ln2_b[i], alpha)
    return xf.reshape(bn, seq, dm).astype(x.dtype)
```

```python
import functools
import math

import numpy as np

import jax
import jax.numpy as jnp
from jax import lax
from jax.experimental import pallas as pl
from jax.experimental.pallas import tpu as pltpu

F32 = jnp.float32
BF16 = jnp.bfloat16

A_HEADS, A_DK, A_DV, A_CONV = 8, 128, 128, 5
B_HEADS, B_DK, B_DV, B_RANK, B_TAU = 4, 128, 256, 16, 16.0
LN_EPS, RMS_EPS, L2_EPS = 1e-5, 1e-6, 1e-6

CH = 128
N_LEVELS = 7
HALO = 8
GATE_ROWS = 16
SOLVE_LEVELS_PER_STAGE = 1
PREP_SLICES = 8
GLA_GROUP = 2
GLA_AHEAD = 2
GLA_PIECES = 2
FIN_GROUP = 4
NEG_BIG = -1e30
LOG2E = math.log2(math.e)
VMEM_LIMIT = 56 * 1024 * 1024
ROW_TILE = 512
FF_TILE = 1024

assert CH == A_DK == B_DK and 2 ** N_LEVELS == CH


def _dot(a, b):
    return jnp.dot(a, b, preferred_element_type=F32)


def _dot_nt(a, b):
    return lax.dot_general(a, b, (((1,), (1,)), ((), ())), preferred_element_type=F32)


def _dot_tn(a, b):
    return lax.dot_general(a, b, (((0,), (0,)), ((), ())), preferred_element_type=F32)


def _split(x, n, axis=1):
    pieces = []
    for _ in range(n - 1):
        p = x.astype(BF16)
        pieces.append(p)
        x = x - p.astype(F32)
    pieces.append(x.astype(BF16))
    return jnp.concatenate(pieces, axis=axis)


def _fold(y, n, axis=1):
    w = y.shape[axis] // n
    blocks = [lax.slice_in_dim(y, i * w, (i + 1) * w, axis=axis) for i in range(n)]
    out = blocks[0]
    for b in blocks[1:]:
        out = out + b
    return out


def _dot_exact(m01, x):
    return _fold(_dot(m01, _split(x, 3)), 3)


def _sigmoid(x):
    return 0.5 + 0.5 * jnp.tanh(0.5 * x)


def _silu(x):
    h = 0.5 * x
    return h + h * jnp.tanh(h)


def _softplus(x):
    return jnp.maximum(x, 0.0) + jnp.log(1.0 + jnp.exp(-jnp.abs(x)))


def _layernorm(y, g, b):
    mu = jnp.mean(y, axis=-1, keepdims=True)
    yc = y - mu
    var = jnp.mean(yc * yc, axis=-1, keepdims=True)
    return yc * lax.rsqrt(var + LN_EPS) * g + b


def _order_masks(rev):
    row = lax.broadcasted_iota(jnp.int32, (CH, CH), 0)
    col = lax.broadcasted_iota(jnp.int32, (CH, CH), 1)
    if rev:
        return col >= row, col > row
    return col <= row, col < row


def _group(n, want):
    return math.gcd(n, want)


def _gdn_kernel(xb_ref, wh_ref, cw_ref, hp_ref, ng_ref, lm_ref, sh_ref, o_ref,
                p_ref, a_ref, t_ref, qk_ref, rhs_ref, qd_ref, kd_ref, gl_ref,
                mc_ref, qc_ref, rc_ref, oc_ref, *, seq):
    nc = seq // CH
    spare = 2 * nc

    p_ref[0:HALO, :] = jnp.zeros((HALO, p_ref.shape[1]), F32)
    p_ref[HALO + seq:, :] = jnp.zeros((HALO, p_ref.shape[1]), F32)
    p_ref[HALO:HALO + seq, :] = _dot(xb_ref[0], wh_ref[0])
    for ref in (a_ref, t_ref, qk_ref, kd_ref, rhs_ref, qd_ref):
        ref[spare] = jnp.zeros(ref.shape[1:], ref.dtype)

    cw = cw_ref[0]
    hp = hp_ref[0]
    zero16 = jnp.zeros((CH, CH), BF16)

    def pair(f, b):
        return jnp.concatenate([jnp.concatenate([f, zero16], axis=1),
                                jnp.concatenate([zero16, b], axis=1)], axis=0)

    def tiles(it, lag):
        c = it - lag
        ok = jnp.logical_and(c >= 0, c < nc)
        return jnp.where(ok, c, spare), jnp.where(ok, nc + c, spare)

    def prep(c):
        base = pl.multiple_of(c * CH, CH)
        win = p_ref[pl.ds(base, CH + 2 * HALO), 0:3 * A_DK]
        shifted = _dot(sh_ref[...], win.astype(BF16))
        mid = A_CONV // 2
        acc = win[HALO:HALO + CH, :] * cw[mid:mid + 1, :]
        yield
        for n, i in enumerate([i for i in range(A_CONV) if i != mid]):
            acc = acc + shifted[n * CH:(n + 1) * CH, :] * cw[i:i + 1, :]
        yield
        s = _silu(acc)
        q = s[:, 0:A_DK]
        k = s[:, A_DK:2 * A_DK]
        v = s[:, 2 * A_DK:3 * A_DK]
        q = q * (lax.rsqrt(jnp.sum(q * q, axis=-1, keepdims=True) + L2_EPS) * (A_DK ** -0.5))
        k = k * lax.rsqrt(jnp.sum(k * k, axis=-1, keepdims=True) + L2_EPS)
        gates = p_ref[pl.ds(base + HALO, CH), 4 * A_DK:5 * A_DK].T[0:GATE_ROWS, :]
        kb16 = k.astype(BF16)
        eye = (lax.broadcasted_iota(jnp.int32, (CH, CH), 0)
               == lax.broadcasted_iota(jnp.int32, (CH, CH), 1)).astype(F32)
        yield
        dirs = ((0, False), (1, True))
        beta_rows = _sigmoid(gates)
        g_rows = -hp[0] * _softplus(gates + hp[1])
        pieces = _split(g_rows, 3, axis=0)
        gr = [jnp.broadcast_to(
            _fold(_dot(pieces, _order_masks(not rev)[0].astype(BF16)), 3, axis=0)[2 + d:3 + d, :], (CH, CH))
            for d, rev in dirs]
        gc = [x.T for x in gr]
        beta = [jnp.broadcast_to(beta_rows[d:d + 1, :], (CH, CH)).T for d, _ in dirs]
        yield
        kbeta = [k * beta[d] for d, _ in dirs]
        kq = [_dot_nt(jnp.concatenate([kbeta[d], q], axis=0).astype(BF16), kb16) for d, _ in dirs]
        gtot = [gc[d][0:1, :] if rev else gc[d][CH - 1:CH, :] for d, rev in dirs]
        eg = [jnp.exp2(gc[d]) for d, _ in dirs]
        yield
        dmat = [jnp.exp2(jnp.where(_order_masks(rev)[0], gc[d] - gr[d], NEG_BIG)) for d, rev in dirs]
        yield
        for d, rev in dirs:
            idx = d * nc + c
            a = jnp.where(_order_masks(rev)[1], kq[d][0:CH] * dmat[d], 0.0).astype(BF16)
            a_ref[idx] = a
            t_ref[idx] = (eye - (a * lm_ref[0]).astype(F32)).astype(BF16)
            qk_ref[idx] = (kq[d][CH:] * dmat[d]).astype(BF16)
        yield
        for d, rev in dirs:
            idx = d * nc + c
            rhs_ref[idx] = jnp.concatenate([v * beta[d], kbeta[d] * eg[d]], axis=1).astype(BF16)
            qd_ref[idx] = q * eg[d]
            kd_ref[idx] = (k * jnp.exp2(gtot[d] - gc[d])).astype(BF16)
            gl_ref[idx] = jnp.broadcast_to(jnp.exp2(gtot[d]), (8, A_DK))
        yield

    per_stage = SOLVE_LEVELS_PER_STAGE
    n_stages = (N_LEVELS - 1) // per_stage
    solve_stages = tuple((1 + s, tuple(range(1 + s * per_stage, 1 + (s + 1) * per_stage)))
                         for s in range(n_stages))
    ops_lag = n_stages + 1
    fills = -(-PREP_SLICES // (2 * per_stage))

    def step(it, with_prep):
        filler = prep(it) if with_prep else iter(())

        def fill(n):
            for _ in range(n):
                next(filler, None)

        loaded = []
        for lag, lvs in solve_stages:
            jf, jb = tiles(it, lag)
            loaded.append((jf, jb, t_ref[jf], t_ref[jb], a_ref[jf], a_ref[jb]))
        ops_in = [(i, t_ref[i], rhs_ref[i], kd_ref[i], qk_ref[i], qd_ref[i]) for i in tiles(it, ops_lag)]
        uws = [_dot(t, rhs) for i, t, rhs, kd, qkm, qd in ops_in]
        for half in range(per_stage):
            xs = []
            for (lag, lvs), (jf, jb, tf, tb, af, ab) in zip(solve_stages, loaded):
                m = lm_ref[lvs[half]]
                xs.append(_dot(jnp.concatenate([tf, tb], axis=1), pair(af * m, ab * m)))
            fill(fills)
            ys = [_dot(x.astype(BF16), pair(tf, tb)) for x, (jf, jb, tf, tb, af, ab) in zip(xs, loaded)]
            if half == 0:
                uws = [uw.astype(BF16) for uw in uws]
                kuws = [_dot_tn(kd, uw) for uw, (i, t, rhs, kd, qkm, qd) in zip(uws, ops_in)]
                quws = [_dot(qkm, uw) for uw, (i, t, rhs, kd, qkm, qd) in zip(uws, ops_in)]
            fill(fills)
            loaded = [(jf, jb, tf - y[:, 0:CH].astype(BF16), tb - y[:, CH:].astype(BF16), af, ab)
                      for y, (jf, jb, tf, tb, af, ab) in zip(ys, loaded)]
        for _ in filler:
            pass
        for jf, jb, tf, tb, _, _ in loaded:
            t_ref[jf] = tf
            t_ref[jb] = tb
        for kuw, quw, (i, t, rhs, kd, qkm, qd) in zip(kuws, quws, ops_in):
            qc_ref[i] = kuw[:, 0:A_DV]
            mc_ref[i] = kuw[:, A_DV:].astype(BF16)
            oc_ref[i] = quw[:, 0:A_DV]
            rc_ref[i] = (qd - quw[:, A_DV:]).astype(BF16)

    def main(it, carry):
        step(it, True)
        return carry

    def drain(it, carry):
        step(it, False)
        return carry

    lax.fori_loop(0, nc, main, 0)
    lax.fori_loop(nc, nc + ops_lag, drain, 0)

    def scan(c, carry):
        sf, sb = carry
        cf = c
        cb = 2 * nc - 1 - c
        of = _dot(rc_ref[cf], sf.astype(BF16)) + oc_ref[cf]
        ob = _dot(rc_ref[cb], sb.astype(BF16)) + oc_ref[cb]
        oc_ref[cf] = of
        oc_ref[cb] = ob
        sf = gl_ref[cf][0:1, :] * sf - _dot(mc_ref[cf], sf.astype(BF16)) + qc_ref[cf]
        sb = gl_ref[cb][0:1, :] * sb - _dot(mc_ref[cb], sb.astype(BF16)) + qc_ref[cb]
        return sf, sb

    zero = jnp.zeros((A_DK, A_DV), F32)
    lax.fori_loop(0, nc, scan, (zero, zero))

    ng = ng_ref[...]
    fg = _group(nc, FIN_GROUP)

    def fin(it, carry):
        cs = [it * fg + u for u in range(fg)]
        rows = [pl.ds(pl.multiple_of(c * CH, CH), CH) for c in cs]
        os_ = [oc_ref[c] + oc_ref[nc + c] for c in cs]
        inv = [lax.rsqrt(jnp.mean(o * o, axis=-1, keepdims=True) + RMS_EPS) for o in os_]
        for c, r, o, s in zip(cs, rows, os_, inv):
            z = p_ref[pl.ds(pl.multiple_of(c * CH, CH) + HALO, CH), 3 * A_DK:4 * A_DK]
            o_ref[0, r, :] = (o * s * ng * _silu(z)).astype(BF16)
        return carry

    lax.fori_loop(0, nc // fg, fin, 0)


def _gdn_level_masks():
    idx = np.arange(CH)
    x = idx[:, None] ^ idx[None, :]
    return jnp.asarray(np.stack([(x >> lv) == 1 for lv in range(N_LEVELS)]), BF16)


def _conv_shift_matrices():
    t = np.arange(CH)[:, None]
    r = np.arange(CH + 2 * HALO)[None, :]
    taps = [i for i in range(A_CONV) if i != A_CONV // 2]
    return jnp.asarray(np.concatenate([r == t + HALO + i - A_CONV // 2 for i in taps], axis=0), BF16)


def _gdn_mixer(xb, w_in, conv_w, a_log, dt_bias, norm_g):
    bn, seq, dm = xb.shape
    h, dk = A_HEADS, A_DK
    nc = seq // CH
    w = w_in
    hw = h * dk
    ba = w[:, 4 * hw:].reshape(dm, 2, 2, h)
    per_head = [w[:, i * hw:(i + 1) * hw].reshape(dm, h, dk).transpose(1, 0, 2) for i in range(4)]
    gate_cols = jnp.pad(ba.reshape(dm, 4, h).transpose(2, 0, 1), ((0, 0), (0, 0), (0, dk - 4)))
    wh = jnp.concatenate(per_head + [gate_cols], axis=2).astype(BF16)
    cw = conv_w.reshape(A_CONV, 3, h, dk).transpose(2, 0, 1, 3).reshape(h, A_CONV, 3 * dk)
    cw = jnp.pad(cw, ((0, 0), (0, 8 - A_CONV), (0, 0))).astype(F32)
    scale = jnp.zeros((h, GATE_ROWS), F32).at[:, 2:4].set((jnp.exp(a_log.astype(F32)) * LOG2E).T)
    bias = jnp.zeros((h, GATE_ROWS), F32).at[:, 2:4].set(dt_bias.astype(F32).T)
    hp = jnp.broadcast_to(jnp.stack([scale, bias], axis=1)[:, :, :, None], (h, 2, GATE_ROWS, dk))
    ng = norm_g.astype(F32).reshape(1, A_DV)
    lm = _gdn_level_masks()
    sh = _conv_shift_matrices()
    nw = wh.shape[2]

    kern = functools.partial(_gdn_kernel, seq=seq)
    tile = lambda dt: pltpu.VMEM((2 * nc + 1, CH, CH), dt)
    return pl.pallas_call(
        kern,
        out_shape=jax.ShapeDtypeStruct((bn, seq, h * A_DV), BF16),
        grid=(bn, h),
        in_specs=[
            pl.BlockSpec((1, seq, dm), lambda b, i: (b, 0, 0)),
            pl.BlockSpec((1, dm, nw), lambda b, i: (i, 0, 0)),
            pl.BlockSpec((1, 8, 3 * dk), lambda b, i: (i, 0, 0)),
            pl.BlockSpec((1, 2, GATE_ROWS, dk), lambda b, i: (i, 0, 0, 0)),
            pl.BlockSpec((1, A_DV), lambda b, i: (0, 0)),
            pl.BlockSpec(lm.shape, lambda b, i: (0, 0, 0)),
            pl.BlockSpec(sh.shape, lambda b, i: (0, 0)),
        ],
        out_specs=pl.BlockSpec((1, seq, A_DV), lambda b, i: (b, 0, i)),
        scratch_shapes=[
            pltpu.VMEM((seq + 2 * HALO, nw), F32),
            tile(BF16),
            tile(BF16),
            tile(BF16),
            pltpu.VMEM((2 * nc + 1, CH, A_DV + A_DK), BF16),
            tile(F32),
            tile(BF16),
            pltpu.VMEM((2 * nc, 8, A_DK), F32),
            tile(BF16),
            tile(F32),
            tile(BF16),
            tile(F32),
        ],
        compiler_params=pltpu.CompilerParams(
            dimension_semantics=("arbitrary", "arbitrary"), vmem_limit_bytes=VMEM_LIMIT),
        name="gdn_mixer",
    )(xb, wh, cw, hp, ng, lm, sh)


def _gla_tables():
    i = np.arange(CH)[:, None]
    t = np.arange(CH)[None, :]
    seg = np.zeros((2, N_LEVELS + 1, CH, CH), np.float32)
    lvl = np.zeros((2, CH, CH), np.int32)
    for d in range(2):
        rev = d == 1
        seg[d, 0] = (t >= i) if rev else (t <= i)
        lv = np.full((CH, CH), N_LEVELS + 1, np.int32)
        lv[np.arange(CH), np.arange(CH)] = N_LEVELS
        x = i ^ t
        for l in range(N_LEVELS):
            h = 2 ** (N_LEVELS - 1 - l)
            b0 = (i // (2 * h)) * (2 * h)
            if rev:
                r = b0 + h
                late = i < r
                m = np.where(late, (t >= i) & (t < r), (t >= r) & (t < i))
                own = ((x >> (N_LEVELS - 1 - l)) == 1) & (t > i)
            else:
                r = b0 + h - 1
                late = i > r
                m = np.where(late, (t > r) & (t <= i), (t > i) & (t <= r))
                own = ((x >> (N_LEVELS - 1 - l)) == 1) & (t < i)
            seg[d, 1 + l] = m
            lv[own] = l
        lvl[d] = lv
    return jnp.asarray(seg, BF16), jnp.asarray(lvl)


def _gla_kernel(xb_ref, wh_ref, w2_ref, gb_ref, ng_ref, seg_ref, lvl_ref, o_ref,
                p_ref, qs_ref, kv_ref, st_ref, dec_ref, oi_ref, *, seq):
    nc = seq // CH
    dk, dv = B_DK, B_DV
    p_ref[...] = _dot(xb_ref[0], wh_ref[0])

    c_q, c_k, c_v, c_r, c_g = 0, dk, 2 * dk, 2 * dk + dv, 2 * dk + 2 * dv
    gg = _group(nc, GLA_GROUP)
    lanes = [(u, d) for u in range(gg) for d in range(2)]

    def prep(it, carry):
        cs = [it * gg + u for u in range(gg)]
        rows = [pl.ds(pl.multiple_of(c * CH, CH), CH) for c in cs]
        q = [p_ref[r, c_q:c_q + dk] * (dk ** -0.5) for r in rows]
        k = [p_ref[r, c_k:c_k + dk] for r in rows]
        q16 = [x.astype(BF16) for x in q]
        k16 = [x.astype(BF16) for x in k]
        v16 = [p_ref[r, c_v:c_v + dv].astype(BF16) for r in rows]
        gin = [p_ref[r, c_g:c_g + dk].astype(BF16) for r in rows]
        logit = [_dot(gin[u], w2_ref[0, d]) + gb_ref[0, d][0:1, :] for u, d in lanes]
        la3 = [_split(-_softplus(-x) * (LOG2E / B_TAU), 3) for x in logit]
        la2 = [y[:, 0:GLA_PIECES * dk] for y in la3]
        bc = [_fold(_dot(seg_ref[d, 0], y), 3) for (u, d), y in zip(lanes, la3)]

        def level_sums(l):
            h = CH >> (l + 1)
            if h < HALO:
                return [_fold(_dot(seg_ref[d, 1 + l], y), GLA_PIECES) for (u, d), y in zip(lanes, la2)]
            out = []
            for (u, d), b in zip(lanes, bc):
                blocks = []
                for lo in range(0, CH, 2 * h):
                    if d == 1:
                        ref = b[lo + h:lo + h + 1, :]
                        blocks += [b[lo:lo + h, :] - ref, ref - b[lo + h:lo + 2 * h, :]]
                    else:
                        ref = b[lo + h - 1:lo + h, :]
                        blocks += [ref - b[lo:lo + h, :], b[lo + h:lo + 2 * h, :] - ref]
                out.append(jnp.concatenate(blocks, axis=0))
            return out

        half = CH // 2
        zero_half = jnp.zeros((half, dk), BF16)

        def top_level(x, e, d, late):
            upper = (d == 1) != late
            rows = slice(half, CH) if upper else slice(0, half)
            kept = (x[rows] * e[rows]).astype(BF16)
            return jnp.concatenate([zero_half, kept] if upper else [kept, zero_half], axis=0)

        ahead = [level_sums(l) for l in range(GLA_AHEAD)]
        scores = prod = None
        for l in range(N_LEVELS):
            if l + GLA_AHEAD < N_LEVELS:
                ahead.append(level_sums(l + GLA_AHEAD))
            e = [jnp.exp2(x) for x in ahead[l]]
            if l == 0:
                ql = [top_level(q[u], x, d, True) for (u, d), x in zip(lanes, e)]
                kl = [top_level(k[u], x, d, False) for (u, d), x in zip(lanes, e)]
            else:
                ql = [(q[u] * x).astype(BF16) for (u, d), x in zip(lanes, e)]
                kl = [(k[u] * x).astype(BF16) for (u, d), x in zip(lanes, e)]
            if l == 1:
                scores = prod
            elif l > 1:
                own = [lvl_ref[d] == l - 1 for d in range(2)]
                scores = [jnp.where(own[d], p, s) for (u, d), p, s in zip(lanes, prod, scores)]
            prod = [_dot_nt(a, b) for a, b in zip(ql, kl)]
        own = [lvl_ref[d] == N_LEVELS - 1 for d in range(2)]
        scores = [jnp.where(own[d], p, s) for (u, d), p, s in zip(lanes, prod, scores)]
        diag = [_dot_nt(q16[u], k16[u]) for u in range(gg)]
        own = [lvl_ref[d] == N_LEVELS for d in range(2)]
        scores = [jnp.where(own[d], diag[u], s) for (u, d), s in zip(lanes, scores)]
        for (u, d), s, b in zip(lanes, scores, bc):
            idx = d * nc + cs[u]
            btot = b[0:1, :] if d == 1 else b[CH - 1:CH, :]
            oi_ref[idx] = _dot(s.astype(BF16), v16[u])
            qs_ref[idx] = (q[u] * jnp.exp2(b)).astype(BF16)
            kv_ref[idx] = _dot_tn((k[u] * jnp.exp2(btot - b)).astype(BF16), v16[u])
            dec_ref[idx] = jnp.exp2(jnp.broadcast_to(btot, (CH, dk)).T)
        return carry

    lax.fori_loop(0, nc // gg, prep, 0)

    def scan(c, carry):
        sf, sb = carry
        cf = c
        ib = 2 * nc - 1 - c
        st_ref[cf] = sf.astype(BF16)
        st_ref[ib] = sb.astype(BF16)
        ef = dec_ref[cf]
        eb = dec_ref[ib]
        sf = sf * jnp.concatenate([ef, ef], axis=1) + kv_ref[cf]
        sb = sb * jnp.concatenate([eb, eb], axis=1) + kv_ref[ib]
        return sf, sb

    zero = jnp.zeros((dk, dv), F32)
    lax.fori_loop(0, nc, scan, (zero, zero))

    ng = ng_ref[...]
    fg = _group(nc, FIN_GROUP)

    def fin(it, carry):
        cs = [it * fg + u for u in range(fg)]
        rows = [pl.ds(pl.multiple_of(c * CH, CH), CH) for c in cs]
        os_ = [oi_ref[c] + oi_ref[nc + c] + _dot(qs_ref[c], st_ref[c]) + _dot(qs_ref[nc + c], st_ref[nc + c])
               for c in cs]
        inv = [lax.rsqrt(jnp.mean(o * o, axis=-1, keepdims=True) + RMS_EPS) for o in os_]
        for r, o, s in zip(rows, os_, inv):
            o_ref[0, r, :] = (o * s * ng * _silu(p_ref[r, c_r:c_r + dv])).astype(BF16)
        return carry

    lax.fori_loop(0, nc // fg, fin, 0)


def _gla_mixer(xb, w_in, gate_w2, gate_b, norm_g):
    bn, seq, dm = xb.shape
    h, dk, dv = B_HEADS, B_DK, B_DV
    nc = seq // CH
    kw, vw = h * dk, h * dv
    w = w_in

    def heads(cols, width):
        return cols.reshape(dm, h, width).transpose(1, 0, 2)

    gl = jnp.pad(w[:, 2 * kw + 2 * vw:], ((0, 0), (0, dk - 2 * B_RANK)))
    wh = jnp.concatenate([
        heads(w[:, 0:kw], dk), heads(w[:, kw:2 * kw], dk),
        heads(w[:, 2 * kw:2 * kw + vw], dv), heads(w[:, 2 * kw + vw:2 * kw + 2 * vw], dv),
        jnp.broadcast_to(gl[None], (h, dm, dk))], axis=2).astype(BF16)
    w2 = gate_w2.reshape(2, B_RANK, h, dk).transpose(2, 0, 1, 3)
    w2p = jnp.zeros((h, 2, dk, dk), F32)
    w2p = w2p.at[:, 0, 0:B_RANK].set(w2[:, 0]).at[:, 1, B_RANK:2 * B_RANK].set(w2[:, 1]).astype(BF16)
    gb = gate_b.reshape(2, h, dk).transpose(1, 0, 2).astype(F32)
    gb = jnp.broadcast_to(gb[:, :, None, :], (h, 2, 8, dk))
    ng = norm_g.astype(F32).reshape(1, dv)
    seg, lvl = _gla_tables()
    nw = wh.shape[2]

    kern = functools.partial(_gla_kernel, seq=seq)
    return pl.pallas_call(
        kern,
        out_shape=jax.ShapeDtypeStruct((bn, seq, vw), BF16),
        grid=(bn, h),
        in_specs=[
            pl.BlockSpec((1, seq, dm), lambda b, i: (b, 0, 0)),
            pl.BlockSpec((1, dm, nw), lambda b, i: (i, 0, 0)),
            pl.BlockSpec((1, 2, dk, dk), lambda b, i: (i, 0, 0, 0)),
            pl.BlockSpec((1, 2, 8, dk), lambda b, i: (i, 0, 0, 0)),
            pl.BlockSpec((1, dv), lambda b, i: (0, 0)),
            pl.BlockSpec(seg.shape, lambda b, i: (0, 0, 0, 0)),
            pl.BlockSpec(lvl.shape, lambda b, i: (0, 0, 0)),
        ],
        out_specs=pl.BlockSpec((1, seq, dv), lambda b, i: (b, 0, i)),
        scratch_shapes=[
            pltpu.VMEM((seq, nw), F32),
            pltpu.VMEM((2 * nc, CH, dk), BF16),
            pltpu.VMEM((2 * nc, dk, dv), F32),
            pltpu.VMEM((2 * nc, dk, dv), BF16),
            pltpu.VMEM((2 * nc, dk, dk), F32),
            pltpu.VMEM((2 * nc, CH, dv), F32),
        ],
        compiler_params=pltpu.CompilerParams(
            dimension_semantics=("arbitrary", "arbitrary"), vmem_limit_bytes=VMEM_LIMIT),
        name="gla_mixer",
    )(xb, wh, w2p, gb, ng, seg, lvl)


def _post_kernel(o_ref, x_ref, wo_ref, w1_ref, w2_ref, ln_ref, y_ref, yb_ref, *, alpha):
    ln = ln_ref[...]
    x = x_ref[...]
    x1 = _layernorm(alpha * x + _dot(o_ref[...], wo_ref[...]), ln[0:1, :], ln[1:2, :])
    x1b = x1.astype(BF16)
    acc = jnp.zeros(x.shape, F32)
    dff = w1_ref.shape[1]
    for j in range(dff // FF_TILE):
        cols = slice(j * FF_TILE, (j + 1) * FF_TILE)
        hcur = jnp.maximum(_dot(x1b, w1_ref[:, cols]), 0.0)
        acc = acc + _dot((hcur * hcur).astype(BF16), w2_ref[cols, :])
    y = _layernorm(alpha * x1 + acc, ln[2:3, :], ln[3:4, :])
    y_ref[...] = y
    yb_ref[...] = y.astype(BF16)


def _post(o, x, w_out, w1, w2, g1, b1, g2, b2, alpha):
    t, dm = x.shape
    vw = o.shape[1]
    dff = w1.shape[1]
    tm = min(ROW_TILE, t)
    ln = jnp.pad(jnp.stack([g1, b1, g2, b2]).astype(F32), ((0, 4), (0, 0)))
    const = lambda shape: pl.BlockSpec(shape, lambda i: (0, 0), pipeline_mode=pl.Buffered(1))
    return pl.pallas_call(
        functools.partial(_post_kernel, alpha=alpha),
        out_shape=(jax.ShapeDtypeStruct((t, dm), F32), jax.ShapeDtypeStruct((t, dm), BF16)),
        grid=(t // tm,),
        in_specs=[
            pl.BlockSpec((tm, vw), lambda i: (i, 0)),
            pl.BlockSpec((tm, dm), lambda i: (i, 0)),
            const((vw, dm)), const((dm, dff)), const((dff, dm)), const((8, dm)),
        ],
        out_specs=(pl.BlockSpec((tm, dm), lambda i: (i, 0)), pl.BlockSpec((tm, dm), lambda i: (i, 0))),
        compiler_params=pltpu.CompilerParams(
            dimension_semantics=("arbitrary",), vmem_limit_bytes=VMEM_LIMIT),
        name="post",
    )(o, x, w_out.astype(BF16), w1.astype(BF16), w2.astype(BF16), ln)


def kernel(x, a_w_in, a_conv, a_alog, a_dt_bias, a_norm_g, a_w_out, b_w_in, b_gate_w2, b_gate_b,
           b_norm_g, b_w_out, ln1_g, ln1_b, mlp_w1, mlp_w2, ln2_g, ln2_b):
    bn, seq, dm = x.shape
    depth = ln1_g.shape[0]
    alpha = (2 * depth) ** 0.25
    xf = x.astype(F32).reshape(bn * seq, dm)
    xb = xf.astype(BF16)
    for i in range(depth):
        j = i // 2
        xb3 = xb.reshape(bn, seq, dm)
        if i % 2 == 0:
            o = _gdn_mixer(xb3, a_w_in[j], a_conv[j], a_alog[j], a_dt_bias[j], a_norm_g[j])
            w_out = a_w_out[j]
        else:
            o = _gla_mixer(xb3, b_w_in[j], b_gate_w2[j], b_gate_b[j], b_norm_g[j])
            w_out = b_w_out[j]
        xf, xb = _post(o.reshape(bn * seq, -1), xf, w_out, mlp_w1[i], mlp_w2[i],
                       ln1_g[i], ln1_b[i], ln2_g[i], ln2_b[i], alpha)
    return xf.reshape(bn, seq, dm).astype(x.dtype)
```

```python
import functools
import math

import numpy as np

import jax
import jax.numpy as jnp
from jax import lax
from jax.experimental import pallas as pl
from jax.experimental.pallas import tpu as pltpu

F32 = jnp.float32
BF16 = jnp.bfloat16

A_HEADS, A_DK, A_DV, A_CONV = 8, 128, 128, 5
B_HEADS, B_DK, B_DV, B_RANK, B_TAU = 4, 128, 256, 16, 16.0
LN_EPS, RMS_EPS, L2_EPS = 1e-5, 1e-6, 1e-6

CH = 128
N_LEVELS = 7
HALO = 8
GATE_ROWS = 16
SOLVE_LEVELS_PER_STAGE = 1
PREP_SLICES = 8
GLA_GROUP = 4
GLA_AHEAD = 2
GLA_PIECES = 2
FIN_GROUP = 4
NEG_BIG = -1e30
LOG2E = math.log2(math.e)
VMEM_LIMIT = 56 * 1024 * 1024
ROW_TILE = 1024
FF_TILE = 1024

assert CH == A_DK == B_DK and 2 ** N_LEVELS == CH


def _dot(a, b):
    return jnp.dot(a, b, preferred_element_type=F32)


def _dot_nt(a, b):
    return lax.dot_general(a, b, (((1,), (1,)), ((), ())), preferred_element_type=F32)


def _dot_tn(a, b):
    return lax.dot_general(a, b, (((0,), (0,)), ((), ())), preferred_element_type=F32)


def _split(x, n, axis=1):
    pieces = []
    for _ in range(n - 1):
        p = x.astype(BF16)
        pieces.append(p)
        x = x - p.astype(F32)
    pieces.append(x.astype(BF16))
    return jnp.concatenate(pieces, axis=axis)


def _fold(y, n, axis=1):
    w = y.shape[axis] // n
    blocks = [lax.slice_in_dim(y, i * w, (i + 1) * w, axis=axis) for i in range(n)]
    out = blocks[0]
    for b in blocks[1:]:
        out = out + b
    return out


def _dot_exact(m01, x):
    return _fold(_dot(m01, _split(x, 3)), 3)


def _sigmoid(x):
    return 0.5 + 0.5 * jnp.tanh(0.5 * x)


def _silu(x):
    h = 0.5 * x
    return h + h * jnp.tanh(h)


def _softplus(x):
    return jnp.maximum(x, 0.0) + jnp.log(1.0 + jnp.exp(-jnp.abs(x)))


def _layernorm(y, g, b):
    mu = jnp.mean(y, axis=-1, keepdims=True)
    yc = y - mu
    var = jnp.mean(yc * yc, axis=-1, keepdims=True)
    return yc * lax.rsqrt(var + LN_EPS) * g + b


def _order_masks(rev):
    row = lax.broadcasted_iota(jnp.int32, (CH, CH), 0)
    col = lax.broadcasted_iota(jnp.int32, (CH, CH), 1)
    if rev:
        return col >= row, col > row
    return col <= row, col < row


def _group(n, want):
    return math.gcd(n, want)


def _gdn_kernel(xb_ref, wh_ref, cw_ref, hp_ref, ng_ref, lm_ref, sh_ref, o_ref,
                p_ref, a_ref, t_ref, qk_ref, rhs_ref, qd_ref, kd_ref, gl_ref,
                mc_ref, qc_ref, rc_ref, oc_ref, *, seq):
    nc = seq // CH
    spare = 2 * nc

    p_ref[0:HALO, :] = jnp.zeros((HALO, p_ref.shape[1]), F32)
    p_ref[HALO + seq:, :] = jnp.zeros((HALO, p_ref.shape[1]), F32)
    p_ref[HALO:HALO + seq, :] = _dot(xb_ref[0], wh_ref[0])
    for ref in (a_ref, t_ref, qk_ref, kd_ref, rhs_ref, qd_ref):
        ref[spare] = jnp.zeros(ref.shape[1:], ref.dtype)

    cw = cw_ref[0]
    hp = hp_ref[0]
    zero16 = jnp.zeros((CH, CH), BF16)

    def pair(f, b):
        return jnp.concatenate([jnp.concatenate([f, zero16], axis=1),
                                jnp.concatenate([zero16, b], axis=1)], axis=0)

    def tiles(it, lag):
        c = it - lag
        ok = jnp.logical_and(c >= 0, c < nc)
        return jnp.where(ok, c, spare), jnp.where(ok, nc + c, spare)

    def prep(c):
        base = pl.multiple_of(c * CH, CH)
        win = p_ref[pl.ds(base, CH + 2 * HALO), 0:3 * A_DK]
        shifted = _dot(sh_ref[...], win.astype(BF16))
        mid = A_CONV // 2
        acc = win[HALO:HALO + CH, :] * cw[mid:mid + 1, :]
        yield
        for n, i in enumerate([i for i in range(A_CONV) if i != mid]):
            acc = acc + shifted[n * CH:(n + 1) * CH, :] * cw[i:i + 1, :]
        yield
        s = _silu(acc)
        q = s[:, 0:A_DK]
        k = s[:, A_DK:2 * A_DK]
        v = s[:, 2 * A_DK:3 * A_DK]
        q = q * (lax.rsqrt(jnp.sum(q * q, axis=-1, keepdims=True) + L2_EPS) * (A_DK ** -0.5))
        k = k * lax.rsqrt(jnp.sum(k * k, axis=-1, keepdims=True) + L2_EPS)
        gates = p_ref[pl.ds(base + HALO, CH), 4 * A_DK:5 * A_DK].T[0:GATE_ROWS, :]
        kb16 = k.astype(BF16)
        eye = (lax.broadcasted_iota(jnp.int32, (CH, CH), 0)
               == lax.broadcasted_iota(jnp.int32, (CH, CH), 1)).astype(F32)
        yield
        dirs = ((0, False), (1, True))
        beta_rows = _sigmoid(gates)
        g_rows = -hp[0] * _softplus(gates + hp[1])
        pieces = _split(g_rows, 3, axis=0)
        gr = [jnp.broadcast_to(
            _fold(_dot(pieces, _order_masks(not rev)[0].astype(BF16)), 3, axis=0)[2 + d:3 + d, :], (CH, CH))
            for d, rev in dirs]
        gc = [x.T for x in gr]
        beta = [jnp.broadcast_to(beta_rows[d:d + 1, :], (CH, CH)).T for d, _ in dirs]
        yield
        kbeta = [k * beta[d] for d, _ in dirs]
        kq = [_dot_nt(jnp.concatenate([kbeta[d], q], axis=0).astype(BF16), kb16) for d, _ in dirs]
        gtot = [gc[d][0:1, :] if rev else gc[d][CH - 1:CH, :] for d, rev in dirs]
        eg = [jnp.exp2(gc[d]) for d, _ in dirs]
        yield
        dmat = [jnp.exp2(jnp.where(_order_masks(rev)[0], gc[d] - gr[d], NEG_BIG)) for d, rev in dirs]
        yield
        for d, rev in dirs:
            idx = d * nc + c
            a = jnp.where(_order_masks(rev)[1], kq[d][0:CH] * dmat[d], 0.0).astype(BF16)
            a_ref[idx] = a
            t_ref[idx] = (eye - (a * lm_ref[0]).astype(F32)).astype(BF16)
            qk_ref[idx] = (kq[d][CH:] * dmat[d]).astype(BF16)
        yield
        for d, rev in dirs:
            idx = d * nc + c
            rhs_ref[idx] = jnp.concatenate([v * beta[d], kbeta[d] * eg[d]], axis=1).astype(BF16)
            qd_ref[idx] = q * eg[d]
            kd_ref[idx] = (k * jnp.exp2(gtot[d] - gc[d])).astype(BF16)
            gl_ref[idx] = jnp.broadcast_to(jnp.exp2(gtot[d]), (8, A_DK))
        yield

    per_stage = SOLVE_LEVELS_PER_STAGE
    n_stages = (N_LEVELS - 1) // per_stage
    solve_stages = tuple((1 + s, tuple(range(1 + s * per_stage, 1 + (s + 1) * per_stage)))
                         for s in range(n_stages))
    ops_lag = n_stages + 1
    fills = -(-PREP_SLICES // (2 * per_stage))

    def step(it, with_prep):
        filler = prep(it) if with_prep else iter(())

        def fill(n):
            for _ in range(n):
                next(filler, None)

        loaded = []
        for lag, lvs in solve_stages:
            jf, jb = tiles(it, lag)
            loaded.append((jf, jb, t_ref[jf], t_ref[jb], a_ref[jf], a_ref[jb]))
        ops_in = [(i, t_ref[i], rhs_ref[i], kd_ref[i], qk_ref[i], qd_ref[i]) for i in tiles(it, ops_lag)]
        uws = [_dot(t, rhs) for i, t, rhs, kd, qkm, qd in ops_in]
        for half in range(per_stage):
            xs = []
            for (lag, lvs), (jf, jb, tf, tb, af, ab) in zip(solve_stages, loaded):
                m = lm_ref[lvs[half]]
                xs.append(_dot(jnp.concatenate([tf, tb], axis=1), pair(af * m, ab * m)))
            fill(fills)
            ys = [_dot(x.astype(BF16), pair(tf, tb)) for x, (jf, jb, tf, tb, af, ab) in zip(xs, loaded)]
            if half == 0:
                uws = [uw.astype(BF16) for uw in uws]
                kuws = [_dot_tn(kd, uw) for uw, (i, t, rhs, kd, qkm, qd) in zip(uws, ops_in)]
                quws = [_dot(qkm, uw) for uw, (i, t, rhs, kd, qkm, qd) in zip(uws, ops_in)]
            fill(fills)
            loaded = [(jf, jb, tf - y[:, 0:CH].astype(BF16), tb - y[:, CH:].astype(BF16), af, ab)
                      for y, (jf, jb, tf, tb, af, ab) in zip(ys, loaded)]
        for _ in filler:
            pass
        for jf, jb, tf, tb, _, _ in loaded:
            t_ref[jf] = tf
            t_ref[jb] = tb
        for kuw, quw, (i, t, rhs, kd, qkm, qd) in zip(kuws, quws, ops_in):
            qc_ref[i] = kuw[:, 0:A_DV]
            mc_ref[i] = kuw[:, A_DV:].astype(BF16)
            oc_ref[i] = quw[:, 0:A_DV]
            rc_ref[i] = (qd - quw[:, A_DV:]).astype(BF16)

    def main(it, carry):
        step(it, True)
        return carry

    def drain(it, carry):
        step(it, False)
        return carry

    lax.fori_loop(0, nc, main, 0)
    lax.fori_loop(nc, nc + ops_lag, drain, 0)

    def scan(c, carry):
        sf, sb = carry
        cf = c
        cb = 2 * nc - 1 - c
        of = _dot(rc_ref[cf], sf.astype(BF16)) + oc_ref[cf]
        ob = _dot(rc_ref[cb], sb.astype(BF16)) + oc_ref[cb]
        oc_ref[cf] = of
        oc_ref[cb] = ob
        sf = gl_ref[cf][0:1, :] * sf - _dot(mc_ref[cf], sf.astype(BF16)) + qc_ref[cf]
        sb = gl_ref[cb][0:1, :] * sb - _dot(mc_ref[cb], sb.astype(BF16)) + qc_ref[cb]
        return sf, sb

    zero = jnp.zeros((A_DK, A_DV), F32)
    lax.fori_loop(0, nc, scan, (zero, zero))

    ng = ng_ref[...]
    fg = _group(nc, FIN_GROUP)

    def fin(it, carry):
        cs = [it * fg + u for u in range(fg)]
        rows = [pl.ds(pl.multiple_of(c * CH, CH), CH) for c in cs]
        os_ = [oc_ref[c] + oc_ref[nc + c] for c in cs]
        inv = [lax.rsqrt(jnp.mean(o * o, axis=-1, keepdims=True) + RMS_EPS) for o in os_]
        for c, r, o, s in zip(cs, rows, os_, inv):
            z = p_ref[pl.ds(pl.multiple_of(c * CH, CH) + HALO, CH), 3 * A_DK:4 * A_DK]
            o_ref[0, r, :] = (o * s * ng * _silu(z)).astype(BF16)
        return carry

    lax.fori_loop(0, nc // fg, fin, 0)


def _gdn_level_masks():
    idx = np.arange(CH)
    x = idx[:, None] ^ idx[None, :]
    return jnp.asarray(np.stack([(x >> lv) == 1 for lv in range(N_LEVELS)]), BF16)


def _conv_shift_matrices():
    t = np.arange(CH)[:, None]
    r = np.arange(CH + 2 * HALO)[None, :]
    taps = [i for i in range(A_CONV) if i != A_CONV // 2]
    return jnp.asarray(np.concatenate([r == t + HALO + i - A_CONV // 2 for i in taps], axis=0), BF16)


def _gdn_mixer(xb, w_in, conv_w, a_log, dt_bias, norm_g):
    bn, seq, dm = xb.shape
    h, dk = A_HEADS, A_DK
    nc = seq // CH
    w = w_in
    hw = h * dk
    ba = w[:, 4 * hw:].reshape(dm, 2, 2, h)
    per_head = [w[:, i * hw:(i + 1) * hw].reshape(dm, h, dk).transpose(1, 0, 2) for i in range(4)]
    gate_cols = jnp.pad(ba.reshape(dm, 4, h).transpose(2, 0, 1), ((0, 0), (0, 0), (0, dk - 4)))
    wh = jnp.concatenate(per_head + [gate_cols], axis=2).astype(BF16)
    cw = conv_w.reshape(A_CONV, 3, h, dk).transpose(2, 0, 1, 3).reshape(h, A_CONV, 3 * dk)
    cw = jnp.pad(cw, ((0, 0), (0, 8 - A_CONV), (0, 0))).astype(F32)
    scale = jnp.zeros((h, GATE_ROWS), F32).at[:, 2:4].set((jnp.exp(a_log.astype(F32)) * LOG2E).T)
    bias = jnp.zeros((h, GATE_ROWS), F32).at[:, 2:4].set(dt_bias.astype(F32).T)
    hp = jnp.broadcast_to(jnp.stack([scale, bias], axis=1)[:, :, :, None], (h, 2, GATE_ROWS, dk))
    ng = norm_g.astype(F32).reshape(1, A_DV)
    lm = _gdn_level_masks()
    sh = _conv_shift_matrices()
    nw = wh.shape[2]

    kern = functools.partial(_gdn_kernel, seq=seq)
    tile = lambda dt: pltpu.VMEM((2 * nc + 1, CH, CH), dt)
    return pl.pallas_call(
        kern,
        out_shape=jax.ShapeDtypeStruct((bn, seq, h * A_DV), BF16),
        grid=(bn, h),
        in_specs=[
            pl.BlockSpec((1, seq, dm), lambda b, i: (b, 0, 0)),
            pl.BlockSpec((1, dm, nw), lambda b, i: (i, 0, 0)),
            pl.BlockSpec((1, 8, 3 * dk), lambda b, i: (i, 0, 0)),
            pl.BlockSpec((1, 2, GATE_ROWS, dk), lambda b, i: (i, 0, 0, 0)),
            pl.BlockSpec((1, A_DV), lambda b, i: (0, 0)),
            pl.BlockSpec(lm.shape, lambda b, i: (0, 0, 0)),
            pl.BlockSpec(sh.shape, lambda b, i: (0, 0)),
        ],
        out_specs=pl.BlockSpec((1, seq, A_DV), lambda b, i: (b, 0, i)),
        scratch_shapes=[
            pltpu.VMEM((seq + 2 * HALO, nw), F32),
            tile(BF16),
            tile(BF16),
            tile(BF16),
            pltpu.VMEM((2 * nc + 1, CH, A_DV + A_DK), BF16),
            tile(F32),
            tile(BF16),
            pltpu.VMEM((2 * nc, 8, A_DK), F32),
            tile(BF16),
            tile(F32),
            tile(BF16),
            tile(F32),
        ],
        compiler_params=pltpu.CompilerParams(
            dimension_semantics=("arbitrary", "arbitrary"), vmem_limit_bytes=VMEM_LIMIT),
        name="gdn_mixer",
    )(xb, wh, cw, hp, ng, lm, sh)


def _gla_tables():
    i = np.arange(CH)[:, None]
    t = np.arange(CH)[None, :]
    seg = np.zeros((2, N_LEVELS + 1, CH, CH), np.float32)
    lvl = np.zeros((2, CH, CH), np.int32)
    for d in range(2):
        rev = d == 1
        seg[d, 0] = (t >= i) if rev else (t <= i)
        lv = np.full((CH, CH), N_LEVELS + 1, np.int32)
        lv[np.arange(CH), np.arange(CH)] = N_LEVELS
        x = i ^ t
        for l in range(N_LEVELS):
            h = 2 ** (N_LEVELS - 1 - l)
            b0 = (i // (2 * h)) * (2 * h)
            if rev:
                r = b0 + h
                late = i < r
                m = np.where(late, (t >= i) & (t < r), (t >= r) & (t < i))
                own = ((x >> (N_LEVELS - 1 - l)) == 1) & (t > i)
            else:
                r = b0 + h - 1
                late = i > r
                m = np.where(late, (t > r) & (t <= i), (t > i) & (t <= r))
                own = ((x >> (N_LEVELS - 1 - l)) == 1) & (t < i)
            seg[d, 1 + l] = m
            lv[own] = l
        lvl[d] = lv
    return jnp.asarray(seg, BF16), jnp.asarray(lvl)


def _gla_kernel(xb_ref, wh_ref, w2_ref, gb_ref, ng_ref, seg_ref, lvl_ref, o_ref,
                p_ref, qs_ref, kv_ref, st_ref, dec_ref, oi_ref, *, seq):
    nc = seq // CH
    dk, dv = B_DK, B_DV
    p_ref[...] = _dot(xb_ref[0], wh_ref[0])

    c_q, c_k, c_v, c_r, c_g = 0, dk, 2 * dk, 2 * dk + dv, 2 * dk + 2 * dv
    gg = _group(nc, GLA_GROUP)
    lanes = [(u, d) for u in range(gg) for d in range(2)]

    def prep(it, carry):
        cs = [it * gg + u for u in range(gg)]
        rows = [pl.ds(pl.multiple_of(c * CH, CH), CH) for c in cs]
        q = [p_ref[r, c_q:c_q + dk] * (dk ** -0.5) for r in rows]
        k = [p_ref[r, c_k:c_k + dk] for r in rows]
        q16 = [x.astype(BF16) for x in q]
        k16 = [x.astype(BF16) for x in k]
        v16 = [p_ref[r, c_v:c_v + dv].astype(BF16) for r in rows]
        gin = [p_ref[r, c_g:c_g + dk].astype(BF16) for r in rows]
        logit = [_dot(gin[u], w2_ref[0, d]) + gb_ref[0, d][0:1, :] for u, d in lanes]
        la3 = [_split(-_softplus(-x) * (LOG2E / B_TAU), 3) for x in logit]
        la2 = [y[:, 0:GLA_PIECES * dk] for y in la3]
        bc = [_fold(_dot(seg_ref[d, 0], y), 3) for (u, d), y in zip(lanes, la3)]

        def level_sums(l):
            h = CH >> (l + 1)
            if h < HALO:
                return [_fold(_dot(seg_ref[d, 1 + l], y), GLA_PIECES) for (u, d), y in zip(lanes, la2)]
            out = []
            for (u, d), b in zip(lanes, bc):
                blocks = []
                for lo in range(0, CH, 2 * h):
                    if d == 1:
                        ref = b[lo + h:lo + h + 1, :]
                        blocks += [b[lo:lo + h, :] - ref, ref - b[lo + h:lo + 2 * h, :]]
                    else:
                        ref = b[lo + h - 1:lo + h, :]
                        blocks += [ref - b[lo:lo + h, :], b[lo + h:lo + 2 * h, :] - ref]
                out.append(jnp.concatenate(blocks, axis=0))
            return out

        half = CH // 2
        zero_half = jnp.zeros((half, dk), BF16)

        def top_level(x, e, d, late):
            upper = (d == 1) != late
            rows = slice(half, CH) if upper else slice(0, half)
            kept = (x[rows] * e[rows]).astype(BF16)
            return jnp.concatenate([zero_half, kept] if upper else [kept, zero_half], axis=0)

        ahead = [level_sums(l) for l in range(GLA_AHEAD)]
        scores = prod = None
        for l in range(N_LEVELS):
            if l + GLA_AHEAD < N_LEVELS:
                ahead.append(level_sums(l + GLA_AHEAD))
            e = [jnp.exp2(x) for x in ahead[l]]
            if l == 0:
                ql = [top_level(q[u], x, d, True) for (u, d), x in zip(lanes, e)]
                kl = [top_level(k[u], x, d, False) for (u, d), x in zip(lanes, e)]
            else:
                ql = [(q[u] * x).astype(BF16) for (u, d), x in zip(lanes, e)]
                kl = [(k[u] * x).astype(BF16) for (u, d), x in zip(lanes, e)]
            if l == 1:
                scores = prod
            elif l > 1:
                own = [lvl_ref[d] == l - 1 for d in range(2)]
                scores = [jnp.where(own[d], p, s) for (u, d), p, s in zip(lanes, prod, scores)]
            prod = [_dot_nt(a, b) for a, b in zip(ql, kl)]
        own = [lvl_ref[d] == N_LEVELS - 1 for d in range(2)]
        scores = [jnp.where(own[d], p, s) for (u, d), p, s in zip(lanes, prod, scores)]
        diag = [_dot_nt(q16[u], k16[u]) for u in range(gg)]
        own = [lvl_ref[d] == N_LEVELS for d in range(2)]
        scores = [jnp.where(own[d], diag[u], s) for (u, d), s in zip(lanes, scores)]
        for (u, d), s, b in zip(lanes, scores, bc):
            idx = d * nc + cs[u]
            btot = b[0:1, :] if d == 1 else b[CH - 1:CH, :]
            oi_ref[idx] = _dot(s.astype(BF16), v16[u])
            qs_ref[idx] = (q[u] * jnp.exp2(b)).astype(BF16)
            kv_ref[idx] = _dot_tn((k[u] * jnp.exp2(btot - b)).astype(BF16), v16[u])
            dec_ref[idx] = jnp.exp2(jnp.broadcast_to(btot, (CH, dk)).T)
        return carry

    lax.fori_loop(0, nc // gg, prep, 0)

    def scan(c, carry):
        sf, sb = carry
        cf = c
        ib = 2 * nc - 1 - c
        st_ref[cf] = sf.astype(BF16)
        st_ref[ib] = sb.astype(BF16)
        ef = dec_ref[cf]
        eb = dec_ref[ib]
        sf = sf * jnp.concatenate([ef, ef], axis=1) + kv_ref[cf]
        sb = sb * jnp.concatenate([eb, eb], axis=1) + kv_ref[ib]
        return sf, sb

    zero = jnp.zeros((dk, dv), F32)
    lax.fori_loop(0, nc, scan, (zero, zero))

    ng = ng_ref[...]
    fg = _group(nc, FIN_GROUP)

    def fin(it, carry):
        cs = [it * fg + u for u in range(fg)]
        rows = [pl.ds(pl.multiple_of(c * CH, CH), CH) for c in cs]
        os_ = [oi_ref[c] + oi_ref[nc + c] + _dot(qs_ref[c], st_ref[c]) + _dot(qs_ref[nc + c], st_ref[nc + c])
               for c in cs]
        inv = [lax.rsqrt(jnp.mean(o * o, axis=-1, keepdims=True) + RMS_EPS) for o in os_]
        for r, o, s in zip(rows, os_, inv):
            o_ref[0, r, :] = (o * s * ng * _silu(p_ref[r, c_r:c_r + dv])).astype(BF16)
        return carry

    lax.fori_loop(0, nc // fg, fin, 0)


def _gla_mixer(xb, w_in, gate_w2, gate_b, norm_g):
    bn, seq, dm = xb.shape
    h, dk, dv = B_HEADS, B_DK, B_DV
    nc = seq // CH
    kw, vw = h * dk, h * dv
    w = w_in

    def heads(cols, width):
        return cols.reshape(dm, h, width).transpose(1, 0, 2)

    gl = jnp.pad(w[:, 2 * kw + 2 * vw:], ((0, 0), (0, dk - 2 * B_RANK)))
    wh = jnp.concatenate([
        heads(w[:, 0:kw], dk), heads(w[:, kw:2 * kw], dk),
        heads(w[:, 2 * kw:2 * kw + vw], dv), heads(w[:, 2 * kw + vw:2 * kw + 2 * vw], dv),
        jnp.broadcast_to(gl[None], (h, dm, dk))], axis=2).astype(BF16)
    w2 = gate_w2.reshape(2, B_RANK, h, dk).transpose(2, 0, 1, 3)
    w2p = jnp.zeros((h, 2, dk, dk), F32)
    w2p = w2p.at[:, 0, 0:B_RANK].set(w2[:, 0]).at[:, 1, B_RANK:2 * B_RANK].set(w2[:, 1]).astype(BF16)
    gb = gate_b.reshape(2, h, dk).transpose(1, 0, 2).astype(F32)
    gb = jnp.broadcast_to(gb[:, :, None, :], (h, 2, 8, dk))
    ng = norm_g.astype(F32).reshape(1, dv)
    seg, lvl = _gla_tables()
    nw = wh.shape[2]

    kern = functools.partial(_gla_kernel, seq=seq)
    return pl.pallas_call(
        kern,
        out_shape=jax.ShapeDtypeStruct((bn, seq, vw), BF16),
        grid=(bn, h),
        in_specs=[
            pl.BlockSpec((1, seq, dm), lambda b, i: (b, 0, 0)),
            pl.BlockSpec((1, dm, nw), lambda b, i: (i, 0, 0)),
            pl.BlockSpec((1, 2, dk, dk), lambda b, i: (i, 0, 0, 0)),
            pl.BlockSpec((1, 2, 8, dk), lambda b, i: (i, 0, 0, 0)),
            pl.BlockSpec((1, dv), lambda b, i: (0, 0)),
            pl.BlockSpec(seg.shape, lambda b, i: (0, 0, 0, 0)),
            pl.BlockSpec(lvl.shape, lambda b, i: (0, 0, 0)),
        ],
        out_specs=pl.BlockSpec((1, seq, dv), lambda b, i: (b, 0, i)),
        scratch_shapes=[
            pltpu.VMEM((seq, nw), F32),
            pltpu.VMEM((2 * nc, CH, dk), BF16),
            pltpu.VMEM((2 * nc, dk, dv), F32),
            pltpu.VMEM((2 * nc, dk, dv), BF16),
            pltpu.VMEM((2 * nc, dk, dk), F32),
            pltpu.VMEM((2 * nc, CH, dv), F32),
        ],
        compiler_params=pltpu.CompilerParams(
            dimension_semantics=("arbitrary", "arbitrary"), vmem_limit_bytes=VMEM_LIMIT),
        name="gla_mixer",
    )(xb, wh, w2p, gb, ng, seg, lvl)


def _post_kernel(o_ref, x_ref, wo_ref, w1_ref, w2_ref, ln_ref, y_ref, yb_ref, *, alpha):
    ln = ln_ref[...]
    x = x_ref[...]
    x1 = _layernorm(alpha * x + _dot(o_ref[...], wo_ref[...]), ln[0:1, :], ln[1:2, :])
    x1b = x1.astype(BF16)
    acc = jnp.zeros(x.shape, F32)
    dff = w1_ref.shape[1]
    for j in range(dff // FF_TILE):
        cols = slice(j * FF_TILE, (j + 1) * FF_TILE)
        hcur = jnp.maximum(_dot(x1b, w1_ref[:, cols]), 0.0)
        acc = acc + _dot((hcur * hcur).astype(BF16), w2_ref[cols, :])
    y = _layernorm(alpha * x1 + acc, ln[2:3, :], ln[3:4, :])
    y_ref[...] = y
    yb_ref[...] = y.astype(BF16)


def _post(o, x, w_out, w1, w2, g1, b1, g2, b2, alpha):
    t, dm = x.shape
    vw = o.shape[1]
    dff = w1.shape[1]
    tm = min(ROW_TILE, t)
    ln = jnp.pad(jnp.stack([g1, b1, g2, b2]).astype(F32), ((0, 4), (0, 0)))
    const = lambda shape: pl.BlockSpec(shape, lambda i: (0, 0), pipeline_mode=pl.Buffered(1))
    return pl.pallas_call(
        functools.partial(_post_kernel, alpha=alpha),
        out_shape=(jax.ShapeDtypeStruct((t, dm), F32), jax.ShapeDtypeStruct((t, dm), BF16)),
        grid=(t // tm,),
        in_specs=[
            pl.BlockSpec((tm, vw), lambda i: (i, 0)),
            pl.BlockSpec((tm, dm), lambda i: (i, 0)),
            const((vw, dm)), const((dm, dff)), const((dff, dm)), const((8, dm)),
        ],
        out_specs=(pl.BlockSpec((tm, dm), lambda i: (i, 0)), pl.BlockSpec((tm, dm), lambda i: (i, 0))),
        compiler_params=pltpu.CompilerParams(
            dimension_semantics=("arbitrary",), vmem_limit_bytes=VMEM_LIMIT),
        name="post",
    )(o, x, w_out.astype(BF16), w1.astype(BF16), w2.astype(BF16), ln)


def kernel(x, a_w_in, a_conv, a_alog, a_dt_bias, a_norm_g, a_w_out, b_w_in, b_gate_w2, b_gate_b,
           b_norm_g, b_w_out, ln1_g, ln1_b, mlp_w1, mlp_w2, ln2_g, ln2_b):
    bn, seq, dm = x.shape
    depth = ln1_g.shape[0]
    alpha = (2 * depth) ** 0.25
    xf = x.astype(F32).reshape(bn * seq, dm)
    xb = xf.astype(BF16)
    for i in range(depth):
        j = i // 2
        xb3 = xb.reshape(bn, seq, dm)
        if i % 2 == 0:
            o = _gdn_mixer(xb3, a_w_in[j], a_conv[j], a_alog[j], a_dt_bias[j], a_norm_g[j])
            w_out = a_w_out[j]
        else:
            o = _gla_mixer(xb3, b_w_in[j], b_gate_w2[j], b_gate_b[j], b_norm_g[j])
            w_out = b_w_out[j]
        xf, xb = _post(o.reshape(bn * seq, -1), xf, w_out, mlp_w1[i], mlp_w2[i],
                       ln1_g[i], ln1_b[i], ln2_g[i], ln2_b[i], alpha)
    return xf.reshape(bn, seq, dm).astype(x.dtype)
```

```python
import functools
import math

import numpy as np

import jax
import jax.numpy as jnp
from jax import lax
from jax.experimental import pallas as pl
from jax.experimental.pallas import tpu as pltpu

F32 = jnp.float32
BF16 = jnp.bfloat16

A_HEADS, A_DK, A_DV, A_CONV = 8, 128, 128, 5
B_HEADS, B_DK, B_DV, B_RANK, B_TAU = 4, 128, 256, 16, 16.0
LN_EPS, RMS_EPS, L2_EPS = 1e-5, 1e-6, 1e-6

CH = 128
N_LEVELS = 7
HALO = 8
GATE_ROWS = 16
SOLVE_LEVELS_PER_STAGE = 1
GLA_GROUP = 4
GLA_AHEAD = 2
GLA_PIECES = 2
FIN_GROUP = 4
NEG_BIG = -1e30
LOG2E = math.log2(math.e)
VMEM_LIMIT = 56 * 1024 * 1024
ROW_TILE = 1024
FF_TILE = 1024

assert CH == A_DK == B_DK and 2 ** N_LEVELS == CH


def _dot(a, b):
    return jnp.dot(a, b, preferred_element_type=F32)


def _dot_nt(a, b):
    return lax.dot_general(a, b, (((1,), (1,)), ((), ())), preferred_element_type=F32)


def _dot_tn(a, b):
    return lax.dot_general(a, b, (((0,), (0,)), ((), ())), preferred_element_type=F32)


def _split(x, n, axis=1):
    pieces = []
    for _ in range(n - 1):
        p = x.astype(BF16)
        pieces.append(p)
        x = x - p.astype(F32)
    pieces.append(x.astype(BF16))
    return jnp.concatenate(pieces, axis=axis)


def _fold(y, n, axis=1):
    w = y.shape[axis] // n
    blocks = [lax.slice_in_dim(y, i * w, (i + 1) * w, axis=axis) for i in range(n)]
    out = blocks[0]
    for b in blocks[1:]:
        out = out + b
    return out


def _dot_exact(m01, x):
    return _fold(_dot(m01, _split(x, 3)), 3)


def _sigmoid(x):
    return 0.5 + 0.5 * jnp.tanh(0.5 * x)


def _silu(x):
    h = 0.5 * x
    return h + h * jnp.tanh(h)


def _softplus(x):
    return jnp.maximum(x, 0.0) + jnp.log(1.0 + jnp.exp(-jnp.abs(x)))


def _layernorm(y, g, b):
    mu = jnp.mean(y, axis=-1, keepdims=True)
    yc = y - mu
    var = jnp.mean(yc * yc, axis=-1, keepdims=True)
    return yc * lax.rsqrt(var + LN_EPS) * g + b


def _order_masks(rev):
    row = lax.broadcasted_iota(jnp.int32, (CH, CH), 0)
    col = lax.broadcasted_iota(jnp.int32, (CH, CH), 1)
    if rev:
        return col >= row, col > row
    return col <= row, col < row


def _group(n, want):
    return math.gcd(n, want)


def _gdn_kernel(xb_ref, wh_ref, cw_ref, hp_ref, ng_ref, lm_ref, sh_ref, o_ref,
                p_ref, qkv_ref, gate_ref, a_ref, t_ref, qk_ref, rhs_ref, qd_ref, kd_ref, gl_ref,
                mc_ref, qc_ref, rc_ref, oc_ref, *, seq):
    nc = seq // CH
    spare = 2 * nc

    p_ref[0:HALO, :] = jnp.zeros((HALO, p_ref.shape[1]), F32)
    p_ref[HALO + seq:, :] = jnp.zeros((HALO, p_ref.shape[1]), F32)
    p_ref[HALO:HALO + seq, :] = _dot(xb_ref[0], wh_ref[0])
    for ref in (a_ref, t_ref, qk_ref, kd_ref, rhs_ref, qd_ref):
        ref[spare] = jnp.zeros(ref.shape[1:], ref.dtype)
    qkv_ref[nc] = jnp.zeros(qkv_ref.shape[1:], F32)
    gate_ref[nc] = jnp.zeros(gate_ref.shape[1:], F32)

    cw = cw_ref[0]
    hp = hp_ref[0]
    zero16 = jnp.zeros((CH, CH), BF16)

    def pair(f, b):
        return jnp.concatenate([jnp.concatenate([f, zero16], axis=1),
                                jnp.concatenate([zero16, b], axis=1)], axis=0)

    def tiles(it, lag):
        c = it - lag
        ok = jnp.logical_and(c >= 0, c < nc)
        return jnp.where(ok, c, spare), jnp.where(ok, nc + c, spare)

    def prep_one(c):
        base = pl.multiple_of(c * CH, CH)
        win = p_ref[pl.ds(base, CH + 2 * HALO), 0:3 * A_DK]
        gates = p_ref[pl.ds(base + HALO, CH), 4 * A_DK:5 * A_DK].T[0:GATE_ROWS, :]
        shifted = _dot(sh_ref[...], win.astype(BF16))
        mid = A_CONV // 2
        acc = win[HALO:HALO + CH, :] * cw[mid:mid + 1, :]
        g_rows = -hp[0] * _softplus(gates + hp[1])
        pieces = _split(g_rows, 3, axis=0)
        gcum = [_fold(_dot(pieces, _order_masks(not rev)[0].astype(BF16)), 3, axis=0) for rev in (False, True)]
        yield
        for n, i in enumerate([i for i in range(A_CONV) if i != mid]):
            acc = acc + shifted[n * CH:(n + 1) * CH, :] * cw[i:i + 1, :]
        yield
        s = _silu(acc)
        q = s[:, 0:A_DK]
        k = s[:, A_DK:2 * A_DK]
        q = q * (lax.rsqrt(jnp.sum(q * q, axis=-1, keepdims=True) + L2_EPS) * (A_DK ** -0.5))
        k = k * lax.rsqrt(jnp.sum(k * k, axis=-1, keepdims=True) + L2_EPS)
        yield
        qkv_ref[c] = jnp.concatenate([q, k, s[:, 2 * A_DK:]], axis=1)
        gate_ref[c] = jnp.concatenate([_sigmoid(gates), gcum[0], gcum[1]], axis=0)
        yield

    def prep_two(jf, jb, qkv, gate):
        q = qkv[:, 0:A_DK]
        k = qkv[:, A_DK:2 * A_DK]
        v = qkv[:, 2 * A_DK:]
        dirs = ((0, False, jf), (1, True, jb))
        gr = [jnp.broadcast_to(gate[(1 + d) * GATE_ROWS + 2 + d:(1 + d) * GATE_ROWS + 3 + d, :], (CH, CH))
              for d, _, _ in dirs]
        gc = [x.T for x in gr]
        beta = [jnp.broadcast_to(gate[d:d + 1, :], (CH, CH)).T for d, _, _ in dirs]
        kb16 = k.astype(BF16)
        eye = (lax.broadcasted_iota(jnp.int32, (CH, CH), 0)
               == lax.broadcasted_iota(jnp.int32, (CH, CH), 1)).astype(F32)
        yield
        kbeta = [k * beta[d] for d, _, _ in dirs]
        kq = [_dot_nt(jnp.concatenate([kbeta[d], q], axis=0).astype(BF16), kb16) for d, _, _ in dirs]
        gtot = [gc[d][0:1, :] if rev else gc[d][CH - 1:CH, :] for d, rev, _ in dirs]
        eg = [jnp.exp2(gc[d]) for d, _, _ in dirs]
        dmat = [jnp.exp2(jnp.where(_order_masks(rev)[0], gc[d] - gr[d], NEG_BIG)) for d, rev, _ in dirs]
        yield
        for d, rev, idx in dirs:
            rhs_ref[idx] = jnp.concatenate([v * beta[d], kbeta[d] * eg[d]], axis=1).astype(BF16)
            qd_ref[idx] = q * eg[d]
            kd_ref[idx] = (k * jnp.exp2(gtot[d] - gc[d])).astype(BF16)
            gl_ref[idx] = jnp.broadcast_to(jnp.exp2(gtot[d]), (8, A_DK))
        yield
        for d, rev, idx in dirs:
            a = jnp.where(_order_masks(rev)[1], kq[d][0:CH] * dmat[d], 0.0).astype(BF16)
            a_ref[idx] = a
            t_ref[idx] = (eye - (a * lm_ref[0]).astype(F32)).astype(BF16)
            qk_ref[idx] = (kq[d][CH:] * dmat[d]).astype(BF16)
        yield

    per_stage = SOLVE_LEVELS_PER_STAGE
    n_stages = (N_LEVELS - 1) // per_stage
    solve_stages = tuple((2 + s, tuple(range(1 + s * per_stage, 1 + (s + 1) * per_stage)))
                         for s in range(n_stages))
    ops_lag = n_stages + 2

    def step(it, stage_one, stage_two):
        fillers = []
        if stage_two:
            c2 = it - 1
            ok = jnp.logical_and(c2 >= 0, c2 < nc)
            cq = jnp.where(ok, c2, nc)
            fillers.append(prep_two(*tiles(it, 1), qkv_ref[cq], gate_ref[cq]))
        loaded = []
        for lag, lvs in solve_stages:
            jf, jb = tiles(it, lag)
            loaded.append((jf, jb, t_ref[jf], t_ref[jb], a_ref[jf], a_ref[jb]))
        ops_in = [(i, t_ref[i], rhs_ref[i], kd_ref[i], qk_ref[i], qd_ref[i]) for i in tiles(it, ops_lag)]
        if stage_one:
            fillers.insert(0, prep_one(it))

        def fill():
            for f in fillers:
                next(f, None)

        uws = [_dot(t, rhs) for i, t, rhs, kd, qkm, qd in ops_in]
        for half in range(per_stage):
            xs = []
            for (lag, lvs), (jf, jb, tf, tb, af, ab) in zip(solve_stages, loaded):
                m = lm_ref[lvs[half]]
                xs.append(_dot(jnp.concatenate([tf, tb], axis=1), pair(af * m, ab * m)))
            fill()
            fill()
            ys = [_dot(x.astype(BF16), pair(tf, tb)) for x, (jf, jb, tf, tb, af, ab) in zip(xs, loaded)]
            if half == 0:
                uws = [uw.astype(BF16) for uw in uws]
                kuws = [_dot_tn(kd, uw) for uw, (i, t, rhs, kd, qkm, qd) in zip(uws, ops_in)]
                quws = [_dot(qkm, uw) for uw, (i, t, rhs, kd, qkm, qd) in zip(uws, ops_in)]
            fill()
            fill()
            loaded = [(jf, jb, tf - y[:, 0:CH].astype(BF16), tb - y[:, CH:].astype(BF16), af, ab)
                      for y, (jf, jb, tf, tb, af, ab) in zip(ys, loaded)]
        for f in fillers:
            for _ in f:
                pass
        for jf, jb, tf, tb, _, _ in loaded:
            t_ref[jf] = tf
            t_ref[jb] = tb
        for kuw, quw, (i, t, rhs, kd, qkm, qd) in zip(kuws, quws, ops_in):
            qc_ref[i] = kuw[:, 0:A_DV]
            mc_ref[i] = kuw[:, A_DV:].astype(BF16)
            oc_ref[i] = quw[:, 0:A_DV]
            rc_ref[i] = (qd - quw[:, A_DV:]).astype(BF16)

    def loop(lo, hi, stage_one, stage_two):
        def body(it, carry):
            step(it, stage_one, stage_two)
            return carry
        lax.fori_loop(lo, hi, body, 0)

    loop(0, nc, True, True)
    loop(nc, nc + 1, False, True)
    loop(nc + 1, nc + ops_lag, False, False)

    def scan(c, carry):
        sf, sb = carry
        cf = c
        cb = 2 * nc - 1 - c
        of = _dot(rc_ref[cf], sf.astype(BF16)) + oc_ref[cf]
        ob = _dot(rc_ref[cb], sb.astype(BF16)) + oc_ref[cb]
        oc_ref[cf] = of
        oc_ref[cb] = ob
        sf = gl_ref[cf][0:1, :] * sf - _dot(mc_ref[cf], sf.astype(BF16)) + qc_ref[cf]
        sb = gl_ref[cb][0:1, :] * sb - _dot(mc_ref[cb], sb.astype(BF16)) + qc_ref[cb]
        return sf, sb

    zero = jnp.zeros((A_DK, A_DV), F32)
    lax.fori_loop(0, nc, scan, (zero, zero))

    ng = ng_ref[...]
    fg = _group(nc, FIN_GROUP)

    def fin(it, carry):
        cs = [it * fg + u for u in range(fg)]
        rows = [pl.ds(pl.multiple_of(c * CH, CH), CH) for c in cs]
        os_ = [oc_ref[c] + oc_ref[nc + c] for c in cs]
        inv = [lax.rsqrt(jnp.mean(o * o, axis=-1, keepdims=True) + RMS_EPS) for o in os_]
        for c, r, o, s in zip(cs, rows, os_, inv):
            z = p_ref[pl.ds(pl.multiple_of(c * CH, CH) + HALO, CH), 3 * A_DK:4 * A_DK]
            o_ref[0, r, :] = (o * s * ng * _silu(z)).astype(BF16)
        return carry

    lax.fori_loop(0, nc // fg, fin, 0)


def _gdn_level_masks():
    idx = np.arange(CH)
    x = idx[:, None] ^ idx[None, :]
    return jnp.asarray(np.stack([(x >> lv) == 1 for lv in range(N_LEVELS)]), BF16)


def _conv_shift_matrices():
    t = np.arange(CH)[:, None]
    r = np.arange(CH + 2 * HALO)[None, :]
    taps = [i for i in range(A_CONV) if i != A_CONV // 2]
    return jnp.asarray(np.concatenate([r == t + HALO + i - A_CONV // 2 for i in taps], axis=0), BF16)


def _gdn_mixer(xb, w_in, conv_w, a_log, dt_bias, norm_g):
    bn, seq, dm = xb.shape
    h, dk = A_HEADS, A_DK
    nc = seq // CH
    w = w_in
    hw = h * dk
    ba = w[:, 4 * hw:].reshape(dm, 2, 2, h)
    per_head = [w[:, i * hw:(i + 1) * hw].reshape(dm, h, dk).transpose(1, 0, 2) for i in range(4)]
    gate_cols = jnp.pad(ba.reshape(dm, 4, h).transpose(2, 0, 1), ((0, 0), (0, 0), (0, dk - 4)))
    wh = jnp.concatenate(per_head + [gate_cols], axis=2).astype(BF16)
    cw = conv_w.reshape(A_CONV, 3, h, dk).transpose(2, 0, 1, 3).reshape(h, A_CONV, 3 * dk)
    cw = jnp.pad(cw, ((0, 0), (0, 8 - A_CONV), (0, 0))).astype(F32)
    scale = jnp.zeros((h, GATE_ROWS), F32).at[:, 2:4].set((jnp.exp(a_log.astype(F32)) * LOG2E).T)
    bias = jnp.zeros((h, GATE_ROWS), F32).at[:, 2:4].set(dt_bias.astype(F32).T)
    hp = jnp.broadcast_to(jnp.stack([scale, bias], axis=1)[:, :, :, None], (h, 2, GATE_ROWS, dk))
    ng = norm_g.astype(F32).reshape(1, A_DV)
    lm = _gdn_level_masks()
    sh = _conv_shift_matrices()
    nw = wh.shape[2]

    kern = functools.partial(_gdn_kernel, seq=seq)
    tile = lambda dt: pltpu.VMEM((2 * nc + 1, CH, CH), dt)
    return pl.pallas_call(
        kern,
        out_shape=jax.ShapeDtypeStruct((bn, seq, h * A_DV), BF16),
        grid=(bn, h),
        in_specs=[
            pl.BlockSpec((1, seq, dm), lambda b, i: (b, 0, 0)),
            pl.BlockSpec((1, dm, nw), lambda b, i: (i, 0, 0)),
            pl.BlockSpec((1, 8, 3 * dk), lambda b, i: (i, 0, 0)),
            pl.BlockSpec((1, 2, GATE_ROWS, dk), lambda b, i: (i, 0, 0, 0)),
            pl.BlockSpec((1, A_DV), lambda b, i: (0, 0)),
            pl.BlockSpec(lm.shape, lambda b, i: (0, 0, 0)),
            pl.BlockSpec(sh.shape, lambda b, i: (0, 0)),
        ],
        out_specs=pl.BlockSpec((1, seq, A_DV), lambda b, i: (b, 0, i)),
        scratch_shapes=[
            pltpu.VMEM((seq + 2 * HALO, nw), F32),
            pltpu.VMEM((nc + 1, CH, 3 * A_DK), F32),
            pltpu.VMEM((nc + 1, 3 * GATE_ROWS, CH), F32),
            tile(BF16),
            tile(BF16),
            tile(BF16),
            pltpu.VMEM((2 * nc + 1, CH, A_DV + A_DK), BF16),
            tile(F32),
            tile(BF16),
            pltpu.VMEM((2 * nc + 1, 8, A_DK), F32),
            tile(BF16),
            tile(F32),
            tile(BF16),
            tile(F32),
        ],
        compiler_params=pltpu.CompilerParams(
            dimension_semantics=("arbitrary", "arbitrary"), vmem_limit_bytes=VMEM_LIMIT),
        name="gdn_mixer",
    )(xb, wh, cw, hp, ng, lm, sh)


def _gla_tables():
    i = np.arange(CH)[:, None]
    t = np.arange(CH)[None, :]
    seg = np.zeros((2, N_LEVELS + 1, CH, CH), np.float32)
    lvl = np.zeros((2, CH, CH), np.int32)
    for d in range(2):
        rev = d == 1
        seg[d, 0] = (t >= i) if rev else (t <= i)
        lv = np.full((CH, CH), N_LEVELS + 1, np.int32)
        lv[np.arange(CH), np.arange(CH)] = N_LEVELS
        x = i ^ t
        for l in range(N_LEVELS):
            h = 2 ** (N_LEVELS - 1 - l)
            b0 = (i // (2 * h)) * (2 * h)
            if rev:
                r = b0 + h
                late = i < r
                m = np.where(late, (t >= i) & (t < r), (t >= r) & (t < i))
                own = ((x >> (N_LEVELS - 1 - l)) == 1) & (t > i)
            else:
                r = b0 + h - 1
                late = i > r
                m = np.where(late, (t > r) & (t <= i), (t > i) & (t <= r))
                own = ((x >> (N_LEVELS - 1 - l)) == 1) & (t < i)
            seg[d, 1 + l] = m
            lv[own] = l
        lvl[d] = lv
    return jnp.asarray(seg, BF16), jnp.asarray(lvl)


def _gla_kernel(xb_ref, wh_ref, w2_ref, gb_ref, ng_ref, seg_ref, lvl_ref, o_ref,
                p_ref, qs_ref, kv_ref, st_ref, dec_ref, oi_ref, *, seq):
    nc = seq // CH
    dk, dv = B_DK, B_DV
    p_ref[...] = _dot(xb_ref[0], wh_ref[0])

    c_q, c_k, c_v, c_r, c_g = 0, dk, 2 * dk, 2 * dk + dv, 2 * dk + 2 * dv
    gg = _group(nc, GLA_GROUP)
    lanes = [(u, d) for u in range(gg) for d in range(2)]

    def prep(it, carry):
        cs = [it * gg + u for u in range(gg)]
        rows = [pl.ds(pl.multiple_of(c * CH, CH), CH) for c in cs]
        q = [p_ref[r, c_q:c_q + dk] * (dk ** -0.5) for r in rows]
        k = [p_ref[r, c_k:c_k + dk] for r in rows]
        q16 = [x.astype(BF16) for x in q]
        k16 = [x.astype(BF16) for x in k]
        v16 = [p_ref[r, c_v:c_v + dv].astype(BF16) for r in rows]
        gin = [p_ref[r, c_g:c_g + dk].astype(BF16) for r in rows]
        logit = [_dot(gin[u], w2_ref[0, d]) + gb_ref[0, d][0:1, :] for u, d in lanes]
        la3 = [_split(-_softplus(-x) * (LOG2E / B_TAU), 3) for x in logit]
        la2 = [y[:, 0:GLA_PIECES * dk] for y in la3]
        bc = [_fold(_dot(seg_ref[d, 0], y), 3) for (u, d), y in zip(lanes, la3)]

        def level_sums(l):
            h = CH >> (l + 1)
            if h < HALO:
                return [_fold(_dot(seg_ref[d, 1 + l], y), GLA_PIECES) for (u, d), y in zip(lanes, la2)]
            out = []
            for (u, d), b in zip(lanes, bc):
                blocks = []
                for lo in range(0, CH, 2 * h):
                    if d == 1:
                        ref = b[lo + h:lo + h + 1, :]
                        blocks += [b[lo:lo + h, :] - ref, ref - b[lo + h:lo + 2 * h, :]]
                    else:
                        ref = b[lo + h - 1:lo + h, :]
                        blocks += [ref - b[lo:lo + h, :], b[lo + h:lo + 2 * h, :] - ref]
                out.append(jnp.concatenate(blocks, axis=0))
            return out

        half = CH // 2
        zero_half = jnp.zeros((half, dk), BF16)

        def top_level(x, e, d, late):
            upper = (d == 1) != late
            rows = slice(half, CH) if upper else slice(0, half)
            kept = (x[rows] * e[rows]).astype(BF16)
            return jnp.concatenate([zero_half, kept] if upper else [kept, zero_half], axis=0)

        ahead = [level_sums(l) for l in range(GLA_AHEAD)]
        scores = prod = None
        for l in range(N_LEVELS):
            if l + GLA_AHEAD < N_LEVELS:
                ahead.append(level_sums(l + GLA_AHEAD))
            e = [jnp.exp2(x) for x in ahead[l]]
            if l == 0:
                ql = [top_level(q[u], x, d, True) for (u, d), x in zip(lanes, e)]
                kl = [top_level(k[u], x, d, False) for (u, d), x in zip(lanes, e)]
            else:
                ql = [(q[u] * x).astype(BF16) for (u, d), x in zip(lanes, e)]
                kl = [(k[u] * x).astype(BF16) for (u, d), x in zip(lanes, e)]
            if l == 1:
                scores = prod
            elif l > 1:
                own = [lvl_ref[d] == l - 1 for d in range(2)]
                scores = [jnp.where(own[d], p, s) for (u, d), p, s in zip(lanes, prod, scores)]
            prod = [_dot_nt(a, b) for a, b in zip(ql, kl)]
        own = [lvl_ref[d] == N_LEVELS - 1 for d in range(2)]
        scores = [jnp.where(own[d], p, s) for (u, d), p, s in zip(lanes, prod, scores)]
        diag = [_dot_nt(q16[u], k16[u]) for u in range(gg)]
        own = [lvl_ref[d] == N_LEVELS for d in range(2)]
        scores = [jnp.where(own[d], diag[u], s) for (u, d), s in zip(lanes, scores)]
        for (u, d), s, b in zip(lanes, scores, bc):
            idx = d * nc + cs[u]
            btot = b[0:1, :] if d == 1 else b[CH - 1:CH, :]
            oi_ref[idx] = _dot(s.astype(BF16), v16[u])
            qs_ref[idx] = (q[u] * jnp.exp2(b)).astype(BF16)
            kv_ref[idx] = _dot_tn((k[u] * jnp.exp2(btot - b)).astype(BF16), v16[u])
            dec_ref[idx] = jnp.exp2(jnp.broadcast_to(btot, (CH, dk)).T)
        return carry

    lax.fori_loop(0, nc // gg, prep, 0)

    def scan(c, carry):
        sf, sb = carry
        cf = c
        ib = 2 * nc - 1 - c
        st_ref[cf] = sf.astype(BF16)
        st_ref[ib] = sb.astype(BF16)
        ef = dec_ref[cf]
        eb = dec_ref[ib]
        sf = sf * jnp.concatenate([ef, ef], axis=1) + kv_ref[cf]
        sb = sb * jnp.concatenate([eb, eb], axis=1) + kv_ref[ib]
        return sf, sb

    zero = jnp.zeros((dk, dv), F32)
    lax.fori_loop(0, nc, scan, (zero, zero))

    ng = ng_ref[...]
    fg = _group(nc, FIN_GROUP)

    def fin(it, carry):
        cs = [it * fg + u for u in range(fg)]
        rows = [pl.ds(pl.multiple_of(c * CH, CH), CH) for c in cs]
        os_ = [oi_ref[c] + oi_ref[nc + c] + _dot(qs_ref[c], st_ref[c]) + _dot(qs_ref[nc + c], st_ref[nc + c])
               for c in cs]
        inv = [lax.rsqrt(jnp.mean(o * o, axis=-1, keepdims=True) + RMS_EPS) for o in os_]
        for r, o, s in zip(rows, os_, inv):
            o_ref[0, r, :] = (o * s * ng * _silu(p_ref[r, c_r:c_r + dv])).astype(BF16)
        return carry

    lax.fori_loop(0, nc // fg, fin, 0)


def _gla_mixer(xb, w_in, gate_w2, gate_b, norm_g):
    bn, seq, dm = xb.shape
    h, dk, dv = B_HEADS, B_DK, B_DV
    nc = seq // CH
    kw, vw = h * dk, h * dv
    w = w_in

    def heads(cols, width):
        return cols.reshape(dm, h, width).transpose(1, 0, 2)

    gl = jnp.pad(w[:, 2 * kw + 2 * vw:], ((0, 0), (0, dk - 2 * B_RANK)))
    wh = jnp.concatenate([
        heads(w[:, 0:kw], dk), heads(w[:, kw:2 * kw], dk),
        heads(w[:, 2 * kw:2 * kw + vw], dv), heads(w[:, 2 * kw + vw:2 * kw + 2 * vw], dv),
        jnp.broadcast_to(gl[None], (h, dm, dk))], axis=2).astype(BF16)
    w2 = gate_w2.reshape(2, B_RANK, h, dk).transpose(2, 0, 1, 3)
    w2p = jnp.zeros((h, 2, dk, dk), F32)
    w2p = w2p.at[:, 0, 0:B_RANK].set(w2[:, 0]).at[:, 1, B_RANK:2 * B_RANK].set(w2[:, 1]).astype(BF16)
    gb = gate_b.reshape(2, h, dk).transpose(1, 0, 2).astype(F32)
    gb = jnp.broadcast_to(gb[:, :, None, :], (h, 2, 8, dk))
    ng = norm_g.astype(F32).reshape(1, dv)
    seg, lvl = _gla_tables()
    nw = wh.shape[2]

    kern = functools.partial(_gla_kernel, seq=seq)
    return pl.pallas_call(
        kern,
        out_shape=jax.ShapeDtypeStruct((bn, seq, vw), BF16),
        grid=(bn, h),
        in_specs=[
            pl.BlockSpec((1, seq, dm), lambda b, i: (b, 0, 0)),
            pl.BlockSpec((1, dm, nw), lambda b, i: (i, 0, 0)),
            pl.BlockSpec((1, 2, dk, dk), lambda b, i: (i, 0, 0, 0)),
            pl.BlockSpec((1, 2, 8, dk), lambda b, i: (i, 0, 0, 0)),
            pl.BlockSpec((1, dv), lambda b, i: (0, 0)),
            pl.BlockSpec(seg.shape, lambda b, i: (0, 0, 0, 0)),
            pl.BlockSpec(lvl.shape, lambda b, i: (0, 0, 0)),
        ],
        out_specs=pl.BlockSpec((1, seq, dv), lambda b, i: (b, 0, i)),
        scratch_shapes=[
            pltpu.VMEM((seq, nw), F32),
            pltpu.VMEM((2 * nc, CH, dk), BF16),
            pltpu.VMEM((2 * nc, dk, dv), F32),
            pltpu.VMEM((2 * nc, dk, dv), BF16),
            pltpu.VMEM((2 * nc, dk, dk), F32),
            pltpu.VMEM((2 * nc, CH, dv), F32),
        ],
        compiler_params=pltpu.CompilerParams(
            dimension_semantics=("arbitrary", "arbitrary"), vmem_limit_bytes=VMEM_LIMIT),
        name="gla_mixer",
    )(xb, wh, w2p, gb, ng, seg, lvl)


def _post_kernel(o_ref, x_ref, wo_ref, w1_ref, w2_ref, ln_ref, y_ref, yb_ref, *, alpha):
    ln = ln_ref[...]
    x = x_ref[...]
    x1 = _layernorm(alpha * x + _dot(o_ref[...], wo_ref[...]), ln[0:1, :], ln[1:2, :])
    x1b = x1.astype(BF16)
    acc = jnp.zeros(x.shape, F32)
    dff = w1_ref.shape[1]
    for j in range(dff // FF_TILE):
        cols = slice(j * FF_TILE, (j + 1) * FF_TILE)
        hcur = jnp.maximum(_dot(x1b, w1_ref[:, cols]), 0.0)
        acc = acc + _dot((hcur * hcur).astype(BF16), w2_ref[cols, :])
    y = _layernorm(alpha * x1 + acc, ln[2:3, :], ln[3:4, :])
    y_ref[...] = y
    yb_ref[...] = y.astype(BF16)


def _post(o, x, w_out, w1, w2, g1, b1, g2, b2, alpha):
    t, dm = x.shape
    vw = o.shape[1]
    dff = w1.shape[1]
    tm = min(ROW_TILE, t)
    ln = jnp.pad(jnp.stack([g1, b1, g2, b2]).astype(F32), ((0, 4), (0, 0)))
    const = lambda shape: pl.BlockSpec(shape, lambda i: (0, 0), pipeline_mode=pl.Buffered(1))
    return pl.pallas_call(
        functools.partial(_post_kernel, alpha=alpha),
        out_shape=(jax.ShapeDtypeStruct((t, dm), F32), jax.ShapeDtypeStruct((t, dm), BF16)),
        grid=(t // tm,),
        in_specs=[
            pl.BlockSpec((tm, vw), lambda i: (i, 0)),
            pl.BlockSpec((tm, dm), lambda i: (i, 0)),
            const((vw, dm)), const((dm, dff)), const((dff, dm)), const((8, dm)),
        ],
        out_specs=(pl.BlockSpec((tm, dm), lambda i: (i, 0)), pl.BlockSpec((tm, dm), lambda i: (i, 0))),
        compiler_params=pltpu.CompilerParams(
            dimension_semantics=("arbitrary",), vmem_limit_bytes=VMEM_LIMIT),
        name="post",
    )(o, x, w_out.astype(BF16), w1.astype(BF16), w2.astype(BF16), ln)


def kernel(x, a_w_in, a_conv, a_alog, a_dt_bias, a_norm_g, a_w_out, b_w_in, b_gate_w2, b_gate_b,
           b_norm_g, b_w_out, ln1_g, ln1_b, mlp_w1, mlp_w2, ln2_g, ln2_b):
    bn, seq, dm = x.shape
    depth = ln1_g.shape[0]
    alpha = (2 * depth) ** 0.25
    xf = x.astype(F32).reshape(bn * seq, dm)
    xb = xf.astype(BF16)
    for i in range(depth):
        j = i // 2
        xb3 = xb.reshape(bn, seq, dm)
        if i % 2 == 0:
            o = _gdn_mixer(xb3, a_w_in[j], a_conv[j], a_alog[j], a_dt_bias[j], a_norm_g[j])
            w_out = a_w_out[j]
        else:
            o = _gla_mixer(xb3, b_w_in[j], b_gate_w2[j], b_gate_b[j], b_norm_g[j])
            w_out = b_w_out[j]
        xf, xb = _post(o.reshape(bn * seq, -1), xf, w_out, mlp_w1[i], mlp_w2[i],
                       ln1_g[i], ln1_b[i], ln2_g[i], ln2_b[i], alpha)
    return xf.reshape(bn, seq, dm).astype(x.dtype)
```

```python
import functools
import math

import numpy as np

import jax
import jax.numpy as jnp
from jax import lax
from jax.experimental import pallas as pl
from jax.experimental.pallas import tpu as pltpu

F32 = jnp.float32
BF16 = jnp.bfloat16

A_HEADS, A_DK, A_DV, A_CONV = 8, 128, 128, 5
B_HEADS, B_DK, B_DV, B_RANK, B_TAU = 4, 128, 256, 16, 16.0
LN_EPS, RMS_EPS, L2_EPS = 1e-5, 1e-6, 1e-6

CH = 128
N_LEVELS = 7
HALO = 8
GATE_ROWS = 16
SOLVE_LEVELS_PER_STAGE = 1
GLA_GROUP = 4
GLA_AHEAD = 2
GLA_PIECES = 2
FIN_GROUP = 4
NEG_BIG = -1e30
LOG2E = math.log2(math.e)
VMEM_LIMIT = 56 * 1024 * 1024
ROW_TILE = 1024
FF_TILE = 1024

assert CH == A_DK == B_DK and 2 ** N_LEVELS == CH


def _dot(a, b):
    return jnp.dot(a, b, preferred_element_type=F32)


def _dot_nt(a, b):
    return lax.dot_general(a, b, (((1,), (1,)), ((), ())), preferred_element_type=F32)


def _dot_tn(a, b):
    return lax.dot_general(a, b, (((0,), (0,)), ((), ())), preferred_element_type=F32)


def _split(x, n, axis=1):
    pieces = []
    for _ in range(n - 1):
        p = x.astype(BF16)
        pieces.append(p)
        x = x - p.astype(F32)
    pieces.append(x.astype(BF16))
    return jnp.concatenate(pieces, axis=axis)


def _fold(y, n, axis=1):
    w = y.shape[axis] // n
    blocks = [lax.slice_in_dim(y, i * w, (i + 1) * w, axis=axis) for i in range(n)]
    out = blocks[0]
    for b in blocks[1:]:
        out = out + b
    return out


def _dot_exact(m01, x):
    return _fold(_dot(m01, _split(x, 3)), 3)


def _sigmoid(x):
    return 0.5 + 0.5 * jnp.tanh(0.5 * x)


def _silu(x):
    h = 0.5 * x
    return h + h * jnp.tanh(h)


def _softplus(x):
    return jnp.maximum(x, 0.0) + jnp.log(1.0 + jnp.exp(-jnp.abs(x)))


def _layernorm(y, g, b):
    mu = jnp.mean(y, axis=-1, keepdims=True)
    yc = y - mu
    var = jnp.mean(yc * yc, axis=-1, keepdims=True)
    return yc * lax.rsqrt(var + LN_EPS) * g + b


def _order_masks(rev):
    row = lax.broadcasted_iota(jnp.int32, (CH, CH), 0)
    col = lax.broadcasted_iota(jnp.int32, (CH, CH), 1)
    if rev:
        return col >= row, col > row
    return col <= row, col < row


def _group(n, want):
    return math.gcd(n, want)


def _gdn_kernel(xb_ref, wh_ref, cw_ref, hp_ref, ng_ref, lm_ref, sh_ref, o_ref,
                p_ref, qkv_ref, gate_ref, a_ref, t_ref, qk_ref, rhs_ref, qd_ref, kd_ref, gl_ref,
                mc_ref, qc_ref, rc_ref, oc_ref, *, seq):
    nc = seq // CH
    spare = 2 * nc

    p_ref[0:HALO, :] = jnp.zeros((HALO, p_ref.shape[1]), F32)
    p_ref[HALO + seq:, :] = jnp.zeros((HALO, p_ref.shape[1]), F32)
    p_ref[HALO:HALO + seq, :] = _dot(xb_ref[0], wh_ref[0])
    for ref in (a_ref, t_ref, qk_ref, kd_ref, rhs_ref, qd_ref):
        ref[spare] = jnp.zeros(ref.shape[1:], ref.dtype)
    qkv_ref[nc] = jnp.zeros(qkv_ref.shape[1:], F32)
    gate_ref[nc] = jnp.zeros(gate_ref.shape[1:], F32)

    cw = cw_ref[0]
    hp = hp_ref[0]
    zero16 = jnp.zeros((CH, CH), BF16)

    def pair(f, b):
        return jnp.concatenate([jnp.concatenate([f, zero16], axis=1),
                                jnp.concatenate([zero16, b], axis=1)], axis=0)

    def tiles(it, lag):
        c = it - lag
        ok = jnp.logical_and(c >= 0, c < nc)
        return jnp.where(ok, c, spare), jnp.where(ok, nc + c, spare)

    def prep_one(c):
        base = pl.multiple_of(c * CH, CH)
        win = p_ref[pl.ds(base, CH + 2 * HALO), 0:3 * A_DK]
        gates = p_ref[pl.ds(base + HALO, CH), 4 * A_DK:5 * A_DK].T[0:GATE_ROWS, :]
        shifted = _dot(sh_ref[...], win.astype(BF16))
        mid = A_CONV // 2
        acc = win[HALO:HALO + CH, :] * cw[mid:mid + 1, :]
        for i in range(1, A_CONV - 1):
            if i != mid:
                off = HALO + i - mid
                acc = acc + win[off:off + CH, :] * cw[i:i + 1, :]
        g_rows = -hp[0] * _softplus(gates + hp[1])
        pieces = _split(g_rows, 3, axis=0)
        gcum = [_fold(_dot(pieces, _order_masks(not rev)[0].astype(BF16)), 3, axis=0) for rev in (False, True)]
        yield
        for n, i in enumerate((0, A_CONV - 1)):
            acc = acc + shifted[n * CH:(n + 1) * CH, :] * cw[i:i + 1, :]
        yield
        s = _silu(acc)
        q = s[:, 0:A_DK]
        k = s[:, A_DK:2 * A_DK]
        q = q * (lax.rsqrt(jnp.sum(q * q, axis=-1, keepdims=True) + L2_EPS) * (A_DK ** -0.5))
        k = k * lax.rsqrt(jnp.sum(k * k, axis=-1, keepdims=True) + L2_EPS)
        yield
        qkv_ref[c] = jnp.concatenate([q, k, s[:, 2 * A_DK:]], axis=1)
        gate_ref[c] = jnp.concatenate([_sigmoid(gates), gcum[0], gcum[1]], axis=0)
        yield

    def prep_two(jf, jb, qkv, gate):
        q = qkv[:, 0:A_DK]
        k = qkv[:, A_DK:2 * A_DK]
        v = qkv[:, 2 * A_DK:]
        dirs = ((0, False, jf), (1, True, jb))
        gr = [jnp.broadcast_to(gate[(1 + d) * GATE_ROWS + 2 + d:(1 + d) * GATE_ROWS + 3 + d, :], (CH, CH))
              for d, _, _ in dirs]
        gc = [x.T for x in gr]
        beta = [jnp.broadcast_to(gate[d:d + 1, :], (CH, CH)).T for d, _, _ in dirs]
        kb16 = k.astype(BF16)
        eye = (lax.broadcasted_iota(jnp.int32, (CH, CH), 0)
               == lax.broadcasted_iota(jnp.int32, (CH, CH), 1)).astype(F32)
        yield
        kbeta = [k * beta[d] for d, _, _ in dirs]
        kq = [_dot_nt(jnp.concatenate([kbeta[d], q], axis=0).astype(BF16), kb16) for d, _, _ in dirs]
        gtot = [gc[d][0:1, :] if rev else gc[d][CH - 1:CH, :] for d, rev, _ in dirs]
        eg = [jnp.exp2(gc[d]) for d, _, _ in dirs]
        dmat = [jnp.exp2(jnp.where(_order_masks(rev)[0], gc[d] - gr[d], NEG_BIG)) for d, rev, _ in dirs]
        yield
        for d, rev, idx in dirs:
            rhs_ref[idx] = jnp.concatenate([v * beta[d], kbeta[d] * eg[d]], axis=1).astype(BF16)
            qd_ref[idx] = q * eg[d]
            kd_ref[idx] = (k * jnp.exp2(gtot[d] - gc[d])).astype(BF16)
            gl_ref[idx] = jnp.broadcast_to(jnp.exp2(gtot[d]), (8, A_DK))
        yield
        for d, rev, idx in dirs:
            a = jnp.where(_order_masks(rev)[1], kq[d][0:CH] * dmat[d], 0.0).astype(BF16)
            a_ref[idx] = a
            t_ref[idx] = (eye - (a * lm_ref[0]).astype(F32)).astype(BF16)
            qk_ref[idx] = (kq[d][CH:] * dmat[d]).astype(BF16)
        yield

    per_stage = SOLVE_LEVELS_PER_STAGE
    n_stages = (N_LEVELS - 1) // per_stage
    solve_stages = tuple((2 + s, tuple(range(1 + s * per_stage, 1 + (s + 1) * per_stage)))
                         for s in range(n_stages))
    ops_lag = n_stages + 2

    def step(it, stage_one, stage_two):
        fillers = []
        if stage_two:
            c2 = it - 1
            ok = jnp.logical_and(c2 >= 0, c2 < nc)
            cq = jnp.where(ok, c2, nc)
            fillers.append(prep_two(*tiles(it, 1), qkv_ref[cq], gate_ref[cq]))
        loaded = []
        for lag, lvs in solve_stages:
            jf, jb = tiles(it, lag)
            loaded.append((jf, jb, t_ref[jf], t_ref[jb], a_ref[jf], a_ref[jb]))
        ops_in = [(i, t_ref[i], rhs_ref[i], kd_ref[i], qk_ref[i], qd_ref[i]) for i in tiles(it, ops_lag)]
        if stage_one:
            fillers.insert(0, prep_one(it))

        def fill():
            for f in fillers:
                next(f, None)

        uws = [_dot(t, rhs) for i, t, rhs, kd, qkm, qd in ops_in]
        for half in range(per_stage):
            xs = []
            for (lag, lvs), (jf, jb, tf, tb, af, ab) in zip(solve_stages, loaded):
                m = lm_ref[lvs[half]]
                xs.append(_dot(jnp.concatenate([tf, tb], axis=1), pair(af * m, ab * m)))
            fill()
            fill()
            ys = [_dot(x.astype(BF16), pair(tf, tb)) for x, (jf, jb, tf, tb, af, ab) in zip(xs, loaded)]
            if half == 0:
                uws = [uw.astype(BF16) for uw in uws]
                kuws = [_dot_tn(kd, uw) for uw, (i, t, rhs, kd, qkm, qd) in zip(uws, ops_in)]
                quws = [_dot(qkm, uw) for uw, (i, t, rhs, kd, qkm, qd) in zip(uws, ops_in)]
            fill()
            fill()
            loaded = [(jf, jb, tf - y[:, 0:CH].astype(BF16), tb - y[:, CH:].astype(BF16), af, ab)
                      for y, (jf, jb, tf, tb, af, ab) in zip(ys, loaded)]
        for f in fillers:
            for _ in f:
                pass
        for jf, jb, tf, tb, _, _ in loaded:
            t_ref[jf] = tf
            t_ref[jb] = tb
        for kuw, quw, (i, t, rhs, kd, qkm, qd) in zip(kuws, quws, ops_in):
            qc_ref[i] = kuw[:, 0:A_DV]
            mc_ref[i] = kuw[:, A_DV:].astype(BF16)
            oc_ref[i] = quw[:, 0:A_DV]
            rc_ref[i] = (qd - quw[:, A_DV:]).astype(BF16)

    def loop(lo, hi, stage_one, stage_two):
        def body(it, carry):
            step(it, stage_one, stage_two)
            return carry
        lax.fori_loop(lo, hi, body, 0)

    loop(0, nc, True, True)
    loop(nc, nc + 1, False, True)
    loop(nc + 1, nc + ops_lag, False, False)

    def scan(c, carry):
        sf, sb = carry
        cf = c
        cb = 2 * nc - 1 - c
        of = _dot(rc_ref[cf], sf.astype(BF16)) + oc_ref[cf]
        ob = _dot(rc_ref[cb], sb.astype(BF16)) + oc_ref[cb]
        oc_ref[cf] = of
        oc_ref[cb] = ob
        sf = gl_ref[cf][0:1, :] * sf - _dot(mc_ref[cf], sf.astype(BF16)) + qc_ref[cf]
        sb = gl_ref[cb][0:1, :] * sb - _dot(mc_ref[cb], sb.astype(BF16)) + qc_ref[cb]
        return sf, sb

    zero = jnp.zeros((A_DK, A_DV), F32)
    lax.fori_loop(0, nc, scan, (zero, zero))

    ng = ng_ref[...]
    fg = _group(nc, FIN_GROUP)

    def fin(it, carry):
        cs = [it * fg + u for u in range(fg)]
        rows = [pl.ds(pl.multiple_of(c * CH, CH), CH) for c in cs]
        os_ = [oc_ref[c] + oc_ref[nc + c] for c in cs]
        inv = [lax.rsqrt(jnp.mean(o * o, axis=-1, keepdims=True) + RMS_EPS) for o in os_]
        for c, r, o, s in zip(cs, rows, os_, inv):
            z = p_ref[pl.ds(pl.multiple_of(c * CH, CH) + HALO, CH), 3 * A_DK:4 * A_DK]
            o_ref[0, r, :] = (o * s * ng * _silu(z)).astype(BF16)
        return carry

    lax.fori_loop(0, nc // fg, fin, 0)


def _gdn_level_masks():
    idx = np.arange(CH)
    x = idx[:, None] ^ idx[None, :]
    return jnp.asarray(np.stack([(x >> lv) == 1 for lv in range(N_LEVELS)]), BF16)


def _conv_shift_matrices():
    t = np.arange(CH)[:, None]
    r = np.arange(CH + 2 * HALO)[None, :]
    taps = (0, A_CONV - 1)
    return jnp.asarray(np.concatenate([r == t + HALO + i - A_CONV // 2 for i in taps], axis=0), BF16)


def _gdn_mixer(xb, w_in, conv_w, a_log, dt_bias, norm_g):
    bn, seq, dm = xb.shape
    h, dk = A_HEADS, A_DK
    nc = seq // CH
    w = w_in
    hw = h * dk
    ba = w[:, 4 * hw:].reshape(dm, 2, 2, h)
    per_head = [w[:, i * hw:(i + 1) * hw].reshape(dm, h, dk).transpose(1, 0, 2) for i in range(4)]
    gate_cols = jnp.pad(ba.reshape(dm, 4, h).transpose(2, 0, 1), ((0, 0), (0, 0), (0, dk - 4)))
    wh = jnp.concatenate(per_head + [gate_cols], axis=2).astype(BF16)
    cw = conv_w.reshape(A_CONV, 3, h, dk).transpose(2, 0, 1, 3).reshape(h, A_CONV, 3 * dk)
    cw = jnp.pad(cw, ((0, 0), (0, 8 - A_CONV), (0, 0))).astype(F32)
    scale = jnp.zeros((h, GATE_ROWS), F32).at[:, 2:4].set((jnp.exp(a_log.astype(F32)) * LOG2E).T)
    bias = jnp.zeros((h, GATE_ROWS), F32).at[:, 2:4].set(dt_bias.astype(F32).T)
    hp = jnp.broadcast_to(jnp.stack([scale, bias], axis=1)[:, :, :, None], (h, 2, GATE_ROWS, dk))
    ng = norm_g.astype(F32).reshape(1, A_DV)
    lm = _gdn_level_masks()
    sh = _conv_shift_matrices()
    nw = wh.shape[2]

    kern = functools.partial(_gdn_kernel, seq=seq)
    tile = lambda dt: pltpu.VMEM((2 * nc + 1, CH, CH), dt)
    return pl.pallas_call(
        kern,
        out_shape=jax.ShapeDtypeStruct((bn, seq, h * A_DV), BF16),
        grid=(bn, h),
        in_specs=[
            pl.BlockSpec((1, seq, dm), lambda b, i: (b, 0, 0)),
            pl.BlockSpec((1, dm, nw), lambda b, i: (i, 0, 0)),
            pl.BlockSpec((1, 8, 3 * dk), lambda b, i: (i, 0, 0)),
            pl.BlockSpec((1, 2, GATE_ROWS, dk), lambda b, i: (i, 0, 0, 0)),
            pl.BlockSpec((1, A_DV), lambda b, i: (0, 0)),
            pl.BlockSpec(lm.shape, lambda b, i: (0, 0, 0)),
            pl.BlockSpec(sh.shape, lambda b, i: (0, 0)),
        ],
        out_specs=pl.BlockSpec((1, seq, A_DV), lambda b, i: (b, 0, i)),
        scratch_shapes=[
            pltpu.VMEM((seq + 2 * HALO, nw), F32),
            pltpu.VMEM((nc + 1, CH, 3 * A_DK), F32),
            pltpu.VMEM((nc + 1, 3 * GATE_ROWS, CH), F32),
            tile(BF16),
            tile(BF16),
            tile(BF16),
            pltpu.VMEM((2 * nc + 1, CH, A_DV + A_DK), BF16),
            tile(F32),
            tile(BF16),
            pltpu.VMEM((2 * nc + 1, 8, A_DK), F32),
            tile(BF16),
            tile(F32),
            tile(BF16),
            tile(F32),
        ],
        compiler_params=pltpu.CompilerParams(
            dimension_semantics=("arbitrary", "arbitrary"), vmem_limit_bytes=VMEM_LIMIT),
        name="gdn_mixer",
    )(xb, wh, cw, hp, ng, lm, sh)


def _gla_tables():
    i = np.arange(CH)[:, None]
    t = np.arange(CH)[None, :]
    seg = np.zeros((2, N_LEVELS + 1, CH, CH), np.float32)
    lvl = np.zeros((2, CH, CH), np.int32)
    for d in range(2):
        rev = d == 1
        seg[d, 0] = (t >= i) if rev else (t <= i)
        lv = np.full((CH, CH), N_LEVELS + 1, np.int32)
        lv[np.arange(CH), np.arange(CH)] = N_LEVELS
        x = i ^ t
        for l in range(N_LEVELS):
            h = 2 ** (N_LEVELS - 1 - l)
            b0 = (i // (2 * h)) * (2 * h)
            if rev:
                r = b0 + h
                late = i < r
                m = np.where(late, (t >= i) & (t < r), (t >= r) & (t < i))
                own = ((x >> (N_LEVELS - 1 - l)) == 1) & (t > i)
            else:
                r = b0 + h - 1
                late = i > r
                m = np.where(late, (t > r) & (t <= i), (t > i) & (t <= r))
                own = ((x >> (N_LEVELS - 1 - l)) == 1) & (t < i)
            seg[d, 1 + l] = m
            lv[own] = l
        lvl[d] = lv
    return jnp.asarray(seg, BF16), jnp.asarray(lvl)


def _gla_kernel(xb_ref, wh_ref, w2_ref, gb_ref, ng_ref, seg_ref, lvl_ref, o_ref,
                p_ref, qs_ref, kv_ref, st_ref, dec_ref, oi_ref, *, seq):
    nc = seq // CH
    dk, dv = B_DK, B_DV
    p_ref[...] = _dot(xb_ref[0], wh_ref[0])

    c_q, c_k, c_v, c_r, c_g = 0, dk, 2 * dk, 2 * dk + dv, 2 * dk + 2 * dv
    gg = _group(nc, GLA_GROUP)
    lanes = [(u, d) for u in range(gg) for d in range(2)]

    def prep(it, carry):
        cs = [it * gg + u for u in range(gg)]
        rows = [pl.ds(pl.multiple_of(c * CH, CH), CH) for c in cs]
        q = [p_ref[r, c_q:c_q + dk] * (dk ** -0.5) for r in rows]
        k = [p_ref[r, c_k:c_k + dk] for r in rows]
        q16 = [x.astype(BF16) for x in q]
        k16 = [x.astype(BF16) for x in k]
        v16 = [p_ref[r, c_v:c_v + dv].astype(BF16) for r in rows]
        gin = [p_ref[r, c_g:c_g + dk].astype(BF16) for r in rows]
        logit = [_dot(gin[u], w2_ref[0, d]) + gb_ref[0, d][0:1, :] for u, d in lanes]
        la3 = [_split(-_softplus(-x) * (LOG2E / B_TAU), 3) for x in logit]
        la2 = [y[:, 0:GLA_PIECES * dk] for y in la3]
        bc = [_fold(_dot(seg_ref[d, 0], y), 3) for (u, d), y in zip(lanes, la3)]

        def level_sums(l):
            h = CH >> (l + 1)
            if h < HALO:
                return [_fold(_dot(seg_ref[d, 1 + l], y), GLA_PIECES) for (u, d), y in zip(lanes, la2)]
            out = []
            for (u, d), b in zip(lanes, bc):
                blocks = []
                for lo in range(0, CH, 2 * h):
                    if d == 1:
                        ref = b[lo + h:lo + h + 1, :]
                        blocks += [b[lo:lo + h, :] - ref, ref - b[lo + h:lo + 2 * h, :]]
                    else:
                        ref = b[lo + h - 1:lo + h, :]
                        blocks += [ref - b[lo:lo + h, :], b[lo + h:lo + 2 * h, :] - ref]
                out.append(jnp.concatenate(blocks, axis=0))
            return out

        half = CH // 2
        zero_half = jnp.zeros((half, dk), BF16)

        def top_level(x, e, d, late):
            upper = (d == 1) != late
            rows = slice(half, CH) if upper else slice(0, half)
            kept = (x[rows] * e[rows]).astype(BF16)
            return jnp.concatenate([zero_half, kept] if upper else [kept, zero_half], axis=0)

        ahead = [level_sums(l) for l in range(GLA_AHEAD)]
        scores = prod = None
        for l in range(N_LEVELS):
            if l + GLA_AHEAD < N_LEVELS:
                ahead.append(level_sums(l + GLA_AHEAD))
            e = [jnp.exp2(x) for x in ahead[l]]
            if l == 0:
                ql = [top_level(q[u], x, d, True) for (u, d), x in zip(lanes, e)]
                kl = [top_level(k[u], x, d, False) for (u, d), x in zip(lanes, e)]
            else:
                ql = [(q[u] * x).astype(BF16) for (u, d), x in zip(lanes, e)]
                kl = [(k[u] * x).astype(BF16) for (u, d), x in zip(lanes, e)]
            if l == 1:
                scores = prod
            elif l > 1:
                own = [lvl_ref[d] == l - 1 for d in range(2)]
                scores = [jnp.where(own[d], p, s) for (u, d), p, s in zip(lanes, prod, scores)]
            prod = [_dot_nt(a, b) for a, b in zip(ql, kl)]
        own = [lvl_ref[d] == N_LEVELS - 1 for d in range(2)]
        scores = [jnp.where(own[d], p, s) for (u, d), p, s in zip(lanes, prod, scores)]
        diag = [_dot_nt(q16[u], k16[u]) for u in range(gg)]
        own = [lvl_ref[d] == N_LEVELS for d in range(2)]
        scores = [jnp.where(own[d], diag[u], s) for (u, d), s in zip(lanes, scores)]
        for (u, d), s, b in zip(lanes, scores, bc):
            idx = d * nc + cs[u]
            btot = b[0:1, :] if d == 1 else b[CH - 1:CH, :]
            oi_ref[idx] = _dot(s.astype(BF16), v16[u])
            qs_ref[idx] = (q[u] * jnp.exp2(b)).astype(BF16)
            kv_ref[idx] = _dot_tn((k[u] * jnp.exp2(btot - b)).astype(BF16), v16[u])
            dec_ref[idx] = jnp.exp2(jnp.broadcast_to(btot, (CH, dk)).T)
        return carry

    lax.fori_loop(0, nc // gg, prep, 0)

    def scan(c, carry):
        sf, sb = carry
        cf = c
        ib = 2 * nc - 1 - c
        st_ref[cf] = sf.astype(BF16)
        st_ref[ib] = sb.astype(BF16)
        ef = dec_ref[cf]
        eb = dec_ref[ib]
        sf = sf * jnp.concatenate([ef, ef], axis=1) + kv_ref[cf]
        sb = sb * jnp.concatenate([eb, eb], axis=1) + kv_ref[ib]
        return sf, sb

    zero = jnp.zeros((dk, dv), F32)
    lax.fori_loop(0, nc, scan, (zero, zero))

    ng = ng_ref[...]
    fg = _group(nc, FIN_GROUP)

    def fin(it, carry):
        cs = [it * fg + u for u in range(fg)]
        rows = [pl.ds(pl.multiple_of(c * CH, CH), CH) for c in cs]
        os_ = [oi_ref[c] + oi_ref[nc + c] + _dot(qs_ref[c], st_ref[c]) + _dot(qs_ref[nc + c], st_ref[nc + c])
               for c in cs]
        inv = [lax.rsqrt(jnp.mean(o * o, axis=-1, keepdims=True) + RMS_EPS) for o in os_]
        for r, o, s in zip(rows, os_, inv):
            o_ref[0, r, :] = (o * s * ng * _silu(p_ref[r, c_r:c_r + dv])).astype(BF16)
        return carry

    lax.fori_loop(0, nc // fg, fin, 0)


def _gla_mixer(xb, w_in, gate_w2, gate_b, norm_g):
    bn, seq, dm = xb.shape
    h, dk, dv = B_HEADS, B_DK, B_DV
    nc = seq // CH
    kw, vw = h * dk, h * dv
    w = w_in

    def heads(cols, width):
        return cols.reshape(dm, h, width).transpose(1, 0, 2)

    gl = jnp.pad(w[:, 2 * kw + 2 * vw:], ((0, 0), (0, dk - 2 * B_RANK)))
    wh = jnp.concatenate([
        heads(w[:, 0:kw], dk), heads(w[:, kw:2 * kw], dk),
        heads(w[:, 2 * kw:2 * kw + vw], dv), heads(w[:, 2 * kw + vw:2 * kw + 2 * vw], dv),
        jnp.broadcast_to(gl[None], (h, dm, dk))], axis=2).astype(BF16)
    w2 = gate_w2.reshape(2, B_RANK, h, dk).transpose(2, 0, 1, 3)
    w2p = jnp.zeros((h, 2, dk, dk), F32)
    w2p = w2p.at[:, 0, 0:B_RANK].set(w2[:, 0]).at[:, 1, B_RANK:2 * B_RANK].set(w2[:, 1]).astype(BF16)
    gb = gate_b.reshape(2, h, dk).transpose(1, 0, 2).astype(F32)
    gb = jnp.broadcast_to(gb[:, :, None, :], (h, 2, 8, dk))
    ng = norm_g.astype(F32).reshape(1, dv)
    seg, lvl = _gla_tables()
    nw = wh.shape[2]

    kern = functools.partial(_gla_kernel, seq=seq)
    return pl.pallas_call(
        kern,
        out_shape=jax.ShapeDtypeStruct((bn, seq, vw), BF16),
        grid=(bn, h),
        in_specs=[
            pl.BlockSpec((1, seq, dm), lambda b, i: (b, 0, 0)),
            pl.BlockSpec((1, dm, nw), lambda b, i: (i, 0, 0)),
            pl.BlockSpec((1, 2, dk, dk), lambda b, i: (i, 0, 0, 0)),
            pl.BlockSpec((1, 2, 8, dk), lambda b, i: (i, 0, 0, 0)),
            pl.BlockSpec((1, dv), lambda b, i: (0, 0)),
            pl.BlockSpec(seg.shape, lambda b, i: (0, 0, 0, 0)),
            pl.BlockSpec(lvl.shape, lambda b, i: (0, 0, 0)),
        ],
        out_specs=pl.BlockSpec((1, seq, dv), lambda b, i: (b, 0, i)),
        scratch_shapes=[
            pltpu.VMEM((seq, nw), F32),
            pltpu.VMEM((2 * nc, CH, dk), BF16),
            pltpu.VMEM((2 * nc, dk, dv), F32),
            pltpu.VMEM((2 * nc, dk, dv), BF16),
            pltpu.VMEM((2 * nc, dk, dk), F32),
            pltpu.VMEM((2 * nc, CH, dv), F32),
        ],
        compiler_params=pltpu.CompilerParams(
            dimension_semantics=("arbitrary", "arbitrary"), vmem_limit_bytes=VMEM_LIMIT),
        name="gla_mixer",
    )(xb, wh, w2p, gb, ng, seg, lvl)


def _post_kernel(o_ref, x_ref, wo_ref, w1_ref, w2_ref, ln_ref, y_ref, yb_ref, *, alpha):
    ln = ln_ref[...]
    x = x_ref[...]
    x1 = _layernorm(alpha * x + _dot(o_ref[...], wo_ref[...]), ln[0:1, :], ln[1:2, :])
    x1b = x1.astype(BF16)
    acc = jnp.zeros(x.shape, F32)
    dff = w1_ref.shape[1]
    for j in range(dff // FF_TILE):
        cols = slice(j * FF_TILE, (j + 1) * FF_TILE)
        hcur = jnp.maximum(_dot(x1b, w1_ref[:, cols]), 0.0)
        acc = acc + _dot((hcur * hcur).astype(BF16), w2_ref[cols, :])
    y = _layernorm(alpha * x1 + acc, ln[2:3, :], ln[3:4, :])
    y_ref[...] = y
    yb_ref[...] = y.astype(BF16)


def _post(o, x, w_out, w1, w2, g1, b1, g2, b2, alpha):
    t, dm = x.shape
    vw = o.shape[1]
    dff = w1.shape[1]
    tm = min(ROW_TILE, t)
    ln = jnp.pad(jnp.stack([g1, b1, g2, b2]).astype(F32), ((0, 4), (0, 0)))
    const = lambda shape: pl.BlockSpec(shape, lambda i: (0, 0), pipeline_mode=pl.Buffered(1))
    return pl.pallas_call(
        functools.partial(_post_kernel, alpha=alpha),
        out_shape=(jax.ShapeDtypeStruct((t, dm), F32), jax.ShapeDtypeStruct((t, dm), BF16)),
        grid=(t // tm,),
        in_specs=[
            pl.BlockSpec((tm, vw), lambda i: (i, 0)),
            pl.BlockSpec((tm, dm), lambda i: (i, 0)),
            const((vw, dm)), const((dm, dff)), const((dff, dm)), const((8, dm)),
        ],
        out_specs=(pl.BlockSpec((tm, dm), lambda i: (i, 0)), pl.BlockSpec((tm, dm), lambda i: (i, 0))),
        compiler_params=pltpu.CompilerParams(
            dimension_semantics=("arbitrary",), vmem_limit_bytes=VMEM_LIMIT),
        name="post",
    )(o, x, w_out.astype(BF16), w1.astype(BF16), w2.astype(BF16), ln)


def kernel(x, a_w_in, a_conv, a_alog, a_dt_bias, a_norm_g, a_w_out, b_w_in, b_gate_w2, b_gate_b,
           b_norm_g, b_w_out, ln1_g, ln1_b, mlp_w1, mlp_w2, ln2_g, ln2_b):
    bn, seq, dm = x.shape
    depth = ln1_g.shape[0]
    alpha = (2 * depth) ** 0.25
    xf = x.astype(F32).reshape(bn * seq, dm)
    xb = xf.astype(BF16)
    for i in range(depth):
        j = i // 2
        xb3 = xb.reshape(bn, seq, dm)
        if i % 2 == 0:
            o = _gdn_mixer(xb3, a_w_in[j], a_conv[j], a_alog[j], a_dt_bias[j], a_norm_g[j])
            w_out = a_w_out[j]
        else:
            o = _gla_mixer(xb3, b_w_in[j], b_gate_w2[j], b_gate_b[j], b_norm_g[j])
            w_out = b_w_out[j]
        xf, xb = _post(o.reshape(bn * seq, -1), xf, w_out, mlp_w1[i], mlp_w2[i],
                       ln1_g[i], ln1_b[i], ln2_g[i], ln2_b[i], alpha)
    return xf.reshape(bn, seq, dm).astype(x.dtype)
```

```python
import functools
import math

import numpy as np

import jax
import jax.numpy as jnp
from jax import lax
from jax.experimental import pallas as pl
from jax.experimental.pallas import tpu as pltpu

F32 = jnp.float32
BF16 = jnp.bfloat16

A_HEADS, A_DK, A_DV, A_CONV = 8, 128, 128, 5
B_HEADS, B_DK, B_DV, B_RANK, B_TAU = 4, 128, 256, 16, 16.0
LN_EPS, RMS_EPS, L2_EPS = 1e-5, 1e-6, 1e-6

CH = 128
N_LEVELS = 7
HALO = 8
CONV_MXU_TAPS = ()
GATE_ROWS = 16
SOLVE_LEVELS_PER_STAGE = 1
GLA_GROUP = 4
GLA_AHEAD = 2
GLA_PIECES = 2
FIN_GROUP = 4
NEG_BIG = -1e30
LOG2E = math.log2(math.e)
VMEM_LIMIT = 56 * 1024 * 1024
ROW_TILE = 1024
FF_TILE = 1024

assert CH == A_DK == B_DK and 2 ** N_LEVELS == CH


def _dot(a, b):
    return jnp.dot(a, b, preferred_element_type=F32)


def _dot_nt(a, b):
    return lax.dot_general(a, b, (((1,), (1,)), ((), ())), preferred_element_type=F32)


def _dot_tn(a, b):
    return lax.dot_general(a, b, (((0,), (0,)), ((), ())), preferred_element_type=F32)


def _split(x, n, axis=1):
    pieces = []
    for _ in range(n - 1):
        p = x.astype(BF16)
        pieces.append(p)
        x = x - p.astype(F32)
    pieces.append(x.astype(BF16))
    return jnp.concatenate(pieces, axis=axis)


def _fold(y, n, axis=1):
    w = y.shape[axis] // n
    blocks = [lax.slice_in_dim(y, i * w, (i + 1) * w, axis=axis) for i in range(n)]
    out = blocks[0]
    for b in blocks[1:]:
        out = out + b
    return out


def _dot_exact(m01, x):
    return _fold(_dot(m01, _split(x, 3)), 3)


def _sigmoid(x):
    return 0.5 + 0.5 * jnp.tanh(0.5 * x)


def _silu(x):
    h = 0.5 * x
    return h + h * jnp.tanh(h)


def _softplus(x):
    return jnp.maximum(x, 0.0) + jnp.log(1.0 + jnp.exp(-jnp.abs(x)))


def _layernorm(y, g, b):
    mu = jnp.mean(y, axis=-1, keepdims=True)
    yc = y - mu
    var = jnp.mean(yc * yc, axis=-1, keepdims=True)
    return yc * lax.rsqrt(var + LN_EPS) * g + b


def _order_masks(rev):
    row = lax.broadcasted_iota(jnp.int32, (CH, CH), 0)
    col = lax.broadcasted_iota(jnp.int32, (CH, CH), 1)
    if rev:
        return col >= row, col > row
    return col <= row, col < row


def _group(n, want):
    return math.gcd(n, want)


def _gdn_kernel(xb_ref, wh_ref, cw_ref, hp_ref, ng_ref, lm_ref, sh_ref, o_ref,
                p_ref, qkv_ref, gate_ref, a_ref, t_ref, qk_ref, rhs_ref, qd_ref, kd_ref, gl_ref,
                mc_ref, qc_ref, rc_ref, oc_ref, *, seq):
    nc = seq // CH
    spare = 2 * nc

    p_ref[0:HALO, :] = jnp.zeros((HALO, p_ref.shape[1]), F32)
    p_ref[HALO + seq:, :] = jnp.zeros((HALO, p_ref.shape[1]), F32)
    p_ref[HALO:HALO + seq, :] = _dot(xb_ref[0], wh_ref[0])
    for ref in (a_ref, t_ref, qk_ref, kd_ref, rhs_ref, qd_ref):
        ref[spare] = jnp.zeros(ref.shape[1:], ref.dtype)
    qkv_ref[nc] = jnp.zeros(qkv_ref.shape[1:], F32)
    gate_ref[nc] = jnp.zeros(gate_ref.shape[1:], F32)

    cw = cw_ref[0]
    hp = hp_ref[0]
    zero16 = jnp.zeros((CH, CH), BF16)

    def pair(f, b):
        return jnp.concatenate([jnp.concatenate([f, zero16], axis=1),
                                jnp.concatenate([zero16, b], axis=1)], axis=0)

    def tiles(it, lag):
        c = it - lag
        ok = jnp.logical_and(c >= 0, c < nc)
        return jnp.where(ok, c, spare), jnp.where(ok, nc + c, spare)

    def prep_one(c):
        base = pl.multiple_of(c * CH, CH)
        win = p_ref[pl.ds(base, CH + 2 * HALO), 0:3 * A_DK]
        gates = p_ref[pl.ds(base + HALO, CH), 4 * A_DK:5 * A_DK].T[0:GATE_ROWS, :]
        if CONV_MXU_TAPS:
            shifted = _dot(sh_ref[...], win.astype(BF16))
        mid = A_CONV // 2
        acc = win[HALO:HALO + CH, :] * cw[mid:mid + 1, :]
        for i in range(A_CONV):
            if i != mid and i not in CONV_MXU_TAPS:
                off = HALO + i - mid
                acc = acc + win[off:off + CH, :] * cw[i:i + 1, :]
        g_rows = -hp[0] * _softplus(gates + hp[1])
        pieces = _split(g_rows, 3, axis=0)
        gcum = [_fold(_dot(pieces, _order_masks(not rev)[0].astype(BF16)), 3, axis=0) for rev in (False, True)]
        yield
        for n, i in enumerate(CONV_MXU_TAPS):
            acc = acc + shifted[n * CH:(n + 1) * CH, :] * cw[i:i + 1, :]
        yield
        s = _silu(acc)
        q = s[:, 0:A_DK]
        k = s[:, A_DK:2 * A_DK]
        q = q * (lax.rsqrt(jnp.sum(q * q, axis=-1, keepdims=True) + L2_EPS) * (A_DK ** -0.5))
        k = k * lax.rsqrt(jnp.sum(k * k, axis=-1, keepdims=True) + L2_EPS)
        yield
        qkv_ref[c] = jnp.concatenate([q, k, s[:, 2 * A_DK:]], axis=1)
        gate_ref[c] = jnp.concatenate([_sigmoid(gates), gcum[0], gcum[1]], axis=0)
        yield

    def prep_two(jf, jb, qkv, gate):
        q = qkv[:, 0:A_DK]
        k = qkv[:, A_DK:2 * A_DK]
        v = qkv[:, 2 * A_DK:]
        dirs = ((0, False, jf), (1, True, jb))
        gr = [jnp.broadcast_to(gate[(1 + d) * GATE_ROWS + 2 + d:(1 + d) * GATE_ROWS + 3 + d, :], (CH, CH))
              for d, _, _ in dirs]
        gc = [x.T for x in gr]
        beta = [jnp.broadcast_to(gate[d:d + 1, :], (CH, CH)).T for d, _, _ in dirs]
        kb16 = k.astype(BF16)
        eye = (lax.broadcasted_iota(jnp.int32, (CH, CH), 0)
               == lax.broadcasted_iota(jnp.int32, (CH, CH), 1)).astype(F32)
        yield
        kbeta = [k * beta[d] for d, _, _ in dirs]
        kq = [_dot_nt(jnp.concatenate([kbeta[d], q], axis=0).astype(BF16), kb16) for d, _, _ in dirs]
        gtot = [gc[d][0:1, :] if rev else gc[d][CH - 1:CH, :] for d, rev, _ in dirs]
        eg = [jnp.exp2(gc[d]) for d, _, _ in dirs]
        dmat = [jnp.exp2(jnp.where(_order_masks(rev)[0], gc[d] - gr[d], NEG_BIG)) for d, rev, _ in dirs]
        yield
        for d, rev, idx in dirs:
            rhs_ref[idx] = jnp.concatenate([v * beta[d], kbeta[d] * eg[d]], axis=1).astype(BF16)
            qd_ref[idx] = q * eg[d]
            kd_ref[idx] = (k * jnp.exp2(gtot[d] - gc[d])).astype(BF16)
            gl_ref[idx] = jnp.broadcast_to(jnp.exp2(gtot[d]), (8, A_DK))
        yield
        for d, rev, idx in dirs:
            a = jnp.where(_order_masks(rev)[1], kq[d][0:CH] * dmat[d], 0.0).astype(BF16)
            a_ref[idx] = a
            t_ref[idx] = (eye - (a * lm_ref[0]).astype(F32)).astype(BF16)
            qk_ref[idx] = (kq[d][CH:] * dmat[d]).astype(BF16)
        yield

    per_stage = SOLVE_LEVELS_PER_STAGE
    n_stages = (N_LEVELS - 1) // per_stage
    solve_stages = tuple((2 + s, tuple(range(1 + s * per_stage, 1 + (s + 1) * per_stage)))
                         for s in range(n_stages))
    ops_lag = n_stages + 2

    def step(it, stage_one, stage_two):
        fillers = []
        if stage_two:
            c2 = it - 1
            ok = jnp.logical_and(c2 >= 0, c2 < nc)
            cq = jnp.where(ok, c2, nc)
            fillers.append(prep_two(*tiles(it, 1), qkv_ref[cq], gate_ref[cq]))
        loaded = []
        for lag, lvs in solve_stages:
            jf, jb = tiles(it, lag)
            loaded.append((jf, jb, t_ref[jf], t_ref[jb], a_ref[jf], a_ref[jb]))
        ops_in = [(i, t_ref[i], rhs_ref[i], kd_ref[i], qk_ref[i], qd_ref[i]) for i in tiles(it, ops_lag)]
        if stage_one:
            fillers.insert(0, prep_one(it))

        def fill():
            for f in fillers:
                next(f, None)

        uws = [_dot(t, rhs) for i, t, rhs, kd, qkm, qd in ops_in]
        for half in range(per_stage):
            xs = []
            for (lag, lvs), (jf, jb, tf, tb, af, ab) in zip(solve_stages, loaded):
                m = lm_ref[lvs[half]]
                xs.append(_dot(jnp.concatenate([tf, tb], axis=1), pair(af * m, ab * m)))
            fill()
            fill()
            ys = [_dot(x.astype(BF16), pair(tf, tb)) for x, (jf, jb, tf, tb, af, ab) in zip(xs, loaded)]
            if half == 0:
                uws = [uw.astype(BF16) for uw in uws]
                kuws = [_dot_tn(kd, uw) for uw, (i, t, rhs, kd, qkm, qd) in zip(uws, ops_in)]
                quws = [_dot(qkm, uw) for uw, (i, t, rhs, kd, qkm, qd) in zip(uws, ops_in)]
            fill()
            fill()
            loaded = [(jf, jb, tf - y[:, 0:CH].astype(BF16), tb - y[:, CH:].astype(BF16), af, ab)
                      for y, (jf, jb, tf, tb, af, ab) in zip(ys, loaded)]
        for f in fillers:
            for _ in f:
                pass
        for jf, jb, tf, tb, _, _ in loaded:
            t_ref[jf] = tf
            t_ref[jb] = tb
        for kuw, quw, (i, t, rhs, kd, qkm, qd) in zip(kuws, quws, ops_in):
            qc_ref[i] = kuw[:, 0:A_DV]
            mc_ref[i] = kuw[:, A_DV:].astype(BF16)
            oc_ref[i] = quw[:, 0:A_DV]
            rc_ref[i] = (qd - quw[:, A_DV:]).astype(BF16)

    def loop(lo, hi, stage_one, stage_two):
        def body(it, carry):
            step(it, stage_one, stage_two)
            return carry
        lax.fori_loop(lo, hi, body, 0)

    loop(0, nc, True, True)
    loop(nc, nc + 1, False, True)
    loop(nc + 1, nc + ops_lag, False, False)

    def scan(c, carry):
        sf, sb = carry
        cf = c
        cb = 2 * nc - 1 - c
        of = _dot(rc_ref[cf], sf.astype(BF16)) + oc_ref[cf]
        ob = _dot(rc_ref[cb], sb.astype(BF16)) + oc_ref[cb]
        oc_ref[cf] = of
        oc_ref[cb] = ob
        sf = gl_ref[cf][0:1, :] * sf - _dot(mc_ref[cf], sf.astype(BF16)) + qc_ref[cf]
        sb = gl_ref[cb][0:1, :] * sb - _dot(mc_ref[cb], sb.astype(BF16)) + qc_ref[cb]
        return sf, sb

    zero = jnp.zeros((A_DK, A_DV), F32)
    lax.fori_loop(0, nc, scan, (zero, zero))

    ng = ng_ref[...]
    fg = _group(nc, FIN_GROUP)

    def fin(it, carry):
        cs = [it * fg + u for u in range(fg)]
        rows = [pl.ds(pl.multiple_of(c * CH, CH), CH) for c in cs]
        os_ = [oc_ref[c] + oc_ref[nc + c] for c in cs]
        inv = [lax.rsqrt(jnp.mean(o * o, axis=-1, keepdims=True) + RMS_EPS) for o in os_]
        for c, r, o, s in zip(cs, rows, os_, inv):
            z = p_ref[pl.ds(pl.multiple_of(c * CH, CH) + HALO, CH), 3 * A_DK:4 * A_DK]
            o_ref[0, r, :] = (o * s * ng * _silu(z)).astype(BF16)
        return carry

    lax.fori_loop(0, nc // fg, fin, 0)


def _gdn_level_masks():
    idx = np.arange(CH)
    x = idx[:, None] ^ idx[None, :]
    return jnp.asarray(np.stack([(x >> lv) == 1 for lv in range(N_LEVELS)]), BF16)


def _conv_shift_matrices():
    t = np.arange(CH)[:, None]
    r = np.arange(CH + 2 * HALO)[None, :]
    taps = CONV_MXU_TAPS or (0,)
    return jnp.asarray(np.concatenate([r == t + HALO + i - A_CONV // 2 for i in taps], axis=0), BF16)


def _gdn_mixer(xb, w_in, conv_w, a_log, dt_bias, norm_g):
    bn, seq, dm = xb.shape
    h, dk = A_HEADS, A_DK
    nc = seq // CH
    w = w_in
    hw = h * dk
    ba = w[:, 4 * hw:].reshape(dm, 2, 2, h)
    per_head = [w[:, i * hw:(i + 1) * hw].reshape(dm, h, dk).transpose(1, 0, 2) for i in range(4)]
    gate_cols = jnp.pad(ba.reshape(dm, 4, h).transpose(2, 0, 1), ((0, 0), (0, 0), (0, dk - 4)))
    wh = jnp.concatenate(per_head + [gate_cols], axis=2).astype(BF16)
    cw = conv_w.reshape(A_CONV, 3, h, dk).transpose(2, 0, 1, 3).reshape(h, A_CONV, 3 * dk)
    cw = jnp.pad(cw, ((0, 0), (0, 8 - A_CONV), (0, 0))).astype(F32)
    scale = jnp.zeros((h, GATE_ROWS), F32).at[:, 2:4].set((jnp.exp(a_log.astype(F32)) * LOG2E).T)
    bias = jnp.zeros((h, GATE_ROWS), F32).at[:, 2:4].set(dt_bias.astype(F32).T)
    hp = jnp.broadcast_to(jnp.stack([scale, bias], axis=1)[:, :, :, None], (h, 2, GATE_ROWS, dk))
    ng = norm_g.astype(F32).reshape(1, A_DV)
    lm = _gdn_level_masks()
    sh = _conv_shift_matrices()
    nw = wh.shape[2]

    kern = functools.partial(_gdn_kernel, seq=seq)
    tile = lambda dt: pltpu.VMEM((2 * nc + 1, CH, CH), dt)
    return pl.pallas_call(
        kern,
        out_shape=jax.ShapeDtypeStruct((bn, seq, h * A_DV), BF16),
        grid=(bn, h),
        in_specs=[
            pl.BlockSpec((1, seq, dm), lambda b, i: (b, 0, 0)),
            pl.BlockSpec((1, dm, nw), lambda b, i: (i, 0, 0)),
            pl.BlockSpec((1, 8, 3 * dk), lambda b, i: (i, 0, 0)),
            pl.BlockSpec((1, 2, GATE_ROWS, dk), lambda b, i: (i, 0, 0, 0)),
            pl.BlockSpec((1, A_DV), lambda b, i: (0, 0)),
            pl.BlockSpec(lm.shape, lambda b, i: (0, 0, 0)),
            pl.BlockSpec(sh.shape, lambda b, i: (0, 0)),
        ],
        out_specs=pl.BlockSpec((1, seq, A_DV), lambda b, i: (b, 0, i)),
        scratch_shapes=[
            pltpu.VMEM((seq + 2 * HALO, nw), F32),
            pltpu.VMEM((nc + 1, CH, 3 * A_DK), F32),
            pltpu.VMEM((nc + 1, 3 * GATE_ROWS, CH), F32),
            tile(BF16),
            tile(BF16),
            tile(BF16),
            pltpu.VMEM((2 * nc + 1, CH, A_DV + A_DK), BF16),
            tile(F32),
            tile(BF16),
            pltpu.VMEM((2 * nc + 1, 8, A_DK), F32),
            tile(BF16),
            tile(F32),
            tile(BF16),
            tile(F32),
        ],
        compiler_params=pltpu.CompilerParams(
            dimension_semantics=("arbitrary", "arbitrary"), vmem_limit_bytes=VMEM_LIMIT),
        name="gdn_mixer",
    )(xb, wh, cw, hp, ng, lm, sh)


def _gla_tables():
    i = np.arange(CH)[:, None]
    t = np.arange(CH)[None, :]
    seg = np.zeros((2, N_LEVELS + 1, CH, CH), np.float32)
    lvl = np.zeros((2, CH, CH), np.int32)
    for d in range(2):
        rev = d == 1
        seg[d, 0] = (t >= i) if rev else (t <= i)
        lv = np.full((CH, CH), N_LEVELS + 1, np.int32)
        lv[np.arange(CH), np.arange(CH)] = N_LEVELS
        x = i ^ t
        for l in range(N_LEVELS):
            h = 2 ** (N_LEVELS - 1 - l)
            b0 = (i // (2 * h)) * (2 * h)
            if rev:
                r = b0 + h
                late = i < r
                m = np.where(late, (t >= i) & (t < r), (t >= r) & (t < i))
                own = ((x >> (N_LEVELS - 1 - l)) == 1) & (t > i)
            else:
                r = b0 + h - 1
                late = i > r
                m = np.where(late, (t > r) & (t <= i), (t > i) & (t <= r))
                own = ((x >> (N_LEVELS - 1 - l)) == 1) & (t < i)
            seg[d, 1 + l] = m
            lv[own] = l
        lvl[d] = lv
    return jnp.asarray(seg, BF16), jnp.asarray(lvl)


def _gla_kernel(xb_ref, wh_ref, w2_ref, gb_ref, ng_ref, seg_ref, lvl_ref, o_ref,
                p_ref, qs_ref, kv_ref, st_ref, dec_ref, oi_ref, *, seq):
    nc = seq // CH
    dk, dv = B_DK, B_DV
    p_ref[...] = _dot(xb_ref[0], wh_ref[0])

    c_q, c_k, c_v, c_r, c_g = 0, dk, 2 * dk, 2 * dk + dv, 2 * dk + 2 * dv
    gg = _group(nc, GLA_GROUP)
    lanes = [(u, d) for u in range(gg) for d in range(2)]

    def prep(it, carry):
        cs = [it * gg + u for u in range(gg)]
        rows = [pl.ds(pl.multiple_of(c * CH, CH), CH) for c in cs]
        q = [p_ref[r, c_q:c_q + dk] * (dk ** -0.5) for r in rows]
        k = [p_ref[r, c_k:c_k + dk] for r in rows]
        q16 = [x.astype(BF16) for x in q]
        k16 = [x.astype(BF16) for x in k]
        v16 = [p_ref[r, c_v:c_v + dv].astype(BF16) for r in rows]
        gin = [p_ref[r, c_g:c_g + dk].astype(BF16) for r in rows]
        logit = [_dot(gin[u], w2_ref[0, d]) + gb_ref[0, d][0:1, :] for u, d in lanes]
        la3 = [_split(-_softplus(-x) * (LOG2E / B_TAU), 3) for x in logit]
        la2 = [y[:, 0:GLA_PIECES * dk] for y in la3]
        bc = [_fold(_dot(seg_ref[d, 0], y), 3) for (u, d), y in zip(lanes, la3)]

        def level_sums(l):
            h = CH >> (l + 1)
            if h < HALO:
                return [_fold(_dot(seg_ref[d, 1 + l], y), GLA_PIECES) for (u, d), y in zip(lanes, la2)]
            out = []
            for (u, d), b in zip(lanes, bc):
                blocks = []
                for lo in range(0, CH, 2 * h):
                    if d == 1:
                        ref = b[lo + h:lo + h + 1, :]
                        blocks += [b[lo:lo + h, :] - ref, ref - b[lo + h:lo + 2 * h, :]]
                    else:
                        ref = b[lo + h - 1:lo + h, :]
                        blocks += [ref - b[lo:lo + h, :], b[lo + h:lo + 2 * h, :] - ref]
                out.append(jnp.concatenate(blocks, axis=0))
            return out

        half = CH // 2
        zero_half = jnp.zeros((half, dk), BF16)

        def top_level(x, e, d, late):
            upper = (d == 1) != late
            rows = slice(half, CH) if upper else slice(0, half)
            kept = (x[rows] * e[rows]).astype(BF16)
            return jnp.concatenate([zero_half, kept] if upper else [kept, zero_half], axis=0)

        ahead = [level_sums(l) for l in range(GLA_AHEAD)]
        scores = prod = None
        for l in range(N_LEVELS):
            if l + GLA_AHEAD < N_LEVELS:
                ahead.append(level_sums(l + GLA_AHEAD))
            e = [jnp.exp2(x) for x in ahead[l]]
            if l == 0:
                ql = [top_level(q[u], x, d, True) for (u, d), x in zip(lanes, e)]
                kl = [top_level(k[u], x, d, False) for (u, d), x in zip(lanes, e)]
            else:
                ql = [(q[u] * x).astype(BF16) for (u, d), x in zip(lanes, e)]
                kl = [(k[u] * x).astype(BF16) for (u, d), x in zip(lanes, e)]
            if l == 1:
                scores = prod
            elif l > 1:
                own = [lvl_ref[d] == l - 1 for d in range(2)]
                scores = [jnp.where(own[d], p, s) for (u, d), p, s in zip(lanes, prod, scores)]
            prod = [_dot_nt(a, b) for a, b in zip(ql, kl)]
        own = [lvl_ref[d] == N_LEVELS - 1 for d in range(2)]
        scores = [jnp.where(own[d], p, s) for (u, d), p, s in zip(lanes, prod, scores)]
        diag = [_dot_nt(q16[u], k16[u]) for u in range(gg)]
        own = [lvl_ref[d] == N_LEVELS for d in range(2)]
        scores = [jnp.where(own[d], diag[u], s) for (u, d), s in zip(lanes, scores)]
        for (u, d), s, b in zip(lanes, scores, bc):
            idx = d * nc + cs[u]
            btot = b[0:1, :] if d == 1 else b[CH - 1:CH, :]
            oi_ref[idx] = _dot(s.astype(BF16), v16[u])
            qs_ref[idx] = (q[u] * jnp.exp2(b)).astype(BF16)
            kv_ref[idx] = _dot_tn((k[u] * jnp.exp2(btot - b)).astype(BF16), v16[u])
            dec_ref[idx] = jnp.exp2(jnp.broadcast_to(btot, (CH, dk)).T)
        return carry

    lax.fori_loop(0, nc // gg, prep, 0)

    def scan(c, carry):
        sf, sb = carry
        cf = c
        ib = 2 * nc - 1 - c
        st_ref[cf] = sf.astype(BF16)
        st_ref[ib] = sb.astype(BF16)
        ef = dec_ref[cf]
        eb = dec_ref[ib]
        sf = sf * jnp.concatenate([ef, ef], axis=1) + kv_ref[cf]
        sb = sb * jnp.concatenate([eb, eb], axis=1) + kv_ref[ib]
        return sf, sb

    zero = jnp.zeros((dk, dv), F32)
    lax.fori_loop(0, nc, scan, (zero, zero))

    ng = ng_ref[...]
    fg = _group(nc, FIN_GROUP)

    def fin(it, carry):
        cs = [it * fg + u for u in range(fg)]
        rows = [pl.ds(pl.multiple_of(c * CH, CH), CH) for c in cs]
        os_ = [oi_ref[c] + oi_ref[nc + c] + _dot(qs_ref[c], st_ref[c]) + _dot(qs_ref[nc + c], st_ref[nc + c])
               for c in cs]
        inv = [lax.rsqrt(jnp.mean(o * o, axis=-1, keepdims=True) + RMS_EPS) for o in os_]
        for r, o, s in zip(rows, os_, inv):
            o_ref[0, r, :] = (o * s * ng * _silu(p_ref[r, c_r:c_r + dv])).astype(BF16)
        return carry

    lax.fori_loop(0, nc // fg, fin, 0)


def _gla_mixer(xb, w_in, gate_w2, gate_b, norm_g):
    bn, seq, dm = xb.shape
    h, dk, dv = B_HEADS, B_DK, B_DV
    nc = seq // CH
    kw, vw = h * dk, h * dv
    w = w_in

    def heads(cols, width):
        return cols.reshape(dm, h, width).transpose(1, 0, 2)

    gl = jnp.pad(w[:, 2 * kw + 2 * vw:], ((0, 0), (0, dk - 2 * B_RANK)))
    wh = jnp.concatenate([
        heads(w[:, 0:kw], dk), heads(w[:, kw:2 * kw], dk),
        heads(w[:, 2 * kw:2 * kw + vw], dv), heads(w[:, 2 * kw + vw:2 * kw + 2 * vw], dv),
        jnp.broadcast_to(gl[None], (h, dm, dk))], axis=2).astype(BF16)
    w2 = gate_w2.reshape(2, B_RANK, h, dk).transpose(2, 0, 1, 3)
    w2p = jnp.zeros((h, 2, dk, dk), F32)
    w2p = w2p.at[:, 0, 0:B_RANK].set(w2[:, 0]).at[:, 1, B_RANK:2 * B_RANK].set(w2[:, 1]).astype(BF16)
    gb = gate_b.reshape(2, h, dk).transpose(1, 0, 2).astype(F32)
    gb = jnp.broadcast_to(gb[:, :, None, :], (h, 2, 8, dk))
    ng = norm_g.astype(F32).reshape(1, dv)
    seg, lvl = _gla_tables()
    nw = wh.shape[2]

    kern = functools.partial(_gla_kernel, seq=seq)
    return pl.pallas_call(
        kern,
        out_shape=jax.ShapeDtypeStruct((bn, seq, vw), BF16),
        grid=(bn, h),
        in_specs=[
            pl.BlockSpec((1, seq, dm), lambda b, i: (b, 0, 0)),
            pl.BlockSpec((1, dm, nw), lambda b, i: (i, 0, 0)),
            pl.BlockSpec((1, 2, dk, dk), lambda b, i: (i, 0, 0, 0)),
            pl.BlockSpec((1, 2, 8, dk), lambda b, i: (i, 0, 0, 0)),
            pl.BlockSpec((1, dv), lambda b, i: (0, 0)),
            pl.BlockSpec(seg.shape, lambda b, i: (0, 0, 0, 0)),
            pl.BlockSpec(lvl.shape, lambda b, i: (0, 0, 0)),
        ],
        out_specs=pl.BlockSpec((1, seq, dv), lambda b, i: (b, 0, i)),
        scratch_shapes=[
            pltpu.VMEM((seq, nw), F32),
            pltpu.VMEM((2 * nc, CH, dk), BF16),
            pltpu.VMEM((2 * nc, dk, dv), F32),
            pltpu.VMEM((2 * nc, dk, dv), BF16),
            pltpu.VMEM((2 * nc, dk, dk), F32),
            pltpu.VMEM((2 * nc, CH, dv), F32),
        ],
        compiler_params=pltpu.CompilerParams(
            dimension_semantics=("arbitrary", "arbitrary"), vmem_limit_bytes=VMEM_LIMIT),
        name="gla_mixer",
    )(xb, wh, w2p, gb, ng, seg, lvl)


def _post_kernel(o_ref, x_ref, wo_ref, w1_ref, w2_ref, ln_ref, y_ref, yb_ref, *, alpha):
    ln = ln_ref[...]
    x = x_ref[...]
    x1 = _layernorm(alpha * x + _dot(o_ref[...], wo_ref[...]), ln[0:1, :], ln[1:2, :])
    x1b = x1.astype(BF16)
    acc = jnp.zeros(x.shape, F32)
    dff = w1_ref.shape[1]
    for j in range(dff // FF_TILE):
        cols = slice(j * FF_TILE, (j + 1) * FF_TILE)
        hcur = jnp.maximum(_dot(x1b, w1_ref[:, cols]), 0.0)
        acc = acc + _dot((hcur * hcur).astype(BF16), w2_ref[cols, :])
    y = _layernorm(alpha * x1 + acc, ln[2:3, :], ln[3:4, :])
    y_ref[...] = y
    yb_ref[...] = y.astype(BF16)


def _post(o, x, w_out, w1, w2, g1, b1, g2, b2, alpha):
    t, dm = x.shape
    vw = o.shape[1]
    dff = w1.shape[1]
    tm = min(ROW_TILE, t)
    ln = jnp.pad(jnp.stack([g1, b1, g2, b2]).astype(F32), ((0, 4), (0, 0)))
    const = lambda shape: pl.BlockSpec(shape, lambda i: (0, 0), pipeline_mode=pl.Buffered(1))
    return pl.pallas_call(
        functools.partial(_post_kernel, alpha=alpha),
        out_shape=(jax.ShapeDtypeStruct((t, dm), F32), jax.ShapeDtypeStruct((t, dm), BF16)),
        grid=(t // tm,),
        in_specs=[
            pl.BlockSpec((tm, vw), lambda i: (i, 0)),
            pl.BlockSpec((tm, dm), lambda i: (i, 0)),
            const((vw, dm)), const((dm, dff)), const((dff, dm)), const((8, dm)),
        ],
        out_specs=(pl.BlockSpec((tm, dm), lambda i: (i, 0)), pl.BlockSpec((tm, dm), lambda i: (i, 0))),
        compiler_params=pltpu.CompilerParams(
            dimension_semantics=("arbitrary",), vmem_limit_bytes=VMEM_LIMIT),
        name="post",
    )(o, x, w_out.astype(BF16), w1.astype(BF16), w2.astype(BF16), ln)


def kernel(x, a_w_in, a_conv, a_alog, a_dt_bias, a_norm_g, a_w_out, b_w_in, b_gate_w2, b_gate_b,
           b_norm_g, b_w_out, ln1_g, ln1_b, mlp_w1, mlp_w2, ln2_g, ln2_b):
    bn, seq, dm = x.shape
    depth = ln1_g.shape[0]
    alpha = (2 * depth) ** 0.25
    xf = x.astype(F32).reshape(bn * seq, dm)
    xb = xf.astype(BF16)
    for i in range(depth):
        j = i // 2
        xb3 = xb.reshape(bn, seq, dm)
        if i % 2 == 0:
            o = _gdn_mixer(xb3, a_w_in[j], a_conv[j], a_alog[j], a_dt_bias[j], a_norm_g[j])
            w_out = a_w_out[j]
        else:
            o = _gla_mixer(xb3, b_w_in[j], b_gate_w2[j], b_gate_b[j], b_norm_g[j])
            w_out = b_w_out[j]
        xf, xb = _post(o.reshape(bn * seq, -1), xf, w_out, mlp_w1[i], mlp_w2[i],
                       ln1_g[i], ln1_b[i], ln2_g[i], ln2_b[i], alpha)
    return xf.reshape(bn, seq, dm).astype(x.dtype)
```

```python
import functools
import math

import numpy as np

import jax
import jax.numpy as jnp
from jax import lax
from jax.experimental import pallas as pl
from jax.experimental.pallas import tpu as pltpu

F32 = jnp.float32
BF16 = jnp.bfloat16

A_HEADS, A_DK, A_DV, A_CONV = 8, 128, 128, 5
B_HEADS, B_DK, B_DV, B_RANK, B_TAU = 4, 128, 256, 16, 16.0
LN_EPS, RMS_EPS, L2_EPS = 1e-5, 1e-6, 1e-6

CH = 128
N_LEVELS = 7
HALO = 8
CONV_MXU_TAPS = (0, 4)
GATE_ROWS = 16
SOLVE_LEVELS_PER_STAGE = 1
GLA_GROUP = 4
GLA_AHEAD = 2
GLA_PIECES = 2
FIN_GROUP = 4
NEG_BIG = -1e30
LOG2E = math.log2(math.e)
VMEM_LIMIT = 56 * 1024 * 1024
ROW_TILE = 1024
FF_TILE = 1024

assert CH == A_DK == B_DK and 2 ** N_LEVELS == CH


def _dot(a, b):
    return jnp.dot(a, b, preferred_element_type=F32)


def _dot_nt(a, b):
    return lax.dot_general(a, b, (((1,), (1,)), ((), ())), preferred_element_type=F32)


def _dot_tn(a, b):
    return lax.dot_general(a, b, (((0,), (0,)), ((), ())), preferred_element_type=F32)


def _split(x, n, axis=1):
    pieces = []
    for _ in range(n - 1):
        p = x.astype(BF16)
        pieces.append(p)
        x = x - p.astype(F32)
    pieces.append(x.astype(BF16))
    return jnp.concatenate(pieces, axis=axis)


def _fold(y, n, axis=1):
    w = y.shape[axis] // n
    blocks = [lax.slice_in_dim(y, i * w, (i + 1) * w, axis=axis) for i in range(n)]
    out = blocks[0]
    for b in blocks[1:]:
        out = out + b
    return out


def _dot_exact(m01, x):
    return _fold(_dot(m01, _split(x, 3)), 3)


def _sigmoid(x):
    return 0.5 + 0.5 * jnp.tanh(0.5 * x)


def _silu(x):
    h = 0.5 * x
    return h + h * jnp.tanh(h)


def _softplus(x):
    return jnp.maximum(x, 0.0) + jnp.log(1.0 + jnp.exp(-jnp.abs(x)))


def _layernorm(y, g, b):
    mu = jnp.mean(y, axis=-1, keepdims=True)
    yc = y - mu
    var = jnp.mean(yc * yc, axis=-1, keepdims=True)
    return yc * lax.rsqrt(var + LN_EPS) * g + b


def _order_masks(rev):
    row = lax.broadcasted_iota(jnp.int32, (CH, CH), 0)
    col = lax.broadcasted_iota(jnp.int32, (CH, CH), 1)
    if rev:
        return col >= row, col > row
    return col <= row, col < row


def _group(n, want):
    return math.gcd(n, want)


def _gdn_kernel(xb_ref, wh_ref, cw_ref, hp_ref, ng_ref, lm_ref, sh_ref, o_ref,
                p_ref, qkv_ref, gate_ref, a_ref, t_ref, qk_ref, rhs_ref, qd_ref, kd_ref, gl_ref,
                mc_ref, qc_ref, rc_ref, oc_ref, *, seq):
    nc = seq // CH
    spare = 2 * nc

    p_ref[0:HALO, :] = jnp.zeros((HALO, p_ref.shape[1]), F32)
    p_ref[HALO + seq:, :] = jnp.zeros((HALO, p_ref.shape[1]), F32)
    p_ref[HALO:HALO + seq, :] = _dot(xb_ref[0], wh_ref[0])
    for ref in (a_ref, t_ref, qk_ref, kd_ref, rhs_ref, qd_ref):
        ref[spare] = jnp.zeros(ref.shape[1:], ref.dtype)
    qkv_ref[nc] = jnp.zeros(qkv_ref.shape[1:], F32)
    gate_ref[nc] = jnp.zeros(gate_ref.shape[1:], F32)

    cw = cw_ref[0]
    hp = hp_ref[0]
    zero16 = jnp.zeros((CH, CH), BF16)

    def pair(f, b):
        return jnp.concatenate([jnp.concatenate([f, zero16], axis=1),
                                jnp.concatenate([zero16, b], axis=1)], axis=0)

    def tiles(it, lag):
        c = it - lag
        ok = jnp.logical_and(c >= 0, c < nc)
        return jnp.where(ok, c, spare), jnp.where(ok, nc + c, spare)

    def prep_one(c):
        base = pl.multiple_of(c * CH, CH)
        win = p_ref[pl.ds(base, CH + 2 * HALO), 0:3 * A_DK]
        gates = p_ref[pl.ds(base + HALO, CH), 4 * A_DK:5 * A_DK].T[0:GATE_ROWS, :]
        if CONV_MXU_TAPS:
            shifted = _dot(sh_ref[...], win.astype(BF16))
        mid = A_CONV // 2
        acc = win[HALO:HALO + CH, :] * cw[mid:mid + 1, :]
        for i in range(A_CONV):
            if i != mid and i not in CONV_MXU_TAPS:
                off = HALO + i - mid
                acc = acc + win[off:off + CH, :] * cw[i:i + 1, :]
        g_rows = -hp[0] * _softplus(gates + hp[1])
        pieces = _split(g_rows, 3, axis=0)
        gcum = [_fold(_dot(pieces, _order_masks(not rev)[0].astype(BF16)), 3, axis=0) for rev in (False, True)]
        yield
        for n, i in enumerate(CONV_MXU_TAPS):
            acc = acc + shifted[n * CH:(n + 1) * CH, :] * cw[i:i + 1, :]
        yield
        s = _silu(acc)
        q = s[:, 0:A_DK]
        k = s[:, A_DK:2 * A_DK]
        q = q * (lax.rsqrt(jnp.sum(q * q, axis=-1, keepdims=True) + L2_EPS) * (A_DK ** -0.5))
        k = k * lax.rsqrt(jnp.sum(k * k, axis=-1, keepdims=True) + L2_EPS)
        yield
        qkv_ref[c] = jnp.concatenate([q, k, s[:, 2 * A_DK:]], axis=1)
        gate_ref[c] = jnp.concatenate([_sigmoid(gates), gcum[0], gcum[1]], axis=0)
        yield

    def prep_two(jf, jb, qkv, gate):
        q = qkv[:, 0:A_DK]
        k = qkv[:, A_DK:2 * A_DK]
        v = qkv[:, 2 * A_DK:]
        dirs = ((0, False, jf), (1, True, jb))
        gr = [jnp.broadcast_to(gate[(1 + d) * GATE_ROWS + 2 + d:(1 + d) * GATE_ROWS + 3 + d, :], (CH, CH))
              for d, _, _ in dirs]
        gc = [x.T for x in gr]
        beta = [jnp.broadcast_to(gate[d:d + 1, :], (CH, CH)).T for d, _, _ in dirs]
        kb16 = k.astype(BF16)
        eye = (lax.broadcasted_iota(jnp.int32, (CH, CH), 0)
               == lax.broadcasted_iota(jnp.int32, (CH, CH), 1)).astype(F32)
        yield
        kbeta = [k * beta[d] for d, _, _ in dirs]
        kkq = _dot_nt(jnp.concatenate([kbeta[0], kbeta[1], q], axis=0).astype(BF16), kb16)
        qkt = kkq[2 * CH:]
        gtot = [gc[d][0:1, :] if rev else gc[d][CH - 1:CH, :] for d, rev, _ in dirs]
        eg = [jnp.exp2(gc[d]) for d, _, _ in dirs]
        dmat = [jnp.exp2(jnp.where(_order_masks(rev)[0], gc[d] - gr[d], NEG_BIG)) for d, rev, _ in dirs]
        yield
        for d, rev, idx in dirs:
            rhs_ref[idx] = jnp.concatenate([v * beta[d], kbeta[d] * eg[d]], axis=1).astype(BF16)
            qd_ref[idx] = q * eg[d]
            kd_ref[idx] = (k * jnp.exp2(gtot[d] - gc[d])).astype(BF16)
            gl_ref[idx] = jnp.broadcast_to(jnp.exp2(gtot[d]), (8, A_DK))
        yield
        for d, rev, idx in dirs:
            a = jnp.where(_order_masks(rev)[1], kkq[d * CH:(d + 1) * CH] * dmat[d], 0.0).astype(BF16)
            a_ref[idx] = a
            t_ref[idx] = (eye - (a * lm_ref[0]).astype(F32)).astype(BF16)
            qk_ref[idx] = (qkt * dmat[d]).astype(BF16)
        yield

    per_stage = SOLVE_LEVELS_PER_STAGE
    n_stages = (N_LEVELS - 1) // per_stage
    solve_stages = tuple((2 + s, tuple(range(1 + s * per_stage, 1 + (s + 1) * per_stage)))
                         for s in range(n_stages))
    ops_lag = n_stages + 2

    def step(it, stage_one, stage_two):
        fillers = []
        if stage_two:
            c2 = it - 1
            ok = jnp.logical_and(c2 >= 0, c2 < nc)
            cq = jnp.where(ok, c2, nc)
            fillers.append(prep_two(*tiles(it, 1), qkv_ref[cq], gate_ref[cq]))
        loaded = []
        for lag, lvs in solve_stages:
            jf, jb = tiles(it, lag)
            loaded.append((jf, jb, t_ref[jf], t_ref[jb], a_ref[jf], a_ref[jb]))
        ops_in = [(i, t_ref[i], rhs_ref[i], kd_ref[i], qk_ref[i], qd_ref[i]) for i in tiles(it, ops_lag)]
        if stage_one:
            fillers.insert(0, prep_one(it))

        def fill():
            for f in fillers:
                next(f, None)

        uws = [_dot(t, rhs) for i, t, rhs, kd, qkm, qd in ops_in]
        for half in range(per_stage):
            xs = []
            for (lag, lvs), (jf, jb, tf, tb, af, ab) in zip(solve_stages, loaded):
                m = lm_ref[lvs[half]]
                xs.append(_dot(jnp.concatenate([tf, tb], axis=1), pair(af * m, ab * m)))
            fill()
            fill()
            ys = [_dot(x.astype(BF16), pair(tf, tb)) for x, (jf, jb, tf, tb, af, ab) in zip(xs, loaded)]
            if half == 0:
                uws = [uw.astype(BF16) for uw in uws]
                kuws = [_dot_tn(kd, uw) for uw, (i, t, rhs, kd, qkm, qd) in zip(uws, ops_in)]
                quws = [_dot(qkm, uw) for uw, (i, t, rhs, kd, qkm, qd) in zip(uws, ops_in)]
            fill()
            fill()
            loaded = [(jf, jb, tf - y[:, 0:CH].astype(BF16), tb - y[:, CH:].astype(BF16), af, ab)
                      for y, (jf, jb, tf, tb, af, ab) in zip(ys, loaded)]
        for f in fillers:
            for _ in f:
                pass
        for jf, jb, tf, tb, _, _ in loaded:
            t_ref[jf] = tf
            t_ref[jb] = tb
        for kuw, quw, (i, t, rhs, kd, qkm, qd) in zip(kuws, quws, ops_in):
            qc_ref[i] = kuw[:, 0:A_DV]
            mc_ref[i] = kuw[:, A_DV:].astype(BF16)
            oc_ref[i] = quw[:, 0:A_DV]
            rc_ref[i] = (qd - quw[:, A_DV:]).astype(BF16)

    def loop(lo, hi, stage_one, stage_two):
        def body(it, carry):
            step(it, stage_one, stage_two)
            return carry
        lax.fori_loop(lo, hi, body, 0)

    loop(0, nc, True, True)
    loop(nc, nc + 1, False, True)
    loop(nc + 1, nc + ops_lag, False, False)

    def scan(c, carry):
        sf, sb = carry
        cf = c
        cb = 2 * nc - 1 - c
        of = _dot(rc_ref[cf], sf.astype(BF16)) + oc_ref[cf]
        ob = _dot(rc_ref[cb], sb.astype(BF16)) + oc_ref[cb]
        oc_ref[cf] = of
        oc_ref[cb] = ob
        sf = gl_ref[cf][0:1, :] * sf - _dot(mc_ref[cf], sf.astype(BF16)) + qc_ref[cf]
        sb = gl_ref[cb][0:1, :] * sb - _dot(mc_ref[cb], sb.astype(BF16)) + qc_ref[cb]
        return sf, sb

    zero = jnp.zeros((A_DK, A_DV), F32)
    lax.fori_loop(0, nc, scan, (zero, zero))

    ng = ng_ref[...]
    fg = _group(nc, FIN_GROUP)

    def fin(it, carry):
        cs = [it * fg + u for u in range(fg)]
        rows = [pl.ds(pl.multiple_of(c * CH, CH), CH) for c in cs]
        os_ = [oc_ref[c] + oc_ref[nc + c] for c in cs]
        inv = [lax.rsqrt(jnp.mean(o * o, axis=-1, keepdims=True) + RMS_EPS) for o in os_]
        for c, r, o, s in zip(cs, rows, os_, inv):
            z = p_ref[pl.ds(pl.multiple_of(c * CH, CH) + HALO, CH), 3 * A_DK:4 * A_DK]
            o_ref[0, r, :] = (o * s * ng * _silu(z)).astype(BF16)
        return carry

    lax.fori_loop(0, nc // fg, fin, 0)


def _gdn_level_masks():
    idx = np.arange(CH)
    x = idx[:, None] ^ idx[None, :]
    return jnp.asarray(np.stack([(x >> lv) == 1 for lv in range(N_LEVELS)]), BF16)


def _conv_shift_matrices():
    t = np.arange(CH)[:, None]
    r = np.arange(CH + 2 * HALO)[None, :]
    taps = CONV_MXU_TAPS or (0,)
    return jnp.asarray(np.concatenate([r == t + HALO + i - A_CONV // 2 for i in taps], axis=0), BF16)


def _gdn_mixer(xb, w_in, conv_w, a_log, dt_bias, norm_g):
    bn, seq, dm = xb.shape
    h, dk = A_HEADS, A_DK
    nc = seq // CH
    w = w_in
    hw = h * dk
    ba = w[:, 4 * hw:].reshape(dm, 2, 2, h)
    per_head = [w[:, i * hw:(i + 1) * hw].reshape(dm, h, dk).transpose(1, 0, 2) for i in range(4)]
    gate_cols = jnp.pad(ba.reshape(dm, 4, h).transpose(2, 0, 1), ((0, 0), (0, 0), (0, dk - 4)))
    wh = jnp.concatenate(per_head + [gate_cols], axis=2).astype(BF16)
    cw = conv_w.reshape(A_CONV, 3, h, dk).transpose(2, 0, 1, 3).reshape(h, A_CONV, 3 * dk)
    cw = jnp.pad(cw, ((0, 0), (0, 8 - A_CONV), (0, 0))).astype(F32)
    scale = jnp.zeros((h, GATE_ROWS), F32).at[:, 2:4].set((jnp.exp(a_log.astype(F32)) * LOG2E).T)
    bias = jnp.zeros((h, GATE_ROWS), F32).at[:, 2:4].set(dt_bias.astype(F32).T)
    hp = jnp.broadcast_to(jnp.stack([scale, bias], axis=1)[:, :, :, None], (h, 2, GATE_ROWS, dk))
    ng = norm_g.astype(F32).reshape(1, A_DV)
    lm = _gdn_level_masks()
    sh = _conv_shift_matrices()
    nw = wh.shape[2]

    kern = functools.partial(_gdn_kernel, seq=seq)
    tile = lambda dt: pltpu.VMEM((2 * nc + 1, CH, CH), dt)
    return pl.pallas_call(
        kern,
        out_shape=jax.ShapeDtypeStruct((bn, seq, h * A_DV), BF16),
        grid=(bn, h),
        in_specs=[
            pl.BlockSpec((1, seq, dm), lambda b, i: (b, 0, 0)),
            pl.BlockSpec((1, dm, nw), lambda b, i: (i, 0, 0)),
            pl.BlockSpec((1, 8, 3 * dk), lambda b, i: (i, 0, 0)),
            pl.BlockSpec((1, 2, GATE_ROWS, dk), lambda b, i: (i, 0, 0, 0)),
            pl.BlockSpec((1, A_DV), lambda b, i: (0, 0)),
            pl.BlockSpec(lm.shape, lambda b, i: (0, 0, 0)),
            pl.BlockSpec(sh.shape, lambda b, i: (0, 0)),
        ],
        out_specs=pl.BlockSpec((1, seq, A_DV), lambda b, i: (b, 0, i)),
        scratch_shapes=[
            pltpu.VMEM((seq + 2 * HALO, nw), F32),
            pltpu.VMEM((nc + 1, CH, 3 * A_DK), F32),
            pltpu.VMEM((nc + 1, 3 * GATE_ROWS, CH), F32),
            tile(BF16),
            tile(BF16),
            tile(BF16),
            pltpu.VMEM((2 * nc + 1, CH, A_DV + A_DK), BF16),
            tile(F32),
            tile(BF16),
            pltpu.VMEM((2 * nc + 1, 8, A_DK), F32),
            tile(BF16),
            tile(F32),
            tile(BF16),
            tile(F32),
        ],
        compiler_params=pltpu.CompilerParams(
            dimension_semantics=("arbitrary", "arbitrary"), vmem_limit_bytes=VMEM_LIMIT),
        name="gdn_mixer",
    )(xb, wh, cw, hp, ng, lm, sh)


def _gla_tables():
    i = np.arange(CH)[:, None]
    t = np.arange(CH)[None, :]
    seg = np.zeros((2, N_LEVELS + 1, CH, CH), np.float32)
    lvl = np.zeros((2, CH, CH), np.int32)
    for d in range(2):
        rev = d == 1
        seg[d, 0] = (t >= i) if rev else (t <= i)
        lv = np.full((CH, CH), N_LEVELS + 1, np.int32)
        lv[np.arange(CH), np.arange(CH)] = N_LEVELS
        x = i ^ t
        for l in range(N_LEVELS):
            h = 2 ** (N_LEVELS - 1 - l)
            b0 = (i // (2 * h)) * (2 * h)
            if rev:
                r = b0 + h
                late = i < r
                m = np.where(late, (t >= i) & (t < r), (t >= r) & (t < i))
                own = ((x >> (N_LEVELS - 1 - l)) == 1) & (t > i)
            else:
                r = b0 + h - 1
                late = i > r
                m = np.where(late, (t > r) & (t <= i), (t > i) & (t <= r))
                own = ((x >> (N_LEVELS - 1 - l)) == 1) & (t < i)
            seg[d, 1 + l] = m
            lv[own] = l
        lvl[d] = lv
    return jnp.asarray(seg, BF16), jnp.asarray(lvl)


def _gla_kernel(xb_ref, wh_ref, w2_ref, gb_ref, ng_ref, seg_ref, lvl_ref, o_ref,
                p_ref, qs_ref, kv_ref, st_ref, dec_ref, oi_ref, *, seq):
    nc = seq // CH
    dk, dv = B_DK, B_DV
    p_ref[...] = _dot(xb_ref[0], wh_ref[0])

    c_q, c_k, c_v, c_r, c_g = 0, dk, 2 * dk, 2 * dk + dv, 2 * dk + 2 * dv
    gg = _group(nc, GLA_GROUP)
    lanes = [(u, d) for u in range(gg) for d in range(2)]

    def prep(it, carry):
        cs = [it * gg + u for u in range(gg)]
        rows = [pl.ds(pl.multiple_of(c * CH, CH), CH) for c in cs]
        q = [p_ref[r, c_q:c_q + dk] * (dk ** -0.5) for r in rows]
        k = [p_ref[r, c_k:c_k + dk] for r in rows]
        q16 = [x.astype(BF16) for x in q]
        k16 = [x.astype(BF16) for x in k]
        v16 = [p_ref[r, c_v:c_v + dv].astype(BF16) for r in rows]
        gin = [p_ref[r, c_g:c_g + dk].astype(BF16) for r in rows]
        w2cat = jnp.concatenate([w2_ref[0, 0], w2_ref[0, 1]], axis=1)
        both = [_dot(g, w2cat) for g in gin]
        logit = [both[u][:, d * dk:(d + 1) * dk] + gb_ref[0, d][0:1, :] for u, d in lanes]
        la3 = [_split(-_softplus(-x) * (LOG2E / B_TAU), 3) for x in logit]
        la2 = [y[:, 0:GLA_PIECES * dk] for y in la3]
        bc = [_fold(_dot(seg_ref[d, 0], y), 3) for (u, d), y in zip(lanes, la3)]

        def level_sums(l):
            h = CH >> (l + 1)
            if h < HALO:
                return [_fold(_dot(seg_ref[d, 1 + l], y), GLA_PIECES) for (u, d), y in zip(lanes, la2)]
            out = []
            for (u, d), b in zip(lanes, bc):
                blocks = []
                for lo in range(0, CH, 2 * h):
                    if d == 1:
                        ref = b[lo + h:lo + h + 1, :]
                        blocks += [b[lo:lo + h, :] - ref, ref - b[lo + h:lo + 2 * h, :]]
                    else:
                        ref = b[lo + h - 1:lo + h, :]
                        blocks += [ref - b[lo:lo + h, :], b[lo + h:lo + 2 * h, :] - ref]
                out.append(jnp.concatenate(blocks, axis=0))
            return out

        half = CH // 2
        zero_half = jnp.zeros((half, dk), BF16)
        zero_tile = jnp.zeros((CH, dk), BF16)

        def pair(f, b):
            return jnp.concatenate([jnp.concatenate([f, zero_tile], axis=1),
                                    jnp.concatenate([zero_tile, b], axis=1)], axis=0)

        def level_products(ql, kl):
            out = []
            for u in range(gg):
                both = _dot_nt(jnp.concatenate([ql[2 * u], ql[2 * u + 1]], axis=1), pair(kl[2 * u], kl[2 * u + 1]))
                out += [both[:, 0:CH], both[:, CH:]]
            return out

        def top_level(x, e, d, late):
            upper = (d == 1) != late
            rows = slice(half, CH) if upper else slice(0, half)
            kept = (x[rows] * e[rows]).astype(BF16)
            return jnp.concatenate([zero_half, kept] if upper else [kept, zero_half], axis=0)

        ahead = [level_sums(l) for l in range(GLA_AHEAD)]
        scores = prod = None
        for l in range(N_LEVELS):
            if l + GLA_AHEAD < N_LEVELS:
                ahead.append(level_sums(l + GLA_AHEAD))
            e = [jnp.exp2(x) for x in ahead[l]]
            if l == 0:
                ql = [top_level(q[u], x, d, True) for (u, d), x in zip(lanes, e)]
                kl = [top_level(k[u], x, d, False) for (u, d), x in zip(lanes, e)]
            else:
                ql = [(q[u] * x).astype(BF16) for (u, d), x in zip(lanes, e)]
                kl = [(k[u] * x).astype(BF16) for (u, d), x in zip(lanes, e)]
            if l == 1:
                scores = prod
            elif l > 1:
                own = [lvl_ref[d] == l - 1 for d in range(2)]
                scores = [jnp.where(own[d], p, s) for (u, d), p, s in zip(lanes, prod, scores)]
            prod = level_products(ql, kl)
        own = [lvl_ref[d] == N_LEVELS - 1 for d in range(2)]
        scores = [jnp.where(own[d], p, s) for (u, d), p, s in zip(lanes, prod, scores)]
        diag = [_dot_nt(q16[u], k16[u]) for u in range(gg)]
        own = [lvl_ref[d] == N_LEVELS for d in range(2)]
        scores = [jnp.where(own[d], diag[u], s) for (u, d), s in zip(lanes, scores)]
        for u in range(gg):
            oi_ref[cs[u]] = _dot(jnp.concatenate([scores[2 * u], scores[2 * u + 1]], axis=1).astype(BF16),
                                 jnp.concatenate([v16[u], v16[u]], axis=0))
        for (u, d), b in zip(lanes, bc):
            idx = d * nc + cs[u]
            btot = b[0:1, :] if d == 1 else b[CH - 1:CH, :]
            qs_ref[idx] = (q[u] * jnp.exp2(b)).astype(BF16)
            kv_ref[idx] = _dot_tn((k[u] * jnp.exp2(btot - b)).astype(BF16), v16[u])
            dec_ref[idx] = jnp.exp2(jnp.broadcast_to(btot, (CH, dk)).T)
        return carry

    lax.fori_loop(0, nc // gg, prep, 0)

    def scan(c, carry):
        sf, sb = carry
        cf = c
        ib = 2 * nc - 1 - c
        st_ref[cf] = sf.astype(BF16)
        st_ref[ib] = sb.astype(BF16)
        ef = dec_ref[cf]
        eb = dec_ref[ib]
        sf = sf * jnp.concatenate([ef, ef], axis=1) + kv_ref[cf]
        sb = sb * jnp.concatenate([eb, eb], axis=1) + kv_ref[ib]
        return sf, sb

    zero = jnp.zeros((dk, dv), F32)
    lax.fori_loop(0, nc, scan, (zero, zero))

    ng = ng_ref[...]
    fg = _group(nc, FIN_GROUP)

    def fin(it, carry):
        cs = [it * fg + u for u in range(fg)]
        rows = [pl.ds(pl.multiple_of(c * CH, CH), CH) for c in cs]
        os_ = [oi_ref[c] + _dot(jnp.concatenate([qs_ref[c], qs_ref[nc + c]], axis=1),
                                jnp.concatenate([st_ref[c], st_ref[nc + c]], axis=0)) for c in cs]
        inv = [lax.rsqrt(jnp.mean(o * o, axis=-1, keepdims=True) + RMS_EPS) for o in os_]
        for r, o, s in zip(rows, os_, inv):
            o_ref[0, r, :] = (o * s * ng * _silu(p_ref[r, c_r:c_r + dv])).astype(BF16)
        return carry

    lax.fori_loop(0, nc // fg, fin, 0)


def _gla_mixer(xb, w_in, gate_w2, gate_b, norm_g):
    bn, seq, dm = xb.shape
    h, dk, dv = B_HEADS, B_DK, B_DV
    nc = seq // CH
    kw, vw = h * dk, h * dv
    w = w_in

    def heads(cols, width):
        return cols.reshape(dm, h, width).transpose(1, 0, 2)

    gl = jnp.pad(w[:, 2 * kw + 2 * vw:], ((0, 0), (0, dk - 2 * B_RANK)))
    wh = jnp.concatenate([
        heads(w[:, 0:kw], dk), heads(w[:, kw:2 * kw], dk),
        heads(w[:, 2 * kw:2 * kw + vw], dv), heads(w[:, 2 * kw + vw:2 * kw + 2 * vw], dv),
        jnp.broadcast_to(gl[None], (h, dm, dk))], axis=2).astype(BF16)
    w2 = gate_w2.reshape(2, B_RANK, h, dk).transpose(2, 0, 1, 3)
    w2p = jnp.zeros((h, 2, dk, dk), F32)
    w2p = w2p.at[:, 0, 0:B_RANK].set(w2[:, 0]).at[:, 1, B_RANK:2 * B_RANK].set(w2[:, 1]).astype(BF16)
    gb = gate_b.reshape(2, h, dk).transpose(1, 0, 2).astype(F32)
    gb = jnp.broadcast_to(gb[:, :, None, :], (h, 2, 8, dk))
    ng = norm_g.astype(F32).reshape(1, dv)
    seg, lvl = _gla_tables()
    nw = wh.shape[2]

    kern = functools.partial(_gla_kernel, seq=seq)
    return pl.pallas_call(
        kern,
        out_shape=jax.ShapeDtypeStruct((bn, seq, vw), BF16),
        grid=(bn, h),
        in_specs=[
            pl.BlockSpec((1, seq, dm), lambda b, i: (b, 0, 0)),
            pl.BlockSpec((1, dm, nw), lambda b, i: (i, 0, 0)),
            pl.BlockSpec((1, 2, dk, dk), lambda b, i: (i, 0, 0, 0)),
            pl.BlockSpec((1, 2, 8, dk), lambda b, i: (i, 0, 0, 0)),
            pl.BlockSpec((1, dv), lambda b, i: (0, 0)),
            pl.BlockSpec(seg.shape, lambda b, i: (0, 0, 0, 0)),
            pl.BlockSpec(lvl.shape, lambda b, i: (0, 0, 0)),
        ],
        out_specs=pl.BlockSpec((1, seq, dv), lambda b, i: (b, 0, i)),
        scratch_shapes=[
            pltpu.VMEM((seq, nw), F32),
            pltpu.VMEM((2 * nc, CH, dk), BF16),
            pltpu.VMEM((2 * nc, dk, dv), F32),
            pltpu.VMEM((2 * nc, dk, dv), BF16),
            pltpu.VMEM((2 * nc, dk, dk), F32),
            pltpu.VMEM((nc, CH, dv), F32),
        ],
        compiler_params=pltpu.CompilerParams(
            dimension_semantics=("arbitrary", "arbitrary"), vmem_limit_bytes=VMEM_LIMIT),
        name="gla_mixer",
    )(xb, wh, w2p, gb, ng, seg, lvl)


def _post_kernel(o_ref, x_ref, wo_ref, w1_ref, w2_ref, ln_ref, y_ref, yb_ref, *, alpha):
    ln = ln_ref[...]
    x = x_ref[...]
    x1 = _layernorm(alpha * x + _dot(o_ref[...], wo_ref[...]), ln[0:1, :], ln[1:2, :])
    x1b = x1.astype(BF16)
    acc = jnp.zeros(x.shape, F32)
    dff = w1_ref.shape[1]
    for j in range(dff // FF_TILE):
        cols = slice(j * FF_TILE, (j + 1) * FF_TILE)
        hcur = jnp.maximum(_dot(x1b, w1_ref[:, cols]), 0.0)
        acc = acc + _dot((hcur * hcur).astype(BF16), w2_ref[cols, :])
    y = _layernorm(alpha * x1 + acc, ln[2:3, :], ln[3:4, :])
    y_ref[...] = y
    yb_ref[...] = y.astype(BF16)


def _post(o, x, w_out, w1, w2, g1, b1, g2, b2, alpha):
    t, dm = x.shape
    vw = o.shape[1]
    dff = w1.shape[1]
    tm = min(ROW_TILE, t)
    ln = jnp.pad(jnp.stack([g1, b1, g2, b2]).astype(F32), ((0, 4), (0, 0)))
    const = lambda shape: pl.BlockSpec(shape, lambda i: (0, 0), pipeline_mode=pl.Buffered(1))
    return pl.pallas_call(
        functools.partial(_post_kernel, alpha=alpha),
        out_shape=(jax.ShapeDtypeStruct((t, dm), F32), jax.ShapeDtypeStruct((t, dm), BF16)),
        grid=(t // tm,),
        in_specs=[
            pl.BlockSpec((tm, vw), lambda i: (i, 0)),
            pl.BlockSpec((tm, dm), lambda i: (i, 0)),
            const((vw, dm)), const((dm, dff)), const((dff, dm)), const((8, dm)),
        ],
        out_specs=(pl.BlockSpec((tm, dm), lambda i: (i, 0)), pl.BlockSpec((tm, dm), lambda i: (i, 0))),
        compiler_params=pltpu.CompilerParams(
            dimension_semantics=("arbitrary",), vmem_limit_bytes=VMEM_LIMIT),
        name="post",
    )(o, x, w_out.astype(BF16), w1.astype(BF16), w2.astype(BF16), ln)


def kernel(x, a_w_in, a_conv, a_alog, a_dt_bias, a_norm_g, a_w_out, b_w_in, b_gate_w2, b_gate_b,
           b_norm_g, b_w_out, ln1_g, ln1_b, mlp_w1, mlp_w2, ln2_g, ln2_b):
    bn, seq, dm = x.shape
    depth = ln1_g.shape[0]
    alpha = (2 * depth) ** 0.25
    xf = x.astype(F32).reshape(bn * seq, dm)
    xb = xf.astype(BF16)
    for i in range(depth):
        j = i // 2
        xb3 = xb.reshape(bn, seq, dm)
        if i % 2 == 0:
            o = _gdn_mixer(xb3, a_w_in[j], a_conv[j], a_alog[j], a_dt_bias[j], a_norm_g[j])
            w_out = a_w_out[j]
        else:
            o = _gla_mixer(xb3, b_w_in[j], b_gate_w2[j], b_gate_b[j], b_norm_g[j])
            w_out = b_w_out[j]
        xf, xb = _post(o.reshape(bn * seq, -1), xf, w_out, mlp_w1[i], mlp_w2[i],
                       ln1_g[i], ln1_b[i], ln2_g[i], ln2_b[i], alpha)
    return xf.reshape(bn, seq, dm).astype(x.dtype)
```

```python
import functools
import math

import numpy as np

import jax
import jax.numpy as jnp
from jax import lax
from jax.experimental import pallas as pl
from jax.experimental.pallas import tpu as pltpu

F32 = jnp.float32
BF16 = jnp.bfloat16

A_HEADS, A_DK, A_DV, A_CONV = 8, 128, 128, 5
B_HEADS, B_DK, B_DV, B_RANK, B_TAU = 4, 128, 256, 16, 16.0
LN_EPS, RMS_EPS, L2_EPS = 1e-5, 1e-6, 1e-6

CH = 128
N_LEVELS = 7
HALO = 8
CONV_MXU_TAPS = (0, 4)
GATE_ROWS = 16
SOLVE_LEVELS_PER_STAGE = 1
GLA_GROUP = 4
GLA_AHEAD = 2
GLA_PIECES = 2
FIN_GROUP = 4
NEG_BIG = -1e30
LOG2E = math.log2(math.e)
VMEM_LIMIT = 56 * 1024 * 1024
ROW_TILE = 1024
FF_TILE = 1024

assert CH == A_DK == B_DK and 2 ** N_LEVELS == CH


def _dot(a, b):
    return jnp.dot(a, b, preferred_element_type=F32)


def _dot_nt(a, b):
    return lax.dot_general(a, b, (((1,), (1,)), ((), ())), preferred_element_type=F32)


def _dot_tn(a, b):
    return lax.dot_general(a, b, (((0,), (0,)), ((), ())), preferred_element_type=F32)


def _split(x, n, axis=1):
    pieces = []
    for _ in range(n - 1):
        p = x.astype(BF16)
        pieces.append(p)
        x = x - p.astype(F32)
    pieces.append(x.astype(BF16))
    return jnp.concatenate(pieces, axis=axis)


def _fold(y, n, axis=1):
    w = y.shape[axis] // n
    blocks = [lax.slice_in_dim(y, i * w, (i + 1) * w, axis=axis) for i in range(n)]
    out = blocks[0]
    for b in blocks[1:]:
        out = out + b
    return out


def _dot_exact(m01, x):
    return _fold(_dot(m01, _split(x, 3)), 3)


def _sigmoid(x):
    return 0.5 + 0.5 * jnp.tanh(0.5 * x)


def _silu(x):
    h = 0.5 * x
    return h + h * jnp.tanh(h)


def _softplus(x):
    return jnp.maximum(x, 0.0) + jnp.log(1.0 + jnp.exp(-jnp.abs(x)))


def _layernorm(y, g, b):
    mu = jnp.mean(y, axis=-1, keepdims=True)
    yc = y - mu
    var = jnp.mean(yc * yc, axis=-1, keepdims=True)
    return yc * lax.rsqrt(var + LN_EPS) * g + b


def _order_masks(rev):
    row = lax.broadcasted_iota(jnp.int32, (CH, CH), 0)
    col = lax.broadcasted_iota(jnp.int32, (CH, CH), 1)
    if rev:
        return col >= row, col > row
    return col <= row, col < row


def _group(n, want):
    return math.gcd(n, want)


def _gdn_kernel(xb_ref, wh_ref, cw_ref, hp_ref, ng_ref, lm_ref, sh_ref, o_ref,
                p_ref, qkv_ref, gate_ref, a_ref, t_ref, qk_ref, rhs_ref, qd_ref, kd_ref, gl_ref,
                mc_ref, qc_ref, rc_ref, oc_ref, *, seq):
    nc = seq // CH
    spare = 2 * nc

    p_ref[0:HALO, :] = jnp.zeros((HALO, p_ref.shape[1]), F32)
    p_ref[HALO + seq:, :] = jnp.zeros((HALO, p_ref.shape[1]), F32)
    p_ref[HALO:HALO + seq, :] = _dot(xb_ref[0], wh_ref[0])
    for ref in (a_ref, t_ref, qk_ref, kd_ref, rhs_ref, qd_ref):
        ref[spare] = jnp.zeros(ref.shape[1:], ref.dtype)
    qkv_ref[nc] = jnp.zeros(qkv_ref.shape[1:], F32)
    gate_ref[nc] = jnp.zeros(gate_ref.shape[1:], F32)

    cw = cw_ref[0]
    hp = hp_ref[0]
    zero16 = jnp.zeros((CH, CH), BF16)

    def pair(f, b):
        return jnp.concatenate([jnp.concatenate([f, zero16], axis=1),
                                jnp.concatenate([zero16, b], axis=1)], axis=0)

    def tiles(it, lag):
        c = it - lag
        ok = jnp.logical_and(c >= 0, c < nc)
        return jnp.where(ok, c, spare), jnp.where(ok, nc + c, spare)

    def prep_one(c):
        base = pl.multiple_of(c * CH, CH)
        win = p_ref[pl.ds(base, CH + 2 * HALO), 0:3 * A_DK]
        gates = p_ref[pl.ds(base + HALO, CH), 4 * A_DK:5 * A_DK].T[0:GATE_ROWS, :]
        if CONV_MXU_TAPS:
            shifted = _dot(sh_ref[...], win.astype(BF16))
        mid = A_CONV // 2
        acc = win[HALO:HALO + CH, :] * cw[mid:mid + 1, :]
        for i in range(A_CONV):
            if i != mid and i not in CONV_MXU_TAPS:
                off = HALO + i - mid
                acc = acc + win[off:off + CH, :] * cw[i:i + 1, :]
        g_rows = -hp[0] * _softplus(gates + hp[1])
        pieces = _split(g_rows, 3, axis=0)
        gcum = [_fold(_dot(pieces, _order_masks(not rev)[0].astype(BF16)), 3, axis=0) for rev in (False, True)]
        yield
        for n, i in enumerate(CONV_MXU_TAPS):
            acc = acc + shifted[n * CH:(n + 1) * CH, :] * cw[i:i + 1, :]
        yield
        s = _silu(acc)
        q = s[:, 0:A_DK]
        k = s[:, A_DK:2 * A_DK]
        q = q * (lax.rsqrt(jnp.sum(q * q, axis=-1, keepdims=True) + L2_EPS) * (A_DK ** -0.5))
        k = k * lax.rsqrt(jnp.sum(k * k, axis=-1, keepdims=True) + L2_EPS)
        yield
        qkv_ref[c] = jnp.concatenate([q, k, s[:, 2 * A_DK:]], axis=1)
        gate_ref[c] = jnp.concatenate([_sigmoid(gates), gcum[0], gcum[1]], axis=0)
        yield

    def prep_two(jf, jb, qkv, gate):
        q = qkv[:, 0:A_DK]
        k = qkv[:, A_DK:2 * A_DK]
        v = qkv[:, 2 * A_DK:]
        dirs = ((0, False, jf), (1, True, jb))
        gr = [jnp.broadcast_to(gate[(1 + d) * GATE_ROWS + 2 + d:(1 + d) * GATE_ROWS + 3 + d, :], (CH, CH))
              for d, _, _ in dirs]
        gc = [x.T for x in gr]
        beta = [jnp.broadcast_to(gate[d:d + 1, :], (CH, CH)).T for d, _, _ in dirs]
        kb16 = k.astype(BF16)
        eye = (lax.broadcasted_iota(jnp.int32, (CH, CH), 0)
               == lax.broadcasted_iota(jnp.int32, (CH, CH), 1)).astype(F32)
        yield
        kbeta = [k * beta[d] for d, _, _ in dirs]
        kq = [_dot_nt(jnp.concatenate([kbeta[d], q], axis=0).astype(BF16), kb16) for d, _, _ in dirs]
        gtot = [gc[d][0:1, :] if rev else gc[d][CH - 1:CH, :] for d, rev, _ in dirs]
        eg = [jnp.exp2(gc[d]) for d, _, _ in dirs]
        dmat = [jnp.exp2(jnp.where(_order_masks(rev)[0], gc[d] - gr[d], NEG_BIG)) for d, rev, _ in dirs]
        yield
        for d, rev, idx in dirs:
            rhs_ref[idx] = jnp.concatenate([v * beta[d], kbeta[d] * eg[d]], axis=1).astype(BF16)
            qd_ref[idx] = q * eg[d]
            kd_ref[idx] = (k * jnp.exp2(gtot[d] - gc[d])).astype(BF16)
            gl_ref[idx] = jnp.broadcast_to(jnp.exp2(gtot[d]), (8, A_DK))
        yield
        for d, rev, idx in dirs:
            a = jnp.where(_order_masks(rev)[1], kq[d][0:CH] * dmat[d], 0.0).astype(BF16)
            a_ref[idx] = a
            t_ref[idx] = (eye - (a * lm_ref[0]).astype(F32)).astype(BF16)
            qk_ref[idx] = (kq[d][CH:] * dmat[d]).astype(BF16)
        yield

    per_stage = SOLVE_LEVELS_PER_STAGE
    n_stages = (N_LEVELS - 1) // per_stage
    solve_stages = tuple((2 + s, tuple(range(1 + s * per_stage, 1 + (s + 1) * per_stage)))
                         for s in range(n_stages))
    ops_lag = n_stages + 2

    def step(it, stage_one, stage_two):
        fillers = []
        if stage_two:
            c2 = it - 1
            ok = jnp.logical_and(c2 >= 0, c2 < nc)
            cq = jnp.where(ok, c2, nc)
            fillers.append(prep_two(*tiles(it, 1), qkv_ref[cq], gate_ref[cq]))
        loaded = []
        for lag, lvs in solve_stages:
            jf, jb = tiles(it, lag)
            loaded.append((jf, jb, t_ref[jf], t_ref[jb], a_ref[jf], a_ref[jb]))
        ops_in = [(i, t_ref[i], rhs_ref[i], kd_ref[i], qk_ref[i], qd_ref[i]) for i in tiles(it, ops_lag)]
        if stage_one:
            fillers.insert(0, prep_one(it))

        def fill():
            for f in fillers:
                next(f, None)

        uws = [_dot(t, rhs) for i, t, rhs, kd, qkm, qd in ops_in]
        for half in range(per_stage):
            xs = []
            for (lag, lvs), (jf, jb, tf, tb, af, ab) in zip(solve_stages, loaded):
                m = lm_ref[lvs[half]]
                xs.append(_dot(jnp.concatenate([tf, tb], axis=1), pair(af * m, ab * m)))
            fill()
            fill()
            ys = [_dot(x.astype(BF16), pair(tf, tb)) for x, (jf, jb, tf, tb, af, ab) in zip(xs, loaded)]
            if half == 0:
                uws = [uw.astype(BF16) for uw in uws]
                kuws = [_dot_tn(kd, uw) for uw, (i, t, rhs, kd, qkm, qd) in zip(uws, ops_in)]
                quws = [_dot(qkm, uw) for uw, (i, t, rhs, kd, qkm, qd) in zip(uws, ops_in)]
            fill()
            fill()
            loaded = [(jf, jb, tf - y[:, 0:CH].astype(BF16), tb - y[:, CH:].astype(BF16), af, ab)
                      for y, (jf, jb, tf, tb, af, ab) in zip(ys, loaded)]
        for f in fillers:
            for _ in f:
                pass
        for jf, jb, tf, tb, _, _ in loaded:
            t_ref[jf] = tf
            t_ref[jb] = tb
        for kuw, quw, (i, t, rhs, kd, qkm, qd) in zip(kuws, quws, ops_in):
            qc_ref[i] = kuw[:, 0:A_DV]
            mc_ref[i] = kuw[:, A_DV:].astype(BF16)
            oc_ref[i] = quw[:, 0:A_DV]
            rc_ref[i] = (qd - quw[:, A_DV:]).astype(BF16)

    def loop(lo, hi, stage_one, stage_two):
        def body(it, carry):
            step(it, stage_one, stage_two)
            return carry
        lax.fori_loop(lo, hi, body, 0)

    loop(0, nc, True, True)
    loop(nc, nc + 1, False, True)
    loop(nc + 1, nc + ops_lag, False, False)

    def scan(c, carry):
        sf, sb = carry
        cf = c
        cb = 2 * nc - 1 - c
        of = _dot(rc_ref[cf], sf.astype(BF16)) + oc_ref[cf]
        ob = _dot(rc_ref[cb], sb.astype(BF16)) + oc_ref[cb]
        oc_ref[cf] = of
        oc_ref[cb] = ob
        sf = gl_ref[cf][0:1, :] * sf - _dot(mc_ref[cf], sf.astype(BF16)) + qc_ref[cf]
        sb = gl_ref[cb][0:1, :] * sb - _dot(mc_ref[cb], sb.astype(BF16)) + qc_ref[cb]
        return sf, sb

    zero = jnp.zeros((A_DK, A_DV), F32)
    lax.fori_loop(0, nc, scan, (zero, zero))

    ng = ng_ref[...]
    fg = _group(nc, FIN_GROUP)

    def fin(it, carry):
        cs = [it * fg + u for u in range(fg)]
        rows = [pl.ds(pl.multiple_of(c * CH, CH), CH) for c in cs]
        os_ = [oc_ref[c] + oc_ref[nc + c] for c in cs]
        inv = [lax.rsqrt(jnp.mean(o * o, axis=-1, keepdims=True) + RMS_EPS) for o in os_]
        for c, r, o, s in zip(cs, rows, os_, inv):
            z = p_ref[pl.ds(pl.multiple_of(c * CH, CH) + HALO, CH), 3 * A_DK:4 * A_DK]
            o_ref[0, r, :] = (o * s * ng * _silu(z)).astype(BF16)
        return carry

    lax.fori_loop(0, nc // fg, fin, 0)


def _gdn_level_masks():
    idx = np.arange(CH)
    x = idx[:, None] ^ idx[None, :]
    return jnp.asarray(np.stack([(x >> lv) == 1 for lv in range(N_LEVELS)]), BF16)


def _conv_shift_matrices():
    t = np.arange(CH)[:, None]
    r = np.arange(CH + 2 * HALO)[None, :]
    taps = CONV_MXU_TAPS or (0,)
    return jnp.asarray(np.concatenate([r == t + HALO + i - A_CONV // 2 for i in taps], axis=0), BF16)


def _gdn_mixer(xb, w_in, conv_w, a_log, dt_bias, norm_g):
    bn, seq, dm = xb.shape
    h, dk = A_HEADS, A_DK
    nc = seq // CH
    w = w_in
    hw = h * dk
    ba = w[:, 4 * hw:].reshape(dm, 2, 2, h)
    per_head = [w[:, i * hw:(i + 1) * hw].reshape(dm, h, dk).transpose(1, 0, 2) for i in range(4)]
    gate_cols = jnp.pad(ba.reshape(dm, 4, h).transpose(2, 0, 1), ((0, 0), (0, 0), (0, dk - 4)))
    wh = jnp.concatenate(per_head + [gate_cols], axis=2).astype(BF16)
    cw = conv_w.reshape(A_CONV, 3, h, dk).transpose(2, 0, 1, 3).reshape(h, A_CONV, 3 * dk)
    cw = jnp.pad(cw, ((0, 0), (0, 8 - A_CONV), (0, 0))).astype(F32)
    scale = jnp.zeros((h, GATE_ROWS), F32).at[:, 2:4].set((jnp.exp(a_log.astype(F32)) * LOG2E).T)
    bias = jnp.zeros((h, GATE_ROWS), F32).at[:, 2:4].set(dt_bias.astype(F32).T)
    hp = jnp.broadcast_to(jnp.stack([scale, bias], axis=1)[:, :, :, None], (h, 2, GATE_ROWS, dk))
    ng = norm_g.astype(F32).reshape(1, A_DV)
    lm = _gdn_level_masks()
    sh = _conv_shift_matrices()
    nw = wh.shape[2]

    kern = functools.partial(_gdn_kernel, seq=seq)
    tile = lambda dt: pltpu.VMEM((2 * nc + 1, CH, CH), dt)
    return pl.pallas_call(
        kern,
        out_shape=jax.ShapeDtypeStruct((bn, seq, h * A_DV), BF16),
        grid=(bn, h),
        in_specs=[
            pl.BlockSpec((1, seq, dm), lambda b, i: (b, 0, 0)),
            pl.BlockSpec((1, dm, nw), lambda b, i: (i, 0, 0)),
            pl.BlockSpec((1, 8, 3 * dk), lambda b, i: (i, 0, 0)),
            pl.BlockSpec((1, 2, GATE_ROWS, dk), lambda b, i: (i, 0, 0, 0)),
            pl.BlockSpec((1, A_DV), lambda b, i: (0, 0)),
            pl.BlockSpec(lm.shape, lambda b, i: (0, 0, 0)),
            pl.BlockSpec(sh.shape, lambda b, i: (0, 0)),
        ],
        out_specs=pl.BlockSpec((1, seq, A_DV), lambda b, i: (b, 0, i)),
        scratch_shapes=[
            pltpu.VMEM((seq + 2 * HALO, nw), F32),
            pltpu.VMEM((nc + 1, CH, 3 * A_DK), F32),
            pltpu.VMEM((nc + 1, 3 * GATE_ROWS, CH), F32),
            tile(BF16),
            tile(BF16),
            tile(BF16),
            pltpu.VMEM((2 * nc + 1, CH, A_DV + A_DK), BF16),
            tile(F32),
            tile(BF16),
            pltpu.VMEM((2 * nc + 1, 8, A_DK), F32),
            tile(BF16),
            tile(F32),
            tile(BF16),
            tile(F32),
        ],
        compiler_params=pltpu.CompilerParams(
            dimension_semantics=("arbitrary", "arbitrary"), vmem_limit_bytes=VMEM_LIMIT),
        name="gdn_mixer",
    )(xb, wh, cw, hp, ng, lm, sh)


def _gla_tables():
    i = np.arange(CH)[:, None]
    t = np.arange(CH)[None, :]
    seg = np.zeros((2, N_LEVELS + 1, CH, CH), np.float32)
    lvl = np.zeros((2, CH, CH), np.int32)
    for d in range(2):
        rev = d == 1
        seg[d, 0] = (t >= i) if rev else (t <= i)
        lv = np.full((CH, CH), N_LEVELS + 1, np.int32)
        lv[np.arange(CH), np.arange(CH)] = N_LEVELS
        x = i ^ t
        for l in range(N_LEVELS):
            h = 2 ** (N_LEVELS - 1 - l)
            b0 = (i // (2 * h)) * (2 * h)
            if rev:
                r = b0 + h
                late = i < r
                m = np.where(late, (t >= i) & (t < r), (t >= r) & (t < i))
                own = ((x >> (N_LEVELS - 1 - l)) == 1) & (t > i)
            else:
                r = b0 + h - 1
                late = i > r
                m = np.where(late, (t > r) & (t <= i), (t > i) & (t <= r))
                own = ((x >> (N_LEVELS - 1 - l)) == 1) & (t < i)
            seg[d, 1 + l] = m
            lv[own] = l
        lvl[d] = lv
    return jnp.asarray(seg, BF16), jnp.asarray(lvl)


def _gla_kernel(xb_ref, wh_ref, w2_ref, gb_ref, ng_ref, seg_ref, lvl_ref, o_ref,
                p_ref, qs_ref, kv_ref, st_ref, dec_ref, oi_ref, *, seq):
    nc = seq // CH
    dk, dv = B_DK, B_DV
    p_ref[...] = _dot(xb_ref[0], wh_ref[0])

    c_q, c_k, c_v, c_r, c_g = 0, dk, 2 * dk, 2 * dk + dv, 2 * dk + 2 * dv
    gg = _group(nc, GLA_GROUP)
    lanes = [(u, d) for u in range(gg) for d in range(2)]

    def prep(it, carry):
        cs = [it * gg + u for u in range(gg)]
        rows = [pl.ds(pl.multiple_of(c * CH, CH), CH) for c in cs]
        q = [p_ref[r, c_q:c_q + dk] * (dk ** -0.5) for r in rows]
        k = [p_ref[r, c_k:c_k + dk] for r in rows]
        q16 = [x.astype(BF16) for x in q]
        k16 = [x.astype(BF16) for x in k]
        v16 = [p_ref[r, c_v:c_v + dv].astype(BF16) for r in rows]
        gin = [p_ref[r, c_g:c_g + dk].astype(BF16) for r in rows]
        logit = [_dot(gin[u], w2_ref[0, d]) + gb_ref[0, d][0:1, :] for u, d in lanes]
        la3 = [_split(-_softplus(-x) * (LOG2E / B_TAU), 3) for x in logit]
        la2 = [y[:, 0:GLA_PIECES * dk] for y in la3]
        bc = [_fold(_dot(seg_ref[d, 0], y), 3) for (u, d), y in zip(lanes, la3)]

        def level_sums(l):
            h = CH >> (l + 1)
            if h < HALO:
                return [_fold(_dot(seg_ref[d, 1 + l], y), GLA_PIECES) for (u, d), y in zip(lanes, la2)]
            out = []
            for (u, d), b in zip(lanes, bc):
                blocks = []
                for lo in range(0, CH, 2 * h):
                    if d == 1:
                        ref = b[lo + h:lo + h + 1, :]
                        blocks += [b[lo:lo + h, :] - ref, ref - b[lo + h:lo + 2 * h, :]]
                    else:
                        ref = b[lo + h - 1:lo + h, :]
                        blocks += [ref - b[lo:lo + h, :], b[lo + h:lo + 2 * h, :] - ref]
                out.append(jnp.concatenate(blocks, axis=0))
            return out

        half = CH // 2
        zero_half = jnp.zeros((half, dk), BF16)

        def top_level(x, e, d, late):
            upper = (d == 1) != late
            rows = slice(half, CH) if upper else slice(0, half)
            kept = (x[rows] * e[rows]).astype(BF16)
            return jnp.concatenate([zero_half, kept] if upper else [kept, zero_half], axis=0)

        ahead = [level_sums(l) for l in range(GLA_AHEAD)]
        scores = prod = None
        for l in range(N_LEVELS):
            if l + GLA_AHEAD < N_LEVELS:
                ahead.append(level_sums(l + GLA_AHEAD))
            e = [jnp.exp2(x) for x in ahead[l]]
            if l == 0:
                ql = [top_level(q[u], x, d, True) for (u, d), x in zip(lanes, e)]
                kl = [top_level(k[u], x, d, False) for (u, d), x in zip(lanes, e)]
            else:
                ql = [(q[u] * x).astype(BF16) for (u, d), x in zip(lanes, e)]
                kl = [(k[u] * x).astype(BF16) for (u, d), x in zip(lanes, e)]
            if l == 1:
                scores = prod
            elif l > 1:
                own = [lvl_ref[d] == l - 1 for d in range(2)]
                scores = [jnp.where(own[d], p, s) for (u, d), p, s in zip(lanes, prod, scores)]
            prod = [_dot_nt(a, b) for a, b in zip(ql, kl)]
        own = [lvl_ref[d] == N_LEVELS - 1 for d in range(2)]
        scores = [jnp.where(own[d], p, s) for (u, d), p, s in zip(lanes, prod, scores)]
        diag = [_dot_nt(q16[u], k16[u]) for u in range(gg)]
        own = [lvl_ref[d] == N_LEVELS for d in range(2)]
        scores = [jnp.where(own[d], diag[u], s) for (u, d), s in zip(lanes, scores)]
        for (u, d), s, b in zip(lanes, scores, bc):
            idx = d * nc + cs[u]
            btot = b[0:1, :] if d == 1 else b[CH - 1:CH, :]
            oi_ref[idx] = _dot(s.astype(BF16), v16[u])
            qs_ref[idx] = (q[u] * jnp.exp2(b)).astype(BF16)
            kv_ref[idx] = _dot_tn((k[u] * jnp.exp2(btot - b)).astype(BF16), v16[u])
            dec_ref[idx] = jnp.exp2(jnp.broadcast_to(btot, (CH, dk)).T)
        return carry

    lax.fori_loop(0, nc // gg, prep, 0)

    def scan(c, carry):
        sf, sb = carry
        cf = c
        ib = 2 * nc - 1 - c
        st_ref[cf] = sf.astype(BF16)
        st_ref[ib] = sb.astype(BF16)
        ef = dec_ref[cf]
        eb = dec_ref[ib]
        sf = sf * jnp.concatenate([ef, ef], axis=1) + kv_ref[cf]
        sb = sb * jnp.concatenate([eb, eb], axis=1) + kv_ref[ib]
        return sf, sb

    zero = jnp.zeros((dk, dv), F32)
    lax.fori_loop(0, nc, scan, (zero, zero))

    ng = ng_ref[...]
    fg = _group(nc, FIN_GROUP)

    def fin(it, carry):
        cs = [it * fg + u for u in range(fg)]
        rows = [pl.ds(pl.multiple_of(c * CH, CH), CH) for c in cs]
        os_ = [oi_ref[c] + oi_ref[nc + c] + _dot(qs_ref[c], st_ref[c]) + _dot(qs_ref[nc + c], st_ref[nc + c])
               for c in cs]
        inv = [lax.rsqrt(jnp.mean(o * o, axis=-1, keepdims=True) + RMS_EPS) for o in os_]
        for r, o, s in zip(rows, os_, inv):
            o_ref[0, r, :] = (o * s * ng * _silu(p_ref[r, c_r:c_r + dv])).astype(BF16)
        return carry

    lax.fori_loop(0, nc // fg, fin, 0)


def _gla_mixer(xb, w_in, gate_w2, gate_b, norm_g):
    bn, seq, dm = xb.shape
    h, dk, dv = B_HEADS, B_DK, B_DV
    nc = seq // CH
    kw, vw = h * dk, h * dv
    w = w_in

    def heads(cols, width):
        return cols.reshape(dm, h, width).transpose(1, 0, 2)

    gl = jnp.pad(w[:, 2 * kw + 2 * vw:], ((0, 0), (0, dk - 2 * B_RANK)))
    wh = jnp.concatenate([
        heads(w[:, 0:kw], dk), heads(w[:, kw:2 * kw], dk),
        heads(w[:, 2 * kw:2 * kw + vw], dv), heads(w[:, 2 * kw + vw:2 * kw + 2 * vw], dv),
        jnp.broadcast_to(gl[None], (h, dm, dk))], axis=2).astype(BF16)
    w2 = gate_w2.reshape(2, B_RANK, h, dk).transpose(2, 0, 1, 3)
    w2p = jnp.zeros((h, 2, dk, dk), F32)
    w2p = w2p.at[:, 0, 0:B_RANK].set(w2[:, 0]).at[:, 1, B_RANK:2 * B_RANK].set(w2[:, 1]).astype(BF16)
    gb = gate_b.reshape(2, h, dk).transpose(1, 0, 2).astype(F32)
    gb = jnp.broadcast_to(gb[:, :, None, :], (h, 2, 8, dk))
    ng = norm_g.astype(F32).reshape(1, dv)
    seg, lvl = _gla_tables()
    nw = wh.shape[2]

    kern = functools.partial(_gla_kernel, seq=seq)
    return pl.pallas_call(
        kern,
        out_shape=jax.ShapeDtypeStruct((bn, seq, vw), BF16),
        grid=(bn, h),
        in_specs=[
            pl.BlockSpec((1, seq, dm), lambda b, i: (b, 0, 0)),
            pl.BlockSpec((1, dm, nw), lambda b, i: (i, 0, 0)),
            pl.BlockSpec((1, 2, dk, dk), lambda b, i: (i, 0, 0, 0)),
            pl.BlockSpec((1, 2, 8, dk), lambda b, i: (i, 0, 0, 0)),
            pl.BlockSpec((1, dv), lambda b, i: (0, 0)),
            pl.BlockSpec(seg.shape, lambda b, i: (0, 0, 0, 0)),
            pl.BlockSpec(lvl.shape, lambda b, i: (0, 0, 0)),
        ],
        out_specs=pl.BlockSpec((1, seq, dv), lambda b, i: (b, 0, i)),
        scratch_shapes=[
            pltpu.VMEM((seq, nw), F32),
            pltpu.VMEM((2 * nc, CH, dk), BF16),
            pltpu.VMEM((2 * nc, dk, dv), F32),
            pltpu.VMEM((2 * nc, dk, dv), BF16),
            pltpu.VMEM((2 * nc, dk, dk), F32),
            pltpu.VMEM((2 * nc, CH, dv), F32),
        ],
        compiler_params=pltpu.CompilerParams(
            dimension_semantics=("arbitrary", "arbitrary"), vmem_limit_bytes=VMEM_LIMIT),
        name="gla_mixer",
    )(xb, wh, w2p, gb, ng, seg, lvl)


def _post_kernel(o_ref, x_ref, wo_ref, w1_ref, w2_ref, ln_ref, y_ref, yb_ref, *, alpha):
    ln = ln_ref[...]
    x = x_ref[...]
    x1 = _layernorm(alpha * x + _dot(o_ref[...], wo_ref[...]), ln[0:1, :], ln[1:2, :])
    x1b = x1.astype(BF16)
    acc = jnp.zeros(x.shape, F32)
    dff = w1_ref.shape[1]
    for j in range(dff // FF_TILE):
        cols = slice(j * FF_TILE, (j + 1) * FF_TILE)
        hcur = jnp.maximum(_dot(x1b, w1_ref[:, cols]), 0.0)
        acc = acc + _dot((hcur * hcur).astype(BF16), w2_ref[cols, :])
    y = _layernorm(alpha * x1 + acc, ln[2:3, :], ln[3:4, :])
    y_ref[...] = y
    yb_ref[...] = y.astype(BF16)


def _post(o, x, w_out, w1, w2, g1, b1, g2, b2, alpha):
    t, dm = x.shape
    vw = o.shape[1]
    dff = w1.shape[1]
    tm = min(ROW_TILE, t)
    ln = jnp.pad(jnp.stack([g1, b1, g2, b2]).astype(F32), ((0, 4), (0, 0)))
    const = lambda shape: pl.BlockSpec(shape, lambda i: (0, 0), pipeline_mode=pl.Buffered(1))
    return pl.pallas_call(
        functools.partial(_post_kernel, alpha=alpha),
        out_shape=(jax.ShapeDtypeStruct((t, dm), F32), jax.ShapeDtypeStruct((t, dm), BF16)),
        grid=(t // tm,),
        in_specs=[
            pl.BlockSpec((tm, vw), lambda i: (i, 0)),
            pl.BlockSpec((tm, dm), lambda i: (i, 0)),
            const((vw, dm)), const((dm, dff)), const((dff, dm)), const((8, dm)),
        ],
        out_specs=(pl.BlockSpec((tm, dm), lambda i: (i, 0)), pl.BlockSpec((tm, dm), lambda i: (i, 0))),
        compiler_params=pltpu.CompilerParams(
            dimension_semantics=("arbitrary",), vmem_limit_bytes=VMEM_LIMIT),
        name="post",
    )(o, x, w_out.astype(BF16), w1.astype(BF16), w2.astype(BF16), ln)


def kernel(x, a_w_in, a_conv, a_alog, a_dt_bias, a_norm_g, a_w_out, b_w_in, b_gate_w2, b_gate_b,
           b_norm_g, b_w_out, ln1_g, ln1_b, mlp_w1, mlp_w2, ln2_g, ln2_b):
    bn, seq, dm = x.shape
    depth = ln1_g.shape[0]
    alpha = (2 * depth) ** 0.25
    xf = x.astype(F32).reshape(bn * seq, dm)
    xb = xf.astype(BF16)
    for i in range(depth):
        j = i // 2
        xb3 = xb.reshape(bn, seq, dm)
        if i % 2 == 0:
            o = _gdn_mixer(xb3, a_w_in[j], a_conv[j], a_alog[j], a_dt_bias[j], a_norm_g[j])
            w_out = a_w_out[j]
        else:
            o = _gla_mixer(xb3, b_w_in[j], b_gate_w2[j], b_gate_b[j], b_norm_g[j])
            w_out = b_w_out[j]
        xf, xb = _post(o.reshape(bn * seq, -1), xf, w_out, mlp_w1[i], mlp_w2[i],
                       ln1_g[i], ln1_b[i], ln2_g[i], ln2_b[i], alpha)
    return xf.reshape(bn, seq, dm).astype(x.dtype)
```

```python
import functools
import math

import numpy as np

import jax
import jax.numpy as jnp
from jax import lax
from jax.experimental import pallas as pl
from jax.experimental.pallas import tpu as pltpu

F32 = jnp.float32
BF16 = jnp.bfloat16

A_HEADS, A_DK, A_DV, A_CONV = 8, 128, 128, 5
B_HEADS, B_DK, B_DV, B_RANK, B_TAU = 4, 128, 256, 16, 16.0
LN_EPS, RMS_EPS, L2_EPS = 1e-5, 1e-6, 1e-6

CH = 128
N_LEVELS = 7
HALO = 8
CONV_MXU_TAPS = (0, 4)
GATE_ROWS = 16
SOLVE_LEVELS_PER_STAGE = 2
GLA_GROUP = 4
GLA_AHEAD = 2
GLA_PIECES = 2
FIN_GROUP = 4
NEG_BIG = -1e30
LOG2E = math.log2(math.e)
VMEM_LIMIT = 56 * 1024 * 1024
ROW_TILE = 1024
FF_TILE = 1024

assert CH == A_DK == B_DK and 2 ** N_LEVELS == CH


def _dot(a, b):
    return jnp.dot(a, b, preferred_element_type=F32)


def _dot_nt(a, b):
    return lax.dot_general(a, b, (((1,), (1,)), ((), ())), preferred_element_type=F32)


def _dot_tn(a, b):
    return lax.dot_general(a, b, (((0,), (0,)), ((), ())), preferred_element_type=F32)


def _split(x, n, axis=1):
    pieces = []
    for _ in range(n - 1):
        p = x.astype(BF16)
        pieces.append(p)
        x = x - p.astype(F32)
    pieces.append(x.astype(BF16))
    return jnp.concatenate(pieces, axis=axis)


def _fold(y, n, axis=1):
    w = y.shape[axis] // n
    blocks = [lax.slice_in_dim(y, i * w, (i + 1) * w, axis=axis) for i in range(n)]
    out = blocks[0]
    for b in blocks[1:]:
        out = out + b
    return out


def _dot_exact(m01, x):
    return _fold(_dot(m01, _split(x, 3)), 3)


def _sigmoid(x):
    return 0.5 + 0.5 * jnp.tanh(0.5 * x)


def _silu(x):
    h = 0.5 * x
    return h + h * jnp.tanh(h)


def _softplus(x):
    return jnp.maximum(x, 0.0) + jnp.log(1.0 + jnp.exp(-jnp.abs(x)))


def _layernorm(y, g, b):
    mu = jnp.mean(y, axis=-1, keepdims=True)
    yc = y - mu
    var = jnp.mean(yc * yc, axis=-1, keepdims=True)
    return yc * lax.rsqrt(var + LN_EPS) * g + b


def _order_masks(rev):
    row = lax.broadcasted_iota(jnp.int32, (CH, CH), 0)
    col = lax.broadcasted_iota(jnp.int32, (CH, CH), 1)
    if rev:
        return col >= row, col > row
    return col <= row, col < row


def _group(n, want):
    return math.gcd(n, want)


def _gdn_kernel(xb_ref, wh_ref, cw_ref, hp_ref, ng_ref, lm_ref, sh_ref, o_ref,
                p_ref, qkv_ref, gate_ref, a_ref, t_ref, qk_ref, rhs_ref, qd_ref, kd_ref, gl_ref,
                mc_ref, qc_ref, rc_ref, oc_ref, *, seq):
    nc = seq // CH
    spare = 2 * nc

    p_ref[0:HALO, :] = jnp.zeros((HALO, p_ref.shape[1]), F32)
    p_ref[HALO + seq:, :] = jnp.zeros((HALO, p_ref.shape[1]), F32)
    p_ref[HALO:HALO + seq, :] = _dot(xb_ref[0], wh_ref[0])
    for ref in (a_ref, t_ref, qk_ref, kd_ref, rhs_ref, qd_ref):
        ref[spare] = jnp.zeros(ref.shape[1:], ref.dtype)
    qkv_ref[nc] = jnp.zeros(qkv_ref.shape[1:], F32)
    gate_ref[nc] = jnp.zeros(gate_ref.shape[1:], F32)

    cw = cw_ref[0]
    hp = hp_ref[0]
    zero16 = jnp.zeros((CH, CH), BF16)

    def pair(f, b):
        return jnp.concatenate([jnp.concatenate([f, zero16], axis=1),
                                jnp.concatenate([zero16, b], axis=1)], axis=0)

    def tiles(it, lag):
        c = it - lag
        ok = jnp.logical_and(c >= 0, c < nc)
        return jnp.where(ok, c, spare), jnp.where(ok, nc + c, spare)

    def prep_one(c):
        base = pl.multiple_of(c * CH, CH)
        win = p_ref[pl.ds(base, CH + 2 * HALO), 0:3 * A_DK]
        gates = p_ref[pl.ds(base + HALO, CH), 4 * A_DK:5 * A_DK].T[0:GATE_ROWS, :]
        if CONV_MXU_TAPS:
            shifted = _dot(sh_ref[...], win.astype(BF16))
        mid = A_CONV // 2
        acc = win[HALO:HALO + CH, :] * cw[mid:mid + 1, :]
        for i in range(A_CONV):
            if i != mid and i not in CONV_MXU_TAPS:
                off = HALO + i - mid
                acc = acc + win[off:off + CH, :] * cw[i:i + 1, :]
        g_rows = -hp[0] * _softplus(gates + hp[1])
        pieces = _split(g_rows, 3, axis=0)
        gcum = [_fold(_dot(pieces, _order_masks(not rev)[0].astype(BF16)), 3, axis=0) for rev in (False, True)]
        yield
        for n, i in enumerate(CONV_MXU_TAPS):
            acc = acc + shifted[n * CH:(n + 1) * CH, :] * cw[i:i + 1, :]
        yield
        s = _silu(acc)
        q = s[:, 0:A_DK]
        k = s[:, A_DK:2 * A_DK]
        q = q * (lax.rsqrt(jnp.sum(q * q, axis=-1, keepdims=True) + L2_EPS) * (A_DK ** -0.5))
        k = k * lax.rsqrt(jnp.sum(k * k, axis=-1, keepdims=True) + L2_EPS)
        yield
        qkv_ref[c] = jnp.concatenate([q, k, s[:, 2 * A_DK:]], axis=1)
        gate_ref[c] = jnp.concatenate([_sigmoid(gates), gcum[0], gcum[1]], axis=0)
        yield

    def prep_two(jf, jb, qkv, gate):
        q = qkv[:, 0:A_DK]
        k = qkv[:, A_DK:2 * A_DK]
        v = qkv[:, 2 * A_DK:]
        dirs = ((0, False, jf), (1, True, jb))
        gr = [jnp.broadcast_to(gate[(1 + d) * GATE_ROWS + 2 + d:(1 + d) * GATE_ROWS + 3 + d, :], (CH, CH))
              for d, _, _ in dirs]
        gc = [x.T for x in gr]
        beta = [jnp.broadcast_to(gate[d:d + 1, :], (CH, CH)).T for d, _, _ in dirs]
        kb16 = k.astype(BF16)
        eye = (lax.broadcasted_iota(jnp.int32, (CH, CH), 0)
               == lax.broadcasted_iota(jnp.int32, (CH, CH), 1)).astype(F32)
        yield
        kbeta = [k * beta[d] for d, _, _ in dirs]
        kq = [_dot_nt(jnp.concatenate([kbeta[d], q], axis=0).astype(BF16), kb16) for d, _, _ in dirs]
        gtot = [gc[d][0:1, :] if rev else gc[d][CH - 1:CH, :] for d, rev, _ in dirs]
        eg = [jnp.exp2(gc[d]) for d, _, _ in dirs]
        dmat = [jnp.exp2(jnp.where(_order_masks(rev)[0], gc[d] - gr[d], NEG_BIG)) for d, rev, _ in dirs]
        yield
        for d, rev, idx in dirs:
            rhs_ref[idx] = jnp.concatenate([v * beta[d], kbeta[d] * eg[d]], axis=1).astype(BF16)
            qd_ref[idx] = q * eg[d]
            kd_ref[idx] = (k * jnp.exp2(gtot[d] - gc[d])).astype(BF16)
            gl_ref[idx] = jnp.broadcast_to(jnp.exp2(gtot[d]), (8, A_DK))
        yield
        for d, rev, idx in dirs:
            a = jnp.where(_order_masks(rev)[1], kq[d][0:CH] * dmat[d], 0.0).astype(BF16)
            a_ref[idx] = a
            t_ref[idx] = (eye - (a * lm_ref[0]).astype(F32)).astype(BF16)
            qk_ref[idx] = (kq[d][CH:] * dmat[d]).astype(BF16)
        yield

    per_stage = SOLVE_LEVELS_PER_STAGE
    n_stages = (N_LEVELS - 1) // per_stage
    solve_stages = tuple((2 + s, tuple(range(1 + s * per_stage, 1 + (s + 1) * per_stage)))
                         for s in range(n_stages))
    ops_lag = n_stages + 2

    def step(it, stage_one, stage_two):
        fillers = []
        if stage_two:
            c2 = it - 1
            ok = jnp.logical_and(c2 >= 0, c2 < nc)
            cq = jnp.where(ok, c2, nc)
            fillers.append(prep_two(*tiles(it, 1), qkv_ref[cq], gate_ref[cq]))
        loaded = []
        for lag, lvs in solve_stages:
            jf, jb = tiles(it, lag)
            loaded.append((jf, jb, t_ref[jf], t_ref[jb], a_ref[jf], a_ref[jb]))
        ops_in = [(i, t_ref[i], rhs_ref[i], kd_ref[i], qk_ref[i], qd_ref[i]) for i in tiles(it, ops_lag)]
        if stage_one:
            fillers.insert(0, prep_one(it))

        def fill():
            for f in fillers:
                next(f, None)

        uws = [_dot(t, rhs) for i, t, rhs, kd, qkm, qd in ops_in]
        for half in range(per_stage):
            xs = []
            for (lag, lvs), (jf, jb, tf, tb, af, ab) in zip(solve_stages, loaded):
                m = lm_ref[lvs[half]]
                xs.append(_dot(jnp.concatenate([tf, tb], axis=1), pair(af * m, ab * m)))
            fill()
            fill()
            ys = [_dot(x.astype(BF16), pair(tf, tb)) for x, (jf, jb, tf, tb, af, ab) in zip(xs, loaded)]
            if half == 0:
                uws = [uw.astype(BF16) for uw in uws]
                kuws = [_dot_tn(kd, uw) for uw, (i, t, rhs, kd, qkm, qd) in zip(uws, ops_in)]
                quws = [_dot(qkm, uw) for uw, (i, t, rhs, kd, qkm, qd) in zip(uws, ops_in)]
            fill()
            fill()
            loaded = [(jf, jb, tf - y[:, 0:CH].astype(BF16), tb - y[:, CH:].astype(BF16), af, ab)
                      for y, (jf, jb, tf, tb, af, ab) in zip(ys, loaded)]
        for f in fillers:
            for _ in f:
                pass
        for jf, jb, tf, tb, _, _ in loaded:
            t_ref[jf] = tf
            t_ref[jb] = tb
        for kuw, quw, (i, t, rhs, kd, qkm, qd) in zip(kuws, quws, ops_in):
            qc_ref[i] = kuw[:, 0:A_DV]
            mc_ref[i] = kuw[:, A_DV:].astype(BF16)
            oc_ref[i] = quw[:, 0:A_DV]
            rc_ref[i] = (qd - quw[:, A_DV:]).astype(BF16)

    def loop(lo, hi, stage_one, stage_two):
        def body(it, carry):
            step(it, stage_one, stage_two)
            return carry
        lax.fori_loop(lo, hi, body, 0)

    loop(0, nc, True, True)
    loop(nc, nc + 1, False, True)
    loop(nc + 1, nc + ops_lag, False, False)

    def scan(c, carry):
        sf, sb = carry
        cf = c
        cb = 2 * nc - 1 - c
        of = _dot(rc_ref[cf], sf.astype(BF16)) + oc_ref[cf]
        ob = _dot(rc_ref[cb], sb.astype(BF16)) + oc_ref[cb]
        oc_ref[cf] = of
        oc_ref[cb] = ob
        sf = gl_ref[cf][0:1, :] * sf - _dot(mc_ref[cf], sf.astype(BF16)) + qc_ref[cf]
        sb = gl_ref[cb][0:1, :] * sb - _dot(mc_ref[cb], sb.astype(BF16)) + qc_ref[cb]
        return sf, sb

    zero = jnp.zeros((A_DK, A_DV), F32)
    lax.fori_loop(0, nc, scan, (zero, zero))

    ng = ng_ref[...]
    fg = _group(nc, FIN_GROUP)

    def fin(it, carry):
        cs = [it * fg + u for u in range(fg)]
        rows = [pl.ds(pl.multiple_of(c * CH, CH), CH) for c in cs]
        os_ = [oc_ref[c] + oc_ref[nc + c] for c in cs]
        inv = [lax.rsqrt(jnp.mean(o * o, axis=-1, keepdims=True) + RMS_EPS) for o in os_]
        for c, r, o, s in zip(cs, rows, os_, inv):
            z = p_ref[pl.ds(pl.multiple_of(c * CH, CH) + HALO, CH), 3 * A_DK:4 * A_DK]
            o_ref[0, r, :] = (o * s * ng * _silu(z)).astype(BF16)
        return carry

    lax.fori_loop(0, nc // fg, fin, 0)


def _gdn_level_masks():
    idx = np.arange(CH)
    x = idx[:, None] ^ idx[None, :]
    return jnp.asarray(np.stack([(x >> lv) == 1 for lv in range(N_LEVELS)]), BF16)


def _conv_shift_matrices():
    t = np.arange(CH)[:, None]
    r = np.arange(CH + 2 * HALO)[None, :]
    taps = CONV_MXU_TAPS or (0,)
    return jnp.asarray(np.concatenate([r == t + HALO + i - A_CONV // 2 for i in taps], axis=0), BF16)


def _gdn_mixer(xb, w_in, conv_w, a_log, dt_bias, norm_g):
    bn, seq, dm = xb.shape
    h, dk = A_HEADS, A_DK
    nc = seq // CH
    w = w_in
    hw = h * dk
    ba = w[:, 4 * hw:].reshape(dm, 2, 2, h)
    per_head = [w[:, i * hw:(i + 1) * hw].reshape(dm, h, dk).transpose(1, 0, 2) for i in range(4)]
    gate_cols = jnp.pad(ba.reshape(dm, 4, h).transpose(2, 0, 1), ((0, 0), (0, 0), (0, dk - 4)))
    wh = jnp.concatenate(per_head + [gate_cols], axis=2).astype(BF16)
    cw = conv_w.reshape(A_CONV, 3, h, dk).transpose(2, 0, 1, 3).reshape(h, A_CONV, 3 * dk)
    cw = jnp.pad(cw, ((0, 0), (0, 8 - A_CONV), (0, 0))).astype(F32)
    scale = jnp.zeros((h, GATE_ROWS), F32).at[:, 2:4].set((jnp.exp(a_log.astype(F32)) * LOG2E).T)
    bias = jnp.zeros((h, GATE_ROWS), F32).at[:, 2:4].set(dt_bias.astype(F32).T)
    hp = jnp.broadcast_to(jnp.stack([scale, bias], axis=1)[:, :, :, None], (h, 2, GATE_ROWS, dk))
    ng = norm_g.astype(F32).reshape(1, A_DV)
    lm = _gdn_level_masks()
    sh = _conv_shift_matrices()
    nw = wh.shape[2]

    kern = functools.partial(_gdn_kernel, seq=seq)
    tile = lambda dt: pltpu.VMEM((2 * nc + 1, CH, CH), dt)
    return pl.pallas_call(
        kern,
        out_shape=jax.ShapeDtypeStruct((bn, seq, h * A_DV), BF16),
        grid=(bn, h),
        in_specs=[
            pl.BlockSpec((1, seq, dm), lambda b, i: (b, 0, 0)),
            pl.BlockSpec((1, dm, nw), lambda b, i: (i, 0, 0)),
            pl.BlockSpec((1, 8, 3 * dk), lambda b, i: (i, 0, 0)),
            pl.BlockSpec((1, 2, GATE_ROWS, dk), lambda b, i: (i, 0, 0, 0)),
            pl.BlockSpec((1, A_DV), lambda b, i: (0, 0)),
            pl.BlockSpec(lm.shape, lambda b, i: (0, 0, 0)),
            pl.BlockSpec(sh.shape, lambda b, i: (0, 0)),
        ],
        out_specs=pl.BlockSpec((1, seq, A_DV), lambda b, i: (b, 0, i)),
        scratch_shapes=[
            pltpu.VMEM((seq + 2 * HALO, nw), F32),
            pltpu.VMEM((nc + 1, CH, 3 * A_DK), F32),
            pltpu.VMEM((nc + 1, 3 * GATE_ROWS, CH), F32),
            tile(BF16),
            tile(BF16),
            tile(BF16),
            pltpu.VMEM((2 * nc + 1, CH, A_DV + A_DK), BF16),
            tile(F32),
            tile(BF16),
            pltpu.VMEM((2 * nc + 1, 8, A_DK), F32),
            tile(BF16),
            tile(F32),
            tile(BF16),
            tile(F32),
        ],
        compiler_params=pltpu.CompilerParams(
            dimension_semantics=("arbitrary", "arbitrary"), vmem_limit_bytes=VMEM_LIMIT),
        name="gdn_mixer",
    )(xb, wh, cw, hp, ng, lm, sh)


def _gla_tables():
    i = np.arange(CH)[:, None]
    t = np.arange(CH)[None, :]
    seg = np.zeros((2, N_LEVELS + 1, CH, CH), np.float32)
    lvl = np.zeros((2, CH, CH), np.int32)
    for d in range(2):
        rev = d == 1
        seg[d, 0] = (t >= i) if rev else (t <= i)
        lv = np.full((CH, CH), N_LEVELS + 1, np.int32)
        lv[np.arange(CH), np.arange(CH)] = N_LEVELS
        x = i ^ t
        for l in range(N_LEVELS):
            h = 2 ** (N_LEVELS - 1 - l)
            b0 = (i // (2 * h)) * (2 * h)
            if rev:
                r = b0 + h
                late = i < r
                m = np.where(late, (t >= i) & (t < r), (t >= r) & (t < i))
                own = ((x >> (N_LEVELS - 1 - l)) == 1) & (t > i)
            else:
                r = b0 + h - 1
                late = i > r
                m = np.where(late, (t > r) & (t <= i), (t > i) & (t <= r))
                own = ((x >> (N_LEVELS - 1 - l)) == 1) & (t < i)
            seg[d, 1 + l] = m
            lv[own] = l
        lvl[d] = lv
    return jnp.asarray(seg, BF16), jnp.asarray(lvl)


def _gla_kernel(xb_ref, wh_ref, w2_ref, gb_ref, ng_ref, seg_ref, lvl_ref, o_ref,
                p_ref, qs_ref, kv_ref, st_ref, dec_ref, oi_ref, *, seq):
    nc = seq // CH
    dk, dv = B_DK, B_DV
    p_ref[...] = _dot(xb_ref[0], wh_ref[0])

    c_q, c_k, c_v, c_r, c_g = 0, dk, 2 * dk, 2 * dk + dv, 2 * dk + 2 * dv
    gg = _group(nc, GLA_GROUP)
    lanes = [(u, d) for u in range(gg) for d in range(2)]

    def prep(it, carry):
        cs = [it * gg + u for u in range(gg)]
        rows = [pl.ds(pl.multiple_of(c * CH, CH), CH) for c in cs]
        q = [p_ref[r, c_q:c_q + dk] * (dk ** -0.5) for r in rows]
        k = [p_ref[r, c_k:c_k + dk] for r in rows]
        q16 = [x.astype(BF16) for x in q]
        k16 = [x.astype(BF16) for x in k]
        v16 = [p_ref[r, c_v:c_v + dv].astype(BF16) for r in rows]
        gin = [p_ref[r, c_g:c_g + dk].astype(BF16) for r in rows]
        logit = [_dot(gin[u], w2_ref[0, d]) + gb_ref[0, d][0:1, :] for u, d in lanes]
        la3 = [_split(-_softplus(-x) * (LOG2E / B_TAU), 3) for x in logit]
        la2 = [y[:, 0:GLA_PIECES * dk] for y in la3]
        bc = [_fold(_dot(seg_ref[d, 0], y), 3) for (u, d), y in zip(lanes, la3)]

        def level_sums(l):
            h = CH >> (l + 1)
            if h < HALO:
                return [_fold(_dot(seg_ref[d, 1 + l], y), GLA_PIECES) for (u, d), y in zip(lanes, la2)]
            out = []
            for (u, d), b in zip(lanes, bc):
                blocks = []
                for lo in range(0, CH, 2 * h):
                    if d == 1:
                        ref = b[lo + h:lo + h + 1, :]
                        blocks += [b[lo:lo + h, :] - ref, ref - b[lo + h:lo + 2 * h, :]]
                    else:
                        ref = b[lo + h - 1:lo + h, :]
                        blocks += [ref - b[lo:lo + h, :], b[lo + h:lo + 2 * h, :] - ref]
                out.append(jnp.concatenate(blocks, axis=0))
            return out

        half = CH // 2
        zero_half = jnp.zeros((half, dk), BF16)

        def top_level(x, e, d, late):
            upper = (d == 1) != late
            rows = slice(half, CH) if upper else slice(0, half)
            kept = (x[rows] * e[rows]).astype(BF16)
            return jnp.concatenate([zero_half, kept] if upper else [kept, zero_half], axis=0)

        ahead = [level_sums(l) for l in range(GLA_AHEAD)]
        scores = prod = None
        for l in range(N_LEVELS):
            if l + GLA_AHEAD < N_LEVELS:
                ahead.append(level_sums(l + GLA_AHEAD))
            e = [jnp.exp2(x) for x in ahead[l]]
            if l == 0:
                ql = [top_level(q[u], x, d, True) for (u, d), x in zip(lanes, e)]
                kl = [top_level(k[u], x, d, False) for (u, d), x in zip(lanes, e)]
            else:
                ql = [(q[u] * x).astype(BF16) for (u, d), x in zip(lanes, e)]
                kl = [(k[u] * x).astype(BF16) for (u, d), x in zip(lanes, e)]
            if l == 1:
                scores = prod
            elif l > 1:
                own = [lvl_ref[d] == l - 1 for d in range(2)]
                scores = [jnp.where(own[d], p, s) for (u, d), p, s in zip(lanes, prod, scores)]
            prod = [_dot_nt(a, b) for a, b in zip(ql, kl)]
        own = [lvl_ref[d] == N_LEVELS - 1 for d in range(2)]
        scores = [jnp.where(own[d], p, s) for (u, d), p, s in zip(lanes, prod, scores)]
        diag = [_dot_nt(q16[u], k16[u]) for u in range(gg)]
        own = [lvl_ref[d] == N_LEVELS for d in range(2)]
        scores = [jnp.where(own[d], diag[u], s) for (u, d), s in zip(lanes, scores)]
        for (u, d), s, b in zip(lanes, scores, bc):
            idx = d * nc + cs[u]
            btot = b[0:1, :] if d == 1 else b[CH - 1:CH, :]
            oi_ref[idx] = _dot(s.astype(BF16), v16[u])
            qs_ref[idx] = (q[u] * jnp.exp2(b)).astype(BF16)
            kv_ref[idx] = _dot_tn((k[u] * jnp.exp2(btot - b)).astype(BF16), v16[u])
            dec_ref[idx] = jnp.exp2(jnp.broadcast_to(btot, (CH, dk)).T)
        return carry

    lax.fori_loop(0, nc // gg, prep, 0)

    def scan(c, carry):
        sf, sb = carry
        cf = c
        ib = 2 * nc - 1 - c
        st_ref[cf] = sf.astype(BF16)
        st_ref[ib] = sb.astype(BF16)
        ef = dec_ref[cf]
        eb = dec_ref[ib]
        sf = sf * jnp.concatenate([ef, ef], axis=1) + kv_ref[cf]
        sb = sb * jnp.concatenate([eb, eb], axis=1) + kv_ref[ib]
        return sf, sb

    zero = jnp.zeros((dk, dv), F32)
    lax.fori_loop(0, nc, scan, (zero, zero))

    ng = ng_ref[...]
    fg = _group(nc, FIN_GROUP)

    def fin(it, carry):
        cs = [it * fg + u for u in range(fg)]
        rows = [pl.ds(pl.multiple_of(c * CH, CH), CH) for c in cs]
        os_ = [oi_ref[c] + oi_ref[nc + c] + _dot(qs_ref[c], st_ref[c]) + _dot(qs_ref[nc + c], st_ref[nc + c])
               for c in cs]
        inv = [lax.rsqrt(jnp.mean(o * o, axis=-1, keepdims=True) + RMS_EPS) for o in os_]
        for r, o, s in zip(rows, os_, inv):
            o_ref[0, r, :] = (o * s * ng * _silu(p_ref[r, c_r:c_r + dv])).astype(BF16)
        return carry

    lax.fori_loop(0, nc // fg, fin, 0)


def _gla_mixer(xb, w_in, gate_w2, gate_b, norm_g):
    bn, seq, dm = xb.shape
    h, dk, dv = B_HEADS, B_DK, B_DV
    nc = seq // CH
    kw, vw = h * dk, h * dv
    w = w_in

    def heads(cols, width):
        return cols.reshape(dm, h, width).transpose(1, 0, 2)

    gl = jnp.pad(w[:, 2 * kw + 2 * vw:], ((0, 0), (0, dk - 2 * B_RANK)))
    wh = jnp.concatenate([
        heads(w[:, 0:kw], dk), heads(w[:, kw:2 * kw], dk),
        heads(w[:, 2 * kw:2 * kw + vw], dv), heads(w[:, 2 * kw + vw:2 * kw + 2 * vw], dv),
        jnp.broadcast_to(gl[None], (h, dm, dk))], axis=2).astype(BF16)
    w2 = gate_w2.reshape(2, B_RANK, h, dk).transpose(2, 0, 1, 3)
    w2p = jnp.zeros((h, 2, dk, dk), F32)
    w2p = w2p.at[:, 0, 0:B_RANK].set(w2[:, 0]).at[:, 1, B_RANK:2 * B_RANK].set(w2[:, 1]).astype(BF16)
    gb = gate_b.reshape(2, h, dk).transpose(1, 0, 2).astype(F32)
    gb = jnp.broadcast_to(gb[:, :, None, :], (h, 2, 8, dk))
    ng = norm_g.astype(F32).reshape(1, dv)
    seg, lvl = _gla_tables()
    nw = wh.shape[2]

    kern = functools.partial(_gla_kernel, seq=seq)
    return pl.pallas_call(
        kern,
        out_shape=jax.ShapeDtypeStruct((bn, seq, vw), BF16),
        grid=(bn, h),
        in_specs=[
            pl.BlockSpec((1, seq, dm), lambda b, i: (b, 0, 0)),
            pl.BlockSpec((1, dm, nw), lambda b, i: (i, 0, 0)),
            pl.BlockSpec((1, 2, dk, dk), lambda b, i: (i, 0, 0, 0)),
            pl.BlockSpec((1, 2, 8, dk), lambda b, i: (i, 0, 0, 0)),
            pl.BlockSpec((1, dv), lambda b, i: (0, 0)),
            pl.BlockSpec(seg.shape, lambda b, i: (0, 0, 0, 0)),
            pl.BlockSpec(lvl.shape, lambda b, i: (0, 0, 0)),
        ],
        out_specs=pl.BlockSpec((1, seq, dv), lambda b, i: (b, 0, i)),
        scratch_shapes=[
            pltpu.VMEM((seq, nw), F32),
            pltpu.VMEM((2 * nc, CH, dk), BF16),
            pltpu.VMEM((2 * nc, dk, dv), F32),
            pltpu.VMEM((2 * nc, dk, dv), BF16),
            pltpu.VMEM((2 * nc, dk, dk), F32),
            pltpu.VMEM((2 * nc, CH, dv), F32),
        ],
        compiler_params=pltpu.CompilerParams(
            dimension_semantics=("arbitrary", "arbitrary"), vmem_limit_bytes=VMEM_LIMIT),
        name="gla_mixer",
    )(xb, wh, w2p, gb, ng, seg, lvl)


def _post_kernel(o_ref, x_ref, wo_ref, w1_ref, w2_ref, ln_ref, y_ref, yb_ref, *, alpha):
    ln = ln_ref[...]
    x = x_ref[...]
    x1 = _layernorm(alpha * x + _dot(o_ref[...], wo_ref[...]), ln[0:1, :], ln[1:2, :])
    x1b = x1.astype(BF16)
    acc = jnp.zeros(x.shape, F32)
    dff = w1_ref.shape[1]
    for j in range(dff // FF_TILE):
        cols = slice(j * FF_TILE, (j + 1) * FF_TILE)
        hcur = jnp.maximum(_dot(x1b, w1_ref[:, cols]), 0.0)
        acc = acc + _dot((hcur * hcur).astype(BF16), w2_ref[cols, :])
    y = _layernorm(alpha * x1 + acc, ln[2:3, :], ln[3:4, :])
    y_ref[...] = y
    yb_ref[...] = y.astype(BF16)


def _post(o, x, w_out, w1, w2, g1, b1, g2, b2, alpha):
    t, dm = x.shape
    vw = o.shape[1]
    dff = w1.shape[1]
    tm = min(ROW_TILE, t)
    ln = jnp.pad(jnp.stack([g1, b1, g2, b2]).astype(F32), ((0, 4), (0, 0)))
    const = lambda shape: pl.BlockSpec(shape, lambda i: (0, 0), pipeline_mode=pl.Buffered(1))
    return pl.pallas_call(
        functools.partial(_post_kernel, alpha=alpha),
        out_shape=(jax.ShapeDtypeStruct((t, dm), F32), jax.ShapeDtypeStruct((t, dm), BF16)),
        grid=(t // tm,),
        in_specs=[
            pl.BlockSpec((tm, vw), lambda i: (i, 0)),
            pl.BlockSpec((tm, dm), lambda i: (i, 0)),
            const((vw, dm)), const((dm, dff)), const((dff, dm)), const((8, dm)),
        ],
        out_specs=(pl.BlockSpec((tm, dm), lambda i: (i, 0)), pl.BlockSpec((tm, dm), lambda i: (i, 0))),
        compiler_params=pltpu.CompilerParams(
            dimension_semantics=("arbitrary",), vmem_limit_bytes=VMEM_LIMIT),
        name="post",
    )(o, x, w_out.astype(BF16), w1.astype(BF16), w2.astype(BF16), ln)


def kernel(x, a_w_in, a_conv, a_alog, a_dt_bias, a_norm_g, a_w_out, b_w_in, b_gate_w2, b_gate_b,
           b_norm_g, b_w_out, ln1_g, ln1_b, mlp_w1, mlp_w2, ln2_g, ln2_b):
    bn, seq, dm = x.shape
    depth = ln1_g.shape[0]
    alpha = (2 * depth) ** 0.25
    xf = x.astype(F32).reshape(bn * seq, dm)
    xb = xf.astype(BF16)
    for i in range(depth):
        j = i // 2
        xb3 = xb.reshape(bn, seq, dm)
        if i % 2 == 0:
            o = _gdn_mixer(xb3, a_w_in[j], a_conv[j], a_alog[j], a_dt_bias[j], a_norm_g[j])
            w_out = a_w_out[j]
        else:
            o = _gla_mixer(xb3, b_w_in[j], b_gate_w2[j], b_gate_b[j], b_norm_g[j])
            w_out = b_w_out[j]
        xf, xb = _post(o.reshape(bn * seq, -1), xf, w_out, mlp_w1[i], mlp_w2[i],
                       ln1_g[i], ln1_b[i], ln2_g[i], ln2_b[i], alpha)
    return xf.reshape(bn, seq, dm).astype(x.dtype)
```

```python
import functools
import math

import numpy as np

import jax
import jax.numpy as jnp
from jax import lax
from jax.experimental import pallas as pl
from jax.experimental.pallas import tpu as pltpu

F32 = jnp.float32
BF16 = jnp.bfloat16

A_HEADS, A_DK, A_DV, A_CONV = 8, 128, 128, 5
B_HEADS, B_DK, B_DV, B_RANK, B_TAU = 4, 128, 256, 16, 16.0
LN_EPS, RMS_EPS, L2_EPS = 1e-5, 1e-6, 1e-6

CH = 128
N_LEVELS = 7
HALO = 8
CONV_MXU_TAPS = (0, 4)
GATE_ROWS = 16
SOLVE_LEVELS_PER_STAGE = 1
GLA_GROUP = 4
GLA_AHEAD = 2
GLA_PIECES = 2
FIN_GROUP = 4
NEG_BIG = -1e30
LOG2E = math.log2(math.e)
VMEM_LIMIT = 56 * 1024 * 1024
ROW_TILE = 1024
FF_TILE = 1024

assert CH == A_DK == B_DK and 2 ** N_LEVELS == CH


def _dot(a, b):
    return jnp.dot(a, b, preferred_element_type=F32)


def _dot_nt(a, b):
    return lax.dot_general(a, b, (((1,), (1,)), ((), ())), preferred_element_type=F32)


def _dot_tn(a, b):
    return lax.dot_general(a, b, (((0,), (0,)), ((), ())), preferred_element_type=F32)


def _split(x, n, axis=1):
    pieces = []
    for _ in range(n - 1):
        p = x.astype(BF16)
        pieces.append(p)
        x = x - p.astype(F32)
    pieces.append(x.astype(BF16))
    return jnp.concatenate(pieces, axis=axis)


def _fold(y, n, axis=1):
    w = y.shape[axis] // n
    blocks = [lax.slice_in_dim(y, i * w, (i + 1) * w, axis=axis) for i in range(n)]
    out = blocks[0]
    for b in blocks[1:]:
        out = out + b
    return out


def _dot_exact(m01, x):
    return _fold(_dot(m01, _split(x, 3)), 3)


def _sigmoid(x):
    return 0.5 + 0.5 * jnp.tanh(0.5 * x)


def _silu(x):
    h = 0.5 * x
    return h + h * jnp.tanh(h)


def _softplus(x):
    return jnp.maximum(x, 0.0) + jnp.log(1.0 + jnp.exp(-jnp.abs(x)))


def _layernorm(y, g, b):
    mu = jnp.mean(y, axis=-1, keepdims=True)
    yc = y - mu
    var = jnp.mean(yc * yc, axis=-1, keepdims=True)
    return yc * lax.rsqrt(var + LN_EPS) * g + b


def _order_masks(rev):
    row = lax.broadcasted_iota(jnp.int32, (CH, CH), 0)
    col = lax.broadcasted_iota(jnp.int32, (CH, CH), 1)
    if rev:
        return col >= row, col > row
    return col <= row, col < row


def _group(n, want):
    return math.gcd(n, want)


def _gdn_kernel(xb_ref, wh_ref, cw_ref, hp_ref, ng_ref, lm_ref, sh_ref, o_ref,
                p_ref, qkv_ref, gate_ref, a_ref, t_ref, qk_ref, rhs_ref, qd_ref, kd_ref, gl_ref,
                mc_ref, qc_ref, rc_ref, oc_ref, *, seq):
    nc = seq // CH
    spare = 2 * nc

    p_ref[0:HALO, :] = jnp.zeros((HALO, p_ref.shape[1]), F32)
    p_ref[HALO + seq:, :] = jnp.zeros((HALO, p_ref.shape[1]), F32)
    p_ref[HALO:HALO + seq, :] = _dot(xb_ref[0], wh_ref[0])
    for ref in (a_ref, t_ref, qk_ref, kd_ref, rhs_ref, qd_ref):
        ref[spare] = jnp.zeros(ref.shape[1:], ref.dtype)
    qkv_ref[nc] = jnp.zeros(qkv_ref.shape[1:], F32)
    gate_ref[nc] = jnp.zeros(gate_ref.shape[1:], F32)

    cw = cw_ref[0]
    hp = hp_ref[0]
    zero16 = jnp.zeros((CH, CH), BF16)

    def pair(f, b):
        return jnp.concatenate([jnp.concatenate([f, zero16], axis=1),
                                jnp.concatenate([zero16, b], axis=1)], axis=0)

    def tiles(it, lag):
        c = it - lag
        ok = jnp.logical_and(c >= 0, c < nc)
        return jnp.where(ok, c, spare), jnp.where(ok, nc + c, spare)

    def prep_one(c):
        base = pl.multiple_of(c * CH, CH)
        win = p_ref[pl.ds(base, CH + 2 * HALO), 0:3 * A_DK]
        gates = p_ref[pl.ds(base + HALO, CH), 4 * A_DK:5 * A_DK].T[0:GATE_ROWS, :]
        if CONV_MXU_TAPS:
            shifted = _dot(sh_ref[...], win.astype(BF16))
        mid = A_CONV // 2
        acc = win[HALO:HALO + CH, :] * cw[mid:mid + 1, :]
        for i in range(A_CONV):
            if i != mid and i not in CONV_MXU_TAPS:
                off = HALO + i - mid
                acc = acc + win[off:off + CH, :] * cw[i:i + 1, :]
        g_rows = -hp[0] * _softplus(gates + hp[1])
        pieces = _split(g_rows, 3, axis=0)
        gcum = [_fold(_dot(pieces, _order_masks(not rev)[0].astype(BF16)), 3, axis=0) for rev in (False, True)]
        yield
        for n, i in enumerate(CONV_MXU_TAPS):
            acc = acc + shifted[n * CH:(n + 1) * CH, :] * cw[i:i + 1, :]
        yield
        s = _silu(acc)
        q = s[:, 0:A_DK]
        k = s[:, A_DK:2 * A_DK]
        q = q * (lax.rsqrt(jnp.sum(q * q, axis=-1, keepdims=True) + L2_EPS) * (A_DK ** -0.5))
        k = k * lax.rsqrt(jnp.sum(k * k, axis=-1, keepdims=True) + L2_EPS)
        yield
        qkv_ref[c] = jnp.concatenate([q, k, s[:, 2 * A_DK:]], axis=1)
        gate_ref[c] = jnp.concatenate([_sigmoid(gates), gcum[0], gcum[1]], axis=0)
        yield

    def prep_two(jf, jb, qkv, gate):
        q = qkv[:, 0:A_DK]
        k = qkv[:, A_DK:2 * A_DK]
        v = qkv[:, 2 * A_DK:]
        dirs = ((0, False, jf), (1, True, jb))
        gr = [jnp.broadcast_to(gate[(1 + d) * GATE_ROWS + 2 + d:(1 + d) * GATE_ROWS + 3 + d, :], (CH, CH))
              for d, _, _ in dirs]
        gc = [x.T for x in gr]
        beta = [jnp.broadcast_to(gate[d:d + 1, :], (CH, CH)).T for d, _, _ in dirs]
        kb16 = k.astype(BF16)
        eye = (lax.broadcasted_iota(jnp.int32, (CH, CH), 0)
               == lax.broadcasted_iota(jnp.int32, (CH, CH), 1)).astype(F32)
        yield
        kbeta = [k * beta[d] for d, _, _ in dirs]
        kq = [_dot_nt(jnp.concatenate([kbeta[d], q], axis=0).astype(BF16), kb16) for d, _, _ in dirs]
        gtot = [gc[d][0:1, :] if rev else gc[d][CH - 1:CH, :] for d, rev, _ in dirs]
        eg = [jnp.exp2(gc[d]) for d, _, _ in dirs]
        dmat = [jnp.exp2(jnp.where(_order_masks(rev)[0], gc[d] - gr[d], NEG_BIG)) for d, rev, _ in dirs]
        yield
        for d, rev, idx in dirs:
            rhs_ref[idx] = jnp.concatenate([v * beta[d], kbeta[d] * eg[d]], axis=1).astype(BF16)
            qd_ref[idx] = q * eg[d]
            kd_ref[idx] = (k * jnp.exp2(gtot[d] - gc[d])).astype(BF16)
            gl_ref[idx] = jnp.broadcast_to(jnp.exp2(gtot[d]), (8, A_DK))
        yield
        for d, rev, idx in dirs:
            a = jnp.where(_order_masks(rev)[1], kq[d][0:CH] * dmat[d], 0.0).astype(BF16)
            a_ref[idx] = a
            t_ref[idx] = (eye - (a * lm_ref[0]).astype(F32)).astype(BF16)
            qk_ref[idx] = (kq[d][CH:] * dmat[d]).astype(BF16)
        yield

    per_stage = SOLVE_LEVELS_PER_STAGE
    n_stages = (N_LEVELS - 1) // per_stage
    solve_stages = tuple((2 + s, tuple(range(1 + s * per_stage, 1 + (s + 1) * per_stage)))
                         for s in range(n_stages))
    ops_lag = n_stages + 2

    def step(it, stage_one, stage_two):
        fillers = []
        if stage_two:
            c2 = it - 1
            ok = jnp.logical_and(c2 >= 0, c2 < nc)
            cq = jnp.where(ok, c2, nc)
            fillers.append(prep_two(*tiles(it, 1), qkv_ref[cq], gate_ref[cq]))
        loaded = []
        for lag, lvs in solve_stages:
            jf, jb = tiles(it, lag)
            loaded.append((jf, jb, t_ref[jf], t_ref[jb], a_ref[jf], a_ref[jb]))
        ops_in = [(i, t_ref[i], rhs_ref[i], kd_ref[i], qk_ref[i], qd_ref[i]) for i in tiles(it, ops_lag)]
        if stage_one:
            fillers.insert(0, prep_one(it))

        def fill():
            for f in fillers:
                next(f, None)

        uws = [_dot(t, rhs) for i, t, rhs, kd, qkm, qd in ops_in]
        for half in range(per_stage):
            xs = []
            for (lag, lvs), (jf, jb, tf, tb, af, ab) in zip(solve_stages, loaded):
                m = lm_ref[lvs[half]]
                xs.append((_dot(tf, af * m), _dot(tb, ab * m)))
            fill()
            fill()
            ys = [jnp.concatenate([_dot(xf.astype(BF16), tf), _dot(xb.astype(BF16), tb)], axis=1)
                  for (xf, xb), (jf, jb, tf, tb, af, ab) in zip(xs, loaded)]
            if half == 0:
                uws = [uw.astype(BF16) for uw in uws]
                kuws = [_dot_tn(kd, uw) for uw, (i, t, rhs, kd, qkm, qd) in zip(uws, ops_in)]
                quws = [_dot(qkm, uw) for uw, (i, t, rhs, kd, qkm, qd) in zip(uws, ops_in)]
            fill()
            fill()
            loaded = [(jf, jb, tf - y[:, 0:CH].astype(BF16), tb - y[:, CH:].astype(BF16), af, ab)
                      for y, (jf, jb, tf, tb, af, ab) in zip(ys, loaded)]
        for f in fillers:
            for _ in f:
                pass
        for jf, jb, tf, tb, _, _ in loaded:
            t_ref[jf] = tf
            t_ref[jb] = tb
        for kuw, quw, (i, t, rhs, kd, qkm, qd) in zip(kuws, quws, ops_in):
            qc_ref[i] = kuw[:, 0:A_DV]
            mc_ref[i] = kuw[:, A_DV:].astype(BF16)
            oc_ref[i] = quw[:, 0:A_DV]
            rc_ref[i] = (qd - quw[:, A_DV:]).astype(BF16)

    def loop(lo, hi, stage_one, stage_two):
        def body(it, carry):
            step(it, stage_one, stage_two)
            return carry
        lax.fori_loop(lo, hi, body, 0)

    loop(0, nc, True, True)
    loop(nc, nc + 1, False, True)
    loop(nc + 1, nc + ops_lag, False, False)

    def scan(c, carry):
        sf, sb = carry
        cf = c
        cb = 2 * nc - 1 - c
        of = _dot(rc_ref[cf], sf.astype(BF16)) + oc_ref[cf]
        ob = _dot(rc_ref[cb], sb.astype(BF16)) + oc_ref[cb]
        oc_ref[cf] = of
        oc_ref[cb] = ob
        sf = gl_ref[cf][0:1, :] * sf - _dot(mc_ref[cf], sf.astype(BF16)) + qc_ref[cf]
        sb = gl_ref[cb][0:1, :] * sb - _dot(mc_ref[cb], sb.astype(BF16)) + qc_ref[cb]
        return sf, sb

    zero = jnp.zeros((A_DK, A_DV), F32)
    lax.fori_loop(0, nc, scan, (zero, zero))

    ng = ng_ref[...]
    fg = _group(nc, FIN_GROUP)

    def fin(it, carry):
        cs = [it * fg + u for u in range(fg)]
        rows = [pl.ds(pl.multiple_of(c * CH, CH), CH) for c in cs]
        os_ = [oc_ref[c] + oc_ref[nc + c] for c in cs]
        inv = [lax.rsqrt(jnp.mean(o * o, axis=-1, keepdims=True) + RMS_EPS) for o in os_]
        for c, r, o, s in zip(cs, rows, os_, inv):
            z = p_ref[pl.ds(pl.multiple_of(c * CH, CH) + HALO, CH), 3 * A_DK:4 * A_DK]
            o_ref[0, r, :] = (o * s * ng * _silu(z)).astype(BF16)
        return carry

    lax.fori_loop(0, nc // fg, fin, 0)


def _gdn_level_masks():
    idx = np.arange(CH)
    x = idx[:, None] ^ idx[None, :]
    return jnp.asarray(np.stack([(x >> lv) == 1 for lv in range(N_LEVELS)]), BF16)


def _conv_shift_matrices():
    t = np.arange(CH)[:, None]
    r = np.arange(CH + 2 * HALO)[None, :]
    taps = CONV_MXU_TAPS or (0,)
    return jnp.asarray(np.concatenate([r == t + HALO + i - A_CONV // 2 for i in taps], axis=0), BF16)


def _gdn_mixer(xb, w_in, conv_w, a_log, dt_bias, norm_g):
    bn, seq, dm = xb.shape
    h, dk = A_HEADS, A_DK
    nc = seq // CH
    w = w_in
    hw = h * dk
    ba = w[:, 4 * hw:].reshape(dm, 2, 2, h)
    per_head = [w[:, i * hw:(i + 1) * hw].reshape(dm, h, dk).transpose(1, 0, 2) for i in range(4)]
    gate_cols = jnp.pad(ba.reshape(dm, 4, h).transpose(2, 0, 1), ((0, 0), (0, 0), (0, dk - 4)))
    wh = jnp.concatenate(per_head + [gate_cols], axis=2).astype(BF16)
    cw = conv_w.reshape(A_CONV, 3, h, dk).transpose(2, 0, 1, 3).reshape(h, A_CONV, 3 * dk)
    cw = jnp.pad(cw, ((0, 0), (0, 8 - A_CONV), (0, 0))).astype(F32)
    scale = jnp.zeros((h, GATE_ROWS), F32).at[:, 2:4].set((jnp.exp(a_log.astype(F32)) * LOG2E).T)
    bias = jnp.zeros((h, GATE_ROWS), F32).at[:, 2:4].set(dt_bias.astype(F32).T)
    hp = jnp.broadcast_to(jnp.stack([scale, bias], axis=1)[:, :, :, None], (h, 2, GATE_ROWS, dk))
    ng = norm_g.astype(F32).reshape(1, A_DV)
    lm = _gdn_level_masks()
    sh = _conv_shift_matrices()
    nw = wh.shape[2]

    kern = functools.partial(_gdn_kernel, seq=seq)
    tile = lambda dt: pltpu.VMEM((2 * nc + 1, CH, CH), dt)
    return pl.pallas_call(
        kern,
        out_shape=jax.ShapeDtypeStruct((bn, seq, h * A_DV), BF16),
        grid=(bn, h),
        in_specs=[
            pl.BlockSpec((1, seq, dm), lambda b, i: (b, 0, 0)),
            pl.BlockSpec((1, dm, nw), lambda b, i: (i, 0, 0)),
            pl.BlockSpec((1, 8, 3 * dk), lambda b, i: (i, 0, 0)),
            pl.BlockSpec((1, 2, GATE_ROWS, dk), lambda b, i: (i, 0, 0, 0)),
            pl.BlockSpec((1, A_DV), lambda b, i: (0, 0)),
            pl.BlockSpec(lm.shape, lambda b, i: (0, 0, 0)),
            pl.BlockSpec(sh.shape, lambda b, i: (0, 0)),
        ],
        out_specs=pl.BlockSpec((1, seq, A_DV), lambda b, i: (b, 0, i)),
        scratch_shapes=[
            pltpu.VMEM((seq + 2 * HALO, nw), F32),
            pltpu.VMEM((nc + 1, CH, 3 * A_DK), F32),
            pltpu.VMEM((nc + 1, 3 * GATE_ROWS, CH), F32),
            tile(BF16),
            tile(BF16),
            tile(BF16),
            pltpu.VMEM((2 * nc + 1, CH, A_DV + A_DK), BF16),
            tile(F32),
            tile(BF16),
            pltpu.VMEM((2 * nc + 1, 8, A_DK), F32),
            tile(BF16),
            tile(F32),
            tile(BF16),
            tile(F32),
        ],
        compiler_params=pltpu.CompilerParams(
            dimension_semantics=("arbitrary", "arbitrary"), vmem_limit_bytes=VMEM_LIMIT),
        name="gdn_mixer",
    )(xb, wh, cw, hp, ng, lm, sh)


def _gla_tables():
    i = np.arange(CH)[:, None]
    t = np.arange(CH)[None, :]
    seg = np.zeros((2, N_LEVELS + 1, CH, CH), np.float32)
    lvl = np.zeros((2, CH, CH), np.int32)
    for d in range(2):
        rev = d == 1
        seg[d, 0] = (t >= i) if rev else (t <= i)
        lv = np.full((CH, CH), N_LEVELS + 1, np.int32)
        lv[np.arange(CH), np.arange(CH)] = N_LEVELS
        x = i ^ t
        for l in range(N_LEVELS):
            h = 2 ** (N_LEVELS - 1 - l)
            b0 = (i // (2 * h)) * (2 * h)
            if rev:
                r = b0 + h
                late = i < r
                m = np.where(late, (t >= i) & (t < r), (t >= r) & (t < i))
                own = ((x >> (N_LEVELS - 1 - l)) == 1) & (t > i)
            else:
                r = b0 + h - 1
                late = i > r
                m = np.where(late, (t > r) & (t <= i), (t > i) & (t <= r))
                own = ((x >> (N_LEVELS - 1 - l)) == 1) & (t < i)
            seg[d, 1 + l] = m
            lv[own] = l
        lvl[d] = lv
    return jnp.asarray(seg, BF16), jnp.asarray(lvl)


def _gla_kernel(xb_ref, wh_ref, w2_ref, gb_ref, ng_ref, seg_ref, lvl_ref, o_ref,
                p_ref, qs_ref, kv_ref, st_ref, dec_ref, oi_ref, *, seq):
    nc = seq // CH
    dk, dv = B_DK, B_DV
    p_ref[...] = _dot(xb_ref[0], wh_ref[0])

    c_q, c_k, c_v, c_r, c_g = 0, dk, 2 * dk, 2 * dk + dv, 2 * dk + 2 * dv
    gg = _group(nc, GLA_GROUP)
    lanes = [(u, d) for u in range(gg) for d in range(2)]

    def prep(it, carry):
        cs = [it * gg + u for u in range(gg)]
        rows = [pl.ds(pl.multiple_of(c * CH, CH), CH) for c in cs]
        q = [p_ref[r, c_q:c_q + dk] * (dk ** -0.5) for r in rows]
        k = [p_ref[r, c_k:c_k + dk] for r in rows]
        q16 = [x.astype(BF16) for x in q]
        k16 = [x.astype(BF16) for x in k]
        v16 = [p_ref[r, c_v:c_v + dv].astype(BF16) for r in rows]
        gin = [p_ref[r, c_g:c_g + dk].astype(BF16) for r in rows]
        logit = [_dot(gin[u], w2_ref[0, d]) + gb_ref[0, d][0:1, :] for u, d in lanes]
        la3 = [_split(-_softplus(-x) * (LOG2E / B_TAU), 3) for x in logit]
        la2 = [y[:, 0:GLA_PIECES * dk] for y in la3]
        bc = [_fold(_dot(seg_ref[d, 0], y), 3) for (u, d), y in zip(lanes, la3)]

        def level_sums(l):
            h = CH >> (l + 1)
            if h < HALO:
                return [_fold(_dot(seg_ref[d, 1 + l], y), GLA_PIECES) for (u, d), y in zip(lanes, la2)]
            out = []
            for (u, d), b in zip(lanes, bc):
                blocks = []
                for lo in range(0, CH, 2 * h):
                    if d == 1:
                        ref = b[lo + h:lo + h + 1, :]
                        blocks += [b[lo:lo + h, :] - ref, ref - b[lo + h:lo + 2 * h, :]]
                    else:
                        ref = b[lo + h - 1:lo + h, :]
                        blocks += [ref - b[lo:lo + h, :], b[lo + h:lo + 2 * h, :] - ref]
                out.append(jnp.concatenate(blocks, axis=0))
            return out

        half = CH // 2
        zero_half = jnp.zeros((half, dk), BF16)

        def top_level(x, e, d, late):
            upper = (d == 1) != late
            rows = slice(half, CH) if upper else slice(0, half)
            kept = (x[rows] * e[rows]).astype(BF16)
            return jnp.concatenate([zero_half, kept] if upper else [kept, zero_half], axis=0)

        ahead = [level_sums(l) for l in range(GLA_AHEAD)]
        scores = prod = None
        for l in range(N_LEVELS):
            if l + GLA_AHEAD < N_LEVELS:
                ahead.append(level_sums(l + GLA_AHEAD))
            e = [jnp.exp2(x) for x in ahead[l]]
            if l == 0:
                ql = [top_level(q[u], x, d, True) for (u, d), x in zip(lanes, e)]
                kl = [top_level(k[u], x, d, False) for (u, d), x in zip(lanes, e)]
            else:
                ql = [(q[u] * x).astype(BF16) for (u, d), x in zip(lanes, e)]
                kl = [(k[u] * x).astype(BF16) for (u, d), x in zip(lanes, e)]
            if l == 1:
                scores = prod
            elif l > 1:
                own = [lvl_ref[d] == l - 1 for d in range(2)]
                scores = [jnp.where(own[d], p, s) for (u, d), p, s in zip(lanes, prod, scores)]
            prod = [_dot_nt(a, b) for a, b in zip(ql, kl)]
        own = [lvl_ref[d] == N_LEVELS - 1 for d in range(2)]
        scores = [jnp.where(own[d], p, s) for (u, d), p, s in zip(lanes, prod, scores)]
        diag = [_dot_nt(q16[u], k16[u]) for u in range(gg)]
        own = [lvl_ref[d] == N_LEVELS for d in range(2)]
        scores = [jnp.where(own[d], diag[u], s) for (u, d), s in zip(lanes, scores)]
        for (u, d), s, b in zip(lanes, scores, bc):
            idx = d * nc + cs[u]
            btot = b[0:1, :] if d == 1 else b[CH - 1:CH, :]
            oi_ref[idx] = _dot(s.astype(BF16), v16[u])
            qs_ref[idx] = (q[u] * jnp.exp2(b)).astype(BF16)
            kv_ref[idx] = _dot_tn((k[u] * jnp.exp2(btot - b)).astype(BF16), v16[u])
            dec_ref[idx] = jnp.exp2(jnp.broadcast_to(btot, (CH, dk)).T)
        return carry

    lax.fori_loop(0, nc // gg, prep, 0)

    def scan(c, carry):
        sf, sb = carry
        cf = c
        ib = 2 * nc - 1 - c
        st_ref[cf] = sf.astype(BF16)
        st_ref[ib] = sb.astype(BF16)
        ef = dec_ref[cf]
        eb = dec_ref[ib]
        sf = sf * jnp.concatenate([ef, ef], axis=1) + kv_ref[cf]
        sb = sb * jnp.concatenate([eb, eb], axis=1) + kv_ref[ib]
        return sf, sb

    zero = jnp.zeros((dk, dv), F32)
    lax.fori_loop(0, nc, scan, (zero, zero))

    ng = ng_ref[...]
    fg = _group(nc, FIN_GROUP)

    def fin(it, carry):
        cs = [it * fg + u for u in range(fg)]
        rows = [pl.ds(pl.multiple_of(c * CH, CH), CH) for c in cs]
        os_ = [oi_ref[c] + oi_ref[nc + c] + _dot(qs_ref[c], st_ref[c]) + _dot(qs_ref[nc + c], st_ref[nc + c])
               for c in cs]
        inv = [lax.rsqrt(jnp.mean(o * o, axis=-1, keepdims=True) + RMS_EPS) for o in os_]
        for r, o, s in zip(rows, os_, inv):
            o_ref[0, r, :] = (o * s * ng * _silu(p_ref[r, c_r:c_r + dv])).astype(BF16)
        return carry

    lax.fori_loop(0, nc // fg, fin, 0)


def _gla_mixer(xb, w_in, gate_w2, gate_b, norm_g):
    bn, seq, dm = xb.shape
    h, dk, dv = B_HEADS, B_DK, B_DV
    nc = seq // CH
    kw, vw = h * dk, h * dv
    w = w_in

    def heads(cols, width):
        return cols.reshape(dm, h, width).transpose(1, 0, 2)

    gl = jnp.pad(w[:, 2 * kw + 2 * vw:], ((0, 0), (0, dk - 2 * B_RANK)))
    wh = jnp.concatenate([
        heads(w[:, 0:kw], dk), heads(w[:, kw:2 * kw], dk),
        heads(w[:, 2 * kw:2 * kw + vw], dv), heads(w[:, 2 * kw + vw:2 * kw + 2 * vw], dv),
        jnp.broadcast_to(gl[None], (h, dm, dk))], axis=2).astype(BF16)
    w2 = gate_w2.reshape(2, B_RANK, h, dk).transpose(2, 0, 1, 3)
    w2p = jnp.zeros((h, 2, dk, dk), F32)
    w2p = w2p.at[:, 0, 0:B_RANK].set(w2[:, 0]).at[:, 1, B_RANK:2 * B_RANK].set(w2[:, 1]).astype(BF16)
    gb = gate_b.reshape(2, h, dk).transpose(1, 0, 2).astype(F32)
    gb = jnp.broadcast_to(gb[:, :, None, :], (h, 2, 8, dk))
    ng = norm_g.astype(F32).reshape(1, dv)
    seg, lvl = _gla_tables()
    nw = wh.shape[2]

    kern = functools.partial(_gla_kernel, seq=seq)
    return pl.pallas_call(
        kern,
        out_shape=jax.ShapeDtypeStruct((bn, seq, vw), BF16),
        grid=(bn, h),
        in_specs=[
            pl.BlockSpec((1, seq, dm), lambda b, i: (b, 0, 0)),
            pl.BlockSpec((1, dm, nw), lambda b, i: (i, 0, 0)),
            pl.BlockSpec((1, 2, dk, dk), lambda b, i: (i, 0, 0, 0)),
            pl.BlockSpec((1, 2, 8, dk), lambda b, i: (i, 0, 0, 0)),
            pl.BlockSpec((1, dv), lambda b, i: (0, 0)),
            pl.BlockSpec(seg.shape, lambda b, i: (0, 0, 0, 0)),
            pl.BlockSpec(lvl.shape, lambda b, i: (0, 0, 0)),
        ],
        out_specs=pl.BlockSpec((1, seq, dv), lambda b, i: (b, 0, i)),
        scratch_shapes=[
            pltpu.VMEM((seq, nw), F32),
            pltpu.VMEM((2 * nc, CH, dk), BF16),
            pltpu.VMEM((2 * nc, dk, dv), F32),
            pltpu.VMEM((2 * nc, dk, dv), BF16),
            pltpu.VMEM((2 * nc, dk, dk), F32),
            pltpu.VMEM((2 * nc, CH, dv), F32),
        ],
        compiler_params=pltpu.CompilerParams(
            dimension_semantics=("arbitrary", "arbitrary"), vmem_limit_bytes=VMEM_LIMIT),
        name="gla_mixer",
    )(xb, wh, w2p, gb, ng, seg, lvl)


def _post_kernel(o_ref, x_ref, wo_ref, w1_ref, w2_ref, ln_ref, y_ref, yb_ref, *, alpha):
    ln = ln_ref[...]
    x = x_ref[...]
    x1 = _layernorm(alpha * x + _dot(o_ref[...], wo_ref[...]), ln[0:1, :], ln[1:2, :])
    x1b = x1.astype(BF16)
    acc = jnp.zeros(x.shape, F32)
    dff = w1_ref.shape[1]
    for j in range(dff // FF_TILE):
        cols = slice(j * FF_TILE, (j + 1) * FF_TILE)
        hcur = jnp.maximum(_dot(x1b, w1_ref[:, cols]), 0.0)
        acc = acc + _dot((hcur * hcur).astype(BF16), w2_ref[cols, :])
    y = _layernorm(alpha * x1 + acc, ln[2:3, :], ln[3:4, :])
    y_ref[...] = y
    yb_ref[...] = y.astype(BF16)


def _post(o, x, w_out, w1, w2, g1, b1, g2, b2, alpha):
    t, dm = x.shape
    vw = o.shape[1]
    dff = w1.shape[1]
    tm = min(ROW_TILE, t)
    ln = jnp.pad(jnp.stack([g1, b1, g2, b2]).astype(F32), ((0, 4), (0, 0)))
    const = lambda shape: pl.BlockSpec(shape, lambda i: (0, 0), pipeline_mode=pl.Buffered(1))
    return pl.pallas_call(
        functools.partial(_post_kernel, alpha=alpha),
        out_shape=(jax.ShapeDtypeStruct((t, dm), F32), jax.ShapeDtypeStruct((t, dm), BF16)),
        grid=(t // tm,),
        in_specs=[
            pl.BlockSpec((tm, vw), lambda i: (i, 0)),
            pl.BlockSpec((tm, dm), lambda i: (i, 0)),
            const((vw, dm)), const((dm, dff)), const((dff, dm)), const((8, dm)),
        ],
        out_specs=(pl.BlockSpec((tm, dm), lambda i: (i, 0)), pl.BlockSpec((tm, dm), lambda i: (i, 0))),
        compiler_params=pltpu.CompilerParams(
            dimension_semantics=("arbitrary",), vmem_limit_bytes=VMEM_LIMIT),
        name="post",
    )(o, x, w_out.astype(BF16), w1.astype(BF16), w2.astype(BF16), ln)


def kernel(x, a_w_in, a_conv, a_alog, a_dt_bias, a_norm_g, a_w_out, b_w_in, b_gate_w2, b_gate_b,
           b_norm_g, b_w_out, ln1_g, ln1_b, mlp_w1, mlp_w2, ln2_g, ln2_b):
    bn, seq, dm = x.shape
    depth = ln1_g.shape[0]
    alpha = (2 * depth) ** 0.25
    xf = x.astype(F32).reshape(bn * seq, dm)
    xb = xf.astype(BF16)
    for i in range(depth):
        j = i // 2
        xb3 = xb.reshape(bn, seq, dm)
        if i % 2 == 0:
            o = _gdn_mixer(xb3, a_w_in[j], a_conv[j], a_alog[j], a_dt_bias[j], a_norm_g[j])
            w_out = a_w_out[j]
        else:
            o = _gla_mixer(xb3, b_w_in[j], b_gate_w2[j], b_gate_b[j], b_norm_g[j])
            w_out = b_w_out[j]
        xf, xb = _post(o.reshape(bn * seq, -1), xf, w_out, mlp_w1[i], mlp_w2[i],
                       ln1_g[i], ln1_b[i], ln2_g[i], ln2_b[i], alpha)
    return xf.reshape(bn, seq, dm).astype(x.dtype)
```

```python
import functools
import math

import numpy as np

import jax
import jax.numpy as jnp
from jax import lax
from jax.experimental import pallas as pl
from jax.experimental.pallas import tpu as pltpu

F32 = jnp.float32
BF16 = jnp.bfloat16

A_HEADS, A_DK, A_DV, A_CONV = 8, 128, 128, 5
B_HEADS, B_DK, B_DV, B_RANK, B_TAU = 4, 128, 256, 16, 16.0
LN_EPS, RMS_EPS, L2_EPS = 1e-5, 1e-6, 1e-6

CH = 128
N_LEVELS = 7
HALO = 8
CONV_MXU_TAPS = (0, 4)
GATE_ROWS = 16
SOLVE_LEVELS_PER_STAGE = 1
GLA_GROUP = 4
GLA_AHEAD = 2
GLA_PIECES = 2
FIN_GROUP = 4
NEG_BIG = -1e30
LOG2E = math.log2(math.e)
VMEM_LIMIT = 56 * 1024 * 1024
ROW_TILE = 1024
FF_TILE = 1024

assert CH == A_DK == B_DK and 2 ** N_LEVELS == CH


def _dot(a, b):
    return jnp.dot(a, b, preferred_element_type=F32)


def _dot_nt(a, b):
    return lax.dot_general(a, b, (((1,), (1,)), ((), ())), preferred_element_type=F32)


def _dot_tn(a, b):
    return lax.dot_general(a, b, (((0,), (0,)), ((), ())), preferred_element_type=F32)


def _split(x, n, axis=1):
    pieces = []
    for _ in range(n - 1):
        p = x.astype(BF16)
        pieces.append(p)
        x = x - p.astype(F32)
    pieces.append(x.astype(BF16))
    return jnp.concatenate(pieces, axis=axis)


def _fold(y, n, axis=1):
    w = y.shape[axis] // n
    blocks = [lax.slice_in_dim(y, i * w, (i + 1) * w, axis=axis) for i in range(n)]
    out = blocks[0]
    for b in blocks[1:]:
        out = out + b
    return out


def _dot_exact(m01, x):
    return _fold(_dot(m01, _split(x, 3)), 3)


def _sigmoid(x):
    return 0.5 + 0.5 * jnp.tanh(0.5 * x)


def _silu(x):
    h = 0.5 * x
    return h + h * jnp.tanh(h)


def _softplus(x):
    return jnp.maximum(x, 0.0) + jnp.log(1.0 + jnp.exp(-jnp.abs(x)))


def _layernorm(y, g, b):
    mu = jnp.mean(y, axis=-1, keepdims=True)
    yc = y - mu
    var = jnp.mean(yc * yc, axis=-1, keepdims=True)
    return yc * lax.rsqrt(var + LN_EPS) * g + b


def _order_masks(rev):
    row = lax.broadcasted_iota(jnp.int32, (CH, CH), 0)
    col = lax.broadcasted_iota(jnp.int32, (CH, CH), 1)
    if rev:
        return col >= row, col > row
    return col <= row, col < row


def _group(n, want):
    return math.gcd(n, want)


def _gdn_kernel(xb_ref, wh_ref, cw_ref, hp_ref, ng_ref, lm_ref, sh_ref, o_ref,
                p_ref, qkv_ref, gate_ref, a_ref, t_ref, qk_ref, rhs_ref, qd_ref, kd_ref, gl_ref,
                mc_ref, qc_ref, rc_ref, oc_ref, st_ref, *, seq):
    nc = seq // CH
    spare = 2 * nc

    p_ref[0:HALO, :] = jnp.zeros((HALO, p_ref.shape[1]), F32)
    p_ref[HALO + seq:, :] = jnp.zeros((HALO, p_ref.shape[1]), F32)
    p_ref[HALO:HALO + seq, :] = _dot(xb_ref[0], wh_ref[0])
    for ref in (a_ref, t_ref, qk_ref, kd_ref, rhs_ref, qd_ref):
        ref[spare] = jnp.zeros(ref.shape[1:], ref.dtype)
    qkv_ref[nc] = jnp.zeros(qkv_ref.shape[1:], F32)
    gate_ref[nc] = jnp.zeros(gate_ref.shape[1:], F32)

    cw = cw_ref[0]
    hp = hp_ref[0]

    def tiles(it, lag):
        c = it - lag
        ok = jnp.logical_and(c >= 0, c < nc)
        return jnp.where(ok, c, spare), jnp.where(ok, nc + c, spare)

    def prep_one(c):
        base = pl.multiple_of(c * CH, CH)
        win = p_ref[pl.ds(base, CH + 2 * HALO), 0:3 * A_DK]
        gates = p_ref[pl.ds(base + HALO, CH), 4 * A_DK:5 * A_DK].T[0:GATE_ROWS, :]
        if CONV_MXU_TAPS:
            shifted = _dot(sh_ref[...], win.astype(BF16))
        mid = A_CONV // 2
        acc = win[HALO:HALO + CH, :] * cw[mid:mid + 1, :]
        for i in range(A_CONV):
            if i != mid and i not in CONV_MXU_TAPS:
                off = HALO + i - mid
                acc = acc + win[off:off + CH, :] * cw[i:i + 1, :]
        g_rows = -hp[0] * _softplus(gates + hp[1])
        pieces = _split(g_rows, 3, axis=0)
        gcum = [_fold(_dot(pieces, _order_masks(not rev)[0].astype(BF16)), 3, axis=0) for rev in (False, True)]
        yield
        for n, i in enumerate(CONV_MXU_TAPS):
            acc = acc + shifted[n * CH:(n + 1) * CH, :] * cw[i:i + 1, :]
        yield
        s = _silu(acc)
        q = s[:, 0:A_DK]
        k = s[:, A_DK:2 * A_DK]
        q = q * (lax.rsqrt(jnp.sum(q * q, axis=-1, keepdims=True) + L2_EPS) * (A_DK ** -0.5))
        k = k * lax.rsqrt(jnp.sum(k * k, axis=-1, keepdims=True) + L2_EPS)
        yield
        qkv_ref[c] = jnp.concatenate([q, k, s[:, 2 * A_DK:]], axis=1)
        gate_ref[c] = jnp.concatenate([_sigmoid(gates), gcum[0], gcum[1]], axis=0)
        yield

    def prep_two(jf, jb, qkv, gate):
        q = qkv[:, 0:A_DK]
        k = qkv[:, A_DK:2 * A_DK]
        v = qkv[:, 2 * A_DK:]
        dirs = ((0, False, jf), (1, True, jb))
        gr = [jnp.broadcast_to(gate[(1 + d) * GATE_ROWS + 2 + d:(1 + d) * GATE_ROWS + 3 + d, :], (CH, CH))
              for d, _, _ in dirs]
        gc = [x.T for x in gr]
        beta = [jnp.broadcast_to(gate[d:d + 1, :], (CH, CH)).T for d, _, _ in dirs]
        kb16 = k.astype(BF16)
        eye = (lax.broadcasted_iota(jnp.int32, (CH, CH), 0)
               == lax.broadcasted_iota(jnp.int32, (CH, CH), 1)).astype(F32)
        yield
        kbeta = [k * beta[d] for d, _, _ in dirs]
        kq = [_dot_nt(jnp.concatenate([kbeta[d], q], axis=0).astype(BF16), kb16) for d, _, _ in dirs]
        gtot = [gc[d][0:1, :] if rev else gc[d][CH - 1:CH, :] for d, rev, _ in dirs]
        eg = [jnp.exp2(gc[d]) for d, _, _ in dirs]
        dmat = [jnp.exp2(jnp.where(_order_masks(rev)[0], gc[d] - gr[d], NEG_BIG)) for d, rev, _ in dirs]
        yield
        for d, rev, idx in dirs:
            rhs_ref[idx] = jnp.concatenate([v * beta[d], kbeta[d] * eg[d]], axis=1).astype(BF16)
            qd_ref[idx] = q * eg[d]
            kd_ref[idx] = (k * jnp.exp2(gtot[d] - gc[d])).astype(BF16)
            gl_ref[idx] = jnp.broadcast_to(jnp.exp2(gtot[d]), (8, A_DK))
        yield
        for d, rev, idx in dirs:
            a = jnp.where(_order_masks(rev)[1], kq[d][0:CH] * dmat[d], 0.0).astype(BF16)
            a_ref[idx] = a
            t_ref[idx] = (eye - (a * lm_ref[0]).astype(F32)).astype(BF16)
            qk_ref[idx] = (kq[d][CH:] * dmat[d]).astype(BF16)
        yield

    per_stage = SOLVE_LEVELS_PER_STAGE
    n_stages = (N_LEVELS - 1) // per_stage
    solve_stages = tuple((2 + s, tuple(range(1 + s * per_stage, 1 + (s + 1) * per_stage)))
                         for s in range(n_stages))
    ops_lag = n_stages + 2

    def step(it, stage_one, stage_two):
        fillers = []
        if stage_two:
            c2 = it - 1
            ok = jnp.logical_and(c2 >= 0, c2 < nc)
            cq = jnp.where(ok, c2, nc)
            fillers.append(prep_two(*tiles(it, 1), qkv_ref[cq], gate_ref[cq]))
        loaded = []
        for lag, lvs in solve_stages:
            jf, jb = tiles(it, lag)
            loaded.append((jf, jb, t_ref[jf], t_ref[jb], a_ref[jf], a_ref[jb]))
        ops_in = [(i, t_ref[i], rhs_ref[i], kd_ref[i], qk_ref[i], qd_ref[i]) for i in tiles(it, ops_lag)]
        if stage_one:
            fillers.insert(0, prep_one(it))

        def fill():
            for f in fillers:
                next(f, None)

        uws = [_dot(t, rhs) for i, t, rhs, kd, qkm, qd in ops_in]
        for half in range(per_stage):
            xs = []
            for (lag, lvs), (jf, jb, tf, tb, af, ab) in zip(solve_stages, loaded):
                m = lm_ref[lvs[half]]
                xs.append((_dot(tf, af * m), _dot(tb, ab * m)))
            fill()
            fill()
            ys = [jnp.concatenate([_dot(xf.astype(BF16), tf), _dot(xb.astype(BF16), tb)], axis=1)
                  for (xf, xb), (jf, jb, tf, tb, af, ab) in zip(xs, loaded)]
            if half == 0:
                uws = [uw.astype(BF16) for uw in uws]
                kuws = [_dot_tn(kd, uw) for uw, (i, t, rhs, kd, qkm, qd) in zip(uws, ops_in)]
                quws = [_dot(qkm, uw) for uw, (i, t, rhs, kd, qkm, qd) in zip(uws, ops_in)]
            fill()
            fill()
            loaded = [(jf, jb, tf - y[:, 0:CH].astype(BF16), tb - y[:, CH:].astype(BF16), af, ab)
                      for y, (jf, jb, tf, tb, af, ab) in zip(ys, loaded)]
        for f in fillers:
            for _ in f:
                pass
        for jf, jb, tf, tb, _, _ in loaded:
            t_ref[jf] = tf
            t_ref[jb] = tb
        for kuw, quw, (i, t, rhs, kd, qkm, qd) in zip(kuws, quws, ops_in):
            qc_ref[i] = kuw[:, 0:A_DV]
            mc_ref[i] = kuw[:, A_DV:].astype(BF16)
            oc_ref[i] = quw[:, 0:A_DV]
            rc_ref[i] = (qd - quw[:, A_DV:]).astype(BF16)

    def loop(lo, hi, stage_one, stage_two):
        def body(it, carry):
            step(it, stage_one, stage_two)
            return carry
        lax.fori_loop(lo, hi, body, 0)

    loop(0, nc, True, True)
    loop(nc, nc + 1, False, True)
    loop(nc + 1, nc + ops_lag, False, False)

    def scan(c, carry):
        sf, sb = carry
        cf = c
        cb = 2 * nc - 1 - c
        sf16 = sf.astype(BF16)
        sb16 = sb.astype(BF16)
        st_ref[cf] = sf16
        st_ref[cb] = sb16
        sf = gl_ref[cf][0:1, :] * sf - _dot(mc_ref[cf], sf16) + qc_ref[cf]
        sb = gl_ref[cb][0:1, :] * sb - _dot(mc_ref[cb], sb16) + qc_ref[cb]
        return sf, sb

    zero = jnp.zeros((A_DK, A_DV), F32)
    lax.fori_loop(0, nc, scan, (zero, zero))

    ng = ng_ref[...]
    fg = _group(nc, FIN_GROUP)

    def fin(it, carry):
        cs = [it * fg + u for u in range(fg)]
        rows = [pl.ds(pl.multiple_of(c * CH, CH), CH) for c in cs]
        os_ = [oc_ref[c] + oc_ref[nc + c] + _dot(rc_ref[c], st_ref[c]) + _dot(rc_ref[nc + c], st_ref[nc + c])
               for c in cs]
        inv = [lax.rsqrt(jnp.mean(o * o, axis=-1, keepdims=True) + RMS_EPS) for o in os_]
        for c, r, o, s in zip(cs, rows, os_, inv):
            z = p_ref[pl.ds(pl.multiple_of(c * CH, CH) + HALO, CH), 3 * A_DK:4 * A_DK]
            o_ref[0, r, :] = (o * s * ng * _silu(z)).astype(BF16)
        return carry

    lax.fori_loop(0, nc // fg, fin, 0)


def _gdn_level_masks():
    idx = np.arange(CH)
    x = idx[:, None] ^ idx[None, :]
    return jnp.asarray(np.stack([(x >> lv) == 1 for lv in range(N_LEVELS)]), BF16)


def _conv_shift_matrices():
    t = np.arange(CH)[:, None]
    r = np.arange(CH + 2 * HALO)[None, :]
    taps = CONV_MXU_TAPS or (0,)
    return jnp.asarray(np.concatenate([r == t + HALO + i - A_CONV // 2 for i in taps], axis=0), BF16)


def _gdn_mixer(xb, w_in, conv_w, a_log, dt_bias, norm_g):
    bn, seq, dm = xb.shape
    h, dk = A_HEADS, A_DK
    nc = seq // CH
    w = w_in
    hw = h * dk
    ba = w[:, 4 * hw:].reshape(dm, 2, 2, h)
    per_head = [w[:, i * hw:(i + 1) * hw].reshape(dm, h, dk).transpose(1, 0, 2) for i in range(4)]
    gate_cols = jnp.pad(ba.reshape(dm, 4, h).transpose(2, 0, 1), ((0, 0), (0, 0), (0, dk - 4)))
    wh = jnp.concatenate(per_head + [gate_cols], axis=2).astype(BF16)
    cw = conv_w.reshape(A_CONV, 3, h, dk).transpose(2, 0, 1, 3).reshape(h, A_CONV, 3 * dk)
    cw = jnp.pad(cw, ((0, 0), (0, 8 - A_CONV), (0, 0))).astype(F32)
    scale = jnp.zeros((h, GATE_ROWS), F32).at[:, 2:4].set((jnp.exp(a_log.astype(F32)) * LOG2E).T)
    bias = jnp.zeros((h, GATE_ROWS), F32).at[:, 2:4].set(dt_bias.astype(F32).T)
    hp = jnp.broadcast_to(jnp.stack([scale, bias], axis=1)[:, :, :, None], (h, 2, GATE_ROWS, dk))
    ng = norm_g.astype(F32).reshape(1, A_DV)
    lm = _gdn_level_masks()
    sh = _conv_shift_matrices()
    nw = wh.shape[2]

    kern = functools.partial(_gdn_kernel, seq=seq)
    tile = lambda dt: pltpu.VMEM((2 * nc + 1, CH, CH), dt)
    return pl.pallas_call(
        kern,
        out_shape=jax.ShapeDtypeStruct((bn, seq, h * A_DV), BF16),
        grid=(bn, h),
        in_specs=[
            pl.BlockSpec((1, seq, dm), lambda b, i: (b, 0, 0)),
            pl.BlockSpec((1, dm, nw), lambda b, i: (i, 0, 0)),
            pl.BlockSpec((1, 8, 3 * dk), lambda b, i: (i, 0, 0)),
            pl.BlockSpec((1, 2, GATE_ROWS, dk), lambda b, i: (i, 0, 0, 0)),
            pl.BlockSpec((1, A_DV), lambda b, i: (0, 0)),
            pl.BlockSpec(lm.shape, lambda b, i: (0, 0, 0)),
            pl.BlockSpec(sh.shape, lambda b, i: (0, 0)),
        ],
        out_specs=pl.BlockSpec((1, seq, A_DV), lambda b, i: (b, 0, i)),
        scratch_shapes=[
            pltpu.VMEM((seq + 2 * HALO, nw), F32),
            pltpu.VMEM((nc + 1, CH, 3 * A_DK), F32),
            pltpu.VMEM((nc + 1, 3 * GATE_ROWS, CH), F32),
            tile(BF16),
            tile(BF16),
            tile(BF16),
            pltpu.VMEM((2 * nc + 1, CH, A_DV + A_DK), BF16),
            tile(F32),
            tile(BF16),
            pltpu.VMEM((2 * nc + 1, 8, A_DK), F32),
            tile(BF16),
            tile(F32),
            tile(BF16),
            tile(F32),
            tile(BF16),
        ],
        compiler_params=pltpu.CompilerParams(
            dimension_semantics=("arbitrary", "arbitrary"), vmem_limit_bytes=VMEM_LIMIT),
        name="gdn_mixer",
    )(xb, wh, cw, hp, ng, lm, sh)


def _gla_tables():
    i = np.arange(CH)[:, None]
    t = np.arange(CH)[None, :]
    seg = np.zeros((2, N_LEVELS + 1, CH, CH), np.float32)
    lvl = np.zeros((2, CH, CH), np.int32)
    for d in range(2):
        rev = d == 1
        seg[d, 0] = (t >= i) if rev else (t <= i)
        lv = np.full((CH, CH), N_LEVELS + 1, np.int32)
        lv[np.arange(CH), np.arange(CH)] = N_LEVELS
        x = i ^ t
        for l in range(N_LEVELS):
            h = 2 ** (N_LEVELS - 1 - l)
            b0 = (i // (2 * h)) * (2 * h)
            if rev:
                r = b0 + h
                late = i < r
                m = np.where(late, (t >= i) & (t < r), (t >= r) & (t < i))
                own = ((x >> (N_LEVELS - 1 - l)) == 1) & (t > i)
            else:
                r = b0 + h - 1
                late = i > r
                m = np.where(late, (t > r) & (t <= i), (t > i) & (t <= r))
                own = ((x >> (N_LEVELS - 1 - l)) == 1) & (t < i)
            seg[d, 1 + l] = m
            lv[own] = l
        lvl[d] = lv
    return jnp.asarray(seg, BF16), jnp.asarray(lvl)


def _gla_kernel(xb_ref, wh_ref, w2_ref, gb_ref, ng_ref, seg_ref, lvl_ref, o_ref,
                p_ref, qs_ref, kv_ref, st_ref, dec_ref, oi_ref, *, seq):
    nc = seq // CH
    dk, dv = B_DK, B_DV
    p_ref[...] = _dot(xb_ref[0], wh_ref[0])

    c_q, c_k, c_v, c_r, c_g = 0, dk, 2 * dk, 2 * dk + dv, 2 * dk + 2 * dv
    gg = _group(nc, GLA_GROUP)
    lanes = [(u, d) for u in range(gg) for d in range(2)]

    def prep(it, carry):
        cs = [it * gg + u for u in range(gg)]
        rows = [pl.ds(pl.multiple_of(c * CH, CH), CH) for c in cs]
        q = [p_ref[r, c_q:c_q + dk] * (dk ** -0.5) for r in rows]
        k = [p_ref[r, c_k:c_k + dk] for r in rows]
        q16 = [x.astype(BF16) for x in q]
        k16 = [x.astype(BF16) for x in k]
        v16 = [p_ref[r, c_v:c_v + dv].astype(BF16) for r in rows]
        gin = [p_ref[r, c_g:c_g + dk].astype(BF16) for r in rows]
        logit = [_dot(gin[u], w2_ref[0, d]) + gb_ref[0, d][0:1, :] for u, d in lanes]
        la3 = [_split(-_softplus(-x) * (LOG2E / B_TAU), 3) for x in logit]
        la2 = [y[:, 0:GLA_PIECES * dk] for y in la3]
        bc = [_fold(_dot(seg_ref[d, 0], y), 3) for (u, d), y in zip(lanes, la3)]

        def level_sums(l):
            h = CH >> (l + 1)
            if h < HALO:
                return [_fold(_dot(seg_ref[d, 1 + l], y), GLA_PIECES) for (u, d), y in zip(lanes, la2)]
            out = []
            for (u, d), b in zip(lanes, bc):
                blocks = []
                for lo in range(0, CH, 2 * h):
                    if d == 1:
                        ref = b[lo + h:lo + h + 1, :]
                        blocks += [b[lo:lo + h, :] - ref, ref - b[lo + h:lo + 2 * h, :]]
                    else:
                        ref = b[lo + h - 1:lo + h, :]
                        blocks += [ref - b[lo:lo + h, :], b[lo + h:lo + 2 * h, :] - ref]
                out.append(jnp.concatenate(blocks, axis=0))
            return out

        half = CH // 2
        zero_half = jnp.zeros((half, dk), BF16)

        def top_level(x, e, d, late):
            upper = (d == 1) != late
            rows = slice(half, CH) if upper else slice(0, half)
            kept = (x[rows] * e[rows]).astype(BF16)
            return jnp.concatenate([zero_half, kept] if upper else [kept, zero_half], axis=0)

        ahead = [level_sums(l) for l in range(GLA_AHEAD)]
        scores = prod = None
        for l in range(N_LEVELS):
            if l + GLA_AHEAD < N_LEVELS:
                ahead.append(level_sums(l + GLA_AHEAD))
            e = [jnp.exp2(x) for x in ahead[l]]
            if l == 0:
                ql = [top_level(q[u], x, d, True) for (u, d), x in zip(lanes, e)]
                kl = [top_level(k[u], x, d, False) for (u, d), x in zip(lanes, e)]
            else:
                ql = [(q[u] * x).astype(BF16) for (u, d), x in zip(lanes, e)]
                kl = [(k[u] * x).astype(BF16) for (u, d), x in zip(lanes, e)]
            if l == 1:
                scores = prod
            elif l > 1:
                own = [lvl_ref[d] == l - 1 for d in range(2)]
                scores = [jnp.where(own[d], p, s) for (u, d), p, s in zip(lanes, prod, scores)]
            prod = [_dot_nt(a, b) for a, b in zip(ql, kl)]
        own = [lvl_ref[d] == N_LEVELS - 1 for d in range(2)]
        scores = [jnp.where(own[d], p, s) for (u, d), p, s in zip(lanes, prod, scores)]
        diag = [_dot_nt(q16[u], k16[u]) for u in range(gg)]
        own = [lvl_ref[d] == N_LEVELS for d in range(2)]
        scores = [jnp.where(own[d], diag[u], s) for (u, d), s in zip(lanes, scores)]
        for (u, d), s, b in zip(lanes, scores, bc):
            idx = d * nc + cs[u]
            btot = b[0:1, :] if d == 1 else b[CH - 1:CH, :]
            oi_ref[idx] = _dot(s.astype(BF16), v16[u])
            qs_ref[idx] = (q[u] * jnp.exp2(b)).astype(BF16)
            kv_ref[idx] = _dot_tn((k[u] * jnp.exp2(btot - b)).astype(BF16), v16[u])
            dec_ref[idx] = jnp.exp2(jnp.broadcast_to(btot, (CH, dk)).T)
        return carry

    lax.fori_loop(0, nc // gg, prep, 0)

    def scan(c, carry):
        sf, sb = carry
        cf = c
        ib = 2 * nc - 1 - c
        st_ref[cf] = sf.astype(BF16)
        st_ref[ib] = sb.astype(BF16)
        ef = dec_ref[cf]
        eb = dec_ref[ib]
        sf = sf * jnp.concatenate([ef, ef], axis=1) + kv_ref[cf]
        sb = sb * jnp.concatenate([eb, eb], axis=1) + kv_ref[ib]
        return sf, sb

    zero = jnp.zeros((dk, dv), F32)
    lax.fori_loop(0, nc, scan, (zero, zero))

    ng = ng_ref[...]
    fg = _group(nc, FIN_GROUP)

    def fin(it, carry):
        cs = [it * fg + u for u in range(fg)]
        rows = [pl.ds(pl.multiple_of(c * CH, CH), CH) for c in cs]
        os_ = [oi_ref[c] + oi_ref[nc + c] + _dot(qs_ref[c], st_ref[c]) + _dot(qs_ref[nc + c], st_ref[nc + c])
               for c in cs]
        inv = [lax.rsqrt(jnp.mean(o * o, axis=-1, keepdims=True) + RMS_EPS) for o in os_]
        for r, o, s in zip(rows, os_, inv):
            o_ref[0, r, :] = (o * s * ng * _silu(p_ref[r, c_r:c_r + dv])).astype(BF16)
        return carry

    lax.fori_loop(0, nc // fg, fin, 0)


def _gla_mixer(xb, w_in, gate_w2, gate_b, norm_g):
    bn, seq, dm = xb.shape
    h, dk, dv = B_HEADS, B_DK, B_DV
    nc = seq // CH
    kw, vw = h * dk, h * dv
    w = w_in

    def heads(cols, width):
        return cols.reshape(dm, h, width).transpose(1, 0, 2)

    gl = jnp.pad(w[:, 2 * kw + 2 * vw:], ((0, 0), (0, dk - 2 * B_RANK)))
    wh = jnp.concatenate([
        heads(w[:, 0:kw], dk), heads(w[:, kw:2 * kw], dk),
        heads(w[:, 2 * kw:2 * kw + vw], dv), heads(w[:, 2 * kw + vw:2 * kw + 2 * vw], dv),
        jnp.broadcast_to(gl[None], (h, dm, dk))], axis=2).astype(BF16)
    w2 = gate_w2.reshape(2, B_RANK, h, dk).transpose(2, 0, 1, 3)
    w2p = jnp.zeros((h, 2, dk, dk), F32)
    w2p = w2p.at[:, 0, 0:B_RANK].set(w2[:, 0]).at[:, 1, B_RANK:2 * B_RANK].set(w2[:, 1]).astype(BF16)
    gb = gate_b.reshape(2, h, dk).transpose(1, 0, 2).astype(F32)
    gb = jnp.broadcast_to(gb[:, :, None, :], (h, 2, 8, dk))
    ng = norm_g.astype(F32).reshape(1, dv)
    seg, lvl = _gla_tables()
    nw = wh.shape[2]

    kern = functools.partial(_gla_kernel, seq=seq)
    return pl.pallas_call(
        kern,
        out_shape=jax.ShapeDtypeStruct((bn, seq, vw), BF16),
        grid=(bn, h),
        in_specs=[
            pl.BlockSpec((1, seq, dm), lambda b, i: (b, 0, 0)),
            pl.BlockSpec((1, dm, nw), lambda b, i: (i, 0, 0)),
            pl.BlockSpec((1, 2, dk, dk), lambda b, i: (i, 0, 0, 0)),
            pl.BlockSpec((1, 2, 8, dk), lambda b, i: (i, 0, 0, 0)),
            pl.BlockSpec((1, dv), lambda b, i: (0, 0)),
            pl.BlockSpec(seg.shape, lambda b, i: (0, 0, 0, 0)),
            pl.BlockSpec(lvl.shape, lambda b, i: (0, 0, 0)),
        ],
        out_specs=pl.BlockSpec((1, seq, dv), lambda b, i: (b, 0, i)),
        scratch_shapes=[
            pltpu.VMEM((seq, nw), F32),
            pltpu.VMEM((2 * nc, CH, dk), BF16),
            pltpu.VMEM((2 * nc, dk, dv), F32),
            pltpu.VMEM((2 * nc, dk, dv), BF16),
            pltpu.VMEM((2 * nc, dk, dk), F32),
            pltpu.VMEM((2 * nc, CH, dv), F32),
        ],
        compiler_params=pltpu.CompilerParams(
            dimension_semantics=("arbitrary", "arbitrary"), vmem_limit_bytes=VMEM_LIMIT),
        name="gla_mixer",
    )(xb, wh, w2p, gb, ng, seg, lvl)


def _post_kernel(o_ref, x_ref, wo_ref, w1_ref, w2_ref, ln_ref, y_ref, yb_ref, *, alpha):
    ln = ln_ref[...]
    x = x_ref[...]
    x1 = _layernorm(alpha * x + _dot(o_ref[...], wo_ref[...]), ln[0:1, :], ln[1:2, :])
    x1b = x1.astype(BF16)
    acc = jnp.zeros(x.shape, F32)
    dff = w1_ref.shape[1]
    for j in range(dff // FF_TILE):
        cols = slice(j * FF_TILE, (j + 1) * FF_TILE)
        hcur = jnp.maximum(_dot(x1b, w1_ref[:, cols]), 0.0)
        acc = acc + _dot((hcur * hcur).astype(BF16), w2_ref[cols, :])
    y = _layernorm(alpha * x1 + acc, ln[2:3, :], ln[3:4, :])
    y_ref[...] = y
    yb_ref[...] = y.astype(BF16)


def _post(o, x, w_out, w1, w2, g1, b1, g2, b2, alpha):
    t, dm = x.shape
    vw = o.shape[1]
    dff = w1.shape[1]
    tm = min(ROW_TILE, t)
    ln = jnp.pad(jnp.stack([g1, b1, g2, b2]).astype(F32), ((0, 4), (0, 0)))
    const = lambda shape: pl.BlockSpec(shape, lambda i: (0, 0), pipeline_mode=pl.Buffered(1))
    return pl.pallas_call(
        functools.partial(_post_kernel, alpha=alpha),
        out_shape=(jax.ShapeDtypeStruct((t, dm), F32), jax.ShapeDtypeStruct((t, dm), BF16)),
        grid=(t // tm,),
        in_specs=[
            pl.BlockSpec((tm, vw), lambda i: (i, 0)),
            pl.BlockSpec((tm, dm), lambda i: (i, 0)),
            const((vw, dm)), const((dm, dff)), const((dff, dm)), const((8, dm)),
        ],
        out_specs=(pl.BlockSpec((tm, dm), lambda i: (i, 0)), pl.BlockSpec((tm, dm), lambda i: (i, 0))),
        compiler_params=pltpu.CompilerParams(
            dimension_semantics=("arbitrary",), vmem_limit_bytes=VMEM_LIMIT),
        name="post",
    )(o, x, w_out.astype(BF16), w1.astype(BF16), w2.astype(BF16), ln)


def kernel(x, a_w_in, a_conv, a_alog, a_dt_bias, a_norm_g, a_w_out, b_w_in, b_gate_w2, b_gate_b,
           b_norm_g, b_w_out, ln1_g, ln1_b, mlp_w1, mlp_w2, ln2_g, ln2_b):
    bn, seq, dm = x.shape
    depth = ln1_g.shape[0]
    alpha = (2 * depth) ** 0.25
    xf = x.astype(F32).reshape(bn * seq, dm)
    xb = xf.astype(BF16)
    for i in range(depth):
        j = i // 2
        xb3 = xb.reshape(bn, seq, dm)
        if i % 2 == 0:
            o = _gdn_mixer(xb3, a_w_in[j], a_conv[j], a_alog[j], a_dt_bias[j], a_norm_g[j])
            w_out = a_w_out[j]
        else:
            o = _gla_mixer(xb3, b_w_in[j], b_gate_w2[j], b_gate_b[j], b_norm_g[j])
            w_out = b_w_out[j]
        xf, xb = _post(o.reshape(bn * seq, -1), xf, w_out, mlp_w1[i], mlp_w2[i],
                       ln1_g[i], ln1_b[i], ln2_g[i], ln2_b[i], alpha)
    return xf.reshape(bn, seq, dm).astype(x.dtype)
```

```python
import functools
import math

import numpy as np

import jax
import jax.numpy as jnp
from jax import lax
from jax.experimental import pallas as pl
from jax.experimental.pallas import tpu as pltpu

F32 = jnp.float32
BF16 = jnp.bfloat16

A_HEADS, A_DK, A_DV, A_CONV = 8, 128, 128, 5
B_HEADS, B_DK, B_DV, B_RANK, B_TAU = 4, 128, 256, 16, 16.0
LN_EPS, RMS_EPS, L2_EPS = 1e-5, 1e-6, 1e-6

CH = 128
N_LEVELS = 7
HALO = 8
CONV_MXU_TAPS = (0, 4)
GATE_ROWS = 16
SOLVE_LEVELS_PER_STAGE = 1
GLA_GROUP = 4
GLA_AHEAD = 2
GLA_PIECES = 2
FIN_GROUP = 4
NEG_BIG = -1e30
LOG2E = math.log2(math.e)
VMEM_LIMIT = 56 * 1024 * 1024
ROW_TILE = 1024
FF_TILE = 1024

assert CH == A_DK == B_DK and 2 ** N_LEVELS == CH


def _dot(a, b):
    return jnp.dot(a, b, preferred_element_type=F32)


def _dot_nt(a, b):
    return lax.dot_general(a, b, (((1,), (1,)), ((), ())), preferred_element_type=F32)


def _dot_tn(a, b):
    return lax.dot_general(a, b, (((0,), (0,)), ((), ())), preferred_element_type=F32)


def _split(x, n, axis=1):
    pieces = []
    for _ in range(n - 1):
        p = x.astype(BF16)
        pieces.append(p)
        x = x - p.astype(F32)
    pieces.append(x.astype(BF16))
    return jnp.concatenate(pieces, axis=axis)


def _fold(y, n, axis=1):
    w = y.shape[axis] // n
    blocks = [lax.slice_in_dim(y, i * w, (i + 1) * w, axis=axis) for i in range(n)]
    out = blocks[0]
    for b in blocks[1:]:
        out = out + b
    return out


def _dot_exact(m01, x):
    return _fold(_dot(m01, _split(x, 3)), 3)


def _sigmoid(x):
    return 0.5 + 0.5 * jnp.tanh(0.5 * x)


def _silu(x):
    h = 0.5 * x
    return h + h * jnp.tanh(h)


def _softplus(x):
    return jnp.maximum(x, 0.0) + jnp.log(1.0 + jnp.exp(-jnp.abs(x)))


def _layernorm(y, g, b):
    mu = jnp.mean(y, axis=-1, keepdims=True)
    yc = y - mu
    var = jnp.mean(yc * yc, axis=-1, keepdims=True)
    return yc * lax.rsqrt(var + LN_EPS) * g + b


def _order_masks(rev):
    row = lax.broadcasted_iota(jnp.int32, (CH, CH), 0)
    col = lax.broadcasted_iota(jnp.int32, (CH, CH), 1)
    if rev:
        return col >= row, col > row
    return col <= row, col < row


def _group(n, want):
    return math.gcd(n, want)


def _gdn_kernel(xb_ref, wh_ref, cw_ref, hp_ref, ng_ref, lm_ref, sh_ref, o_ref,
                p_ref, qkv_ref, gate_ref, a_ref, t_ref, qk_ref, rhs_ref, qd_ref, kd_ref, gl_ref,
                mc_ref, qc_ref, rc_ref, oc_ref, *, seq):
    nc = seq // CH
    spare = 2 * nc

    p_ref[0:HALO, :] = jnp.zeros((HALO, p_ref.shape[1]), F32)
    p_ref[HALO + seq:, :] = jnp.zeros((HALO, p_ref.shape[1]), F32)
    p_ref[HALO:HALO + seq, :] = _dot(xb_ref[0], wh_ref[0])
    for ref in (a_ref, t_ref, qk_ref, kd_ref, rhs_ref, qd_ref):
        ref[spare] = jnp.zeros(ref.shape[1:], ref.dtype)
    qkv_ref[nc] = jnp.zeros(qkv_ref.shape[1:], F32)
    gate_ref[nc] = jnp.zeros(gate_ref.shape[1:], F32)

    cw = cw_ref[0]
    hp = hp_ref[0]

    def tiles(it, lag):
        c = it - lag
        ok = jnp.logical_and(c >= 0, c < nc)
        return jnp.where(ok, c, spare), jnp.where(ok, nc + c, spare)

    def prep_one(c):
        base = pl.multiple_of(c * CH, CH)
        win = p_ref[pl.ds(base, CH + 2 * HALO), 0:3 * A_DK]
        gates = p_ref[pl.ds(base + HALO, CH), 4 * A_DK:5 * A_DK].T[0:GATE_ROWS, :]
        if CONV_MXU_TAPS:
            shifted = _dot(sh_ref[...], win.astype(BF16))
        mid = A_CONV // 2
        acc = win[HALO:HALO + CH, :] * cw[mid:mid + 1, :]
        for i in range(A_CONV):
            if i != mid and i not in CONV_MXU_TAPS:
                off = HALO + i - mid
                acc = acc + win[off:off + CH, :] * cw[i:i + 1, :]
        g_rows = -hp[0] * _softplus(gates + hp[1])
        pieces = _split(g_rows, 3, axis=0)
        gcum = [_fold(_dot(pieces, _order_masks(not rev)[0].astype(BF16)), 3, axis=0) for rev in (False, True)]
        yield
        for n, i in enumerate(CONV_MXU_TAPS):
            acc = acc + shifted[n * CH:(n + 1) * CH, :] * cw[i:i + 1, :]
        yield
        s = _silu(acc)
        q = s[:, 0:A_DK]
        k = s[:, A_DK:2 * A_DK]
        q = q * (lax.rsqrt(jnp.sum(q * q, axis=-1, keepdims=True) + L2_EPS) * (A_DK ** -0.5))
        k = k * lax.rsqrt(jnp.sum(k * k, axis=-1, keepdims=True) + L2_EPS)
        yield
        qkv_ref[c] = jnp.concatenate([q, k, s[:, 2 * A_DK:]], axis=1)
        gate_ref[c] = jnp.concatenate([_sigmoid(gates), gcum[0], gcum[1]], axis=0)
        yield

    def prep_two(jf, jb, qkv, gate):
        q = qkv[:, 0:A_DK]
        k = qkv[:, A_DK:2 * A_DK]
        v = qkv[:, 2 * A_DK:]
        dirs = ((0, False, jf), (1, True, jb))
        gr = [jnp.broadcast_to(gate[(1 + d) * GATE_ROWS + 2 + d:(1 + d) * GATE_ROWS + 3 + d, :], (CH, CH))
              for d, _, _ in dirs]
        gc = [x.T for x in gr]
        beta = [jnp.broadcast_to(gate[d:d + 1, :], (CH, CH)).T for d, _, _ in dirs]
        kb16 = k.astype(BF16)
        eye = (lax.broadcasted_iota(jnp.int32, (CH, CH), 0)
               == lax.broadcasted_iota(jnp.int32, (CH, CH), 1)).astype(F32)
        yield
        kbeta = [k * beta[d] for d, _, _ in dirs]
        kq = [_dot_nt(jnp.concatenate([kbeta[d], q], axis=0).astype(BF16), kb16) for d, _, _ in dirs]
        gtot = [gc[d][0:1, :] if rev else gc[d][CH - 1:CH, :] for d, rev, _ in dirs]
        eg = [jnp.exp2(gc[d]) for d, _, _ in dirs]
        dmat = [jnp.exp2(jnp.where(_order_masks(rev)[0], gc[d] - gr[d], NEG_BIG)) for d, rev, _ in dirs]
        yield
        for d, rev, idx in dirs:
            rhs_ref[idx] = jnp.concatenate([v * beta[d], kbeta[d] * eg[d]], axis=1).astype(BF16)
            qd_ref[idx] = q * eg[d]
            kd_ref[idx] = (k * jnp.exp2(gtot[d] - gc[d])).astype(BF16)
            gl_ref[idx] = jnp.broadcast_to(jnp.exp2(gtot[d]), (8, A_DK))
        yield
        for d, rev, idx in dirs:
            a = jnp.where(_order_masks(rev)[1], kq[d][0:CH] * dmat[d], 0.0).astype(BF16)
            a_ref[idx] = a
            t_ref[idx] = (eye - (a * lm_ref[0]).astype(F32)).astype(BF16)
            qk_ref[idx] = (kq[d][CH:] * dmat[d]).astype(BF16)
        yield

    per_stage = SOLVE_LEVELS_PER_STAGE
    n_stages = (N_LEVELS - 1) // per_stage
    solve_stages = tuple((2 + s, tuple(range(1 + s * per_stage, 1 + (s + 1) * per_stage)))
                         for s in range(n_stages))
    ops_lag = n_stages + 2

    def step(it, stage_one, stage_two):
        fillers = []
        if stage_two:
            c2 = it - 1
            ok = jnp.logical_and(c2 >= 0, c2 < nc)
            cq = jnp.where(ok, c2, nc)
            fillers.append(prep_two(*tiles(it, 1), qkv_ref[cq], gate_ref[cq]))
        loaded = []
        for lag, lvs in solve_stages:
            jf, jb = tiles(it, lag)
            loaded.append((jf, jb, t_ref[jf], t_ref[jb], a_ref[jf], a_ref[jb]))
        ops_in = [(i, t_ref[i], rhs_ref[i], kd_ref[i], qk_ref[i], qd_ref[i]) for i in tiles(it, ops_lag)]
        if stage_one:
            fillers.insert(0, prep_one(it))

        def fill():
            for f in fillers:
                next(f, None)

        uws = [_dot(t, rhs) for i, t, rhs, kd, qkm, qd in ops_in]
        for half in range(per_stage):
            xs = []
            for (lag, lvs), (jf, jb, tf, tb, af, ab) in zip(solve_stages, loaded):
                m = lm_ref[lvs[half]]
                xs.append((_dot(tf, af * m), _dot(tb, ab * m)))
            fill()
            fill()
            ys = [jnp.concatenate([_dot(xf.astype(BF16), tf), _dot(xb.astype(BF16), tb)], axis=1)
                  for (xf, xb), (jf, jb, tf, tb, af, ab) in zip(xs, loaded)]
            if half == 0:
                uws = [uw.astype(BF16) for uw in uws]
                kuws = [_dot_tn(uw, kd) for uw, (i, t, rhs, kd, qkm, qd) in zip(uws, ops_in)]
                quws = [_dot(qkm, uw) for uw, (i, t, rhs, kd, qkm, qd) in zip(uws, ops_in)]
            fill()
            fill()
            loaded = [(jf, jb, tf - y[:, 0:CH].astype(BF16), tb - y[:, CH:].astype(BF16), af, ab)
                      for y, (jf, jb, tf, tb, af, ab) in zip(ys, loaded)]
        for f in fillers:
            for _ in f:
                pass
        for jf, jb, tf, tb, _, _ in loaded:
            t_ref[jf] = tf
            t_ref[jb] = tb
        for kuw, quw, (i, t, rhs, kd, qkm, qd) in zip(kuws, quws, ops_in):
            qc_ref[i] = kuw[0:A_DV, :]
            mc_ref[i] = kuw[A_DV:, :].astype(BF16)
            oc_ref[i] = quw[:, 0:A_DV]
            rc_ref[i] = (qd - quw[:, A_DV:]).astype(BF16)

    def loop(lo, hi, stage_one, stage_two):
        def body(it, carry):
            step(it, stage_one, stage_two)
            return carry
        lax.fori_loop(lo, hi, body, 0)

    loop(0, nc, True, True)
    loop(nc, nc + 1, False, True)
    loop(nc + 1, nc + ops_lag, False, False)

    def scan(c, carry):
        sf, sb = carry
        cf = c
        cb = 2 * nc - 1 - c
        sf16 = sf.astype(BF16)
        sb16 = sb.astype(BF16)
        of = _dot_nt(rc_ref[cf], sf16) + oc_ref[cf]
        ob = _dot_nt(rc_ref[cb], sb16) + oc_ref[cb]
        oc_ref[cf] = of
        oc_ref[cb] = ob
        sf = gl_ref[cf][0:1, :] * sf - _dot(sf16, mc_ref[cf]) + qc_ref[cf]
        sb = gl_ref[cb][0:1, :] * sb - _dot(sb16, mc_ref[cb]) + qc_ref[cb]
        return sf, sb

    zero = jnp.zeros((A_DV, A_DK), F32)
    lax.fori_loop(0, nc, scan, (zero, zero))

    ng = ng_ref[...]
    fg = _group(nc, FIN_GROUP)

    def fin(it, carry):
        cs = [it * fg + u for u in range(fg)]
        rows = [pl.ds(pl.multiple_of(c * CH, CH), CH) for c in cs]
        os_ = [oc_ref[c] + oc_ref[nc + c] for c in cs]
        inv = [lax.rsqrt(jnp.mean(o * o, axis=-1, keepdims=True) + RMS_EPS) for o in os_]
        for c, r, o, s in zip(cs, rows, os_, inv):
            z = p_ref[pl.ds(pl.multiple_of(c * CH, CH) + HALO, CH), 3 * A_DK:4 * A_DK]
            o_ref[0, r, :] = (o * s * ng * _silu(z)).astype(BF16)
        return carry

    lax.fori_loop(0, nc // fg, fin, 0)


def _gdn_level_masks():
    idx = np.arange(CH)
    x = idx[:, None] ^ idx[None, :]
    return jnp.asarray(np.stack([(x >> lv) == 1 for lv in range(N_LEVELS)]), BF16)


def _conv_shift_matrices():
    t = np.arange(CH)[:, None]
    r = np.arange(CH + 2 * HALO)[None, :]
    taps = CONV_MXU_TAPS or (0,)
    return jnp.asarray(np.concatenate([r == t + HALO + i - A_CONV // 2 for i in taps], axis=0), BF16)


def _gdn_mixer(xb, w_in, conv_w, a_log, dt_bias, norm_g):
    bn, seq, dm = xb.shape
    h, dk = A_HEADS, A_DK
    nc = seq // CH
    w = w_in
    hw = h * dk
    ba = w[:, 4 * hw:].reshape(dm, 2, 2, h)
    per_head = [w[:, i * hw:(i + 1) * hw].reshape(dm, h, dk).transpose(1, 0, 2) for i in range(4)]
    gate_cols = jnp.pad(ba.reshape(dm, 4, h).transpose(2, 0, 1), ((0, 0), (0, 0), (0, dk - 4)))
    wh = jnp.concatenate(per_head + [gate_cols], axis=2).astype(BF16)
    cw = conv_w.reshape(A_CONV, 3, h, dk).transpose(2, 0, 1, 3).reshape(h, A_CONV, 3 * dk)
    cw = jnp.pad(cw, ((0, 0), (0, 8 - A_CONV), (0, 0))).astype(F32)
    scale = jnp.zeros((h, GATE_ROWS), F32).at[:, 2:4].set((jnp.exp(a_log.astype(F32)) * LOG2E).T)
    bias = jnp.zeros((h, GATE_ROWS), F32).at[:, 2:4].set(dt_bias.astype(F32).T)
    hp = jnp.broadcast_to(jnp.stack([scale, bias], axis=1)[:, :, :, None], (h, 2, GATE_ROWS, dk))
    ng = norm_g.astype(F32).reshape(1, A_DV)
    lm = _gdn_level_masks()
    sh = _conv_shift_matrices()
    nw = wh.shape[2]

    kern = functools.partial(_gdn_kernel, seq=seq)
    tile = lambda dt: pltpu.VMEM((2 * nc + 1, CH, CH), dt)
    return pl.pallas_call(
        kern,
        out_shape=jax.ShapeDtypeStruct((bn, seq, h * A_DV), BF16),
        grid=(bn, h),
        in_specs=[
            pl.BlockSpec((1, seq, dm), lambda b, i: (b, 0, 0)),
            pl.BlockSpec((1, dm, nw), lambda b, i: (i, 0, 0)),
            pl.BlockSpec((1, 8, 3 * dk), lambda b, i: (i, 0, 0)),
            pl.BlockSpec((1, 2, GATE_ROWS, dk), lambda b, i: (i, 0, 0, 0)),
            pl.BlockSpec((1, A_DV), lambda b, i: (0, 0)),
            pl.BlockSpec(lm.shape, lambda b, i: (0, 0, 0)),
            pl.BlockSpec(sh.shape, lambda b, i: (0, 0)),
        ],
        out_specs=pl.BlockSpec((1, seq, A_DV), lambda b, i: (b, 0, i)),
        scratch_shapes=[
            pltpu.VMEM((seq + 2 * HALO, nw), F32),
            pltpu.VMEM((nc + 1, CH, 3 * A_DK), F32),
            pltpu.VMEM((nc + 1, 3 * GATE_ROWS, CH), F32),
            tile(BF16),
            tile(BF16),
            tile(BF16),
            pltpu.VMEM((2 * nc + 1, CH, A_DV + A_DK), BF16),
            tile(F32),
            tile(BF16),
            pltpu.VMEM((2 * nc + 1, 8, A_DK), F32),
            tile(BF16),
            tile(F32),
            tile(BF16),
            tile(F32),
        ],
        compiler_params=pltpu.CompilerParams(
            dimension_semantics=("arbitrary", "arbitrary"), vmem_limit_bytes=VMEM_LIMIT),
        name="gdn_mixer",
    )(xb, wh, cw, hp, ng, lm, sh)


def _gla_tables():
    i = np.arange(CH)[:, None]
    t = np.arange(CH)[None, :]
    seg = np.zeros((2, N_LEVELS + 1, CH, CH), np.float32)
    lvl = np.zeros((2, CH, CH), np.int32)
    for d in range(2):
        rev = d == 1
        seg[d, 0] = (t >= i) if rev else (t <= i)
        lv = np.full((CH, CH), N_LEVELS + 1, np.int32)
        lv[np.arange(CH), np.arange(CH)] = N_LEVELS
        x = i ^ t
        for l in range(N_LEVELS):
            h = 2 ** (N_LEVELS - 1 - l)
            b0 = (i // (2 * h)) * (2 * h)
            if rev:
                r = b0 + h
                late = i < r
                m = np.where(late, (t >= i) & (t < r), (t >= r) & (t < i))
                own = ((x >> (N_LEVELS - 1 - l)) == 1) & (t > i)
            else:
                r = b0 + h - 1
                late = i > r
                m = np.where(late, (t > r) & (t <= i), (t > i) & (t <= r))
                own = ((x >> (N_LEVELS - 1 - l)) == 1) & (t < i)
            seg[d, 1 + l] = m
            lv[own] = l
        lvl[d] = lv
    return jnp.asarray(seg, BF16), jnp.asarray(lvl)


def _gla_kernel(xb_ref, wh_ref, w2_ref, gb_ref, ng_ref, seg_ref, lvl_ref, o_ref,
                p_ref, qs_ref, kv_ref, st_ref, dec_ref, oi_ref, *, seq):
    nc = seq // CH
    dk, dv = B_DK, B_DV
    p_ref[...] = _dot(xb_ref[0], wh_ref[0])

    c_q, c_k, c_v, c_r, c_g = 0, dk, 2 * dk, 2 * dk + dv, 2 * dk + 2 * dv
    gg = _group(nc, GLA_GROUP)
    lanes = [(u, d) for u in range(gg) for d in range(2)]

    def prep(it, carry):
        cs = [it * gg + u for u in range(gg)]
        rows = [pl.ds(pl.multiple_of(c * CH, CH), CH) for c in cs]
        q = [p_ref[r, c_q:c_q + dk] * (dk ** -0.5) for r in rows]
        k = [p_ref[r, c_k:c_k + dk] for r in rows]
        q16 = [x.astype(BF16) for x in q]
        k16 = [x.astype(BF16) for x in k]
        v16 = [p_ref[r, c_v:c_v + dv].astype(BF16) for r in rows]
        gin = [p_ref[r, c_g:c_g + dk].astype(BF16) for r in rows]
        logit = [_dot(gin[u], w2_ref[0, d]) + gb_ref[0, d][0:1, :] for u, d in lanes]
        la3 = [_split(-_softplus(-x) * (LOG2E / B_TAU), 3) for x in logit]
        la2 = [y[:, 0:GLA_PIECES * dk] for y in la3]
        bc = [_fold(_dot(seg_ref[d, 0], y), 3) for (u, d), y in zip(lanes, la3)]

        def level_sums(l):
            h = CH >> (l + 1)
            if h < HALO:
                return [_fold(_dot(seg_ref[d, 1 + l], y), GLA_PIECES) for (u, d), y in zip(lanes, la2)]
            out = []
            for (u, d), b in zip(lanes, bc):
                blocks = []
                for lo in range(0, CH, 2 * h):
                    if d == 1:
                        ref = b[lo + h:lo + h + 1, :]
                        blocks += [b[lo:lo + h, :] - ref, ref - b[lo + h:lo + 2 * h, :]]
                    else:
                        ref = b[lo + h - 1:lo + h, :]
                        blocks += [ref - b[lo:lo + h, :], b[lo + h:lo + 2 * h, :] - ref]
                out.append(jnp.concatenate(blocks, axis=0))
            return out

        half = CH // 2
        zero_half = jnp.zeros((half, dk), BF16)

        def top_level(x, e, d, late):
            upper = (d == 1) != late
            rows = slice(half, CH) if upper else slice(0, half)
            kept = (x[rows] * e[rows]).astype(BF16)
            return jnp.concatenate([zero_half, kept] if upper else [kept, zero_half], axis=0)

        ahead = [level_sums(l) for l in range(GLA_AHEAD)]
        scores = prod = None
        for l in range(N_LEVELS):
            if l + GLA_AHEAD < N_LEVELS:
                ahead.append(level_sums(l + GLA_AHEAD))
            e = [jnp.exp2(x) for x in ahead[l]]
            if l == 0:
                ql = [top_level(q[u], x, d, True) for (u, d), x in zip(lanes, e)]
                kl = [top_level(k[u], x, d, False) for (u, d), x in zip(lanes, e)]
            else:
                ql = [(q[u] * x).astype(BF16) for (u, d), x in zip(lanes, e)]
                kl = [(k[u] * x).astype(BF16) for (u, d), x in zip(lanes, e)]
            if l == 1:
                scores = prod
            elif l > 1:
                own = [lvl_ref[d] == l - 1 for d in range(2)]
                scores = [jnp.where(own[d], p, s) for (u, d), p, s in zip(lanes, prod, scores)]
            prod = [_dot_nt(a, b) for a, b in zip(ql, kl)]
        own = [lvl_ref[d] == N_LEVELS - 1 for d in range(2)]
        scores = [jnp.where(own[d], p, s) for (u, d), p, s in zip(lanes, prod, scores)]
        diag = [_dot_nt(q16[u], k16[u]) for u in range(gg)]
        own = [lvl_ref[d] == N_LEVELS for d in range(2)]
        scores = [jnp.where(own[d], diag[u], s) for (u, d), s in zip(lanes, scores)]
        for (u, d), s, b in zip(lanes, scores, bc):
            idx = d * nc + cs[u]
            btot = b[0:1, :] if d == 1 else b[CH - 1:CH, :]
            oi_ref[idx] = _dot(s.astype(BF16), v16[u])
            qs_ref[idx] = (q[u] * jnp.exp2(b)).astype(BF16)
            kv_ref[idx] = _dot_tn((k[u] * jnp.exp2(btot - b)).astype(BF16), v16[u])
            dec_ref[idx] = jnp.exp2(jnp.broadcast_to(btot, (CH, dk)).T)
        return carry

    lax.fori_loop(0, nc // gg, prep, 0)

    def scan(c, carry):
        sf, sb = carry
        cf = c
        ib = 2 * nc - 1 - c
        st_ref[cf] = sf.astype(BF16)
        st_ref[ib] = sb.astype(BF16)
        ef = dec_ref[cf]
        eb = dec_ref[ib]
        sf = sf * jnp.concatenate([ef, ef], axis=1) + kv_ref[cf]
        sb = sb * jnp.concatenate([eb, eb], axis=1) + kv_ref[ib]
        return sf, sb

    zero = jnp.zeros((dk, dv), F32)
    lax.fori_loop(0, nc, scan, (zero, zero))

    ng = ng_ref[...]
    fg = _group(nc, FIN_GROUP)

    def fin(it, carry):
        cs = [it * fg + u for u in range(fg)]
        rows = [pl.ds(pl.multiple_of(c * CH, CH), CH) for c in cs]
        os_ = [oi_ref[c] + oi_ref[nc + c] + _dot(qs_ref[c], st_ref[c]) + _dot(qs_ref[nc + c], st_ref[nc + c])
               for c in cs]
        inv = [lax.rsqrt(jnp.mean(o * o, axis=-1, keepdims=True) + RMS_EPS) for o in os_]
        for r, o, s in zip(rows, os_, inv):
            o_ref[0, r, :] = (o * s * ng * _silu(p_ref[r, c_r:c_r + dv])).astype(BF16)
        return carry

    lax.fori_loop(0, nc // fg, fin, 0)


def _gla_mixer(xb, w_in, gate_w2, gate_b, norm_g):
    bn, seq, dm = xb.shape
    h, dk, dv = B_HEADS, B_DK, B_DV
    nc = seq // CH
    kw, vw = h * dk, h * dv
    w = w_in

    def heads(cols, width):
        return cols.reshape(dm, h, width).transpose(1, 0, 2)

    gl = jnp.pad(w[:, 2 * kw + 2 * vw:], ((0, 0), (0, dk - 2 * B_RANK)))
    wh = jnp.concatenate([
        heads(w[:, 0:kw], dk), heads(w[:, kw:2 * kw], dk),
        heads(w[:, 2 * kw:2 * kw + vw], dv), heads(w[:, 2 * kw + vw:2 * kw + 2 * vw], dv),
        jnp.broadcast_to(gl[None], (h, dm, dk))], axis=2).astype(BF16)
    w2 = gate_w2.reshape(2, B_RANK, h, dk).transpose(2, 0, 1, 3)
    w2p = jnp.zeros((h, 2, dk, dk), F32)
    w2p = w2p.at[:, 0, 0:B_RANK].set(w2[:, 0]).at[:, 1, B_RANK:2 * B_RANK].set(w2[:, 1]).astype(BF16)
    gb = gate_b.reshape(2, h, dk).transpose(1, 0, 2).astype(F32)
    gb = jnp.broadcast_to(gb[:, :, None, :], (h, 2, 8, dk))
    ng = norm_g.astype(F32).reshape(1, dv)
    seg, lvl = _gla_tables()
    nw = wh.shape[2]

    kern = functools.partial(_gla_kernel, seq=seq)
    return pl.pallas_call(
        kern,
        out_shape=jax.ShapeDtypeStruct((bn, seq, vw), BF16),
        grid=(bn, h),
        in_specs=[
            pl.BlockSpec((1, seq, dm), lambda b, i: (b, 0, 0)),
            pl.BlockSpec((1, dm, nw), lambda b, i: (i, 0, 0)),
            pl.BlockSpec((1, 2, dk, dk), lambda b, i: (i, 0, 0, 0)),
            pl.BlockSpec((1, 2, 8, dk), lambda b, i: (i, 0, 0, 0)),
            pl.BlockSpec((1, dv), lambda b, i: (0, 0)),
            pl.BlockSpec(seg.shape, lambda b, i: (0, 0, 0, 0)),
            pl.BlockSpec(lvl.shape, lambda b, i: (0, 0, 0)),
        ],
        out_specs=pl.BlockSpec((1, seq, dv), lambda b, i: (b, 0, i)),
        scratch_shapes=[
            pltpu.VMEM((seq, nw), F32),
            pltpu.VMEM((2 * nc, CH, dk), BF16),
            pltpu.VMEM((2 * nc, dk, dv), F32),
            pltpu.VMEM((2 * nc, dk, dv), BF16),
            pltpu.VMEM((2 * nc, dk, dk), F32),
            pltpu.VMEM((2 * nc, CH, dv), F32),
        ],
        compiler_params=pltpu.CompilerParams(
            dimension_semantics=("arbitrary", "arbitrary"), vmem_limit_bytes=VMEM_LIMIT),
        name="gla_mixer",
    )(xb, wh, w2p, gb, ng, seg, lvl)


def _post_kernel(o_ref, x_ref, wo_ref, w1_ref, w2_ref, ln_ref, y_ref, yb_ref, *, alpha):
    ln = ln_ref[...]
    x = x_ref[...]
    x1 = _layernorm(alpha * x + _dot(o_ref[...], wo_ref[...]), ln[0:1, :], ln[1:2, :])
    x1b = x1.astype(BF16)
    acc = jnp.zeros(x.shape, F32)
    dff = w1_ref.shape[1]
    for j in range(dff // FF_TILE):
        cols = slice(j * FF_TILE, (j + 1) * FF_TILE)
        hcur = jnp.maximum(_dot(x1b, w1_ref[:, cols]), 0.0)
        acc = acc + _dot((hcur * hcur).astype(BF16), w2_ref[cols, :])
    y = _layernorm(alpha * x1 + acc, ln[2:3, :], ln[3:4, :])
    y_ref[...] = y
    yb_ref[...] = y.astype(BF16)


def _post(o, x, w_out, w1, w2, g1, b1, g2, b2, alpha):
    t, dm = x.shape
    vw = o.shape[1]
    dff = w1.shape[1]
    tm = min(ROW_TILE, t)
    ln = jnp.pad(jnp.stack([g1, b1, g2, b2]).astype(F32), ((0, 4), (0, 0)))
    const = lambda shape: pl.BlockSpec(shape, lambda i: (0, 0), pipeline_mode=pl.Buffered(1))
    return pl.pallas_call(
        functools.partial(_post_kernel, alpha=alpha),
        out_shape=(jax.ShapeDtypeStruct((t, dm), F32), jax.ShapeDtypeStruct((t, dm), BF16)),
        grid=(t // tm,),
        in_specs=[
            pl.BlockSpec((tm, vw), lambda i: (i, 0)),
            pl.BlockSpec((tm, dm), lambda i: (i, 0)),
            const((vw, dm)), const((dm, dff)), const((dff, dm)), const((8, dm)),
        ],
        out_specs=(pl.BlockSpec((tm, dm), lambda i: (i, 0)), pl.BlockSpec((tm, dm), lambda i: (i, 0))),
        compiler_params=pltpu.CompilerParams(
            dimension_semantics=("arbitrary",), vmem_limit_bytes=VMEM_LIMIT),
        name="post",
    )(o, x, w_out.astype(BF16), w1.astype(BF16), w2.astype(BF16), ln)


def kernel(x, a_w_in, a_conv, a_alog, a_dt_bias, a_norm_g, a_w_out, b_w_in, b_gate_w2, b_gate_b,
           b_norm_g, b_w_out, ln1_g, ln1_b, mlp_w1, mlp_w2, ln2_g, ln2_b):
    bn, seq, dm = x.shape
    depth = ln1_g.shape[0]
    alpha = (2 * depth) ** 0.25
    xf = x.astype(F32).reshape(bn * seq, dm)
    xb = xf.astype(BF16)
    for i in range(depth):
        j = i // 2
        xb3 = xb.reshape(bn, seq, dm)
        if i % 2 == 0:
            o = _gdn_mixer(xb3, a_w_in[j], a_conv[j], a_alog[j], a_dt_bias[j], a_norm_g[j])
            w_out = a_w_out[j]
        else:
            o = _gla_mixer(xb3, b_w_in[j], b_gate_w2[j], b_gate_b[j], b_norm_g[j])
            w_out = b_w_out[j]
        xf, xb = _post(o.reshape(bn * seq, -1), xf, w_out, mlp_w1[i], mlp_w2[i],
                       ln1_g[i], ln1_b[i], ln2_g[i], ln2_b[i], alpha)
    return xf.reshape(bn, seq, dm).astype(x.dtype)
```

```python
import functools
import math

import numpy as np

import jax
import jax.numpy as jnp
from jax import lax
from jax.experimental import pallas as pl
from jax.experimental.pallas import tpu as pltpu

F32 = jnp.float32
BF16 = jnp.bfloat16

A_HEADS, A_DK, A_DV, A_CONV = 8, 128, 128, 5
B_HEADS, B_DK, B_DV, B_RANK, B_TAU = 4, 128, 256, 16, 16.0
LN_EPS, RMS_EPS, L2_EPS = 1e-5, 1e-6, 1e-6

CH = 128
N_LEVELS = 7
HALO = 8
CONV_MXU_TAPS = (0, 4)
GATE_ROWS = 16
SOLVE_LEVELS_PER_STAGE = 1
GLA_GROUP = 4
GLA_AHEAD = 2
GLA_PIECES = 2
FIN_GROUP = 4
NEG_BIG = -1e30
LOG2E = math.log2(math.e)
VMEM_LIMIT = 56 * 1024 * 1024
ROW_TILE = 1024
FF_TILE = 1024

assert CH == A_DK == B_DK and 2 ** N_LEVELS == CH


def _dot(a, b):
    return jnp.dot(a, b, preferred_element_type=F32)


def _dot_nt(a, b):
    return lax.dot_general(a, b, (((1,), (1,)), ((), ())), preferred_element_type=F32)


def _dot_tn(a, b):
    return lax.dot_general(a, b, (((0,), (0,)), ((), ())), preferred_element_type=F32)


def _split(x, n, axis=1):
    pieces = []
    for _ in range(n - 1):
        p = x.astype(BF16)
        pieces.append(p)
        x = x - p.astype(F32)
    pieces.append(x.astype(BF16))
    return jnp.concatenate(pieces, axis=axis)


def _fold(y, n, axis=1):
    w = y.shape[axis] // n
    blocks = [lax.slice_in_dim(y, i * w, (i + 1) * w, axis=axis) for i in range(n)]
    out = blocks[0]
    for b in blocks[1:]:
        out = out + b
    return out


def _dot_exact(m01, x):
    return _fold(_dot(m01, _split(x, 3)), 3)


def _sigmoid(x):
    return 0.5 + 0.5 * jnp.tanh(0.5 * x)


def _silu(x):
    h = 0.5 * x
    return h + h * jnp.tanh(h)


def _softplus(x):
    return jnp.maximum(x, 0.0) + jnp.log(1.0 + jnp.exp(-jnp.abs(x)))


def _layernorm(y, g, b):
    mu = jnp.mean(y, axis=-1, keepdims=True)
    yc = y - mu
    var = jnp.mean(yc * yc, axis=-1, keepdims=True)
    return yc * lax.rsqrt(var + LN_EPS) * g + b


def _order_masks(rev):
    row = lax.broadcasted_iota(jnp.int32, (CH, CH), 0)
    col = lax.broadcasted_iota(jnp.int32, (CH, CH), 1)
    if rev:
        return col >= row, col > row
    return col <= row, col < row


def _group(n, want):
    return math.gcd(n, want)


def _gdn_kernel(xb_ref, wh_ref, cw_ref, hp_ref, ng_ref, lm_ref, sh_ref, o_ref,
                p_ref, qkv_ref, gate_ref, a_ref, t_ref, qk_ref, rhs_ref, qd_ref, kd_ref, gl_ref,
                mc_ref, qc_ref, rc_ref, oc_ref, *, seq):
    nc = seq // CH
    spare = 2 * nc

    p_ref[0:HALO, :] = jnp.zeros((HALO, p_ref.shape[1]), F32)
    p_ref[HALO + seq:, :] = jnp.zeros((HALO, p_ref.shape[1]), F32)
    p_ref[HALO:HALO + seq, :] = _dot(xb_ref[0], wh_ref[0])
    for ref in (a_ref, t_ref, qk_ref, kd_ref, rhs_ref, qd_ref):
        ref[spare] = jnp.zeros(ref.shape[1:], ref.dtype)
    qkv_ref[nc] = jnp.zeros(qkv_ref.shape[1:], F32)
    gate_ref[nc] = jnp.zeros(gate_ref.shape[1:], F32)

    cw = cw_ref[0]
    hp = hp_ref[0]

    def tiles(it, lag):
        c = it - lag
        ok = jnp.logical_and(c >= 0, c < nc)
        return jnp.where(ok, c, spare), jnp.where(ok, nc + c, spare)

    def prep_one(c):
        base = pl.multiple_of(c * CH, CH)
        win = p_ref[pl.ds(base, CH + 2 * HALO), 0:3 * A_DK]
        gates = p_ref[pl.ds(base + HALO, CH), 4 * A_DK:5 * A_DK].T[0:GATE_ROWS, :]
        if CONV_MXU_TAPS:
            shifted = _dot(sh_ref[...], win.astype(BF16))
        mid = A_CONV // 2
        acc = win[HALO:HALO + CH, :] * cw[mid:mid + 1, :]
        for i in range(A_CONV):
            if i != mid and i not in CONV_MXU_TAPS:
                off = HALO + i - mid
                acc = acc + win[off:off + CH, :] * cw[i:i + 1, :]
        g_rows = -hp[0] * _softplus(gates + hp[1])
        pieces = _split(g_rows, 3, axis=0)
        gcum = [_fold(_dot(pieces, _order_masks(not rev)[0].astype(BF16)), 3, axis=0) for rev in (False, True)]
        yield
        for n, i in enumerate(CONV_MXU_TAPS):
            acc = acc + shifted[n * CH:(n + 1) * CH, :] * cw[i:i + 1, :]
        yield
        s = _silu(acc)
        q = s[:, 0:A_DK]
        k = s[:, A_DK:2 * A_DK]
        q = q * (lax.rsqrt(jnp.sum(q * q, axis=-1, keepdims=True) + L2_EPS) * (A_DK ** -0.5))
        k = k * lax.rsqrt(jnp.sum(k * k, axis=-1, keepdims=True) + L2_EPS)
        yield
        qkv_ref[c] = jnp.concatenate([q, k, s[:, 2 * A_DK:]], axis=1)
        gate_ref[c] = jnp.concatenate([_sigmoid(gates), gcum[0], gcum[1]], axis=0)
        yield

    def prep_two(jf, jb, qkv, gate, gblk):
        q = qkv[:, 0:A_DK]
        k = qkv[:, A_DK:2 * A_DK]
        v = qkv[:, 2 * A_DK:]
        dirs = ((0, False, jf), (1, True, jb))
        gr = [jnp.broadcast_to(gate[(1 + d) * GATE_ROWS + 2 + d:(1 + d) * GATE_ROWS + 3 + d, :], (CH, CH))
              for d, _, _ in dirs]
        gc = [x.T for x in gr]
        beta = [_sigmoid(jnp.broadcast_to(gblk[:, d:d + 1], (CH, CH))) for d, _, _ in dirs]
        kb16 = k.astype(BF16)
        eye = (lax.broadcasted_iota(jnp.int32, (CH, CH), 0)
               == lax.broadcasted_iota(jnp.int32, (CH, CH), 1)).astype(F32)
        yield
        kbeta = [k * beta[d] for d, _, _ in dirs]
        kq = [_dot_nt(jnp.concatenate([kbeta[d], q], axis=0).astype(BF16), kb16) for d, _, _ in dirs]
        gtot = [gc[d][0:1, :] if rev else gc[d][CH - 1:CH, :] for d, rev, _ in dirs]
        eg = [jnp.exp2(gc[d]) for d, _, _ in dirs]
        dmat = [jnp.exp2(jnp.where(_order_masks(rev)[0], gc[d] - gr[d], NEG_BIG)) for d, rev, _ in dirs]
        yield
        for d, rev, idx in dirs:
            rhs_ref[idx] = jnp.concatenate([v * beta[d], kbeta[d] * eg[d]], axis=1).astype(BF16)
            qd_ref[idx] = q * eg[d]
            kd_ref[idx] = (k * jnp.exp2(gtot[d] - gc[d])).astype(BF16)
            gl_ref[idx] = jnp.broadcast_to(jnp.exp2(gtot[d]), (8, A_DK))
        yield
        for d, rev, idx in dirs:
            a = jnp.where(_order_masks(rev)[1], kq[d][0:CH] * dmat[d], 0.0).astype(BF16)
            a_ref[idx] = a
            t_ref[idx] = (eye - (a * lm_ref[0]).astype(F32)).astype(BF16)
            qk_ref[idx] = (kq[d][CH:] * dmat[d]).astype(BF16)
        yield

    per_stage = SOLVE_LEVELS_PER_STAGE
    n_stages = (N_LEVELS - 1) // per_stage
    solve_stages = tuple((2 + s, tuple(range(1 + s * per_stage, 1 + (s + 1) * per_stage)))
                         for s in range(n_stages))
    ops_lag = n_stages + 2

    def step(it, stage_one, stage_two):
        fillers = []
        if stage_two:
            c2 = it - 1
            ok = jnp.logical_and(c2 >= 0, c2 < nc)
            cq = jnp.where(ok, c2, nc)
            rows2 = pl.ds(pl.multiple_of(jnp.clip(c2, 0, nc - 1) * CH, CH) + HALO, CH)
            fillers.append(prep_two(*tiles(it, 1), qkv_ref[cq], gate_ref[cq], p_ref[rows2, 4 * A_DK:5 * A_DK]))
        loaded = []
        for lag, lvs in solve_stages:
            jf, jb = tiles(it, lag)
            loaded.append((jf, jb, t_ref[jf], t_ref[jb], a_ref[jf], a_ref[jb]))
        ops_in = [(i, t_ref[i], rhs_ref[i], kd_ref[i], qk_ref[i], qd_ref[i]) for i in tiles(it, ops_lag)]
        if stage_one:
            fillers.insert(0, prep_one(it))

        def fill():
            for f in fillers:
                next(f, None)

        uws = [_dot(t, rhs) for i, t, rhs, kd, qkm, qd in ops_in]
        for half in range(per_stage):
            xs = []
            for (lag, lvs), (jf, jb, tf, tb, af, ab) in zip(solve_stages, loaded):
                m = lm_ref[lvs[half]]
                xs.append((_dot(tf, af * m), _dot(tb, ab * m)))
            fill()
            fill()
            ys = [jnp.concatenate([_dot(xf.astype(BF16), tf), _dot(xb.astype(BF16), tb)], axis=1)
                  for (xf, xb), (jf, jb, tf, tb, af, ab) in zip(xs, loaded)]
            if half == 0:
                uws = [uw.astype(BF16) for uw in uws]
                kuws = [_dot_tn(kd, uw) for uw, (i, t, rhs, kd, qkm, qd) in zip(uws, ops_in)]
                quws = [_dot(qkm, uw) for uw, (i, t, rhs, kd, qkm, qd) in zip(uws, ops_in)]
            fill()
            fill()
            loaded = [(jf, jb, tf - y[:, 0:CH].astype(BF16), tb - y[:, CH:].astype(BF16), af, ab)
                      for y, (jf, jb, tf, tb, af, ab) in zip(ys, loaded)]
        for f in fillers:
            for _ in f:
                pass
        for jf, jb, tf, tb, _, _ in loaded:
            t_ref[jf] = tf
            t_ref[jb] = tb
        for kuw, quw, (i, t, rhs, kd, qkm, qd) in zip(kuws, quws, ops_in):
            qc_ref[i] = kuw[:, 0:A_DV]
            mc_ref[i] = kuw[:, A_DV:].astype(BF16)
            oc_ref[i] = quw[:, 0:A_DV]
            rc_ref[i] = (qd - quw[:, A_DV:]).astype(BF16)

    def loop(lo, hi, stage_one, stage_two):
        def body(it, carry):
            step(it, stage_one, stage_two)
            return carry
        lax.fori_loop(lo, hi, body, 0)

    loop(0, nc, True, True)
    loop(nc, nc + 1, False, True)
    loop(nc + 1, nc + ops_lag, False, False)

    def scan(c, carry):
        sf, sb = carry
        cf = c
        cb = 2 * nc - 1 - c
        of = _dot(rc_ref[cf], sf.astype(BF16)) + oc_ref[cf]
        ob = _dot(rc_ref[cb], sb.astype(BF16)) + oc_ref[cb]
        oc_ref[cf] = of
        oc_ref[cb] = ob
        sf = gl_ref[cf][0:1, :] * sf - _dot(mc_ref[cf], sf.astype(BF16)) + qc_ref[cf]
        sb = gl_ref[cb][0:1, :] * sb - _dot(mc_ref[cb], sb.astype(BF16)) + qc_ref[cb]
        return sf, sb

    zero = jnp.zeros((A_DK, A_DV), F32)
    lax.fori_loop(0, nc, scan, (zero, zero))

    ng = ng_ref[...]
    fg = _group(nc, FIN_GROUP)

    def fin(it, carry):
        cs = [it * fg + u for u in range(fg)]
        rows = [pl.ds(pl.multiple_of(c * CH, CH), CH) for c in cs]
        os_ = [oc_ref[c] + oc_ref[nc + c] for c in cs]
        inv = [lax.rsqrt(jnp.mean(o * o, axis=-1, keepdims=True) + RMS_EPS) for o in os_]
        for c, r, o, s in zip(cs, rows, os_, inv):
            z = p_ref[pl.ds(pl.multiple_of(c * CH, CH) + HALO, CH), 3 * A_DK:4 * A_DK]
            o_ref[0, r, :] = (o * s * ng * _silu(z)).astype(BF16)
        return carry

    lax.fori_loop(0, nc // fg, fin, 0)


def _gdn_level_masks():
    idx = np.arange(CH)
    x = idx[:, None] ^ idx[None, :]
    return jnp.asarray(np.stack([(x >> lv) == 1 for lv in range(N_LEVELS)]), BF16)


def _conv_shift_matrices():
    t = np.arange(CH)[:, None]
    r = np.arange(CH + 2 * HALO)[None, :]
    taps = CONV_MXU_TAPS or (0,)
    return jnp.asarray(np.concatenate([r == t + HALO + i - A_CONV // 2 for i in taps], axis=0), BF16)


def _gdn_mixer(xb, w_in, conv_w, a_log, dt_bias, norm_g):
    bn, seq, dm = xb.shape
    h, dk = A_HEADS, A_DK
    nc = seq // CH
    w = w_in
    hw = h * dk
    ba = w[:, 4 * hw:].reshape(dm, 2, 2, h)
    per_head = [w[:, i * hw:(i + 1) * hw].reshape(dm, h, dk).transpose(1, 0, 2) for i in range(4)]
    gate_cols = jnp.pad(ba.reshape(dm, 4, h).transpose(2, 0, 1), ((0, 0), (0, 0), (0, dk - 4)))
    wh = jnp.concatenate(per_head + [gate_cols], axis=2).astype(BF16)
    cw = conv_w.reshape(A_CONV, 3, h, dk).transpose(2, 0, 1, 3).reshape(h, A_CONV, 3 * dk)
    cw = jnp.pad(cw, ((0, 0), (0, 8 - A_CONV), (0, 0))).astype(F32)
    scale = jnp.zeros((h, GATE_ROWS), F32).at[:, 2:4].set((jnp.exp(a_log.astype(F32)) * LOG2E).T)
    bias = jnp.zeros((h, GATE_ROWS), F32).at[:, 2:4].set(dt_bias.astype(F32).T)
    hp = jnp.broadcast_to(jnp.stack([scale, bias], axis=1)[:, :, :, None], (h, 2, GATE_ROWS, dk))
    ng = norm_g.astype(F32).reshape(1, A_DV)
    lm = _gdn_level_masks()
    sh = _conv_shift_matrices()
    nw = wh.shape[2]

    kern = functools.partial(_gdn_kernel, seq=seq)
    tile = lambda dt: pltpu.VMEM((2 * nc + 1, CH, CH), dt)
    return pl.pallas_call(
        kern,
        out_shape=jax.ShapeDtypeStruct((bn, seq, h * A_DV), BF16),
        grid=(bn, h),
        in_specs=[
            pl.BlockSpec((1, seq, dm), lambda b, i: (b, 0, 0)),
            pl.BlockSpec((1, dm, nw), lambda b, i: (i, 0, 0)),
            pl.BlockSpec((1, 8, 3 * dk), lambda b, i: (i, 0, 0)),
            pl.BlockSpec((1, 2, GATE_ROWS, dk), lambda b, i: (i, 0, 0, 0)),
            pl.BlockSpec((1, A_DV), lambda b, i: (0, 0)),
            pl.BlockSpec(lm.shape, lambda b, i: (0, 0, 0)),
            pl.BlockSpec(sh.shape, lambda b, i: (0, 0)),
        ],
        out_specs=pl.BlockSpec((1, seq, A_DV), lambda b, i: (b, 0, i)),
        scratch_shapes=[
            pltpu.VMEM((seq + 2 * HALO, nw), F32),
            pltpu.VMEM((nc + 1, CH, 3 * A_DK), F32),
            pltpu.VMEM((nc + 1, 3 * GATE_ROWS, CH), F32),
            tile(BF16),
            tile(BF16),
            tile(BF16),
            pltpu.VMEM((2 * nc + 1, CH, A_DV + A_DK), BF16),
            tile(F32),
            tile(BF16),
            pltpu.VMEM((2 * nc + 1, 8, A_DK), F32),
            tile(BF16),
            tile(F32),
            tile(BF16),
            tile(F32),
        ],
        compiler_params=pltpu.CompilerParams(
            dimension_semantics=("arbitrary", "arbitrary"), vmem_limit_bytes=VMEM_LIMIT),
        name="gdn_mixer",
    )(xb, wh, cw, hp, ng, lm, sh)


def _gla_tables():
    i = np.arange(CH)[:, None]
    t = np.arange(CH)[None, :]
    seg = np.zeros((2, N_LEVELS + 1, CH, CH), np.float32)
    lvl = np.zeros((2, CH, CH), np.int32)
    for d in range(2):
        rev = d == 1
        seg[d, 0] = (t >= i) if rev else (t <= i)
        lv = np.full((CH, CH), N_LEVELS + 1, np.int32)
        lv[np.arange(CH), np.arange(CH)] = N_LEVELS
        x = i ^ t
        for l in range(N_LEVELS):
            h = 2 ** (N_LEVELS - 1 - l)
            b0 = (i // (2 * h)) * (2 * h)
            if rev:
                r = b0 + h
                late = i < r
                m = np.where(late, (t >= i) & (t < r), (t >= r) & (t < i))
                own = ((x >> (N_LEVELS - 1 - l)) == 1) & (t > i)
            else:
                r = b0 + h - 1
                late = i > r
                m = np.where(late, (t > r) & (t <= i), (t > i) & (t <= r))
                own = ((x >> (N_LEVELS - 1 - l)) == 1) & (t < i)
            seg[d, 1 + l] = m
            lv[own] = l
        lvl[d] = lv
    return jnp.asarray(seg, BF16), jnp.asarray(lvl)


def _gla_kernel(xb_ref, wh_ref, w2_ref, gb_ref, ng_ref, seg_ref, lvl_ref, o_ref,
                p_ref, qs_ref, kv_ref, st_ref, dec_ref, oi_ref, *, seq):
    nc = seq // CH
    dk, dv = B_DK, B_DV
    p_ref[...] = _dot(xb_ref[0], wh_ref[0])

    c_q, c_k, c_v, c_r, c_g = 0, dk, 2 * dk, 2 * dk + dv, 2 * dk + 2 * dv
    gg = _group(nc, GLA_GROUP)
    lanes = [(u, d) for u in range(gg) for d in range(2)]

    def prep(it, carry):
        cs = [it * gg + u for u in range(gg)]
        rows = [pl.ds(pl.multiple_of(c * CH, CH), CH) for c in cs]
        q = [p_ref[r, c_q:c_q + dk] * (dk ** -0.5) for r in rows]
        k = [p_ref[r, c_k:c_k + dk] for r in rows]
        q16 = [x.astype(BF16) for x in q]
        k16 = [x.astype(BF16) for x in k]
        v16 = [p_ref[r, c_v:c_v + dv].astype(BF16) for r in rows]
        gin = [p_ref[r, c_g:c_g + dk].astype(BF16) for r in rows]
        logit = [_dot(gin[u], w2_ref[0, d]) + gb_ref[0, d][0:1, :] for u, d in lanes]
        la3 = [_split(-_softplus(-x) * (LOG2E / B_TAU), 3) for x in logit]
        la2 = [y[:, 0:GLA_PIECES * dk] for y in la3]
        bc = [_fold(_dot(seg_ref[d, 0], y), 3) for (u, d), y in zip(lanes, la3)]

        def level_sums(l):
            h = CH >> (l + 1)
            if h < HALO:
                return [_fold(_dot(seg_ref[d, 1 + l], y), GLA_PIECES) for (u, d), y in zip(lanes, la2)]
            out = []
            for (u, d), b in zip(lanes, bc):
                blocks = []
                for lo in range(0, CH, 2 * h):
                    if d == 1:
                        ref = b[lo + h:lo + h + 1, :]
                        blocks += [b[lo:lo + h, :] - ref, ref - b[lo + h:lo + 2 * h, :]]
                    else:
                        ref = b[lo + h - 1:lo + h, :]
                        blocks += [ref - b[lo:lo + h, :], b[lo + h:lo + 2 * h, :] - ref]
                out.append(jnp.concatenate(blocks, axis=0))
            return out

        half = CH // 2
        zero_half = jnp.zeros((half, dk), BF16)

        def top_level(x, e, d, late):
            upper = (d == 1) != late
            rows = slice(half, CH) if upper else slice(0, half)
            kept = (x[rows] * e[rows]).astype(BF16)
            return jnp.concatenate([zero_half, kept] if upper else [kept, zero_half], axis=0)

        ahead = [level_sums(l) for l in range(GLA_AHEAD)]
        scores = prod = None
        for l in range(N_LEVELS):
            if l + GLA_AHEAD < N_LEVELS:
                ahead.append(level_sums(l + GLA_AHEAD))
            e = [jnp.exp2(x) for x in ahead[l]]
            if l == 0:
                ql = [top_level(q[u], x, d, True) for (u, d), x in zip(lanes, e)]
                kl = [top_level(k[u], x, d, False) for (u, d), x in zip(lanes, e)]
            else:
                ql = [(q[u] * x).astype(BF16) for (u, d), x in zip(lanes, e)]
                kl = [(k[u] * x).astype(BF16) for (u, d), x in zip(lanes, e)]
            if l == 1:
                scores = prod
            elif l > 1:
                own = [lvl_ref[d] == l - 1 for d in range(2)]
                scores = [jnp.where(own[d], p, s) for (u, d), p, s in zip(lanes, prod, scores)]
            prod = [_dot_nt(a, b) for a, b in zip(ql, kl)]
        own = [lvl_ref[d] == N_LEVELS - 1 for d in range(2)]
        scores = [jnp.where(own[d], p, s) for (u, d), p, s in zip(lanes, prod, scores)]
        diag = [_dot_nt(q16[u], k16[u]) for u in range(gg)]
        own = [lvl_ref[d] == N_LEVELS for d in range(2)]
        scores = [jnp.where(own[d], diag[u], s) for (u, d), s in zip(lanes, scores)]
        for (u, d), s, b in zip(lanes, scores, bc):
            idx = d * nc + cs[u]
            btot = b[0:1, :] if d == 1 else b[CH - 1:CH, :]
            oi_ref[idx] = _dot(s.astype(BF16), v16[u])
            qs_ref[idx] = (q[u] * jnp.exp2(b)).astype(BF16)
            kv_ref[idx] = _dot_tn((k[u] * jnp.exp2(btot - b)).astype(BF16), v16[u])
            dec_ref[idx] = jnp.exp2(jnp.broadcast_to(btot, (CH, dk)).T)
        return carry

    lax.fori_loop(0, nc // gg, prep, 0)

    def scan(c, carry):
        sf, sb = carry
        cf = c
        ib = 2 * nc - 1 - c
        st_ref[cf] = sf.astype(BF16)
        st_ref[ib] = sb.astype(BF16)
        ef = dec_ref[cf]
        eb = dec_ref[ib]
        sf = sf * jnp.concatenate([ef, ef], axis=1) + kv_ref[cf]
        sb = sb * jnp.concatenate([eb, eb], axis=1) + kv_ref[ib]
        return sf, sb

    zero = jnp.zeros((dk, dv), F32)
    lax.fori_loop(0, nc, scan, (zero, zero))

    ng = ng_ref[...]
    fg = _group(nc, FIN_GROUP)

    def fin(it, carry):
        cs = [it * fg + u for u in range(fg)]
        rows = [pl.ds(pl.multiple_of(c * CH, CH), CH) for c in cs]
        os_ = [oi_ref[c] + oi_ref[nc + c] + _dot(qs_ref[c], st_ref[c]) + _dot(qs_ref[nc + c], st_ref[nc + c])
               for c in cs]
        inv = [lax.rsqrt(jnp.mean(o * o, axis=-1, keepdims=True) + RMS_EPS) for o in os_]
        for r, o, s in zip(rows, os_, inv):
            o_ref[0, r, :] = (o * s * ng * _silu(p_ref[r, c_r:c_r + dv])).astype(BF16)
        return carry

    lax.fori_loop(0, nc // fg, fin, 0)


def _gla_mixer(xb, w_in, gate_w2, gate_b, norm_g):
    bn, seq, dm = xb.shape
    h, dk, dv = B_HEADS, B_DK, B_DV
    nc = seq // CH
    kw, vw = h * dk, h * dv
    w = w_in

    def heads(cols, width):
        return cols.reshape(dm, h, width).transpose(1, 0, 2)

    gl = jnp.pad(w[:, 2 * kw + 2 * vw:], ((0, 0), (0, dk - 2 * B_RANK)))
    wh = jnp.concatenate([
        heads(w[:, 0:kw], dk), heads(w[:, kw:2 * kw], dk),
        heads(w[:, 2 * kw:2 * kw + vw], dv), heads(w[:, 2 * kw + vw:2 * kw + 2 * vw], dv),
        jnp.broadcast_to(gl[None], (h, dm, dk))], axis=2).astype(BF16)
    w2 = gate_w2.reshape(2, B_RANK, h, dk).transpose(2, 0, 1, 3)
    w2p = jnp.zeros((h, 2, dk, dk), F32)
    w2p = w2p.at[:, 0, 0:B_RANK].set(w2[:, 0]).at[:, 1, B_RANK:2 * B_RANK].set(w2[:, 1]).astype(BF16)
    gb = gate_b.reshape(2, h, dk).transpose(1, 0, 2).astype(F32)
    gb = jnp.broadcast_to(gb[:, :, None, :], (h, 2, 8, dk))
    ng = norm_g.astype(F32).reshape(1, dv)
    seg, lvl = _gla_tables()
    nw = wh.shape[2]

    kern = functools.partial(_gla_kernel, seq=seq)
    return pl.pallas_call(
        kern,
        out_shape=jax.ShapeDtypeStruct((bn, seq, vw), BF16),
        grid=(bn, h),
        in_specs=[
            pl.BlockSpec((1, seq, dm), lambda b, i: (b, 0, 0)),
            pl.BlockSpec((1, dm, nw), lambda b, i: (i, 0, 0)),
            pl.BlockSpec((1, 2, dk, dk), lambda b, i: (i, 0, 0, 0)),
            pl.BlockSpec((1, 2, 8, dk), lambda b, i: (i, 0, 0, 0)),
            pl.BlockSpec((1, dv), lambda b, i: (0, 0)),
            pl.BlockSpec(seg.shape, lambda b, i: (0, 0, 0, 0)),
            pl.BlockSpec(lvl.shape, lambda b, i: (0, 0, 0)),
        ],
        out_specs=pl.BlockSpec((1, seq, dv), lambda b, i: (b, 0, i)),
        scratch_shapes=[
            pltpu.VMEM((seq, nw), F32),
            pltpu.VMEM((2 * nc, CH, dk), BF16),
            pltpu.VMEM((2 * nc, dk, dv), F32),
            pltpu.VMEM((2 * nc, dk, dv), BF16),
            pltpu.VMEM((2 * nc, dk, dk), F32),
            pltpu.VMEM((2 * nc, CH, dv), F32),
        ],
        compiler_params=pltpu.CompilerParams(
            dimension_semantics=("arbitrary", "arbitrary"), vmem_limit_bytes=VMEM_LIMIT),
        name="gla_mixer",
    )(xb, wh, w2p, gb, ng, seg, lvl)


def _post_kernel(o_ref, x_ref, wo_ref, w1_ref, w2_ref, ln_ref, y_ref, yb_ref, *, alpha):
    ln = ln_ref[...]
    x = x_ref[...]
    x1 = _layernorm(alpha * x + _dot(o_ref[...], wo_ref[...]), ln[0:1, :], ln[1:2, :])
    x1b = x1.astype(BF16)
    acc = jnp.zeros(x.shape, F32)
    dff = w1_ref.shape[1]
    for j in range(dff // FF_TILE):
        cols = slice(j * FF_TILE, (j + 1) * FF_TILE)
        hcur = jnp.maximum(_dot(x1b, w1_ref[:, cols]), 0.0)
        acc = acc + _dot((hcur * hcur).astype(BF16), w2_ref[cols, :])
    y = _layernorm(alpha * x1 + acc, ln[2:3, :], ln[3:4, :])
    y_ref[...] = y
    yb_ref[...] = y.astype(BF16)


def _post(o, x, w_out, w1, w2, g1, b1, g2, b2, alpha):
    t, dm = x.shape
    vw = o.shape[1]
    dff = w1.shape[1]
    tm = min(ROW_TILE, t)
    ln = jnp.pad(jnp.stack([g1, b1, g2, b2]).astype(F32), ((0, 4), (0, 0)))
    const = lambda shape: pl.BlockSpec(shape, lambda i: (0, 0), pipeline_mode=pl.Buffered(1))
    return pl.pallas_call(
        functools.partial(_post_kernel, alpha=alpha),
        out_shape=(jax.ShapeDtypeStruct((t, dm), F32), jax.ShapeDtypeStruct((t, dm), BF16)),
        grid=(t // tm,),
        in_specs=[
            pl.BlockSpec((tm, vw), lambda i: (i, 0)),
            pl.BlockSpec((tm, dm), lambda i: (i, 0)),
            const((vw, dm)), const((dm, dff)), const((dff, dm)), const((8, dm)),
        ],
        out_specs=(pl.BlockSpec((tm, dm), lambda i: (i, 0)), pl.BlockSpec((tm, dm), lambda i: (i, 0))),
        compiler_params=pltpu.CompilerParams(
            dimension_semantics=("arbitrary",), vmem_limit_bytes=VMEM_LIMIT),
        name="post",
    )(o, x, w_out.astype(BF16), w1.astype(BF16), w2.astype(BF16), ln)


def kernel(x, a_w_in, a_conv, a_alog, a_dt_bias, a_norm_g, a_w_out, b_w_in, b_gate_w2, b_gate_b,
           b_norm_g, b_w_out, ln1_g, ln1_b, mlp_w1, mlp_w2, ln2_g, ln2_b):
    bn, seq, dm = x.shape
    depth = ln1_g.shape[0]
    alpha = (2 * depth) ** 0.25
    xf = x.astype(F32).reshape(bn * seq, dm)
    xb = xf.astype(BF16)
    for i in range(depth):
        j = i // 2
        xb3 = xb.reshape(bn, seq, dm)
        if i % 2 == 0:
            o = _gdn_mixer(xb3, a_w_in[j], a_conv[j], a_alog[j], a_dt_bias[j], a_norm_g[j])
            w_out = a_w_out[j]
        else:
            o = _gla_mixer(xb3, b_w_in[j], b_gate_w2[j], b_gate_b[j], b_norm_g[j])
            w_out = b_w_out[j]
        xf, xb = _post(o.reshape(bn * seq, -1), xf, w_out, mlp_w1[i], mlp_w2[i],
                       ln1_g[i], ln1_b[i], ln2_g[i], ln2_b[i], alpha)
    return xf.reshape(bn, seq, dm).astype(x.dtype)
```

```python
import functools
import math

import numpy as np

import jax
import jax.numpy as jnp
from jax import lax
from jax.experimental import pallas as pl
from jax.experimental.pallas import tpu as pltpu

F32 = jnp.float32
BF16 = jnp.bfloat16

A_HEADS, A_DK, A_DV, A_CONV = 8, 128, 128, 5
B_HEADS, B_DK, B_DV, B_RANK, B_TAU = 4, 128, 256, 16, 16.0
LN_EPS, RMS_EPS, L2_EPS = 1e-5, 1e-6, 1e-6

CH = 128
N_LEVELS = 7
HALO = 8
CONV_MXU_TAPS = (0, 4)
GATE_ROWS = 16
SOLVE_LEVELS_PER_STAGE = 1
GLA_GROUP = 4
GLA_AHEAD = 2
GLA_PIECES = 2
FIN_GROUP = 4
NEG_BIG = -1e30
LOG2E = math.log2(math.e)
VMEM_LIMIT = 56 * 1024 * 1024
ROW_TILE = 1024
FF_TILE = 1024

assert CH == A_DK == B_DK and 2 ** N_LEVELS == CH


def _dot(a, b):
    return jnp.dot(a, b, preferred_element_type=F32)


def _dot_nt(a, b):
    return lax.dot_general(a, b, (((1,), (1,)), ((), ())), preferred_element_type=F32)


def _dot_tn(a, b):
    return lax.dot_general(a, b, (((0,), (0,)), ((), ())), preferred_element_type=F32)


def _split(x, n, axis=1):
    pieces = []
    for _ in range(n - 1):
        p = x.astype(BF16)
        pieces.append(p)
        x = x - p.astype(F32)
    pieces.append(x.astype(BF16))
    return jnp.concatenate(pieces, axis=axis)


def _fold(y, n, axis=1):
    w = y.shape[axis] // n
    blocks = [lax.slice_in_dim(y, i * w, (i + 1) * w, axis=axis) for i in range(n)]
    out = blocks[0]
    for b in blocks[1:]:
        out = out + b
    return out


def _dot_exact(m01, x):
    return _fold(_dot(m01, _split(x, 3)), 3)


def _sigmoid(x):
    return 0.5 + 0.5 * jnp.tanh(0.5 * x)


def _silu(x):
    h = 0.5 * x
    return h + h * jnp.tanh(h)


def _softplus(x):
    return jnp.maximum(x, 0.0) + jnp.log(1.0 + jnp.exp(-jnp.abs(x)))


def _layernorm(y, g, b):
    mu = jnp.mean(y, axis=-1, keepdims=True)
    yc = y - mu
    var = jnp.mean(yc * yc, axis=-1, keepdims=True)
    return yc * lax.rsqrt(var + LN_EPS) * g + b


def _order_masks(rev):
    row = lax.broadcasted_iota(jnp.int32, (CH, CH), 0)
    col = lax.broadcasted_iota(jnp.int32, (CH, CH), 1)
    if rev:
        return col >= row, col > row
    return col <= row, col < row


def _group(n, want):
    return math.gcd(n, want)


def _gdn_kernel(xb_ref, wh_ref, cw_ref, hp_ref, ng_ref, lm_ref, sh_ref, o_ref, p_ref, *scratch, seq):
    g = pl.program_id(0)
    cur = (g + 1) % 2
    nxt = g % 2
    p_ref[nxt, 0:HALO, :] = jnp.zeros((HALO, p_ref.shape[2]), F32)
    p_ref[nxt, HALO + seq:, :] = jnp.zeros((HALO, p_ref.shape[2]), F32)

    @pl.when(g == 0)
    def _():
        p_ref[nxt, HALO:HALO + seq, :] = _dot(xb_ref[0], wh_ref[0])

    @pl.when(g > 0)
    def _():
        _gdn_body(xb_ref, wh_ref, cw_ref, hp_ref, ng_ref, lm_ref, sh_ref, o_ref, p_ref, *scratch,
                  seq=seq, cur=cur, nxt=nxt)


def _gdn_body(xb_ref, wh_ref, cw_ref, hp_ref, ng_ref, lm_ref, sh_ref, o_ref,
              p_ref, qkv_ref, gate_ref, a_ref, t_ref, qk_ref, rhs_ref, qd_ref, kd_ref, gl_ref,
              mc_ref, qc_ref, rc_ref, oc_ref, *, seq, cur, nxt):
    nc = seq // CH
    spare = 2 * nc
    for ref in (a_ref, t_ref, qk_ref, kd_ref, rhs_ref, qd_ref):
        ref[spare] = jnp.zeros(ref.shape[1:], ref.dtype)
    qkv_ref[nc] = jnp.zeros(qkv_ref.shape[1:], F32)
    gate_ref[nc] = jnp.zeros(gate_ref.shape[1:], F32)

    cw = cw_ref[0]
    hp = hp_ref[0]

    def tiles(it, lag):
        c = it - lag
        ok = jnp.logical_and(c >= 0, c < nc)
        return jnp.where(ok, c, spare), jnp.where(ok, nc + c, spare)

    def prep_one(c):
        base = pl.multiple_of(c * CH, CH)
        win = p_ref[cur, pl.ds(base, CH + 2 * HALO), 0:3 * A_DK]
        gates = p_ref[cur, pl.ds(base + HALO, CH), 4 * A_DK:5 * A_DK].T[0:GATE_ROWS, :]
        if CONV_MXU_TAPS:
            shifted = _dot(sh_ref[...], win.astype(BF16))
        mid = A_CONV // 2
        acc = win[HALO:HALO + CH, :] * cw[mid:mid + 1, :]
        for i in range(A_CONV):
            if i != mid and i not in CONV_MXU_TAPS:
                off = HALO + i - mid
                acc = acc + win[off:off + CH, :] * cw[i:i + 1, :]
        g_rows = -hp[0] * _softplus(gates + hp[1])
        pieces = _split(g_rows, 3, axis=0)
        gcum = [_fold(_dot(pieces, _order_masks(not rev)[0].astype(BF16)), 3, axis=0) for rev in (False, True)]
        yield
        for n, i in enumerate(CONV_MXU_TAPS):
            acc = acc + shifted[n * CH:(n + 1) * CH, :] * cw[i:i + 1, :]
        yield
        s = _silu(acc)
        q = s[:, 0:A_DK]
        k = s[:, A_DK:2 * A_DK]
        q = q * (lax.rsqrt(jnp.sum(q * q, axis=-1, keepdims=True) + L2_EPS) * (A_DK ** -0.5))
        k = k * lax.rsqrt(jnp.sum(k * k, axis=-1, keepdims=True) + L2_EPS)
        yield
        qkv_ref[c] = jnp.concatenate([q, k, s[:, 2 * A_DK:]], axis=1)
        gate_ref[c] = jnp.concatenate([_sigmoid(gates), gcum[0], gcum[1]], axis=0)
        yield

    def prep_two(jf, jb, qkv, gate):
        q = qkv[:, 0:A_DK]
        k = qkv[:, A_DK:2 * A_DK]
        v = qkv[:, 2 * A_DK:]
        dirs = ((0, False, jf), (1, True, jb))
        gr = [jnp.broadcast_to(gate[(1 + d) * GATE_ROWS + 2 + d:(1 + d) * GATE_ROWS + 3 + d, :], (CH, CH))
              for d, _, _ in dirs]
        gc = [x.T for x in gr]
        beta = [jnp.broadcast_to(gate[d:d + 1, :], (CH, CH)).T for d, _, _ in dirs]
        kb16 = k.astype(BF16)
        eye = (lax.broadcasted_iota(jnp.int32, (CH, CH), 0)
               == lax.broadcasted_iota(jnp.int32, (CH, CH), 1)).astype(F32)
        yield
        kbeta = [k * beta[d] for d, _, _ in dirs]
        kq = [_dot_nt(jnp.concatenate([kbeta[d], q], axis=0).astype(BF16), kb16) for d, _, _ in dirs]
        gtot = [gc[d][0:1, :] if rev else gc[d][CH - 1:CH, :] for d, rev, _ in dirs]
        eg = [jnp.exp2(gc[d]) for d, _, _ in dirs]
        dmat = [jnp.exp2(jnp.where(_order_masks(rev)[0], gc[d] - gr[d], NEG_BIG)) for d, rev, _ in dirs]
        yield
        for d, rev, idx in dirs:
            rhs_ref[idx] = jnp.concatenate([v * beta[d], kbeta[d] * eg[d]], axis=1).astype(BF16)
            qd_ref[idx] = q * eg[d]
            kd_ref[idx] = (k * jnp.exp2(gtot[d] - gc[d])).astype(BF16)
            gl_ref[idx] = jnp.broadcast_to(jnp.exp2(gtot[d]), (8, A_DK))
        yield
        for d, rev, idx in dirs:
            a = jnp.where(_order_masks(rev)[1], kq[d][0:CH] * dmat[d], 0.0).astype(BF16)
            a_ref[idx] = a
            t_ref[idx] = (eye - (a * lm_ref[0]).astype(F32)).astype(BF16)
            qk_ref[idx] = (kq[d][CH:] * dmat[d]).astype(BF16)
        yield

    per_stage = SOLVE_LEVELS_PER_STAGE
    n_stages = (N_LEVELS - 1) // per_stage
    solve_stages = tuple((2 + s, tuple(range(1 + s * per_stage, 1 + (s + 1) * per_stage)))
                         for s in range(n_stages))
    ops_lag = n_stages + 2
    tail_rows = seq // ops_lag
    assert tail_rows * ops_lag == seq and tail_rows % HALO == 0

    def step(it, stage_one, stage_two):
        fillers = []
        if stage_two:
            c2 = it - 1
            ok = jnp.logical_and(c2 >= 0, c2 < nc)
            cq = jnp.where(ok, c2, nc)
            fillers.append(prep_two(*tiles(it, 1), qkv_ref[cq], gate_ref[cq]))
        loaded = []
        for lag, lvs in solve_stages:
            jf, jb = tiles(it, lag)
            loaded.append((jf, jb, t_ref[jf], t_ref[jb], a_ref[jf], a_ref[jb]))
        ops_in = [(i, t_ref[i], rhs_ref[i], kd_ref[i], qk_ref[i], qd_ref[i]) for i in tiles(it, ops_lag)]
        if stage_one:
            fillers.insert(0, prep_one(it))
        else:
            r0 = pl.multiple_of((it - nc) * tail_rows, HALO)
            proj = _dot(xb_ref[0, pl.ds(r0, tail_rows), :], wh_ref[0])

        def fill():
            for f in fillers:
                next(f, None)

        uws = [_dot(t, rhs) for i, t, rhs, kd, qkm, qd in ops_in]
        for half in range(per_stage):
            xs = []
            for (lag, lvs), (jf, jb, tf, tb, af, ab) in zip(solve_stages, loaded):
                m = lm_ref[lvs[half]]
                xs.append((_dot(tf, af * m), _dot(tb, ab * m)))
            fill()
            fill()
            ys = [jnp.concatenate([_dot(xf.astype(BF16), tf), _dot(xb.astype(BF16), tb)], axis=1)
                  for (xf, xb), (jf, jb, tf, tb, af, ab) in zip(xs, loaded)]
            if half == 0:
                uws = [uw.astype(BF16) for uw in uws]
                kuws = [_dot_tn(kd, uw) for uw, (i, t, rhs, kd, qkm, qd) in zip(uws, ops_in)]
                quws = [_dot(qkm, uw) for uw, (i, t, rhs, kd, qkm, qd) in zip(uws, ops_in)]
            fill()
            fill()
            loaded = [(jf, jb, tf - y[:, 0:CH].astype(BF16), tb - y[:, CH:].astype(BF16), af, ab)
                      for y, (jf, jb, tf, tb, af, ab) in zip(ys, loaded)]
        for f in fillers:
            for _ in f:
                pass
        if not stage_one:
            p_ref[nxt, pl.ds(r0 + HALO, tail_rows), :] = proj
        for jf, jb, tf, tb, _, _ in loaded:
            t_ref[jf] = tf
            t_ref[jb] = tb
        for kuw, quw, (i, t, rhs, kd, qkm, qd) in zip(kuws, quws, ops_in):
            qc_ref[i] = kuw[:, 0:A_DV]
            mc_ref[i] = kuw[:, A_DV:].astype(BF16)
            oc_ref[i] = quw[:, 0:A_DV]
            rc_ref[i] = (qd - quw[:, A_DV:]).astype(BF16)

    def loop(lo, hi, stage_one, stage_two):
        def body(it, carry):
            step(it, stage_one, stage_two)
            return carry
        lax.fori_loop(lo, hi, body, 0)

    loop(0, nc, True, True)
    loop(nc, nc + 1, False, True)
    loop(nc + 1, nc + ops_lag, False, False)

    def scan(c, carry):
        sf, sb = carry
        cf = c
        cb = 2 * nc - 1 - c
        of = _dot(rc_ref[cf], sf.astype(BF16)) + oc_ref[cf]
        ob = _dot(rc_ref[cb], sb.astype(BF16)) + oc_ref[cb]
        oc_ref[cf] = of
        oc_ref[cb] = ob
        sf = gl_ref[cf][0:1, :] * sf - _dot(mc_ref[cf], sf.astype(BF16)) + qc_ref[cf]
        sb = gl_ref[cb][0:1, :] * sb - _dot(mc_ref[cb], sb.astype(BF16)) + qc_ref[cb]
        return sf, sb

    zero = jnp.zeros((A_DK, A_DV), F32)
    lax.fori_loop(0, nc, scan, (zero, zero))

    ng = ng_ref[...]
    fg = _group(nc, FIN_GROUP)

    def fin(it, carry):
        cs = [it * fg + u for u in range(fg)]
        rows = [pl.ds(pl.multiple_of(c * CH, CH), CH) for c in cs]
        os_ = [oc_ref[c] + oc_ref[nc + c] for c in cs]
        inv = [lax.rsqrt(jnp.mean(o * o, axis=-1, keepdims=True) + RMS_EPS) for o in os_]
        for c, r, o, s in zip(cs, rows, os_, inv):
            z = p_ref[cur, pl.ds(pl.multiple_of(c * CH, CH) + HALO, CH), 3 * A_DK:4 * A_DK]
            o_ref[0, r, :] = (o * s * ng * _silu(z)).astype(BF16)
        return carry

    lax.fori_loop(0, nc // fg, fin, 0)


def _gdn_level_masks():
    idx = np.arange(CH)
    x = idx[:, None] ^ idx[None, :]
    return jnp.asarray(np.stack([(x >> lv) == 1 for lv in range(N_LEVELS)]), BF16)


def _conv_shift_matrices():
    t = np.arange(CH)[:, None]
    r = np.arange(CH + 2 * HALO)[None, :]
    taps = CONV_MXU_TAPS or (0,)
    return jnp.asarray(np.concatenate([r == t + HALO + i - A_CONV // 2 for i in taps], axis=0), BF16)


def _gdn_mixer(xb, w_in, conv_w, a_log, dt_bias, norm_g):
    bn, seq, dm = xb.shape
    h, dk = A_HEADS, A_DK
    nc = seq // CH
    w = w_in
    hw = h * dk
    ba = w[:, 4 * hw:].reshape(dm, 2, 2, h)
    per_head = [w[:, i * hw:(i + 1) * hw].reshape(dm, h, dk).transpose(1, 0, 2) for i in range(4)]
    gate_cols = jnp.pad(ba.reshape(dm, 4, h).transpose(2, 0, 1), ((0, 0), (0, 0), (0, dk - 4)))
    wh = jnp.concatenate(per_head + [gate_cols], axis=2).astype(BF16)
    cw = conv_w.reshape(A_CONV, 3, h, dk).transpose(2, 0, 1, 3).reshape(h, A_CONV, 3 * dk)
    cw = jnp.pad(cw, ((0, 0), (0, 8 - A_CONV), (0, 0))).astype(F32)
    scale = jnp.zeros((h, GATE_ROWS), F32).at[:, 2:4].set((jnp.exp(a_log.astype(F32)) * LOG2E).T)
    bias = jnp.zeros((h, GATE_ROWS), F32).at[:, 2:4].set(dt_bias.astype(F32).T)
    hp = jnp.broadcast_to(jnp.stack([scale, bias], axis=1)[:, :, :, None], (h, 2, GATE_ROWS, dk))
    ng = norm_g.astype(F32).reshape(1, A_DV)
    lm = _gdn_level_masks()
    sh = _conv_shift_matrices()
    nw = wh.shape[2]

    kern = functools.partial(_gdn_kernel, seq=seq)
    tile = lambda dt: pltpu.VMEM((2 * nc + 1, CH, CH), dt)
    last = bn * h - 1
    proj_of = lambda g: jnp.minimum(g, last)
    mix_of = lambda g: jnp.maximum(g - 1, 0)
    return pl.pallas_call(
        kern,
        out_shape=jax.ShapeDtypeStruct((bn, seq, h * A_DV), BF16),
        grid=(last + 2,),
        in_specs=[
            pl.BlockSpec((1, seq, dm), lambda g: (proj_of(g) // h, 0, 0)),
            pl.BlockSpec((1, dm, nw), lambda g: (proj_of(g) % h, 0, 0)),
            pl.BlockSpec((1, 8, 3 * dk), lambda g: (mix_of(g) % h, 0, 0)),
            pl.BlockSpec((1, 2, GATE_ROWS, dk), lambda g: (mix_of(g) % h, 0, 0, 0)),
            pl.BlockSpec((1, A_DV), lambda g: (0, 0)),
            pl.BlockSpec(lm.shape, lambda g: (0, 0, 0)),
            pl.BlockSpec(sh.shape, lambda g: (0, 0)),
        ],
        out_specs=pl.BlockSpec((1, seq, A_DV), lambda g: (mix_of(g) // h, 0, mix_of(g) % h)),
        scratch_shapes=[
            pltpu.VMEM((2, seq + 2 * HALO, nw), F32),
            pltpu.VMEM((nc + 1, CH, 3 * A_DK), F32),
            pltpu.VMEM((nc + 1, 3 * GATE_ROWS, CH), F32),
            tile(BF16),
            tile(BF16),
            tile(BF16),
            pltpu.VMEM((2 * nc + 1, CH, A_DV + A_DK), BF16),
            tile(F32),
            tile(BF16),
            pltpu.VMEM((2 * nc + 1, 8, A_DK), F32),
            tile(BF16),
            tile(F32),
            tile(BF16),
            tile(F32),
        ],
        compiler_params=pltpu.CompilerParams(
            dimension_semantics=("arbitrary",), vmem_limit_bytes=VMEM_LIMIT),
        name="gdn_mixer",
    )(xb, wh, cw, hp, ng, lm, sh)


def _gla_tables():
    i = np.arange(CH)[:, None]
    t = np.arange(CH)[None, :]
    seg = np.zeros((2, N_LEVELS + 1, CH, CH), np.float32)
    lvl = np.zeros((2, CH, CH), np.int32)
    for d in range(2):
        rev = d == 1
        seg[d, 0] = (t >= i) if rev else (t <= i)
        lv = np.full((CH, CH), N_LEVELS + 1, np.int32)
        lv[np.arange(CH), np.arange(CH)] = N_LEVELS
        x = i ^ t
        for l in range(N_LEVELS):
            h = 2 ** (N_LEVELS - 1 - l)
            b0 = (i // (2 * h)) * (2 * h)
            if rev:
                r = b0 + h
                late = i < r
                m = np.where(late, (t >= i) & (t < r), (t >= r) & (t < i))
                own = ((x >> (N_LEVELS - 1 - l)) == 1) & (t > i)
            else:
                r = b0 + h - 1
                late = i > r
                m = np.where(late, (t > r) & (t <= i), (t > i) & (t <= r))
                own = ((x >> (N_LEVELS - 1 - l)) == 1) & (t < i)
            seg[d, 1 + l] = m
            lv[own] = l
        lvl[d] = lv
    return jnp.asarray(seg, BF16), jnp.asarray(lvl)


def _gla_kernel(xb_ref, wh_ref, w2_ref, gb_ref, ng_ref, seg_ref, lvl_ref, o_ref,
                p_ref, qs_ref, kv_ref, st_ref, dec_ref, oi_ref, *, seq):
    nc = seq // CH
    dk, dv = B_DK, B_DV
    p_ref[...] = _dot(xb_ref[0], wh_ref[0])

    c_q, c_k, c_v, c_r, c_g = 0, dk, 2 * dk, 2 * dk + dv, 2 * dk + 2 * dv
    gg = _group(nc, GLA_GROUP)
    lanes = [(u, d) for u in range(gg) for d in range(2)]

    def prep(it, carry):
        cs = [it * gg + u for u in range(gg)]
        rows = [pl.ds(pl.multiple_of(c * CH, CH), CH) for c in cs]
        q = [p_ref[r, c_q:c_q + dk] * (dk ** -0.5) for r in rows]
        k = [p_ref[r, c_k:c_k + dk] for r in rows]
        q16 = [x.astype(BF16) for x in q]
        k16 = [x.astype(BF16) for x in k]
        v16 = [p_ref[r, c_v:c_v + dv].astype(BF16) for r in rows]
        gin = [p_ref[r, c_g:c_g + dk].astype(BF16) for r in rows]
        logit = [_dot(gin[u], w2_ref[0, d]) + gb_ref[0, d][0:1, :] for u, d in lanes]
        la3 = [_split(-_softplus(-x) * (LOG2E / B_TAU), 3) for x in logit]
        la2 = [y[:, 0:GLA_PIECES * dk] for y in la3]
        bc = [_fold(_dot(seg_ref[d, 0], y), 3) for (u, d), y in zip(lanes, la3)]

        def level_sums(l):
            h = CH >> (l + 1)
            if h < HALO:
                return [_fold(_dot(seg_ref[d, 1 + l], y), GLA_PIECES) for (u, d), y in zip(lanes, la2)]
            out = []
            for (u, d), b in zip(lanes, bc):
                blocks = []
                for lo in range(0, CH, 2 * h):
                    if d == 1:
                        ref = b[lo + h:lo + h + 1, :]
                        blocks += [b[lo:lo + h, :] - ref, ref - b[lo + h:lo + 2 * h, :]]
                    else:
                        ref = b[lo + h - 1:lo + h, :]
                        blocks += [ref - b[lo:lo + h, :], b[lo + h:lo + 2 * h, :] - ref]
                out.append(jnp.concatenate(blocks, axis=0))
            return out

        half = CH // 2
        zero_half = jnp.zeros((half, dk), BF16)

        def top_level(x, e, d, late):
            upper = (d == 1) != late
            rows = slice(half, CH) if upper else slice(0, half)
            kept = (x[rows] * e[rows]).astype(BF16)
            return jnp.concatenate([zero_half, kept] if upper else [kept, zero_half], axis=0)

        ahead = [level_sums(l) for l in range(GLA_AHEAD)]
        scores = prod = None
        for l in range(N_LEVELS):
            if l + GLA_AHEAD < N_LEVELS:
                ahead.append(level_sums(l + GLA_AHEAD))
            e = [jnp.exp2(x) for x in ahead[l]]
            if l == 0:
                ql = [top_level(q[u], x, d, True) for (u, d), x in zip(lanes, e)]
                kl = [top_level(k[u], x, d, False) for (u, d), x in zip(lanes, e)]
            else:
                ql = [(q[u] * x).astype(BF16) for (u, d), x in zip(lanes, e)]
                kl = [(k[u] * x).astype(BF16) for (u, d), x in zip(lanes, e)]
            if l == 1:
                scores = prod
            elif l > 1:
                own = [lvl_ref[d] == l - 1 for d in range(2)]
                scores = [jnp.where(own[d], p, s) for (u, d), p, s in zip(lanes, prod, scores)]
            prod = [_dot_nt(a, b) for a, b in zip(ql, kl)]
        own = [lvl_ref[d] == N_LEVELS - 1 for d in range(2)]
        scores = [jnp.where(own[d], p, s) for (u, d), p, s in zip(lanes, prod, scores)]
        diag = [_dot_nt(q16[u], k16[u]) for u in range(gg)]
        own = [lvl_ref[d] == N_LEVELS for d in range(2)]
        scores = [jnp.where(own[d], diag[u], s) for (u, d), s in zip(lanes, scores)]
        for (u, d), s, b in zip(lanes, scores, bc):
            idx = d * nc + cs[u]
            btot = b[0:1, :] if d == 1 else b[CH - 1:CH, :]
            oi_ref[idx] = _dot(s.astype(BF16), v16[u])
            qs_ref[idx] = (q[u] * jnp.exp2(b)).astype(BF16)
            kv_ref[idx] = _dot_tn((k[u] * jnp.exp2(btot - b)).astype(BF16), v16[u])
            dec_ref[idx] = jnp.exp2(jnp.broadcast_to(btot, (CH, dk)).T)
        return carry

    lax.fori_loop(0, nc // gg, prep, 0)

    def scan(c, carry):
        sf, sb = carry
        cf = c
        ib = 2 * nc - 1 - c
        st_ref[cf] = sf.astype(BF16)
        st_ref[ib] = sb.astype(BF16)
        ef = dec_ref[cf]
        eb = dec_ref[ib]
        sf = sf * jnp.concatenate([ef, ef], axis=1) + kv_ref[cf]
        sb = sb * jnp.concatenate([eb, eb], axis=1) + kv_ref[ib]
        return sf, sb

    zero = jnp.zeros((dk, dv), F32)
    lax.fori_loop(0, nc, scan, (zero, zero))

    ng = ng_ref[...]
    fg = _group(nc, FIN_GROUP)

    def fin(it, carry):
        cs = [it * fg + u for u in range(fg)]
        rows = [pl.ds(pl.multiple_of(c * CH, CH), CH) for c in cs]
        os_ = [oi_ref[c] + oi_ref[nc + c] + _dot(qs_ref[c], st_ref[c]) + _dot(qs_ref[nc + c], st_ref[nc + c])
               for c in cs]
        inv = [lax.rsqrt(jnp.mean(o * o, axis=-1, keepdims=True) + RMS_EPS) for o in os_]
        for r, o, s in zip(rows, os_, inv):
            o_ref[0, r, :] = (o * s * ng * _silu(p_ref[r, c_r:c_r + dv])).astype(BF16)
        return carry

    lax.fori_loop(0, nc // fg, fin, 0)


def _gla_mixer(xb, w_in, gate_w2, gate_b, norm_g):
    bn, seq, dm = xb.shape
    h, dk, dv = B_HEADS, B_DK, B_DV
    nc = seq // CH
    kw, vw = h * dk, h * dv
    w = w_in

    def heads(cols, width):
        return cols.reshape(dm, h, width).transpose(1, 0, 2)

    gl = jnp.pad(w[:, 2 * kw + 2 * vw:], ((0, 0), (0, dk - 2 * B_RANK)))
    wh = jnp.concatenate([
        heads(w[:, 0:kw], dk), heads(w[:, kw:2 * kw], dk),
        heads(w[:, 2 * kw:2 * kw + vw], dv), heads(w[:, 2 * kw + vw:2 * kw + 2 * vw], dv),
        jnp.broadcast_to(gl[None], (h, dm, dk))], axis=2).astype(BF16)
    w2 = gate_w2.reshape(2, B_RANK, h, dk).transpose(2, 0, 1, 3)
    w2p = jnp.zeros((h, 2, dk, dk), F32)
    w2p = w2p.at[:, 0, 0:B_RANK].set(w2[:, 0]).at[:, 1, B_RANK:2 * B_RANK].set(w2[:, 1]).astype(BF16)
    gb = gate_b.reshape(2, h, dk).transpose(1, 0, 2).astype(F32)
    gb = jnp.broadcast_to(gb[:, :, None, :], (h, 2, 8, dk))
    ng = norm_g.astype(F32).reshape(1, dv)
    seg, lvl = _gla_tables()
    nw = wh.shape[2]

    kern = functools.partial(_gla_kernel, seq=seq)
    return pl.pallas_call(
        kern,
        out_shape=jax.ShapeDtypeStruct((bn, seq, vw), BF16),
        grid=(bn, h),
        in_specs=[
            pl.BlockSpec((1, seq, dm), lambda b, i: (b, 0, 0)),
            pl.BlockSpec((1, dm, nw), lambda b, i: (i, 0, 0)),
            pl.BlockSpec((1, 2, dk, dk), lambda b, i: (i, 0, 0, 0)),
            pl.BlockSpec((1, 2, 8, dk), lambda b, i: (i, 0, 0, 0)),
            pl.BlockSpec((1, dv), lambda b, i: (0, 0)),
            pl.BlockSpec(seg.shape, lambda b, i: (0, 0, 0, 0)),
            pl.BlockSpec(lvl.shape, lambda b, i: (0, 0, 0)),
        ],
        out_specs=pl.BlockSpec((1, seq, dv), lambda b, i: (b, 0, i)),
        scratch_shapes=[
            pltpu.VMEM((seq, nw), F32),
            pltpu.VMEM((2 * nc, CH, dk), BF16),
            pltpu.VMEM((2 * nc, dk, dv), F32),
            pltpu.VMEM((2 * nc, dk, dv), BF16),
            pltpu.VMEM((2 * nc, dk, dk), F32),
            pltpu.VMEM((2 * nc, CH, dv), F32),
        ],
        compiler_params=pltpu.CompilerParams(
            dimension_semantics=("arbitrary", "arbitrary"), vmem_limit_bytes=VMEM_LIMIT),
        name="gla_mixer",
    )(xb, wh, w2p, gb, ng, seg, lvl)


def _post_kernel(o_ref, x_ref, wo_ref, w1_ref, w2_ref, ln_ref, y_ref, yb_ref, *, alpha):
    ln = ln_ref[...]
    x = x_ref[...]
    x1 = _layernorm(alpha * x + _dot(o_ref[...], wo_ref[...]), ln[0:1, :], ln[1:2, :])
    x1b = x1.astype(BF16)
    acc = jnp.zeros(x.shape, F32)
    dff = w1_ref.shape[1]
    for j in range(dff // FF_TILE):
        cols = slice(j * FF_TILE, (j + 1) * FF_TILE)
        hcur = jnp.maximum(_dot(x1b, w1_ref[:, cols]), 0.0)
        acc = acc + _dot((hcur * hcur).astype(BF16), w2_ref[cols, :])
    y = _layernorm(alpha * x1 + acc, ln[2:3, :], ln[3:4, :])
    y_ref[...] = y
    yb_ref[...] = y.astype(BF16)


def _post(o, x, w_out, w1, w2, g1, b1, g2, b2, alpha):
    t, dm = x.shape
    vw = o.shape[1]
    dff = w1.shape[1]
    tm = min(ROW_TILE, t)
    ln = jnp.pad(jnp.stack([g1, b1, g2, b2]).astype(F32), ((0, 4), (0, 0)))
    const = lambda shape: pl.BlockSpec(shape, lambda i: (0, 0), pipeline_mode=pl.Buffered(1))
    return pl.pallas_call(
        functools.partial(_post_kernel, alpha=alpha),
        out_shape=(jax.ShapeDtypeStruct((t, dm), F32), jax.ShapeDtypeStruct((t, dm), BF16)),
        grid=(t // tm,),
        in_specs=[
            pl.BlockSpec((tm, vw), lambda i: (i, 0)),
            pl.BlockSpec((tm, dm), lambda i: (i, 0)),
            const((vw, dm)), const((dm, dff)), const((dff, dm)), const((8, dm)),
        ],
        out_specs=(pl.BlockSpec((tm, dm), lambda i: (i, 0)), pl.BlockSpec((tm, dm), lambda i: (i, 0))),
        compiler_params=pltpu.CompilerParams(
            dimension_semantics=("arbitrary",), vmem_limit_bytes=VMEM_LIMIT),
        name="post",
    )(o, x, w_out.astype(BF16), w1.astype(BF16), w2.astype(BF16), ln)


def kernel(x, a_w_in, a_conv, a_alog, a_dt_bias, a_norm_g, a_w_out, b_w_in, b_gate_w2, b_gate_b,
           b_norm_g, b_w_out, ln1_g, ln1_b, mlp_w1, mlp_w2, ln2_g, ln2_b):
    bn, seq, dm = x.shape
    depth = ln1_g.shape[0]
    alpha = (2 * depth) ** 0.25
    xf = x.astype(F32).reshape(bn * seq, dm)
    xb = xf.astype(BF16)
    for i in range(depth):
        j = i // 2
        xb3 = xb.reshape(bn, seq, dm)
        if i % 2 == 0:
            o = _gdn_mixer(xb3, a_w_in[j], a_conv[j], a_alog[j], a_dt_bias[j], a_norm_g[j])
            w_out = a_w_out[j]
        else:
            o = _gla_mixer(xb3, b_w_in[j], b_gate_w2[j], b_gate_b[j], b_norm_g[j])
            w_out = b_w_out[j]
        xf, xb = _post(o.reshape(bn * seq, -1), xf, w_out, mlp_w1[i], mlp_w2[i],
                       ln1_g[i], ln1_b[i], ln2_g[i], ln2_b[i], alpha)
    return xf.reshape(bn, seq, dm).astype(x.dtype)
```

```python
import functools
import math

import numpy as np

import jax
import jax.numpy as jnp
from jax import lax
from jax.experimental import pallas as pl
from jax.experimental.pallas import tpu as pltpu

F32 = jnp.float32
BF16 = jnp.bfloat16

A_HEADS, A_DK, A_DV, A_CONV = 8, 128, 128, 5
B_HEADS, B_DK, B_DV, B_RANK, B_TAU = 4, 128, 256, 16, 16.0
LN_EPS, RMS_EPS, L2_EPS = 1e-5, 1e-6, 1e-6

CH = 128
N_LEVELS = 7
HALO = 8
CONV_MXU_TAPS = (0, 4)
GATE_ROWS = 16
SOLVE_LEVELS_PER_STAGE = 1
GLA_GROUP = 4
GLA_AHEAD = 2
GLA_PIECES = 2
FIN_GROUP = 4
NEG_BIG = -1e30
LOG2E = math.log2(math.e)
VMEM_LIMIT = 56 * 1024 * 1024
ROW_TILE = 1024
FF_TILE = 1024

assert CH == A_DK == B_DK and 2 ** N_LEVELS == CH


def _dot(a, b):
    return jnp.dot(a, b, preferred_element_type=F32)


def _dot_nt(a, b):
    return lax.dot_general(a, b, (((1,), (1,)), ((), ())), preferred_element_type=F32)


def _dot_tn(a, b):
    return lax.dot_general(a, b, (((0,), (0,)), ((), ())), preferred_element_type=F32)


def _split(x, n, axis=1):
    pieces = []
    for _ in range(n - 1):
        p = x.astype(BF16)
        pieces.append(p)
        x = x - p.astype(F32)
    pieces.append(x.astype(BF16))
    return jnp.concatenate(pieces, axis=axis)


def _fold(y, n, axis=1):
    w = y.shape[axis] // n
    blocks = [lax.slice_in_dim(y, i * w, (i + 1) * w, axis=axis) for i in range(n)]
    out = blocks[0]
    for b in blocks[1:]:
        out = out + b
    return out


def _sigmoid(x):
    return 0.5 + 0.5 * jnp.tanh(0.5 * x)


def _silu(x):
    h = 0.5 * x
    return h + h * jnp.tanh(h)


def _softplus(x):
    return jnp.maximum(x, 0.0) + jnp.log(1.0 + jnp.exp(-jnp.abs(x)))


def _layernorm(y, g, b):
    mu = jnp.mean(y, axis=-1, keepdims=True)
    yc = y - mu
    var = jnp.mean(yc * yc, axis=-1, keepdims=True)
    return yc * lax.rsqrt(var + LN_EPS) * g + b


def _order_masks(rev):
    row = lax.broadcasted_iota(jnp.int32, (CH, CH), 0)
    col = lax.broadcasted_iota(jnp.int32, (CH, CH), 1)
    if rev:
        return col >= row, col > row
    return col <= row, col < row


def _group(n, want):
    return math.gcd(n, want)


def _gdn_kernel(xb_ref, wh_ref, cw_ref, hp_ref, ng_ref, lm_ref, sh_ref, o_ref,
                p_ref, qkv_ref, gate_ref, a_ref, t_ref, qk_ref, rhs_ref, qd_ref, kd_ref, gl_ref,
                mc_ref, qc_ref, rc_ref, oc_ref, *, seq):
    nc = seq // CH
    spare = 2 * nc

    p_ref[0:HALO, :] = jnp.zeros((HALO, p_ref.shape[1]), F32)
    p_ref[HALO + seq:, :] = jnp.zeros((HALO, p_ref.shape[1]), F32)
    p_ref[HALO:HALO + seq, :] = _dot(xb_ref[0], wh_ref[0])
    for ref in (a_ref, t_ref, qk_ref, kd_ref, rhs_ref, qd_ref):
        ref[spare] = jnp.zeros(ref.shape[1:], ref.dtype)
    qkv_ref[nc] = jnp.zeros(qkv_ref.shape[1:], F32)
    gate_ref[nc] = jnp.zeros(gate_ref.shape[1:], F32)

    cw = cw_ref[0]
    hp = hp_ref[0]

    def tiles(it, lag):
        c = it - lag
        ok = jnp.logical_and(c >= 0, c < nc)
        return jnp.where(ok, c, spare), jnp.where(ok, nc + c, spare)

    def prep_one(c):
        base = pl.multiple_of(c * CH, CH)
        win = p_ref[pl.ds(base, CH + 2 * HALO), 0:3 * A_DK]
        gates = p_ref[pl.ds(base + HALO, CH), 4 * A_DK:5 * A_DK].T[0:GATE_ROWS, :]
        if CONV_MXU_TAPS:
            shifted = _dot(sh_ref[...], win.astype(BF16))
        mid = A_CONV // 2
        acc = win[HALO:HALO + CH, :] * cw[mid:mid + 1, :]
        for i in range(A_CONV):
            if i != mid and i not in CONV_MXU_TAPS:
                off = HALO + i - mid
                acc = acc + win[off:off + CH, :] * cw[i:i + 1, :]
        g_rows = -hp[0] * _softplus(gates + hp[1])
        pieces = _split(g_rows, 3, axis=0)
        gcum = [_fold(_dot(pieces, _order_masks(not rev)[0].astype(BF16)), 3, axis=0) for rev in (False, True)]
        yield
        for n, i in enumerate(CONV_MXU_TAPS):
            acc = acc + shifted[n * CH:(n + 1) * CH, :] * cw[i:i + 1, :]
        yield
        s = _silu(acc)
        q = s[:, 0:A_DK]
        k = s[:, A_DK:2 * A_DK]
        q = q * (lax.rsqrt(jnp.sum(q * q, axis=-1, keepdims=True) + L2_EPS) * (A_DK ** -0.5))
        k = k * lax.rsqrt(jnp.sum(k * k, axis=-1, keepdims=True) + L2_EPS)
        yield
        qkv_ref[c] = jnp.concatenate([q, k, s[:, 2 * A_DK:]], axis=1)
        gate_ref[c] = jnp.concatenate([_sigmoid(gates), gcum[0], gcum[1]], axis=0)
        yield

    def prep_two(jf, jb, qkv, gate):
        q = qkv[:, 0:A_DK]
        k = qkv[:, A_DK:2 * A_DK]
        v = qkv[:, 2 * A_DK:]
        dirs = ((0, False, jf), (1, True, jb))
        gr = [jnp.broadcast_to(gate[(1 + d) * GATE_ROWS + 2 + d:(1 + d) * GATE_ROWS + 3 + d, :], (CH, CH))
              for d, _, _ in dirs]
        gc = [x.T for x in gr]
        beta = [jnp.broadcast_to(gate[d:d + 1, :], (CH, CH)).T for d, _, _ in dirs]
        kb16 = k.astype(BF16)
        eye = (lax.broadcasted_iota(jnp.int32, (CH, CH), 0)
               == lax.broadcasted_iota(jnp.int32, (CH, CH), 1)).astype(F32)
        yield
        kbeta = [k * beta[d] for d, _, _ in dirs]
        kq = [_dot_nt(jnp.concatenate([kbeta[d], q], axis=0).astype(BF16), kb16) for d, _, _ in dirs]
        gtot = [gc[d][0:1, :] if rev else gc[d][CH - 1:CH, :] for d, rev, _ in dirs]
        eg = [jnp.exp2(gc[d]) for d, _, _ in dirs]
        dmat = [jnp.exp2(jnp.where(_order_masks(rev)[0], gc[d] - gr[d], NEG_BIG)) for d, rev, _ in dirs]
        yield
        for d, rev, idx in dirs:
            rhs_ref[idx] = jnp.concatenate([v * beta[d], kbeta[d] * eg[d]], axis=1).astype(BF16)
            qd_ref[idx] = q * eg[d]
            kd_ref[idx] = (k * jnp.exp2(gtot[d] - gc[d])).astype(BF16)
            gl_ref[idx] = jnp.broadcast_to(jnp.exp2(gtot[d]), (8, A_DK))
        yield
        for d, rev, idx in dirs:
            a = jnp.where(_order_masks(rev)[1], kq[d][0:CH] * dmat[d], 0.0).astype(BF16)
            a_ref[idx] = a
            t_ref[idx] = (eye - (a * lm_ref[0]).astype(F32)).astype(BF16)
            qk_ref[idx] = (kq[d][CH:] * dmat[d]).astype(BF16)
        yield

    per_stage = SOLVE_LEVELS_PER_STAGE
    n_stages = (N_LEVELS - 1) // per_stage
    solve_stages = tuple((2 + s, tuple(range(1 + s * per_stage, 1 + (s + 1) * per_stage)))
                         for s in range(n_stages))
    ops_lag = n_stages + 2

    def step(it, stage_one, stage_two):
        fillers = []
        if stage_two:
            c2 = it - 1
            ok = jnp.logical_and(c2 >= 0, c2 < nc)
            cq = jnp.where(ok, c2, nc)
            fillers.append(prep_two(*tiles(it, 1), qkv_ref[cq], gate_ref[cq]))
        loaded = []
        for lag, lvs in solve_stages:
            jf, jb = tiles(it, lag)
            loaded.append((jf, jb, t_ref[jf], t_ref[jb], a_ref[jf], a_ref[jb]))
        ops_in = [(i, t_ref[i], rhs_ref[i], kd_ref[i], qk_ref[i], qd_ref[i]) for i in tiles(it, ops_lag)]
        if stage_one:
            fillers.insert(0, prep_one(it))

        def fill():
            for f in fillers:
                next(f, None)

        uws = [_dot(t, rhs) for i, t, rhs, kd, qkm, qd in ops_in]
        for half in range(per_stage):
            xs = []
            for (lag, lvs), (jf, jb, tf, tb, af, ab) in zip(solve_stages, loaded):
                m = lm_ref[lvs[half]]
                xs.append((_dot(tf, af * m), _dot(tb, ab * m)))
            fill()
            fill()
            ys = [jnp.concatenate([_dot(xf.astype(BF16), tf), _dot(xb.astype(BF16), tb)], axis=1)
                  for (xf, xb), (jf, jb, tf, tb, af, ab) in zip(xs, loaded)]
            if half == 0:
                uws = [uw.astype(BF16) for uw in uws]
                kuws = [_dot_tn(kd, uw) for uw, (i, t, rhs, kd, qkm, qd) in zip(uws, ops_in)]
                quws = [_dot(qkm, uw) for uw, (i, t, rhs, kd, qkm, qd) in zip(uws, ops_in)]
            fill()
            fill()
            loaded = [(jf, jb, tf - y[:, 0:CH].astype(BF16), tb - y[:, CH:].astype(BF16), af, ab)
                      for y, (jf, jb, tf, tb, af, ab) in zip(ys, loaded)]
        for f in fillers:
            for _ in f:
                pass
        for jf, jb, tf, tb, _, _ in loaded:
            t_ref[jf] = tf
            t_ref[jb] = tb
        for kuw, quw, (i, t, rhs, kd, qkm, qd) in zip(kuws, quws, ops_in):
            qc_ref[i] = kuw[:, 0:A_DV]
            mc_ref[i] = kuw[:, A_DV:].astype(BF16)
            oc_ref[i] = quw[:, 0:A_DV]
            rc_ref[i] = (qd - quw[:, A_DV:]).astype(BF16)

    def loop(lo, hi, stage_one, stage_two):
        def body(it, carry):
            step(it, stage_one, stage_two)
            return carry
        lax.fori_loop(lo, hi, body, 0)

    loop(0, nc, True, True)
    loop(nc, nc + 1, False, True)
    loop(nc + 1, nc + ops_lag, False, False)

    def scan(c, carry):
        sf, sb = carry
        cf = c
        cb = 2 * nc - 1 - c
        of = _dot(rc_ref[cf], sf.astype(BF16)) + oc_ref[cf]
        ob = _dot(rc_ref[cb], sb.astype(BF16)) + oc_ref[cb]
        oc_ref[cf] = of
        oc_ref[cb] = ob
        sf = gl_ref[cf][0:1, :] * sf - _dot(mc_ref[cf], sf.astype(BF16)) + qc_ref[cf]
        sb = gl_ref[cb][0:1, :] * sb - _dot(mc_ref[cb], sb.astype(BF16)) + qc_ref[cb]
        return sf, sb

    zero = jnp.zeros((A_DK, A_DV), F32)
    lax.fori_loop(0, nc, scan, (zero, zero))

    ng = ng_ref[...]
    fg = _group(nc, FIN_GROUP)

    def fin(it, carry):
        cs = [it * fg + u for u in range(fg)]
        rows = [pl.ds(pl.multiple_of(c * CH, CH), CH) for c in cs]
        os_ = [oc_ref[c] + oc_ref[nc + c] for c in cs]
        inv = [lax.rsqrt(jnp.mean(o * o, axis=-1, keepdims=True) + RMS_EPS) for o in os_]
        for c, r, o, s in zip(cs, rows, os_, inv):
            z = p_ref[pl.ds(pl.multiple_of(c * CH, CH) + HALO, CH), 3 * A_DK:4 * A_DK]
            o_ref[0, r, :] = (o * s * ng * _silu(z)).astype(BF16)
        return carry

    lax.fori_loop(0, nc // fg, fin, 0)


def _gdn_level_masks():
    idx = np.arange(CH)
    x = idx[:, None] ^ idx[None, :]
    return jnp.asarray(np.stack([(x >> lv) == 1 for lv in range(N_LEVELS)]), BF16)


def _conv_shift_matrices():
    t = np.arange(CH)[:, None]
    r = np.arange(CH + 2 * HALO)[None, :]
    taps = CONV_MXU_TAPS or (0,)
    return jnp.asarray(np.concatenate([r == t + HALO + i - A_CONV // 2 for i in taps], axis=0), BF16)


def _gdn_mixer(xb, w_in, conv_w, a_log, dt_bias, norm_g):
    bn, seq, dm = xb.shape
    h, dk = A_HEADS, A_DK
    nc = seq // CH
    w = w_in
    hw = h * dk
    ba = w[:, 4 * hw:].reshape(dm, 2, 2, h)
    per_head = [w[:, i * hw:(i + 1) * hw].reshape(dm, h, dk).transpose(1, 0, 2) for i in range(4)]
    gate_cols = jnp.pad(ba.reshape(dm, 4, h).transpose(2, 0, 1), ((0, 0), (0, 0), (0, dk - 4)))
    wh = jnp.concatenate(per_head + [gate_cols], axis=2).astype(BF16)
    cw = conv_w.reshape(A_CONV, 3, h, dk).transpose(2, 0, 1, 3).reshape(h, A_CONV, 3 * dk)
    cw = jnp.pad(cw, ((0, 0), (0, 8 - A_CONV), (0, 0))).astype(F32)
    scale = jnp.zeros((h, GATE_ROWS), F32).at[:, 2:4].set((jnp.exp(a_log.astype(F32)) * LOG2E).T)
    bias = jnp.zeros((h, GATE_ROWS), F32).at[:, 2:4].set(dt_bias.astype(F32).T)
    hp = jnp.broadcast_to(jnp.stack([scale, bias], axis=1)[:, :, :, None], (h, 2, GATE_ROWS, dk))
    ng = norm_g.astype(F32).reshape(1, A_DV)
    lm = _gdn_level_masks()
    sh = _conv_shift_matrices()
    nw = wh.shape[2]

    kern = functools.partial(_gdn_kernel, seq=seq)
    tile = lambda dt: pltpu.VMEM((2 * nc + 1, CH, CH), dt)
    return pl.pallas_call(
        kern,
        out_shape=jax.ShapeDtypeStruct((bn, seq, h * A_DV), BF16),
        grid=(bn, h),
        in_specs=[
            pl.BlockSpec((1, seq, dm), lambda b, i: (b, 0, 0)),
            pl.BlockSpec((1, dm, nw), lambda b, i: (i, 0, 0)),
            pl.BlockSpec((1, 8, 3 * dk), lambda b, i: (i, 0, 0)),
            pl.BlockSpec((1, 2, GATE_ROWS, dk), lambda b, i: (i, 0, 0, 0)),
            pl.BlockSpec((1, A_DV), lambda b, i: (0, 0)),
            pl.BlockSpec(lm.shape, lambda b, i: (0, 0, 0)),
            pl.BlockSpec(sh.shape, lambda b, i: (0, 0)),
        ],
        out_specs=pl.BlockSpec((1, seq, A_DV), lambda b, i: (b, 0, i)),
        scratch_shapes=[
            pltpu.VMEM((seq + 2 * HALO, nw), F32),
            pltpu.VMEM((nc + 1, CH, 3 * A_DK), F32),
            pltpu.VMEM((nc + 1, 3 * GATE_ROWS, CH), F32),
            tile(BF16),
            tile(BF16),
            tile(BF16),
            pltpu.VMEM((2 * nc + 1, CH, A_DV + A_DK), BF16),
            tile(F32),
            tile(BF16),
            pltpu.VMEM((2 * nc + 1, 8, A_DK), F32),
            tile(BF16),
            tile(F32),
            tile(BF16),
            tile(F32),
        ],
        compiler_params=pltpu.CompilerParams(
            dimension_semantics=("arbitrary", "arbitrary"), vmem_limit_bytes=VMEM_LIMIT),
        name="gdn_mixer",
    )(xb, wh, cw, hp, ng, lm, sh)


def _gla_tables():
    i = np.arange(CH)[:, None]
    t = np.arange(CH)[None, :]
    seg = np.zeros((2, N_LEVELS + 1, CH, CH), np.float32)
    lvl = np.zeros((2, CH, CH), np.int32)
    for d in range(2):
        rev = d == 1
        seg[d, 0] = (t >= i) if rev else (t <= i)
        lv = np.full((CH, CH), N_LEVELS + 1, np.int32)
        lv[np.arange(CH), np.arange(CH)] = N_LEVELS
        x = i ^ t
        for l in range(N_LEVELS):
            h = 2 ** (N_LEVELS - 1 - l)
            b0 = (i // (2 * h)) * (2 * h)
            if rev:
                r = b0 + h
                late = i < r
                m = np.where(late, (t >= i) & (t < r), (t >= r) & (t < i))
                own = ((x >> (N_LEVELS - 1 - l)) == 1) & (t > i)
            else:
                r = b0 + h - 1
                late = i > r
                m = np.where(late, (t > r) & (t <= i), (t > i) & (t <= r))
                own = ((x >> (N_LEVELS - 1 - l)) == 1) & (t < i)
            seg[d, 1 + l] = m
            lv[own] = l
        lvl[d] = lv
    return jnp.asarray(seg, BF16), jnp.asarray(lvl)


def _gla_kernel(xb_ref, wh_ref, w2_ref, gb_ref, ng_ref, seg_ref, lvl_ref, o_ref,
                p_ref, qs_ref, kv_ref, st_ref, dec_ref, oi_ref, *, seq):
    nc = seq // CH
    dk, dv = B_DK, B_DV
    p_ref[...] = _dot(xb_ref[0], wh_ref[0])

    c_q, c_k, c_v, c_r, c_g = 0, dk, 2 * dk, 2 * dk + dv, 2 * dk + 2 * dv
    gg = _group(nc, GLA_GROUP)
    lanes = [(u, d) for u in range(gg) for d in range(2)]

    def prep(it, carry):
        cs = [it * gg + u for u in range(gg)]
        rows = [pl.ds(pl.multiple_of(c * CH, CH), CH) for c in cs]
        q = [p_ref[r, c_q:c_q + dk] * (dk ** -0.5) for r in rows]
        k = [p_ref[r, c_k:c_k + dk] for r in rows]
        q16 = [x.astype(BF16) for x in q]
        k16 = [x.astype(BF16) for x in k]
        v16 = [p_ref[r, c_v:c_v + dv].astype(BF16) for r in rows]
        gin = [p_ref[r, c_g:c_g + dk].astype(BF16) for r in rows]
        logit = [_dot(gin[u], w2_ref[0, d]) + gb_ref[0, d][0:1, :] for u, d in lanes]
        la3 = [_split(-_softplus(-x) * (LOG2E / B_TAU), 3) for x in logit]
        la2 = [y[:, 0:GLA_PIECES * dk] for y in la3]
        bc = [_fold(_dot(seg_ref[d, 0], y), 3) for (u, d), y in zip(lanes, la3)]

        def level_sums(l):
            h = CH >> (l + 1)
            if h < HALO:
                return [_fold(_dot(seg_ref[d, 1 + l], y), GLA_PIECES) for (u, d), y in zip(lanes, la2)]
            out = []
            for (u, d), b in zip(lanes, bc):
                blocks = []
                for lo in range(0, CH, 2 * h):
                    if d == 1:
                        ref = b[lo + h:lo + h + 1, :]
                        blocks += [b[lo:lo + h, :] - ref, ref - b[lo + h:lo + 2 * h, :]]
                    else:
                        ref = b[lo + h - 1:lo + h, :]
                        blocks += [ref - b[lo:lo + h, :], b[lo + h:lo + 2 * h, :] - ref]
                out.append(jnp.concatenate(blocks, axis=0))
            return out

        half = CH // 2
        zero_half = jnp.zeros((half, dk), BF16)

        def top_level(x, e, d, late):
            upper = (d == 1) != late
            rows = slice(half, CH) if upper else slice(0, half)
            kept = (x[rows] * e[rows]).astype(BF16)
            return jnp.concatenate([zero_half, kept] if upper else [kept, zero_half], axis=0)

        ahead = [level_sums(l) for l in range(GLA_AHEAD)]
        scores = prod = None
        for l in range(N_LEVELS):
            if l + GLA_AHEAD < N_LEVELS:
                ahead.append(level_sums(l + GLA_AHEAD))
            e = [jnp.exp2(x) for x in ahead[l]]
            if l == 0:
                ql = [top_level(q[u], x, d, True) for (u, d), x in zip(lanes, e)]
                kl = [top_level(k[u], x, d, False) for (u, d), x in zip(lanes, e)]
            else:
                ql = [(q[u] * x).astype(BF16) for (u, d), x in zip(lanes, e)]
                kl = [(k[u] * x).astype(BF16) for (u, d), x in zip(lanes, e)]
            if l == 1:
                scores = prod
            elif l > 1:
                own = [lvl_ref[d] == l - 1 for d in range(2)]
                scores = [jnp.where(own[d], p, s) for (u, d), p, s in zip(lanes, prod, scores)]
            prod = [_dot_nt(a, b) for a, b in zip(ql, kl)]
        own = [lvl_ref[d] == N_LEVELS - 1 for d in range(2)]
        scores = [jnp.where(own[d], p, s) for (u, d), p, s in zip(lanes, prod, scores)]
        diag = [_dot_nt(q16[u], k16[u]) for u in range(gg)]
        own = [lvl_ref[d] == N_LEVELS for d in range(2)]
        scores = [jnp.where(own[d], diag[u], s) for (u, d), s in zip(lanes, scores)]
        for (u, d), s, b in zip(lanes, scores, bc):
            idx = d * nc + cs[u]
            btot = b[0:1, :] if d == 1 else b[CH - 1:CH, :]
            oi_ref[idx] = _dot(s.astype(BF16), v16[u])
            qs_ref[idx] = (q[u] * jnp.exp2(b)).astype(BF16)
            kv_ref[idx] = _dot_tn((k[u] * jnp.exp2(btot - b)).astype(BF16), v16[u])
            dec_ref[idx] = jnp.exp2(jnp.broadcast_to(btot, (CH, dk)).T)
        return carry

    lax.fori_loop(0, nc // gg, prep, 0)

    def scan(c, carry):
        sf, sb = carry
        cf = c
        ib = 2 * nc - 1 - c
        st_ref[cf] = sf.astype(BF16)
        st_ref[ib] = sb.astype(BF16)
        ef = dec_ref[cf]
        eb = dec_ref[ib]
        sf = sf * jnp.concatenate([ef, ef], axis=1) + kv_ref[cf]
        sb = sb * jnp.concatenate([eb, eb], axis=1) + kv_ref[ib]
        return sf, sb

    zero = jnp.zeros((dk, dv), F32)
    lax.fori_loop(0, nc, scan, (zero, zero))

    ng = ng_ref[...]
    fg = _group(nc, FIN_GROUP)

    def fin(it, carry):
        cs = [it * fg + u for u in range(fg)]
        rows = [pl.ds(pl.multiple_of(c * CH, CH), CH) for c in cs]
        os_ = [oi_ref[c] + oi_ref[nc + c] + _dot(qs_ref[c], st_ref[c]) + _dot(qs_ref[nc + c], st_ref[nc + c])
               for c in cs]
        inv = [lax.rsqrt(jnp.mean(o * o, axis=-1, keepdims=True) + RMS_EPS) for o in os_]
        for r, o, s in zip(rows, os_, inv):
            o_ref[0, r, :] = (o * s * ng * _silu(p_ref[r, c_r:c_r + dv])).astype(BF16)
        return carry

    lax.fori_loop(0, nc // fg, fin, 0)


def _gla_mixer(xb, w_in, gate_w2, gate_b, norm_g):
    bn, seq, dm = xb.shape
    h, dk, dv = B_HEADS, B_DK, B_DV
    nc = seq // CH
    kw, vw = h * dk, h * dv
    w = w_in

    def heads(cols, width):
        return cols.reshape(dm, h, width).transpose(1, 0, 2)

    gl = jnp.pad(w[:, 2 * kw + 2 * vw:], ((0, 0), (0, dk - 2 * B_RANK)))
    wh = jnp.concatenate([
        heads(w[:, 0:kw], dk), heads(w[:, kw:2 * kw], dk),
        heads(w[:, 2 * kw:2 * kw + vw], dv), heads(w[:, 2 * kw + vw:2 * kw + 2 * vw], dv),
        jnp.broadcast_to(gl[None], (h, dm, dk))], axis=2).astype(BF16)
    w2 = gate_w2.reshape(2, B_RANK, h, dk).transpose(2, 0, 1, 3)
    w2p = jnp.zeros((h, 2, dk, dk), F32)
    w2p = w2p.at[:, 0, 0:B_RANK].set(w2[:, 0]).at[:, 1, B_RANK:2 * B_RANK].set(w2[:, 1]).astype(BF16)
    gb = gate_b.reshape(2, h, dk).transpose(1, 0, 2).astype(F32)
    gb = jnp.broadcast_to(gb[:, :, None, :], (h, 2, 8, dk))
    ng = norm_g.astype(F32).reshape(1, dv)
    seg, lvl = _gla_tables()
    nw = wh.shape[2]

    kern = functools.partial(_gla_kernel, seq=seq)
    return pl.pallas_call(
        kern,
        out_shape=jax.ShapeDtypeStruct((bn, seq, vw), BF16),
        grid=(bn, h),
        in_specs=[
            pl.BlockSpec((1, seq, dm), lambda b, i: (b, 0, 0)),
            pl.BlockSpec((1, dm, nw), lambda b, i: (i, 0, 0)),
            pl.BlockSpec((1, 2, dk, dk), lambda b, i: (i, 0, 0, 0)),
            pl.BlockSpec((1, 2, 8, dk), lambda b, i: (i, 0, 0, 0)),
            pl.BlockSpec((1, dv), lambda b, i: (0, 0)),
            pl.BlockSpec(seg.shape, lambda b, i: (0, 0, 0, 0)),
            pl.BlockSpec(lvl.shape, lambda b, i: (0, 0, 0)),
        ],
        out_specs=pl.BlockSpec((1, seq, dv), lambda b, i: (b, 0, i)),
        scratch_shapes=[
            pltpu.VMEM((seq, nw), F32),
            pltpu.VMEM((2 * nc, CH, dk), BF16),
            pltpu.VMEM((2 * nc, dk, dv), F32),
            pltpu.VMEM((2 * nc, dk, dv), BF16),
            pltpu.VMEM((2 * nc, dk, dk), F32),
            pltpu.VMEM((2 * nc, CH, dv), F32),
        ],
        compiler_params=pltpu.CompilerParams(
            dimension_semantics=("arbitrary", "arbitrary"), vmem_limit_bytes=VMEM_LIMIT),
        name="gla_mixer",
    )(xb, wh, w2p, gb, ng, seg, lvl)


def _post_kernel(o_ref, x_ref, wo_ref, w1_ref, w2_ref, ln_ref, y_ref, yb_ref, *, alpha):
    ln = ln_ref[...]
    x = x_ref[...]
    x1 = _layernorm(alpha * x + _dot(o_ref[...], wo_ref[...]), ln[0:1, :], ln[1:2, :])
    x1b = x1.astype(BF16)
    acc = jnp.zeros(x.shape, F32)
    dff = w1_ref.shape[1]
    for j in range(dff // FF_TILE):
        cols = slice(j * FF_TILE, (j + 1) * FF_TILE)
        hcur = jnp.maximum(_dot(x1b, w1_ref[:, cols]), 0.0)
        acc = acc + _dot((hcur * hcur).astype(BF16), w2_ref[cols, :])
    y = _layernorm(alpha * x1 + acc, ln[2:3, :], ln[3:4, :])
    y_ref[...] = y
    yb_ref[...] = y.astype(BF16)


def _post(o, x, w_out, w1, w2, g1, b1, g2, b2, alpha):
    t, dm = x.shape
    vw = o.shape[1]
    dff = w1.shape[1]
    tm = min(ROW_TILE, t)
    ln = jnp.pad(jnp.stack([g1, b1, g2, b2]).astype(F32), ((0, 4), (0, 0)))
    const = lambda shape: pl.BlockSpec(shape, lambda i: (0, 0), pipeline_mode=pl.Buffered(1))
    return pl.pallas_call(
        functools.partial(_post_kernel, alpha=alpha),
        out_shape=(jax.ShapeDtypeStruct((t, dm), F32), jax.ShapeDtypeStruct((t, dm), BF16)),
        grid=(t // tm,),
        in_specs=[
            pl.BlockSpec((tm, vw), lambda i: (i, 0)),
            pl.BlockSpec((tm, dm), lambda i: (i, 0)),
            const((vw, dm)), const((dm, dff)), const((dff, dm)), const((8, dm)),
        ],
        out_specs=(pl.BlockSpec((tm, dm), lambda i: (i, 0)), pl.BlockSpec((tm, dm), lambda i: (i, 0))),
        compiler_params=pltpu.CompilerParams(
            dimension_semantics=("arbitrary",), vmem_limit_bytes=VMEM_LIMIT),
        name="post",
    )(o, x, w_out.astype(BF16), w1.astype(BF16), w2.astype(BF16), ln)


def kernel(x, a_w_in, a_conv, a_alog, a_dt_bias, a_norm_g, a_w_out, b_w_in, b_gate_w2, b_gate_b,
           b_norm_g, b_w_out, ln1_g, ln1_b, mlp_w1, mlp_w2, ln2_g, ln2_b):
    bn, seq, dm = x.shape
    depth = ln1_g.shape[0]
    alpha = (2 * depth) ** 0.25
    xf = x.astype(F32).reshape(bn * seq, dm)
    xb = xf.astype(BF16)
    for i in range(depth):
        j = i // 2
        xb3 = xb.reshape(bn, seq, dm)
        if i % 2 == 0:
            o = _gdn_mixer(xb3, a_w_in[j], a_conv[j], a_alog[j], a_dt_bias[j], a_norm_g[j])
            w_out = a_w_out[j]
        else:
            o = _gla_mixer(xb3, b_w_in[j], b_gate_w2[j], b_gate_b[j], b_norm_g[j])
            w_out = b_w_out[j]
        xf, xb = _post(o.reshape(bn * seq, -1), xf, w_out, mlp_w1[i], mlp_w2[i],
                       ln1_g[i], ln1_b[i], ln2_g[i], ln2_b[i], alpha)
    return xf.reshape(bn, seq, dm).astype(x.dtype)
```

```python
import functools
import math

import numpy as np

import jax
import jax.numpy as jnp
from jax import lax
from jax.experimental import pallas as pl
from jax.experimental.pallas import tpu as pltpu

F32 = jnp.float32
BF16 = jnp.bfloat16

A_HEADS, A_DK, A_DV, A_CONV = 8, 128, 128, 5
B_HEADS, B_DK, B_DV, B_RANK, B_TAU = 4, 128, 256, 16, 16.0
LN_EPS, RMS_EPS, L2_EPS = 1e-5, 1e-6, 1e-6

CH = 128
N_LEVELS = 7
HALO = 8
CONV_MXU_TAPS = (0, 4)
GATE_ROWS = 16
SOLVE_LEVELS_PER_STAGE = 1
GLA_GROUP = 8
GLA_AHEAD = 2
GLA_PIECES = 2
FIN_GROUP = 4
NEG_BIG = -1e30
LOG2E = math.log2(math.e)
VMEM_LIMIT = 56 * 1024 * 1024
ROW_TILE = 1024
FF_TILE = 1024

assert CH == A_DK == B_DK and 2 ** N_LEVELS == CH


def _dot(a, b):
    return jnp.dot(a, b, preferred_element_type=F32)


def _dot_nt(a, b):
    return lax.dot_general(a, b, (((1,), (1,)), ((), ())), preferred_element_type=F32)


def _dot_tn(a, b):
    return lax.dot_general(a, b, (((0,), (0,)), ((), ())), preferred_element_type=F32)


def _split(x, n, axis=1):
    pieces = []
    for _ in range(n - 1):
        p = x.astype(BF16)
        pieces.append(p)
        x = x - p.astype(F32)
    pieces.append(x.astype(BF16))
    return jnp.concatenate(pieces, axis=axis)


def _fold(y, n, axis=1):
    w = y.shape[axis] // n
    blocks = [lax.slice_in_dim(y, i * w, (i + 1) * w, axis=axis) for i in range(n)]
    out = blocks[0]
    for b in blocks[1:]:
        out = out + b
    return out


def _sigmoid(x):
    return 0.5 + 0.5 * jnp.tanh(0.5 * x)


def _silu(x):
    h = 0.5 * x
    return h + h * jnp.tanh(h)


def _softplus(x):
    return jnp.maximum(x, 0.0) + jnp.log(1.0 + jnp.exp(-jnp.abs(x)))


def _layernorm(y, g, b):
    mu = jnp.mean(y, axis=-1, keepdims=True)
    yc = y - mu
    var = jnp.mean(yc * yc, axis=-1, keepdims=True)
    return yc * lax.rsqrt(var + LN_EPS) * g + b


def _order_masks(rev):
    row = lax.broadcasted_iota(jnp.int32, (CH, CH), 0)
    col = lax.broadcasted_iota(jnp.int32, (CH, CH), 1)
    if rev:
        return col >= row, col > row
    return col <= row, col < row


def _group(n, want):
    return math.gcd(n, want)


def _gdn_kernel(xb_ref, wh_ref, cw_ref, hp_ref, ng_ref, lm_ref, sh_ref, o_ref,
                p_ref, qkv_ref, gate_ref, a_ref, t_ref, qk_ref, rhs_ref, qd_ref, kd_ref, gl_ref,
                mc_ref, qc_ref, rc_ref, oc_ref, *, seq):
    nc = seq // CH
    spare = 2 * nc

    p_ref[0:HALO, :] = jnp.zeros((HALO, p_ref.shape[1]), F32)
    p_ref[HALO + seq:, :] = jnp.zeros((HALO, p_ref.shape[1]), F32)
    p_ref[HALO:HALO + seq, :] = _dot(xb_ref[0], wh_ref[0])
    for ref in (a_ref, t_ref, qk_ref, kd_ref, rhs_ref, qd_ref):
        ref[spare] = jnp.zeros(ref.shape[1:], ref.dtype)
    qkv_ref[nc] = jnp.zeros(qkv_ref.shape[1:], F32)
    gate_ref[nc] = jnp.zeros(gate_ref.shape[1:], F32)

    cw = cw_ref[0]
    hp = hp_ref[0]

    def tiles(it, lag):
        c = it - lag
        ok = jnp.logical_and(c >= 0, c < nc)
        return jnp.where(ok, c, spare), jnp.where(ok, nc + c, spare)

    def prep_one(c):
        base = pl.multiple_of(c * CH, CH)
        win = p_ref[pl.ds(base, CH + 2 * HALO), 0:3 * A_DK]
        gates = p_ref[pl.ds(base + HALO, CH), 4 * A_DK:5 * A_DK].T[0:GATE_ROWS, :]
        if CONV_MXU_TAPS:
            shifted = _dot(sh_ref[...], win.astype(BF16))
        mid = A_CONV // 2
        acc = win[HALO:HALO + CH, :] * cw[mid:mid + 1, :]
        for i in range(A_CONV):
            if i != mid and i not in CONV_MXU_TAPS:
                off = HALO + i - mid
                acc = acc + win[off:off + CH, :] * cw[i:i + 1, :]
        g_rows = -hp[0] * _softplus(gates + hp[1])
        pieces = _split(g_rows, 3, axis=0)
        gcum = [_fold(_dot(pieces, _order_masks(not rev)[0].astype(BF16)), 3, axis=0) for rev in (False, True)]
        yield
        for n, i in enumerate(CONV_MXU_TAPS):
            acc = acc + shifted[n * CH:(n + 1) * CH, :] * cw[i:i + 1, :]
        yield
        s = _silu(acc)
        q = s[:, 0:A_DK]
        k = s[:, A_DK:2 * A_DK]
        q = q * (lax.rsqrt(jnp.sum(q * q, axis=-1, keepdims=True) + L2_EPS) * (A_DK ** -0.5))
        k = k * lax.rsqrt(jnp.sum(k * k, axis=-1, keepdims=True) + L2_EPS)
        yield
        qkv_ref[c] = jnp.concatenate([q, k, s[:, 2 * A_DK:]], axis=1)
        gate_ref[c] = jnp.concatenate([_sigmoid(gates), gcum[0], gcum[1]], axis=0)
        yield

    def prep_two(jf, jb, qkv, gate):
        q = qkv[:, 0:A_DK]
        k = qkv[:, A_DK:2 * A_DK]
        v = qkv[:, 2 * A_DK:]
        dirs = ((0, False, jf), (1, True, jb))
        gr = [jnp.broadcast_to(gate[(1 + d) * GATE_ROWS + 2 + d:(1 + d) * GATE_ROWS + 3 + d, :], (CH, CH))
              for d, _, _ in dirs]
        gc = [x.T for x in gr]
        beta = [jnp.broadcast_to(gate[d:d + 1, :], (CH, CH)).T for d, _, _ in dirs]
        kb16 = k.astype(BF16)
        eye = (lax.broadcasted_iota(jnp.int32, (CH, CH), 0)
               == lax.broadcasted_iota(jnp.int32, (CH, CH), 1)).astype(F32)
        yield
        kbeta = [k * beta[d] for d, _, _ in dirs]
        kq = [_dot_nt(jnp.concatenate([kbeta[d], q], axis=0).astype(BF16), kb16) for d, _, _ in dirs]
        gtot = [gc[d][0:1, :] if rev else gc[d][CH - 1:CH, :] for d, rev, _ in dirs]
        eg = [jnp.exp2(gc[d]) for d, _, _ in dirs]
        dmat = [jnp.exp2(jnp.where(_order_masks(rev)[0], gc[d] - gr[d], NEG_BIG)) for d, rev, _ in dirs]
        yield
        for d, rev, idx in dirs:
            rhs_ref[idx] = jnp.concatenate([v * beta[d], kbeta[d] * eg[d]], axis=1).astype(BF16)
            qd_ref[idx] = q * eg[d]
            kd_ref[idx] = (k * jnp.exp2(gtot[d] - gc[d])).astype(BF16)
            gl_ref[idx] = jnp.broadcast_to(jnp.exp2(gtot[d]), (8, A_DK))
        yield
        for d, rev, idx in dirs:
            a = jnp.where(_order_masks(rev)[1], kq[d][0:CH] * dmat[d], 0.0).astype(BF16)
            a_ref[idx] = a
            t_ref[idx] = (eye - (a * lm_ref[0]).astype(F32)).astype(BF16)
            qk_ref[idx] = (kq[d][CH:] * dmat[d]).astype(BF16)
        yield

    per_stage = SOLVE_LEVELS_PER_STAGE
    n_stages = (N_LEVELS - 1) // per_stage
    solve_stages = tuple((2 + s, tuple(range(1 + s * per_stage, 1 + (s + 1) * per_stage)))
                         for s in range(n_stages))
    ops_lag = n_stages + 2

    def step(it, stage_one, stage_two):
        fillers = []
        if stage_two:
            c2 = it - 1
            ok = jnp.logical_and(c2 >= 0, c2 < nc)
            cq = jnp.where(ok, c2, nc)
            fillers.append(prep_two(*tiles(it, 1), qkv_ref[cq], gate_ref[cq]))
        loaded = []
        for lag, lvs in solve_stages:
            jf, jb = tiles(it, lag)
            loaded.append((jf, jb, t_ref[jf], t_ref[jb], a_ref[jf], a_ref[jb]))
        ops_in = [(i, t_ref[i], rhs_ref[i], kd_ref[i], qk_ref[i], qd_ref[i]) for i in tiles(it, ops_lag)]
        if stage_one:
            fillers.insert(0, prep_one(it))

        def fill():
            for f in fillers:
                next(f, None)

        uws = [_dot(t, rhs) for i, t, rhs, kd, qkm, qd in ops_in]
        for half in range(per_stage):
            xs = []
            for (lag, lvs), (jf, jb, tf, tb, af, ab) in zip(solve_stages, loaded):
                m = lm_ref[lvs[half]]
                xs.append((_dot(tf, af * m), _dot(tb, ab * m)))
            fill()
            fill()
            ys = [jnp.concatenate([_dot(xf.astype(BF16), tf), _dot(xb.astype(BF16), tb)], axis=1)
                  for (xf, xb), (jf, jb, tf, tb, af, ab) in zip(xs, loaded)]
            if half == 0:
                uws = [uw.astype(BF16) for uw in uws]
                kuws = [_dot_tn(kd, uw) for uw, (i, t, rhs, kd, qkm, qd) in zip(uws, ops_in)]
                quws = [_dot(qkm, uw) for uw, (i, t, rhs, kd, qkm, qd) in zip(uws, ops_in)]
            fill()
            fill()
            loaded = [(jf, jb, tf - y[:, 0:CH].astype(BF16), tb - y[:, CH:].astype(BF16), af, ab)
                      for y, (jf, jb, tf, tb, af, ab) in zip(ys, loaded)]
        for f in fillers:
            for _ in f:
                pass
        for jf, jb, tf, tb, _, _ in loaded:
            t_ref[jf] = tf
            t_ref[jb] = tb
        for kuw, quw, (i, t, rhs, kd, qkm, qd) in zip(kuws, quws, ops_in):
            qc_ref[i] = kuw[:, 0:A_DV]
            mc_ref[i] = kuw[:, A_DV:].astype(BF16)
            oc_ref[i] = quw[:, 0:A_DV]
            rc_ref[i] = (qd - quw[:, A_DV:]).astype(BF16)

    def loop(lo, hi, stage_one, stage_two):
        def body(it, carry):
            step(it, stage_one, stage_two)
            return carry
        lax.fori_loop(lo, hi, body, 0)

    loop(0, nc, True, True)
    loop(nc, nc + 1, False, True)
    loop(nc + 1, nc + ops_lag, False, False)

    def scan(c, carry):
        sf, sb = carry
        cf = c
        cb = 2 * nc - 1 - c
        of = _dot(rc_ref[cf], sf.astype(BF16)) + oc_ref[cf]
        ob = _dot(rc_ref[cb], sb.astype(BF16)) + oc_ref[cb]
        oc_ref[cf] = of
        oc_ref[cb] = ob
        sf = gl_ref[cf][0:1, :] * sf - _dot(mc_ref[cf], sf.astype(BF16)) + qc_ref[cf]
        sb = gl_ref[cb][0:1, :] * sb - _dot(mc_ref[cb], sb.astype(BF16)) + qc_ref[cb]
        return sf, sb

    zero = jnp.zeros((A_DK, A_DV), F32)
    lax.fori_loop(0, nc, scan, (zero, zero))

    ng = ng_ref[...]
    fg = _group(nc, FIN_GROUP)

    def fin(it, carry):
        cs = [it * fg + u for u in range(fg)]
        rows = [pl.ds(pl.multiple_of(c * CH, CH), CH) for c in cs]
        os_ = [oc_ref[c] + oc_ref[nc + c] for c in cs]
        inv = [lax.rsqrt(jnp.mean(o * o, axis=-1, keepdims=True) + RMS_EPS) for o in os_]
        for c, r, o, s in zip(cs, rows, os_, inv):
            z = p_ref[pl.ds(pl.multiple_of(c * CH, CH) + HALO, CH), 3 * A_DK:4 * A_DK]
            o_ref[0, r, :] = (o * s * ng * _silu(z)).astype(BF16)
        return carry

    lax.fori_loop(0, nc // fg, fin, 0)


def _gdn_level_masks():
    idx = np.arange(CH)
    x = idx[:, None] ^ idx[None, :]
    return jnp.asarray(np.stack([(x >> lv) == 1 for lv in range(N_LEVELS)]), BF16)


def _conv_shift_matrices():
    t = np.arange(CH)[:, None]
    r = np.arange(CH + 2 * HALO)[None, :]
    taps = CONV_MXU_TAPS or (0,)
    return jnp.asarray(np.concatenate([r == t + HALO + i - A_CONV // 2 for i in taps], axis=0), BF16)


def _gdn_mixer(xb, w_in, conv_w, a_log, dt_bias, norm_g):
    bn, seq, dm = xb.shape
    h, dk = A_HEADS, A_DK
    nc = seq // CH
    w = w_in
    hw = h * dk
    ba = w[:, 4 * hw:].reshape(dm, 2, 2, h)
    per_head = [w[:, i * hw:(i + 1) * hw].reshape(dm, h, dk).transpose(1, 0, 2) for i in range(4)]
    gate_cols = jnp.pad(ba.reshape(dm, 4, h).transpose(2, 0, 1), ((0, 0), (0, 0), (0, dk - 4)))
    wh = jnp.concatenate(per_head + [gate_cols], axis=2).astype(BF16)
    cw = conv_w.reshape(A_CONV, 3, h, dk).transpose(2, 0, 1, 3).reshape(h, A_CONV, 3 * dk)
    cw = jnp.pad(cw, ((0, 0), (0, 8 - A_CONV), (0, 0))).astype(F32)
    scale = jnp.zeros((h, GATE_ROWS), F32).at[:, 2:4].set((jnp.exp(a_log.astype(F32)) * LOG2E).T)
    bias = jnp.zeros((h, GATE_ROWS), F32).at[:, 2:4].set(dt_bias.astype(F32).T)
    hp = jnp.broadcast_to(jnp.stack([scale, bias], axis=1)[:, :, :, None], (h, 2, GATE_ROWS, dk))
    ng = norm_g.astype(F32).reshape(1, A_DV)
    lm = _gdn_level_masks()
    sh = _conv_shift_matrices()
    nw = wh.shape[2]

    kern = functools.partial(_gdn_kernel, seq=seq)
    tile = lambda dt: pltpu.VMEM((2 * nc + 1, CH, CH), dt)
    return pl.pallas_call(
        kern,
        out_shape=jax.ShapeDtypeStruct((bn, seq, h * A_DV), BF16),
        grid=(bn, h),
        in_specs=[
            pl.BlockSpec((1, seq, dm), lambda b, i: (b, 0, 0)),
            pl.BlockSpec((1, dm, nw), lambda b, i: (i, 0, 0)),
            pl.BlockSpec((1, 8, 3 * dk), lambda b, i: (i, 0, 0)),
            pl.BlockSpec((1, 2, GATE_ROWS, dk), lambda b, i: (i, 0, 0, 0)),
            pl.BlockSpec((1, A_DV), lambda b, i: (0, 0)),
            pl.BlockSpec(lm.shape, lambda b, i: (0, 0, 0)),
            pl.BlockSpec(sh.shape, lambda b, i: (0, 0)),
        ],
        out_specs=pl.BlockSpec((1, seq, A_DV), lambda b, i: (b, 0, i)),
        scratch_shapes=[
            pltpu.VMEM((seq + 2 * HALO, nw), F32),
            pltpu.VMEM((nc + 1, CH, 3 * A_DK), F32),
            pltpu.VMEM((nc + 1, 3 * GATE_ROWS, CH), F32),
            tile(BF16),
            tile(BF16),
            tile(BF16),
            pltpu.VMEM((2 * nc + 1, CH, A_DV + A_DK), BF16),
            tile(F32),
            tile(BF16),
            pltpu.VMEM((2 * nc + 1, 8, A_DK), F32),
            tile(BF16),
            tile(F32),
            tile(BF16),
            tile(F32),
        ],
        compiler_params=pltpu.CompilerParams(
            dimension_semantics=("arbitrary", "arbitrary"), vmem_limit_bytes=VMEM_LIMIT),
        name="gdn_mixer",
    )(xb, wh, cw, hp, ng, lm, sh)


def _gla_tables():
    i = np.arange(CH)[:, None]
    t = np.arange(CH)[None, :]
    seg = np.zeros((2, N_LEVELS + 1, CH, CH), np.float32)
    lvl = np.zeros((2, CH, CH), np.int32)
    for d in range(2):
        rev = d == 1
        seg[d, 0] = (t >= i) if rev else (t <= i)
        lv = np.full((CH, CH), N_LEVELS + 1, np.int32)
        lv[np.arange(CH), np.arange(CH)] = N_LEVELS
        x = i ^ t
        for l in range(N_LEVELS):
            h = 2 ** (N_LEVELS - 1 - l)
            b0 = (i // (2 * h)) * (2 * h)
            if rev:
                r = b0 + h
                late = i < r
                m = np.where(late, (t >= i) & (t < r), (t >= r) & (t < i))
                own = ((x >> (N_LEVELS - 1 - l)) == 1) & (t > i)
            else:
                r = b0 + h - 1
                late = i > r
                m = np.where(late, (t > r) & (t <= i), (t > i) & (t <= r))
                own = ((x >> (N_LEVELS - 1 - l)) == 1) & (t < i)
            seg[d, 1 + l] = m
            lv[own] = l
        lvl[d] = lv
    return jnp.asarray(seg, BF16), jnp.asarray(lvl)


def _gla_kernel(xb_ref, wh_ref, w2_ref, gb_ref, ng_ref, seg_ref, lvl_ref, o_ref,
                p_ref, qs_ref, kv_ref, st_ref, dec_ref, oi_ref, *, seq):
    nc = seq // CH
    dk, dv = B_DK, B_DV
    p_ref[...] = _dot(xb_ref[0], wh_ref[0])

    c_q, c_k, c_v, c_r, c_g = 0, dk, 2 * dk, 2 * dk + dv, 2 * dk + 2 * dv
    gg = _group(nc, GLA_GROUP)
    lanes = [(u, d) for u in range(gg) for d in range(2)]

    def prep(it, carry):
        cs = [it * gg + u for u in range(gg)]
        rows = [pl.ds(pl.multiple_of(c * CH, CH), CH) for c in cs]
        q = [p_ref[r, c_q:c_q + dk] * (dk ** -0.5) for r in rows]
        k = [p_ref[r, c_k:c_k + dk] for r in rows]
        q16 = [x.astype(BF16) for x in q]
        k16 = [x.astype(BF16) for x in k]
        v16 = [p_ref[r, c_v:c_v + dv].astype(BF16) for r in rows]
        gin = [p_ref[r, c_g:c_g + dk].astype(BF16) for r in rows]
        logit = [_dot(gin[u], w2_ref[0, d]) + gb_ref[0, d][0:1, :] for u, d in lanes]
        la3 = [_split(-_softplus(-x) * (LOG2E / B_TAU), 3) for x in logit]
        la2 = [y[:, 0:GLA_PIECES * dk] for y in la3]
        bc = [_fold(_dot(seg_ref[d, 0], y), 3) for (u, d), y in zip(lanes, la3)]

        def level_sums(l):
            h = CH >> (l + 1)
            if h < HALO:
                return [_fold(_dot(seg_ref[d, 1 + l], y), GLA_PIECES) for (u, d), y in zip(lanes, la2)]
            out = []
            for (u, d), b in zip(lanes, bc):
                blocks = []
                for lo in range(0, CH, 2 * h):
                    if d == 1:
                        ref = b[lo + h:lo + h + 1, :]
                        blocks += [b[lo:lo + h, :] - ref, ref - b[lo + h:lo + 2 * h, :]]
                    else:
                        ref = b[lo + h - 1:lo + h, :]
                        blocks += [ref - b[lo:lo + h, :], b[lo + h:lo + 2 * h, :] - ref]
                out.append(jnp.concatenate(blocks, axis=0))
            return out

        half = CH // 2
        zero_half = jnp.zeros((half, dk), BF16)

        def top_level(x, e, d, late):
            upper = (d == 1) != late
            rows = slice(half, CH) if upper else slice(0, half)
            kept = (x[rows] * e[rows]).astype(BF16)
            return jnp.concatenate([zero_half, kept] if upper else [kept, zero_half], axis=0)

        ahead = [level_sums(l) for l in range(GLA_AHEAD)]
        scores = prod = None
        for l in range(N_LEVELS):
            if l + GLA_AHEAD < N_LEVELS:
                ahead.append(level_sums(l + GLA_AHEAD))
            e = [jnp.exp2(x) for x in ahead[l]]
            if l == 0:
                ql = [top_level(q[u], x, d, True) for (u, d), x in zip(lanes, e)]
                kl = [top_level(k[u], x, d, False) for (u, d), x in zip(lanes, e)]
            else:
                ql = [(q[u] * x).astype(BF16) for (u, d), x in zip(lanes, e)]
                kl = [(k[u] * x).astype(BF16) for (u, d), x in zip(lanes, e)]
            if l == 1:
                scores = prod
            elif l > 1:
                own = [lvl_ref[d] == l - 1 for d in range(2)]
                scores = [jnp.where(own[d], p, s) for (u, d), p, s in zip(lanes, prod, scores)]
            prod = [_dot_nt(a, b) for a, b in zip(ql, kl)]
        own = [lvl_ref[d] == N_LEVELS - 1 for d in range(2)]
        scores = [jnp.where(own[d], p, s) for (u, d), p, s in zip(lanes, prod, scores)]
        diag = [_dot_nt(q16[u], k16[u]) for u in range(gg)]
        own = [lvl_ref[d] == N_LEVELS for d in range(2)]
        scores = [jnp.where(own[d], diag[u], s) for (u, d), s in zip(lanes, scores)]
        for (u, d), s, b in zip(lanes, scores, bc):
            idx = d * nc + cs[u]
            btot = b[0:1, :] if d == 1 else b[CH - 1:CH, :]
            oi_ref[idx] = _dot(s.astype(BF16), v16[u])
            qs_ref[idx] = (q[u] * jnp.exp2(b)).astype(BF16)
            kv_ref[idx] = _dot_tn((k[u] * jnp.exp2(btot - b)).astype(BF16), v16[u])
            dec_ref[idx] = jnp.exp2(jnp.broadcast_to(btot, (CH, dk)).T)
        return carry

    lax.fori_loop(0, nc // gg, prep, 0)

    def scan(c, carry):
        sf, sb = carry
        cf = c
        ib = 2 * nc - 1 - c
        st_ref[cf] = sf.astype(BF16)
        st_ref[ib] = sb.astype(BF16)
        ef = dec_ref[cf]
        eb = dec_ref[ib]
        sf = sf * jnp.concatenate([ef, ef], axis=1) + kv_ref[cf]
        sb = sb * jnp.concatenate([eb, eb], axis=1) + kv_ref[ib]
        return sf, sb

    zero = jnp.zeros((dk, dv), F32)
    lax.fori_loop(0, nc, scan, (zero, zero))

    ng = ng_ref[...]
    fg = _group(nc, FIN_GROUP)

    def fin(it, carry):
        cs = [it * fg + u for u in range(fg)]
        rows = [pl.ds(pl.multiple_of(c * CH, CH), CH) for c in cs]
        os_ = [oi_ref[c] + oi_ref[nc + c] + _dot(qs_ref[c], st_ref[c]) + _dot(qs_ref[nc + c], st_ref[nc + c])
               for c in cs]
        inv = [lax.rsqrt(jnp.mean(o * o, axis=-1, keepdims=True) + RMS_EPS) for o in os_]
        for r, o, s in zip(rows, os_, inv):
            o_ref[0, r, :] = (o * s * ng * _silu(p_ref[r, c_r:c_r + dv])).astype(BF16)
        return carry

    lax.fori_loop(0, nc // fg, fin, 0)


def _gla_mixer(xb, w_in, gate_w2, gate_b, norm_g):
    bn, seq, dm = xb.shape
    h, dk, dv = B_HEADS, B_DK, B_DV
    nc = seq // CH
    kw, vw = h * dk, h * dv
    w = w_in

    def heads(cols, width):
        return cols.reshape(dm, h, width).transpose(1, 0, 2)

    gl = jnp.pad(w[:, 2 * kw + 2 * vw:], ((0, 0), (0, dk - 2 * B_RANK)))
    wh = jnp.concatenate([
        heads(w[:, 0:kw], dk), heads(w[:, kw:2 * kw], dk),
        heads(w[:, 2 * kw:2 * kw + vw], dv), heads(w[:, 2 * kw + vw:2 * kw + 2 * vw], dv),
        jnp.broadcast_to(gl[None], (h, dm, dk))], axis=2).astype(BF16)
    w2 = gate_w2.reshape(2, B_RANK, h, dk).transpose(2, 0, 1, 3)
    w2p = jnp.zeros((h, 2, dk, dk), F32)
    w2p = w2p.at[:, 0, 0:B_RANK].set(w2[:, 0]).at[:, 1, B_RANK:2 * B_RANK].set(w2[:, 1]).astype(BF16)
    gb = gate_b.reshape(2, h, dk).transpose(1, 0, 2).astype(F32)
    gb = jnp.broadcast_to(gb[:, :, None, :], (h, 2, 8, dk))
    ng = norm_g.astype(F32).reshape(1, dv)
    seg, lvl = _gla_tables()
    nw = wh.shape[2]

    kern = functools.partial(_gla_kernel, seq=seq)
    return pl.pallas_call(
        kern,
        out_shape=jax.ShapeDtypeStruct((bn, seq, vw), BF16),
        grid=(bn, h),
        in_specs=[
            pl.BlockSpec((1, seq, dm), lambda b, i: (b, 0, 0)),
            pl.BlockSpec((1, dm, nw), lambda b, i: (i, 0, 0)),
            pl.BlockSpec((1, 2, dk, dk), lambda b, i: (i, 0, 0, 0)),
            pl.BlockSpec((1, 2, 8, dk), lambda b, i: (i, 0, 0, 0)),
            pl.BlockSpec((1, dv), lambda b, i: (0, 0)),
            pl.BlockSpec(seg.shape, lambda b, i: (0, 0, 0, 0)),
            pl.BlockSpec(lvl.shape, lambda b, i: (0, 0, 0)),
        ],
        out_specs=pl.BlockSpec((1, seq, dv), lambda b, i: (b, 0, i)),
        scratch_shapes=[
            pltpu.VMEM((seq, nw), F32),
            pltpu.VMEM((2 * nc, CH, dk), BF16),
            pltpu.VMEM((2 * nc, dk, dv), F32),
            pltpu.VMEM((2 * nc, dk, dv), BF16),
            pltpu.VMEM((2 * nc, dk, dk), F32),
            pltpu.VMEM((2 * nc, CH, dv), F32),
        ],
        compiler_params=pltpu.CompilerParams(
            dimension_semantics=("arbitrary", "arbitrary"), vmem_limit_bytes=VMEM_LIMIT),
        name="gla_mixer",
    )(xb, wh, w2p, gb, ng, seg, lvl)


def _post_kernel(o_ref, x_ref, wo_ref, w1_ref, w2_ref, ln_ref, y_ref, yb_ref, *, alpha):
    ln = ln_ref[...]
    x = x_ref[...]
    x1 = _layernorm(alpha * x + _dot(o_ref[...], wo_ref[...]), ln[0:1, :], ln[1:2, :])
    x1b = x1.astype(BF16)
    acc = jnp.zeros(x.shape, F32)
    dff = w1_ref.shape[1]
    for j in range(dff // FF_TILE):
        cols = slice(j * FF_TILE, (j + 1) * FF_TILE)
        hcur = jnp.maximum(_dot(x1b, w1_ref[:, cols]), 0.0)
        acc = acc + _dot((hcur * hcur).astype(BF16), w2_ref[cols, :])
    y = _layernorm(alpha * x1 + acc, ln[2:3, :], ln[3:4, :])
    y_ref[...] = y
    yb_ref[...] = y.astype(BF16)


def _post(o, x, w_out, w1, w2, g1, b1, g2, b2, alpha):
    t, dm = x.shape
    vw = o.shape[1]
    dff = w1.shape[1]
    tm = min(ROW_TILE, t)
    ln = jnp.pad(jnp.stack([g1, b1, g2, b2]).astype(F32), ((0, 4), (0, 0)))
    const = lambda shape: pl.BlockSpec(shape, lambda i: (0, 0), pipeline_mode=pl.Buffered(1))
    return pl.pallas_call(
        functools.partial(_post_kernel, alpha=alpha),
        out_shape=(jax.ShapeDtypeStruct((t, dm), F32), jax.ShapeDtypeStruct((t, dm), BF16)),
        grid=(t // tm,),
        in_specs=[
            pl.BlockSpec((tm, vw), lambda i: (i, 0)),
            pl.BlockSpec((tm, dm), lambda i: (i, 0)),
            const((vw, dm)), const((dm, dff)), const((dff, dm)), const((8, dm)),
        ],
        out_specs=(pl.BlockSpec((tm, dm), lambda i: (i, 0)), pl.BlockSpec((tm, dm), lambda i: (i, 0))),
        compiler_params=pltpu.CompilerParams(
            dimension_semantics=("arbitrary",), vmem_limit_bytes=VMEM_LIMIT),
        name="post",
    )(o, x, w_out.astype(BF16), w1.astype(BF16), w2.astype(BF16), ln)


def kernel(x, a_w_in, a_conv, a_alog, a_dt_bias, a_norm_g, a_w_out, b_w_in, b_gate_w2, b_gate_b,
           b_norm_g, b_w_out, ln1_g, ln1_b, mlp_w1, mlp_w2, ln2_g, ln2_b):
    bn, seq, dm = x.shape
    depth = ln1_g.shape[0]
    alpha = (2 * depth) ** 0.25
    xf = x.astype(F32).reshape(bn * seq, dm)
    xb = xf.astype(BF16)
    for i in range(depth):
        j = i // 2
        xb3 = xb.reshape(bn, seq, dm)
        if i % 2 == 0:
            o = _gdn_mixer(xb3, a_w_in[j], a_conv[j], a_alog[j], a_dt_bias[j], a_norm_g[j])
            w_out = a_w_out[j]
        else:
            o = _gla_mixer(xb3, b_w_in[j], b_gate_w2[j], b_gate_b[j], b_norm_g[j])
            w_out = b_w_out[j]
        xf, xb = _post(o.reshape(bn * seq, -1), xf, w_out, mlp_w1[i], mlp_w2[i],
                       ln1_g[i], ln1_b[i], ln2_g[i], ln2_b[i], alpha)
    return xf.reshape(bn, seq, dm).astype(x.dtype)
```

```python
import functools
import math

import numpy as np

import jax
import jax.numpy as jnp
from jax import lax
from jax.experimental import pallas as pl
from jax.experimental.pallas import tpu as pltpu

F32 = jnp.float32
BF16 = jnp.bfloat16

A_HEADS, A_DK, A_DV, A_CONV = 8, 128, 128, 5
B_HEADS, B_DK, B_DV, B_RANK, B_TAU = 4, 128, 256, 16, 16.0
LN_EPS, RMS_EPS, L2_EPS = 1e-5, 1e-6, 1e-6

CH = 128
N_LEVELS = 7
HALO = 8
CONV_MXU_TAPS = (0, 4)
GATE_ROWS = 16
SOLVE_LEVELS_PER_STAGE = 1
GLA_GROUP = 8
GLA_AHEAD = 2
GLA_PIECES = 2
FIN_GROUP = 8
NEG_BIG = -1e30
LOG2E = math.log2(math.e)
VMEM_LIMIT = 56 * 1024 * 1024
ROW_TILE = 1024
FF_TILE = 1024

assert CH == A_DK == B_DK and 2 ** N_LEVELS == CH


def _dot(a, b):
    return jnp.dot(a, b, preferred_element_type=F32)


def _dot_nt(a, b):
    return lax.dot_general(a, b, (((1,), (1,)), ((), ())), preferred_element_type=F32)


def _dot_tn(a, b):
    return lax.dot_general(a, b, (((0,), (0,)), ((), ())), preferred_element_type=F32)


def _split(x, n, axis=1):
    pieces = []
    for _ in range(n - 1):
        p = x.astype(BF16)
        pieces.append(p)
        x = x - p.astype(F32)
    pieces.append(x.astype(BF16))
    return jnp.concatenate(pieces, axis=axis)


def _fold(y, n, axis=1):
    w = y.shape[axis] // n
    blocks = [lax.slice_in_dim(y, i * w, (i + 1) * w, axis=axis) for i in range(n)]
    out = blocks[0]
    for b in blocks[1:]:
        out = out + b
    return out


def _sigmoid(x):
    return 0.5 + 0.5 * jnp.tanh(0.5 * x)


def _silu(x):
    h = 0.5 * x
    return h + h * jnp.tanh(h)


def _softplus(x):
    return jnp.maximum(x, 0.0) + jnp.log(1.0 + jnp.exp(-jnp.abs(x)))


def _layernorm(y, g, b):
    mu = jnp.mean(y, axis=-1, keepdims=True)
    yc = y - mu
    var = jnp.mean(yc * yc, axis=-1, keepdims=True)
    return yc * lax.rsqrt(var + LN_EPS) * g + b


def _order_masks(rev):
    row = lax.broadcasted_iota(jnp.int32, (CH, CH), 0)
    col = lax.broadcasted_iota(jnp.int32, (CH, CH), 1)
    if rev:
        return col >= row, col > row
    return col <= row, col < row


def _group(n, want):
    return math.gcd(n, want)


def _gdn_kernel(xb_ref, wh_ref, cw_ref, hp_ref, ng_ref, lm_ref, sh_ref, o_ref,
                p_ref, qkv_ref, gate_ref, a_ref, t_ref, qk_ref, rhs_ref, qd_ref, kd_ref, gl_ref,
                mc_ref, qc_ref, rc_ref, oc_ref, *, seq):
    nc = seq // CH
    spare = 2 * nc

    p_ref[0:HALO, :] = jnp.zeros((HALO, p_ref.shape[1]), F32)
    p_ref[HALO + seq:, :] = jnp.zeros((HALO, p_ref.shape[1]), F32)
    p_ref[HALO:HALO + seq, :] = _dot(xb_ref[0], wh_ref[0])
    for ref in (a_ref, t_ref, qk_ref, kd_ref, rhs_ref, qd_ref):
        ref[spare] = jnp.zeros(ref.shape[1:], ref.dtype)
    qkv_ref[nc] = jnp.zeros(qkv_ref.shape[1:], F32)
    gate_ref[nc] = jnp.zeros(gate_ref.shape[1:], F32)

    cw = cw_ref[0]
    hp = hp_ref[0]

    def tiles(it, lag):
        c = it - lag
        ok = jnp.logical_and(c >= 0, c < nc)
        return jnp.where(ok, c, spare), jnp.where(ok, nc + c, spare)

    def prep_one(c):
        base = pl.multiple_of(c * CH, CH)
        win = p_ref[pl.ds(base, CH + 2 * HALO), 0:3 * A_DK]
        gates = p_ref[pl.ds(base + HALO, CH), 4 * A_DK:5 * A_DK].T[0:GATE_ROWS, :]
        if CONV_MXU_TAPS:
            shifted = _dot(sh_ref[...], win.astype(BF16))
        mid = A_CONV // 2
        acc = win[HALO:HALO + CH, :] * cw[mid:mid + 1, :]
        for i in range(A_CONV):
            if i != mid and i not in CONV_MXU_TAPS:
                off = HALO + i - mid
                acc = acc + win[off:off + CH, :] * cw[i:i + 1, :]
        g_rows = -hp[0] * _softplus(gates + hp[1])
        pieces = _split(g_rows, 3, axis=0)
        gcum = [_fold(_dot(pieces, _order_masks(not rev)[0].astype(BF16)), 3, axis=0) for rev in (False, True)]
        yield
        for n, i in enumerate(CONV_MXU_TAPS):
            acc = acc + shifted[n * CH:(n + 1) * CH, :] * cw[i:i + 1, :]
        yield
        s = _silu(acc)
        q = s[:, 0:A_DK]
        k = s[:, A_DK:2 * A_DK]
        q = q * (lax.rsqrt(jnp.sum(q * q, axis=-1, keepdims=True) + L2_EPS) * (A_DK ** -0.5))
        k = k * lax.rsqrt(jnp.sum(k * k, axis=-1, keepdims=True) + L2_EPS)
        yield
        qkv_ref[c] = jnp.concatenate([q, k, s[:, 2 * A_DK:]], axis=1)
        gate_ref[c] = jnp.concatenate([_sigmoid(gates), gcum[0], gcum[1]], axis=0)
        yield

    def prep_two(jf, jb, qkv, gate):
        q = qkv[:, 0:A_DK]
        k = qkv[:, A_DK:2 * A_DK]
        v = qkv[:, 2 * A_DK:]
        dirs = ((0, False, jf), (1, True, jb))
        gr = [jnp.broadcast_to(gate[(1 + d) * GATE_ROWS + 2 + d:(1 + d) * GATE_ROWS + 3 + d, :], (CH, CH))
              for d, _, _ in dirs]
        gc = [x.T for x in gr]
        beta = [jnp.broadcast_to(gate[d:d + 1, :], (CH, CH)).T for d, _, _ in dirs]
        kb16 = k.astype(BF16)
        eye = (lax.broadcasted_iota(jnp.int32, (CH, CH), 0)
               == lax.broadcasted_iota(jnp.int32, (CH, CH), 1)).astype(F32)
        yield
        kbeta = [k * beta[d] for d, _, _ in dirs]
        kq = [_dot_nt(jnp.concatenate([kbeta[d], q], axis=0).astype(BF16), kb16) for d, _, _ in dirs]
        gtot = [gc[d][0:1, :] if rev else gc[d][CH - 1:CH, :] for d, rev, _ in dirs]
        eg = [jnp.exp2(gc[d]) for d, _, _ in dirs]
        dmat = [jnp.exp2(jnp.where(_order_masks(rev)[0], gc[d] - gr[d], NEG_BIG)) for d, rev, _ in dirs]
        yield
        for d, rev, idx in dirs:
            rhs_ref[idx] = jnp.concatenate([v * beta[d], kbeta[d] * eg[d]], axis=1).astype(BF16)
            qd_ref[idx] = q * eg[d]
            kd_ref[idx] = (k * jnp.exp2(gtot[d] - gc[d])).astype(BF16)
            gl_ref[idx] = jnp.broadcast_to(jnp.exp2(gtot[d]), (8, A_DK))
        yield
        for d, rev, idx in dirs:
            a = jnp.where(_order_masks(rev)[1], kq[d][0:CH] * dmat[d], 0.0).astype(BF16)
            a_ref[idx] = a
            t_ref[idx] = (eye - (a * lm_ref[0]).astype(F32)).astype(BF16)
            qk_ref[idx] = (kq[d][CH:] * dmat[d]).astype(BF16)
        yield

    per_stage = SOLVE_LEVELS_PER_STAGE
    n_stages = (N_LEVELS - 1) // per_stage
    solve_stages = tuple((2 + s, tuple(range(1 + s * per_stage, 1 + (s + 1) * per_stage)))
                         for s in range(n_stages))
    ops_lag = n_stages + 2

    def step(it, stage_one, stage_two):
        fillers = []
        if stage_two:
            c2 = it - 1
            ok = jnp.logical_and(c2 >= 0, c2 < nc)
            cq = jnp.where(ok, c2, nc)
            fillers.append(prep_two(*tiles(it, 1), qkv_ref[cq], gate_ref[cq]))
        loaded = []
        for lag, lvs in solve_stages:
            jf, jb = tiles(it, lag)
            loaded.append((jf, jb, t_ref[jf], t_ref[jb], a_ref[jf], a_ref[jb]))
        ops_in = [(i, t_ref[i], rhs_ref[i], kd_ref[i], qk_ref[i], qd_ref[i]) for i in tiles(it, ops_lag)]
        if stage_one:
            fillers.insert(0, prep_one(it))

        def fill():
            for f in fillers:
                next(f, None)

        uws = [_dot(t, rhs) for i, t, rhs, kd, qkm, qd in ops_in]
        for half in range(per_stage):
            xs = []
            for (lag, lvs), (jf, jb, tf, tb, af, ab) in zip(solve_stages, loaded):
                m = lm_ref[lvs[half]]
                xs.append((_dot(tf, af * m), _dot(tb, ab * m)))
            fill()
            fill()
            ys = [jnp.concatenate([_dot(xf.astype(BF16), tf), _dot(xb.astype(BF16), tb)], axis=1)
                  for (xf, xb), (jf, jb, tf, tb, af, ab) in zip(xs, loaded)]
            if half == 0:
                uws = [uw.astype(BF16) for uw in uws]
                kuws = [_dot_tn(kd, uw) for uw, (i, t, rhs, kd, qkm, qd) in zip(uws, ops_in)]
                quws = [_dot(qkm, uw) for uw, (i, t, rhs, kd, qkm, qd) in zip(uws, ops_in)]
            fill()
            fill()
            loaded = [(jf, jb, tf - y[:, 0:CH].astype(BF16), tb - y[:, CH:].astype(BF16), af, ab)
                      for y, (jf, jb, tf, tb, af, ab) in zip(ys, loaded)]
        for f in fillers:
            for _ in f:
                pass
        for jf, jb, tf, tb, _, _ in loaded:
            t_ref[jf] = tf
            t_ref[jb] = tb
        for kuw, quw, (i, t, rhs, kd, qkm, qd) in zip(kuws, quws, ops_in):
            qc_ref[i] = kuw[:, 0:A_DV]
            mc_ref[i] = kuw[:, A_DV:].astype(BF16)
            oc_ref[i] = quw[:, 0:A_DV]
            rc_ref[i] = (qd - quw[:, A_DV:]).astype(BF16)

    def loop(lo, hi, stage_one, stage_two):
        def body(it, carry):
            step(it, stage_one, stage_two)
            return carry
        lax.fori_loop(lo, hi, body, 0)

    loop(0, nc, True, True)
    loop(nc, nc + 1, False, True)
    loop(nc + 1, nc + ops_lag, False, False)

    def scan(c, carry):
        sf, sb = carry
        cf = c
        cb = 2 * nc - 1 - c
        of = _dot(rc_ref[cf], sf.astype(BF16)) + oc_ref[cf]
        ob = _dot(rc_ref[cb], sb.astype(BF16)) + oc_ref[cb]
        oc_ref[cf] = of
        oc_ref[cb] = ob
        sf = gl_ref[cf][0:1, :] * sf - _dot(mc_ref[cf], sf.astype(BF16)) + qc_ref[cf]
        sb = gl_ref[cb][0:1, :] * sb - _dot(mc_ref[cb], sb.astype(BF16)) + qc_ref[cb]
        return sf, sb

    zero = jnp.zeros((A_DK, A_DV), F32)
    lax.fori_loop(0, nc, scan, (zero, zero))

    ng = ng_ref[...]
    fg = _group(nc, FIN_GROUP)

    def fin(it, carry):
        cs = [it * fg + u for u in range(fg)]
        rows = [pl.ds(pl.multiple_of(c * CH, CH), CH) for c in cs]
        os_ = [oc_ref[c] + oc_ref[nc + c] for c in cs]
        inv = [lax.rsqrt(jnp.mean(o * o, axis=-1, keepdims=True) + RMS_EPS) for o in os_]
        for c, r, o, s in zip(cs, rows, os_, inv):
            z = p_ref[pl.ds(pl.multiple_of(c * CH, CH) + HALO, CH), 3 * A_DK:4 * A_DK]
            o_ref[0, r, :] = (o * s * ng * _silu(z)).astype(BF16)
        return carry

    lax.fori_loop(0, nc // fg, fin, 0)


def _gdn_level_masks():
    idx = np.arange(CH)
    x = idx[:, None] ^ idx[None, :]
    return jnp.asarray(np.stack([(x >> lv) == 1 for lv in range(N_LEVELS)]), BF16)


def _conv_shift_matrices():
    t = np.arange(CH)[:, None]
    r = np.arange(CH + 2 * HALO)[None, :]
    taps = CONV_MXU_TAPS or (0,)
    return jnp.asarray(np.concatenate([r == t + HALO + i - A_CONV // 2 for i in taps], axis=0), BF16)


def _gdn_mixer(xb, w_in, conv_w, a_log, dt_bias, norm_g):
    bn, seq, dm = xb.shape
    h, dk = A_HEADS, A_DK
    nc = seq // CH
    w = w_in
    hw = h * dk
    ba = w[:, 4 * hw:].reshape(dm, 2, 2, h)
    per_head = [w[:, i * hw:(i + 1) * hw].reshape(dm, h, dk).transpose(1, 0, 2) for i in range(4)]
    gate_cols = jnp.pad(ba.reshape(dm, 4, h).transpose(2, 0, 1), ((0, 0), (0, 0), (0, dk - 4)))
    wh = jnp.concatenate(per_head + [gate_cols], axis=2).astype(BF16)
    cw = conv_w.reshape(A_CONV, 3, h, dk).transpose(2, 0, 1, 3).reshape(h, A_CONV, 3 * dk)
    cw = jnp.pad(cw, ((0, 0), (0, 8 - A_CONV), (0, 0))).astype(F32)
    scale = jnp.zeros((h, GATE_ROWS), F32).at[:, 2:4].set((jnp.exp(a_log.astype(F32)) * LOG2E).T)
    bias = jnp.zeros((h, GATE_ROWS), F32).at[:, 2:4].set(dt_bias.astype(F32).T)
    hp = jnp.broadcast_to(jnp.stack([scale, bias], axis=1)[:, :, :, None], (h, 2, GATE_ROWS, dk))
    ng = norm_g.astype(F32).reshape(1, A_DV)
    lm = _gdn_level_masks()
    sh = _conv_shift_matrices()
    nw = wh.shape[2]

    kern = functools.partial(_gdn_kernel, seq=seq)
    tile = lambda dt: pltpu.VMEM((2 * nc + 1, CH, CH), dt)
    return pl.pallas_call(
        kern,
        out_shape=jax.ShapeDtypeStruct((bn, seq, h * A_DV), BF16),
        grid=(bn, h),
        in_specs=[
            pl.BlockSpec((1, seq, dm), lambda b, i: (b, 0, 0)),
            pl.BlockSpec((1, dm, nw), lambda b, i: (i, 0, 0)),
            pl.BlockSpec((1, 8, 3 * dk), lambda b, i: (i, 0, 0)),
            pl.BlockSpec((1, 2, GATE_ROWS, dk), lambda b, i: (i, 0, 0, 0)),
            pl.BlockSpec((1, A_DV), lambda b, i: (0, 0)),
            pl.BlockSpec(lm.shape, lambda b, i: (0, 0, 0)),
            pl.BlockSpec(sh.shape, lambda b, i: (0, 0)),
        ],
        out_specs=pl.BlockSpec((1, seq, A_DV), lambda b, i: (b, 0, i)),
        scratch_shapes=[
            pltpu.VMEM((seq + 2 * HALO, nw), F32),
            pltpu.VMEM((nc + 1, CH, 3 * A_DK), F32),
            pltpu.VMEM((nc + 1, 3 * GATE_ROWS, CH), F32),
            tile(BF16),
            tile(BF16),
            tile(BF16),
            pltpu.VMEM((2 * nc + 1, CH, A_DV + A_DK), BF16),
            tile(F32),
            tile(BF16),
            pltpu.VMEM((2 * nc + 1, 8, A_DK), F32),
            tile(BF16),
            tile(F32),
            tile(BF16),
            tile(F32),
        ],
        compiler_params=pltpu.CompilerParams(
            dimension_semantics=("arbitrary", "arbitrary"), vmem_limit_bytes=VMEM_LIMIT),
        name="gdn_mixer",
    )(xb, wh, cw, hp, ng, lm, sh)


def _gla_tables():
    i = np.arange(CH)[:, None]
    t = np.arange(CH)[None, :]
    seg = np.zeros((2, N_LEVELS + 1, CH, CH), np.float32)
    lvl = np.zeros((2, CH, CH), np.int32)
    for d in range(2):
        rev = d == 1
        seg[d, 0] = (t >= i) if rev else (t <= i)
        lv = np.full((CH, CH), N_LEVELS + 1, np.int32)
        lv[np.arange(CH), np.arange(CH)] = N_LEVELS
        x = i ^ t
        for l in range(N_LEVELS):
            h = 2 ** (N_LEVELS - 1 - l)
            b0 = (i // (2 * h)) * (2 * h)
            if rev:
                r = b0 + h
                late = i < r
                m = np.where(late, (t >= i) & (t < r), (t >= r) & (t < i))
                own = ((x >> (N_LEVELS - 1 - l)) == 1) & (t > i)
            else:
                r = b0 + h - 1
                late = i > r
                m = np.where(late, (t > r) & (t <= i), (t > i) & (t <= r))
                own = ((x >> (N_LEVELS - 1 - l)) == 1) & (t < i)
            seg[d, 1 + l] = m
            lv[own] = l
        lvl[d] = lv
    return jnp.asarray(seg, BF16), jnp.asarray(lvl)


def _gla_kernel(xb_ref, wh_ref, w2_ref, gb_ref, ng_ref, seg_ref, lvl_ref, o_ref,
                p_ref, qs_ref, kv_ref, st_ref, dec_ref, oi_ref, *, seq):
    nc = seq // CH
    dk, dv = B_DK, B_DV
    p_ref[...] = _dot(xb_ref[0], wh_ref[0])

    c_q, c_k, c_v, c_r, c_g = 0, dk, 2 * dk, 2 * dk + dv, 2 * dk + 2 * dv
    gg = _group(nc, GLA_GROUP)
    lanes = [(u, d) for u in range(gg) for d in range(2)]

    def prep(it, carry):
        cs = [it * gg + u for u in range(gg)]
        rows = [pl.ds(pl.multiple_of(c * CH, CH), CH) for c in cs]
        q = [p_ref[r, c_q:c_q + dk] * (dk ** -0.5) for r in rows]
        k = [p_ref[r, c_k:c_k + dk] for r in rows]
        q16 = [x.astype(BF16) for x in q]
        k16 = [x.astype(BF16) for x in k]
        v16 = [p_ref[r, c_v:c_v + dv].astype(BF16) for r in rows]
        gin = [p_ref[r, c_g:c_g + dk].astype(BF16) for r in rows]
        logit = [_dot(gin[u], w2_ref[0, d]) + gb_ref[0, d][0:1, :] for u, d in lanes]
        la3 = [_split(-_softplus(-x) * (LOG2E / B_TAU), 3) for x in logit]
        la2 = [y[:, 0:GLA_PIECES * dk] for y in la3]
        bc = [_fold(_dot(seg_ref[d, 0], y), 3) for (u, d), y in zip(lanes, la3)]

        def level_sums(l):
            h = CH >> (l + 1)
            if h < HALO:
                return [_fold(_dot(seg_ref[d, 1 + l], y), GLA_PIECES) for (u, d), y in zip(lanes, la2)]
            out = []
            for (u, d), b in zip(lanes, bc):
                blocks = []
                for lo in range(0, CH, 2 * h):
                    if d == 1:
                        ref = b[lo + h:lo + h + 1, :]
                        blocks += [b[lo:lo + h, :] - ref, ref - b[lo + h:lo + 2 * h, :]]
                    else:
                        ref = b[lo + h - 1:lo + h, :]
                        blocks += [ref - b[lo:lo + h, :], b[lo + h:lo + 2 * h, :] - ref]
                out.append(jnp.concatenate(blocks, axis=0))
            return out

        half = CH // 2
        zero_half = jnp.zeros((half, dk), BF16)

        def top_level(x, e, d, late):
            upper = (d == 1) != late
            rows = slice(half, CH) if upper else slice(0, half)
            kept = (x[rows] * e[rows]).astype(BF16)
            return jnp.concatenate([zero_half, kept] if upper else [kept, zero_half], axis=0)

        ahead = [level_sums(l) for l in range(GLA_AHEAD)]
        scores = prod = None
        for l in range(N_LEVELS):
            if l + GLA_AHEAD < N_LEVELS:
                ahead.append(level_sums(l + GLA_AHEAD))
            e = [jnp.exp2(x) for x in ahead[l]]
            if l == 0:
                ql = [top_level(q[u], x, d, True) for (u, d), x in zip(lanes, e)]
                kl = [top_level(k[u], x, d, False) for (u, d), x in zip(lanes, e)]
            else:
                ql = [(q[u] * x).astype(BF16) for (u, d), x in zip(lanes, e)]
                kl = [(k[u] * x).astype(BF16) for (u, d), x in zip(lanes, e)]
            if l == 1:
                scores = prod
            elif l > 1:
                own = [lvl_ref[d] == l - 1 for d in range(2)]
                scores = [jnp.where(own[d], p, s) for (u, d), p, s in zip(lanes, prod, scores)]
            prod = [_dot_nt(a, b) for a, b in zip(ql, kl)]
        own = [lvl_ref[d] == N_LEVELS - 1 for d in range(2)]
        scores = [jnp.where(own[d], p, s) for (u, d), p, s in zip(lanes, prod, scores)]
        diag = [_dot_nt(q16[u], k16[u]) for u in range(gg)]
        own = [lvl_ref[d] == N_LEVELS for d in range(2)]
        scores = [jnp.where(own[d], diag[u], s) for (u, d), s in zip(lanes, scores)]
        for (u, d), s, b in zip(lanes, scores, bc):
            idx = d * nc + cs[u]
            btot = b[0:1, :] if d == 1 else b[CH - 1:CH, :]
            oi_ref[idx] = _dot(s.astype(BF16), v16[u])
            qs_ref[idx] = (q[u] * jnp.exp2(b)).astype(BF16)
            kv_ref[idx] = _dot_tn((k[u] * jnp.exp2(btot - b)).astype(BF16), v16[u])
            dec_ref[idx] = jnp.exp2(jnp.broadcast_to(btot, (CH, dk)).T)
        return carry

    lax.fori_loop(0, nc // gg, prep, 0)

    def scan(c, carry):
        sf, sb = carry
        cf = c
        ib = 2 * nc - 1 - c
        st_ref[cf] = sf.astype(BF16)
        st_ref[ib] = sb.astype(BF16)
        ef = dec_ref[cf]
        eb = dec_ref[ib]
        sf = sf * jnp.concatenate([ef, ef], axis=1) + kv_ref[cf]
        sb = sb * jnp.concatenate([eb, eb], axis=1) + kv_ref[ib]
        return sf, sb

    zero = jnp.zeros((dk, dv), F32)
    lax.fori_loop(0, nc, scan, (zero, zero))

    ng = ng_ref[...]
    fg = _group(nc, FIN_GROUP)

    def fin(it, carry):
        cs = [it * fg + u for u in range(fg)]
        rows = [pl.ds(pl.multiple_of(c * CH, CH), CH) for c in cs]
        os_ = [oi_ref[c] + oi_ref[nc + c] + _dot(qs_ref[c], st_ref[c]) + _dot(qs_ref[nc + c], st_ref[nc + c])
               for c in cs]
        inv = [lax.rsqrt(jnp.mean(o * o, axis=-1, keepdims=True) + RMS_EPS) for o in os_]
        for r, o, s in zip(rows, os_, inv):
            o_ref[0, r, :] = (o * s * ng * _silu(p_ref[r, c_r:c_r + dv])).astype(BF16)
        return carry

    lax.fori_loop(0, nc // fg, fin, 0)


def _gla_mixer(xb, w_in, gate_w2, gate_b, norm_g):
    bn, seq, dm = xb.shape
    h, dk, dv = B_HEADS, B_DK, B_DV
    nc = seq // CH
    kw, vw = h * dk, h * dv
    w = w_in

    def heads(cols, width):
        return cols.reshape(dm, h, width).transpose(1, 0, 2)

    gl = jnp.pad(w[:, 2 * kw + 2 * vw:], ((0, 0), (0, dk - 2 * B_RANK)))
    wh = jnp.concatenate([
        heads(w[:, 0:kw], dk), heads(w[:, kw:2 * kw], dk),
        heads(w[:, 2 * kw:2 * kw + vw], dv), heads(w[:, 2 * kw + vw:2 * kw + 2 * vw], dv),
        jnp.broadcast_to(gl[None], (h, dm, dk))], axis=2).astype(BF16)
    w2 = gate_w2.reshape(2, B_RANK, h, dk).transpose(2, 0, 1, 3)
    w2p = jnp.zeros((h, 2, dk, dk), F32)
    w2p = w2p.at[:, 0, 0:B_RANK].set(w2[:, 0]).at[:, 1, B_RANK:2 * B_RANK].set(w2[:, 1]).astype(BF16)
    gb = gate_b.reshape(2, h, dk).transpose(1, 0, 2).astype(F32)
    gb = jnp.broadcast_to(gb[:, :, None, :], (h, 2, 8, dk))
    ng = norm_g.astype(F32).reshape(1, dv)
    seg, lvl = _gla_tables()
    nw = wh.shape[2]

    kern = functools.partial(_gla_kernel, seq=seq)
    return pl.pallas_call(
        kern,
        out_shape=jax.ShapeDtypeStruct((bn, seq, vw), BF16),
        grid=(bn, h),
        in_specs=[
            pl.BlockSpec((1, seq, dm), lambda b, i: (b, 0, 0)),
            pl.BlockSpec((1, dm, nw), lambda b, i: (i, 0, 0)),
            pl.BlockSpec((1, 2, dk, dk), lambda b, i: (i, 0, 0, 0)),
            pl.BlockSpec((1, 2, 8, dk), lambda b, i: (i, 0, 0, 0)),
            pl.BlockSpec((1, dv), lambda b, i: (0, 0)),
            pl.BlockSpec(seg.shape, lambda b, i: (0, 0, 0, 0)),
            pl.BlockSpec(lvl.shape, lambda b, i: (0, 0, 0)),
        ],
        out_specs=pl.BlockSpec((1, seq, dv), lambda b, i: (b, 0, i)),
        scratch_shapes=[
            pltpu.VMEM((seq, nw), F32),
            pltpu.VMEM((2 * nc, CH, dk), BF16),
            pltpu.VMEM((2 * nc, dk, dv), F32),
            pltpu.VMEM((2 * nc, dk, dv), BF16),
            pltpu.VMEM((2 * nc, dk, dk), F32),
            pltpu.VMEM((2 * nc, CH, dv), F32),
        ],
        compiler_params=pltpu.CompilerParams(
            dimension_semantics=("arbitrary", "arbitrary"), vmem_limit_bytes=VMEM_LIMIT),
        name="gla_mixer",
    )(xb, wh, w2p, gb, ng, seg, lvl)


def _post_kernel(o_ref, x_ref, wo_ref, w1_ref, w2_ref, ln_ref, y_ref, yb_ref, *, alpha):
    ln = ln_ref[...]
    x = x_ref[...]
    x1 = _layernorm(alpha * x + _dot(o_ref[...], wo_ref[...]), ln[0:1, :], ln[1:2, :])
    x1b = x1.astype(BF16)
    acc = jnp.zeros(x.shape, F32)
    dff = w1_ref.shape[1]
    for j in range(dff // FF_TILE):
        cols = slice(j * FF_TILE, (j + 1) * FF_TILE)
        hcur = jnp.maximum(_dot(x1b, w1_ref[:, cols]), 0.0)
        acc = acc + _dot((hcur * hcur).astype(BF16), w2_ref[cols, :])
    y = _layernorm(alpha * x1 + acc, ln[2:3, :], ln[3:4, :])
    y_ref[...] = y
    yb_ref[...] = y.astype(BF16)


def _post(o, x, w_out, w1, w2, g1, b1, g2, b2, alpha):
    t, dm = x.shape
    vw = o.shape[1]
    dff = w1.shape[1]
    tm = min(ROW_TILE, t)
    ln = jnp.pad(jnp.stack([g1, b1, g2, b2]).astype(F32), ((0, 4), (0, 0)))
    const = lambda shape: pl.BlockSpec(shape, lambda i: (0, 0), pipeline_mode=pl.Buffered(1))
    return pl.pallas_call(
        functools.partial(_post_kernel, alpha=alpha),
        out_shape=(jax.ShapeDtypeStruct((t, dm), F32), jax.ShapeDtypeStruct((t, dm), BF16)),
        grid=(t // tm,),
        in_specs=[
            pl.BlockSpec((tm, vw), lambda i: (i, 0)),
            pl.BlockSpec((tm, dm), lambda i: (i, 0)),
            const((vw, dm)), const((dm, dff)), const((dff, dm)), const((8, dm)),
        ],
        out_specs=(pl.BlockSpec((tm, dm), lambda i: (i, 0)), pl.BlockSpec((tm, dm), lambda i: (i, 0))),
        compiler_params=pltpu.CompilerParams(
            dimension_semantics=("arbitrary",), vmem_limit_bytes=VMEM_LIMIT),
        name="post",
    )(o, x, w_out.astype(BF16), w1.astype(BF16), w2.astype(BF16), ln)


def kernel(x, a_w_in, a_conv, a_alog, a_dt_bias, a_norm_g, a_w_out, b_w_in, b_gate_w2, b_gate_b,
           b_norm_g, b_w_out, ln1_g, ln1_b, mlp_w1, mlp_w2, ln2_g, ln2_b):
    bn, seq, dm = x.shape
    depth = ln1_g.shape[0]
    alpha = (2 * depth) ** 0.25
    xf = x.astype(F32).reshape(bn * seq, dm)
    xb = xf.astype(BF16)
    for i in range(depth):
        j = i // 2
        xb3 = xb.reshape(bn, seq, dm)
        if i % 2 == 0:
            o = _gdn_mixer(xb3, a_w_in[j], a_conv[j], a_alog[j], a_dt_bias[j], a_norm_g[j])
            w_out = a_w_out[j]
        else:
            o = _gla_mixer(xb3, b_w_in[j], b_gate_w2[j], b_gate_b[j], b_norm_g[j])
            w_out = b_w_out[j]
        xf, xb = _post(o.reshape(bn * seq, -1), xf, w_out, mlp_w1[i], mlp_w2[i],
                       ln1_g[i], ln1_b[i], ln2_g[i], ln2_b[i], alpha)
    return xf.reshape(bn, seq, dm).astype(x.dtype)
```

```python
import functools
import math

import numpy as np

import jax
import jax.numpy as jnp
from jax import lax
from jax.experimental import pallas as pl
from jax.experimental.pallas import tpu as pltpu

F32 = jnp.float32
BF16 = jnp.bfloat16

A_HEADS, A_DK, A_DV, A_CONV = 8, 128, 128, 5
B_HEADS, B_DK, B_DV, B_RANK, B_TAU = 4, 128, 256, 16, 16.0
LN_EPS, RMS_EPS, L2_EPS = 1e-5, 1e-6, 1e-6

CH = 128
N_LEVELS = 7
HALO = 8
CONV_MXU_TAPS = (0, 4)
GATE_ROWS = 16
SOLVE_LEVELS_PER_STAGE = 1
GLA_GROUP = 16
GLA_AHEAD = 2
GLA_PIECES = 2
FIN_GROUP = 16
NEG_BIG = -1e30
LOG2E = math.log2(math.e)
VMEM_LIMIT = 56 * 1024 * 1024
ROW_TILE = 1024
FF_TILE = 1024

assert CH == A_DK == B_DK and 2 ** N_LEVELS == CH


def _dot(a, b):
    return jnp.dot(a, b, preferred_element_type=F32)


def _dot_nt(a, b):
    return lax.dot_general(a, b, (((1,), (1,)), ((), ())), preferred_element_type=F32)


def _dot_tn(a, b):
    return lax.dot_general(a, b, (((0,), (0,)), ((), ())), preferred_element_type=F32)


def _split(x, n, axis=1):
    pieces = []
    for _ in range(n - 1):
        p = x.astype(BF16)
        pieces.append(p)
        x = x - p.astype(F32)
    pieces.append(x.astype(BF16))
    return jnp.concatenate(pieces, axis=axis)


def _fold(y, n, axis=1):
    w = y.shape[axis] // n
    blocks = [lax.slice_in_dim(y, i * w, (i + 1) * w, axis=axis) for i in range(n)]
    out = blocks[0]
    for b in blocks[1:]:
        out = out + b
    return out


def _sigmoid(x):
    return 0.5 + 0.5 * jnp.tanh(0.5 * x)


def _silu(x):
    h = 0.5 * x
    return h + h * jnp.tanh(h)


def _softplus(x):
    return jnp.maximum(x, 0.0) + jnp.log(1.0 + jnp.exp(-jnp.abs(x)))


def _layernorm(y, g, b):
    mu = jnp.mean(y, axis=-1, keepdims=True)
    yc = y - mu
    var = jnp.mean(yc * yc, axis=-1, keepdims=True)
    return yc * lax.rsqrt(var + LN_EPS) * g + b


def _order_masks(rev):
    row = lax.broadcasted_iota(jnp.int32, (CH, CH), 0)
    col = lax.broadcasted_iota(jnp.int32, (CH, CH), 1)
    if rev:
        return col >= row, col > row
    return col <= row, col < row


def _group(n, want):
    return math.gcd(n, want)


def _gdn_kernel(xb_ref, wh_ref, cw_ref, hp_ref, ng_ref, lm_ref, sh_ref, o_ref,
                p_ref, qkv_ref, gate_ref, a_ref, t_ref, qk_ref, rhs_ref, qd_ref, kd_ref, gl_ref,
                mc_ref, qc_ref, rc_ref, oc_ref, *, seq):
    nc = seq // CH
    spare = 2 * nc

    p_ref[0:HALO, :] = jnp.zeros((HALO, p_ref.shape[1]), F32)
    p_ref[HALO + seq:, :] = jnp.zeros((HALO, p_ref.shape[1]), F32)
    p_ref[HALO:HALO + seq, :] = _dot(xb_ref[0], wh_ref[0])
    for ref in (a_ref, t_ref, qk_ref, kd_ref, rhs_ref, qd_ref):
        ref[spare] = jnp.zeros(ref.shape[1:], ref.dtype)
    qkv_ref[nc] = jnp.zeros(qkv_ref.shape[1:], F32)
    gate_ref[nc] = jnp.zeros(gate_ref.shape[1:], F32)

    cw = cw_ref[0]
    hp = hp_ref[0]

    def tiles(it, lag):
        c = it - lag
        ok = jnp.logical_and(c >= 0, c < nc)
        return jnp.where(ok, c, spare), jnp.where(ok, nc + c, spare)

    def prep_one(c):
        base = pl.multiple_of(c * CH, CH)
        win = p_ref[pl.ds(base, CH + 2 * HALO), 0:3 * A_DK]
        gates = p_ref[pl.ds(base + HALO, CH), 4 * A_DK:5 * A_DK].T[0:GATE_ROWS, :]
        if CONV_MXU_TAPS:
            shifted = _dot(sh_ref[...], win.astype(BF16))
        mid = A_CONV // 2
        acc = win[HALO:HALO + CH, :] * cw[mid:mid + 1, :]
        for i in range(A_CONV):
            if i != mid and i not in CONV_MXU_TAPS:
                off = HALO + i - mid
                acc = acc + win[off:off + CH, :] * cw[i:i + 1, :]
        g_rows = -hp[0] * _softplus(gates + hp[1])
        pieces = _split(g_rows, 3, axis=0)
        gcum = [_fold(_dot(pieces, _order_masks(not rev)[0].astype(BF16)), 3, axis=0) for rev in (False, True)]
        yield
        for n, i in enumerate(CONV_MXU_TAPS):
            acc = acc + shifted[n * CH:(n + 1) * CH, :] * cw[i:i + 1, :]
        yield
        s = _silu(acc)
        q = s[:, 0:A_DK]
        k = s[:, A_DK:2 * A_DK]
        q = q * (lax.rsqrt(jnp.sum(q * q, axis=-1, keepdims=True) + L2_EPS) * (A_DK ** -0.5))
        k = k * lax.rsqrt(jnp.sum(k * k, axis=-1, keepdims=True) + L2_EPS)
        yield
        qkv_ref[c] = jnp.concatenate([q, k, s[:, 2 * A_DK:]], axis=1)
        gate_ref[c] = jnp.concatenate([_sigmoid(gates), gcum[0], gcum[1]], axis=0)
        yield

    def prep_two(jf, jb, qkv, gate):
        q = qkv[:, 0:A_DK]
        k = qkv[:, A_DK:2 * A_DK]
        v = qkv[:, 2 * A_DK:]
        dirs = ((0, False, jf), (1, True, jb))
        gr = [jnp.broadcast_to(gate[(1 + d) * GATE_ROWS + 2 + d:(1 + d) * GATE_ROWS + 3 + d, :], (CH, CH))
              for d, _, _ in dirs]
        gc = [x.T for x in gr]
        beta = [jnp.broadcast_to(gate[d:d + 1, :], (CH, CH)).T for d, _, _ in dirs]
        kb16 = k.astype(BF16)
        eye = (lax.broadcasted_iota(jnp.int32, (CH, CH), 0)
               == lax.broadcasted_iota(jnp.int32, (CH, CH), 1)).astype(F32)
        yield
        kbeta = [k * beta[d] for d, _, _ in dirs]
        kq = [_dot_nt(jnp.concatenate([kbeta[d], q], axis=0).astype(BF16), kb16) for d, _, _ in dirs]
        gtot = [gc[d][0:1, :] if rev else gc[d][CH - 1:CH, :] for d, rev, _ in dirs]
        eg = [jnp.exp2(gc[d]) for d, _, _ in dirs]
        dmat = [jnp.exp2(jnp.where(_order_masks(rev)[0], gc[d] - gr[d], NEG_BIG)) for d, rev, _ in dirs]
        yield
        for d, rev, idx in dirs:
            rhs_ref[idx] = jnp.concatenate([v * beta[d], kbeta[d] * eg[d]], axis=1).astype(BF16)
            qd_ref[idx] = q * eg[d]
            kd_ref[idx] = (k * jnp.exp2(gtot[d] - gc[d])).astype(BF16)
            gl_ref[idx] = jnp.broadcast_to(jnp.exp2(gtot[d]), (8, A_DK))
        yield
        for d, rev, idx in dirs:
            a = jnp.where(_order_masks(rev)[1], kq[d][0:CH] * dmat[d], 0.0).astype(BF16)
            a_ref[idx] = a
            t_ref[idx] = (eye - (a * lm_ref[0]).astype(F32)).astype(BF16)
            qk_ref[idx] = (kq[d][CH:] * dmat[d]).astype(BF16)
        yield

    per_stage = SOLVE_LEVELS_PER_STAGE
    n_stages = (N_LEVELS - 1) // per_stage
    solve_stages = tuple((2 + s, tuple(range(1 + s * per_stage, 1 + (s + 1) * per_stage)))
                         for s in range(n_stages))
    ops_lag = n_stages + 2

    def step(it, stage_one, stage_two):
        fillers = []
        if stage_two:
            c2 = it - 1
            ok = jnp.logical_and(c2 >= 0, c2 < nc)
            cq = jnp.where(ok, c2, nc)
            fillers.append(prep_two(*tiles(it, 1), qkv_ref[cq], gate_ref[cq]))
        loaded = []
        for lag, lvs in solve_stages:
            jf, jb = tiles(it, lag)
            loaded.append((jf, jb, t_ref[jf], t_ref[jb], a_ref[jf], a_ref[jb]))
        ops_in = [(i, t_ref[i], rhs_ref[i], kd_ref[i], qk_ref[i], qd_ref[i]) for i in tiles(it, ops_lag)]
        if stage_one:
            fillers.insert(0, prep_one(it))

        def fill():
            for f in fillers:
                next(f, None)

        uws = [_dot(t, rhs) for i, t, rhs, kd, qkm, qd in ops_in]
        for half in range(per_stage):
            xs = []
            for (lag, lvs), (jf, jb, tf, tb, af, ab) in zip(solve_stages, loaded):
                m = lm_ref[lvs[half]]
                xs.append((_dot(tf, af * m), _dot(tb, ab * m)))
            fill()
            fill()
            ys = [jnp.concatenate([_dot(xf.astype(BF16), tf), _dot(xb.astype(BF16), tb)], axis=1)
                  for (xf, xb), (jf, jb, tf, tb, af, ab) in zip(xs, loaded)]
            if half == 0:
                uws = [uw.astype(BF16) for uw in uws]
                kuws = [_dot_tn(kd, uw) for uw, (i, t, rhs, kd, qkm, qd) in zip(uws, ops_in)]
                quws = [_dot(qkm, uw) for uw, (i, t, rhs, kd, qkm, qd) in zip(uws, ops_in)]
            fill()
            fill()
            loaded = [(jf, jb, tf - y[:, 0:CH].astype(BF16), tb - y[:, CH:].astype(BF16), af, ab)
                      for y, (jf, jb, tf, tb, af, ab) in zip(ys, loaded)]
        for f in fillers:
            for _ in f:
                pass
        for jf, jb, tf, tb, _, _ in loaded:
            t_ref[jf] = tf
            t_ref[jb] = tb
        for kuw, quw, (i, t, rhs, kd, qkm, qd) in zip(kuws, quws, ops_in):
            qc_ref[i] = kuw[:, 0:A_DV]
            mc_ref[i] = kuw[:, A_DV:].astype(BF16)
            oc_ref[i] = quw[:, 0:A_DV]
            rc_ref[i] = (qd - quw[:, A_DV:]).astype(BF16)

    def loop(lo, hi, stage_one, stage_two):
        def body(it, carry):
            step(it, stage_one, stage_two)
            return carry
        lax.fori_loop(lo, hi, body, 0)

    loop(0, nc, True, True)
    loop(nc, nc + 1, False, True)
    loop(nc + 1, nc + ops_lag, False, False)

    def scan(c, carry):
        sf, sb = carry
        cf = c
        cb = 2 * nc - 1 - c
        of = _dot(rc_ref[cf], sf.astype(BF16)) + oc_ref[cf]
        ob = _dot(rc_ref[cb], sb.astype(BF16)) + oc_ref[cb]
        oc_ref[cf] = of
        oc_ref[cb] = ob
        sf = gl_ref[cf][0:1, :] * sf - _dot(mc_ref[cf], sf.astype(BF16)) + qc_ref[cf]
        sb = gl_ref[cb][0:1, :] * sb - _dot(mc_ref[cb], sb.astype(BF16)) + qc_ref[cb]
        return sf, sb

    zero = jnp.zeros((A_DK, A_DV), F32)
    lax.fori_loop(0, nc, scan, (zero, zero))

    ng = ng_ref[...]
    fg = _group(nc, FIN_GROUP)

    def fin(it, carry):
        cs = [it * fg + u for u in range(fg)]
        rows = [pl.ds(pl.multiple_of(c * CH, CH), CH) for c in cs]
        os_ = [oc_ref[c] + oc_ref[nc + c] for c in cs]
        inv = [lax.rsqrt(jnp.mean(o * o, axis=-1, keepdims=True) + RMS_EPS) for o in os_]
        for c, r, o, s in zip(cs, rows, os_, inv):
            z = p_ref[pl.ds(pl.multiple_of(c * CH, CH) + HALO, CH), 3 * A_DK:4 * A_DK]
            o_ref[0, r, :] = (o * s * ng * _silu(z)).astype(BF16)
        return carry

    lax.fori_loop(0, nc // fg, fin, 0)


def _gdn_level_masks():
    idx = np.arange(CH)
    x = idx[:, None] ^ idx[None, :]
    return jnp.asarray(np.stack([(x >> lv) == 1 for lv in range(N_LEVELS)]), BF16)


def _conv_shift_matrices():
    t = np.arange(CH)[:, None]
    r = np.arange(CH + 2 * HALO)[None, :]
    taps = CONV_MXU_TAPS or (0,)
    return jnp.asarray(np.concatenate([r == t + HALO + i - A_CONV // 2 for i in taps], axis=0), BF16)


def _gdn_mixer(xb, w_in, conv_w, a_log, dt_bias, norm_g):
    bn, seq, dm = xb.shape
    h, dk = A_HEADS, A_DK
    nc = seq // CH
    w = w_in
    hw = h * dk
    ba = w[:, 4 * hw:].reshape(dm, 2, 2, h)
    per_head = [w[:, i * hw:(i + 1) * hw].reshape(dm, h, dk).transpose(1, 0, 2) for i in range(4)]
    gate_cols = jnp.pad(ba.reshape(dm, 4, h).transpose(2, 0, 1), ((0, 0), (0, 0), (0, dk - 4)))
    wh = jnp.concatenate(per_head + [gate_cols], axis=2).astype(BF16)
    cw = conv_w.reshape(A_CONV, 3, h, dk).transpose(2, 0, 1, 3).reshape(h, A_CONV, 3 * dk)
    cw = jnp.pad(cw, ((0, 0), (0, 8 - A_CONV), (0, 0))).astype(F32)
    scale = jnp.zeros((h, GATE_ROWS), F32).at[:, 2:4].set((jnp.exp(a_log.astype(F32)) * LOG2E).T)
    bias = jnp.zeros((h, GATE_ROWS), F32).at[:, 2:4].set(dt_bias.astype(F32).T)
    hp = jnp.broadcast_to(jnp.stack([scale, bias], axis=1)[:, :, :, None], (h, 2, GATE_ROWS, dk))
    ng = norm_g.astype(F32).reshape(1, A_DV)
    lm = _gdn_level_masks()
    sh = _conv_shift_matrices()
    nw = wh.shape[2]

    kern = functools.partial(_gdn_kernel, seq=seq)
    tile = lambda dt: pltpu.VMEM((2 * nc + 1, CH, CH), dt)
    return pl.pallas_call(
        kern,
        out_shape=jax.ShapeDtypeStruct((bn, seq, h * A_DV), BF16),
        grid=(bn, h),
        in_specs=[
            pl.BlockSpec((1, seq, dm), lambda b, i: (b, 0, 0)),
            pl.BlockSpec((1, dm, nw), lambda b, i: (i, 0, 0)),
            pl.BlockSpec((1, 8, 3 * dk), lambda b, i: (i, 0, 0)),
            pl.BlockSpec((1, 2, GATE_ROWS, dk), lambda b, i: (i, 0, 0, 0)),
            pl.BlockSpec((1, A_DV), lambda b, i: (0, 0)),
            pl.BlockSpec(lm.shape, lambda b, i: (0, 0, 0)),
            pl.BlockSpec(sh.shape, lambda b, i: (0, 0)),
        ],
        out_specs=pl.BlockSpec((1, seq, A_DV), lambda b, i: (b, 0, i)),
        scratch_shapes=[
            pltpu.VMEM((seq + 2 * HALO, nw), F32),
            pltpu.VMEM((nc + 1, CH, 3 * A_DK), F32),
            pltpu.VMEM((nc + 1, 3 * GATE_ROWS, CH), F32),
            tile(BF16),
            tile(BF16),
            tile(BF16),
            pltpu.VMEM((2 * nc + 1, CH, A_DV + A_DK), BF16),
            tile(F32),
            tile(BF16),
            pltpu.VMEM((2 * nc + 1, 8, A_DK), F32),
            tile(BF16),
            tile(F32),
            tile(BF16),
            tile(F32),
        ],
        compiler_params=pltpu.CompilerParams(
            dimension_semantics=("arbitrary", "arbitrary"), vmem_limit_bytes=VMEM_LIMIT),
        name="gdn_mixer",
    )(xb, wh, cw, hp, ng, lm, sh)


def _gla_tables():
    i = np.arange(CH)[:, None]
    t = np.arange(CH)[None, :]
    seg = np.zeros((2, N_LEVELS + 1, CH, CH), np.float32)
    lvl = np.zeros((2, CH, CH), np.int32)
    for d in range(2):
        rev = d == 1
        seg[d, 0] = (t >= i) if rev else (t <= i)
        lv = np.full((CH, CH), N_LEVELS + 1, np.int32)
        lv[np.arange(CH), np.arange(CH)] = N_LEVELS
        x = i ^ t
        for l in range(N_LEVELS):
            h = 2 ** (N_LEVELS - 1 - l)
            b0 = (i // (2 * h)) * (2 * h)
            if rev:
                r = b0 + h
                late = i < r
                m = np.where(late, (t >= i) & (t < r), (t >= r) & (t < i))
                own = ((x >> (N_LEVELS - 1 - l)) == 1) & (t > i)
            else:
                r = b0 + h - 1
                late = i > r
                m = np.where(late, (t > r) & (t <= i), (t > i) & (t <= r))
                own = ((x >> (N_LEVELS - 1 - l)) == 1) & (t < i)
            seg[d, 1 + l] = m
            lv[own] = l
        lvl[d] = lv
    return jnp.asarray(seg, BF16), jnp.asarray(lvl)


def _gla_kernel(xb_ref, wh_ref, w2_ref, gb_ref, ng_ref, seg_ref, lvl_ref, o_ref,
                p_ref, qs_ref, kv_ref, st_ref, dec_ref, oi_ref, *, seq):
    nc = seq // CH
    dk, dv = B_DK, B_DV
    p_ref[...] = _dot(xb_ref[0], wh_ref[0])

    c_q, c_k, c_v, c_r, c_g = 0, dk, 2 * dk, 2 * dk + dv, 2 * dk + 2 * dv
    gg = _group(nc, GLA_GROUP)
    lanes = [(u, d) for u in range(gg) for d in range(2)]

    def prep(it, carry):
        cs = [it * gg + u for u in range(gg)]
        rows = [pl.ds(pl.multiple_of(c * CH, CH), CH) for c in cs]
        q = [p_ref[r, c_q:c_q + dk] * (dk ** -0.5) for r in rows]
        k = [p_ref[r, c_k:c_k + dk] for r in rows]
        q16 = [x.astype(BF16) for x in q]
        k16 = [x.astype(BF16) for x in k]
        v16 = [p_ref[r, c_v:c_v + dv].astype(BF16) for r in rows]
        gin = [p_ref[r, c_g:c_g + dk].astype(BF16) for r in rows]
        logit = [_dot(gin[u], w2_ref[0, d]) + gb_ref[0, d][0:1, :] for u, d in lanes]
        la3 = [_split(-_softplus(-x) * (LOG2E / B_TAU), 3) for x in logit]
        la2 = [y[:, 0:GLA_PIECES * dk] for y in la3]
        bc = [_fold(_dot(seg_ref[d, 0], y), 3) for (u, d), y in zip(lanes, la3)]

        def level_sums(l):
            h = CH >> (l + 1)
            if h < HALO:
                return [_fold(_dot(seg_ref[d, 1 + l], y), GLA_PIECES) for (u, d), y in zip(lanes, la2)]
            out = []
            for (u, d), b in zip(lanes, bc):
                blocks = []
                for lo in range(0, CH, 2 * h):
                    if d == 1:
                        ref = b[lo + h:lo + h + 1, :]
                        blocks += [b[lo:lo + h, :] - ref, ref - b[lo + h:lo + 2 * h, :]]
                    else:
                        ref = b[lo + h - 1:lo + h, :]
                        blocks += [ref - b[lo:lo + h, :], b[lo + h:lo + 2 * h, :] - ref]
                out.append(jnp.concatenate(blocks, axis=0))
            return out

        half = CH // 2
        zero_half = jnp.zeros((half, dk), BF16)

        def top_level(x, e, d, late):
            upper = (d == 1) != late
            rows = slice(half, CH) if upper else slice(0, half)
            kept = (x[rows] * e[rows]).astype(BF16)
            return jnp.concatenate([zero_half, kept] if upper else [kept, zero_half], axis=0)

        ahead = [level_sums(l) for l in range(GLA_AHEAD)]
        scores = prod = None
        for l in range(N_LEVELS):
            if l + GLA_AHEAD < N_LEVELS:
                ahead.append(level_sums(l + GLA_AHEAD))
            e = [jnp.exp2(x) for x in ahead[l]]
            if l == 0:
                ql = [top_level(q[u], x, d, True) for (u, d), x in zip(lanes, e)]
                kl = [top_level(k[u], x, d, False) for (u, d), x in zip(lanes, e)]
            else:
                ql = [(q[u] * x).astype(BF16) for (u, d), x in zip(lanes, e)]
                kl = [(k[u] * x).astype(BF16) for (u, d), x in zip(lanes, e)]
            if l == 1:
                scores = prod
            elif l > 1:
                own = [lvl_ref[d] == l - 1 for d in range(2)]
                scores = [jnp.where(own[d], p, s) for (u, d), p, s in zip(lanes, prod, scores)]
            prod = [_dot_nt(a, b) for a, b in zip(ql, kl)]
        own = [lvl_ref[d] == N_LEVELS - 1 for d in range(2)]
        scores = [jnp.where(own[d], p, s) for (u, d), p, s in zip(lanes, prod, scores)]
        diag = [_dot_nt(q16[u], k16[u]) for u in range(gg)]
        own = [lvl_ref[d] == N_LEVELS for d in range(2)]
        scores = [jnp.where(own[d], diag[u], s) for (u, d), s in zip(lanes, scores)]
        for (u, d), s, b in zip(lanes, scores, bc):
            idx = d * nc + cs[u]
            btot = b[0:1, :] if d == 1 else b[CH - 1:CH, :]
            oi_ref[idx] = _dot(s.astype(BF16), v16[u])
            qs_ref[idx] = (q[u] * jnp.exp2(b)).astype(BF16)
            kv_ref[idx] = _dot_tn((k[u] * jnp.exp2(btot - b)).astype(BF16), v16[u])
            dec_ref[idx] = jnp.exp2(jnp.broadcast_to(btot, (CH, dk)).T)
        return carry

    lax.fori_loop(0, nc // gg, prep, 0)

    def scan(c, carry):
        sf, sb = carry
        cf = c
        ib = 2 * nc - 1 - c
        st_ref[cf] = sf.astype(BF16)
        st_ref[ib] = sb.astype(BF16)
        ef = dec_ref[cf]
        eb = dec_ref[ib]
        sf = sf * jnp.concatenate([ef, ef], axis=1) + kv_ref[cf]
        sb = sb * jnp.concatenate([eb, eb], axis=1) + kv_ref[ib]
        return sf, sb

    zero = jnp.zeros((dk, dv), F32)
    lax.fori_loop(0, nc, scan, (zero, zero))

    ng = ng_ref[...]
    fg = _group(nc, FIN_GROUP)

    def fin(it, carry):
        cs = [it * fg + u for u in range(fg)]
        rows = [pl.ds(pl.multiple_of(c * CH, CH), CH) for c in cs]
        os_ = [oi_ref[c] + oi_ref[nc + c] + _dot(qs_ref[c], st_ref[c]) + _dot(qs_ref[nc + c], st_ref[nc + c])
               for c in cs]
        inv = [lax.rsqrt(jnp.mean(o * o, axis=-1, keepdims=True) + RMS_EPS) for o in os_]
        for r, o, s in zip(rows, os_, inv):
            o_ref[0, r, :] = (o * s * ng * _silu(p_ref[r, c_r:c_r + dv])).astype(BF16)
        return carry

    lax.fori_loop(0, nc // fg, fin, 0)


def _gla_mixer(xb, w_in, gate_w2, gate_b, norm_g):
    bn, seq, dm = xb.shape
    h, dk, dv = B_HEADS, B_DK, B_DV
    nc = seq // CH
    kw, vw = h * dk, h * dv
    w = w_in

    def heads(cols, width):
        return cols.reshape(dm, h, width).transpose(1, 0, 2)

    gl = jnp.pad(w[:, 2 * kw + 2 * vw:], ((0, 0), (0, dk - 2 * B_RANK)))
    wh = jnp.concatenate([
        heads(w[:, 0:kw], dk), heads(w[:, kw:2 * kw], dk),
        heads(w[:, 2 * kw:2 * kw + vw], dv), heads(w[:, 2 * kw + vw:2 * kw + 2 * vw], dv),
        jnp.broadcast_to(gl[None], (h, dm, dk))], axis=2).astype(BF16)
    w2 = gate_w2.reshape(2, B_RANK, h, dk).transpose(2, 0, 1, 3)
    w2p = jnp.zeros((h, 2, dk, dk), F32)
    w2p = w2p.at[:, 0, 0:B_RANK].set(w2[:, 0]).at[:, 1, B_RANK:2 * B_RANK].set(w2[:, 1]).astype(BF16)
    gb = gate_b.reshape(2, h, dk).transpose(1, 0, 2).astype(F32)
    gb = jnp.broadcast_to(gb[:, :, None, :], (h, 2, 8, dk))
    ng = norm_g.astype(F32).reshape(1, dv)
    seg, lvl = _gla_tables()
    nw = wh.shape[2]

    kern = functools.partial(_gla_kernel, seq=seq)
    return pl.pallas_call(
        kern,
        out_shape=jax.ShapeDtypeStruct((bn, seq, vw), BF16),
        grid=(bn, h),
        in_specs=[
            pl.BlockSpec((1, seq, dm), lambda b, i: (b, 0, 0)),
            pl.BlockSpec((1, dm, nw), lambda b, i: (i, 0, 0)),
            pl.BlockSpec((1, 2, dk, dk), lambda b, i: (i, 0, 0, 0)),
            pl.BlockSpec((1, 2, 8, dk), lambda b, i: (i, 0, 0, 0)),
            pl.BlockSpec((1, dv), lambda b, i: (0, 0)),
            pl.BlockSpec(seg.shape, lambda b, i: (0, 0, 0, 0)),
            pl.BlockSpec(lvl.shape, lambda b, i: (0, 0, 0)),
        ],
        out_specs=pl.BlockSpec((1, seq, dv), lambda b, i: (b, 0, i)),
        scratch_shapes=[
            pltpu.VMEM((seq, nw), F32),
            pltpu.VMEM((2 * nc, CH, dk), BF16),
            pltpu.VMEM((2 * nc, dk, dv), F32),
            pltpu.VMEM((2 * nc, dk, dv), BF16),
            pltpu.VMEM((2 * nc, dk, dk), F32),
            pltpu.VMEM((2 * nc, CH, dv), F32),
        ],
        compiler_params=pltpu.CompilerParams(
            dimension_semantics=("arbitrary", "arbitrary"), vmem_limit_bytes=VMEM_LIMIT),
        name="gla_mixer",
    )(xb, wh, w2p, gb, ng, seg, lvl)


def _post_kernel(o_ref, x_ref, wo_ref, w1_ref, w2_ref, ln_ref, y_ref, yb_ref, *, alpha):
    ln = ln_ref[...]
    x = x_ref[...]
    x1 = _layernorm(alpha * x + _dot(o_ref[...], wo_ref[...]), ln[0:1, :], ln[1:2, :])
    x1b = x1.astype(BF16)
    acc = jnp.zeros(x.shape, F32)
    dff = w1_ref.shape[1]
    for j in range(dff // FF_TILE):
        cols = slice(j * FF_TILE, (j + 1) * FF_TILE)
        hcur = jnp.maximum(_dot(x1b, w1_ref[:, cols]), 0.0)
        acc = acc + _dot((hcur * hcur).astype(BF16), w2_ref[cols, :])
    y = _layernorm(alpha * x1 + acc, ln[2:3, :], ln[3:4, :])
    y_ref[...] = y
    yb_ref[...] = y.astype(BF16)


def _post(o, x, w_out, w1, w2, g1, b1, g2, b2, alpha):
    t, dm = x.shape
    vw = o.shape[1]
    dff = w1.shape[1]
    tm = min(ROW_TILE, t)
    ln = jnp.pad(jnp.stack([g1, b1, g2, b2]).astype(F32), ((0, 4), (0, 0)))
    const = lambda shape: pl.BlockSpec(shape, lambda i: (0, 0), pipeline_mode=pl.Buffered(1))
    return pl.pallas_call(
        functools.partial(_post_kernel, alpha=alpha),
        out_shape=(jax.ShapeDtypeStruct((t, dm), F32), jax.ShapeDtypeStruct((t, dm), BF16)),
        grid=(t // tm,),
        in_specs=[
            pl.BlockSpec((tm, vw), lambda i: (i, 0)),
            pl.BlockSpec((tm, dm), lambda i: (i, 0)),
            const((vw, dm)), const((dm, dff)), const((dff, dm)), const((8, dm)),
        ],
        out_specs=(pl.BlockSpec((tm, dm), lambda i: (i, 0)), pl.BlockSpec((tm, dm), lambda i: (i, 0))),
        compiler_params=pltpu.CompilerParams(
            dimension_semantics=("arbitrary",), vmem_limit_bytes=VMEM_LIMIT),
        name="post",
    )(o, x, w_out.astype(BF16), w1.astype(BF16), w2.astype(BF16), ln)


def kernel(x, a_w_in, a_conv, a_alog, a_dt_bias, a_norm_g, a_w_out, b_w_in, b_gate_w2, b_gate_b,
           b_norm_g, b_w_out, ln1_g, ln1_b, mlp_w1, mlp_w2, ln2_g, ln2_b):
    bn, seq, dm = x.shape
    depth = ln1_g.shape[0]
    alpha = (2 * depth) ** 0.25
    xf = x.astype(F32).reshape(bn * seq, dm)
    xb = xf.astype(BF16)
    for i in range(depth):
        j = i // 2
        xb3 = xb.reshape(bn, seq, dm)
        if i % 2 == 0:
            o = _gdn_mixer(xb3, a_w_in[j], a_conv[j], a_alog[j], a_dt_bias[j], a_norm_g[j])
            w_out = a_w_out[j]
        else:
            o = _gla_mixer(xb3, b_w_in[j], b_gate_w2[j], b_gate_b[j], b_norm_g[j])
            w_out = b_w_out[j]
        xf, xb = _post(o.reshape(bn * seq, -1), xf, w_out, mlp_w1[i], mlp_w2[i],
                       ln1_g[i], ln1_b[i], ln2_g[i], ln2_b[i], alpha)
    return xf.reshape(bn, seq, dm).astype(x.dtype)
```

```python
import functools
import math

import numpy as np

import jax
import jax.numpy as jnp
from jax import lax
from jax.experimental import pallas as pl
from jax.experimental.pallas import tpu as pltpu

F32 = jnp.float32
BF16 = jnp.bfloat16

A_HEADS, A_DK, A_DV, A_CONV = 8, 128, 128, 5
B_HEADS, B_DK, B_DV, B_RANK, B_TAU = 4, 128, 256, 16, 16.0
LN_EPS, RMS_EPS, L2_EPS = 1e-5, 1e-6, 1e-6

CH = 128
N_LEVELS = 7
HALO = 8
CONV_MXU_TAPS = (0, 4)
GATE_ROWS = 16
SOLVE_LEVELS_PER_STAGE = 1
GLA_GROUP = 8
GLA_AHEAD = 2
GLA_PIECES = 2
FIN_GROUP = 16
NEG_BIG = -1e30
LOG2E = math.log2(math.e)
VMEM_LIMIT = 56 * 1024 * 1024
ROW_TILE = 1024
FF_TILE = 1024

assert CH == A_DK == B_DK and 2 ** N_LEVELS == CH


def _dot(a, b):
    return jnp.dot(a, b, preferred_element_type=F32)


def _dot_nt(a, b):
    return lax.dot_general(a, b, (((1,), (1,)), ((), ())), preferred_element_type=F32)


def _dot_tn(a, b):
    return lax.dot_general(a, b, (((0,), (0,)), ((), ())), preferred_element_type=F32)


def _split(x, n, axis=1):
    pieces = []
    for _ in range(n - 1):
        p = x.astype(BF16)
        pieces.append(p)
        x = x - p.astype(F32)
    pieces.append(x.astype(BF16))
    return jnp.concatenate(pieces, axis=axis)


def _fold(y, n, axis=1):
    w = y.shape[axis] // n
    blocks = [lax.slice_in_dim(y, i * w, (i + 1) * w, axis=axis) for i in range(n)]
    out = blocks[0]
    for b in blocks[1:]:
        out = out + b
    return out


def _sigmoid(x):
    return 0.5 + 0.5 * jnp.tanh(0.5 * x)


def _silu(x):
    h = 0.5 * x
    return h + h * jnp.tanh(h)


def _softplus(x):
    return jnp.maximum(x, 0.0) + jnp.log(1.0 + jnp.exp(-jnp.abs(x)))


def _layernorm(y, g, b):
    mu = jnp.mean(y, axis=-1, keepdims=True)
    yc = y - mu
    var = jnp.mean(yc * yc, axis=-1, keepdims=True)
    return yc * lax.rsqrt(var + LN_EPS) * g + b


def _order_masks(rev):
    row = lax.broadcasted_iota(jnp.int32, (CH, CH), 0)
    col = lax.broadcasted_iota(jnp.int32, (CH, CH), 1)
    if rev:
        return col >= row, col > row
    return col <= row, col < row


def _group(n, want):
    return math.gcd(n, want)


def _gdn_kernel(xb_ref, wh_ref, cw_ref, hp_ref, ng_ref, lm_ref, sh_ref, o_ref,
                p_ref, qkv_ref, gate_ref, a_ref, t_ref, qk_ref, rhs_ref, qd_ref, kd_ref, gl_ref,
                mc_ref, qc_ref, rc_ref, oc_ref, *, seq):
    nc = seq // CH
    spare = 2 * nc

    p_ref[0:HALO, :] = jnp.zeros((HALO, p_ref.shape[1]), F32)
    p_ref[HALO + seq:, :] = jnp.zeros((HALO, p_ref.shape[1]), F32)
    p_ref[HALO:HALO + seq, :] = _dot(xb_ref[0], wh_ref[0])
    for ref in (a_ref, t_ref, qk_ref, kd_ref, rhs_ref, qd_ref):
        ref[spare] = jnp.zeros(ref.shape[1:], ref.dtype)
    qkv_ref[nc] = jnp.zeros(qkv_ref.shape[1:], F32)
    gate_ref[nc] = jnp.zeros(gate_ref.shape[1:], F32)

    cw = cw_ref[0]
    hp = hp_ref[0]

    def tiles(it, lag):
        c = it - lag
        ok = jnp.logical_and(c >= 0, c < nc)
        return jnp.where(ok, c, spare), jnp.where(ok, nc + c, spare)

    def prep_one(c):
        base = pl.multiple_of(c * CH, CH)
        win = p_ref[pl.ds(base, CH + 2 * HALO), 0:3 * A_DK]
        gates = p_ref[pl.ds(base + HALO, CH), 4 * A_DK:5 * A_DK].T[0:GATE_ROWS, :]
        if CONV_MXU_TAPS:
            shifted = _dot(sh_ref[...], win.astype(BF16))
        mid = A_CONV // 2
        acc = win[HALO:HALO + CH, :] * cw[mid:mid + 1, :]
        for i in range(A_CONV):
            if i != mid and i not in CONV_MXU_TAPS:
                off = HALO + i - mid
                acc = acc + win[off:off + CH, :] * cw[i:i + 1, :]
        g_rows = -hp[0] * _softplus(gates + hp[1])
        pieces = _split(g_rows, 3, axis=0)
        gcum = [_fold(_dot(pieces, _order_masks(not rev)[0].astype(BF16)), 3, axis=0) for rev in (False, True)]
        yield
        for n, i in enumerate(CONV_MXU_TAPS):
            acc = acc + shifted[n * CH:(n + 1) * CH, :] * cw[i:i + 1, :]
        yield
        s = _silu(acc)
        q = s[:, 0:A_DK]
        k = s[:, A_DK:2 * A_DK]
        q = q * (lax.rsqrt(jnp.sum(q * q, axis=-1, keepdims=True) + L2_EPS) * (A_DK ** -0.5))
        k = k * lax.rsqrt(jnp.sum(k * k, axis=-1, keepdims=True) + L2_EPS)
        yield
        qkv_ref[c] = jnp.concatenate([q, k, s[:, 2 * A_DK:]], axis=1)
        gate_ref[c] = jnp.concatenate([_sigmoid(gates), gcum[0], gcum[1]], axis=0)
        yield

    def prep_two(jf, jb, qkv, gate):
        q = qkv[:, 0:A_DK]
        k = qkv[:, A_DK:2 * A_DK]
        v = qkv[:, 2 * A_DK:]
        dirs = ((0, False, jf), (1, True, jb))
        gr = [jnp.broadcast_to(gate[(1 + d) * GATE_ROWS + 2 + d:(1 + d) * GATE_ROWS + 3 + d, :], (CH, CH))
              for d, _, _ in dirs]
        gc = [x.T for x in gr]
        beta = [jnp.broadcast_to(gate[d:d + 1, :], (CH, CH)).T for d, _, _ in dirs]
        kb16 = k.astype(BF16)
        eye = (lax.broadcasted_iota(jnp.int32, (CH, CH), 0)
               == lax.broadcasted_iota(jnp.int32, (CH, CH), 1)).astype(F32)
        yield
        kbeta = [k * beta[d] for d, _, _ in dirs]
        kq = [_dot_nt(jnp.concatenate([kbeta[d], q], axis=0).astype(BF16), kb16) for d, _, _ in dirs]
        gtot = [gc[d][0:1, :] if rev else gc[d][CH - 1:CH, :] for d, rev, _ in dirs]
        eg = [jnp.exp2(gc[d]) for d, _, _ in dirs]
        dmat = [jnp.exp2(jnp.where(_order_masks(rev)[0], gc[d] - gr[d], NEG_BIG)) for d, rev, _ in dirs]
        yield
        for d, rev, idx in dirs:
            rhs_ref[idx] = jnp.concatenate([v * beta[d], kbeta[d] * eg[d]], axis=1).astype(BF16)
            qd_ref[idx] = q * eg[d]
            kd_ref[idx] = (k * jnp.exp2(gtot[d] - gc[d])).astype(BF16)
            gl_ref[idx] = jnp.broadcast_to(jnp.exp2(gtot[d]), (8, A_DK))
        yield
        for d, rev, idx in dirs:
            a = jnp.where(_order_masks(rev)[1], kq[d][0:CH] * dmat[d], 0.0).astype(BF16)
            a_ref[idx] = a
            t_ref[idx] = (eye - (a * lm_ref[0]).astype(F32)).astype(BF16)
            qk_ref[idx] = (kq[d][CH:] * dmat[d]).astype(BF16)
        yield

    per_stage = SOLVE_LEVELS_PER_STAGE
    n_stages = (N_LEVELS - 1) // per_stage
    solve_stages = tuple((2 + s, tuple(range(1 + s * per_stage, 1 + (s + 1) * per_stage)))
                         for s in range(n_stages))
    ops_lag = n_stages + 2

    def step(it, stage_one, stage_two):
        fillers = []
        if stage_two:
            c2 = it - 1
            ok = jnp.logical_and(c2 >= 0, c2 < nc)
            cq = jnp.where(ok, c2, nc)
            fillers.append(prep_two(*tiles(it, 1), qkv_ref[cq], gate_ref[cq]))
        loaded = []
        for lag, lvs in solve_stages:
            jf, jb = tiles(it, lag)
            loaded.append((jf, jb, t_ref[jf], t_ref[jb], a_ref[jf], a_ref[jb]))
        ops_in = [(i, t_ref[i], rhs_ref[i], kd_ref[i], qk_ref[i], qd_ref[i]) for i in tiles(it, ops_lag)]
        if stage_one:
            fillers.insert(0, prep_one(it))

        def fill():
            for f in fillers:
                next(f, None)

        uws = [_dot(t, rhs) for i, t, rhs, kd, qkm, qd in ops_in]
        for half in range(per_stage):
            xs = []
            for (lag, lvs), (jf, jb, tf, tb, af, ab) in zip(solve_stages, loaded):
                m = lm_ref[lvs[half]]
                xs.append((_dot(tf, af * m), _dot(tb, ab * m)))
            fill()
            fill()
            ys = [jnp.concatenate([_dot(xf.astype(BF16), tf), _dot(xb.astype(BF16), tb)], axis=1)
                  for (xf, xb), (jf, jb, tf, tb, af, ab) in zip(xs, loaded)]
            if half == 0:
                uws = [uw.astype(BF16) for uw in uws]
                kuws = [_dot_tn(kd, uw) for uw, (i, t, rhs, kd, qkm, qd) in zip(uws, ops_in)]
                quws = [_dot(qkm, uw) for uw, (i, t, rhs, kd, qkm, qd) in zip(uws, ops_in)]
            fill()
            fill()
            loaded = [(jf, jb, tf - y[:, 0:CH].astype(BF16), tb - y[:, CH:].astype(BF16), af, ab)
                      for y, (jf, jb, tf, tb, af, ab) in zip(ys, loaded)]
        for f in fillers:
            for _ in f:
                pass
        for jf, jb, tf, tb, _, _ in loaded:
            t_ref[jf] = tf
            t_ref[jb] = tb
        for kuw, quw, (i, t, rhs, kd, qkm, qd) in zip(kuws, quws, ops_in):
            qc_ref[i] = kuw[:, 0:A_DV]
            mc_ref[i] = kuw[:, A_DV:].astype(BF16)
            oc_ref[i] = quw[:, 0:A_DV]
            rc_ref[i] = (qd - quw[:, A_DV:]).astype(BF16)

    def loop(lo, hi, stage_one, stage_two):
        def body(it, carry):
            step(it, stage_one, stage_two)
            return carry
        lax.fori_loop(lo, hi, body, 0)

    loop(0, nc, True, True)
    loop(nc, nc + 1, False, True)
    loop(nc + 1, nc + ops_lag, False, False)

    def scan(c, carry):
        sf, sb = carry
        cf = c
        cb = 2 * nc - 1 - c
        of = _dot(rc_ref[cf], sf.astype(BF16)) + oc_ref[cf]
        ob = _dot(rc_ref[cb], sb.astype(BF16)) + oc_ref[cb]
        oc_ref[cf] = of
        oc_ref[cb] = ob
        sf = gl_ref[cf][0:1, :] * sf - _dot(mc_ref[cf], sf.astype(BF16)) + qc_ref[cf]
        sb = gl_ref[cb][0:1, :] * sb - _dot(mc_ref[cb], sb.astype(BF16)) + qc_ref[cb]
        return sf, sb

    zero = jnp.zeros((A_DK, A_DV), F32)
    lax.fori_loop(0, nc, scan, (zero, zero), unroll=2)

    ng = ng_ref[...]
    fg = _group(nc, FIN_GROUP)

    def fin(it, carry):
        cs = [it * fg + u for u in range(fg)]
        rows = [pl.ds(pl.multiple_of(c * CH, CH), CH) for c in cs]
        os_ = [oc_ref[c] + oc_ref[nc + c] for c in cs]
        inv = [lax.rsqrt(jnp.mean(o * o, axis=-1, keepdims=True) + RMS_EPS) for o in os_]
        for c, r, o, s in zip(cs, rows, os_, inv):
            z = p_ref[pl.ds(pl.multiple_of(c * CH, CH) + HALO, CH), 3 * A_DK:4 * A_DK]
            o_ref[0, r, :] = (o * s * ng * _silu(z)).astype(BF16)
        return carry

    lax.fori_loop(0, nc // fg, fin, 0)


def _gdn_level_masks():
    idx = np.arange(CH)
    x = idx[:, None] ^ idx[None, :]
    return jnp.asarray(np.stack([(x >> lv) == 1 for lv in range(N_LEVELS)]), BF16)


def _conv_shift_matrices():
    t = np.arange(CH)[:, None]
    r = np.arange(CH + 2 * HALO)[None, :]
    taps = CONV_MXU_TAPS or (0,)
    return jnp.asarray(np.concatenate([r == t + HALO + i - A_CONV // 2 for i in taps], axis=0), BF16)


def _gdn_mixer(xb, w_in, conv_w, a_log, dt_bias, norm_g):
    bn, seq, dm = xb.shape
    h, dk = A_HEADS, A_DK
    nc = seq // CH
    w = w_in
    hw = h * dk
    ba = w[:, 4 * hw:].reshape(dm, 2, 2, h)
    per_head = [w[:, i * hw:(i + 1) * hw].reshape(dm, h, dk).transpose(1, 0, 2) for i in range(4)]
    gate_cols = jnp.pad(ba.reshape(dm, 4, h).transpose(2, 0, 1), ((0, 0), (0, 0), (0, dk - 4)))
    wh = jnp.concatenate(per_head + [gate_cols], axis=2).astype(BF16)
    cw = conv_w.reshape(A_CONV, 3, h, dk).transpose(2, 0, 1, 3).reshape(h, A_CONV, 3 * dk)
    cw = jnp.pad(cw, ((0, 0), (0, 8 - A_CONV), (0, 0))).astype(F32)
    scale = jnp.zeros((h, GATE_ROWS), F32).at[:, 2:4].set((jnp.exp(a_log.astype(F32)) * LOG2E).T)
    bias = jnp.zeros((h, GATE_ROWS), F32).at[:, 2:4].set(dt_bias.astype(F32).T)
    hp = jnp.broadcast_to(jnp.stack([scale, bias], axis=1)[:, :, :, None], (h, 2, GATE_ROWS, dk))
    ng = norm_g.astype(F32).reshape(1, A_DV)
    lm = _gdn_level_masks()
    sh = _conv_shift_matrices()
    nw = wh.shape[2]

    kern = functools.partial(_gdn_kernel, seq=seq)
    tile = lambda dt: pltpu.VMEM((2 * nc + 1, CH, CH), dt)
    return pl.pallas_call(
        kern,
        out_shape=jax.ShapeDtypeStruct((bn, seq, h * A_DV), BF16),
        grid=(bn, h),
        in_specs=[
            pl.BlockSpec((1, seq, dm), lambda b, i: (b, 0, 0)),
            pl.BlockSpec((1, dm, nw), lambda b, i: (i, 0, 0)),
            pl.BlockSpec((1, 8, 3 * dk), lambda b, i: (i, 0, 0)),
            pl.BlockSpec((1, 2, GATE_ROWS, dk), lambda b, i: (i, 0, 0, 0)),
            pl.BlockSpec((1, A_DV), lambda b, i: (0, 0)),
            pl.BlockSpec(lm.shape, lambda b, i: (0, 0, 0)),
            pl.BlockSpec(sh.shape, lambda b, i: (0, 0)),
        ],
        out_specs=pl.BlockSpec((1, seq, A_DV), lambda b, i: (b, 0, i)),
        scratch_shapes=[
            pltpu.VMEM((seq + 2 * HALO, nw), F32),
            pltpu.VMEM((nc + 1, CH, 3 * A_DK), F32),
            pltpu.VMEM((nc + 1, 3 * GATE_ROWS, CH), F32),
            tile(BF16),
            tile(BF16),
            tile(BF16),
            pltpu.VMEM((2 * nc + 1, CH, A_DV + A_DK), BF16),
            tile(F32),
            tile(BF16),
            pltpu.VMEM((2 * nc + 1, 8, A_DK), F32),
            tile(BF16),
            tile(F32),
            tile(BF16),
            tile(F32),
        ],
        compiler_params=pltpu.CompilerParams(
            dimension_semantics=("arbitrary", "arbitrary"), vmem_limit_bytes=VMEM_LIMIT),
        name="gdn_mixer",
    )(xb, wh, cw, hp, ng, lm, sh)


def _gla_tables():
    i = np.arange(CH)[:, None]
    t = np.arange(CH)[None, :]
    seg = np.zeros((2, N_LEVELS + 1, CH, CH), np.float32)
    lvl = np.zeros((2, CH, CH), np.int32)
    for d in range(2):
        rev = d == 1
        seg[d, 0] = (t >= i) if rev else (t <= i)
        lv = np.full((CH, CH), N_LEVELS + 1, np.int32)
        lv[np.arange(CH), np.arange(CH)] = N_LEVELS
        x = i ^ t
        for l in range(N_LEVELS):
            h = 2 ** (N_LEVELS - 1 - l)
            b0 = (i // (2 * h)) * (2 * h)
            if rev:
                r = b0 + h
                late = i < r
                m = np.where(late, (t >= i) & (t < r), (t >= r) & (t < i))
                own = ((x >> (N_LEVELS - 1 - l)) == 1) & (t > i)
            else:
                r = b0 + h - 1
                late = i > r
                m = np.where(late, (t > r) & (t <= i), (t > i) & (t <= r))
                own = ((x >> (N_LEVELS - 1 - l)) == 1) & (t < i)
            seg[d, 1 + l] = m
            lv[own] = l
        lvl[d] = lv
    return jnp.asarray(seg, BF16), jnp.asarray(lvl)


def _gla_kernel(xb_ref, wh_ref, w2_ref, gb_ref, ng_ref, seg_ref, lvl_ref, o_ref,
                p_ref, qs_ref, kv_ref, st_ref, dec_ref, oi_ref, *, seq):
    nc = seq // CH
    dk, dv = B_DK, B_DV
    p_ref[...] = _dot(xb_ref[0], wh_ref[0])

    c_q, c_k, c_v, c_r, c_g = 0, dk, 2 * dk, 2 * dk + dv, 2 * dk + 2 * dv
    gg = _group(nc, GLA_GROUP)
    lanes = [(u, d) for u in range(gg) for d in range(2)]

    def prep(it, carry):
        cs = [it * gg + u for u in range(gg)]
        rows = [pl.ds(pl.multiple_of(c * CH, CH), CH) for c in cs]
        q = [p_ref[r, c_q:c_q + dk] * (dk ** -0.5) for r in rows]
        k = [p_ref[r, c_k:c_k + dk] for r in rows]
        q16 = [x.astype(BF16) for x in q]
        k16 = [x.astype(BF16) for x in k]
        v16 = [p_ref[r, c_v:c_v + dv].astype(BF16) for r in rows]
        gin = [p_ref[r, c_g:c_g + dk].astype(BF16) for r in rows]
        logit = [_dot(gin[u], w2_ref[0, d]) + gb_ref[0, d][0:1, :] for u, d in lanes]
        la3 = [_split(-_softplus(-x) * (LOG2E / B_TAU), 3) for x in logit]
        la2 = [y[:, 0:GLA_PIECES * dk] for y in la3]
        bc = [_fold(_dot(seg_ref[d, 0], y), 3) for (u, d), y in zip(lanes, la3)]

        def level_sums(l):
            h = CH >> (l + 1)
            if h < HALO:
                return [_fold(_dot(seg_ref[d, 1 + l], y), GLA_PIECES) for (u, d), y in zip(lanes, la2)]
            out = []
            for (u, d), b in zip(lanes, bc):
                blocks = []
                for lo in range(0, CH, 2 * h):
                    if d == 1:
                        ref = b[lo + h:lo + h + 1, :]
                        blocks += [b[lo:lo + h, :] - ref, ref - b[lo + h:lo + 2 * h, :]]
                    else:
                        ref = b[lo + h - 1:lo + h, :]
                        blocks += [ref - b[lo:lo + h, :], b[lo + h:lo + 2 * h, :] - ref]
                out.append(jnp.concatenate(blocks, axis=0))
            return out

        half = CH // 2
        zero_half = jnp.zeros((half, dk), BF16)

        def top_level(x, e, d, late):
            upper = (d == 1) != late
            rows = slice(half, CH) if upper else slice(0, half)
            kept = (x[rows] * e[rows]).astype(BF16)
            return jnp.concatenate([zero_half, kept] if upper else [kept, zero_half], axis=0)

        ahead = [level_sums(l) for l in range(GLA_AHEAD)]
        scores = prod = None
        for l in range(N_LEVELS):
            if l + GLA_AHEAD < N_LEVELS:
                ahead.append(level_sums(l + GLA_AHEAD))
            e = [jnp.exp2(x) for x in ahead[l]]
            if l == 0:
                ql = [top_level(q[u], x, d, True) for (u, d), x in zip(lanes, e)]
                kl = [top_level(k[u], x, d, False) for (u, d), x in zip(lanes, e)]
            else:
                ql = [(q[u] * x).astype(BF16) for (u, d), x in zip(lanes, e)]
                kl = [(k[u] * x).astype(BF16) for (u, d), x in zip(lanes, e)]
            if l == 1:
                scores = prod
            elif l > 1:
                own = [lvl_ref[d] == l - 1 for d in range(2)]
                scores = [jnp.where(own[d], p, s) for (u, d), p, s in zip(lanes, prod, scores)]
            prod = [_dot_nt(a, b) for a, b in zip(ql, kl)]
        own = [lvl_ref[d] == N_LEVELS - 1 for d in range(2)]
        scores = [jnp.where(own[d], p, s) for (u, d), p, s in zip(lanes, prod, scores)]
        diag = [_dot_nt(q16[u], k16[u]) for u in range(gg)]
        own = [lvl_ref[d] == N_LEVELS for d in range(2)]
        scores = [jnp.where(own[d], diag[u], s) for (u, d), s in zip(lanes, scores)]
        for (u, d), s, b in zip(lanes, scores, bc):
            idx = d * nc + cs[u]
            btot = b[0:1, :] if d == 1 else b[CH - 1:CH, :]
            oi_ref[idx] = _dot(s.astype(BF16), v16[u])
            qs_ref[idx] = (q[u] * jnp.exp2(b)).astype(BF16)
            kv_ref[idx] = _dot_tn((k[u] * jnp.exp2(btot - b)).astype(BF16), v16[u])
            dec_ref[idx] = jnp.exp2(jnp.broadcast_to(btot, (CH, dk)).T)
        return carry

    lax.fori_loop(0, nc // gg, prep, 0)

    def scan(c, carry):
        sf, sb = carry
        cf = c
        ib = 2 * nc - 1 - c
        st_ref[cf] = sf.astype(BF16)
        st_ref[ib] = sb.astype(BF16)
        ef = dec_ref[cf]
        eb = dec_ref[ib]
        sf = sf * jnp.concatenate([ef, ef], axis=1) + kv_ref[cf]
        sb = sb * jnp.concatenate([eb, eb], axis=1) + kv_ref[ib]
        return sf, sb

    zero = jnp.zeros((dk, dv), F32)
    lax.fori_loop(0, nc, scan, (zero, zero))

    ng = ng_ref[...]
    fg = _group(nc, FIN_GROUP)

    def fin(it, carry):
        cs = [it * fg + u for u in range(fg)]
        rows = [pl.ds(pl.multiple_of(c * CH, CH), CH) for c in cs]
        os_ = [oi_ref[c] + oi_ref[nc + c] + _dot(qs_ref[c], st_ref[c]) + _dot(qs_ref[nc + c], st_ref[nc + c])
               for c in cs]
        inv = [lax.rsqrt(jnp.mean(o * o, axis=-1, keepdims=True) + RMS_EPS) for o in os_]
        for r, o, s in zip(rows, os_, inv):
            o_ref[0, r, :] = (o * s * ng * _silu(p_ref[r, c_r:c_r + dv])).astype(BF16)
        return carry

    lax.fori_loop(0, nc // fg, fin, 0)


def _gla_mixer(xb, w_in, gate_w2, gate_b, norm_g):
    bn, seq, dm = xb.shape
    h, dk, dv = B_HEADS, B_DK, B_DV
    nc = seq // CH
    kw, vw = h * dk, h * dv
    w = w_in

    def heads(cols, width):
        return cols.reshape(dm, h, width).transpose(1, 0, 2)

    gl = jnp.pad(w[:, 2 * kw + 2 * vw:], ((0, 0), (0, dk - 2 * B_RANK)))
    wh = jnp.concatenate([
        heads(w[:, 0:kw], dk), heads(w[:, kw:2 * kw], dk),
        heads(w[:, 2 * kw:2 * kw + vw], dv), heads(w[:, 2 * kw + vw:2 * kw + 2 * vw], dv),
        jnp.broadcast_to(gl[None], (h, dm, dk))], axis=2).astype(BF16)
    w2 = gate_w2.reshape(2, B_RANK, h, dk).transpose(2, 0, 1, 3)
    w2p = jnp.zeros((h, 2, dk, dk), F32)
    w2p = w2p.at[:, 0, 0:B_RANK].set(w2[:, 0]).at[:, 1, B_RANK:2 * B_RANK].set(w2[:, 1]).astype(BF16)
    gb = gate_b.reshape(2, h, dk).transpose(1, 0, 2).astype(F32)
    gb = jnp.broadcast_to(gb[:, :, None, :], (h, 2, 8, dk))
    ng = norm_g.astype(F32).reshape(1, dv)
    seg, lvl = _gla_tables()
    nw = wh.shape[2]

    kern = functools.partial(_gla_kernel, seq=seq)
    return pl.pallas_call(
        kern,
        out_shape=jax.ShapeDtypeStruct((bn, seq, vw), BF16),
        grid=(bn, h),
        in_specs=[
            pl.BlockSpec((1, seq, dm), lambda b, i: (b, 0, 0)),
            pl.BlockSpec((1, dm, nw), lambda b, i: (i, 0, 0)),
            pl.BlockSpec((1, 2, dk, dk), lambda b, i: (i, 0, 0, 0)),
            pl.BlockSpec((1, 2, 8, dk), lambda b, i: (i, 0, 0, 0)),
            pl.BlockSpec((1, dv), lambda b, i: (0, 0)),
            pl.BlockSpec(seg.shape, lambda b, i: (0, 0, 0, 0)),
            pl.BlockSpec(lvl.shape, lambda b, i: (0, 0, 0)),
        ],
        out_specs=pl.BlockSpec((1, seq, dv), lambda b, i: (b, 0, i)),
        scratch_shapes=[
            pltpu.VMEM((seq, nw), F32),
            pltpu.VMEM((2 * nc, CH, dk), BF16),
            pltpu.VMEM((2 * nc, dk, dv), F32),
            pltpu.VMEM((2 * nc, dk, dv), BF16),
            pltpu.VMEM((2 * nc, dk, dk), F32),
            pltpu.VMEM((2 * nc, CH, dv), F32),
        ],
        compiler_params=pltpu.CompilerParams(
            dimension_semantics=("arbitrary", "arbitrary"), vmem_limit_bytes=VMEM_LIMIT),
        name="gla_mixer",
    )(xb, wh, w2p, gb, ng, seg, lvl)


def _post_kernel(o_ref, x_ref, wo_ref, w1_ref, w2_ref, ln_ref, y_ref, yb_ref, *, alpha):
    ln = ln_ref[...]
    x = x_ref[...]
    x1 = _layernorm(alpha * x + _dot(o_ref[...], wo_ref[...]), ln[0:1, :], ln[1:2, :])
    x1b = x1.astype(BF16)
    acc = jnp.zeros(x.shape, F32)
    dff = w1_ref.shape[1]
    for j in range(dff // FF_TILE):
        cols = slice(j * FF_TILE, (j + 1) * FF_TILE)
        hcur = jnp.maximum(_dot(x1b, w1_ref[:, cols]), 0.0)
        acc = acc + _dot((hcur * hcur).astype(BF16), w2_ref[cols, :])
    y = _layernorm(alpha * x1 + acc, ln[2:3, :], ln[3:4, :])
    y_ref[...] = y
    yb_ref[...] = y.astype(BF16)


def _post(o, x, w_out, w1, w2, g1, b1, g2, b2, alpha):
    t, dm = x.shape
    vw = o.shape[1]
    dff = w1.shape[1]
    tm = min(ROW_TILE, t)
    ln = jnp.pad(jnp.stack([g1, b1, g2, b2]).astype(F32), ((0, 4), (0, 0)))
    const = lambda shape: pl.BlockSpec(shape, lambda i: (0, 0), pipeline_mode=pl.Buffered(1))
    return pl.pallas_call(
        functools.partial(_post_kernel, alpha=alpha),
        out_shape=(jax.ShapeDtypeStruct((t, dm), F32), jax.ShapeDtypeStruct((t, dm), BF16)),
        grid=(t // tm,),
        in_specs=[
            pl.BlockSpec((tm, vw), lambda i: (i, 0)),
            pl.BlockSpec((tm, dm), lambda i: (i, 0)),
            const((vw, dm)), const((dm, dff)), const((dff, dm)), const((8, dm)),
        ],
        out_specs=(pl.BlockSpec((tm, dm), lambda i: (i, 0)), pl.BlockSpec((tm, dm), lambda i: (i, 0))),
        compiler_params=pltpu.CompilerParams(
            dimension_semantics=("arbitrary",), vmem_limit_bytes=VMEM_LIMIT),
        name="post",
    )(o, x, w_out.astype(BF16), w1.astype(BF16), w2.astype(BF16), ln)


def kernel(x, a_w_in, a_conv, a_alog, a_dt_bias, a_norm_g, a_w_out, b_w_in, b_gate_w2, b_gate_b,
           b_norm_g, b_w_out, ln1_g, ln1_b, mlp_w1, mlp_w2, ln2_g, ln2_b):
    bn, seq, dm = x.shape
    depth = ln1_g.shape[0]
    alpha = (2 * depth) ** 0.25
    xf = x.astype(F32).reshape(bn * seq, dm)
    xb = xf.astype(BF16)
    for i in range(depth):
        j = i // 2
        xb3 = xb.reshape(bn, seq, dm)
        if i % 2 == 0:
            o = _gdn_mixer(xb3, a_w_in[j], a_conv[j], a_alog[j], a_dt_bias[j], a_norm_g[j])
            w_out = a_w_out[j]
        else:
            o = _gla_mixer(xb3, b_w_in[j], b_gate_w2[j], b_gate_b[j], b_norm_g[j])
            w_out = b_w_out[j]
        xf, xb = _post(o.reshape(bn * seq, -1), xf, w_out, mlp_w1[i], mlp_w2[i],
                       ln1_g[i], ln1_b[i], ln2_g[i], ln2_b[i], alpha)
    return xf.reshape(bn, seq, dm).astype(x.dtype)
```

```python
import functools
import math

import numpy as np

import jax
import jax.numpy as jnp
from jax import lax
from jax.experimental import pallas as pl
from jax.experimental.pallas import tpu as pltpu

F32 = jnp.float32
BF16 = jnp.bfloat16

A_HEADS, A_DK, A_DV, A_CONV = 8, 128, 128, 5
B_HEADS, B_DK, B_DV, B_RANK, B_TAU = 4, 128, 256, 16, 16.0
LN_EPS, RMS_EPS, L2_EPS = 1e-5, 1e-6, 1e-6

CH = 128
N_LEVELS = 7
HALO = 8
CONV_MXU_TAPS = (0, 4)
GATE_ROWS = 16
SOLVE_LEVELS_PER_STAGE = 1
GLA_GROUP = 8
GLA_AHEAD = 2
GLA_PIECES = 2
FIN_GROUP = 16
NEG_BIG = -1e30
LOG2E = math.log2(math.e)
VMEM_LIMIT = 56 * 1024 * 1024
ROW_TILE = 1024
FF_TILE = 1024

assert CH == A_DK == B_DK and 2 ** N_LEVELS == CH


def _dot(a, b):
    return jnp.dot(a, b, preferred_element_type=F32)


def _dot_nt(a, b):
    return lax.dot_general(a, b, (((1,), (1,)), ((), ())), preferred_element_type=F32)


def _dot_tn(a, b):
    return lax.dot_general(a, b, (((0,), (0,)), ((), ())), preferred_element_type=F32)


def _split(x, n, axis=1):
    pieces = []
    for _ in range(n - 1):
        p = x.astype(BF16)
        pieces.append(p)
        x = x - p.astype(F32)
    pieces.append(x.astype(BF16))
    return jnp.concatenate(pieces, axis=axis)


def _fold(y, n, axis=1):
    w = y.shape[axis] // n
    blocks = [lax.slice_in_dim(y, i * w, (i + 1) * w, axis=axis) for i in range(n)]
    out = blocks[0]
    for b in blocks[1:]:
        out = out + b
    return out


def _sigmoid(x):
    return 0.5 + 0.5 * jnp.tanh(0.5 * x)


def _silu(x):
    h = 0.5 * x
    return h + h * jnp.tanh(h)


def _softplus(x):
    return jnp.maximum(x, 0.0) + jnp.log(1.0 + jnp.exp(-jnp.abs(x)))


def _layernorm(y, g, b):
    mu = jnp.mean(y, axis=-1, keepdims=True)
    yc = y - mu
    var = jnp.mean(yc * yc, axis=-1, keepdims=True)
    return yc * lax.rsqrt(var + LN_EPS) * g + b


def _order_masks(rev):
    row = lax.broadcasted_iota(jnp.int32, (CH, CH), 0)
    col = lax.broadcasted_iota(jnp.int32, (CH, CH), 1)
    if rev:
        return col >= row, col > row
    return col <= row, col < row


def _group(n, want):
    return math.gcd(n, want)


def _gdn_kernel(xb_ref, wh_ref, cw_ref, hp_ref, ng_ref, lm_ref, sh_ref, o_ref,
                p_ref, qkv_ref, gate_ref, a_ref, t_ref, qk_ref, rhs_ref, qd_ref, kd_ref, gl_ref,
                mc_ref, qc_ref, rc_ref, oc_ref, *, seq):
    nc = seq // CH
    spare = 2 * nc

    p_ref[0:HALO, :] = jnp.zeros((HALO, p_ref.shape[1]), F32)
    p_ref[HALO + seq:, :] = jnp.zeros((HALO, p_ref.shape[1]), F32)
    p_ref[HALO:HALO + seq, :] = _dot(xb_ref[0], wh_ref[0])
    for ref in (a_ref, t_ref, qk_ref, kd_ref, rhs_ref, qd_ref):
        ref[spare] = jnp.zeros(ref.shape[1:], ref.dtype)
    qkv_ref[nc] = jnp.zeros(qkv_ref.shape[1:], F32)
    gate_ref[nc] = jnp.zeros(gate_ref.shape[1:], F32)

    cw = cw_ref[0]
    hp = hp_ref[0]

    def tiles(it, lag):
        c = it - lag
        ok = jnp.logical_and(c >= 0, c < nc)
        return jnp.where(ok, c, spare), jnp.where(ok, nc + c, spare)

    def prep_one(c):
        base = pl.multiple_of(c * CH, CH)
        win = p_ref[pl.ds(base, CH + 2 * HALO), 0:3 * A_DK]
        gates = p_ref[pl.ds(base + HALO, CH), 4 * A_DK:5 * A_DK].T[0:GATE_ROWS, :]
        if CONV_MXU_TAPS:
            shifted = _dot(sh_ref[...], win.astype(BF16))
        mid = A_CONV // 2
        acc = win[HALO:HALO + CH, :] * cw[mid:mid + 1, :]
        for i in range(A_CONV):
            if i != mid and i not in CONV_MXU_TAPS:
                off = HALO + i - mid
                acc = acc + win[off:off + CH, :] * cw[i:i + 1, :]
        g_rows = -hp[0] * _softplus(gates + hp[1])
        pieces = _split(g_rows, 3, axis=0)
        gcum = [_fold(_dot(pieces, _order_masks(not rev)[0].astype(BF16)), 3, axis=0) for rev in (False, True)]
        yield
        for n, i in enumerate(CONV_MXU_TAPS):
            acc = acc + shifted[n * CH:(n + 1) * CH, :] * cw[i:i + 1, :]
        yield
        s = _silu(acc)
        q = s[:, 0:A_DK]
        k = s[:, A_DK:2 * A_DK]
        q = q * (lax.rsqrt(jnp.sum(q * q, axis=-1, keepdims=True) + L2_EPS) * (A_DK ** -0.5))
        k = k * lax.rsqrt(jnp.sum(k * k, axis=-1, keepdims=True) + L2_EPS)
        yield
        qkv_ref[c] = jnp.concatenate([q, k, s[:, 2 * A_DK:]], axis=1)
        gate_ref[c] = jnp.concatenate([_sigmoid(gates), gcum[0], gcum[1]], axis=0)
        yield

    def prep_two(jf, jb, qkv, gate):
        q = qkv[:, 0:A_DK]
        k = qkv[:, A_DK:2 * A_DK]
        v = qkv[:, 2 * A_DK:]
        dirs = ((0, False, jf), (1, True, jb))
        gr = [jnp.broadcast_to(gate[(1 + d) * GATE_ROWS + 2 + d:(1 + d) * GATE_ROWS + 3 + d, :], (CH, CH))
              for d, _, _ in dirs]
        gc = [x.T for x in gr]
        beta = [jnp.broadcast_to(gate[d:d + 1, :], (CH, CH)).T for d, _, _ in dirs]
        kb16 = k.astype(BF16)
        eye = (lax.broadcasted_iota(jnp.int32, (CH, CH), 0)
               == lax.broadcasted_iota(jnp.int32, (CH, CH), 1)).astype(F32)
        yield
        kbeta = [k * beta[d] for d, _, _ in dirs]
        kq = [_dot_nt(jnp.concatenate([kbeta[d], q], axis=0).astype(BF16), kb16) for d, _, _ in dirs]
        gtot = [gc[d][0:1, :] if rev else gc[d][CH - 1:CH, :] for d, rev, _ in dirs]
        eg = [jnp.exp2(gc[d]) for d, _, _ in dirs]
        dmat = [jnp.exp2(jnp.where(_order_masks(rev)[0], gc[d] - gr[d], NEG_BIG)) for d, rev, _ in dirs]
        yield
        for d, rev, idx in dirs:
            rhs_ref[idx] = jnp.concatenate([v * beta[d], kbeta[d] * eg[d]], axis=1).astype(BF16)
            qd_ref[idx] = q * eg[d]
            kd_ref[idx] = (k * jnp.exp2(gtot[d] - gc[d])).astype(BF16)
            gl_ref[idx] = jnp.broadcast_to(jnp.exp2(gtot[d]), (8, A_DK))
        yield
        for d, rev, idx in dirs:
            a = jnp.where(_order_masks(rev)[1], kq[d][0:CH] * dmat[d], 0.0).astype(BF16)
            a_ref[idx] = a
            t_ref[idx] = (eye - (a * lm_ref[0]).astype(F32)).astype(BF16)
            qk_ref[idx] = (kq[d][CH:] * dmat[d]).astype(BF16)
        yield

    per_stage = SOLVE_LEVELS_PER_STAGE
    n_stages = (N_LEVELS - 1) // per_stage
    solve_stages = tuple((2 + s, tuple(range(1 + s * per_stage, 1 + (s + 1) * per_stage)))
                         for s in range(n_stages))
    ops_lag = n_stages + 2

    def step(it, stage_one, stage_two):
        fillers = []
        if stage_two:
            c2 = it - 1
            ok = jnp.logical_and(c2 >= 0, c2 < nc)
            cq = jnp.where(ok, c2, nc)
            fillers.append(prep_two(*tiles(it, 1), qkv_ref[cq], gate_ref[cq]))
        loaded = []
        for lag, lvs in solve_stages:
            jf, jb = tiles(it, lag)
            loaded.append((jf, jb, t_ref[jf], t_ref[jb], a_ref[jf], a_ref[jb]))
        ops_in = [(i, t_ref[i], rhs_ref[i], kd_ref[i], qk_ref[i], qd_ref[i]) for i in tiles(it, ops_lag)]
        if stage_one:
            fillers.append(prep_one(it))

        def fill():
            for f in fillers:
                next(f, None)

        uws = [_dot(t, rhs) for i, t, rhs, kd, qkm, qd in ops_in]
        for half in range(per_stage):
            xs = []
            for (lag, lvs), (jf, jb, tf, tb, af, ab) in zip(solve_stages, loaded):
                m = lm_ref[lvs[half]]
                xs.append((_dot(tf, af * m), _dot(tb, ab * m)))
            fill()
            fill()
            ys = [jnp.concatenate([_dot(xf.astype(BF16), tf), _dot(xb.astype(BF16), tb)], axis=1)
                  for (xf, xb), (jf, jb, tf, tb, af, ab) in zip(xs, loaded)]
            if half == 0:
                uws = [uw.astype(BF16) for uw in uws]
                kuws = [_dot_tn(kd, uw) for uw, (i, t, rhs, kd, qkm, qd) in zip(uws, ops_in)]
                quws = [_dot(qkm, uw) for uw, (i, t, rhs, kd, qkm, qd) in zip(uws, ops_in)]
            fill()
            fill()
            loaded = [(jf, jb, tf - y[:, 0:CH].astype(BF16), tb - y[:, CH:].astype(BF16), af, ab)
                      for y, (jf, jb, tf, tb, af, ab) in zip(ys, loaded)]
        for f in fillers:
            for _ in f:
                pass
        for jf, jb, tf, tb, _, _ in loaded:
            t_ref[jf] = tf
            t_ref[jb] = tb
        for kuw, quw, (i, t, rhs, kd, qkm, qd) in zip(kuws, quws, ops_in):
            qc_ref[i] = kuw[:, 0:A_DV]
            mc_ref[i] = kuw[:, A_DV:].astype(BF16)
            oc_ref[i] = quw[:, 0:A_DV]
            rc_ref[i] = (qd - quw[:, A_DV:]).astype(BF16)

    def loop(lo, hi, stage_one, stage_two):
        def body(it, carry):
            step(it, stage_one, stage_two)
            return carry
        lax.fori_loop(lo, hi, body, 0)

    loop(0, nc, True, True)
    loop(nc, nc + 1, False, True)
    loop(nc + 1, nc + ops_lag, False, False)

    def scan(c, carry):
        sf, sb = carry
        cf = c
        cb = 2 * nc - 1 - c
        of = _dot(rc_ref[cf], sf.astype(BF16)) + oc_ref[cf]
        ob = _dot(rc_ref[cb], sb.astype(BF16)) + oc_ref[cb]
        oc_ref[cf] = of
        oc_ref[cb] = ob
        sf = gl_ref[cf][0:1, :] * sf - _dot(mc_ref[cf], sf.astype(BF16)) + qc_ref[cf]
        sb = gl_ref[cb][0:1, :] * sb - _dot(mc_ref[cb], sb.astype(BF16)) + qc_ref[cb]
        return sf, sb

    zero = jnp.zeros((A_DK, A_DV), F32)
    lax.fori_loop(0, nc, scan, (zero, zero))

    ng = ng_ref[...]
    fg = _group(nc, FIN_GROUP)

    def fin(it, carry):
        cs = [it * fg + u for u in range(fg)]
        rows = [pl.ds(pl.multiple_of(c * CH, CH), CH) for c in cs]
        os_ = [oc_ref[c] + oc_ref[nc + c] for c in cs]
        inv = [lax.rsqrt(jnp.mean(o * o, axis=-1, keepdims=True) + RMS_EPS) for o in os_]
        for c, r, o, s in zip(cs, rows, os_, inv):
            z = p_ref[pl.ds(pl.multiple_of(c * CH, CH) + HALO, CH), 3 * A_DK:4 * A_DK]
            o_ref[0, r, :] = (o * s * ng * _silu(z)).astype(BF16)
        return carry

    lax.fori_loop(0, nc // fg, fin, 0)


def _gdn_level_masks():
    idx = np.arange(CH)
    x = idx[:, None] ^ idx[None, :]
    return jnp.asarray(np.stack([(x >> lv) == 1 for lv in range(N_LEVELS)]), BF16)


def _conv_shift_matrices():
    t = np.arange(CH)[:, None]
    r = np.arange(CH + 2 * HALO)[None, :]
    taps = CONV_MXU_TAPS or (0,)
    return jnp.asarray(np.concatenate([r == t + HALO + i - A_CONV // 2 for i in taps], axis=0), BF16)


def _gdn_mixer(xb, w_in, conv_w, a_log, dt_bias, norm_g):
    bn, seq, dm = xb.shape
    h, dk = A_HEADS, A_DK
    nc = seq // CH
    w = w_in
    hw = h * dk
    ba = w[:, 4 * hw:].reshape(dm, 2, 2, h)
    per_head = [w[:, i * hw:(i + 1) * hw].reshape(dm, h, dk).transpose(1, 0, 2) for i in range(4)]
    gate_cols = jnp.pad(ba.reshape(dm, 4, h).transpose(2, 0, 1), ((0, 0), (0, 0), (0, dk - 4)))
    wh = jnp.concatenate(per_head + [gate_cols], axis=2).astype(BF16)
    cw = conv_w.reshape(A_CONV, 3, h, dk).transpose(2, 0, 1, 3).reshape(h, A_CONV, 3 * dk)
    cw = jnp.pad(cw, ((0, 0), (0, 8 - A_CONV), (0, 0))).astype(F32)
    scale = jnp.zeros((h, GATE_ROWS), F32).at[:, 2:4].set((jnp.exp(a_log.astype(F32)) * LOG2E).T)
    bias = jnp.zeros((h, GATE_ROWS), F32).at[:, 2:4].set(dt_bias.astype(F32).T)
    hp = jnp.broadcast_to(jnp.stack([scale, bias], axis=1)[:, :, :, None], (h, 2, GATE_ROWS, dk))
    ng = norm_g.astype(F32).reshape(1, A_DV)
    lm = _gdn_level_masks()
    sh = _conv_shift_matrices()
    nw = wh.shape[2]

    kern = functools.partial(_gdn_kernel, seq=seq)
    tile = lambda dt: pltpu.VMEM((2 * nc + 1, CH, CH), dt)
    return pl.pallas_call(
        kern,
        out_shape=jax.ShapeDtypeStruct((bn, seq, h * A_DV), BF16),
        grid=(bn, h),
        in_specs=[
            pl.BlockSpec((1, seq, dm), lambda b, i: (b, 0, 0)),
            pl.BlockSpec((1, dm, nw), lambda b, i: (i, 0, 0)),
            pl.BlockSpec((1, 8, 3 * dk), lambda b, i: (i, 0, 0)),
            pl.BlockSpec((1, 2, GATE_ROWS, dk), lambda b, i: (i, 0, 0, 0)),
            pl.BlockSpec((1, A_DV), lambda b, i: (0, 0)),
            pl.BlockSpec(lm.shape, lambda b, i: (0, 0, 0)),
            pl.BlockSpec(sh.shape, lambda b, i: (0, 0)),
        ],
        out_specs=pl.BlockSpec((1, seq, A_DV), lambda b, i: (b, 0, i)),
        scratch_shapes=[
            pltpu.VMEM((seq + 2 * HALO, nw), F32),
            pltpu.VMEM((nc + 1, CH, 3 * A_DK), F32),
            pltpu.VMEM((nc + 1, 3 * GATE_ROWS, CH), F32),
            tile(BF16),
            tile(BF16),
            tile(BF16),
            pltpu.VMEM((2 * nc + 1, CH, A_DV + A_DK), BF16),
            tile(F32),
            tile(BF16),
            pltpu.VMEM((2 * nc + 1, 8, A_DK), F32),
            tile(BF16),
            tile(F32),
            tile(BF16),
            tile(F32),
        ],
        compiler_params=pltpu.CompilerParams(
            dimension_semantics=("arbitrary", "arbitrary"), vmem_limit_bytes=VMEM_LIMIT),
        name="gdn_mixer",
    )(xb, wh, cw, hp, ng, lm, sh)


def _gla_tables():
    i = np.arange(CH)[:, None]
    t = np.arange(CH)[None, :]
    seg = np.zeros((2, N_LEVELS + 1, CH, CH), np.float32)
    lvl = np.zeros((2, CH, CH), np.int32)
    for d in range(2):
        rev = d == 1
        seg[d, 0] = (t >= i) if rev else (t <= i)
        lv = np.full((CH, CH), N_LEVELS + 1, np.int32)
        lv[np.arange(CH), np.arange(CH)] = N_LEVELS
        x = i ^ t
        for l in range(N_LEVELS):
            h = 2 ** (N_LEVELS - 1 - l)
            b0 = (i // (2 * h)) * (2 * h)
            if rev:
                r = b0 + h
                late = i < r
                m = np.where(late, (t >= i) & (t < r), (t >= r) & (t < i))
                own = ((x >> (N_LEVELS - 1 - l)) == 1) & (t > i)
            else:
                r = b0 + h - 1
                late = i > r
                m = np.where(late, (t > r) & (t <= i), (t > i) & (t <= r))
                own = ((x >> (N_LEVELS - 1 - l)) == 1) & (t < i)
            seg[d, 1 + l] = m
            lv[own] = l
        lvl[d] = lv
    return jnp.asarray(seg, BF16), jnp.asarray(lvl)


def _gla_kernel(xb_ref, wh_ref, w2_ref, gb_ref, ng_ref, seg_ref, lvl_ref, o_ref,
                p_ref, qs_ref, kv_ref, st_ref, dec_ref, oi_ref, *, seq):
    nc = seq // CH
    dk, dv = B_DK, B_DV
    p_ref[...] = _dot(xb_ref[0], wh_ref[0])

    c_q, c_k, c_v, c_r, c_g = 0, dk, 2 * dk, 2 * dk + dv, 2 * dk + 2 * dv
    gg = _group(nc, GLA_GROUP)
    lanes = [(u, d) for u in range(gg) for d in range(2)]

    def prep(it, carry):
        cs = [it * gg + u for u in range(gg)]
        rows = [pl.ds(pl.multiple_of(c * CH, CH), CH) for c in cs]
        q = [p_ref[r, c_q:c_q + dk] * (dk ** -0.5) for r in rows]
        k = [p_ref[r, c_k:c_k + dk] for r in rows]
        q16 = [x.astype(BF16) for x in q]
        k16 = [x.astype(BF16) for x in k]
        v16 = [p_ref[r, c_v:c_v + dv].astype(BF16) for r in rows]
        gin = [p_ref[r, c_g:c_g + dk].astype(BF16) for r in rows]
        logit = [_dot(gin[u], w2_ref[0, d]) + gb_ref[0, d][0:1, :] for u, d in lanes]
        la3 = [_split(-_softplus(-x) * (LOG2E / B_TAU), 3) for x in logit]
        la2 = [y[:, 0:GLA_PIECES * dk] for y in la3]
        bc = [_fold(_dot(seg_ref[d, 0], y), 3) for (u, d), y in zip(lanes, la3)]

        def level_sums(l):
            h = CH >> (l + 1)
            if h < HALO:
                return [_fold(_dot(seg_ref[d, 1 + l], y), GLA_PIECES) for (u, d), y in zip(lanes, la2)]
            out = []
            for (u, d), b in zip(lanes, bc):
                blocks = []
                for lo in range(0, CH, 2 * h):
                    if d == 1:
                        ref = b[lo + h:lo + h + 1, :]
                        blocks += [b[lo:lo + h, :] - ref, ref - b[lo + h:lo + 2 * h, :]]
                    else:
                        ref = b[lo + h - 1:lo + h, :]
                        blocks += [ref - b[lo:lo + h, :], b[lo + h:lo + 2 * h, :] - ref]
                out.append(jnp.concatenate(blocks, axis=0))
            return out

        half = CH // 2
        zero_half = jnp.zeros((half, dk), BF16)

        def top_level(x, e, d, late):
            upper = (d == 1) != late
            rows = slice(half, CH) if upper else slice(0, half)
            kept = (x[rows] * e[rows]).astype(BF16)
            return jnp.concatenate([zero_half, kept] if upper else [kept, zero_half], axis=0)

        ahead = [level_sums(l) for l in range(GLA_AHEAD)]
        scores = prod = None
        for l in range(N_LEVELS):
            if l + GLA_AHEAD < N_LEVELS:
                ahead.append(level_sums(l + GLA_AHEAD))
            e = [jnp.exp2(x) for x in ahead[l]]
            if l == 0:
                ql = [top_level(q[u], x, d, True) for (u, d), x in zip(lanes, e)]
                kl = [top_level(k[u], x, d, False) for (u, d), x in zip(lanes, e)]
            else:
                ql = [(q[u] * x).astype(BF16) for (u, d), x in zip(lanes, e)]
                kl = [(k[u] * x).astype(BF16) for (u, d), x in zip(lanes, e)]
            if l == 1:
                scores = prod
            elif l > 1:
                own = [lvl_ref[d] == l - 1 for d in range(2)]
                scores = [jnp.where(own[d], p, s) for (u, d), p, s in zip(lanes, prod, scores)]
            prod = [_dot_nt(a, b) for a, b in zip(ql, kl)]
        own = [lvl_ref[d] == N_LEVELS - 1 for d in range(2)]
        scores = [jnp.where(own[d], p, s) for (u, d), p, s in zip(lanes, prod, scores)]
        diag = [_dot_nt(q16[u], k16[u]) for u in range(gg)]
        own = [lvl_ref[d] == N_LEVELS for d in range(2)]
        scores = [jnp.where(own[d], diag[u], s) for (u, d), s in zip(lanes, scores)]
        for (u, d), s, b in zip(lanes, scores, bc):
            idx = d * nc + cs[u]
            btot = b[0:1, :] if d == 1 else b[CH - 1:CH, :]
            oi_ref[idx] = _dot(s.astype(BF16), v16[u])
            qs_ref[idx] = (q[u] * jnp.exp2(b)).astype(BF16)
            kv_ref[idx] = _dot_tn((k[u] * jnp.exp2(btot - b)).astype(BF16), v16[u])
            dec_ref[idx] = jnp.exp2(jnp.broadcast_to(btot, (CH, dk)).T)
        return carry

    lax.fori_loop(0, nc // gg, prep, 0)

    def scan(c, carry):
        sf, sb = carry
        cf = c
        ib = 2 * nc - 1 - c
        st_ref[cf] = sf.astype(BF16)
        st_ref[ib] = sb.astype(BF16)
        ef = dec_ref[cf]
        eb = dec_ref[ib]
        sf = sf * jnp.concatenate([ef, ef], axis=1) + kv_ref[cf]
        sb = sb * jnp.concatenate([eb, eb], axis=1) + kv_ref[ib]
        return sf, sb

    zero = jnp.zeros((dk, dv), F32)
    lax.fori_loop(0, nc, scan, (zero, zero))

    ng = ng_ref[...]
    fg = _group(nc, FIN_GROUP)

    def fin(it, carry):
        cs = [it * fg + u for u in range(fg)]
        rows = [pl.ds(pl.multiple_of(c * CH, CH), CH) for c in cs]
        os_ = [oi_ref[c] + oi_ref[nc + c] + _dot(qs_ref[c], st_ref[c]) + _dot(qs_ref[nc + c], st_ref[nc + c])
               for c in cs]
        inv = [lax.rsqrt(jnp.mean(o * o, axis=-1, keepdims=True) + RMS_EPS) for o in os_]
        for r, o, s in zip(rows, os_, inv):
            o_ref[0, r, :] = (o * s * ng * _silu(p_ref[r, c_r:c_r + dv])).astype(BF16)
        return carry

    lax.fori_loop(0, nc // fg, fin, 0)


def _gla_mixer(xb, w_in, gate_w2, gate_b, norm_g):
    bn, seq, dm = xb.shape
    h, dk, dv = B_HEADS, B_DK, B_DV
    nc = seq // CH
    kw, vw = h * dk, h * dv
    w = w_in

    def heads(cols, width):
        return cols.reshape(dm, h, width).transpose(1, 0, 2)

    gl = jnp.pad(w[:, 2 * kw + 2 * vw:], ((0, 0), (0, dk - 2 * B_RANK)))
    wh = jnp.concatenate([
        heads(w[:, 0:kw], dk), heads(w[:, kw:2 * kw], dk),
        heads(w[:, 2 * kw:2 * kw + vw], dv), heads(w[:, 2 * kw + vw:2 * kw + 2 * vw], dv),
        jnp.broadcast_to(gl[None], (h, dm, dk))], axis=2).astype(BF16)
    w2 = gate_w2.reshape(2, B_RANK, h, dk).transpose(2, 0, 1, 3)
    w2p = jnp.zeros((h, 2, dk, dk), F32)
    w2p = w2p.at[:, 0, 0:B_RANK].set(w2[:, 0]).at[:, 1, B_RANK:2 * B_RANK].set(w2[:, 1]).astype(BF16)
    gb = gate_b.reshape(2, h, dk).transpose(1, 0, 2).astype(F32)
    gb = jnp.broadcast_to(gb[:, :, None, :], (h, 2, 8, dk))
    ng = norm_g.astype(F32).reshape(1, dv)
    seg, lvl = _gla_tables()
    nw = wh.shape[2]

    kern = functools.partial(_gla_kernel, seq=seq)
    return pl.pallas_call(
        kern,
        out_shape=jax.ShapeDtypeStruct((bn, seq, vw), BF16),
        grid=(bn, h),
        in_specs=[
            pl.BlockSpec((1, seq, dm), lambda b, i: (b, 0, 0)),
            pl.BlockSpec((1, dm, nw), lambda b, i: (i, 0, 0)),
            pl.BlockSpec((1, 2, dk, dk), lambda b, i: (i, 0, 0, 0)),
            pl.BlockSpec((1, 2, 8, dk), lambda b, i: (i, 0, 0, 0)),
            pl.BlockSpec((1, dv), lambda b, i: (0, 0)),
            pl.BlockSpec(seg.shape, lambda b, i: (0, 0, 0, 0)),
            pl.BlockSpec(lvl.shape, lambda b, i: (0, 0, 0)),
        ],
        out_specs=pl.BlockSpec((1, seq, dv), lambda b, i: (b, 0, i)),
        scratch_shapes=[
            pltpu.VMEM((seq, nw), F32),
            pltpu.VMEM((2 * nc, CH, dk), BF16),
            pltpu.VMEM((2 * nc, dk, dv), F32),
            pltpu.VMEM((2 * nc, dk, dv), BF16),
            pltpu.VMEM((2 * nc, dk, dk), F32),
            pltpu.VMEM((2 * nc, CH, dv), F32),
        ],
        compiler_params=pltpu.CompilerParams(
            dimension_semantics=("arbitrary", "arbitrary"), vmem_limit_bytes=VMEM_LIMIT),
        name="gla_mixer",
    )(xb, wh, w2p, gb, ng, seg, lvl)


def _post_kernel(o_ref, x_ref, wo_ref, w1_ref, w2_ref, ln_ref, y_ref, yb_ref, *, alpha):
    ln = ln_ref[...]
    x = x_ref[...]
    x1 = _layernorm(alpha * x + _dot(o_ref[...], wo_ref[...]), ln[0:1, :], ln[1:2, :])
    x1b = x1.astype(BF16)
    acc = jnp.zeros(x.shape, F32)
    dff = w1_ref.shape[1]
    for j in range(dff // FF_TILE):
        cols = slice(j * FF_TILE, (j + 1) * FF_TILE)
        hcur = jnp.maximum(_dot(x1b, w1_ref[:, cols]), 0.0)
        acc = acc + _dot((hcur * hcur).astype(BF16), w2_ref[cols, :])
    y = _layernorm(alpha * x1 + acc, ln[2:3, :], ln[3:4, :])
    y_ref[...] = y
    yb_ref[...] = y.astype(BF16)


def _post(o, x, w_out, w1, w2, g1, b1, g2, b2, alpha):
    t, dm = x.shape
    vw = o.shape[1]
    dff = w1.shape[1]
    tm = min(ROW_TILE, t)
    ln = jnp.pad(jnp.stack([g1, b1, g2, b2]).astype(F32), ((0, 4), (0, 0)))
    const = lambda shape: pl.BlockSpec(shape, lambda i: (0, 0), pipeline_mode=pl.Buffered(1))
    return pl.pallas_call(
        functools.partial(_post_kernel, alpha=alpha),
        out_shape=(jax.ShapeDtypeStruct((t, dm), F32), jax.ShapeDtypeStruct((t, dm), BF16)),
        grid=(t // tm,),
        in_specs=[
            pl.BlockSpec((tm, vw), lambda i: (i, 0)),
            pl.BlockSpec((tm, dm), lambda i: (i, 0)),
            const((vw, dm)), const((dm, dff)), const((dff, dm)), const((8, dm)),
        ],
        out_specs=(pl.BlockSpec((tm, dm), lambda i: (i, 0)), pl.BlockSpec((tm, dm), lambda i: (i, 0))),
        compiler_params=pltpu.CompilerParams(
            dimension_semantics=("arbitrary",), vmem_limit_bytes=VMEM_LIMIT),
        name="post",
    )(o, x, w_out.astype(BF16), w1.astype(BF16), w2.astype(BF16), ln)


def kernel(x, a_w_in, a_conv, a_alog, a_dt_bias, a_norm_g, a_w_out, b_w_in, b_gate_w2, b_gate_b,
           b_norm_g, b_w_out, ln1_g, ln1_b, mlp_w1, mlp_w2, ln2_g, ln2_b):
    bn, seq, dm = x.shape
    depth = ln1_g.shape[0]
    alpha = (2 * depth) ** 0.25
    xf = x.astype(F32).reshape(bn * seq, dm)
    xb = xf.astype(BF16)
    for i in range(depth):
        j = i // 2
        xb3 = xb.reshape(bn, seq, dm)
        if i % 2 == 0:
            o = _gdn_mixer(xb3, a_w_in[j], a_conv[j], a_alog[j], a_dt_bias[j], a_norm_g[j])
            w_out = a_w_out[j]
        else:
            o = _gla_mixer(xb3, b_w_in[j], b_gate_w2[j], b_gate_b[j], b_norm_g[j])
            w_out = b_w_out[j]
        xf, xb = _post(o.reshape(bn * seq, -1), xf, w_out, mlp_w1[i], mlp_w2[i],
                       ln1_g[i], ln1_b[i], ln2_g[i], ln2_b[i], alpha)
    return xf.reshape(bn, seq, dm).astype(x.dtype)
```

```python
import functools
import math

import numpy as np

import jax
import jax.numpy as jnp
from jax import lax
from jax.experimental import pallas as pl
from jax.experimental.pallas import tpu as pltpu

F32 = jnp.float32
BF16 = jnp.bfloat16

A_HEADS, A_DK, A_DV, A_CONV = 8, 128, 128, 5
B_HEADS, B_DK, B_DV, B_RANK, B_TAU = 4, 128, 256, 16, 16.0
LN_EPS, RMS_EPS, L2_EPS = 1e-5, 1e-6, 1e-6

CH = 128
N_LEVELS = 7
HALO = 8
CONV_MXU_TAPS = (0, 4)
GATE_ROWS = 16
SOLVE_LEVELS_PER_STAGE = 1
GLA_GROUP = 8
GLA_AHEAD = 2
GLA_PIECES = 2
FIN_GROUP = 16
NEG_BIG = -1e30
LOG2E = math.log2(math.e)
VMEM_LIMIT = 56 * 1024 * 1024
ROW_TILE = 1024
FF_TILE = 1024

assert CH == A_DK == B_DK and 2 ** N_LEVELS == CH


def _dot(a, b):
    return jnp.dot(a, b, preferred_element_type=F32)


def _dot_nt(a, b):
    return lax.dot_general(a, b, (((1,), (1,)), ((), ())), preferred_element_type=F32)


def _dot_tn(a, b):
    return lax.dot_general(a, b, (((0,), (0,)), ((), ())), preferred_element_type=F32)


def _split(x, n, axis=1):
    pieces = []
    for _ in range(n - 1):
        p = x.astype(BF16)
        pieces.append(p)
        x = x - p.astype(F32)
    pieces.append(x.astype(BF16))
    return jnp.concatenate(pieces, axis=axis)


def _fold(y, n, axis=1):
    w = y.shape[axis] // n
    blocks = [lax.slice_in_dim(y, i * w, (i + 1) * w, axis=axis) for i in range(n)]
    out = blocks[0]
    for b in blocks[1:]:
        out = out + b
    return out


def _sigmoid(x):
    return 0.5 + 0.5 * jnp.tanh(0.5 * x)


def _silu(x):
    h = 0.5 * x
    return h + h * jnp.tanh(h)


def _softplus(x):
    return jnp.maximum(x, 0.0) + jnp.log(1.0 + jnp.exp(-jnp.abs(x)))


def _layernorm(y, g, b):
    mu = jnp.mean(y, axis=-1, keepdims=True)
    yc = y - mu
    var = jnp.mean(yc * yc, axis=-1, keepdims=True)
    return yc * lax.rsqrt(var + LN_EPS) * g + b


def _order_masks(rev):
    row = lax.broadcasted_iota(jnp.int32, (CH, CH), 0)
    col = lax.broadcasted_iota(jnp.int32, (CH, CH), 1)
    if rev:
        return col >= row, col > row
    return col <= row, col < row


def _group(n, want):
    return math.gcd(n, want)


def _gdn_kernel(xb_ref, wh_ref, cw_ref, hp_ref, ng_ref, lm_ref, sh_ref, o_ref,
                p_ref, qkv_ref, gate_ref, a_ref, t_ref, qk_ref, rhs_ref, qd_ref, kd_ref, gl_ref,
                mc_ref, qc_ref, rc_ref, oc_ref, *, seq):
    nc = seq // CH
    spare = 2 * nc

    p_ref[0:HALO, :] = jnp.zeros((HALO, p_ref.shape[1]), F32)
    p_ref[HALO + seq:, :] = jnp.zeros((HALO, p_ref.shape[1]), F32)
    p_ref[HALO:HALO + seq, :] = _dot(xb_ref[0], wh_ref[0])
    for ref in (a_ref, t_ref, qk_ref, kd_ref, rhs_ref, qd_ref):
        ref[spare] = jnp.zeros(ref.shape[1:], ref.dtype)
    qkv_ref[nc] = jnp.zeros(qkv_ref.shape[1:], F32)
    gate_ref[nc] = jnp.zeros(gate_ref.shape[1:], F32)

    cw = cw_ref[0]
    hp = hp_ref[0]

    def tiles(it, lag):
        c = it - lag
        ok = jnp.logical_and(c >= 0, c < nc)
        return jnp.where(ok, c, spare), jnp.where(ok, nc + c, spare)

    def prep_one(c):
        base = pl.multiple_of(c * CH, CH)
        win = p_ref[pl.ds(base, CH + 2 * HALO), 0:3 * A_DK]
        gates = p_ref[pl.ds(base + HALO, CH), 4 * A_DK:5 * A_DK].T[0:GATE_ROWS, :]
        if CONV_MXU_TAPS:
            shifted = _dot(sh_ref[...], win.astype(BF16))
        mid = A_CONV // 2
        acc = win[HALO:HALO + CH, :] * cw[mid:mid + 1, :]
        for i in range(A_CONV):
            if i != mid and i not in CONV_MXU_TAPS:
                off = HALO + i - mid
                acc = acc + win[off:off + CH, :] * cw[i:i + 1, :]
        g_rows = -hp[0] * _softplus(gates + hp[1])
        pieces = _split(g_rows, 3, axis=0)
        gcum = [_fold(_dot(pieces, _order_masks(not rev)[0].astype(BF16)), 3, axis=0) for rev in (False, True)]
        yield
        for n, i in enumerate(CONV_MXU_TAPS):
            acc = acc + shifted[n * CH:(n + 1) * CH, :] * cw[i:i + 1, :]
        yield
        s = _silu(acc)
        q = s[:, 0:A_DK]
        k = s[:, A_DK:2 * A_DK]
        q = q * (lax.rsqrt(jnp.sum(q * q, axis=-1, keepdims=True) + L2_EPS) * (A_DK ** -0.5))
        k = k * lax.rsqrt(jnp.sum(k * k, axis=-1, keepdims=True) + L2_EPS)
        yield
        qkv_ref[c] = jnp.concatenate([q, k, s[:, 2 * A_DK:]], axis=1)
        gate_ref[c] = jnp.concatenate([_sigmoid(gates), gcum[0], gcum[1]], axis=0)
        yield

    def prep_two(jf, jb, qkv, gate):
        q = qkv[:, 0:A_DK]
        k = qkv[:, A_DK:2 * A_DK]
        v = qkv[:, 2 * A_DK:]
        dirs = ((0, False, jf), (1, True, jb))
        gr = [jnp.broadcast_to(gate[(1 + d) * GATE_ROWS + 2 + d:(1 + d) * GATE_ROWS + 3 + d, :], (CH, CH))
              for d, _, _ in dirs]
        gc = [x.T for x in gr]
        beta = [jnp.broadcast_to(gate[d:d + 1, :], (CH, CH)).T for d, _, _ in dirs]
        kb16 = k.astype(BF16)
        eye = (lax.broadcasted_iota(jnp.int32, (CH, CH), 0)
               == lax.broadcasted_iota(jnp.int32, (CH, CH), 1)).astype(F32)
        yield
        kbeta = [k * beta[d] for d, _, _ in dirs]
        kq = [_dot_nt(jnp.concatenate([kbeta[d], q], axis=0).astype(BF16), kb16) for d, _, _ in dirs]
        gtot = [gc[d][0:1, :] if rev else gc[d][CH - 1:CH, :] for d, rev, _ in dirs]
        eg = [jnp.exp2(gc[d]) for d, _, _ in dirs]
        dmat = [jnp.exp2(jnp.where(_order_masks(rev)[0], gc[d] - gr[d], NEG_BIG)) for d, rev, _ in dirs]
        yield
        for d, rev, idx in dirs:
            rhs_ref[idx] = jnp.concatenate([v * beta[d], kbeta[d] * eg[d]], axis=1).astype(BF16)
            qd_ref[idx] = q * eg[d]
            kd_ref[idx] = (k * jnp.exp2(gtot[d] - gc[d])).astype(BF16)
            gl_ref[idx] = jnp.broadcast_to(jnp.exp2(gtot[d]), (8, A_DK))
        yield
        for d, rev, idx in dirs:
            a = jnp.where(_order_masks(rev)[1], kq[d][0:CH] * dmat[d], 0.0).astype(BF16)
            a_ref[idx] = a
            t_ref[idx] = (eye - (a * lm_ref[0]).astype(F32)).astype(BF16)
            qk_ref[idx] = (kq[d][CH:] * dmat[d]).astype(BF16)
        yield

    per_stage = SOLVE_LEVELS_PER_STAGE
    n_stages = (N_LEVELS - 1) // per_stage
    solve_stages = tuple((2 + s, tuple(range(1 + s * per_stage, 1 + (s + 1) * per_stage)))
                         for s in range(n_stages))
    ops_lag = n_stages + 2

    def step(it, stage_one, stage_two):
        fillers = []
        if stage_two:
            c2 = it - 1
            ok = jnp.logical_and(c2 >= 0, c2 < nc)
            cq = jnp.where(ok, c2, nc)
            fillers.append(prep_two(*tiles(it, 1), qkv_ref[cq], gate_ref[cq]))
        loaded = []
        for lag, lvs in solve_stages:
            jf, jb = tiles(it, lag)
            loaded.append((jf, jb, t_ref[jf], t_ref[jb], a_ref[jf], a_ref[jb]))
        ops_in = [(i, t_ref[i], rhs_ref[i], kd_ref[i], qk_ref[i], qd_ref[i]) for i in tiles(it, ops_lag)]
        if stage_one:
            fillers.insert(0, prep_one(it))

        def fill():
            for f in fillers:
                next(f, None)

        uws = [_dot(t, rhs) for i, t, rhs, kd, qkm, qd in ops_in]
        for half in range(per_stage):
            xs = []
            for (lag, lvs), (jf, jb, tf, tb, af, ab) in zip(solve_stages, loaded):
                m = lm_ref[lvs[half]]
                xs.append((_dot(tf, af * m), _dot(tb, ab * m)))
            fill()
            fill()
            ys = [jnp.concatenate([_dot(xf.astype(BF16), tf), _dot(xb.astype(BF16), tb)], axis=1)
                  for (xf, xb), (jf, jb, tf, tb, af, ab) in zip(xs, loaded)]
            if half == 0:
                uws = [uw.astype(BF16) for uw in uws]
                kuws = [_dot_tn(kd, uw) for uw, (i, t, rhs, kd, qkm, qd) in zip(uws, ops_in)]
                quws = [_dot(qkm, uw) for uw, (i, t, rhs, kd, qkm, qd) in zip(uws, ops_in)]
            fill()
            fill()
            loaded = [(jf, jb, tf - y[:, 0:CH].astype(BF16), tb - y[:, CH:].astype(BF16), af, ab)
                      for y, (jf, jb, tf, tb, af, ab) in zip(ys, loaded)]
        for f in fillers:
            for _ in f:
                pass
        for jf, jb, tf, tb, _, _ in loaded:
            t_ref[jf] = tf
            t_ref[jb] = tb
        for kuw, quw, (i, t, rhs, kd, qkm, qd) in zip(kuws, quws, ops_in):
            qc_ref[i] = kuw[:, 0:A_DV]
            mc_ref[i] = kuw[:, A_DV:].astype(BF16)
            oc_ref[i] = quw[:, 0:A_DV]
            rc_ref[i] = (qd - quw[:, A_DV:]).astype(BF16)

    def loop(lo, hi, stage_one, stage_two):
        def body(it, carry):
            step(it, stage_one, stage_two)
            return carry
        lax.fori_loop(lo, hi, body, 0)

    loop(0, nc, True, True)
    loop(nc, nc + 1, False, True)
    loop(nc + 1, nc + ops_lag, False, False)

    def scan(c, carry):
        sf, sb = carry
        cf = c
        cb = 2 * nc - 1 - c
        of = _dot(rc_ref[cf], sf.astype(BF16)) + oc_ref[cf]
        ob = _dot(rc_ref[cb], sb.astype(BF16)) + oc_ref[cb]
        oc_ref[cf] = of
        oc_ref[cb] = ob
        sf = gl_ref[cf][0:1, :] * sf - _dot(mc_ref[cf], sf.astype(BF16)) + qc_ref[cf]
        sb = gl_ref[cb][0:1, :] * sb - _dot(mc_ref[cb], sb.astype(BF16)) + qc_ref[cb]
        return sf, sb

    zero = jnp.zeros((A_DK, A_DV), F32)
    lax.fori_loop(0, nc, scan, (zero, zero))

    ng = ng_ref[...]
    fg = _group(nc, FIN_GROUP)

    def fin(it, carry):
        cs = [it * fg + u for u in range(fg)]
        rows = [pl.ds(pl.multiple_of(c * CH, CH), CH) for c in cs]
        os_ = [oc_ref[c] + oc_ref[nc + c] for c in cs]
        inv = [lax.rsqrt(jnp.mean(o * o, axis=-1, keepdims=True) + RMS_EPS) for o in os_]
        for c, r, o, s in zip(cs, rows, os_, inv):
            z = p_ref[pl.ds(pl.multiple_of(c * CH, CH) + HALO, CH), 3 * A_DK:4 * A_DK]
            o_ref[0, r, :] = (o * s * ng * _silu(z)).astype(BF16)
        return carry

    lax.fori_loop(0, nc // fg, fin, 0)


def _gdn_level_masks():
    idx = np.arange(CH)
    x = idx[:, None] ^ idx[None, :]
    return jnp.asarray(np.stack([(x >> lv) == 1 for lv in range(N_LEVELS)]), BF16)


def _conv_shift_matrices():
    t = np.arange(CH)[:, None]
    r = np.arange(CH + 2 * HALO)[None, :]
    taps = CONV_MXU_TAPS or (0,)
    return jnp.asarray(np.concatenate([r == t + HALO + i - A_CONV // 2 for i in taps], axis=0), BF16)


def _gdn_mixer(xb, w_in, conv_w, a_log, dt_bias, norm_g):
    bn, seq, dm = xb.shape
    h, dk = A_HEADS, A_DK
    nc = seq // CH
    w = w_in
    hw = h * dk
    ba = w[:, 4 * hw:].reshape(dm, 2, 2, h)
    per_head = [w[:, i * hw:(i + 1) * hw].reshape(dm, h, dk).transpose(1, 0, 2) for i in range(4)]
    gate_cols = jnp.pad(ba.reshape(dm, 4, h).transpose(2, 0, 1), ((0, 0), (0, 0), (0, dk - 4)))
    wh = jnp.concatenate(per_head + [gate_cols], axis=2).astype(BF16)
    cw = conv_w.reshape(A_CONV, 3, h, dk).transpose(2, 0, 1, 3).reshape(h, A_CONV, 3 * dk)
    cw = jnp.pad(cw, ((0, 0), (0, 8 - A_CONV), (0, 0))).astype(F32)
    scale = jnp.zeros((h, GATE_ROWS), F32).at[:, 2:4].set((jnp.exp(a_log.astype(F32)) * LOG2E).T)
    bias = jnp.zeros((h, GATE_ROWS), F32).at[:, 2:4].set(dt_bias.astype(F32).T)
    hp = jnp.broadcast_to(jnp.stack([scale, bias], axis=1)[:, :, :, None], (h, 2, GATE_ROWS, dk))
    ng = norm_g.astype(F32).reshape(1, A_DV)
    lm = _gdn_level_masks()
    sh = _conv_shift_matrices()
    nw = wh.shape[2]

    kern = functools.partial(_gdn_kernel, seq=seq)
    tile = lambda dt: pltpu.VMEM((2 * nc + 1, CH, CH), dt)
    return pl.pallas_call(
        kern,
        out_shape=jax.ShapeDtypeStruct((bn, seq, h * A_DV), BF16),
        grid=(bn, h),
        in_specs=[
            pl.BlockSpec((1, seq, dm), lambda b, i: (b, 0, 0)),
            pl.BlockSpec((1, dm, nw), lambda b, i: (i, 0, 0)),
            pl.BlockSpec((1, 8, 3 * dk), lambda b, i: (i, 0, 0)),
            pl.BlockSpec((1, 2, GATE_ROWS, dk), lambda b, i: (i, 0, 0, 0)),
            pl.BlockSpec((1, A_DV), lambda b, i: (0, 0)),
            pl.BlockSpec(lm.shape, lambda b, i: (0, 0, 0)),
            pl.BlockSpec(sh.shape, lambda b, i: (0, 0)),
        ],
        out_specs=pl.BlockSpec((1, seq, A_DV), lambda b, i: (b, 0, i)),
        scratch_shapes=[
            pltpu.VMEM((seq + 2 * HALO, nw), F32),
            pltpu.VMEM((nc + 1, CH, 3 * A_DK), F32),
            pltpu.VMEM((nc + 1, 3 * GATE_ROWS, CH), F32),
            tile(BF16),
            tile(BF16),
            tile(BF16),
            pltpu.VMEM((2 * nc + 1, CH, A_DV + A_DK), BF16),
            tile(F32),
            tile(BF16),
            pltpu.VMEM((2 * nc + 1, 8, A_DK), F32),
            tile(BF16),
            tile(F32),
            tile(BF16),
            tile(F32),
        ],
        compiler_params=pltpu.CompilerParams(
            dimension_semantics=("arbitrary", "arbitrary"), vmem_limit_bytes=VMEM_LIMIT),
        name="gdn_mixer",
    )(xb, wh, cw, hp, ng, lm, sh)


def _gla_tables():
    i = np.arange(CH)[:, None]
    t = np.arange(CH)[None, :]
    seg = np.zeros((2, N_LEVELS + 1, CH, CH), np.float32)
    lvl = np.zeros((2, CH, CH), np.int32)
    for d in range(2):
        rev = d == 1
        seg[d, 0] = (t >= i) if rev else (t <= i)
        lv = np.full((CH, CH), N_LEVELS + 1, np.int32)
        lv[np.arange(CH), np.arange(CH)] = N_LEVELS
        x = i ^ t
        for l in range(N_LEVELS):
            h = 2 ** (N_LEVELS - 1 - l)
            b0 = (i // (2 * h)) * (2 * h)
            if rev:
                r = b0 + h
                late = i < r
                m = np.where(late, (t >= i) & (t < r), (t >= r) & (t < i))
                own = ((x >> (N_LEVELS - 1 - l)) == 1) & (t > i)
            else:
                r = b0 + h - 1
                late = i > r
                m = np.where(late, (t > r) & (t <= i), (t > i) & (t <= r))
                own = ((x >> (N_LEVELS - 1 - l)) == 1) & (t < i)
            seg[d, 1 + l] = m
            lv[own] = l
        lvl[d] = lv
    return jnp.asarray(seg, BF16), jnp.asarray(lvl)


def _gla_kernel(xb_ref, wh_ref, w2_ref, gb_ref, ng_ref, seg_ref, lvl_ref, o_ref,
                p_ref, qs_ref, kv_ref, st_ref, dec_ref, oi_ref, *, seq):
    nc = seq // CH
    dk, dv = B_DK, B_DV
    p_ref[...] = _dot(xb_ref[0], wh_ref[0])

    c_q, c_k, c_v, c_r, c_g = 0, dk, 2 * dk, 2 * dk + dv, 2 * dk + 2 * dv
    gg = _group(nc, GLA_GROUP)
    lanes = [(u, d) for u in range(gg) for d in range(2)]

    def prep(it, carry):
        cs = [it * gg + u for u in range(gg)]
        rows = [pl.ds(pl.multiple_of(c * CH, CH), CH) for c in cs]
        q = [p_ref[r, c_q:c_q + dk] * (dk ** -0.5) for r in rows]
        k = [p_ref[r, c_k:c_k + dk] for r in rows]
        q16 = [x.astype(BF16) for x in q]
        k16 = [x.astype(BF16) for x in k]
        v16 = [p_ref[r, c_v:c_v + dv].astype(BF16) for r in rows]
        gin = [p_ref[r, c_g:c_g + dk].astype(BF16) for r in rows]
        logit = [_dot(gin[u], w2_ref[0, d]) + gb_ref[0, d][0:1, :] for u, d in lanes]
        la3 = [_split(-_softplus(-x) * (LOG2E / B_TAU), 3) for x in logit]
        la2 = [y[:, 0:GLA_PIECES * dk] for y in la3]
        bc = [_fold(_dot(seg_ref[d, 0], y), 3) for (u, d), y in zip(lanes, la3)]
        for (u, d), b in zip(lanes, bc):
            idx = d * nc + cs[u]
            btot = b[0:1, :] if d == 1 else b[CH - 1:CH, :]
            qs_ref[idx] = (q[u] * jnp.exp2(b)).astype(BF16)
            kv_ref[idx] = _dot_tn((k[u] * jnp.exp2(btot - b)).astype(BF16), v16[u])
            dec_ref[idx] = jnp.exp2(jnp.broadcast_to(btot, (CH, dk)).T)

        def level_sums(l):
            h = CH >> (l + 1)
            if h < HALO:
                return [_fold(_dot(seg_ref[d, 1 + l], y), GLA_PIECES) for (u, d), y in zip(lanes, la2)]
            out = []
            for (u, d), b in zip(lanes, bc):
                blocks = []
                for lo in range(0, CH, 2 * h):
                    if d == 1:
                        ref = b[lo + h:lo + h + 1, :]
                        blocks += [b[lo:lo + h, :] - ref, ref - b[lo + h:lo + 2 * h, :]]
                    else:
                        ref = b[lo + h - 1:lo + h, :]
                        blocks += [ref - b[lo:lo + h, :], b[lo + h:lo + 2 * h, :] - ref]
                out.append(jnp.concatenate(blocks, axis=0))
            return out

        half = CH // 2
        zero_half = jnp.zeros((half, dk), BF16)

        def top_level(x, e, d, late):
            upper = (d == 1) != late
            rows = slice(half, CH) if upper else slice(0, half)
            kept = (x[rows] * e[rows]).astype(BF16)
            return jnp.concatenate([zero_half, kept] if upper else [kept, zero_half], axis=0)

        ahead = [level_sums(l) for l in range(GLA_AHEAD)]
        scores = prod = None
        for l in range(N_LEVELS):
            if l + GLA_AHEAD < N_LEVELS:
                ahead.append(level_sums(l + GLA_AHEAD))
            e = [jnp.exp2(x) for x in ahead[l]]
            if l == 0:
                ql = [top_level(q[u], x, d, True) for (u, d), x in zip(lanes, e)]
                kl = [top_level(k[u], x, d, False) for (u, d), x in zip(lanes, e)]
            else:
                ql = [(q[u] * x).astype(BF16) for (u, d), x in zip(lanes, e)]
                kl = [(k[u] * x).astype(BF16) for (u, d), x in zip(lanes, e)]
            if l == 1:
                scores = prod
            elif l > 1:
                own = [lvl_ref[d] == l - 1 for d in range(2)]
                scores = [jnp.where(own[d], p, s) for (u, d), p, s in zip(lanes, prod, scores)]
            prod = [_dot_nt(a, b) for a, b in zip(ql, kl)]
        own = [lvl_ref[d] == N_LEVELS - 1 for d in range(2)]
        scores = [jnp.where(own[d], p, s) for (u, d), p, s in zip(lanes, prod, scores)]
        diag = [_dot_nt(q16[u], k16[u]) for u in range(gg)]
        own = [lvl_ref[d] == N_LEVELS for d in range(2)]
        scores = [jnp.where(own[d], diag[u], s) for (u, d), s in zip(lanes, scores)]
        for (u, d), s in zip(lanes, scores):
            oi_ref[d * nc + cs[u]] = _dot(s.astype(BF16), v16[u])
        return carry

    lax.fori_loop(0, nc // gg, prep, 0)

    def scan(c, carry):
        sf, sb = carry
        cf = c
        ib = 2 * nc - 1 - c
        st_ref[cf] = sf.astype(BF16)
        st_ref[ib] = sb.astype(BF16)
        ef = dec_ref[cf]
        eb = dec_ref[ib]
        sf = sf * jnp.concatenate([ef, ef], axis=1) + kv_ref[cf]
        sb = sb * jnp.concatenate([eb, eb], axis=1) + kv_ref[ib]
        return sf, sb

    zero = jnp.zeros((dk, dv), F32)
    lax.fori_loop(0, nc, scan, (zero, zero))

    ng = ng_ref[...]
    fg = _group(nc, FIN_GROUP)

    def fin(it, carry):
        cs = [it * fg + u for u in range(fg)]
        rows = [pl.ds(pl.multiple_of(c * CH, CH), CH) for c in cs]
        os_ = [oi_ref[c] + oi_ref[nc + c] + _dot(qs_ref[c], st_ref[c]) + _dot(qs_ref[nc + c], st_ref[nc + c])
               for c in cs]
        inv = [lax.rsqrt(jnp.mean(o * o, axis=-1, keepdims=True) + RMS_EPS) for o in os_]
        for r, o, s in zip(rows, os_, inv):
            o_ref[0, r, :] = (o * s * ng * _silu(p_ref[r, c_r:c_r + dv])).astype(BF16)
        return carry

    lax.fori_loop(0, nc // fg, fin, 0)


def _gla_mixer(xb, w_in, gate_w2, gate_b, norm_g):
    bn, seq, dm = xb.shape
    h, dk, dv = B_HEADS, B_DK, B_DV
    nc = seq // CH
    kw, vw = h * dk, h * dv
    w = w_in

    def heads(cols, width):
        return cols.reshape(dm, h, width).transpose(1, 0, 2)

    gl = jnp.pad(w[:, 2 * kw + 2 * vw:], ((0, 0), (0, dk - 2 * B_RANK)))
    wh = jnp.concatenate([
        heads(w[:, 0:kw], dk), heads(w[:, kw:2 * kw], dk),
        heads(w[:, 2 * kw:2 * kw + vw], dv), heads(w[:, 2 * kw + vw:2 * kw + 2 * vw], dv),
        jnp.broadcast_to(gl[None], (h, dm, dk))], axis=2).astype(BF16)
    w2 = gate_w2.reshape(2, B_RANK, h, dk).transpose(2, 0, 1, 3)
    w2p = jnp.zeros((h, 2, dk, dk), F32)
    w2p = w2p.at[:, 0, 0:B_RANK].set(w2[:, 0]).at[:, 1, B_RANK:2 * B_RANK].set(w2[:, 1]).astype(BF16)
    gb = gate_b.reshape(2, h, dk).transpose(1, 0, 2).astype(F32)
    gb = jnp.broadcast_to(gb[:, :, None, :], (h, 2, 8, dk))
    ng = norm_g.astype(F32).reshape(1, dv)
    seg, lvl = _gla_tables()
    nw = wh.shape[2]

    kern = functools.partial(_gla_kernel, seq=seq)
    return pl.pallas_call(
        kern,
        out_shape=jax.ShapeDtypeStruct((bn, seq, vw), BF16),
        grid=(bn, h),
        in_specs=[
            pl.BlockSpec((1, seq, dm), lambda b, i: (b, 0, 0)),
            pl.BlockSpec((1, dm, nw), lambda b, i: (i, 0, 0)),
            pl.BlockSpec((1, 2, dk, dk), lambda b, i: (i, 0, 0, 0)),
            pl.BlockSpec((1, 2, 8, dk), lambda b, i: (i, 0, 0, 0)),
            pl.BlockSpec((1, dv), lambda b, i: (0, 0)),
            pl.BlockSpec(seg.shape, lambda b, i: (0, 0, 0, 0)),
            pl.BlockSpec(lvl.shape, lambda b, i: (0, 0, 0)),
        ],
        out_specs=pl.BlockSpec((1, seq, dv), lambda b, i: (b, 0, i)),
        scratch_shapes=[
            pltpu.VMEM((seq, nw), F32),
            pltpu.VMEM((2 * nc, CH, dk), BF16),
            pltpu.VMEM((2 * nc, dk, dv), F32),
            pltpu.VMEM((2 * nc, dk, dv), BF16),
            pltpu.VMEM((2 * nc, dk, dk), F32),
            pltpu.VMEM((2 * nc, CH, dv), F32),
        ],
        compiler_params=pltpu.CompilerParams(
            dimension_semantics=("arbitrary", "arbitrary"), vmem_limit_bytes=VMEM_LIMIT),
        name="gla_mixer",
    )(xb, wh, w2p, gb, ng, seg, lvl)


def _post_kernel(o_ref, x_ref, wo_ref, w1_ref, w2_ref, ln_ref, y_ref, yb_ref, *, alpha):
    ln = ln_ref[...]
    x = x_ref[...]
    x1 = _layernorm(alpha * x + _dot(o_ref[...], wo_ref[...]), ln[0:1, :], ln[1:2, :])
    x1b = x1.astype(BF16)
    acc = jnp.zeros(x.shape, F32)
    dff = w1_ref.shape[1]
    for j in range(dff // FF_TILE):
        cols = slice(j * FF_TILE, (j + 1) * FF_TILE)
        hcur = jnp.maximum(_dot(x1b, w1_ref[:, cols]), 0.0)
        acc = acc + _dot((hcur * hcur).astype(BF16), w2_ref[cols, :])
    y = _layernorm(alpha * x1 + acc, ln[2:3, :], ln[3:4, :])
    y_ref[...] = y
    yb_ref[...] = y.astype(BF16)


def _post(o, x, w_out, w1, w2, g1, b1, g2, b2, alpha):
    t, dm = x.shape
    vw = o.shape[1]
    dff = w1.shape[1]
    tm = min(ROW_TILE, t)
    ln = jnp.pad(jnp.stack([g1, b1, g2, b2]).astype(F32), ((0, 4), (0, 0)))
    const = lambda shape: pl.BlockSpec(shape, lambda i: (0, 0), pipeline_mode=pl.Buffered(1))
    return pl.pallas_call(
        functools.partial(_post_kernel, alpha=alpha),
        out_shape=(jax.ShapeDtypeStruct((t, dm), F32), jax.ShapeDtypeStruct((t, dm), BF16)),
        grid=(t // tm,),
        in_specs=[
            pl.BlockSpec((tm, vw), lambda i: (i, 0)),
            pl.BlockSpec((tm, dm), lambda i: (i, 0)),
            const((vw, dm)), const((dm, dff)), const((dff, dm)), const((8, dm)),
        ],
        out_specs=(pl.BlockSpec((tm, dm), lambda i: (i, 0)), pl.BlockSpec((tm, dm), lambda i: (i, 0))),
        compiler_params=pltpu.CompilerParams(
            dimension_semantics=("arbitrary",), vmem_limit_bytes=VMEM_LIMIT),
        name="post",
    )(o, x, w_out.astype(BF16), w1.astype(BF16), w2.astype(BF16), ln)


def kernel(x, a_w_in, a_conv, a_alog, a_dt_bias, a_norm_g, a_w_out, b_w_in, b_gate_w2, b_gate_b,
           b_norm_g, b_w_out, ln1_g, ln1_b, mlp_w1, mlp_w2, ln2_g, ln2_b):
    bn, seq, dm = x.shape
    depth = ln1_g.shape[0]
    alpha = (2 * depth) ** 0.25
    xf = x.astype(F32).reshape(bn * seq, dm)
    xb = xf.astype(BF16)
    for i in range(depth):
        j = i // 2
        xb3 = xb.reshape(bn, seq, dm)
        if i % 2 == 0:
            o = _gdn_mixer(xb3, a_w_in[j], a_conv[j], a_alog[j], a_dt_bias[j], a_norm_g[j])
            w_out = a_w_out[j]
        else:
            o = _gla_mixer(xb3, b_w_in[j], b_gate_w2[j], b_gate_b[j], b_norm_g[j])
            w_out = b_w_out[j]
        xf, xb = _post(o.reshape(bn * seq, -1), xf, w_out, mlp_w1[i], mlp_w2[i],
                       ln1_g[i], ln1_b[i], ln2_g[i], ln2_b[i], alpha)
    return xf.reshape(bn, seq, dm).astype(x.dtype)
```

```python
import functools
import math

import numpy as np

import jax
import jax.numpy as jnp
from jax import lax
from jax.experimental import pallas as pl
from jax.experimental.pallas import tpu as pltpu

F32 = jnp.float32
BF16 = jnp.bfloat16

A_HEADS, A_DK, A_DV, A_CONV = 8, 128, 128, 5
B_HEADS, B_DK, B_DV, B_RANK, B_TAU = 4, 128, 256, 16, 16.0
LN_EPS, RMS_EPS, L2_EPS = 1e-5, 1e-6, 1e-6

CH = 128
N_LEVELS = 7
HALO = 8
CONV_MXU_TAPS = (0, 4)
GATE_ROWS = 16
SOLVE_LEVELS_PER_STAGE = 1
GLA_GROUP = 8
GLA_AHEAD = 2
GLA_PIECES = 2
FIN_GROUP = 16
NEG_BIG = -1e30
LOG2E = math.log2(math.e)
VMEM_LIMIT = 56 * 1024 * 1024
ROW_TILE = 1024
FF_TILE = 1024

assert CH == A_DK == B_DK and 2 ** N_LEVELS == CH


def _dot(a, b):
    return jnp.dot(a, b, preferred_element_type=F32)


def _dot_nt(a, b):
    return lax.dot_general(a, b, (((1,), (1,)), ((), ())), preferred_element_type=F32)


def _dot_tn(a, b):
    return lax.dot_general(a, b, (((0,), (0,)), ((), ())), preferred_element_type=F32)


def _split(x, n, axis=1):
    pieces = []
    for _ in range(n - 1):
        p = x.astype(BF16)
        pieces.append(p)
        x = x - p.astype(F32)
    pieces.append(x.astype(BF16))
    return jnp.concatenate(pieces, axis=axis)


def _fold(y, n, axis=1):
    w = y.shape[axis] // n
    blocks = [lax.slice_in_dim(y, i * w, (i + 1) * w, axis=axis) for i in range(n)]
    out = blocks[0]
    for b in blocks[1:]:
        out = out + b
    return out


def _sigmoid(x):
    return 0.5 + 0.5 * jnp.tanh(0.5 * x)


def _silu(x):
    h = 0.5 * x
    return h + h * jnp.tanh(h)


def _softplus(x):
    return jnp.maximum(x, 0.0) + jnp.log(1.0 + jnp.exp(-jnp.abs(x)))


def _layernorm(y, g, b):
    mu = jnp.mean(y, axis=-1, keepdims=True)
    yc = y - mu
    var = jnp.mean(yc * yc, axis=-1, keepdims=True)
    return yc * lax.rsqrt(var + LN_EPS) * g + b


def _order_masks(rev):
    row = lax.broadcasted_iota(jnp.int32, (CH, CH), 0)
    col = lax.broadcasted_iota(jnp.int32, (CH, CH), 1)
    if rev:
        return col >= row, col > row
    return col <= row, col < row


def _group(n, want):
    return math.gcd(n, want)


def _gdn_kernel(xb_ref, wh_ref, cw_ref, hp_ref, ng_ref, lm_ref, sh_ref, o_ref,
                p_ref, qkv_ref, gate_ref, a_ref, t_ref, qk_ref, rhs_ref, qd_ref, kd_ref, gl_ref,
                mc_ref, qc_ref, rc_ref, oc_ref, *, seq):
    nc = seq // CH
    spare = 2 * nc

    p_ref[0:HALO, :] = jnp.zeros((HALO, p_ref.shape[1]), F32)
    p_ref[HALO + seq:, :] = jnp.zeros((HALO, p_ref.shape[1]), F32)
    p_ref[HALO:HALO + seq, :] = _dot(xb_ref[0], wh_ref[0])
    for ref in (a_ref, t_ref, qk_ref, kd_ref, rhs_ref, qd_ref):
        ref[spare] = jnp.zeros(ref.shape[1:], ref.dtype)
    qkv_ref[nc] = jnp.zeros(qkv_ref.shape[1:], F32)
    gate_ref[nc] = jnp.zeros(gate_ref.shape[1:], F32)

    cw = cw_ref[0]
    hp = hp_ref[0]

    def tiles(it, lag):
        c = it - lag
        ok = jnp.logical_and(c >= 0, c < nc)
        return jnp.where(ok, c, spare), jnp.where(ok, nc + c, spare)

    def prep_one(c):
        base = pl.multiple_of(c * CH, CH)
        win = p_ref[pl.ds(base, CH + 2 * HALO), 0:3 * A_DK]
        gates = p_ref[pl.ds(base + HALO, CH), 4 * A_DK:5 * A_DK].T[0:GATE_ROWS, :]
        if CONV_MXU_TAPS:
            shifted = _dot(sh_ref[...], win.astype(BF16))
        mid = A_CONV // 2
        acc = win[HALO:HALO + CH, :] * cw[mid:mid + 1, :]
        for i in range(A_CONV):
            if i != mid and i not in CONV_MXU_TAPS:
                off = HALO + i - mid
                acc = acc + win[off:off + CH, :] * cw[i:i + 1, :]
        g_rows = -hp[0] * _softplus(gates + hp[1])
        pieces = _split(g_rows, 3, axis=0)
        gcum = [_fold(_dot(pieces, _order_masks(not rev)[0].astype(BF16)), 3, axis=0) for rev in (False, True)]
        yield
        for n, i in enumerate(CONV_MXU_TAPS):
            acc = acc + shifted[n * CH:(n + 1) * CH, :] * cw[i:i + 1, :]
        yield
        s = _silu(acc)
        q = s[:, 0:A_DK]
        k = s[:, A_DK:2 * A_DK]
        q = q * (lax.rsqrt(jnp.sum(q * q, axis=-1, keepdims=True) + L2_EPS) * (A_DK ** -0.5))
        k = k * lax.rsqrt(jnp.sum(k * k, axis=-1, keepdims=True) + L2_EPS)
        yield
        qkv_ref[c] = jnp.concatenate([q, k, s[:, 2 * A_DK:]], axis=1)
        gate_ref[c] = jnp.concatenate([_sigmoid(gates), gcum[0], gcum[1]], axis=0)
        yield

    def prep_two(jf, jb, qkv, gate):
        q = qkv[:, 0:A_DK]
        k = qkv[:, A_DK:2 * A_DK]
        v = qkv[:, 2 * A_DK:]
        dirs = ((0, False, jf), (1, True, jb))
        gr = [jnp.broadcast_to(gate[(1 + d) * GATE_ROWS + 2 + d:(1 + d) * GATE_ROWS + 3 + d, :], (CH, CH))
              for d, _, _ in dirs]
        gc = [x.T for x in gr]
        beta = [jnp.broadcast_to(gate[d:d + 1, :], (CH, CH)).T for d, _, _ in dirs]
        kb16 = k.astype(BF16)
        eye = (lax.broadcasted_iota(jnp.int32, (CH, CH), 0)
               == lax.broadcasted_iota(jnp.int32, (CH, CH), 1)).astype(F32)
        yield
        kbeta = [k * beta[d] for d, _, _ in dirs]
        kq = [_dot_nt(jnp.concatenate([kbeta[d], q], axis=0).astype(BF16), kb16) for d, _, _ in dirs]
        gtot = [gc[d][0:1, :] if rev else gc[d][CH - 1:CH, :] for d, rev, _ in dirs]
        eg = [jnp.exp2(gc[d]) for d, _, _ in dirs]
        dmat = [jnp.exp2(jnp.where(_order_masks(rev)[0], gc[d] - gr[d], NEG_BIG)) for d, rev, _ in dirs]
        yield
        for d, rev, idx in dirs:
            rhs_ref[idx] = jnp.concatenate([v * beta[d], kbeta[d] * eg[d]], axis=1).astype(BF16)
            qd_ref[idx] = q * eg[d]
            kd_ref[idx] = (k * jnp.exp2(gtot[d] - gc[d])).astype(BF16)
            gl_ref[idx] = jnp.broadcast_to(jnp.exp2(gtot[d]), (8, A_DK))
        yield
        for d, rev, idx in dirs:
            a = jnp.where(_order_masks(rev)[1], kq[d][0:CH] * dmat[d], 0.0).astype(BF16)
            a_ref[idx] = a
            t_ref[idx] = (eye - (a * lm_ref[0]).astype(F32)).astype(BF16)
            qk_ref[idx] = (kq[d][CH:] * dmat[d]).astype(BF16)
        yield

    per_stage = SOLVE_LEVELS_PER_STAGE
    n_stages = (N_LEVELS - 1) // per_stage
    solve_stages = tuple((2 + s, tuple(range(1 + s * per_stage, 1 + (s + 1) * per_stage)))
                         for s in range(n_stages))
    ops_lag = n_stages + 2

    def step(it, stage_one, stage_two):
        fillers = []
        if stage_two:
            c2 = it - 1
            ok = jnp.logical_and(c2 >= 0, c2 < nc)
            cq = jnp.where(ok, c2, nc)
            fillers.append(prep_two(*tiles(it, 1), qkv_ref[cq], gate_ref[cq]))
        loaded = []
        for lag, lvs in solve_stages:
            jf, jb = tiles(it, lag)
            loaded.append((jf, jb, t_ref[jf], t_ref[jb], a_ref[jf], a_ref[jb]))
        ops_in = [(i, t_ref[i], rhs_ref[i], kd_ref[i], qk_ref[i], qd_ref[i]) for i in tiles(it, ops_lag)]
        if stage_one:
            fillers.insert(0, prep_one(it))

        def fill():
            for f in fillers:
                next(f, None)

        uws = [_dot(t, rhs) for i, t, rhs, kd, qkm, qd in ops_in]
        for half in range(per_stage):
            xs = []
            for (lag, lvs), (jf, jb, tf, tb, af, ab) in zip(solve_stages, loaded):
                m = lm_ref[lvs[half]]
                xs.append((_dot(tf, af * m), _dot(tb, ab * m)))
            fill()
            fill()
            ys = [jnp.concatenate([_dot(xf.astype(BF16), tf), _dot(xb.astype(BF16), tb)], axis=1)
                  for (xf, xb), (jf, jb, tf, tb, af, ab) in zip(xs, loaded)]
            if half == 0:
                uws = [uw.astype(BF16) for uw in uws]
                kuws = [_dot_tn(kd, uw) for uw, (i, t, rhs, kd, qkm, qd) in zip(uws, ops_in)]
                quws = [_dot(qkm, uw) for uw, (i, t, rhs, kd, qkm, qd) in zip(uws, ops_in)]
            fill()
            fill()
            loaded = [(jf, jb, tf - y[:, 0:CH].astype(BF16), tb - y[:, CH:].astype(BF16), af, ab)
                      for y, (jf, jb, tf, tb, af, ab) in zip(ys, loaded)]
        for f in fillers:
            for _ in f:
                pass
        for jf, jb, tf, tb, _, _ in loaded:
            t_ref[jf] = tf
            t_ref[jb] = tb
        for kuw, quw, (i, t, rhs, kd, qkm, qd) in zip(kuws, quws, ops_in):
            qc_ref[i] = kuw[:, 0:A_DV]
            mc_ref[i] = kuw[:, A_DV:].astype(BF16)
            oc_ref[i] = quw[:, 0:A_DV]
            rc_ref[i] = (qd - quw[:, A_DV:]).astype(BF16)

    def loop(lo, hi, stage_one, stage_two):
        def body(it, carry):
            step(it, stage_one, stage_two)
            return carry
        lax.fori_loop(lo, hi, body, 0)

    loop(0, nc, True, True)
    loop(nc, nc + 1, False, True)
    loop(nc + 1, nc + ops_lag, False, False)

    def scan(c, carry):
        sf, sb = carry
        cf = c
        cb = 2 * nc - 1 - c
        of = _dot(rc_ref[cf], sf.astype(BF16)) + oc_ref[cf]
        ob = _dot(rc_ref[cb], sb.astype(BF16)) + oc_ref[cb]
        oc_ref[cf] = of
        oc_ref[cb] = ob
        sf = gl_ref[cf][0:1, :] * sf - _dot(mc_ref[cf], sf.astype(BF16)) + qc_ref[cf]
        sb = gl_ref[cb][0:1, :] * sb - _dot(mc_ref[cb], sb.astype(BF16)) + qc_ref[cb]
        return sf, sb

    zero = jnp.zeros((A_DK, A_DV), F32)
    lax.fori_loop(0, nc, scan, (zero, zero))

    ng = ng_ref[...]
    fg = _group(nc, FIN_GROUP)

    def fin(it, carry):
        cs = [it * fg + u for u in range(fg)]
        rows = [pl.ds(pl.multiple_of(c * CH, CH), CH) for c in cs]
        os_ = [oc_ref[c] + oc_ref[nc + c] for c in cs]
        inv = [lax.rsqrt(jnp.mean(o * o, axis=-1, keepdims=True) + RMS_EPS) for o in os_]
        for c, r, o, s in zip(cs, rows, os_, inv):
            z = p_ref[pl.ds(pl.multiple_of(c * CH, CH) + HALO, CH), 3 * A_DK:4 * A_DK]
            o_ref[0, r, :] = (o * s * ng * _silu(z)).astype(BF16)
        return carry

    lax.fori_loop(0, nc // fg, fin, 0)


def _gdn_level_masks():
    idx = np.arange(CH)
    x = idx[:, None] ^ idx[None, :]
    return jnp.asarray(np.stack([(x >> lv) == 1 for lv in range(N_LEVELS)]), BF16)


def _conv_shift_matrices():
    t = np.arange(CH)[:, None]
    r = np.arange(CH + 2 * HALO)[None, :]
    taps = CONV_MXU_TAPS or (0,)
    return jnp.asarray(np.concatenate([r == t + HALO + i - A_CONV // 2 for i in taps], axis=0), BF16)


def _gdn_mixer(xb, w_in, conv_w, a_log, dt_bias, norm_g):
    bn, seq, dm = xb.shape
    h, dk = A_HEADS, A_DK
    nc = seq // CH
    w = w_in
    hw = h * dk
    ba = w[:, 4 * hw:].reshape(dm, 2, 2, h)
    per_head = [w[:, i * hw:(i + 1) * hw].reshape(dm, h, dk).transpose(1, 0, 2) for i in range(4)]
    gate_cols = jnp.pad(ba.reshape(dm, 4, h).transpose(2, 0, 1), ((0, 0), (0, 0), (0, dk - 4)))
    wh = jnp.concatenate(per_head + [gate_cols], axis=2).astype(BF16)
    cw = conv_w.reshape(A_CONV, 3, h, dk).transpose(2, 0, 1, 3).reshape(h, A_CONV, 3 * dk)
    cw = jnp.pad(cw, ((0, 0), (0, 8 - A_CONV), (0, 0))).astype(F32)
    scale = jnp.zeros((h, GATE_ROWS), F32).at[:, 2:4].set((jnp.exp(a_log.astype(F32)) * LOG2E).T)
    bias = jnp.zeros((h, GATE_ROWS), F32).at[:, 2:4].set(dt_bias.astype(F32).T)
    hp = jnp.broadcast_to(jnp.stack([scale, bias], axis=1)[:, :, :, None], (h, 2, GATE_ROWS, dk))
    ng = norm_g.astype(F32).reshape(1, A_DV)
    lm = _gdn_level_masks()
    sh = _conv_shift_matrices()
    nw = wh.shape[2]

    kern = functools.partial(_gdn_kernel, seq=seq)
    tile = lambda dt: pltpu.VMEM((2 * nc + 1, CH, CH), dt)
    return pl.pallas_call(
        kern,
        out_shape=jax.ShapeDtypeStruct((bn, seq, h * A_DV), BF16),
        grid=(bn, h),
        in_specs=[
            pl.BlockSpec((1, seq, dm), lambda b, i: (b, 0, 0)),
            pl.BlockSpec((1, dm, nw), lambda b, i: (i, 0, 0)),
            pl.BlockSpec((1, 8, 3 * dk), lambda b, i: (i, 0, 0)),
            pl.BlockSpec((1, 2, GATE_ROWS, dk), lambda b, i: (i, 0, 0, 0)),
            pl.BlockSpec((1, A_DV), lambda b, i: (0, 0)),
            pl.BlockSpec(lm.shape, lambda b, i: (0, 0, 0)),
            pl.BlockSpec(sh.shape, lambda b, i: (0, 0)),
        ],
        out_specs=pl.BlockSpec((1, seq, A_DV), lambda b, i: (b, 0, i)),
        scratch_shapes=[
            pltpu.VMEM((seq + 2 * HALO, nw), F32),
            pltpu.VMEM((nc + 1, CH, 3 * A_DK), F32),
            pltpu.VMEM((nc + 1, 3 * GATE_ROWS, CH), F32),
            tile(BF16),
            tile(BF16),
            tile(BF16),
            pltpu.VMEM((2 * nc + 1, CH, A_DV + A_DK), BF16),
            tile(F32),
            tile(BF16),
            pltpu.VMEM((2 * nc + 1, 8, A_DK), F32),
            tile(BF16),
            tile(F32),
            tile(BF16),
            tile(F32),
        ],
        compiler_params=pltpu.CompilerParams(
            dimension_semantics=("arbitrary", "arbitrary"), vmem_limit_bytes=VMEM_LIMIT),
        name="gdn_mixer",
    )(xb, wh, cw, hp, ng, lm, sh)


def _gla_tables():
    i = np.arange(CH)[:, None]
    t = np.arange(CH)[None, :]
    seg = np.zeros((2, N_LEVELS + 1, CH, CH), np.float32)
    lvl = np.zeros((2, CH, CH), np.int32)
    for d in range(2):
        rev = d == 1
        seg[d, 0] = (t >= i) if rev else (t <= i)
        lv = np.full((CH, CH), N_LEVELS + 1, np.int32)
        lv[np.arange(CH), np.arange(CH)] = N_LEVELS
        x = i ^ t
        for l in range(N_LEVELS):
            h = 2 ** (N_LEVELS - 1 - l)
            b0 = (i // (2 * h)) * (2 * h)
            if rev:
                r = b0 + h
                late = i < r
                m = np.where(late, (t >= i) & (t < r), (t >= r) & (t < i))
                own = ((x >> (N_LEVELS - 1 - l)) == 1) & (t > i)
            else:
                r = b0 + h - 1
                late = i > r
                m = np.where(late, (t > r) & (t <= i), (t > i) & (t <= r))
                own = ((x >> (N_LEVELS - 1 - l)) == 1) & (t < i)
            seg[d, 1 + l] = m
            lv[own] = l
        lvl[d] = lv
    return jnp.asarray(seg, BF16), jnp.asarray(lvl)


def _gla_kernel(xb_ref, wh_ref, w2_ref, gb_ref, ng_ref, seg_ref, lvl_ref, o_ref,
                p_ref, qs_ref, kv_ref, st_ref, dec_ref, oi_ref, *, seq):
    nc = seq // CH
    dk, dv = B_DK, B_DV
    p_ref[...] = _dot(xb_ref[0], wh_ref[0])

    c_q, c_k, c_v, c_r, c_g = 0, dk, 2 * dk, 2 * dk + dv, 2 * dk + 2 * dv
    gg = _group(nc, GLA_GROUP)
    lanes = [(u, d) for u in range(gg) for d in range(2)]

    def prep(it, carry):
        cs = [it * gg + u for u in range(gg)]
        rows = [pl.ds(pl.multiple_of(c * CH, CH), CH) for c in cs]
        q = [p_ref[r, c_q:c_q + dk] * (dk ** -0.5) for r in rows]
        k = [p_ref[r, c_k:c_k + dk] for r in rows]
        q16 = [x.astype(BF16) for x in q]
        k16 = [x.astype(BF16) for x in k]
        v16 = [p_ref[r, c_v:c_v + dv].astype(BF16) for r in rows]
        gin = [p_ref[r, c_g:c_g + dk].astype(BF16) for r in rows]
        logit = [_dot(gin[u], w2_ref[0, d]) + gb_ref[0, d][0:1, :] for u, d in lanes]
        la3 = [_split(-_softplus(-x) * (LOG2E / B_TAU), 3) for x in logit]
        la2 = [y[:, 0:GLA_PIECES * dk] for y in la3]
        bc = [_fold(_dot(seg_ref[d, 0], y), 3) for (u, d), y in zip(lanes, la3)]
        for (u, d), b in zip(lanes, bc):
            idx = d * nc + cs[u]
            btot = b[0:1, :] if d == 1 else b[CH - 1:CH, :]
            qs_ref[idx] = (q[u] * jnp.exp2(b)).astype(BF16)
            kv_ref[idx] = _dot_tn((k[u] * jnp.exp2(btot - b)).astype(BF16), v16[u])
            dec_ref[idx] = jnp.exp2(jnp.broadcast_to(btot, (CH, dk)).T)

        def level_sums(l):
            h = CH >> (l + 1)
            if h < HALO:
                return [_fold(_dot(seg_ref[d, 1 + l], y), GLA_PIECES) for (u, d), y in zip(lanes, la2)]
            out = []
            for (u, d), b in zip(lanes, bc):
                blocks = []
                for lo in range(0, CH, 2 * h):
                    if d == 1:
                        ref = b[lo + h:lo + h + 1, :]
                        blocks += [b[lo:lo + h, :] - ref, ref - b[lo + h:lo + 2 * h, :]]
                    else:
                        ref = b[lo + h - 1:lo + h, :]
                        blocks += [ref - b[lo:lo + h, :], b[lo + h:lo + 2 * h, :] - ref]
                out.append(jnp.concatenate(blocks, axis=0))
            return out

        half = CH // 2
        zero_half = jnp.zeros((half, dk), BF16)

        def top_level(x, e, d, late):
            upper = (d == 1) != late
            rows = slice(half, CH) if upper else slice(0, half)
            kept = (x[rows] * e[rows]).astype(BF16)
            return jnp.concatenate([zero_half, kept] if upper else [kept, zero_half], axis=0)

        diag = [_dot_nt(q16[u], k16[u]) for u in range(gg)]
        ahead = [level_sums(l) for l in range(GLA_AHEAD)]
        scores = prod = None
        for l in range(N_LEVELS):
            if l + GLA_AHEAD < N_LEVELS:
                ahead.append(level_sums(l + GLA_AHEAD))
            e = [jnp.exp2(x) for x in ahead[l]]
            if l == 0:
                ql = [top_level(q[u], x, d, True) for (u, d), x in zip(lanes, e)]
                kl = [top_level(k[u], x, d, False) for (u, d), x in zip(lanes, e)]
            else:
                ql = [(q[u] * x).astype(BF16) for (u, d), x in zip(lanes, e)]
                kl = [(k[u] * x).astype(BF16) for (u, d), x in zip(lanes, e)]
            if l == 1:
                scores = prod
            elif l > 1:
                own = [lvl_ref[d] == l - 1 for d in range(2)]
                scores = [jnp.where(own[d], p, s) for (u, d), p, s in zip(lanes, prod, scores)]
            prod = [_dot_nt(a, b) for a, b in zip(ql, kl)]
        own = [lvl_ref[d] == N_LEVELS - 1 for d in range(2)]
        scores = [jnp.where(own[d], p, s) for (u, d), p, s in zip(lanes, prod, scores)]
        own = [lvl_ref[d] == N_LEVELS for d in range(2)]
        scores = [jnp.where(own[d], diag[u], s) for (u, d), s in zip(lanes, scores)]
        for (u, d), s in zip(lanes, scores):
            oi_ref[d * nc + cs[u]] = _dot(s.astype(BF16), v16[u])
        return carry

    lax.fori_loop(0, nc // gg, prep, 0)

    def scan(c, carry):
        sf, sb = carry
        cf = c
        ib = 2 * nc - 1 - c
        st_ref[cf] = sf.astype(BF16)
        st_ref[ib] = sb.astype(BF16)
        ef = dec_ref[cf]
        eb = dec_ref[ib]
        sf = sf * jnp.concatenate([ef, ef], axis=1) + kv_ref[cf]
        sb = sb * jnp.concatenate([eb, eb], axis=1) + kv_ref[ib]
        return sf, sb

    zero = jnp.zeros((dk, dv), F32)
    lax.fori_loop(0, nc, scan, (zero, zero))

    ng = ng_ref[...]
    fg = _group(nc, FIN_GROUP)

    def fin(it, carry):
        cs = [it * fg + u for u in range(fg)]
        rows = [pl.ds(pl.multiple_of(c * CH, CH), CH) for c in cs]
        os_ = [oi_ref[c] + oi_ref[nc + c] + _dot(qs_ref[c], st_ref[c]) + _dot(qs_ref[nc + c], st_ref[nc + c])
               for c in cs]
        inv = [lax.rsqrt(jnp.mean(o * o, axis=-1, keepdims=True) + RMS_EPS) for o in os_]
        for r, o, s in zip(rows, os_, inv):
            o_ref[0, r, :] = (o * s * ng * _silu(p_ref[r, c_r:c_r + dv])).astype(BF16)
        return carry

    lax.fori_loop(0, nc // fg, fin, 0)


def _gla_mixer(xb, w_in, gate_w2, gate_b, norm_g):
    bn, seq, dm = xb.shape
    h, dk, dv = B_HEADS, B_DK, B_DV
    nc = seq // CH
    kw, vw = h * dk, h * dv
    w = w_in

    def heads(cols, width):
        return cols.reshape(dm, h, width).transpose(1, 0, 2)

    gl = jnp.pad(w[:, 2 * kw + 2 * vw:], ((0, 0), (0, dk - 2 * B_RANK)))
    wh = jnp.concatenate([
        heads(w[:, 0:kw], dk), heads(w[:, kw:2 * kw], dk),
        heads(w[:, 2 * kw:2 * kw + vw], dv), heads(w[:, 2 * kw + vw:2 * kw + 2 * vw], dv),
        jnp.broadcast_to(gl[None], (h, dm, dk))], axis=2).astype(BF16)
    w2 = gate_w2.reshape(2, B_RANK, h, dk).transpose(2, 0, 1, 3)
    w2p = jnp.zeros((h, 2, dk, dk), F32)
    w2p = w2p.at[:, 0, 0:B_RANK].set(w2[:, 0]).at[:, 1, B_RANK:2 * B_RANK].set(w2[:, 1]).astype(BF16)
    gb = gate_b.reshape(2, h, dk).transpose(1, 0, 2).astype(F32)
    gb = jnp.broadcast_to(gb[:, :, None, :], (h, 2, 8, dk))
    ng = norm_g.astype(F32).reshape(1, dv)
    seg, lvl = _gla_tables()
    nw = wh.shape[2]

    kern = functools.partial(_gla_kernel, seq=seq)
    return pl.pallas_call(
        kern,
        out_shape=jax.ShapeDtypeStruct((bn, seq, vw), BF16),
        grid=(bn, h),
        in_specs=[
            pl.BlockSpec((1, seq, dm), lambda b, i: (b, 0, 0)),
            pl.BlockSpec((1, dm, nw), lambda b, i: (i, 0, 0)),
            pl.BlockSpec((1, 2, dk, dk), lambda b, i: (i, 0, 0, 0)),
            pl.BlockSpec((1, 2, 8, dk), lambda b, i: (i, 0, 0, 0)),
            pl.BlockSpec((1, dv), lambda b, i: (0, 0)),
            pl.BlockSpec(seg.shape, lambda b, i: (0, 0, 0, 0)),
            pl.BlockSpec(lvl.shape, lambda b, i: (0, 0, 0)),
        ],
        out_specs=pl.BlockSpec((1, seq, dv), lambda b, i: (b, 0, i)),
        scratch_shapes=[
            pltpu.VMEM((seq, nw), F32),
            pltpu.VMEM((2 * nc, CH, dk), BF16),
            pltpu.VMEM((2 * nc, dk, dv), F32),
            pltpu.VMEM((2 * nc, dk, dv), BF16),
            pltpu.VMEM((2 * nc, dk, dk), F32),
            pltpu.VMEM((2 * nc, CH, dv), F32),
        ],
        compiler_params=pltpu.CompilerParams(
            dimension_semantics=("arbitrary", "arbitrary"), vmem_limit_bytes=VMEM_LIMIT),
        name="gla_mixer",
    )(xb, wh, w2p, gb, ng, seg, lvl)


def _post_kernel(o_ref, x_ref, wo_ref, w1_ref, w2_ref, ln_ref, y_ref, yb_ref, *, alpha):
    ln = ln_ref[...]
    x = x_ref[...]
    x1 = _layernorm(alpha * x + _dot(o_ref[...], wo_ref[...]), ln[0:1, :], ln[1:2, :])
    x1b = x1.astype(BF16)
    acc = jnp.zeros(x.shape, F32)
    dff = w1_ref.shape[1]
    for j in range(dff // FF_TILE):
        cols = slice(j * FF_TILE, (j + 1) * FF_TILE)
        hcur = jnp.maximum(_dot(x1b, w1_ref[:, cols]), 0.0)
        acc = acc + _dot((hcur * hcur).astype(BF16), w2_ref[cols, :])
    y = _layernorm(alpha * x1 + acc, ln[2:3, :], ln[3:4, :])
    y_ref[...] = y
    yb_ref[...] = y.astype(BF16)


def _post(o, x, w_out, w1, w2, g1, b1, g2, b2, alpha):
    t, dm = x.shape
    vw = o.shape[1]
    dff = w1.shape[1]
    tm = min(ROW_TILE, t)
    ln = jnp.pad(jnp.stack([g1, b1, g2, b2]).astype(F32), ((0, 4), (0, 0)))
    const = lambda shape: pl.BlockSpec(shape, lambda i: (0, 0), pipeline_mode=pl.Buffered(1))
    return pl.pallas_call(
        functools.partial(_post_kernel, alpha=alpha),
        out_shape=(jax.ShapeDtypeStruct((t, dm), F32), jax.ShapeDtypeStruct((t, dm), BF16)),
        grid=(t // tm,),
        in_specs=[
            pl.BlockSpec((tm, vw), lambda i: (i, 0)),
            pl.BlockSpec((tm, dm), lambda i: (i, 0)),
            const((vw, dm)), const((dm, dff)), const((dff, dm)), const((8, dm)),
        ],
        out_specs=(pl.BlockSpec((tm, dm), lambda i: (i, 0)), pl.BlockSpec((tm, dm), lambda i: (i, 0))),
        compiler_params=pltpu.CompilerParams(
            dimension_semantics=("arbitrary",), vmem_limit_bytes=VMEM_LIMIT),
        name="post",
    )(o, x, w_out.astype(BF16), w1.astype(BF16), w2.astype(BF16), ln)


def kernel(x, a_w_in, a_conv, a_alog, a_dt_bias, a_norm_g, a_w_out, b_w_in, b_gate_w2, b_gate_b,
           b_norm_g, b_w_out, ln1_g, ln1_b, mlp_w1, mlp_w2, ln2_g, ln2_b):
    bn, seq, dm = x.shape
    depth = ln1_g.shape[0]
    alpha = (2 * depth) ** 0.25
    xf = x.astype(F32).reshape(bn * seq, dm)
    xb = xf.astype(BF16)
    for i in range(depth):
        j = i // 2
        xb3 = xb.reshape(bn, seq, dm)
        if i % 2 == 0:
            o = _gdn_mixer(xb3, a_w_in[j], a_conv[j], a_alog[j], a_dt_bias[j], a_norm_g[j])
            w_out = a_w_out[j]
        else:
            o = _gla_mixer(xb3, b_w_in[j], b_gate_w2[j], b_gate_b[j], b_norm_g[j])
            w_out = b_w_out[j]
        xf, xb = _post(o.reshape(bn * seq, -1), xf, w_out, mlp_w1[i], mlp_w2[i],
                       ln1_g[i], ln1_b[i], ln2_g[i], ln2_b[i], alpha)
    return xf.reshape(bn, seq, dm).astype(x.dtype)
```
